```python
import jax, jax.numpy as jnp
from jax import lax
import numpy as np

D_MODEL = 1024
BATCH = 8
SEQ = 4096
DEPTH = 1

HG_HEADS = 4
HG_DK = 128
HG_DV = 128
HG_QK_WIDTH = HG_HEADS * HG_DK
HG_WIDTH = HG_HEADS * HG_DV
HG_CHUNK = 64
ATT_GROUPS = ((128, 1), (512, 4), (2048, 16))
ATT_HEADS_PER_GROUP = 4
ATT_HEAD_DIM = 64
ATT_HEADS = ATT_HEADS_PER_GROUP * len(ATT_GROUPS)
ATT_WIDTH = ATT_HEADS * ATT_HEAD_DIM
ATT_OUT_WIDTH = ATT_HEADS_PER_GROUP * ATT_HEAD_DIM
IN_SIZES = (HG_QK_WIDTH, HG_QK_WIDTH, HG_WIDTH, HG_WIDTH, ATT_WIDTH, ATT_WIDTH, ATT_WIDTH, D_MODEL, D_MODEL)
IN_COLS = sum(IN_SIZES)
IN_SPLIT_IDX = tuple(int(v) for v in np.cumsum(IN_SIZES)[:-1])
N_EXPERTS = 256
TOP_K = 8
D_EXPERT = 256
D_SHARED = 256
ROUTE_SCALE = 2.5
MOE_BLOCK = 128
RMS_EPS = 1e-6

kernel_name = 'hybrid_hgrn2_dilated_attn_moe_block'


def rms_norm(x, g):
    xf = x.astype(jnp.float32)
    y = xf * lax.rsqrt(jnp.mean(xf * xf, axis=-1, keepdims=True) + RMS_EPS)
    return (y * g.astype(jnp.float32)).astype(x.dtype)


def swiglu(x, wg, wu, wd):
    return (jax.nn.silu(x @ wg) * (x @ wu)) @ wd


def hgrn2_chunked(q, f_logit, v, lb):
    B, S, H, DK = q.shape
    DV = v.shape[-1]
    C = HG_CHUNK
    NC = S // C
    f = lb + (1.0 - lb) * jax.nn.sigmoid(f_logit.astype(jnp.float32))
    log_f = jnp.log(f)
    k = 1.0 - f

    def chunks(t):
        return t.astype(jnp.float32).reshape(B, NC, C, H, t.shape[-1]).transpose(1, 0, 3, 2, 4)

    causal = jnp.tril(jnp.ones((C, C), dtype=bool))

    def step(state, inp):
        qc, kc, vc, lfc = inp
        b = jnp.cumsum(lfc, axis=2)
        o_inter = jnp.einsum('bhtk,bhkv->bhtv', qc * jnp.exp(b), state)
        diff = b[:, :, :, None, :] - b[:, :, None, :, :]
        decay = jnp.exp(jnp.where(causal[:, :, None], diff, -jnp.inf))
        scores = jnp.einsum('bhtk,bhsk,bhtsk->bhts', qc, kc, decay)
        o_intra = jnp.einsum('bhts,bhsv->bhtv', scores, vc)
        b_end = b[:, :, -1:, :]
        state = jnp.exp(b_end[:, :, 0, :])[..., None] * state + jnp.einsum('bhsk,bhsv->bhkv', kc * jnp.exp(b_end - b), vc)
        return state, o_inter + o_intra

    s0 = jnp.zeros((B, H, DK, DV), jnp.float32)
    _, o = lax.scan(step, s0, (chunks(q), chunks(k), chunks(v), chunks(log_f)))
    return o.transpose(1, 0, 3, 2, 4).reshape(B, S, H, DV)


def dilated_window_group(q, k, v, window, dilation):
    B, S, H, E = q.shape
    nk = window // dilation
    L = S // dilation
    nb = -(-L // nk)
    Lp = nb * nk

    def to_blocks(t):
        t = t.astype(jnp.float32).reshape(B, L, dilation, H, E).transpose(0, 2, 1, 3, 4)
        t = jnp.pad(t, ((0, 0), (0, 0), (0, Lp - L), (0, 0), (0, 0)))
        return t.reshape(B, dilation, nb, nk, H, E)

    def with_prev(t):
        prev = jnp.pad(t[:, :, :-1], ((0, 0), (0, 0), (1, 0), (0, 0), (0, 0), (0, 0)))
        return jnp.concatenate([prev, t], axis=3)

    qb = to_blocks(q)
    kk = with_prev(to_blocks(k))
    vv = with_prev(to_blocks(v))
    s = jnp.einsum('brnqhe,brnkhe->brnhqk', qb, kk) * (E ** -0.5)
    i = jnp.arange(nk)[None, :, None]
    j = jnp.arange(2 * nk)[None, None, :]
    n = jnp.arange(nb)[:, None, None]
    valid = (j >= i) & (j <= i + nk) & (n * nk + j - nk >= 0)
    s = jnp.where(valid[:, None], s, -jnp.inf)
    m = jnp.max(s, axis=-1, keepdims=True)
    p = jnp.exp(s - m)
    l = jnp.sum(p, axis=-1, keepdims=True)
    o = jnp.einsum('brnhqk,brnkhe->brnqhe', p, vv) / jnp.swapaxes(l, 3, 4)
    lse = jnp.swapaxes((m + jnp.log(l))[..., 0], 3, 4)
    o = o.reshape(B, dilation, Lp, H, E)[:, :, :L].transpose(0, 2, 1, 3, 4).reshape(B, S, H, E)
    lse = lse.reshape(B, dilation, Lp, H)[:, :, :L].transpose(0, 2, 1, 3).reshape(B, S, H)
    return o, lse


def dilated_attention(q, k, v):
    B, S, _, E = q.shape
    outs, lses = [], []
    for g, (window, dilation) in enumerate(ATT_GROUPS):
        hs = slice(g * ATT_HEADS_PER_GROUP, (g + 1) * ATT_HEADS_PER_GROUP)
        o, lse = dilated_window_group(q[:, :, hs], k[:, :, hs], v[:, :, hs], window, dilation)
        outs.append(o)
        lses.append(lse)
    w = jax.nn.softmax(jnp.stack(lses, axis=0), axis=0)
    y = jnp.sum(w[..., None] * jnp.stack(outs, axis=0), axis=0)
    return y.reshape(B, S, ATT_OUT_WIDTH).astype(q.dtype)


def routed_experts(h, w_router, router_bias, w_gate, w_up, w_down):
    T, D = h.shape
    scores = jax.nn.sigmoid((h @ w_router).astype(jnp.float32))
    _, idx = lax.top_k(scores + router_bias.astype(jnp.float32), TOP_K)
    sel = jnp.take_along_axis(scores, idx, axis=-1)
    gates = (sel / jnp.sum(sel, axis=-1, keepdims=True) * ROUTE_SCALE).astype(h.dtype)
    flat_e = idx.reshape(-1).astype(jnp.int32)
    flat_tok = jnp.repeat(jnp.arange(T, dtype=jnp.int32), TOP_K)
    flat_w = gates.reshape(-1)
    order = jnp.argsort(flat_e)
    e_s, tok_s, w_s = flat_e[order], flat_tok[order], flat_w[order]
    counts = jnp.bincount(flat_e, length=N_EXPERTS).astype(jnp.int32)
    starts = jnp.cumsum(counts) - counts
    padded = (counts + MOE_BLOCK - 1) // MOE_BLOCK * MOE_BLOCK
    pend = jnp.cumsum(padded)
    pstart = pend - padded
    dest = pstart[e_s] + jnp.arange(T * TOP_K, dtype=jnp.int32) - starts[e_s]
    n_blocks = -(-(T * TOP_K) // MOE_BLOCK) + N_EXPERTS
    n_slots = n_blocks * MOE_BLOCK
    slot_tok = jnp.full((n_slots,), T, jnp.int32).at[dest].set(tok_s)
    slot_w = jnp.zeros((n_slots,), h.dtype).at[dest].set(w_s)
    blk_e = jnp.minimum(jnp.searchsorted(pend, jnp.arange(n_blocks, dtype=jnp.int32) * MOE_BLOCK, side='right'), N_EXPERTS - 1)
    h_pad = jnp.concatenate([h, jnp.zeros((1, D), h.dtype)], axis=0)

    def block_step(acc, blk):
        tok, w, e = blk
        yb = swiglu(h_pad[tok], w_gate[e], w_up[e], w_down[e]) * w[:, None]
        return acc.at[tok].add(yb), None

    acc, _ = lax.scan(block_step, jnp.zeros((T + 1, D), h.dtype),
                      (slot_tok.reshape(n_blocks, MOE_BLOCK), slot_w.reshape(n_blocks, MOE_BLOCK), blk_e))
    return acc[:T]


def setup_inputs(seed: int = 0) -> dict:
    key = jax.random.key(seed)
    ks = jax.random.split(key, 24)
    D = D_MODEL

    def nrm(k, shape, scale):
        return jax.random.normal(k, shape, jnp.float32) * scale

    lb_offset = jnp.where(jnp.arange(DEPTH + 1) == 0, -2.0, 0.0).astype(jnp.float32)[:, None]
    return {
        'x': nrm(ks[0], (BATCH, SEQ, D), 1.0),
        'c': nrm(ks[1], (BATCH, D), 1.0),
        'ada_w': nrm(ks[2], (DEPTH, D, 6 * D), 0.5 * D ** -0.5),
        'ada_b': nrm(ks[3], (DEPTH, 6 * D), 0.02),
        'norm1_g': 1.0 + nrm(ks[4], (DEPTH, D), 0.02),
        'w_in': nrm(ks[5], (DEPTH, D, IN_COLS), D ** -0.5),
        'lb_logits': nrm(ks[6], (DEPTH + 1, HG_QK_WIDTH), 0.1) + lb_offset,
        'hg_norm_g': 1.0 + nrm(ks[7], (DEPTH, HG_WIDTH), 0.02),
        'w_branch_a': nrm(ks[8], (DEPTH, HG_WIDTH, D), HG_WIDTH ** -0.5),
        'w_branch_b': nrm(ks[9], (DEPTH, ATT_OUT_WIDTH, D), ATT_OUT_WIDTH ** -0.5),
        'w_out': nrm(ks[10], (DEPTH, D, D), D ** -0.5),
        'norm2_g': 1.0 + nrm(ks[11], (DEPTH, D), 0.02),
        'w_router': nrm(ks[12], (DEPTH, D, N_EXPERTS), D ** -0.5),
        'router_bias': nrm(ks[13], (DEPTH, N_EXPERTS), 0.01),
        'w_exp_gate': nrm(ks[14], (DEPTH, N_EXPERTS, D, D_EXPERT), D ** -0.5),
        'w_exp_up': nrm(ks[15], (DEPTH, N_EXPERTS, D, D_EXPERT), D ** -0.5),
        'w_exp_down': nrm(ks[16], (DEPTH, N_EXPERTS, D_EXPERT, D), D_EXPERT ** -0.5),
        'w_sh_gate': nrm(ks[17], (DEPTH, D, D_SHARED), D ** -0.5),
        'w_sh_up': nrm(ks[18], (DEPTH, D, D_SHARED), D ** -0.5),
        'w_sh_down': nrm(ks[19], (DEPTH, D_SHARED, D), D_SHARED ** -0.5),
        'final_g': 1.0 + nrm(ks[20], (D,), 0.02),
    }


def reference(x, c, ada_w, ada_b, norm1_g, w_in, lb_logits, hg_norm_g, w_branch_a, w_branch_b, w_out,
              norm2_g, w_router, router_bias, w_exp_gate, w_exp_up, w_exp_down, w_sh_gate, w_sh_up, w_sh_down,
              final_g):
    B, S, D = x.shape
    lb_table = jnp.cumsum(jax.nn.softmax(lb_logits.astype(jnp.float32), axis=0), axis=0)
    for l in range(DEPTH):
        mod = jax.nn.silu(c) @ ada_w[l] + ada_b[l]
        shift1, scale1, gate1, shift2, scale2, gate2 = jnp.split(mod[:, None, :], 6, axis=-1)

        h = rms_norm(x, norm1_g[l]) * (1.0 + scale1) + shift1
        hq, hf, hi, hg, aq, ak, av, ga, gb = jnp.split(h @ w_in[l], IN_SPLIT_IDX, axis=-1)
        o_a = hgrn2_chunked(hq.reshape(B, S, HG_HEADS, HG_DK), hf.reshape(B, S, HG_HEADS, HG_DK),
                            hi.reshape(B, S, HG_HEADS, HG_DV), lb_table[l].reshape(HG_HEADS, HG_DK))
        y_a = rms_norm(o_a.astype(x.dtype), hg_norm_g[l].reshape(HG_HEADS, HG_DV)).reshape(B, S, HG_WIDTH) * jax.nn.silu(hg)
        y_b = dilated_attention(aq.reshape(B, S, ATT_HEADS, ATT_HEAD_DIM), ak.reshape(B, S, ATT_HEADS, ATT_HEAD_DIM),
                                av.reshape(B, S, ATT_HEADS, ATT_HEAD_DIM))
        merged = jax.nn.sigmoid(ga) * (y_a @ w_branch_a[l]) + jax.nn.sigmoid(gb) * (y_b @ w_branch_b[l])
        x = x + gate1 * (merged @ w_out[l])

        h2 = (rms_norm(x, norm2_g[l]) * (1.0 + scale2) + shift2).reshape(B * S, D)
        y = routed_experts(h2, w_router[l], router_bias[l], w_exp_gate[l], w_exp_up[l], w_exp_down[l]) \
            + swiglu(h2, w_sh_gate[l], w_sh_up[l], w_sh_down[l])
        x = x + gate2 * y.reshape(B, S, D)
    return rms_norm(x, final_g)
```

```python
import functools

import jax
import jax.numpy as jnp
from jax import lax
from jax.experimental import pallas as pl
from jax.experimental.pallas import tpu as pltpu

F32 = jnp.float32
BF16 = jnp.bfloat16
I32 = jnp.int32
U32 = jnp.uint32
HIGHEST = lax.Precision.HIGHEST

HG_HEADS = 4
HG_BLOCK = 16
HG_CHUNK = 64
ATT_GROUPS = ((128, 1), (512, 4), (2048, 16))
ATT_HEADS_PER_GROUP = 4
ATT_HEAD_DIM = 64
TOP_K = 8
ROUTE_SCALE = 2.5
MOE_BLOCK = 128
RMS_EPS = 1e-6

VMEM_LIMIT_BYTES = 56 * 1024 * 1024


def _sigmoid(x):
    return 1.0 / (1.0 + jnp.exp(-x))


def _silu(x):
    return x * _sigmoid(x)


def _rms(x, g):
    return x * lax.rsqrt(jnp.mean(x * x, axis=-1, keepdims=True) + RMS_EPS) * g


def _params(n_axes=1):
    return pltpu.CompilerParams(
        dimension_semantics=("arbitrary",) * n_axes, vmem_limit_bytes=VMEM_LIMIT_BYTES)


def _ada_kernel(c_ref, w_ref, b_ref, o_ref):
    sc = _silu(c_ref[...])
    o_ref[...] = jnp.dot(sc, w_ref[...], preferred_element_type=F32, precision=HIGHEST) + b_ref[...]


def _ada(c, w, b):
    bsz, d = c.shape
    n = w.shape[1]
    return pl.pallas_call(
        _ada_kernel,
        out_shape=jax.ShapeDtypeStruct((bsz, n), F32),
        grid=(n // d,),
        in_specs=[pl.BlockSpec((bsz, d), lambda j: (0, 0)),
                  pl.BlockSpec((d, d), lambda j: (0, j)),
                  pl.BlockSpec((1, d), lambda j: (0, j))],
        out_specs=pl.BlockSpec((bsz, d), lambda j: (0, j)),
        compiler_params=_params(),
        name="ada_mod",
    )(c, w, b.reshape(1, n))


def _inproj_kernel(col_ranges, x_ref, g_ref, sc_ref, sh_ref, w_ref, *out_refs):
    h = _rms(x_ref[...], g_ref[...]) * (1.0 + sc_ref[0]) + sh_ref[0]
    hb = h.astype(BF16)
    for (c0, c1), o_ref in zip(col_ranges, out_refs):
        o_ref[...] = jnp.dot(hb, w_ref[:, c0:c1], preferred_element_type=F32).astype(o_ref.dtype)


def _inproj(x2, g, scale, shift, w_bf16, seq, segs, tm):
    t, d = x2.shape
    col_ranges, c = [], 0
    for wdt, _ in segs:
        col_ranges.append((c, c + wdt))
        c += wdt
    per_b = lambda i: ((i * tm) // seq, 0, 0)
    return pl.pallas_call(
        functools.partial(_inproj_kernel, tuple(col_ranges)),
        out_shape=[jax.ShapeDtypeStruct((t, wdt), dt) for wdt, dt in segs],
        grid=(t // tm,),
        in_specs=[pl.BlockSpec((tm, d), lambda i: (i, 0)),
                  pl.BlockSpec((1, d), lambda i: (0, 0)),
                  pl.BlockSpec((1, 1, d), per_b),
                  pl.BlockSpec((1, 1, d), per_b),
                  pl.BlockSpec(w_bf16.shape, lambda i: (0, 0))],
        out_specs=[pl.BlockSpec((tm, wdt), lambda i: (i, 0)) for wdt, _ in segs],
        compiler_params=_params(),
        name="in_proj",
    )(x2, g.reshape(1, d), scale, shift, w_bf16)


def _hgrn_kernel(ts, q_ref, f_ref, v_ref, gt_ref, lb_ref, ng_ref, o_ref, st_ref):
    dk = q_ref.shape[1] // HG_HEADS
    n_chunks = ts // HG_CHUNK
    n_blk = HG_CHUNK // HG_BLOCK

    @pl.when(pl.program_id(1) == 0)
    def _():
        st_ref[...] = jnp.zeros_like(st_ref)

    row = lax.broadcasted_iota(I32, (ts, ts), 0)
    col = lax.broadcasted_iota(I32, (ts, ts), 1)
    same_chunk = (row // HG_CHUNK) == (col // HG_CHUNK)
    cum_mat = jnp.where(same_chunk & (col <= row), 1.0, 0.0).astype(F32)
    t_in_blk = lax.broadcasted_iota(I32, (ts, dk), 0) % HG_BLOCK

    for h in range(HG_HEADS):
        cs = slice(h * dk, (h + 1) * dk)
        q = q_ref[:, cs].astype(F32)
        v = v_ref[:, cs].astype(F32)
        lb = lb_ref[:, cs]
        f = lb + (1.0 - lb) * _sigmoid(f_ref[:, cs])
        k = 1.0 - f
        b = jnp.dot(cum_mat, jnp.log(f), preferred_element_type=F32, precision=HIGHEST)

        o = jnp.sum(q * k, axis=-1, keepdims=True) * v
        for d in range(1, HG_BLOCK):
            k_d = pltpu.roll(k, d, axis=0)
            b_d = pltpu.roll(b, d, axis=0)
            v_d = pltpu.roll(v, d, axis=0)
            w = jnp.sum(q * k_d * jnp.exp(jnp.minimum(b - b_d, 0.0)), axis=-1, keepdims=True)
            o = o + jnp.where(t_in_blk >= d, w * v_d, 0.0)

        st = st_ref[h]
        o_rows = []
        for c in range(n_chunks):
            r0 = c * HG_CHUNK
            bc = b[r0:r0 + HG_CHUNK]
            qc = q[r0:r0 + HG_CHUNK]
            kc = k[r0:r0 + HG_CHUNK]
            vc = v[r0:r0 + HG_CHUNK].astype(BF16)
            st_b = st.astype(BF16)
            for i in range(n_blk):
                i0 = i * HG_BLOCK
                if i == 0:
                    qt = qc[:HG_BLOCK] * jnp.exp(bc[:HG_BLOCK])
                    qs = qt
                else:
                    ref_row = bc[i0 - 1:i0]
                    qt = qc[i0:i0 + HG_BLOCK] * jnp.exp(bc[i0:i0 + HG_BLOCK] - ref_row)
                    qs = qt * jnp.exp(ref_row)
                oi = lax.dot_general(qs.astype(BF16), st_b, (((1,), (1,)), ((), ())),
                                     preferred_element_type=F32)
                if i > 0:
                    kh = kc[:i0] * jnp.exp(ref_row - bc[:i0])
                    a = lax.dot_general(qt.astype(BF16), kh.astype(BF16), (((1,), (1,)), ((), ())),
                                        preferred_element_type=F32)
                    oi = oi + jnp.dot(a.astype(BF16), vc[:i0], preferred_element_type=F32)
                o_rows.append(oi)
            b_end = bc[HG_CHUNK - 1:HG_CHUNK]
            kend = kc * jnp.exp(b_end - bc)
            vt = v[r0:r0 + HG_CHUNK].T.astype(BF16)
            st = st * jnp.exp(b_end) + jnp.dot(vt, kend.astype(BF16), preferred_element_type=F32)
        st_ref[h] = st
        o = o + jnp.concatenate(o_rows, axis=0)
        y = _rms(o, ng_ref[:, cs]) * _silu(gt_ref[:, cs].astype(F32))
        o_ref[:, cs] = y.astype(o_ref.dtype)


def _hgrn(hq, hf, hi, hg, lb, ng, bsz, seq, ts):
    t, w = hq.shape
    dk = w // HG_HEADS
    n_s = seq // ts
    tile = lambda b, s: (b * n_s + s, 0)
    return pl.pallas_call(
        functools.partial(_hgrn_kernel, ts),
        out_shape=jax.ShapeDtypeStruct((t, w), BF16),
        grid=(bsz, n_s),
        in_specs=[pl.BlockSpec((ts, w), tile)] * 4
        + [pl.BlockSpec((1, w), lambda b, s: (0, 0))] * 2,
        out_specs=pl.BlockSpec((ts, w), tile),
        scratch_shapes=[pltpu.VMEM((HG_HEADS, dk, dk), F32)],
        compiler_params=_params(2),
        name="hgrn2",
    )(hq, hf, hi, hg, lb.reshape(1, w), ng.reshape(1, w))


def _attn_kernel(nk, q_ref, kp_ref, kc_ref, vp_ref, vc_ref, o_ref, lse_ref):
    n = pl.program_id(2)
    e = ATT_HEAD_DIM
    i = lax.broadcasted_iota(I32, (nk, 2 * nk), 0)
    j = lax.broadcasted_iota(I32, (nk, 2 * nk), 1)
    valid = (j >= i) & (j <= i + nk) & ((j >= nk) | (n > 0))
    q = q_ref[0]
    kk = jnp.concatenate([kp_ref[0], kc_ref[0]], axis=0)
    vv = jnp.concatenate([vp_ref[0], vc_ref[0]], axis=0)
    for h in range(ATT_HEADS_PER_GROUP):
        cs = slice(h * e, (h + 1) * e)
        s = lax.dot_general(q[:, cs], kk[:, cs], (((1,), (1,)), ((), ())),
                            preferred_element_type=F32) * (e ** -0.5)
        s = jnp.where(valid, s, -jnp.inf)
        m = jnp.max(s, axis=-1, keepdims=True)
        p = jnp.exp(s - m)
        l = jnp.sum(p, axis=-1, keepdims=True)
        o = jnp.dot(p.astype(BF16), vv[:, cs], preferred_element_type=F32) / l
        o_ref[0, :, cs] = o
        lse_ref[0, :, cs] = jnp.broadcast_to(m + jnp.log(l), (nk, e))


def _attn_group(aq, ak, av, bsz, seq, g):
    window, dil = ATT_GROUPS[g]
    nk = window // dil
    n_groups = len(ATT_GROUPS)
    gw = ATT_HEADS_PER_GROUP * ATT_HEAD_DIM
    ln = seq // dil
    assert ln % nk == 0
    view = lambda a: a.reshape(bsz, ln, dil * n_groups * gw)
    cur = lambda b, r, n: (b, n, r * n_groups + g)
    prev = lambda b, r, n: (b, jnp.maximum(n - 1, 0), r * n_groups + g)
    blk = (1, nk, gw)
    o, lse = pl.pallas_call(
        functools.partial(_attn_kernel, nk),
        out_shape=[jax.ShapeDtypeStruct((bsz, ln, dil * gw), F32)] * 2,
        grid=(bsz, dil, ln // nk),
        in_specs=[pl.BlockSpec(blk, cur), pl.BlockSpec(blk, prev), pl.BlockSpec(blk, cur),
                  pl.BlockSpec(blk, prev), pl.BlockSpec(blk, cur)],
        out_specs=[pl.BlockSpec(blk, lambda b, r, n: (b, n, r))] * 2,
        compiler_params=_params(3),
        name=f"dilated_attn_g{g}",
    )(view(aq), view(ak), view(ak), view(av), view(av))
    return o.reshape(bsz * seq, gw), lse.reshape(bsz * seq, gw)


def _merge_kernel(ya_ref, o0_ref, o1_ref, o2_ref, l0_ref, l1_ref, l2_ref, ga_ref, gb_ref, x_ref,
                  g1_ref, sc2_ref, sh2_ref, g2_ref, n2_ref, wa_ref, wb_ref, wo_ref, wr_ref,
                  wsg_ref, wsu_ref, wsd_ref, x1_ref, hp_ref, lg_ref):
    l0, l1, l2 = l0_ref[...], l1_ref[...], l2_ref[...]
    m = jnp.maximum(jnp.maximum(l0, l1), l2)
    e0, e1, e2 = jnp.exp(l0 - m), jnp.exp(l1 - m), jnp.exp(l2 - m)
    yb = (e0 * o0_ref[...] + e1 * o1_ref[...] + e2 * o2_ref[...]) / (e0 + e1 + e2)
    merged = (_sigmoid(ga_ref[...].astype(F32))
              * jnp.dot(ya_ref[...], wa_ref[...], preferred_element_type=F32)
              + _sigmoid(gb_ref[...].astype(F32))
              * jnp.dot(yb.astype(BF16), wb_ref[...], preferred_element_type=F32))
    x1 = x_ref[...] + g1_ref[0] * jnp.dot(merged.astype(BF16), wo_ref[...],
                                           preferred_element_type=F32)
    h2 = _rms(x1, n2_ref[...]) * (1.0 + sc2_ref[0]) + sh2_ref[0]
    hb = h2.astype(BF16)
    act = (_silu(jnp.dot(hb, wsg_ref[...], preferred_element_type=F32))
           * jnp.dot(hb, wsu_ref[...], preferred_element_type=F32))
    shared = jnp.dot(act.astype(BF16), wsd_ref[...], preferred_element_type=F32)
    x1_ref[...] = x1 + g2_ref[0] * shared
    half = hb.shape[1] // 2
    bits = lax.bitcast_convert_type(hb.astype(F32), U32)
    hp_ref[...] = (bits[:, :half] >> 16) | (bits[:, half:] & jnp.uint32(0xFFFF0000))
    lg_ref[...] = lax.dot_general(wr_ref[...], h2, (((1,), (1,)), ((), ())),
                                  preferred_element_type=F32, precision=HIGHEST)


def _merge(ya, att, ga, gb, x2, gate1, scale2, shift2, gate2, norm2_g, wa, wb, wo, wr_t, wsg, wsu,
           wsd, seq, tm):
    t, d = x2.shape
    n_e = wr_t.shape[0]
    per_b = lambda i: ((i * tm) // seq, 0, 0)
    rows = lambda wdt: pl.BlockSpec((tm, wdt), lambda i: (i, 0))
    full = lambda a: pl.BlockSpec(a.shape, lambda i: (0,) * a.ndim)
    vec = pl.BlockSpec((1, 1, d), per_b)
    (o0, l0), (o1, l1), (o2, l2) = att
    gw = o0.shape[1]
    return pl.pallas_call(
        _merge_kernel,
        out_shape=[jax.ShapeDtypeStruct((t, d), F32),
                   jax.ShapeDtypeStruct((t, d // 2), U32),
                   jax.ShapeDtypeStruct((n_e, t), F32)],
        grid=(t // tm,),
        in_specs=[rows(ya.shape[1])] + [rows(gw)] * 6 + [rows(d)] * 3
        + [vec, vec, vec, vec, pl.BlockSpec((1, d), lambda i: (0, 0))]
        + [full(a) for a in (wa, wb, wo, wr_t, wsg, wsu, wsd)],
        out_specs=[rows(d), rows(d // 2), pl.BlockSpec((n_e, tm), lambda i: (0, i))],
        compiler_params=_params(),
        name="merge_router",
    )(ya, o0, o1, o2, l0, l1, l2, ga, gb, x2, gate1, scale2, shift2, gate2,
      norm2_g.reshape(1, d), wa, wb, wo, wr_t, wsg, wsu, wsd)


def _topk_kernel(lg_ref, bias_ref, idx_ref, gate_ref, rank_ref, cnt_ref, carry_ref):
    n_e, tt = lg_ref.shape

    @pl.when(pl.program_id(0) == 0)
    def _():
        carry_ref[...] = jnp.zeros_like(carry_ref)

    scores = _sigmoid(lg_ref[...])
    sel = scores + bias_ref[...]
    eio = lax.broadcasted_iota(I32, (n_e, tt), 0)
    picked = jnp.zeros((n_e, tt), F32)
    idxs, vals = [], []
    for _ in range(TOP_K):
        m = jnp.max(sel, axis=0, keepdims=True)
        ik = jnp.min(jnp.where(sel == m, eio, n_e), axis=0, keepdims=True)
        hit = eio == ik
        vals.append(jnp.sum(jnp.where(hit, scores, 0.0), axis=0, keepdims=True))
        sel = jnp.where(hit, -jnp.inf, sel)
        picked = picked + jnp.where(hit, 1.0, 0.0)
        idxs.append(ik)
    denom = vals[0]
    for v in vals[1:]:
        denom = denom + v
    gate_ref[...] = jnp.concatenate([v / denom * ROUTE_SCALE for v in vals], axis=0)
    idx_ref[...] = jnp.concatenate(idxs, axis=0)

    upper = (lax.broadcasted_iota(I32, (tt, tt), 0) <= lax.broadcasted_iota(I32, (tt, tt), 1))
    incl = jnp.dot(picked.astype(BF16), jnp.where(upper, 1.0, 0.0).astype(BF16),
                   preferred_element_type=F32)
    before = incl - picked + carry_ref[...]
    rank_ref[...] = jnp.concatenate(
        [jnp.sum(jnp.where(eio == ik, before, 0.0), axis=0, keepdims=True) for ik in idxs],
        axis=0).astype(I32)
    carry_ref[...] = carry_ref[...] + jnp.sum(picked, axis=1, keepdims=True)
    cnt_ref[...] = jnp.broadcast_to(carry_ref[...], cnt_ref.shape).astype(I32)


def _topk(logits_t, bias, tt):
    n_e, t = logits_t.shape
    tok = pl.BlockSpec((TOP_K, tt), lambda i: (0, i))
    return pl.pallas_call(
        _topk_kernel,
        out_shape=[jax.ShapeDtypeStruct((TOP_K, t), I32), jax.ShapeDtypeStruct((TOP_K, t), F32),
                   jax.ShapeDtypeStruct((TOP_K, t), I32), jax.ShapeDtypeStruct((n_e, 128), I32)],
        grid=(t // tt,),
        in_specs=[pl.BlockSpec((n_e, tt), lambda i: (0, i)),
                  pl.BlockSpec((n_e, 1), lambda i: (0, 0))],
        out_specs=[tok, tok, tok, pl.BlockSpec((n_e, 128), lambda i: (0, 0))],
        scratch_shapes=[pltpu.VMEM((n_e, 1), F32)],
        compiler_params=_params(),
        name="router_topk",
    )(logits_t, bias.reshape(n_e, 1))


def _dest_kernel(idx_ref, rank_ref, start_ref, o_ref):
    k, tt = idx_ref.shape
    n_e = start_ref.shape[0]
    eio = lax.broadcasted_iota(I32, (n_e, tt), 0)
    start = start_ref[...]
    rows = [jnp.sum(jnp.where(eio == idx_ref[r:r + 1, :], start, 0), axis=0, keepdims=True)
            for r in range(k)]
    o_ref[...] = jnp.concatenate(rows, axis=0) + rank_ref[...]


def _dest(idx, rank, seg_start, tt):
    k, t = idx.shape
    n_e = seg_start.shape[0]
    tok = pl.BlockSpec((k, tt), lambda i: (0, i))
    return pl.pallas_call(
        _dest_kernel,
        out_shape=jax.ShapeDtypeStruct((k, t), I32),
        grid=(t // tt,),
        in_specs=[tok, tok, pl.BlockSpec((n_e, 1), lambda i: (0, 0))],
        out_specs=tok,
        compiler_params=_params(),
        name="moe_dest",
    )(idx, rank, seg_start.reshape(n_e, 1))


def _dispatch_kernel(dest_ref, h_ref, xs_ref, sem):
    k, tt = dest_ref.shape

    def row_copy(i, r):
        return pltpu.make_async_copy(h_ref.at[pl.ds(i, 1), :],
                                     xs_ref.at[pl.ds(dest_ref[r, i], 1), :], sem)

    def start(i, carry):
        for r in range(k):
            row_copy(i, r).start()
        return carry

    def wait(i, carry):
        for r in range(k):
            row_copy(i, r).wait()
        return carry

    lax.fori_loop(0, tt, start, 0)
    lax.fori_loop(0, tt, wait, 0)


def _dispatch(dest, hp, n_slots, tt):
    k, t = dest.shape
    w = hp.shape[1]
    return pl.pallas_call(
        _dispatch_kernel,
        out_shape=jax.ShapeDtypeStruct((n_slots, w), hp.dtype),
        grid=(t // tt,),
        in_specs=[pl.BlockSpec((k, tt), lambda i: (0, i), memory_space=pltpu.SMEM),
                  pl.BlockSpec((tt, w), lambda i: (i, 0))],
        out_specs=pl.BlockSpec(memory_space=pl.ANY),
        scratch_shapes=[pltpu.SemaphoreType.DMA],
        compiler_params=_params(),
        name="moe_dispatch",
    )(dest, hp)


def _expert_kernel(be_ref, nu_ref, xs_ref, wg_ref, wu_ref, wd_ref, ys_ref):
    @pl.when(pl.program_id(0) < nu_ref[0])
    def _():
        word = xs_ref[...]
        half = word.shape[1]
        lo = lax.bitcast_convert_type(word << 16, F32).astype(BF16)
        hi = lax.bitcast_convert_type(word & jnp.uint32(0xFFFF0000), F32).astype(BF16)
        wg = wg_ref[0].astype(BF16)
        wu = wu_ref[0].astype(BF16)
        gate = (jnp.dot(lo, wg[:half], preferred_element_type=F32)
                + jnp.dot(hi, wg[half:], preferred_element_type=F32))
        up = (jnp.dot(lo, wu[:half], preferred_element_type=F32)
              + jnp.dot(hi, wu[half:], preferred_element_type=F32))
        act = (_silu(gate) * up).astype(BF16)
        ys_ref[...] = jnp.dot(act, wd_ref[0].astype(BF16), preferred_element_type=F32)


def _experts(blk_e, n_used, xs, wg, wu, wd):
    n_slots, half = xs.shape
    _, d, de = wg.shape
    n_blocks = n_slots // MOE_BLOCK
    return pl.pallas_call(
        _expert_kernel,
        out_shape=jax.ShapeDtypeStruct((n_slots, d), F32),
        grid_spec=pltpu.PrefetchScalarGridSpec(
            num_scalar_prefetch=2,
            grid=(n_blocks,),
            in_specs=[pl.BlockSpec((MOE_BLOCK, half), lambda i, be, nu: (i, 0)),
                      pl.BlockSpec((1, d, de), lambda i, be, nu: (be[i], 0, 0)),
                      pl.BlockSpec((1, d, de), lambda i, be, nu: (be[i], 0, 0)),
                      pl.BlockSpec((1, de, d), lambda i, be, nu: (be[i], 0, 0))],
            out_specs=pl.BlockSpec((MOE_BLOCK, d), lambda i, be, nu: (i, 0))),
        compiler_params=_params(),
        name="moe_experts",
    )(blk_e, n_used, xs, wg, wu, wd)


def _combine_kernel(dest_ref, gt_ref, x_ref, g2_ref, fg_ref, ys_ref, o_ref, buf_ref, sem):
    k, tc = dest_ref.shape

    def row_copy(i, r):
        return pltpu.make_async_copy(ys_ref.at[pl.ds(dest_ref[r, i], 1), :],
                                     buf_ref.at[r, pl.ds(i, 1), :], sem)

    def start(i, carry):
        for r in range(k):
            row_copy(i, r).start()
        return carry

    def wait(i, carry):
        for r in range(k):
            row_copy(i, r).wait()
        return carry

    lax.fori_loop(0, tc, start, 0)
    lax.fori_loop(0, tc, wait, 0)
    gt = gt_ref[...]
    y = buf_ref[0] * gt[:, 0:1]
    for r in range(1, k):
        y = y + buf_ref[r] * gt[:, r:r + 1]
    o_ref[...] = _rms(x_ref[...] + g2_ref[0] * y, fg_ref[...])


def _combine(dest, gates_t, x1s, gate2, final_g, ys, seq, tc):
    k, t = dest.shape
    d = x1s.shape[1]
    return pl.pallas_call(
        _combine_kernel,
        out_shape=jax.ShapeDtypeStruct((t, d), F32),
        grid=(t // tc,),
        in_specs=[pl.BlockSpec((k, tc), lambda i: (0, i), memory_space=pltpu.SMEM),
                  pl.BlockSpec((tc, k), lambda i: (i, 0)),
                  pl.BlockSpec((tc, d), lambda i: (i, 0)),
                  pl.BlockSpec((1, 1, d), lambda i: ((i * tc) // seq, 0, 0)),
                  pl.BlockSpec((1, d), lambda i: (0, 0)),
                  pl.BlockSpec(memory_space=pl.ANY)],
        out_specs=pl.BlockSpec((tc, d), lambda i: (i, 0)),
        scratch_shapes=[pltpu.VMEM((k, tc, d), F32), pltpu.SemaphoreType.DMA],
        compiler_params=_params(),
        name="moe_combine",
    )(dest, gates_t, x1s, gate2, final_g.reshape(1, d), ys)


def _layer(x2, c, bsz, seq, lb_row, ada_w, ada_b, norm1_g, w_in, hg_norm_g, w_branch_a, w_branch_b,
           w_out, norm2_g, w_router, router_bias, w_exp_gate, w_exp_up, w_exp_down, w_sh_gate,
           w_sh_up, w_sh_down, final_g):
    t, d = x2.shape
    n_e = w_router.shape[1]
    mod = _ada(c, ada_w, ada_b).reshape(bsz, 6, 1, d)
    shift1, scale1, gate1, shift2, scale2, gate2 = (mod[:, j] for j in range(6))

    hw = hg_norm_g.shape[0]
    aw = len(ATT_GROUPS) * ATT_HEADS_PER_GROUP * ATT_HEAD_DIM
    segs = [(hw, BF16), (hw, F32), (hw, BF16), (hw, BF16), (aw, BF16), (aw, BF16), (aw, BF16),
            (d, BF16), (d, BF16)]
    hq, hf, hi, hg, aq, ak, av, ga, gb = _inproj(
        x2, norm1_g, scale1, shift1, w_in.astype(BF16), seq, segs, tm=512)

    ya = _hgrn(hq, hf, hi, hg, lb_row, hg_norm_g, bsz, seq, ts=256)
    att = [_attn_group(aq, ak, av, bsz, seq, g) for g in range(len(ATT_GROUPS))]

    x1s, hp, logits_t = _merge(
        ya, att, ga, gb, x2, gate1, scale2, shift2, gate2, norm2_g, w_branch_a.astype(BF16),
        w_branch_b.astype(BF16), w_out.astype(BF16), w_router.T, w_sh_gate.astype(BF16),
        w_sh_up.astype(BF16), w_sh_down.astype(BF16), seq, tm=256)

    idx, gates, rank, cnt = _topk(logits_t, router_bias, tt=512)
    counts = cnt[:, 0]
    padded = (counts + MOE_BLOCK - 1) // MOE_BLOCK * MOE_BLOCK
    seg_end = jnp.cumsum(padded)
    n_blocks = -(-(t * TOP_K) // MOE_BLOCK) + n_e
    blk_e = jnp.minimum(
        jnp.searchsorted(seg_end, jnp.arange(n_blocks, dtype=I32) * MOE_BLOCK, side="right"),
        n_e - 1).astype(I32)
    n_used = (seg_end[-1:] // MOE_BLOCK).astype(I32)
    dest = _dest(idx, rank, (seg_end - padded).astype(I32), tt=512)

    xs = _dispatch(dest, hp, n_blocks * MOE_BLOCK, tt=256)
    ys = _experts(blk_e, n_used, xs, w_exp_gate, w_exp_up, w_exp_down)
    return _combine(dest, gates.T, x1s, gate2, final_g, ys, seq, tc=128)


def kernel(x, c, ada_w, ada_b, norm1_g, w_in, lb_logits, hg_norm_g, w_branch_a, w_branch_b, w_out,
           norm2_g, w_router, router_bias, w_exp_gate, w_exp_up, w_exp_down, w_sh_gate, w_sh_up,
           w_sh_down, final_g):
    bsz, seq, d = x.shape
    depth = ada_w.shape[0]
    assert depth == 1, "the last layer's kernels also apply the final norm"
    lb_table = jnp.cumsum(jax.nn.softmax(lb_logits.astype(F32), axis=0), axis=0)
    out = _layer(x.reshape(bsz * seq, d), c, bsz, seq, lb_table[0], ada_w[0], ada_b[0], norm1_g[0],
                 w_in[0], hg_norm_g[0], w_branch_a[0], w_branch_b[0], w_out[0], norm2_g[0],
                 w_router[0], router_bias[0], w_exp_gate[0], w_exp_up[0], w_exp_down[0],
                 w_sh_gate[0], w_sh_up[0], w_sh_down[0], final_g)
    return out.reshape(bsz, seq, d)
```

```python
import functools

import jax
import jax.numpy as jnp
from jax import lax
from jax.experimental import pallas as pl
from jax.experimental.pallas import tpu as pltpu

F32 = jnp.float32
BF16 = jnp.bfloat16
I32 = jnp.int32
U32 = jnp.uint32
HIGHEST = lax.Precision.HIGHEST

HG_HEADS = 4
HG_BLOCK = 16
HG_CHUNK = 64
ATT_GROUPS = ((128, 1), (512, 4), (2048, 16))
ATT_HEADS_PER_GROUP = 4
ATT_HEAD_DIM = 64
TOP_K = 8
ROUTE_SCALE = 2.5
MOE_BLOCK = 256
RMS_EPS = 1e-6

VMEM_LIMIT_BYTES = 56 * 1024 * 1024


def _sigmoid(x):
    return 1.0 / (1.0 + jnp.exp(-x))


def _silu(x):
    return x * _sigmoid(x)


def _rms(x, g):
    return x * lax.rsqrt(jnp.mean(x * x, axis=-1, keepdims=True) + RMS_EPS) * g


def _pack_halves(x):
    n = x.shape[1] // 2
    bits = lax.bitcast_convert_type(x.astype(BF16).astype(F32), U32)
    return (bits[:, :n] >> 16) | (bits[:, n:] & jnp.uint32(0xFFFF0000))


def _unpack_halves(word):
    lo = lax.bitcast_convert_type(word << 16, F32)
    hi = lax.bitcast_convert_type(word & jnp.uint32(0xFFFF0000), F32)
    return lo, hi


def _params(n_axes=1):
    return pltpu.CompilerParams(
        dimension_semantics=("arbitrary",) * n_axes, vmem_limit_bytes=VMEM_LIMIT_BYTES)


def _ada_kernel(c_ref, w_ref, b_ref, o_ref):
    sc = _silu(c_ref[...])
    o_ref[...] = jnp.dot(sc, w_ref[...], preferred_element_type=F32, precision=HIGHEST) + b_ref[...]


def _ada(c, w, b):
    bsz, d = c.shape
    n = w.shape[1]
    return pl.pallas_call(
        _ada_kernel,
        out_shape=jax.ShapeDtypeStruct((bsz, n), F32),
        grid=(n // d,),
        in_specs=[pl.BlockSpec((bsz, d), lambda j: (0, 0)),
                  pl.BlockSpec((d, d), lambda j: (0, j)),
                  pl.BlockSpec((1, d), lambda j: (0, j))],
        out_specs=pl.BlockSpec((bsz, d), lambda j: (0, j)),
        compiler_params=_params(),
        name="ada_mod",
    )(c, w, b.reshape(1, n))


def _inproj_kernel(col_ranges, x_ref, g_ref, sc_ref, sh_ref, w_ref, *out_refs):
    h = _rms(x_ref[...], g_ref[...]) * (1.0 + sc_ref[0]) + sh_ref[0]
    hb = h.astype(BF16)
    for (c0, c1), o_ref in zip(col_ranges, out_refs):
        o_ref[...] = jnp.dot(hb, w_ref[:, c0:c1], preferred_element_type=F32).astype(o_ref.dtype)


def _inproj(x2, g, scale, shift, w_bf16, seq, segs, tm):
    t, d = x2.shape
    col_ranges, c = [], 0
    for wdt, _ in segs:
        col_ranges.append((c, c + wdt))
        c += wdt
    per_b = lambda i: ((i * tm) // seq, 0, 0)
    return pl.pallas_call(
        functools.partial(_inproj_kernel, tuple(col_ranges)),
        out_shape=[jax.ShapeDtypeStruct((t, wdt), dt) for wdt, dt in segs],
        grid=(t // tm,),
        in_specs=[pl.BlockSpec((tm, d), lambda i: (i, 0)),
                  pl.BlockSpec((1, d), lambda i: (0, 0)),
                  pl.BlockSpec((1, 1, d), per_b),
                  pl.BlockSpec((1, 1, d), per_b),
                  pl.BlockSpec(w_bf16.shape, lambda i: (0, 0))],
        out_specs=[pl.BlockSpec((tm, wdt), lambda i: (i, 0)) for wdt, _ in segs],
        compiler_params=_params(),
        name="in_proj",
    )(x2, g.reshape(1, d), scale, shift, w_bf16)


def _hgrn_kernel(ts, q_ref, f_ref, v_ref, gt_ref, lb_ref, ng_ref, o_ref, st_ref):
    dk = q_ref.shape[1] // HG_HEADS
    n_chunks = ts // HG_CHUNK
    n_blk = HG_CHUNK // HG_BLOCK

    @pl.when(pl.program_id(1) == 0)
    def _():
        st_ref[...] = jnp.zeros_like(st_ref)

    row = lax.broadcasted_iota(I32, (ts, ts), 0)
    col = lax.broadcasted_iota(I32, (ts, ts), 1)
    same_chunk = (row // HG_CHUNK) == (col // HG_CHUNK)
    cum_mat = jnp.where(same_chunk & (col <= row), 1.0, 0.0).astype(F32)
    t_in_blk = lax.broadcasted_iota(I32, (ts, dk), 0) % HG_BLOCK

    for h in range(HG_HEADS):
        cs = slice(h * dk, (h + 1) * dk)
        q = q_ref[:, cs].astype(F32)
        v = v_ref[:, cs].astype(F32)
        lb = lb_ref[:, cs]
        f = lb + (1.0 - lb) * _sigmoid(f_ref[:, cs])
        k = 1.0 - f
        b = jnp.dot(cum_mat, jnp.log(f), preferred_element_type=F32, precision=HIGHEST)

        o = jnp.sum(q * k, axis=-1, keepdims=True) * v
        for d in range(1, HG_BLOCK):
            k_d = pltpu.roll(k, d, axis=0)
            b_d = pltpu.roll(b, d, axis=0)
            v_d = pltpu.roll(v, d, axis=0)
            w = jnp.sum(q * k_d * jnp.exp(jnp.minimum(b - b_d, 0.0)), axis=-1, keepdims=True)
            o = o + jnp.where(t_in_blk >= d, w * v_d, 0.0)

        st = st_ref[h]
        o_rows = []
        for c in range(n_chunks):
            r0 = c * HG_CHUNK
            bc = b[r0:r0 + HG_CHUNK]
            qc = q[r0:r0 + HG_CHUNK]
            kc = k[r0:r0 + HG_CHUNK]
            vc = v[r0:r0 + HG_CHUNK].astype(BF16)
            st_b = st.astype(BF16)
            for i in range(n_blk):
                i0 = i * HG_BLOCK
                if i == 0:
                    qt = qc[:HG_BLOCK] * jnp.exp(bc[:HG_BLOCK])
                    qs = qt
                else:
                    ref_row = bc[i0 - 1:i0]
                    qt = qc[i0:i0 + HG_BLOCK] * jnp.exp(bc[i0:i0 + HG_BLOCK] - ref_row)
                    qs = qt * jnp.exp(ref_row)
                oi = lax.dot_general(qs.astype(BF16), st_b, (((1,), (1,)), ((), ())),
                                     preferred_element_type=F32)
                if i > 0:
                    kh = kc[:i0] * jnp.exp(ref_row - bc[:i0])
                    a = lax.dot_general(qt.astype(BF16), kh.astype(BF16), (((1,), (1,)), ((), ())),
                                        preferred_element_type=F32)
                    oi = oi + jnp.dot(a.astype(BF16), vc[:i0], preferred_element_type=F32)
                o_rows.append(oi)
            b_end = bc[HG_CHUNK - 1:HG_CHUNK]
            kend = kc * jnp.exp(b_end - bc)
            vt = v[r0:r0 + HG_CHUNK].T.astype(BF16)
            st = st * jnp.exp(b_end) + jnp.dot(vt, kend.astype(BF16), preferred_element_type=F32)
        st_ref[h] = st
        o = o + jnp.concatenate(o_rows, axis=0)
        y = _rms(o, ng_ref[:, cs]) * _silu(gt_ref[:, cs].astype(F32))
        o_ref[:, cs] = y.astype(o_ref.dtype)


def _hgrn(hq, hf, hi, hg, lb, ng, bsz, seq, ts):
    t, w = hq.shape
    dk = w // HG_HEADS
    n_s = seq // ts
    tile = lambda b, s: (b * n_s + s, 0)
    return pl.pallas_call(
        functools.partial(_hgrn_kernel, ts),
        out_shape=jax.ShapeDtypeStruct((t, w), BF16),
        grid=(bsz, n_s),
        in_specs=[pl.BlockSpec((ts, w), tile)] * 4
        + [pl.BlockSpec((1, w), lambda b, s: (0, 0))] * 2,
        out_specs=pl.BlockSpec((ts, w), tile),
        scratch_shapes=[pltpu.VMEM((HG_HEADS, dk, dk), F32)],
        compiler_params=_params(2),
        name="hgrn2",
    )(hq, hf, hi, hg, lb.reshape(1, w), ng.reshape(1, w))


def _attn_kernel(nk, q_ref, kp_ref, kc_ref, vp_ref, vc_ref, o_ref, lse_ref):
    n = pl.program_id(2)
    e = ATT_HEAD_DIM
    i = lax.broadcasted_iota(I32, (nk, 2 * nk), 0)
    j = lax.broadcasted_iota(I32, (nk, 2 * nk), 1)
    valid = (j >= i) & (j <= i + nk) & ((j >= nk) | (n > 0))
    q = q_ref[0]
    kk = jnp.concatenate([kp_ref[0], kc_ref[0]], axis=0)
    vv = jnp.concatenate([vp_ref[0], vc_ref[0]], axis=0)
    for h in range(ATT_HEADS_PER_GROUP):
        cs = slice(h * e, (h + 1) * e)
        s = lax.dot_general(q[:, cs], kk[:, cs], (((1,), (1,)), ((), ())),
                            preferred_element_type=F32) * (e ** -0.5)
        s = jnp.where(valid, s, -jnp.inf)
        m = jnp.max(s, axis=-1, keepdims=True)
        p = jnp.exp(s - m)
        l = jnp.sum(p, axis=-1, keepdims=True)
        o = jnp.dot(p.astype(BF16), vv[:, cs], preferred_element_type=F32) / l
        o_ref[0, :, cs] = o
        lse_ref[0, :, cs] = jnp.broadcast_to(m + jnp.log(l), (nk, e))


def _attn_group(aq, ak, av, bsz, seq, g):
    window, dil = ATT_GROUPS[g]
    nk = window // dil
    n_groups = len(ATT_GROUPS)
    gw = ATT_HEADS_PER_GROUP * ATT_HEAD_DIM
    ln = seq // dil
    assert ln % nk == 0
    view = lambda a: a.reshape(bsz, ln, dil * n_groups * gw)
    cur = lambda b, r, n: (b, n, r * n_groups + g)
    prev = lambda b, r, n: (b, jnp.maximum(n - 1, 0), r * n_groups + g)
    blk = (1, nk, gw)
    o, lse = pl.pallas_call(
        functools.partial(_attn_kernel, nk),
        out_shape=[jax.ShapeDtypeStruct((bsz, ln, dil * gw), F32)] * 2,
        grid=(bsz, dil, ln // nk),
        in_specs=[pl.BlockSpec(blk, cur), pl.BlockSpec(blk, prev), pl.BlockSpec(blk, cur),
                  pl.BlockSpec(blk, prev), pl.BlockSpec(blk, cur)],
        out_specs=[pl.BlockSpec(blk, lambda b, r, n: (b, n, r))] * 2,
        compiler_params=_params(3),
        name=f"dilated_attn_g{g}",
    )(view(aq), view(ak), view(ak), view(av), view(av))
    return o.reshape(bsz * seq, gw), lse.reshape(bsz * seq, gw)


def _merge_kernel(ya_ref, o0_ref, o1_ref, o2_ref, l0_ref, l1_ref, l2_ref, ga_ref, gb_ref, x_ref,
                  g1_ref, sc2_ref, sh2_ref, g2_ref, n2_ref, wa_ref, wb_ref, wo_ref, wr_ref,
                  wsg_ref, wsu_ref, wsd_ref, x1_ref, hp_ref, lg_ref):
    l0, l1, l2 = l0_ref[...], l1_ref[...], l2_ref[...]
    m = jnp.maximum(jnp.maximum(l0, l1), l2)
    e0, e1, e2 = jnp.exp(l0 - m), jnp.exp(l1 - m), jnp.exp(l2 - m)
    yb = (e0 * o0_ref[...] + e1 * o1_ref[...] + e2 * o2_ref[...]) / (e0 + e1 + e2)
    merged = (_sigmoid(ga_ref[...].astype(F32))
              * jnp.dot(ya_ref[...], wa_ref[...], preferred_element_type=F32)
              + _sigmoid(gb_ref[...].astype(F32))
              * jnp.dot(yb.astype(BF16), wb_ref[...], preferred_element_type=F32))
    x1 = x_ref[...] + g1_ref[0] * jnp.dot(merged.astype(BF16), wo_ref[...],
                                           preferred_element_type=F32)
    h2 = _rms(x1, n2_ref[...]) * (1.0 + sc2_ref[0]) + sh2_ref[0]
    hb = h2.astype(BF16)
    act = (_silu(jnp.dot(hb, wsg_ref[...], preferred_element_type=F32))
           * jnp.dot(hb, wsu_ref[...], preferred_element_type=F32))
    shared = jnp.dot(act.astype(BF16), wsd_ref[...], preferred_element_type=F32)
    x1_ref[...] = x1 + g2_ref[0] * shared
    hp_ref[...] = _pack_halves(h2)
    lg_ref[...] = lax.dot_general(wr_ref[...], h2, (((1,), (1,)), ((), ())),
                                  preferred_element_type=F32, precision=HIGHEST)


def _merge(ya, att, ga, gb, x2, gate1, scale2, shift2, gate2, norm2_g, wa, wb, wo, wr_t, wsg, wsu,
           wsd, seq, tm):
    t, d = x2.shape
    n_e = wr_t.shape[0]
    per_b = lambda i: ((i * tm) // seq, 0, 0)
    rows = lambda wdt: pl.BlockSpec((tm, wdt), lambda i: (i, 0))
    full = lambda a: pl.BlockSpec(a.shape, lambda i: (0,) * a.ndim)
    vec = pl.BlockSpec((1, 1, d), per_b)
    (o0, l0), (o1, l1), (o2, l2) = att
    gw = o0.shape[1]
    return pl.pallas_call(
        _merge_kernel,
        out_shape=[jax.ShapeDtypeStruct((t, d), F32),
                   jax.ShapeDtypeStruct((t, d // 2), U32),
                   jax.ShapeDtypeStruct((n_e, t), F32)],
        grid=(t // tm,),
        in_specs=[rows(ya.shape[1])] + [rows(gw)] * 6 + [rows(d)] * 3
        + [vec, vec, vec, vec, pl.BlockSpec((1, d), lambda i: (0, 0))]
        + [full(a) for a in (wa, wb, wo, wr_t, wsg, wsu, wsd)],
        out_specs=[rows(d), rows(d // 2), pl.BlockSpec((n_e, tm), lambda i: (0, i))],
        compiler_params=_params(),
        name="merge_router",
    )(ya, o0, o1, o2, l0, l1, l2, ga, gb, x2, gate1, scale2, shift2, gate2,
      norm2_g.reshape(1, d), wa, wb, wo, wr_t, wsg, wsu, wsd)


def _topk_kernel(lg_ref, bias_ref, idx_ref, gate_ref, rank_ref, cnt_ref, carry_ref):
    n_e, tt = lg_ref.shape

    @pl.when(pl.program_id(0) == 0)
    def _():
        carry_ref[...] = jnp.zeros_like(carry_ref)

    scores = _sigmoid(lg_ref[...])
    sel = scores + bias_ref[...]
    eio = lax.broadcasted_iota(I32, (n_e, tt), 0)
    picked = jnp.zeros((n_e, tt), F32)
    idxs, vals = [], []
    for _ in range(TOP_K):
        m = jnp.max(sel, axis=0, keepdims=True)
        ik = jnp.min(jnp.where(sel == m, eio, n_e), axis=0, keepdims=True)
        hit = eio == ik
        vals.append(jnp.sum(jnp.where(hit, scores, 0.0), axis=0, keepdims=True))
        sel = jnp.where(hit, -jnp.inf, sel)
        picked = picked + jnp.where(hit, 1.0, 0.0)
        idxs.append(ik)
    denom = vals[0]
    for v in vals[1:]:
        denom = denom + v
    gate_ref[...] = jnp.concatenate([v / denom * ROUTE_SCALE for v in vals], axis=0)
    idx_ref[...] = jnp.concatenate(idxs, axis=0)

    upper = (lax.broadcasted_iota(I32, (tt, tt), 0) <= lax.broadcasted_iota(I32, (tt, tt), 1))
    incl = jnp.dot(picked.astype(BF16), jnp.where(upper, 1.0, 0.0).astype(BF16),
                   preferred_element_type=F32)
    before = incl - picked + carry_ref[...]
    rank_ref[...] = jnp.concatenate(
        [jnp.sum(jnp.where(eio == ik, before, 0.0), axis=0, keepdims=True) for ik in idxs],
        axis=0).astype(I32)
    carry_ref[...] = carry_ref[...] + jnp.sum(picked, axis=1, keepdims=True)
    cnt_ref[...] = jnp.broadcast_to(carry_ref[...], cnt_ref.shape).astype(I32)


def _topk(logits_t, bias, tt):
    n_e, t = logits_t.shape
    tok = pl.BlockSpec((TOP_K, tt), lambda i: (0, i))
    return pl.pallas_call(
        _topk_kernel,
        out_shape=[jax.ShapeDtypeStruct((TOP_K, t), I32), jax.ShapeDtypeStruct((TOP_K, t), F32),
                   jax.ShapeDtypeStruct((TOP_K, t), I32), jax.ShapeDtypeStruct((n_e, 128), I32)],
        grid=(t // tt,),
        in_specs=[pl.BlockSpec((n_e, tt), lambda i: (0, i)),
                  pl.BlockSpec((n_e, 1), lambda i: (0, 0))],
        out_specs=[tok, tok, tok, pl.BlockSpec((n_e, 128), lambda i: (0, 0))],
        scratch_shapes=[pltpu.VMEM((n_e, 1), F32)],
        compiler_params=_params(),
        name="router_topk",
    )(logits_t, bias.reshape(n_e, 1))


def _dest_kernel(idx_ref, rank_ref, start_ref, o_ref):
    k, tt = idx_ref.shape
    n_e = start_ref.shape[0]
    eio = lax.broadcasted_iota(I32, (n_e, tt), 0)
    start = start_ref[...]
    rows = [jnp.sum(jnp.where(eio == idx_ref[r:r + 1, :], start, 0), axis=0, keepdims=True)
            for r in range(k)]
    o_ref[...] = jnp.concatenate(rows, axis=0) + rank_ref[...]


def _dest(idx, rank, seg_start, tt):
    k, t = idx.shape
    n_e = seg_start.shape[0]
    tok = pl.BlockSpec((k, tt), lambda i: (0, i))
    return pl.pallas_call(
        _dest_kernel,
        out_shape=jax.ShapeDtypeStruct((k, t), I32),
        grid=(t // tt,),
        in_specs=[tok, tok, pl.BlockSpec((n_e, 1), lambda i: (0, 0))],
        out_specs=tok,
        compiler_params=_params(),
        name="moe_dest",
    )(idx, rank, seg_start.reshape(n_e, 1))


def _dispatch_kernel(dest_ref, h_ref, xs_ref, sem):
    k, tt = dest_ref.shape

    def row_copy(i, r):
        return pltpu.make_async_copy(h_ref.at[pl.ds(i, 1), :],
                                     xs_ref.at[pl.ds(dest_ref[r, i], 1), :], sem)

    def start(i, carry):
        for r in range(k):
            row_copy(i, r).start(priority=r % 2)
        return carry

    def wait(i, carry):
        for r in range(k):
            row_copy(i, r).wait()
        return carry

    lax.fori_loop(0, tt, start, 0)
    lax.fori_loop(0, tt, wait, 0)


def _dispatch(dest, hp, n_slots, tt):
    k, t = dest.shape
    w = hp.shape[1]
    return pl.pallas_call(
        _dispatch_kernel,
        out_shape=jax.ShapeDtypeStruct((n_slots, w), hp.dtype),
        grid=(t // tt,),
        in_specs=[pl.BlockSpec((k, tt), lambda i: (0, i), memory_space=pltpu.SMEM),
                  pl.BlockSpec((tt, w), lambda i: (i, 0))],
        out_specs=pl.BlockSpec(memory_space=pl.ANY),
        scratch_shapes=[pltpu.SemaphoreType.DMA],
        compiler_params=_params(),
        name="moe_dispatch",
    )(dest, hp)


def _expert_kernel(start_ref, nblk_ref, xs_ref, wg_ref, wu_ref, wd_ref, ys_ref,
                   xbuf, ybuf, wgb, wub, wdb, sem_in, sem_out):
    e = pl.program_id(0)
    nb = nblk_ref[e]
    base = start_ref[e]

    def rows(j):
        return pl.ds(pl.multiple_of(base + j * MOE_BLOCK, MOE_BLOCK), MOE_BLOCK)

    def in_copy(j, slot):
        return pltpu.make_async_copy(xs_ref.at[rows(j), :], xbuf.at[slot], sem_in.at[slot])

    def out_copy(j, slot):
        return pltpu.make_async_copy(ybuf.at[slot], ys_ref.at[rows(j), :], sem_out.at[slot])

    @pl.when(nb > 0)
    def _():
        in_copy(0, 0).start()
        wgb[...] = wg_ref[0].astype(BF16)
        wub[...] = wu_ref[0].astype(BF16)
        wdb[...] = wd_ref[0].astype(BF16)
        half = xbuf.shape[2]

        def body(j, carry):
            slot = lax.rem(j, 2)
            in_copy(j, slot).wait()

            @pl.when(j + 1 < nb)
            def _():
                in_copy(j + 1, 1 - slot).start()

            lo, hi = _unpack_halves(xbuf[slot])
            lo, hi = lo.astype(BF16), hi.astype(BF16)
            gate = (jnp.dot(lo, wgb[:half], preferred_element_type=F32)
                    + jnp.dot(hi, wgb[half:], preferred_element_type=F32))
            up = (jnp.dot(lo, wub[:half], preferred_element_type=F32)
                  + jnp.dot(hi, wub[half:], preferred_element_type=F32))
            act = (_silu(gate) * up).astype(BF16)
            y = jnp.dot(act, wdb[...], preferred_element_type=F32)

            @pl.when(j >= 2)
            def _():
                out_copy(j - 2, slot).wait()

            ybuf[slot] = _pack_halves(y)
            out_copy(j, slot).start()
            return carry

        lax.fori_loop(0, nb, body, 0)

        @pl.when(nb >= 2)
        def _():
            out_copy(nb - 2, lax.rem(nb, 2)).wait()

        out_copy(nb - 1, lax.rem(nb - 1, 2)).wait()


def _experts(seg_start, seg_blocks, xs, wg, wu, wd):
    n_slots, half = xs.shape
    n_e, d, de = wg.shape
    return pl.pallas_call(
        _expert_kernel,
        out_shape=jax.ShapeDtypeStruct((n_slots, half), U32),
        grid_spec=pltpu.PrefetchScalarGridSpec(
            num_scalar_prefetch=2,
            grid=(n_e,),
            in_specs=[pl.BlockSpec(memory_space=pl.ANY),
                      pl.BlockSpec((1, d, de), lambda e, s, n: (e, 0, 0)),
                      pl.BlockSpec((1, d, de), lambda e, s, n: (e, 0, 0)),
                      pl.BlockSpec((1, de, d), lambda e, s, n: (e, 0, 0))],
            out_specs=pl.BlockSpec(memory_space=pl.ANY),
            scratch_shapes=[pltpu.VMEM((2, MOE_BLOCK, half), U32),
                            pltpu.VMEM((2, MOE_BLOCK, half), U32),
                            pltpu.VMEM((d, de), BF16), pltpu.VMEM((d, de), BF16),
                            pltpu.VMEM((de, d), BF16),
                            pltpu.SemaphoreType.DMA((2,)), pltpu.SemaphoreType.DMA((2,))]),
        compiler_params=_params(),
        name="moe_experts",
    )(seg_start, seg_blocks, xs, wg, wu, wd)


def _combine_kernel(dest_ref, dnext_ref, gt_ref, x_ref, g2_ref, fg_ref, ys_ref, o_ref, buf_ref, sem):
    k, tc = dest_ref.shape
    step = pl.program_id(0)
    slot = lax.rem(step, 2)

    def row_copy(d_ref, s, i, r):
        return pltpu.make_async_copy(ys_ref.at[pl.ds(d_ref[r, i], 1), :],
                                     buf_ref.at[s, r, pl.ds(i, 1), :], sem.at[s])

    def start_all(d_ref, s):
        def body(i, carry):
            for r in range(k):
                row_copy(d_ref, s, i, r).start(priority=r % 2)
            return carry
        lax.fori_loop(0, tc, body, 0)

    @pl.when(step == 0)
    def _():
        start_all(dest_ref, 0)

    @pl.when(step + 1 < pl.num_programs(0))
    def _():
        start_all(dnext_ref, 1 - slot)

    def wait_body(i, carry):
        for r in range(k):
            row_copy(dest_ref, slot, i, r).wait()
        return carry

    lax.fori_loop(0, tc, wait_body, 0)

    gt = gt_ref[...]
    lo, hi = _unpack_halves(buf_ref[slot, 0])
    y_lo, y_hi = lo * gt[:, 0:1], hi * gt[:, 0:1]
    for r in range(1, k):
        lo, hi = _unpack_halves(buf_ref[slot, r])
        y_lo, y_hi = y_lo + lo * gt[:, r:r + 1], y_hi + hi * gt[:, r:r + 1]
    y = jnp.concatenate([y_lo, y_hi], axis=1)
    o_ref[...] = _rms(x_ref[...] + g2_ref[0] * y, fg_ref[...])


def _combine(dest, gates_t, x1s, gate2, final_g, ys, seq, tc):
    k, t = dest.shape
    d = x1s.shape[1]
    n_steps = t // tc
    return pl.pallas_call(
        _combine_kernel,
        out_shape=jax.ShapeDtypeStruct((t, d), F32),
        grid=(n_steps,),
        in_specs=[pl.BlockSpec((k, tc), lambda i: (0, i), memory_space=pltpu.SMEM),
                  pl.BlockSpec((k, tc), lambda i: (0, jnp.minimum(i + 1, n_steps - 1)),
                               memory_space=pltpu.SMEM),
                  pl.BlockSpec((tc, k), lambda i: (i, 0)),
                  pl.BlockSpec((tc, d), lambda i: (i, 0)),
                  pl.BlockSpec((1, 1, d), lambda i: ((i * tc) // seq, 0, 0)),
                  pl.BlockSpec((1, d), lambda i: (0, 0)),
                  pl.BlockSpec(memory_space=pl.ANY)],
        out_specs=pl.BlockSpec((tc, d), lambda i: (i, 0)),
        scratch_shapes=[pltpu.VMEM((2, k, tc, d // 2), U32), pltpu.SemaphoreType.DMA((2,))],
        compiler_params=_params(),
        name="moe_combine",
    )(dest, dest, gates_t, x1s, gate2, final_g.reshape(1, d), ys)


def _layer(x2, c, bsz, seq, lb_row, ada_w, ada_b, norm1_g, w_in, hg_norm_g, w_branch_a, w_branch_b,
           w_out, norm2_g, w_router, router_bias, w_exp_gate, w_exp_up, w_exp_down, w_sh_gate,
           w_sh_up, w_sh_down, final_g):
    t, d = x2.shape
    n_e = w_router.shape[1]
    mod = _ada(c, ada_w, ada_b).reshape(bsz, 6, 1, d)
    shift1, scale1, gate1, shift2, scale2, gate2 = (mod[:, j] for j in range(6))

    hw = hg_norm_g.shape[0]
    aw = len(ATT_GROUPS) * ATT_HEADS_PER_GROUP * ATT_HEAD_DIM
    segs = [(hw, BF16), (hw, F32), (hw, BF16), (hw, BF16), (aw, BF16), (aw, BF16), (aw, BF16),
            (d, BF16), (d, BF16)]
    hq, hf, hi, hg, aq, ak, av, ga, gb = _inproj(
        x2, norm1_g, scale1, shift1, w_in.astype(BF16), seq, segs, tm=512)

    ya = _hgrn(hq, hf, hi, hg, lb_row, hg_norm_g, bsz, seq, ts=256)
    att = [_attn_group(aq, ak, av, bsz, seq, g) for g in range(len(ATT_GROUPS))]

    x1s, hp, logits_t = _merge(
        ya, att, ga, gb, x2, gate1, scale2, shift2, gate2, norm2_g, w_branch_a.astype(BF16),
        w_branch_b.astype(BF16), w_out.astype(BF16), w_router.T, w_sh_gate.astype(BF16),
        w_sh_up.astype(BF16), w_sh_down.astype(BF16), seq, tm=256)

    idx, gates, rank, cnt = _topk(logits_t, router_bias, tt=512)
    counts = cnt[:, 0]
    padded = (counts + MOE_BLOCK - 1) // MOE_BLOCK * MOE_BLOCK
    seg_start = (jnp.cumsum(padded) - padded).astype(I32)
    n_blocks = -(-(t * TOP_K) // MOE_BLOCK) + n_e
    dest = _dest(idx, rank, seg_start, tt=512)

    xs = _dispatch(dest, hp, n_blocks * MOE_BLOCK, tt=256)
    ys = _experts(seg_start, (padded // MOE_BLOCK).astype(I32), xs, w_exp_gate, w_exp_up,
                  w_exp_down)
    return _combine(dest, gates.T, x1s, gate2, final_g, ys, seq, tc=128)


def kernel(x, c, ada_w, ada_b, norm1_g, w_in, lb_logits, hg_norm_g, w_branch_a, w_branch_b, w_out,
           norm2_g, w_router, router_bias, w_exp_gate, w_exp_up, w_exp_down, w_sh_gate, w_sh_up,
           w_sh_down, final_g):
    bsz, seq, d = x.shape
    depth = ada_w.shape[0]
    assert depth == 1, "the last layer's kernels also apply the final norm"
    lb_table = jnp.cumsum(jax.nn.softmax(lb_logits.astype(F32), axis=0), axis=0)
    out = _layer(x.reshape(bsz * seq, d), c, bsz, seq, lb_table[0], ada_w[0], ada_b[0], norm1_g[0],
                 w_in[0], hg_norm_g[0], w_branch_a[0], w_branch_b[0], w_out[0], norm2_g[0],
                 w_router[0], router_bias[0], w_exp_gate[0], w_exp_up[0], w_exp_down[0],
                 w_sh_gate[0], w_sh_up[0], w_sh_down[0], final_g)
    return out.reshape(bsz, seq, d)
```

```python
import functools

import jax
import jax.numpy as jnp
from jax import lax
from jax.experimental import pallas as pl
from jax.experimental.pallas import tpu as pltpu

F32 = jnp.float32
BF16 = jnp.bfloat16
I32 = jnp.int32
U32 = jnp.uint32
HIGHEST = lax.Precision.HIGHEST

HG_HEADS = 4
HG_BLOCK = 16
HG_CHUNK = 64
HG_MILD_DECAY = -60.0
ATT_GROUPS = ((128, 1), (512, 4), (2048, 16))
ATT_HEADS_PER_GROUP = 4
ATT_HEAD_DIM = 64
TOP_K = 8
ROUTE_SCALE = 2.5
MOE_BLOCK = 256
RMS_EPS = 1e-6
BLOCK_DMA_PRIORITY = 1

LANES = 128
VMEM_LIMIT_BYTES = 56 * 1024 * 1024


def _sigmoid(x):
    return 1.0 / (1.0 + jnp.exp(-x))


def _silu(x):
    return x * _sigmoid(x)


def _rms(x, g):
    return x * lax.rsqrt(jnp.mean(x * x, axis=-1, keepdims=True) + RMS_EPS) * g


def _pack_halves(x):
    n = x.shape[1] // 2
    bits = lax.bitcast_convert_type(x.astype(BF16).astype(F32), U32)
    return (bits[:, :n] >> 16) | (bits[:, n:] & jnp.uint32(0xFFFF0000))


def _unpack_halves(word):
    lo = lax.bitcast_convert_type(word << 16, F32)
    hi = lax.bitcast_convert_type(word & jnp.uint32(0xFFFF0000), F32)
    return lo, hi


def _params(n_axes=1):
    return pltpu.CompilerParams(
        dimension_semantics=("arbitrary",) * n_axes, vmem_limit_bytes=VMEM_LIMIT_BYTES)


def _ada_kernel(c_ref, w_ref, b_ref, o_ref):
    sc = _silu(c_ref[...])
    o_ref[...] = jnp.dot(sc, w_ref[...], preferred_element_type=F32, precision=HIGHEST) + b_ref[...]


def _ada(c, w, b):
    bsz, d = c.shape
    n = w.shape[1]
    return pl.pallas_call(
        _ada_kernel,
        out_shape=jax.ShapeDtypeStruct((bsz, n), F32),
        grid=(n // d,),
        in_specs=[pl.BlockSpec((bsz, d), lambda j: (0, 0)),
                  pl.BlockSpec((d, d), lambda j: (0, j)),
                  pl.BlockSpec((1, d), lambda j: (0, j))],
        out_specs=pl.BlockSpec((bsz, d), lambda j: (0, j)),
        compiler_params=_params(),
        name="ada_mod",
    )(c, w, b.reshape(1, n))


def _inproj_kernel(n_flat, flat_ranges, att_c0, x_ref, g_ref, sc_ref, sh_ref, w_ref, *refs):
    flat_refs, att_refs, scr = refs[:n_flat], refs[n_flat:-1], refs[-1]
    tm = x_ref.shape[0]
    h = _rms(x_ref[...], g_ref[...]) * (1.0 + sc_ref[0]) + sh_ref[0]
    hb = h.astype(BF16)
    for (c0, c1), o_ref in zip(flat_ranges, flat_refs):
        o_ref[...] = jnp.dot(hb, w_ref[:, c0:c1], preferred_element_type=F32).astype(o_ref.dtype)
    gw = ATT_HEADS_PER_GROUP * ATT_HEAD_DIM
    n_groups = len(ATT_GROUPS)
    for part in range(3):
        c0 = att_c0 + part * n_groups * gw
        res = jnp.dot(hb, w_ref[:, c0:c0 + n_groups * gw], preferred_element_type=F32)
        if part == 0:
            res = res * (ATT_HEAD_DIM ** -0.5)
        for g, (_, dil) in enumerate(ATT_GROUPS):
            o_ref = att_refs[g * 3 + part]
            sub = res[:, g * gw:(g + 1) * gw]
            if dil == 1:
                o_ref[0, 0] = sub.astype(BF16)
            else:
                for c in range(gw // LANES):
                    scr[c] = sub[:, c * LANES:(c + 1) * LANES]
                for r in range(dil):
                    o_ref[0, r] = jnp.concatenate(
                        [scr[c, pl.ds(r, tm // dil, stride=dil), :] for c in range(gw // LANES)],
                        axis=1).astype(BF16)


def _inproj(x2, g, scale, shift, w_bf16, bsz, seq, flat_segs, att_c0, tm):
    t, d = x2.shape
    gw = ATT_HEADS_PER_GROUP * ATT_HEAD_DIM
    n_per = seq // tm
    per_b = lambda i: (i // n_per, 0, 0)
    att_shapes, att_specs = [], []
    for _, dil in ATT_GROUPS:
        for _ in range(3):
            att_shapes.append(jax.ShapeDtypeStruct((bsz, dil, seq // dil, gw), BF16))
            att_specs.append(pl.BlockSpec((1, dil, tm // dil, gw),
                                          lambda i: (i // n_per, 0, i % n_per, 0)))
    outs = pl.pallas_call(
        functools.partial(_inproj_kernel, len(flat_segs),
                          tuple((c0, c0 + wdt) for c0, wdt, _ in flat_segs), att_c0),
        out_shape=[jax.ShapeDtypeStruct((t, wdt), dt) for _, wdt, dt in flat_segs] + att_shapes,
        grid=(t // tm,),
        in_specs=[pl.BlockSpec((tm, d), lambda i: (i, 0)),
                  pl.BlockSpec((1, d), lambda i: (0, 0)),
                  pl.BlockSpec((1, 1, d), per_b),
                  pl.BlockSpec((1, 1, d), per_b),
                  pl.BlockSpec(w_bf16.shape, lambda i: (0, 0))],
        out_specs=[pl.BlockSpec((tm, wdt), lambda i: (i, 0)) for _, wdt, _ in flat_segs]
        + att_specs,
        scratch_shapes=[pltpu.VMEM((gw // LANES, tm, LANES), F32)],
        compiler_params=_params(),
        name="in_proj",
    )(x2, g.reshape(1, d), scale, shift, w_bf16)
    return outs[:len(flat_segs)], outs[len(flat_segs):]


def _hgrn_kernel(ts, q_ref, f_ref, v_ref, gt_ref, lb_ref, ng_ref, o_ref, st_ref, b_ref):
    dk = q_ref.shape[1] // HG_HEADS
    n_chunks = ts // HG_CHUNK
    n_blk = HG_CHUNK // HG_BLOCK

    @pl.when(pl.program_id(1) == 0)
    def _():
        st_ref[...] = jnp.zeros_like(st_ref)

    row = lax.broadcasted_iota(I32, (ts, ts), 0)
    col = lax.broadcasted_iota(I32, (ts, ts), 1)
    same_chunk = (row // HG_CHUNK) == (col // HG_CHUNK)
    cum_mat = jnp.where(same_chunk & (col <= row), 1.0, 0.0).astype(F32)

    def forget(cs):
        lb = lb_ref[:, cs]
        return lb + (1.0 - lb) * _sigmoid(f_ref[:, cs])

    b_min = None
    for h in range(HG_HEADS):
        cs = slice(h * dk, (h + 1) * dk)
        b = jnp.dot(cum_mat, jnp.log(forget(cs)), preferred_element_type=F32, precision=HIGHEST)
        b_ref[:, cs] = b
        m = jnp.min(b)
        b_min = m if b_min is None else jnp.minimum(b_min, m)
    mild = b_min >= HG_MILD_DECAY

    def finish(h, o, st):
        cs = slice(h * dk, (h + 1) * dk)
        st_ref[h] = st
        y = _rms(o, ng_ref[:, cs]) * _silu(gt_ref[:, cs].astype(F32))
        o_ref[:, cs] = y.astype(o_ref.dtype)

    @pl.when(mild)
    def _():
        causal = (lax.broadcasted_iota(I32, (HG_CHUNK, HG_CHUNK), 0)
                  >= lax.broadcasted_iota(I32, (HG_CHUNK, HG_CHUNK), 1))
        for h in range(HG_HEADS):
            cs = slice(h * dk, (h + 1) * dk)
            v = v_ref[:, cs]
            eb = jnp.exp(b_ref[:, cs])
            qe = (q_ref[:, cs].astype(F32) * eb).astype(BF16)
            ke = (1.0 - forget(cs)) / eb
            st = st_ref[h]
            o_rows = []
            for c in range(n_chunks):
                sl = slice(c * HG_CHUNK, (c + 1) * HG_CHUNK)
                a = lax.dot_general(qe[sl], ke[sl].astype(BF16), (((1,), (1,)), ((), ())),
                                    preferred_element_type=F32)
                a = jnp.where(causal, a, 0.0).astype(BF16)
                o_rows.append(
                    jnp.dot(a, v[sl], preferred_element_type=F32)
                    + lax.dot_general(qe[sl], st.astype(BF16), (((1,), (1,)), ((), ())),
                                      preferred_element_type=F32))
                e_end = eb[(c + 1) * HG_CHUNK - 1:(c + 1) * HG_CHUNK]
                kend = (ke[sl] * e_end).astype(BF16)
                vt = v[sl].astype(F32).T.astype(BF16)
                st = st * e_end + jnp.dot(vt, kend, preferred_element_type=F32)
            finish(h, jnp.concatenate(o_rows, axis=0), st)

    @pl.when(jnp.logical_not(mild))
    def _():
        _hgrn_steep(ts, dk, n_chunks, n_blk, q_ref, v_ref, b_ref, st_ref, forget, finish)


def _hgrn_steep(ts, dk, n_chunks, n_blk, q_ref, v_ref, b_ref, st_ref, forget, finish):
    t_in_blk = lax.broadcasted_iota(I32, (ts, dk), 0) % HG_BLOCK

    for h in range(HG_HEADS):
        cs = slice(h * dk, (h + 1) * dk)
        q = q_ref[:, cs].astype(F32)
        v = v_ref[:, cs].astype(F32)
        k = 1.0 - forget(cs)
        b = b_ref[:, cs]

        o = jnp.sum(q * k, axis=-1, keepdims=True) * v
        for d in range(1, HG_BLOCK):
            k_d = pltpu.roll(k, d, axis=0)
            b_d = pltpu.roll(b, d, axis=0)
            v_d = pltpu.roll(v, d, axis=0)
            w = jnp.sum(q * k_d * jnp.exp(jnp.minimum(b - b_d, 0.0)), axis=-1, keepdims=True)
            o = o + jnp.where(t_in_blk >= d, w * v_d, 0.0)

        st = st_ref[h]
        o_rows = []
        for c in range(n_chunks):
            r0 = c * HG_CHUNK
            bc = b[r0:r0 + HG_CHUNK]
            qc = q[r0:r0 + HG_CHUNK]
            kc = k[r0:r0 + HG_CHUNK]
            vc = v[r0:r0 + HG_CHUNK].astype(BF16)
            st_b = st.astype(BF16)
            for i in range(n_blk):
                i0 = i * HG_BLOCK
                if i == 0:
                    qt = qc[:HG_BLOCK] * jnp.exp(bc[:HG_BLOCK])
                    qs = qt
                else:
                    ref_row = bc[i0 - 1:i0]
                    qt = qc[i0:i0 + HG_BLOCK] * jnp.exp(bc[i0:i0 + HG_BLOCK] - ref_row)
                    qs = qt * jnp.exp(ref_row)
                oi = lax.dot_general(qs.astype(BF16), st_b, (((1,), (1,)), ((), ())),
                                     preferred_element_type=F32)
                if i > 0:
                    kh = kc[:i0] * jnp.exp(ref_row - bc[:i0])
                    a = lax.dot_general(qt.astype(BF16), kh.astype(BF16), (((1,), (1,)), ((), ())),
                                        preferred_element_type=F32)
                    oi = oi + jnp.dot(a.astype(BF16), vc[:i0], preferred_element_type=F32)
                o_rows.append(oi)
            b_end = bc[HG_CHUNK - 1:HG_CHUNK]
            kend = kc * jnp.exp(b_end - bc)
            vt = v[r0:r0 + HG_CHUNK].T.astype(BF16)
            st = st * jnp.exp(b_end) + jnp.dot(vt, kend.astype(BF16), preferred_element_type=F32)
        finish(h, o + jnp.concatenate(o_rows, axis=0), st)


def _hgrn(hq, hf, hi, hg, lb, ng, bsz, seq, ts):
    t, w = hq.shape
    dk = w // HG_HEADS
    n_s = seq // ts
    tile = lambda b, s: (b * n_s + s, 0)
    return pl.pallas_call(
        functools.partial(_hgrn_kernel, ts),
        out_shape=jax.ShapeDtypeStruct((t, w), BF16),
        grid=(bsz, n_s),
        in_specs=[pl.BlockSpec((ts, w), tile)] * 4
        + [pl.BlockSpec((1, w), lambda b, s: (0, 0))] * 2,
        out_specs=pl.BlockSpec((ts, w), tile),
        scratch_shapes=[pltpu.VMEM((HG_HEADS, dk, dk), F32), pltpu.VMEM((ts, w), F32)],
        compiler_params=_params(2),
        name="hgrn2",
    )(hq, hf, hi, hg, lb.reshape(1, w), ng.reshape(1, w))


def _attn_kernel(nk, nq, q_ref, kp_ref, kc_ref, vp_ref, vc_ref, o_ref, lse_ref):
    n = pl.program_id(2)
    e = ATT_HEAD_DIM
    i = lax.broadcasted_iota(I32, (nk, 2 * nk), 0)
    j = lax.broadcasted_iota(I32, (nk, 2 * nk), 1)
    band = (j >= i) & (j <= i + nk)
    kk = jnp.concatenate([kp_ref[0, 0], kc_ref[0, 0]], axis=0)
    vv = jnp.concatenate([vp_ref[0, 0], vc_ref[0, 0]], axis=0)
    for b in range(nq):
        valid = band & ((j >= nk) | (n * nq + b > 0))
        q = q_ref[0, 0, b * nk:(b + 1) * nk]
        kb = kk[b * nk:(b + 2) * nk]
        vb = vv[b * nk:(b + 2) * nk]
        for h in range(ATT_HEADS_PER_GROUP):
            cs = slice(h * e, (h + 1) * e)
            s = lax.dot_general(q[:, cs], kb[:, cs], (((1,), (1,)), ((), ())),
                                preferred_element_type=F32)
            s = jnp.where(valid, s, -jnp.inf)
            m = jnp.max(s, axis=-1, keepdims=True)
            p = jnp.exp(s - m)
            l = jnp.sum(p, axis=-1, keepdims=True)
            o = jnp.dot(p.astype(BF16), vb[:, cs], preferred_element_type=F32) / l
            o_ref[0, 0, b * nk:(b + 1) * nk, cs] = o
            lse_ref[0, 0, b * nk:(b + 1) * nk, cs] = jnp.broadcast_to(m + jnp.log(l), (nk, e))


def _attn_group(q, k, v, g, nq):
    window, dil = ATT_GROUPS[g]
    nk = window // dil
    bsz, _, ln, gw = q.shape
    nq = min(nq, ln // nk)
    assert ln % (nk * nq) == 0
    cur = pl.BlockSpec((1, 1, nq * nk, gw), lambda b, r, n: (b, r, n, 0))
    prev = pl.BlockSpec((1, 1, nk, gw), lambda b, r, n: (b, r, jnp.maximum(n * nq - 1, 0), 0))
    return pl.pallas_call(
        functools.partial(_attn_kernel, nk, nq),
        out_shape=[jax.ShapeDtypeStruct(q.shape, F32)] * 2,
        grid=(bsz, dil, ln // (nk * nq)),
        in_specs=[cur, prev, cur, prev, cur],
        out_specs=[cur, cur],
        compiler_params=_params(3),
        name=f"dilated_attn_g{g}",
    )(q, k, k, v, v)


def _token_major(ref, scr):
    dil, rows = ref.shape[1], ref.shape[2]
    if dil == 1:
        return ref[0, 0]
    n_col = scr.shape[0]
    for r in range(dil):
        for c in range(n_col):
            scr[c, pl.ds(r, rows, stride=dil), :] = ref[0, r, :, c * LANES:(c + 1) * LANES]
    return jnp.concatenate([scr[c] for c in range(n_col)], axis=1)


def _merge_kernel(ya_ref, o0_ref, o1_ref, o2_ref, l0_ref, l1_ref, l2_ref, ga_ref, gb_ref, x_ref,
                  g1_ref, sc2_ref, sh2_ref, g2_ref, n2_ref, wa_ref, wb_ref, wo_ref, wr_ref,
                  wsg_ref, wsu_ref, wsd_ref, x1_ref, hp_ref, lg_ref, *scr):
    l0, l1, l2 = (_token_major(r, s) for r, s in zip((l0_ref, l1_ref, l2_ref), scr[:3]))
    o0, o1, o2 = (_token_major(r, s) for r, s in zip((o0_ref, o1_ref, o2_ref), scr[3:]))
    m = jnp.maximum(jnp.maximum(l0, l1), l2)
    e0, e1, e2 = jnp.exp(l0 - m), jnp.exp(l1 - m), jnp.exp(l2 - m)
    yb = (e0 * o0 + e1 * o1 + e2 * o2) / (e0 + e1 + e2)
    merged = (_sigmoid(ga_ref[...].astype(F32))
              * jnp.dot(ya_ref[...], wa_ref[...], preferred_element_type=F32)
              + _sigmoid(gb_ref[...].astype(F32))
              * jnp.dot(yb.astype(BF16), wb_ref[...], preferred_element_type=F32))
    x1 = x_ref[...] + g1_ref[0] * jnp.dot(merged.astype(BF16), wo_ref[...],
                                           preferred_element_type=F32)
    h2 = _rms(x1, n2_ref[...]) * (1.0 + sc2_ref[0]) + sh2_ref[0]
    hb = h2.astype(BF16)
    act = (_silu(jnp.dot(hb, wsg_ref[...], preferred_element_type=F32))
           * jnp.dot(hb, wsu_ref[...], preferred_element_type=F32))
    shared = jnp.dot(act.astype(BF16), wsd_ref[...], preferred_element_type=F32)
    x1_ref[...] = x1 + g2_ref[0] * shared
    hp_ref[...] = _pack_halves(h2)
    lg_ref[...] = lax.dot_general(wr_ref[...], h2, (((1,), (1,)), ((), ())),
                                  preferred_element_type=F32, precision=HIGHEST)


def _merge(ya, att, ga, gb, x2, gate1, scale2, shift2, gate2, norm2_g, wa, wb, wo, wr_t, wsg, wsu,
           wsd, seq, tm):
    t, d = x2.shape
    n_e = wr_t.shape[0]
    n_per = seq // tm
    per_b = lambda i: (i // n_per, 0, 0)
    rows = lambda wdt: pl.BlockSpec((tm, wdt), lambda i: (i, 0))
    full = lambda a: pl.BlockSpec(a.shape, lambda i: (0,) * a.ndim)
    vec = pl.BlockSpec((1, 1, d), per_b)
    (o0, l0), (o1, l1), (o2, l2) = att
    gw = o0.shape[3]
    by_residue = lambda a: pl.BlockSpec((1, a.shape[1], tm // a.shape[1], gw),
                                        lambda i: (i // n_per, 0, i % n_per, 0))
    att_in = (o0, o1, o2, l0, l1, l2)
    return pl.pallas_call(
        _merge_kernel,
        out_shape=[jax.ShapeDtypeStruct((t, d), F32),
                   jax.ShapeDtypeStruct((t, d // 2), U32),
                   jax.ShapeDtypeStruct((n_e, t), F32)],
        grid=(t // tm,),
        in_specs=[rows(ya.shape[1])] + [by_residue(a) for a in att_in] + [rows(d)] * 3
        + [vec, vec, vec, vec, pl.BlockSpec((1, d), lambda i: (0, 0))]
        + [full(a) for a in (wa, wb, wo, wr_t, wsg, wsu, wsd)],
        out_specs=[rows(d), rows(d // 2), pl.BlockSpec((n_e, tm), lambda i: (0, i))],
        scratch_shapes=[pltpu.VMEM((gw // LANES, tm, LANES), F32)] * 6,
        compiler_params=_params(),
        name="merge_router",
    )(ya, *att_in, ga, gb, x2, gate1, scale2, shift2, gate2,
      norm2_g.reshape(1, d), wa, wb, wo, wr_t, wsg, wsu, wsd)


def _topk_kernel(lg_ref, bias_ref, idx_ref, gate_ref, rank_ref, cnt_ref, carry_ref):
    n_e, tt = lg_ref.shape

    @pl.when(pl.program_id(0) == 0)
    def _():
        carry_ref[...] = jnp.zeros_like(carry_ref)

    scores = _sigmoid(lg_ref[...])
    sel = scores + bias_ref[...]
    eio = lax.broadcasted_iota(I32, (n_e, tt), 0)
    picked = jnp.zeros((n_e, tt), F32)
    idxs, vals = [], []
    for _ in range(TOP_K):
        m = jnp.max(sel, axis=0, keepdims=True)
        ik = jnp.min(jnp.where(sel == m, eio, n_e), axis=0, keepdims=True)
        hit = eio == ik
        vals.append(jnp.sum(jnp.where(hit, scores, 0.0), axis=0, keepdims=True))
        sel = jnp.where(hit, -jnp.inf, sel)
        picked = picked + jnp.where(hit, 1.0, 0.0)
        idxs.append(ik)
    denom = vals[0]
    for v in vals[1:]:
        denom = denom + v
    gate_ref[...] = jnp.concatenate([v / denom * ROUTE_SCALE for v in vals], axis=0)
    idx_ref[...] = jnp.concatenate(idxs, axis=0)

    upper = (lax.broadcasted_iota(I32, (tt, tt), 0) <= lax.broadcasted_iota(I32, (tt, tt), 1))
    incl = jnp.dot(picked.astype(BF16), jnp.where(upper, 1.0, 0.0).astype(BF16),
                   preferred_element_type=F32)
    before = incl - picked + carry_ref[...]
    rank_ref[...] = jnp.concatenate(
        [jnp.sum(jnp.where(eio == ik, before, 0.0), axis=0, keepdims=True) for ik in idxs],
        axis=0).astype(I32)
    carry_ref[...] = carry_ref[...] + jnp.sum(picked, axis=1, keepdims=True)
    cnt_ref[...] = jnp.broadcast_to(carry_ref[...], cnt_ref.shape).astype(I32)


def _topk(logits_t, bias, tt):
    n_e, t = logits_t.shape
    tok = pl.BlockSpec((TOP_K, tt), lambda i: (0, i))
    return pl.pallas_call(
        _topk_kernel,
        out_shape=[jax.ShapeDtypeStruct((TOP_K, t), I32), jax.ShapeDtypeStruct((TOP_K, t), F32),
                   jax.ShapeDtypeStruct((TOP_K, t), I32), jax.ShapeDtypeStruct((n_e, 128), I32)],
        grid=(t // tt,),
        in_specs=[pl.BlockSpec((n_e, tt), lambda i: (0, i)),
                  pl.BlockSpec((n_e, 1), lambda i: (0, 0))],
        out_specs=[tok, tok, tok, pl.BlockSpec((n_e, 128), lambda i: (0, 0))],
        scratch_shapes=[pltpu.VMEM((n_e, 1), F32)],
        compiler_params=_params(),
        name="router_topk",
    )(logits_t, bias.reshape(n_e, 1))


def _dest_kernel(idx_ref, rank_ref, start_ref, o_ref):
    k, tt = idx_ref.shape
    n_e = start_ref.shape[0]
    eio = lax.broadcasted_iota(I32, (n_e, tt), 0)
    start = start_ref[...]
    rows = [jnp.sum(jnp.where(eio == idx_ref[r:r + 1, :], start, 0), axis=0, keepdims=True)
            for r in range(k)]
    o_ref[...] = jnp.concatenate(rows, axis=0) + rank_ref[...]


def _dest(idx, rank, seg_start, tt):
    k, t = idx.shape
    n_e = seg_start.shape[0]
    tok = pl.BlockSpec((k, tt), lambda i: (0, i))
    return pl.pallas_call(
        _dest_kernel,
        out_shape=jax.ShapeDtypeStruct((k, t), I32),
        grid=(t // tt,),
        in_specs=[tok, tok, pl.BlockSpec((n_e, 1), lambda i: (0, 0))],
        out_specs=tok,
        compiler_params=_params(),
        name="moe_dest",
    )(idx, rank, seg_start.reshape(n_e, 1))


def _dispatch_kernel(dest_ref, h_ref, xs_ref, sem):
    k, tt = dest_ref.shape

    def row_copy(i, r):
        return pltpu.make_async_copy(h_ref.at[pl.ds(i, 1), :],
                                     xs_ref.at[pl.ds(dest_ref[r, i], 1), :], sem)

    def start(i, carry):
        for r in range(k):
            row_copy(i, r).start(priority=r % 2)
        return carry

    def wait(i, carry):
        for r in range(k):
            row_copy(i, r).wait()
        return carry

    lax.fori_loop(0, tt, start, 0)
    lax.fori_loop(0, tt, wait, 0)


def _dispatch(dest, hp, n_slots, tt):
    k, t = dest.shape
    w = hp.shape[1]
    return pl.pallas_call(
        _dispatch_kernel,
        out_shape=jax.ShapeDtypeStruct((n_slots, w), hp.dtype),
        grid=(t // tt,),
        in_specs=[pl.BlockSpec((k, tt), lambda i: (0, i), memory_space=pltpu.SMEM),
                  pl.BlockSpec((tt, w), lambda i: (i, 0))],
        out_specs=pl.BlockSpec(memory_space=pl.ANY),
        scratch_shapes=[pltpu.SemaphoreType.DMA],
        compiler_params=_params(),
        name="moe_dispatch",
    )(dest, hp)


def _expert_kernel(start_ref, nblk_ref, xs_ref, wg_ref, wu_ref, wd_ref, ys_ref,
                   xbuf, ybuf, wgb, wub, wdb, sem_in, sem_out):
    e = pl.program_id(0)
    nb = nblk_ref[e]
    base = start_ref[e]

    def rows(j):
        return pl.ds(pl.multiple_of(base + j * MOE_BLOCK, MOE_BLOCK), MOE_BLOCK)

    def in_copy(j, slot):
        return pltpu.make_async_copy(xs_ref.at[rows(j), :], xbuf.at[slot], sem_in.at[slot])

    def out_copy(j, slot):
        return pltpu.make_async_copy(ybuf.at[slot], ys_ref.at[rows(j), :], sem_out.at[slot])

    @pl.when(nb > 0)
    def _():
        in_copy(0, 0).start(priority=BLOCK_DMA_PRIORITY)
        wgb[...] = wg_ref[0].astype(BF16)
        wub[...] = wu_ref[0].astype(BF16)
        wdb[...] = wd_ref[0].astype(BF16)
        half = xbuf.shape[2]

        def body(j, carry):
            slot = lax.rem(j, 2)
            in_copy(j, slot).wait()

            @pl.when(j + 1 < nb)
            def _():
                in_copy(j + 1, 1 - slot).start(priority=BLOCK_DMA_PRIORITY)

            lo, hi = _unpack_halves(xbuf[slot])
            lo, hi = lo.astype(BF16), hi.astype(BF16)
            gate = (jnp.dot(lo, wgb[:half], preferred_element_type=F32)
                    + jnp.dot(hi, wgb[half:], preferred_element_type=F32))
            up = (jnp.dot(lo, wub[:half], preferred_element_type=F32)
                  + jnp.dot(hi, wub[half:], preferred_element_type=F32))
            act = (_silu(gate) * up).astype(BF16)
            y = jnp.dot(act, wdb[...], preferred_element_type=F32)

            @pl.when(j >= 2)
            def _():
                out_copy(j - 2, slot).wait()

            ybuf[slot] = _pack_halves(y)
            out_copy(j, slot).start(priority=BLOCK_DMA_PRIORITY)
            return carry

        lax.fori_loop(0, nb, body, 0)

        @pl.when(nb >= 2)
        def _():
            out_copy(nb - 2, lax.rem(nb, 2)).wait()

        out_copy(nb - 1, lax.rem(nb - 1, 2)).wait()


def _experts(seg_start, seg_blocks, xs, wg, wu, wd):
    n_slots, half = xs.shape
    n_e, d, de = wg.shape
    return pl.pallas_call(
        _expert_kernel,
        out_shape=jax.ShapeDtypeStruct((n_slots, half), U32),
        grid_spec=pltpu.PrefetchScalarGridSpec(
            num_scalar_prefetch=2,
            grid=(n_e,),
            in_specs=[pl.BlockSpec(memory_space=pl.ANY),
                      pl.BlockSpec((1, d, de), lambda e, s, n: (e, 0, 0)),
                      pl.BlockSpec((1, d, de), lambda e, s, n: (e, 0, 0)),
                      pl.BlockSpec((1, de, d), lambda e, s, n: (e, 0, 0))],
            out_specs=pl.BlockSpec(memory_space=pl.ANY),
            scratch_shapes=[pltpu.VMEM((2, MOE_BLOCK, half), U32),
                            pltpu.VMEM((2, MOE_BLOCK, half), U32),
                            pltpu.VMEM((d, de), BF16), pltpu.VMEM((d, de), BF16),
                            pltpu.VMEM((de, d), BF16),
                            pltpu.SemaphoreType.DMA((2,)), pltpu.SemaphoreType.DMA((2,))]),
        compiler_params=_params(),
        name="moe_experts",
    )(seg_start, seg_blocks, xs, wg, wu, wd)


def _combine_kernel(dest_ref, dnext_ref, gt_ref, x_ref, g2_ref, fg_ref, ys_ref, o_ref, buf_ref, sem):
    k, tc = dest_ref.shape
    step = pl.program_id(0)
    slot = lax.rem(step, 2)

    def row_copy(d_ref, s, i, r):
        return pltpu.make_async_copy(ys_ref.at[pl.ds(d_ref[r, i], 1), :],
                                     buf_ref.at[s, r, pl.ds(i, 1), :], sem.at[s])

    def start_all(d_ref, s):
        def body(i, carry):
            for r in range(k):
                row_copy(d_ref, s, i, r).start(priority=r % 2)
            return carry
        lax.fori_loop(0, tc, body, 0)

    @pl.when(step == 0)
    def _():
        start_all(dest_ref, 0)

    @pl.when(step + 1 < pl.num_programs(0))
    def _():
        start_all(dnext_ref, 1 - slot)

    def wait_body(i, carry):
        for r in range(k):
            row_copy(dest_ref, slot, i, r).wait()
        return carry

    lax.fori_loop(0, tc, wait_body, 0)

    gt = gt_ref[...]
    lo, hi = _unpack_halves(buf_ref[slot, 0])
    y_lo, y_hi = lo * gt[:, 0:1], hi * gt[:, 0:1]
    for r in range(1, k):
        lo, hi = _unpack_halves(buf_ref[slot, r])
        y_lo, y_hi = y_lo + lo * gt[:, r:r + 1], y_hi + hi * gt[:, r:r + 1]
    y = jnp.concatenate([y_lo, y_hi], axis=1)
    o_ref[...] = _rms(x_ref[...] + g2_ref[0] * y, fg_ref[...])


def _combine(dest, gates_t, x1s, gate2, final_g, ys, seq, tc):
    k, t = dest.shape
    d = x1s.shape[1]
    n_steps = t // tc
    return pl.pallas_call(
        _combine_kernel,
        out_shape=jax.ShapeDtypeStruct((t, d), F32),
        grid=(n_steps,),
        in_specs=[pl.BlockSpec((k, tc), lambda i: (0, i), memory_space=pltpu.SMEM),
                  pl.BlockSpec((k, tc), lambda i: (0, jnp.minimum(i + 1, n_steps - 1)),
                               memory_space=pltpu.SMEM),
                  pl.BlockSpec((tc, k), lambda i: (i, 0)),
                  pl.BlockSpec((tc, d), lambda i: (i, 0)),
                  pl.BlockSpec((1, 1, d), lambda i: ((i * tc) // seq, 0, 0)),
                  pl.BlockSpec((1, d), lambda i: (0, 0)),
                  pl.BlockSpec(memory_space=pl.ANY)],
        out_specs=pl.BlockSpec((tc, d), lambda i: (i, 0)),
        scratch_shapes=[pltpu.VMEM((2, k, tc, d // 2), U32), pltpu.SemaphoreType.DMA((2,))],
        compiler_params=_params(),
        name="moe_combine",
    )(dest, dest, gates_t, x1s, gate2, final_g.reshape(1, d), ys)


def _layer(x2, c, bsz, seq, lb_row, ada_w, ada_b, norm1_g, w_in, hg_norm_g, w_branch_a, w_branch_b,
           w_out, norm2_g, w_router, router_bias, w_exp_gate, w_exp_up, w_exp_down, w_sh_gate,
           w_sh_up, w_sh_down, final_g):
    t, d = x2.shape
    n_e = w_router.shape[1]
    mod = _ada(c, ada_w, ada_b).reshape(bsz, 6, 1, d)
    shift1, scale1, gate1, shift2, scale2, gate2 = (mod[:, j] for j in range(6))

    hw = hg_norm_g.shape[0]
    aw = len(ATT_GROUPS) * ATT_HEADS_PER_GROUP * ATT_HEAD_DIM
    flat_segs = [(0, hw, BF16), (hw, hw, F32), (2 * hw, hw, BF16), (3 * hw, hw, BF16),
                 (4 * hw + 3 * aw, d, BF16), (4 * hw + 3 * aw + d, d, BF16)]
    (hq, hf, hi, hg, ga, gb), qkv = _inproj(
        x2, norm1_g, scale1, shift1, w_in.astype(BF16), bsz, seq, flat_segs, 4 * hw, tm=512)

    ya = _hgrn(hq, hf, hi, hg, lb_row, hg_norm_g, bsz, seq, ts=256)
    att = [_attn_group(*qkv[3 * g:3 * g + 3], g, nq=4) for g in range(len(ATT_GROUPS))]

    x1s, hp, logits_t = _merge(
        ya, att, ga, gb, x2, gate1, scale2, shift2, gate2, norm2_g, w_branch_a.astype(BF16),
        w_branch_b.astype(BF16), w_out.astype(BF16), w_router.T, w_sh_gate.astype(BF16),
        w_sh_up.astype(BF16), w_sh_down.astype(BF16), seq, tm=256)

    idx, gates, rank, cnt = _topk(logits_t, router_bias, tt=512)
    counts = cnt[:, 0]
    padded = (counts + MOE_BLOCK - 1) // MOE_BLOCK * MOE_BLOCK
    seg_start = (jnp.cumsum(padded) - padded).astype(I32)
    n_blocks = -(-(t * TOP_K) // MOE_BLOCK) + n_e
    dest = _dest(idx, rank, seg_start, tt=512)

    xs = _dispatch(dest, hp, n_blocks * MOE_BLOCK, tt=256)
    ys = _experts(seg_start, (padded // MOE_BLOCK).astype(I32), xs, w_exp_gate, w_exp_up,
                  w_exp_down)
    return _combine(dest, gates.T, x1s, gate2, final_g, ys, seq, tc=128)


def kernel(x, c, ada_w, ada_b, norm1_g, w_in, lb_logits, hg_norm_g, w_branch_a, w_branch_b, w_out,
           norm2_g, w_router, router_bias, w_exp_gate, w_exp_up, w_exp_down, w_sh_gate, w_sh_up,
           w_sh_down, final_g):
    bsz, seq, d = x.shape
    depth = ada_w.shape[0]
    assert depth == 1, "the last layer's kernels also apply the final norm"
    lb_table = jnp.cumsum(jax.nn.softmax(lb_logits.astype(F32), axis=0), axis=0)
    out = _layer(x.reshape(bsz * seq, d), c, bsz, seq, lb_table[0], ada_w[0], ada_b[0], norm1_g[0],
                 w_in[0], hg_norm_g[0], w_branch_a[0], w_branch_b[0], w_out[0], norm2_g[0],
                 w_router[0], router_bias[0], w_exp_gate[0], w_exp_up[0], w_exp_down[0],
                 w_sh_gate[0], w_sh_up[0], w_sh_down[0], final_g)
    return out.reshape(bsz, seq, d)
```

```python
import functools

import jax
import jax.numpy as jnp
from jax import lax
from jax.experimental import pallas as pl
from jax.experimental.pallas import tpu as pltpu

F32 = jnp.float32
BF16 = jnp.bfloat16
I32 = jnp.int32
U32 = jnp.uint32
HIGHEST = lax.Precision.HIGHEST

HG_HEADS = 4
HG_BLOCK = 16
HG_CHUNK = 32
HG_MILD_DECAY = -80.0
ATT_GROUPS = ((128, 1), (512, 4), (2048, 16))
ATT_HEADS_PER_GROUP = 4
ATT_HEAD_DIM = 64
TOP_K = 8
ROUTE_SCALE = 2.5
MOE_BLOCK = 256
RMS_EPS = 1e-6
BLOCK_DMA_PRIORITY = 1
EXPERT_IN_RING = 4
EXPERT_OUT_RING = 3

LANES = 128
VMEM_LIMIT_BYTES = 56 * 1024 * 1024


def _sigmoid(x):
    return 1.0 / (1.0 + jnp.exp(-x))


def _silu(x):
    return x * _sigmoid(x)


def _rms(x, g):
    return x * lax.rsqrt(jnp.mean(x * x, axis=-1, keepdims=True) + RMS_EPS) * g


def _pack_halves(x):
    n = x.shape[1] // 2
    bits = lax.bitcast_convert_type(x.astype(BF16).astype(F32), U32)
    return (bits[:, :n] >> 16) | (bits[:, n:] & jnp.uint32(0xFFFF0000))


def _unpack_halves(word):
    lo = lax.bitcast_convert_type(word << 16, F32)
    hi = lax.bitcast_convert_type(word & jnp.uint32(0xFFFF0000), F32)
    return lo, hi


def _params(n_axes=1):
    return pltpu.CompilerParams(
        dimension_semantics=("arbitrary",) * n_axes, vmem_limit_bytes=VMEM_LIMIT_BYTES)


def _ada_kernel(c_ref, w_ref, b_ref, o_ref):
    sc = _silu(c_ref[...])
    o_ref[...] = jnp.dot(sc, w_ref[...], preferred_element_type=F32, precision=HIGHEST) + b_ref[...]


def _ada(c, w, b):
    bsz, d = c.shape
    n = w.shape[1]
    return pl.pallas_call(
        _ada_kernel,
        out_shape=jax.ShapeDtypeStruct((bsz, n), F32),
        grid=(n // d,),
        in_specs=[pl.BlockSpec((bsz, d), lambda j: (0, 0)),
                  pl.BlockSpec((d, d), lambda j: (0, j)),
                  pl.BlockSpec((1, d), lambda j: (0, j))],
        out_specs=pl.BlockSpec((bsz, d), lambda j: (0, j)),
        compiler_params=_params(),
        name="ada_mod",
    )(c, w, b.reshape(1, n))


def _inproj_kernel(n_flat, flat_ranges, att_c0, x_ref, g_ref, sc_ref, sh_ref, w_ref, *refs):
    flat_refs, att_refs, scr = refs[:n_flat], refs[n_flat:-1], refs[-1]
    tm = x_ref.shape[0]
    h = _rms(x_ref[...], g_ref[...]) * (1.0 + sc_ref[0]) + sh_ref[0]
    hb = h.astype(BF16)
    for (c0, c1), o_ref in zip(flat_ranges, flat_refs):
        o_ref[...] = jnp.dot(hb, w_ref[:, c0:c1], preferred_element_type=F32).astype(o_ref.dtype)
    gw = ATT_HEADS_PER_GROUP * ATT_HEAD_DIM
    n_groups = len(ATT_GROUPS)
    for part in range(3):
        c0 = att_c0 + part * n_groups * gw
        res = jnp.dot(hb, w_ref[:, c0:c0 + n_groups * gw], preferred_element_type=F32)
        if part == 0:
            res = res * (ATT_HEAD_DIM ** -0.5)
        for g, (_, dil) in enumerate(ATT_GROUPS):
            o_ref = att_refs[g * 3 + part]
            sub = res[:, g * gw:(g + 1) * gw]
            if dil == 1:
                o_ref[0, 0] = sub.astype(BF16)
            else:
                for c in range(gw // LANES):
                    scr[c] = sub[:, c * LANES:(c + 1) * LANES]
                for r in range(dil):
                    o_ref[0, r] = jnp.concatenate(
                        [scr[c, pl.ds(r, tm // dil, stride=dil), :] for c in range(gw // LANES)],
                        axis=1).astype(BF16)


def _inproj(x2, g, scale, shift, w_bf16, bsz, seq, flat_segs, att_c0, tm):
    t, d = x2.shape
    gw = ATT_HEADS_PER_GROUP * ATT_HEAD_DIM
    n_per = seq // tm
    per_b = lambda i: (i // n_per, 0, 0)
    att_shapes, att_specs = [], []
    for _, dil in ATT_GROUPS:
        for _ in range(3):
            att_shapes.append(jax.ShapeDtypeStruct((bsz, dil, seq // dil, gw), BF16))
            att_specs.append(pl.BlockSpec((1, dil, tm // dil, gw),
                                          lambda i: (i // n_per, 0, i % n_per, 0)))
    outs = pl.pallas_call(
        functools.partial(_inproj_kernel, len(flat_segs),
                          tuple((c0, c0 + wdt) for c0, wdt, _ in flat_segs), att_c0),
        out_shape=[jax.ShapeDtypeStruct((t, wdt), dt) for _, wdt, dt in flat_segs] + att_shapes,
        grid=(t // tm,),
        in_specs=[pl.BlockSpec((tm, d), lambda i: (i, 0)),
                  pl.BlockSpec((1, d), lambda i: (0, 0)),
                  pl.BlockSpec((1, 1, d), per_b),
                  pl.BlockSpec((1, 1, d), per_b),
                  pl.BlockSpec(w_bf16.shape, lambda i: (0, 0))],
        out_specs=[pl.BlockSpec((tm, wdt), lambda i: (i, 0)) for _, wdt, _ in flat_segs]
        + att_specs,
        scratch_shapes=[pltpu.VMEM((gw // LANES, tm, LANES), F32)],
        compiler_params=_params(),
        name="in_proj",
    )(x2, g.reshape(1, d), scale, shift, w_bf16)
    return outs[:len(flat_segs)], outs[len(flat_segs):]


def _hgrn_kernel(ts, q_ref, f_ref, v_ref, gt_ref, lb_ref, ng_ref, o_ref, st_ref, b_ref):
    dk = q_ref.shape[1] // HG_HEADS
    n_chunks = ts // HG_CHUNK
    n_blk = HG_CHUNK // HG_BLOCK

    @pl.when(pl.program_id(1) == 0)
    def _():
        st_ref[...] = jnp.zeros_like(st_ref)

    row = lax.broadcasted_iota(I32, (LANES, LANES), 0)
    col = lax.broadcasted_iota(I32, (LANES, LANES), 1)
    same_chunk = (row // HG_CHUNK) == (col // HG_CHUNK)
    cum_mat = jnp.where(same_chunk & (col <= row), 1.0, 0.0).astype(BF16)

    def chunk_cumsum(x):
        out = []
        for r0 in range(0, ts, LANES):
            rest = x[r0:r0 + LANES]
            acc = None
            for _ in range(3):
                term = rest.astype(BF16)
                part = jnp.dot(cum_mat, term, preferred_element_type=F32)
                acc = part if acc is None else acc + part
                rest = rest - term.astype(F32)
            out.append(acc)
        return jnp.concatenate(out, axis=0)

    def forget(cs):
        lb = lb_ref[:, cs]
        return lb + (1.0 - lb) * _sigmoid(f_ref[:, cs])

    b_min = None
    for h in range(HG_HEADS):
        cs = slice(h * dk, (h + 1) * dk)
        b = chunk_cumsum(jnp.log(forget(cs)))
        b_ref[:, cs] = b
        m = jnp.min(b)
        b_min = m if b_min is None else jnp.minimum(b_min, m)
    mild = b_min >= HG_MILD_DECAY

    def finish(h, o, st):
        cs = slice(h * dk, (h + 1) * dk)
        st_ref[h] = st
        y = _rms(o, ng_ref[:, cs]) * _silu(gt_ref[:, cs].astype(F32))
        o_ref[:, cs] = y.astype(o_ref.dtype)

    @pl.when(mild)
    def _():
        causal = (lax.broadcasted_iota(I32, (HG_CHUNK, HG_CHUNK), 0)
                  >= lax.broadcasted_iota(I32, (HG_CHUNK, HG_CHUNK), 1))
        for h in range(HG_HEADS):
            cs = slice(h * dk, (h + 1) * dk)
            v = v_ref[:, cs]
            eb = jnp.exp(b_ref[:, cs])
            qe = (q_ref[:, cs].astype(F32) * eb).astype(BF16)
            ke = (1.0 - forget(cs)) / eb
            st = st_ref[h]
            o_rows = []
            for c in range(n_chunks):
                sl = slice(c * HG_CHUNK, (c + 1) * HG_CHUNK)
                a = lax.dot_general(qe[sl], ke[sl].astype(BF16), (((1,), (1,)), ((), ())),
                                    preferred_element_type=F32)
                a = jnp.where(causal, a, 0.0).astype(BF16)
                o_rows.append(
                    jnp.dot(a, v[sl], preferred_element_type=F32)
                    + lax.dot_general(qe[sl], st.astype(BF16), (((1,), (1,)), ((), ())),
                                      preferred_element_type=F32))
                e_end = eb[(c + 1) * HG_CHUNK - 1:(c + 1) * HG_CHUNK]
                kend = (ke[sl] * e_end).astype(BF16)
                vt = v[sl].astype(F32).T.astype(BF16)
                st = st * e_end + jnp.dot(vt, kend, preferred_element_type=F32)
            finish(h, jnp.concatenate(o_rows, axis=0), st)

    @pl.when(jnp.logical_not(mild))
    def _():
        _hgrn_steep(ts, dk, n_chunks, n_blk, q_ref, v_ref, b_ref, st_ref, forget, finish)


def _hgrn_steep(ts, dk, n_chunks, n_blk, q_ref, v_ref, b_ref, st_ref, forget, finish):
    t_in_blk = lax.broadcasted_iota(I32, (ts, dk), 0) % HG_BLOCK

    for h in range(HG_HEADS):
        cs = slice(h * dk, (h + 1) * dk)
        q = q_ref[:, cs].astype(F32)
        v = v_ref[:, cs].astype(F32)
        k = 1.0 - forget(cs)
        b = b_ref[:, cs]

        o = jnp.sum(q * k, axis=-1, keepdims=True) * v
        for d in range(1, HG_BLOCK):
            k_d = pltpu.roll(k, d, axis=0)
            b_d = pltpu.roll(b, d, axis=0)
            v_d = pltpu.roll(v, d, axis=0)
            w = jnp.sum(q * k_d * jnp.exp(jnp.minimum(b - b_d, 0.0)), axis=-1, keepdims=True)
            o = o + jnp.where(t_in_blk >= d, w * v_d, 0.0)

        st = st_ref[h]
        o_rows = []
        for c in range(n_chunks):
            r0 = c * HG_CHUNK
            bc = b[r0:r0 + HG_CHUNK]
            qc = q[r0:r0 + HG_CHUNK]
            kc = k[r0:r0 + HG_CHUNK]
            vc = v[r0:r0 + HG_CHUNK].astype(BF16)
            st_b = st.astype(BF16)
            for i in range(n_blk):
                i0 = i * HG_BLOCK
                if i == 0:
                    qt = qc[:HG_BLOCK] * jnp.exp(bc[:HG_BLOCK])
                    qs = qt
                else:
                    ref_row = bc[i0 - 1:i0]
                    qt = qc[i0:i0 + HG_BLOCK] * jnp.exp(bc[i0:i0 + HG_BLOCK] - ref_row)
                    qs = qt * jnp.exp(ref_row)
                oi = lax.dot_general(qs.astype(BF16), st_b, (((1,), (1,)), ((), ())),
                                     preferred_element_type=F32)
                if i > 0:
                    kh = kc[:i0] * jnp.exp(ref_row - bc[:i0])
                    a = lax.dot_general(qt.astype(BF16), kh.astype(BF16), (((1,), (1,)), ((), ())),
                                        preferred_element_type=F32)
                    oi = oi + jnp.dot(a.astype(BF16), vc[:i0], preferred_element_type=F32)
                o_rows.append(oi)
            b_end = bc[HG_CHUNK - 1:HG_CHUNK]
            kend = kc * jnp.exp(b_end - bc)
            vt = v[r0:r0 + HG_CHUNK].T.astype(BF16)
            st = st * jnp.exp(b_end) + jnp.dot(vt, kend.astype(BF16), preferred_element_type=F32)
        finish(h, o + jnp.concatenate(o_rows, axis=0), st)


def _hgrn(hq, hf, hi, hg, lb, ng, bsz, seq, ts):
    t, w = hq.shape
    dk = w // HG_HEADS
    n_s = seq // ts
    tile = lambda b, s: (b * n_s + s, 0)
    return pl.pallas_call(
        functools.partial(_hgrn_kernel, ts),
        out_shape=jax.ShapeDtypeStruct((t, w), BF16),
        grid=(bsz, n_s),
        in_specs=[pl.BlockSpec((ts, w), tile)] * 4
        + [pl.BlockSpec((1, w), lambda b, s: (0, 0))] * 2,
        out_specs=pl.BlockSpec((ts, w), tile),
        scratch_shapes=[pltpu.VMEM((HG_HEADS, dk, dk), F32), pltpu.VMEM((ts, w), F32)],
        compiler_params=_params(2),
        name="hgrn2",
    )(hq, hf, hi, hg, lb.reshape(1, w), ng.reshape(1, w))


def _attn_kernel(nk, nq, q_ref, kp_ref, kc_ref, vp_ref, vc_ref, o_ref, lse_ref):
    n = pl.program_id(2)
    e = ATT_HEAD_DIM
    i = lax.broadcasted_iota(I32, (nk, 2 * nk), 0)
    j = lax.broadcasted_iota(I32, (nk, 2 * nk), 1)
    band = (j >= i) & (j <= i + nk)
    kk = jnp.concatenate([kp_ref[0, 0], kc_ref[0, 0]], axis=0)
    vv = jnp.concatenate([vp_ref[0, 0], vc_ref[0, 0]], axis=0)
    for b in range(nq):
        valid = band & ((j >= nk) | (n * nq + b > 0))
        q = q_ref[0, 0, b * nk:(b + 1) * nk]
        kb = kk[b * nk:(b + 2) * nk]
        vb = vv[b * nk:(b + 2) * nk]
        for h in range(ATT_HEADS_PER_GROUP):
            cs = slice(h * e, (h + 1) * e)
            s = lax.dot_general(q[:, cs], kb[:, cs], (((1,), (1,)), ((), ())),
                                preferred_element_type=F32)
            s = jnp.where(valid, s, -jnp.inf)
            m = jnp.max(s, axis=-1, keepdims=True)
            p = jnp.exp(s - m)
            l = jnp.sum(p, axis=-1, keepdims=True)
            o = jnp.dot(p.astype(BF16), vb[:, cs], preferred_element_type=F32) / l
            o_ref[0, 0, b * nk:(b + 1) * nk, cs] = o
            lse_ref[0, 0, b * nk:(b + 1) * nk, cs] = jnp.broadcast_to(m + jnp.log(l), (nk, e))


def _attn_group(q, k, v, g, nq):
    window, dil = ATT_GROUPS[g]
    nk = window // dil
    bsz, _, ln, gw = q.shape
    nq = min(nq, ln // nk)
    assert ln % (nk * nq) == 0
    cur = pl.BlockSpec((1, 1, nq * nk, gw), lambda b, r, n: (b, r, n, 0))
    prev = pl.BlockSpec((1, 1, nk, gw), lambda b, r, n: (b, r, jnp.maximum(n * nq - 1, 0), 0))
    return pl.pallas_call(
        functools.partial(_attn_kernel, nk, nq),
        out_shape=[jax.ShapeDtypeStruct(q.shape, F32)] * 2,
        grid=(bsz, dil, ln // (nk * nq)),
        in_specs=[cur, prev, cur, prev, cur],
        out_specs=[cur, cur],
        compiler_params=_params(3),
        name=f"dilated_attn_g{g}",
    )(q, k, k, v, v)


def _token_major(ref, scr):
    dil, rows = ref.shape[1], ref.shape[2]
    if dil == 1:
        return ref[0, 0]
    n_col = scr.shape[0]
    for r in range(dil):
        for c in range(n_col):
            scr[c, pl.ds(r, rows, stride=dil), :] = ref[0, r, :, c * LANES:(c + 1) * LANES]
    return jnp.concatenate([scr[c] for c in range(n_col)], axis=1)


def _merge_kernel(ya_ref, o0_ref, o1_ref, o2_ref, l0_ref, l1_ref, l2_ref, ga_ref, gb_ref, x_ref,
                  g1_ref, sc2_ref, sh2_ref, g2_ref, n2_ref, wa_ref, wb_ref, wo_ref, wr_ref,
                  wsg_ref, wsu_ref, wsd_ref, x1_ref, hp_ref, lg_ref, *scr):
    l0, l1, l2 = (_token_major(r, s) for r, s in zip((l0_ref, l1_ref, l2_ref), scr[:3]))
    o0, o1, o2 = (_token_major(r, s) for r, s in zip((o0_ref, o1_ref, o2_ref), scr[3:]))
    m = jnp.maximum(jnp.maximum(l0, l1), l2)
    e0, e1, e2 = jnp.exp(l0 - m), jnp.exp(l1 - m), jnp.exp(l2 - m)
    yb = (e0 * o0 + e1 * o1 + e2 * o2) / (e0 + e1 + e2)
    merged = (_sigmoid(ga_ref[...].astype(F32))
              * jnp.dot(ya_ref[...], wa_ref[...], preferred_element_type=F32)
              + _sigmoid(gb_ref[...].astype(F32))
              * jnp.dot(yb.astype(BF16), wb_ref[...], preferred_element_type=F32))
    x1 = x_ref[...] + g1_ref[0] * jnp.dot(merged.astype(BF16), wo_ref[...],
                                           preferred_element_type=F32)
    h2 = _rms(x1, n2_ref[...]) * (1.0 + sc2_ref[0]) + sh2_ref[0]
    hb = h2.astype(BF16)
    act = (_silu(jnp.dot(hb, wsg_ref[...], preferred_element_type=F32))
           * jnp.dot(hb, wsu_ref[...], preferred_element_type=F32))
    shared = jnp.dot(act.astype(BF16), wsd_ref[...], preferred_element_type=F32)
    x1_ref[...] = x1 + g2_ref[0] * shared
    hp_ref[...] = _pack_halves(h2)
    lg_ref[...] = lax.dot_general(wr_ref[...], h2, (((1,), (1,)), ((), ())),
                                  preferred_element_type=F32, precision=HIGHEST)


def _merge(ya, att, ga, gb, x2, gate1, scale2, shift2, gate2, norm2_g, wa, wb, wo, wr_t, wsg, wsu,
           wsd, seq, tm):
    t, d = x2.shape
    n_e = wr_t.shape[0]
    n_per = seq // tm
    per_b = lambda i: (i // n_per, 0, 0)
    rows = lambda wdt: pl.BlockSpec((tm, wdt), lambda i: (i, 0))
    full = lambda a: pl.BlockSpec(a.shape, lambda i: (0,) * a.ndim)
    vec = pl.BlockSpec((1, 1, d), per_b)
    (o0, l0), (o1, l1), (o2, l2) = att
    gw = o0.shape[3]
    by_residue = lambda a: pl.BlockSpec((1, a.shape[1], tm // a.shape[1], gw),
                                        lambda i: (i // n_per, 0, i % n_per, 0))
    att_in = (o0, o1, o2, l0, l1, l2)
    return pl.pallas_call(
        _merge_kernel,
        out_shape=[jax.ShapeDtypeStruct((t, d), F32),
                   jax.ShapeDtypeStruct((t, d // 2), U32),
                   jax.ShapeDtypeStruct((n_e, t), F32)],
        grid=(t // tm,),
        in_specs=[rows(ya.shape[1])] + [by_residue(a) for a in att_in] + [rows(d)] * 3
        + [vec, vec, vec, vec, pl.BlockSpec((1, d), lambda i: (0, 0))]
        + [full(a) for a in (wa, wb, wo, wr_t, wsg, wsu, wsd)],
        out_specs=[rows(d), rows(d // 2), pl.BlockSpec((n_e, tm), lambda i: (0, i))],
        scratch_shapes=[pltpu.VMEM((gw // LANES, tm, LANES), F32)] * 6,
        compiler_params=_params(),
        name="merge_router",
    )(ya, *att_in, ga, gb, x2, gate1, scale2, shift2, gate2,
      norm2_g.reshape(1, d), wa, wb, wo, wr_t, wsg, wsu, wsd)


def _topk_kernel(lg_ref, bias_ref, idx_ref, gate_ref, rank_ref, cnt_ref, carry_ref):
    n_e, tt = lg_ref.shape

    @pl.when(pl.program_id(0) == 0)
    def _():
        carry_ref[...] = jnp.zeros_like(carry_ref)

    scores = _sigmoid(lg_ref[...])
    sel = scores + bias_ref[...]
    eio = lax.broadcasted_iota(I32, (n_e, tt), 0)
    picked = jnp.zeros((n_e, tt), F32)
    idxs, vals = [], []
    for _ in range(TOP_K):
        m = jnp.max(sel, axis=0, keepdims=True)
        ik = jnp.min(jnp.where(sel == m, eio, n_e), axis=0, keepdims=True)
        hit = eio == ik
        vals.append(jnp.sum(jnp.where(hit, scores, 0.0), axis=0, keepdims=True))
        sel = jnp.where(hit, -jnp.inf, sel)
        picked = picked + jnp.where(hit, 1.0, 0.0)
        idxs.append(ik)
    denom = vals[0]
    for v in vals[1:]:
        denom = denom + v
    gate_ref[...] = jnp.concatenate([v / denom * ROUTE_SCALE for v in vals], axis=0)
    idx_ref[...] = jnp.concatenate(idxs, axis=0)

    upper = (lax.broadcasted_iota(I32, (tt, tt), 0) <= lax.broadcasted_iota(I32, (tt, tt), 1))
    incl = jnp.dot(picked.astype(BF16), jnp.where(upper, 1.0, 0.0).astype(BF16),
                   preferred_element_type=F32)
    before = incl - picked + carry_ref[...]
    rank_ref[...] = jnp.concatenate(
        [jnp.sum(jnp.where(eio == ik, before, 0.0), axis=0, keepdims=True) for ik in idxs],
        axis=0).astype(I32)
    carry_ref[...] = carry_ref[...] + jnp.sum(picked, axis=1, keepdims=True)
    cnt_ref[...] = jnp.broadcast_to(carry_ref[...], cnt_ref.shape).astype(I32)


def _topk(logits_t, bias, tt):
    n_e, t = logits_t.shape
    tok = pl.BlockSpec((TOP_K, tt), lambda i: (0, i))
    return pl.pallas_call(
        _topk_kernel,
        out_shape=[jax.ShapeDtypeStruct((TOP_K, t), I32), jax.ShapeDtypeStruct((TOP_K, t), F32),
                   jax.ShapeDtypeStruct((TOP_K, t), I32), jax.ShapeDtypeStruct((n_e, 128), I32)],
        grid=(t // tt,),
        in_specs=[pl.BlockSpec((n_e, tt), lambda i: (0, i)),
                  pl.BlockSpec((n_e, 1), lambda i: (0, 0))],
        out_specs=[tok, tok, tok, pl.BlockSpec((n_e, 128), lambda i: (0, 0))],
        scratch_shapes=[pltpu.VMEM((n_e, 1), F32)],
        compiler_params=_params(),
        name="router_topk",
    )(logits_t, bias.reshape(n_e, 1))


def _dest_kernel(idx_ref, rank_ref, start_ref, o_ref):
    k, tt = idx_ref.shape
    n_e = start_ref.shape[0]
    eio = lax.broadcasted_iota(I32, (n_e, tt), 0)
    start = start_ref[...]
    rows = [jnp.sum(jnp.where(eio == idx_ref[r:r + 1, :], start, 0), axis=0, keepdims=True)
            for r in range(k)]
    o_ref[...] = jnp.concatenate(rows, axis=0) + rank_ref[...]


def _dest(idx, rank, seg_start, tt):
    k, t = idx.shape
    n_e = seg_start.shape[0]
    tok = pl.BlockSpec((k, tt), lambda i: (0, i))
    return pl.pallas_call(
        _dest_kernel,
        out_shape=jax.ShapeDtypeStruct((k, t), I32),
        grid=(t // tt,),
        in_specs=[tok, tok, pl.BlockSpec((n_e, 1), lambda i: (0, 0))],
        out_specs=tok,
        compiler_params=_params(),
        name="moe_dest",
    )(idx, rank, seg_start.reshape(n_e, 1))


def _dispatch_kernel(dest_ref, h_ref, xs_ref, sem):
    k, tt = dest_ref.shape

    def row_copy(i, r):
        return pltpu.make_async_copy(h_ref.at[pl.ds(i, 1), :],
                                     xs_ref.at[pl.ds(dest_ref[r, i], 1), :], sem)

    def start(i, carry):
        for r in range(k):
            row_copy(i, r).start(priority=r % 2)
        return carry

    def wait(i, carry):
        for r in range(k):
            row_copy(i, r).wait()
        return carry

    lax.fori_loop(0, tt, start, 0)
    lax.fori_loop(0, tt, wait, 0)


def _dispatch(dest, hp, n_slots, tt):
    k, t = dest.shape
    w = hp.shape[1]
    return pl.pallas_call(
        _dispatch_kernel,
        out_shape=jax.ShapeDtypeStruct((n_slots, w), hp.dtype),
        grid=(t // tt,),
        in_specs=[pl.BlockSpec((k, tt), lambda i: (0, i), memory_space=pltpu.SMEM),
                  pl.BlockSpec((tt, w), lambda i: (i, 0))],
        out_specs=pl.BlockSpec(memory_space=pl.ANY),
        scratch_shapes=[pltpu.SemaphoreType.DMA],
        compiler_params=_params(),
        name="moe_dispatch",
    )(dest, hp)


def _expert_kernel(start_ref, nblk_ref, xs_ref, wg_ref, wu_ref, wd_ref, ys_ref,
                   xbuf, ybuf, wgb, wub, wdb, sem_in, sem_out):
    e = pl.program_id(0)
    n_e = pl.num_programs(0)
    nb = nblk_ref[e]
    g0 = start_ref[e] // MOE_BLOCK
    n_used = start_ref[n_e - 1] // MOE_BLOCK + nblk_ref[n_e - 1]
    n_in, n_out = xbuf.shape[0], ybuf.shape[0]

    def rows(g):
        return pl.ds(pl.multiple_of(g * MOE_BLOCK, MOE_BLOCK), MOE_BLOCK)

    def in_copy(g):
        slot = lax.rem(g, n_in)
        return pltpu.make_async_copy(xs_ref.at[rows(g), :], xbuf.at[slot], sem_in.at[slot])

    def out_copy(g):
        slot = lax.rem(g, n_out)
        return pltpu.make_async_copy(ybuf.at[slot], ys_ref.at[rows(g), :], sem_out.at[slot])

    @pl.when(e == 0)
    def _():
        for g in range(n_in - 1):
            @pl.when(g < n_used)
            def _():
                in_copy(g).start(priority=BLOCK_DMA_PRIORITY)

    @pl.when(nb > 0)
    def _():
        wgb[...] = wg_ref[0].astype(BF16)
        wub[...] = wu_ref[0].astype(BF16)
        wdb[...] = wd_ref[0].astype(BF16)
        half = xbuf.shape[2]

        def body(g, carry):
            in_copy(g).wait()

            @pl.when(g + n_in - 1 < n_used)
            def _():
                in_copy(g + n_in - 1).start(priority=BLOCK_DMA_PRIORITY)

            lo, hi = _unpack_halves(xbuf[lax.rem(g, n_in)])
            lo, hi = lo.astype(BF16), hi.astype(BF16)
            gate = (jnp.dot(lo, wgb[:half], preferred_element_type=F32)
                    + jnp.dot(hi, wgb[half:], preferred_element_type=F32))
            up = (jnp.dot(lo, wub[:half], preferred_element_type=F32)
                  + jnp.dot(hi, wub[half:], preferred_element_type=F32))
            act = (_silu(gate) * up).astype(BF16)
            y = jnp.dot(act, wdb[...], preferred_element_type=F32)

            @pl.when(g >= n_out)
            def _():
                out_copy(g - n_out).wait()

            ybuf[lax.rem(g, n_out)] = _pack_halves(y)
            out_copy(g).start(priority=BLOCK_DMA_PRIORITY)
            return carry

        lax.fori_loop(g0, g0 + nb, body, 0)

    @pl.when(e == n_e - 1)
    def _():
        for i in range(n_out):
            @pl.when(n_used - 1 - i >= 0)
            def _():
                out_copy(n_used - 1 - i).wait()


def _experts(seg_start, seg_blocks, xs, wg, wu, wd):
    n_slots, half = xs.shape
    n_e, d, de = wg.shape
    return pl.pallas_call(
        _expert_kernel,
        out_shape=jax.ShapeDtypeStruct((n_slots, half), U32),
        grid_spec=pltpu.PrefetchScalarGridSpec(
            num_scalar_prefetch=2,
            grid=(n_e,),
            in_specs=[pl.BlockSpec(memory_space=pl.ANY),
                      pl.BlockSpec((1, d, de), lambda e, s, n: (e, 0, 0)),
                      pl.BlockSpec((1, d, de), lambda e, s, n: (e, 0, 0)),
                      pl.BlockSpec((1, de, d), lambda e, s, n: (e, 0, 0))],
            out_specs=pl.BlockSpec(memory_space=pl.ANY),
            scratch_shapes=[pltpu.VMEM((EXPERT_IN_RING, MOE_BLOCK, half), U32),
                            pltpu.VMEM((EXPERT_OUT_RING, MOE_BLOCK, half), U32),
                            pltpu.VMEM((d, de), BF16), pltpu.VMEM((d, de), BF16),
                            pltpu.VMEM((de, d), BF16),
                            pltpu.SemaphoreType.DMA((EXPERT_IN_RING,)),
                            pltpu.SemaphoreType.DMA((EXPERT_OUT_RING,))]),
        compiler_params=_params(),
        name="moe_experts",
    )(seg_start, seg_blocks, xs, wg, wu, wd)


def _combine_kernel(dest_ref, dnext_ref, gt_ref, x_ref, g2_ref, fg_ref, ys_ref, o_ref, buf_ref, sem):
    k, tc = dest_ref.shape
    step = pl.program_id(0)
    slot = lax.rem(step, 2)

    def row_copy(d_ref, s, i, r):
        return pltpu.make_async_copy(ys_ref.at[pl.ds(d_ref[r, i], 1), :],
                                     buf_ref.at[s, r, pl.ds(i, 1), :], sem.at[s])

    def start_all(d_ref, s):
        def body(i, carry):
            for r in range(k):
                row_copy(d_ref, s, i, r).start(priority=r % 2)
            return carry
        lax.fori_loop(0, tc, body, 0)

    @pl.when(step == 0)
    def _():
        start_all(dest_ref, 0)

    @pl.when(step + 1 < pl.num_programs(0))
    def _():
        start_all(dnext_ref, 1 - slot)

    def wait_body(i, carry):
        for r in range(k):
            row_copy(dest_ref, slot, i, r).wait()
        return carry

    lax.fori_loop(0, tc, wait_body, 0)

    gt = gt_ref[...]
    lo, hi = _unpack_halves(buf_ref[slot, 0])
    y_lo, y_hi = lo * gt[:, 0:1], hi * gt[:, 0:1]
    for r in range(1, k):
        lo, hi = _unpack_halves(buf_ref[slot, r])
        y_lo, y_hi = y_lo + lo * gt[:, r:r + 1], y_hi + hi * gt[:, r:r + 1]
    y = jnp.concatenate([y_lo, y_hi], axis=1)
    o_ref[...] = _rms(x_ref[...] + g2_ref[0] * y, fg_ref[...])


def _combine(dest, gates_t, x1s, gate2, final_g, ys, seq, tc):
    k, t = dest.shape
    d = x1s.shape[1]
    n_steps = t // tc
    return pl.pallas_call(
        _combine_kernel,
        out_shape=jax.ShapeDtypeStruct((t, d), F32),
        grid=(n_steps,),
        in_specs=[pl.BlockSpec((k, tc), lambda i: (0, i), memory_space=pltpu.SMEM),
                  pl.BlockSpec((k, tc), lambda i: (0, jnp.minimum(i + 1, n_steps - 1)),
                               memory_space=pltpu.SMEM),
                  pl.BlockSpec((tc, k), lambda i: (i, 0)),
                  pl.BlockSpec((tc, d), lambda i: (i, 0)),
                  pl.BlockSpec((1, 1, d), lambda i: ((i * tc) // seq, 0, 0)),
                  pl.BlockSpec((1, d), lambda i: (0, 0)),
                  pl.BlockSpec(memory_space=pl.ANY)],
        out_specs=pl.BlockSpec((tc, d), lambda i: (i, 0)),
        scratch_shapes=[pltpu.VMEM((2, k, tc, d // 2), U32), pltpu.SemaphoreType.DMA((2,))],
        compiler_params=_params(),
        name="moe_combine",
    )(dest, dest, gates_t, x1s, gate2, final_g.reshape(1, d), ys)


def _layer(x2, c, bsz, seq, lb_row, ada_w, ada_b, norm1_g, w_in, hg_norm_g, w_branch_a, w_branch_b,
           w_out, norm2_g, w_router, router_bias, w_exp_gate, w_exp_up, w_exp_down, w_sh_gate,
           w_sh_up, w_sh_down, final_g):
    t, d = x2.shape
    n_e = w_router.shape[1]
    mod = _ada(c, ada_w, ada_b).reshape(bsz, 6, 1, d)
    shift1, scale1, gate1, shift2, scale2, gate2 = (mod[:, j] for j in range(6))

    hw = hg_norm_g.shape[0]
    aw = len(ATT_GROUPS) * ATT_HEADS_PER_GROUP * ATT_HEAD_DIM
    flat_segs = [(0, hw, BF16), (hw, hw, F32), (2 * hw, hw, BF16), (3 * hw, hw, BF16),
                 (4 * hw + 3 * aw, d, BF16), (4 * hw + 3 * aw + d, d, BF16)]
    (hq, hf, hi, hg, ga, gb), qkv = _inproj(
        x2, norm1_g, scale1, shift1, w_in.astype(BF16), bsz, seq, flat_segs, 4 * hw, tm=512)

    ya = _hgrn(hq, hf, hi, hg, lb_row, hg_norm_g, bsz, seq, ts=256)
    att = [_attn_group(*qkv[3 * g:3 * g + 3], g, nq=4) for g in range(len(ATT_GROUPS))]

    x1s, hp, logits_t = _merge(
        ya, att, ga, gb, x2, gate1, scale2, shift2, gate2, norm2_g, w_branch_a.astype(BF16),
        w_branch_b.astype(BF16), w_out.astype(BF16), w_router.T, w_sh_gate.astype(BF16),
        w_sh_up.astype(BF16), w_sh_down.astype(BF16), seq, tm=256)

    idx, gates, rank, cnt = _topk(logits_t, router_bias, tt=512)
    counts = cnt[:, 0]
    padded = (counts + MOE_BLOCK - 1) // MOE_BLOCK * MOE_BLOCK
    seg_start = (jnp.cumsum(padded) - padded).astype(I32)
    n_blocks = -(-(t * TOP_K) // MOE_BLOCK) + n_e
    dest = _dest(idx, rank, seg_start, tt=512)

    xs = _dispatch(dest, hp, n_blocks * MOE_BLOCK, tt=256)
    ys = _experts(seg_start, (padded // MOE_BLOCK).astype(I32), xs, w_exp_gate, w_exp_up,
                  w_exp_down)
    return _combine(dest, gates.T, x1s, gate2, final_g, ys, seq, tc=128)


def kernel(x, c, ada_w, ada_b, norm1_g, w_in, lb_logits, hg_norm_g, w_branch_a, w_branch_b, w_out,
           norm2_g, w_router, router_bias, w_exp_gate, w_exp_up, w_exp_down, w_sh_gate, w_sh_up,
           w_sh_down, final_g):
    bsz, seq, d = x.shape
    depth = ada_w.shape[0]
    assert depth == 1, "the last layer's kernels also apply the final norm"
    lb_table = jnp.cumsum(jax.nn.softmax(lb_logits.astype(F32), axis=0), axis=0)
    out = _layer(x.reshape(bsz * seq, d), c, bsz, seq, lb_table[0], ada_w[0], ada_b[0], norm1_g[0],
                 w_in[0], hg_norm_g[0], w_branch_a[0], w_branch_b[0], w_out[0], norm2_g[0],
                 w_router[0], router_bias[0], w_exp_gate[0], w_exp_up[0], w_exp_down[0],
                 w_sh_gate[0], w_sh_up[0], w_sh_down[0], final_g)
    return out.reshape(bsz, seq, d)
```

```python
import functools

import jax
import jax.numpy as jnp
from jax import lax
from jax.experimental import pallas as pl
from jax.experimental.pallas import tpu as pltpu
from jax.experimental.pallas import tpu_sc as plsc

F32 = jnp.float32
BF16 = jnp.bfloat16
I32 = jnp.int32
U32 = jnp.uint32
HIGHEST = lax.Precision.HIGHEST

HG_HEADS = 4
HG_BLOCK = 16
HG_CHUNK = 32
HG_MILD_DECAY = -80.0
ATT_GROUPS = ((128, 1), (512, 4), (2048, 16))
ATT_HEADS_PER_GROUP = 4
ATT_HEAD_DIM = 64
TOP_K = 8
ROUTE_SCALE = 2.5
MOE_BLOCK = 256
RMS_EPS = 1e-6
BLOCK_DMA_PRIORITY = 1
SC_WINDOW = 128
EXPERT_IN_RING = 4
EXPERT_OUT_RING = 3

LANES = 128
VMEM_LIMIT_BYTES = 56 * 1024 * 1024


def _sigmoid(x):
    return 1.0 / (1.0 + jnp.exp(-x))


def _silu(x):
    return x * _sigmoid(x)


def _rms(x, g):
    return x * lax.rsqrt(jnp.mean(x * x, axis=-1, keepdims=True) + RMS_EPS) * g


def _pack_halves(x):
    n = x.shape[1] // 2
    bits = lax.bitcast_convert_type(x.astype(BF16).astype(F32), U32)
    return (bits[:, :n] >> 16) | (bits[:, n:] & jnp.uint32(0xFFFF0000))


def _unpack_halves(word):
    lo = lax.bitcast_convert_type(word << 16, F32)
    hi = lax.bitcast_convert_type(word & jnp.uint32(0xFFFF0000), F32)
    return lo, hi


def _params(n_axes=1):
    return pltpu.CompilerParams(
        dimension_semantics=("arbitrary",) * n_axes, vmem_limit_bytes=VMEM_LIMIT_BYTES)


def _ada_kernel(c_ref, w_ref, b_ref, o_ref):
    sc = _silu(c_ref[...])
    o_ref[...] = jnp.dot(sc, w_ref[...], preferred_element_type=F32, precision=HIGHEST) + b_ref[...]


def _ada(c, w, b):
    bsz, d = c.shape
    n = w.shape[1]
    return pl.pallas_call(
        _ada_kernel,
        out_shape=jax.ShapeDtypeStruct((bsz, n), F32),
        grid=(n // d,),
        in_specs=[pl.BlockSpec((bsz, d), lambda j: (0, 0)),
                  pl.BlockSpec((d, d), lambda j: (0, j)),
                  pl.BlockSpec((1, d), lambda j: (0, j))],
        out_specs=pl.BlockSpec((bsz, d), lambda j: (0, j)),
        compiler_params=_params(),
        name="ada_mod",
    )(c, w, b.reshape(1, n))


def _inproj_kernel(n_flat, flat_ranges, att_c0, x_ref, g_ref, sc_ref, sh_ref, w_ref, *refs):
    flat_refs, att_refs, scr = refs[:n_flat], refs[n_flat:-1], refs[-1]
    tm = x_ref.shape[0]
    h = _rms(x_ref[...], g_ref[...]) * (1.0 + sc_ref[0]) + sh_ref[0]
    hb = h.astype(BF16)
    for (c0, c1), o_ref in zip(flat_ranges, flat_refs):
        o_ref[...] = jnp.dot(hb, w_ref[:, c0:c1], preferred_element_type=F32).astype(o_ref.dtype)
    gw = ATT_HEADS_PER_GROUP * ATT_HEAD_DIM
    n_groups = len(ATT_GROUPS)
    for part in range(3):
        c0 = att_c0 + part * n_groups * gw
        res = jnp.dot(hb, w_ref[:, c0:c0 + n_groups * gw], preferred_element_type=F32)
        if part == 0:
            res = res * (ATT_HEAD_DIM ** -0.5)
        for g, (_, dil) in enumerate(ATT_GROUPS):
            o_ref = att_refs[g * 3 + part]
            sub = res[:, g * gw:(g + 1) * gw]
            if dil == 1:
                o_ref[0, 0] = sub.astype(BF16)
            else:
                for c in range(gw // LANES):
                    scr[c] = sub[:, c * LANES:(c + 1) * LANES]
                for r in range(dil):
                    o_ref[0, r] = jnp.concatenate(
                        [scr[c, pl.ds(r, tm // dil, stride=dil), :] for c in range(gw // LANES)],
                        axis=1).astype(BF16)


def _inproj(x2, g, scale, shift, w_bf16, bsz, seq, flat_segs, att_c0, tm):
    t, d = x2.shape
    gw = ATT_HEADS_PER_GROUP * ATT_HEAD_DIM
    n_per = seq // tm
    per_b = lambda i: (i // n_per, 0, 0)
    att_shapes, att_specs = [], []
    for _, dil in ATT_GROUPS:
        for _ in range(3):
            att_shapes.append(jax.ShapeDtypeStruct((bsz, dil, seq // dil, gw), BF16))
            att_specs.append(pl.BlockSpec((1, dil, tm // dil, gw),
                                          lambda i: (i // n_per, 0, i % n_per, 0)))
    outs = pl.pallas_call(
        functools.partial(_inproj_kernel, len(flat_segs),
                          tuple((c0, c0 + wdt) for c0, wdt, _ in flat_segs), att_c0),
        out_shape=[jax.ShapeDtypeStruct((t, wdt), dt) for _, wdt, dt in flat_segs] + att_shapes,
        grid=(t // tm,),
        in_specs=[pl.BlockSpec((tm, d), lambda i: (i, 0)),
                  pl.BlockSpec((1, d), lambda i: (0, 0)),
                  pl.BlockSpec((1, 1, d), per_b),
                  pl.BlockSpec((1, 1, d), per_b),
                  pl.BlockSpec(w_bf16.shape, lambda i: (0, 0))],
        out_specs=[pl.BlockSpec((tm, wdt), lambda i: (i, 0)) for _, wdt, _ in flat_segs]
        + att_specs,
        scratch_shapes=[pltpu.VMEM((gw // LANES, tm, LANES), F32)],
        compiler_params=_params(),
        name="in_proj",
    )(x2, g.reshape(1, d), scale, shift, w_bf16)
    return outs[:len(flat_segs)], outs[len(flat_segs):]


def _hgrn_kernel(ts, q_ref, f_ref, v_ref, gt_ref, lb_ref, ng_ref, o_ref, st_ref, b_ref):
    dk = q_ref.shape[1] // HG_HEADS
    n_chunks = ts // HG_CHUNK
    n_blk = HG_CHUNK // HG_BLOCK

    @pl.when(pl.program_id(1) == 0)
    def _():
        st_ref[...] = jnp.zeros_like(st_ref)

    row = lax.broadcasted_iota(I32, (LANES, LANES), 0)
    col = lax.broadcasted_iota(I32, (LANES, LANES), 1)
    same_chunk = (row // HG_CHUNK) == (col // HG_CHUNK)
    cum_mat = jnp.where(same_chunk & (col <= row), 1.0, 0.0).astype(BF16)

    def chunk_cumsum(x):
        out = []
        for r0 in range(0, ts, LANES):
            rest = x[r0:r0 + LANES]
            acc = None
            for _ in range(3):
                term = rest.astype(BF16)
                part = jnp.dot(cum_mat, term, preferred_element_type=F32)
                acc = part if acc is None else acc + part
                rest = rest - term.astype(F32)
            out.append(acc)
        return jnp.concatenate(out, axis=0)

    def forget(cs):
        lb = lb_ref[:, cs]
        return lb + (1.0 - lb) * _sigmoid(f_ref[:, cs])

    b_min = None
    for h in range(HG_HEADS):
        cs = slice(h * dk, (h + 1) * dk)
        b = chunk_cumsum(jnp.log(forget(cs)))
        b_ref[:, cs] = b
        m = jnp.min(b)
        b_min = m if b_min is None else jnp.minimum(b_min, m)
    mild = b_min >= HG_MILD_DECAY

    def finish(h, o, st):
        cs = slice(h * dk, (h + 1) * dk)
        st_ref[h] = st
        y = _rms(o, ng_ref[:, cs]) * _silu(gt_ref[:, cs].astype(F32))
        o_ref[:, cs] = y.astype(o_ref.dtype)

    @pl.when(mild)
    def _():
        causal = (lax.broadcasted_iota(I32, (HG_CHUNK, HG_CHUNK), 0)
                  >= lax.broadcasted_iota(I32, (HG_CHUNK, HG_CHUNK), 1))
        for h in range(HG_HEADS):
            cs = slice(h * dk, (h + 1) * dk)
            v = v_ref[:, cs]
            eb = jnp.exp(b_ref[:, cs])
            qe = (q_ref[:, cs].astype(F32) * eb).astype(BF16)
            ke = (1.0 - forget(cs)) / eb
            st = st_ref[h]
            o_rows = []
            for c in range(n_chunks):
                sl = slice(c * HG_CHUNK, (c + 1) * HG_CHUNK)
                a = lax.dot_general(qe[sl], ke[sl].astype(BF16), (((1,), (1,)), ((), ())),
                                    preferred_element_type=F32)
                a = jnp.where(causal, a, 0.0).astype(BF16)
                o_rows.append(
                    jnp.dot(a, v[sl], preferred_element_type=F32)
                    + lax.dot_general(qe[sl], st.astype(BF16), (((1,), (1,)), ((), ())),
                                      preferred_element_type=F32))
                e_end = eb[(c + 1) * HG_CHUNK - 1:(c + 1) * HG_CHUNK]
                kend = (ke[sl] * e_end).astype(BF16)
                vt = v[sl].astype(F32).T.astype(BF16)
                st = st * e_end + jnp.dot(vt, kend, preferred_element_type=F32)
            finish(h, jnp.concatenate(o_rows, axis=0), st)

    @pl.when(jnp.logical_not(mild))
    def _():
        _hgrn_steep(ts, dk, n_chunks, n_blk, q_ref, v_ref, b_ref, st_ref, forget, finish)


def _hgrn_steep(ts, dk, n_chunks, n_blk, q_ref, v_ref, b_ref, st_ref, forget, finish):
    t_in_blk = lax.broadcasted_iota(I32, (ts, dk), 0) % HG_BLOCK

    for h in range(HG_HEADS):
        cs = slice(h * dk, (h + 1) * dk)
        q = q_ref[:, cs].astype(F32)
        v = v_ref[:, cs].astype(F32)
        k = 1.0 - forget(cs)
        b = b_ref[:, cs]

        o = jnp.sum(q * k, axis=-1, keepdims=True) * v
        for d in range(1, HG_BLOCK):
            k_d = pltpu.roll(k, d, axis=0)
            b_d = pltpu.roll(b, d, axis=0)
            v_d = pltpu.roll(v, d, axis=0)
            w = jnp.sum(q * k_d * jnp.exp(jnp.minimum(b - b_d, 0.0)), axis=-1, keepdims=True)
            o = o + jnp.where(t_in_blk >= d, w * v_d, 0.0)

        st = st_ref[h]
        o_rows = []
        for c in range(n_chunks):
            r0 = c * HG_CHUNK
            bc = b[r0:r0 + HG_CHUNK]
            qc = q[r0:r0 + HG_CHUNK]
            kc = k[r0:r0 + HG_CHUNK]
            vc = v[r0:r0 + HG_CHUNK].astype(BF16)
            st_b = st.astype(BF16)
            for i in range(n_blk):
                i0 = i * HG_BLOCK
                if i == 0:
                    qt = qc[:HG_BLOCK] * jnp.exp(bc[:HG_BLOCK])
                    qs = qt
                else:
                    ref_row = bc[i0 - 1:i0]
                    qt = qc[i0:i0 + HG_BLOCK] * jnp.exp(bc[i0:i0 + HG_BLOCK] - ref_row)
                    qs = qt * jnp.exp(ref_row)
                oi = lax.dot_general(qs.astype(BF16), st_b, (((1,), (1,)), ((), ())),
                                     preferred_element_type=F32)
                if i > 0:
                    kh = kc[:i0] * jnp.exp(ref_row - bc[:i0])
                    a = lax.dot_general(qt.astype(BF16), kh.astype(BF16), (((1,), (1,)), ((), ())),
                                        preferred_element_type=F32)
                    oi = oi + jnp.dot(a.astype(BF16), vc[:i0], preferred_element_type=F32)
                o_rows.append(oi)
            b_end = bc[HG_CHUNK - 1:HG_CHUNK]
            kend = kc * jnp.exp(b_end - bc)
            vt = v[r0:r0 + HG_CHUNK].T.astype(BF16)
            st = st * jnp.exp(b_end) + jnp.dot(vt, kend.astype(BF16), preferred_element_type=F32)
        finish(h, o + jnp.concatenate(o_rows, axis=0), st)


def _hgrn(hq, hf, hi, hg, lb, ng, bsz, seq, ts):
    t, w = hq.shape
    dk = w // HG_HEADS
    n_s = seq // ts
    tile = lambda b, s: (b * n_s + s, 0)
    return pl.pallas_call(
        functools.partial(_hgrn_kernel, ts),
        out_shape=jax.ShapeDtypeStruct((t, w), BF16),
        grid=(bsz, n_s),
        in_specs=[pl.BlockSpec((ts, w), tile)] * 4
        + [pl.BlockSpec((1, w), lambda b, s: (0, 0))] * 2,
        out_specs=pl.BlockSpec((ts, w), tile),
        scratch_shapes=[pltpu.VMEM((HG_HEADS, dk, dk), F32), pltpu.VMEM((ts, w), F32)],
        compiler_params=_params(2),
        name="hgrn2",
    )(hq, hf, hi, hg, lb.reshape(1, w), ng.reshape(1, w))


def _attn_kernel(nk, nq, q_ref, kp_ref, kc_ref, vp_ref, vc_ref, o_ref, lse_ref):
    n = pl.program_id(2)
    e = ATT_HEAD_DIM
    i = lax.broadcasted_iota(I32, (nk, 2 * nk), 0)
    j = lax.broadcasted_iota(I32, (nk, 2 * nk), 1)
    band = (j >= i) & (j <= i + nk)
    kk = jnp.concatenate([kp_ref[0, 0], kc_ref[0, 0]], axis=0)
    vv = jnp.concatenate([vp_ref[0, 0], vc_ref[0, 0]], axis=0)
    for b in range(nq):
        valid = band & ((j >= nk) | (n * nq + b > 0))
        q = q_ref[0, 0, b * nk:(b + 1) * nk]
        kb = kk[b * nk:(b + 2) * nk]
        vb = vv[b * nk:(b + 2) * nk]
        for h in range(ATT_HEADS_PER_GROUP):
            cs = slice(h * e, (h + 1) * e)
            s = lax.dot_general(q[:, cs], kb[:, cs], (((1,), (1,)), ((), ())),
                                preferred_element_type=F32)
            s = jnp.where(valid, s, -jnp.inf)
            m = jnp.max(s, axis=-1, keepdims=True)
            p = jnp.exp(s - m)
            l = jnp.sum(p, axis=-1, keepdims=True)
            o = jnp.dot(p.astype(BF16), vb[:, cs], preferred_element_type=F32) / l
            o_ref[0, 0, b * nk:(b + 1) * nk, cs] = o
            lse_ref[0, 0, b * nk:(b + 1) * nk, cs] = jnp.broadcast_to(m + jnp.log(l), (nk, e))


def _attn_group(q, k, v, g, nq):
    window, dil = ATT_GROUPS[g]
    nk = window // dil
    bsz, _, ln, gw = q.shape
    nq = min(nq, ln // nk)
    assert ln % (nk * nq) == 0
    cur = pl.BlockSpec((1, 1, nq * nk, gw), lambda b, r, n: (b, r, n, 0))
    prev = pl.BlockSpec((1, 1, nk, gw), lambda b, r, n: (b, r, jnp.maximum(n * nq - 1, 0), 0))
    return pl.pallas_call(
        functools.partial(_attn_kernel, nk, nq),
        out_shape=[jax.ShapeDtypeStruct(q.shape, F32)] * 2,
        grid=(bsz, dil, ln // (nk * nq)),
        in_specs=[cur, prev, cur, prev, cur],
        out_specs=[cur, cur],
        compiler_params=_params(3),
        name=f"dilated_attn_g{g}",
    )(q, k, k, v, v)


def _token_major(ref, scr):
    dil, rows = ref.shape[1], ref.shape[2]
    if dil == 1:
        return ref[0, 0]
    n_col = scr.shape[0]
    for r in range(dil):
        for c in range(n_col):
            scr[c, pl.ds(r, rows, stride=dil), :] = ref[0, r, :, c * LANES:(c + 1) * LANES]
    return jnp.concatenate([scr[c] for c in range(n_col)], axis=1)


def _merge_kernel(ya_ref, o0_ref, o1_ref, o2_ref, l0_ref, l1_ref, l2_ref, ga_ref, gb_ref, x_ref,
                  g1_ref, sc2_ref, sh2_ref, g2_ref, n2_ref, wa_ref, wb_ref, wo_ref, wr_ref,
                  wsg_ref, wsu_ref, wsd_ref, x1_ref, hp_ref, lg_ref, *scr):
    l0, l1, l2 = (_token_major(r, s) for r, s in zip((l0_ref, l1_ref, l2_ref), scr[:3]))
    o0, o1, o2 = (_token_major(r, s) for r, s in zip((o0_ref, o1_ref, o2_ref), scr[3:]))
    m = jnp.maximum(jnp.maximum(l0, l1), l2)
    e0, e1, e2 = jnp.exp(l0 - m), jnp.exp(l1 - m), jnp.exp(l2 - m)
    yb = (e0 * o0 + e1 * o1 + e2 * o2) / (e0 + e1 + e2)
    merged = (_sigmoid(ga_ref[...].astype(F32))
              * jnp.dot(ya_ref[...], wa_ref[...], preferred_element_type=F32)
              + _sigmoid(gb_ref[...].astype(F32))
              * jnp.dot(yb.astype(BF16), wb_ref[...], preferred_element_type=F32))
    x1 = x_ref[...] + g1_ref[0] * jnp.dot(merged.astype(BF16), wo_ref[...],
                                           preferred_element_type=F32)
    h2 = _rms(x1, n2_ref[...]) * (1.0 + sc2_ref[0]) + sh2_ref[0]
    hb = h2.astype(BF16)
    act = (_silu(jnp.dot(hb, wsg_ref[...], preferred_element_type=F32))
           * jnp.dot(hb, wsu_ref[...], preferred_element_type=F32))
    shared = jnp.dot(act.astype(BF16), wsd_ref[...], preferred_element_type=F32)
    x1_ref[...] = x1 + g2_ref[0] * shared
    hp_ref[...] = _pack_halves(h2)
    lg_ref[...] = lax.dot_general(wr_ref[...], h2, (((1,), (1,)), ((), ())),
                                  preferred_element_type=F32, precision=HIGHEST)


def _merge(ya, att, ga, gb, x2, gate1, scale2, shift2, gate2, norm2_g, wa, wb, wo, wr_t, wsg, wsu,
           wsd, seq, tm):
    t, d = x2.shape
    n_e = wr_t.shape[0]
    n_per = seq // tm
    per_b = lambda i: (i // n_per, 0, 0)
    rows = lambda wdt: pl.BlockSpec((tm, wdt), lambda i: (i, 0))
    full = lambda a: pl.BlockSpec(a.shape, lambda i: (0,) * a.ndim)
    vec = pl.BlockSpec((1, 1, d), per_b)
    (o0, l0), (o1, l1), (o2, l2) = att
    gw = o0.shape[3]
    by_residue = lambda a: pl.BlockSpec((1, a.shape[1], tm // a.shape[1], gw),
                                        lambda i: (i // n_per, 0, i % n_per, 0))
    att_in = (o0, o1, o2, l0, l1, l2)
    return pl.pallas_call(
        _merge_kernel,
        out_shape=[jax.ShapeDtypeStruct((t, d), F32),
                   jax.ShapeDtypeStruct((t, d // 2), U32),
                   jax.ShapeDtypeStruct((n_e, t), F32)],
        grid=(t // tm,),
        in_specs=[rows(ya.shape[1])] + [by_residue(a) for a in att_in] + [rows(d)] * 3
        + [vec, vec, vec, vec, pl.BlockSpec((1, d), lambda i: (0, 0))]
        + [full(a) for a in (wa, wb, wo, wr_t, wsg, wsu, wsd)],
        out_specs=[rows(d), rows(d // 2), pl.BlockSpec((n_e, tm), lambda i: (0, i))],
        scratch_shapes=[pltpu.VMEM((gw // LANES, tm, LANES), F32)] * 6,
        compiler_params=_params(),
        name="merge_router",
    )(ya, *att_in, ga, gb, x2, gate1, scale2, shift2, gate2,
      norm2_g.reshape(1, d), wa, wb, wo, wr_t, wsg, wsu, wsd)


def _topk_kernel(lg_ref, bias_ref, idx_ref, gate_ref, rank_ref, cnt_ref, carry_ref):
    n_e, tt = lg_ref.shape

    @pl.when(pl.program_id(0) == 0)
    def _():
        carry_ref[...] = jnp.zeros_like(carry_ref)

    scores = _sigmoid(lg_ref[...])
    sel = scores + bias_ref[...]
    eio = lax.broadcasted_iota(I32, (n_e, tt), 0)
    picked = jnp.zeros((n_e, tt), F32)
    idxs, vals = [], []
    for _ in range(TOP_K):
        m = jnp.max(sel, axis=0, keepdims=True)
        ik = jnp.min(jnp.where(sel == m, eio, n_e), axis=0, keepdims=True)
        hit = eio == ik
        vals.append(jnp.sum(jnp.where(hit, scores, 0.0), axis=0, keepdims=True))
        sel = jnp.where(hit, -jnp.inf, sel)
        picked = picked + jnp.where(hit, 1.0, 0.0)
        idxs.append(ik)
    denom = vals[0]
    for v in vals[1:]:
        denom = denom + v
    gate_ref[...] = jnp.concatenate([v / denom * ROUTE_SCALE for v in vals], axis=0)
    idx_ref[...] = jnp.concatenate(idxs, axis=0)

    upper = (lax.broadcasted_iota(I32, (tt, tt), 0) <= lax.broadcasted_iota(I32, (tt, tt), 1))
    incl = jnp.dot(picked.astype(BF16), jnp.where(upper, 1.0, 0.0).astype(BF16),
                   preferred_element_type=F32)
    before = incl - picked + carry_ref[...]
    rank_ref[...] = jnp.concatenate(
        [jnp.sum(jnp.where(eio == ik, before, 0.0), axis=0, keepdims=True) for ik in idxs],
        axis=0).astype(I32)
    carry_ref[...] = carry_ref[...] + jnp.sum(picked, axis=1, keepdims=True)
    cnt_ref[...] = jnp.broadcast_to(carry_ref[...], cnt_ref.shape).astype(I32)


def _topk(logits_t, bias, tt):
    n_e, t = logits_t.shape
    tok = pl.BlockSpec((TOP_K, tt), lambda i: (0, i))
    return pl.pallas_call(
        _topk_kernel,
        out_shape=[jax.ShapeDtypeStruct((TOP_K, t), I32), jax.ShapeDtypeStruct((TOP_K, t), F32),
                   jax.ShapeDtypeStruct((TOP_K, t), I32), jax.ShapeDtypeStruct((n_e, 128), I32)],
        grid=(t // tt,),
        in_specs=[pl.BlockSpec((n_e, tt), lambda i: (0, i)),
                  pl.BlockSpec((n_e, 1), lambda i: (0, 0))],
        out_specs=[tok, tok, tok, pl.BlockSpec((n_e, 128), lambda i: (0, 0))],
        scratch_shapes=[pltpu.VMEM((n_e, 1), F32)],
        compiler_params=_params(),
        name="router_topk",
    )(logits_t, bias.reshape(n_e, 1))


def _dest_kernel(idx_ref, rank_ref, start_ref, o_ref):
    k, tt = idx_ref.shape
    n_e = start_ref.shape[0]
    eio = lax.broadcasted_iota(I32, (n_e, tt), 0)
    start = start_ref[...]
    rows = [jnp.sum(jnp.where(eio == idx_ref[r:r + 1, :], start, 0), axis=0, keepdims=True)
            for r in range(k)]
    o_ref[...] = jnp.concatenate(rows, axis=0) + rank_ref[...]


def _dest(idx, rank, seg_start, tt):
    k, t = idx.shape
    n_e = seg_start.shape[0]
    tok = pl.BlockSpec((k, tt), lambda i: (0, i))
    return pl.pallas_call(
        _dest_kernel,
        out_shape=jax.ShapeDtypeStruct((k, t), I32),
        grid=(t // tt,),
        in_specs=[tok, tok, pl.BlockSpec((n_e, 1), lambda i: (0, 0))],
        out_specs=tok,
        compiler_params=_params(),
        name="moe_dest",
    )(idx, rank, seg_start.reshape(n_e, 1))


def _sc_mesh():
    return plsc.VectorSubcoreMesh(core_axis_name="core", subcore_axis_name="subcore")


def _sc_scatter_rows(rows, dest, n_out):
    k, t = dest.shape
    w = rows.shape[1]
    mesh = _sc_mesh()
    n_workers = mesh.num_cores * mesh.num_subcores
    win_per_worker = t // (SC_WINDOW * n_workers)
    assert win_per_worker * SC_WINDOW * n_workers == t

    @functools.partial(
        pl.kernel, out_type=jax.ShapeDtypeStruct((n_out, w), rows.dtype), mesh=mesh,
        scratch_types=[pltpu.VMEM((SC_WINDOW, w), rows.dtype)]
        + [pltpu.VMEM((1, SC_WINDOW), I32)] * k + [pltpu.SemaphoreType.DMA],
        name="moe_dispatch_sc")
    def run(rows_hbm, idx_hbm, out_hbm, rows_v, *rest):
        idx_v, sem = rest[:k], rest[k]
        worker = lax.axis_index("subcore") * mesh.num_cores + lax.axis_index("core")

        @pl.loop(0, win_per_worker)
        def _(j):
            t0 = pl.multiple_of((worker * win_per_worker + j) * SC_WINDOW, SC_WINDOW)
            pltpu.sync_copy(rows_hbm.at[pl.ds(t0, SC_WINDOW)], rows_v)
            for r in range(k):
                pltpu.sync_copy(idx_hbm.at[:, pl.ds(r * t + t0, SC_WINDOW)], idx_v[r])
            copies = [pltpu.async_copy(rows_v, out_hbm.at[idx_v[r].at[0]], sem) for r in range(k)]
            for c in copies:
                c.wait()

    return run(rows, dest.reshape(1, k * t))


def _sc_gather_rows(table, dest):
    k, t = dest.shape
    w = table.shape[1]
    mesh = _sc_mesh()
    n_workers = mesh.num_cores * mesh.num_subcores
    win_per_worker = (k * t) // (SC_WINDOW * n_workers)
    assert win_per_worker * SC_WINDOW * n_workers == k * t

    @functools.partial(
        pl.kernel, out_type=jax.ShapeDtypeStruct((k * t, w), table.dtype), mesh=mesh,
        scratch_types=[pltpu.VMEM((SC_WINDOW, w), table.dtype), pltpu.VMEM((1, SC_WINDOW), I32)],
        name="moe_gather_sc")
    def run(table_hbm, idx_hbm, out_hbm, rows_v, idx_v):
        worker = lax.axis_index("subcore") * mesh.num_cores + lax.axis_index("core")

        @pl.loop(0, win_per_worker)
        def _(j):
            p0 = pl.multiple_of((worker * win_per_worker + j) * SC_WINDOW, SC_WINDOW)
            pltpu.sync_copy(idx_hbm.at[:, pl.ds(p0, SC_WINDOW)], idx_v)
            pltpu.sync_copy(table_hbm.at[idx_v.at[0]], rows_v)
            pltpu.sync_copy(rows_v, out_hbm.at[pl.ds(p0, SC_WINDOW)])

    return run(table, dest.reshape(1, k * t))


def _expert_kernel(start_ref, nblk_ref, xs_ref, wg_ref, wu_ref, wd_ref, ys_ref,
                   xbuf, ybuf, wgb, wub, wdb, sem_in, sem_out):
    e = pl.program_id(0)
    n_e = pl.num_programs(0)
    nb = nblk_ref[e]
    g0 = start_ref[e] // MOE_BLOCK
    n_used = start_ref[n_e - 1] // MOE_BLOCK + nblk_ref[n_e - 1]
    n_in, n_out = xbuf.shape[0], ybuf.shape[0]

    def rows(g):
        return pl.ds(pl.multiple_of(g * MOE_BLOCK, MOE_BLOCK), MOE_BLOCK)

    def in_copy(g):
        slot = lax.rem(g, n_in)
        return pltpu.make_async_copy(xs_ref.at[rows(g), :], xbuf.at[slot], sem_in.at[slot])

    def out_copy(g):
        slot = lax.rem(g, n_out)
        return pltpu.make_async_copy(ybuf.at[slot], ys_ref.at[rows(g), :], sem_out.at[slot])

    @pl.when(e == 0)
    def _():
        for g in range(n_in - 1):
            @pl.when(g < n_used)
            def _():
                in_copy(g).start(priority=BLOCK_DMA_PRIORITY)

    @pl.when(nb > 0)
    def _():
        wgb[...] = wg_ref[0].astype(BF16)
        wub[...] = wu_ref[0].astype(BF16)
        wdb[...] = wd_ref[0].astype(BF16)
        half = xbuf.shape[2]

        def body(g, carry):
            in_copy(g).wait()

            @pl.when(g + n_in - 1 < n_used)
            def _():
                in_copy(g + n_in - 1).start(priority=BLOCK_DMA_PRIORITY)

            lo, hi = _unpack_halves(xbuf[lax.rem(g, n_in)])
            lo, hi = lo.astype(BF16), hi.astype(BF16)
            gate = (jnp.dot(lo, wgb[:half], preferred_element_type=F32)
                    + jnp.dot(hi, wgb[half:], preferred_element_type=F32))
            up = (jnp.dot(lo, wub[:half], preferred_element_type=F32)
                  + jnp.dot(hi, wub[half:], preferred_element_type=F32))
            act = (_silu(gate) * up).astype(BF16)
            y = jnp.dot(act, wdb[...], preferred_element_type=F32)

            @pl.when(g >= n_out)
            def _():
                out_copy(g - n_out).wait()

            ybuf[lax.rem(g, n_out)] = _pack_halves(y)
            out_copy(g).start(priority=BLOCK_DMA_PRIORITY)
            return carry

        lax.fori_loop(g0, g0 + nb, body, 0)

    @pl.when(e == n_e - 1)
    def _():
        for i in range(n_out):
            @pl.when(n_used - 1 - i >= 0)
            def _():
                out_copy(n_used - 1 - i).wait()


def _experts(seg_start, seg_blocks, xs, wg, wu, wd):
    n_slots, half = xs.shape
    n_e, d, de = wg.shape
    return pl.pallas_call(
        _expert_kernel,
        out_shape=jax.ShapeDtypeStruct((n_slots, half), U32),
        grid_spec=pltpu.PrefetchScalarGridSpec(
            num_scalar_prefetch=2,
            grid=(n_e,),
            in_specs=[pl.BlockSpec(memory_space=pl.ANY),
                      pl.BlockSpec((1, d, de), lambda e, s, n: (e, 0, 0)),
                      pl.BlockSpec((1, d, de), lambda e, s, n: (e, 0, 0)),
                      pl.BlockSpec((1, de, d), lambda e, s, n: (e, 0, 0))],
            out_specs=pl.BlockSpec(memory_space=pl.ANY),
            scratch_shapes=[pltpu.VMEM((EXPERT_IN_RING, MOE_BLOCK, half), U32),
                            pltpu.VMEM((EXPERT_OUT_RING, MOE_BLOCK, half), U32),
                            pltpu.VMEM((d, de), BF16), pltpu.VMEM((d, de), BF16),
                            pltpu.VMEM((de, d), BF16),
                            pltpu.SemaphoreType.DMA((EXPERT_IN_RING,)),
                            pltpu.SemaphoreType.DMA((EXPERT_OUT_RING,))]),
        compiler_params=_params(),
        name="moe_experts",
    )(seg_start, seg_blocks, xs, wg, wu, wd)


def _combine_kernel(yg_ref, gt_ref, x_ref, g2_ref, fg_ref, o_ref):
    k = yg_ref.shape[0]
    gt = gt_ref[...]
    lo, hi = _unpack_halves(yg_ref[0])
    y_lo, y_hi = lo * gt[:, 0:1], hi * gt[:, 0:1]
    for r in range(1, k):
        lo, hi = _unpack_halves(yg_ref[r])
        y_lo, y_hi = y_lo + lo * gt[:, r:r + 1], y_hi + hi * gt[:, r:r + 1]
    y = jnp.concatenate([y_lo, y_hi], axis=1)
    o_ref[...] = _rms(x_ref[...] + g2_ref[0] * y, fg_ref[...])


def _combine(yg, gates_t, x1s, gate2, final_g, seq, tc):
    k, t, half = yg.shape
    d = x1s.shape[1]
    return pl.pallas_call(
        _combine_kernel,
        out_shape=jax.ShapeDtypeStruct((t, d), F32),
        grid=(t // tc,),
        in_specs=[pl.BlockSpec((k, tc, half), lambda i: (0, i, 0)),
                  pl.BlockSpec((tc, k), lambda i: (i, 0)),
                  pl.BlockSpec((tc, d), lambda i: (i, 0)),
                  pl.BlockSpec((1, 1, d), lambda i: ((i * tc) // seq, 0, 0)),
                  pl.BlockSpec((1, d), lambda i: (0, 0))],
        out_specs=pl.BlockSpec((tc, d), lambda i: (i, 0)),
        compiler_params=_params(),
        name="moe_combine",
    )(yg, gates_t, x1s, gate2, final_g.reshape(1, d))


def _layer(x2, c, bsz, seq, lb_row, ada_w, ada_b, norm1_g, w_in, hg_norm_g, w_branch_a, w_branch_b,
           w_out, norm2_g, w_router, router_bias, w_exp_gate, w_exp_up, w_exp_down, w_sh_gate,
           w_sh_up, w_sh_down, final_g):
    t, d = x2.shape
    n_e = w_router.shape[1]
    mod = _ada(c, ada_w, ada_b).reshape(bsz, 6, 1, d)
    shift1, scale1, gate1, shift2, scale2, gate2 = (mod[:, j] for j in range(6))

    hw = hg_norm_g.shape[0]
    aw = len(ATT_GROUPS) * ATT_HEADS_PER_GROUP * ATT_HEAD_DIM
    flat_segs = [(0, hw, BF16), (hw, hw, F32), (2 * hw, hw, BF16), (3 * hw, hw, BF16),
                 (4 * hw + 3 * aw, d, BF16), (4 * hw + 3 * aw + d, d, BF16)]
    (hq, hf, hi, hg, ga, gb), qkv = _inproj(
        x2, norm1_g, scale1, shift1, w_in.astype(BF16), bsz, seq, flat_segs, 4 * hw, tm=512)

    ya = _hgrn(hq, hf, hi, hg, lb_row, hg_norm_g, bsz, seq, ts=256)
    att = [_attn_group(*qkv[3 * g:3 * g + 3], g, nq=4) for g in range(len(ATT_GROUPS))]

    x1s, hp, logits_t = _merge(
        ya, att, ga, gb, x2, gate1, scale2, shift2, gate2, norm2_g, w_branch_a.astype(BF16),
        w_branch_b.astype(BF16), w_out.astype(BF16), w_router.T, w_sh_gate.astype(BF16),
        w_sh_up.astype(BF16), w_sh_down.astype(BF16), seq, tm=256)

    idx, gates, rank, cnt = _topk(logits_t, router_bias, tt=512)
    counts = cnt[:, 0]
    padded = (counts + MOE_BLOCK - 1) // MOE_BLOCK * MOE_BLOCK
    seg_start = (jnp.cumsum(padded) - padded).astype(I32)
    n_blocks = -(-(t * TOP_K) // MOE_BLOCK) + n_e
    dest = _dest(idx, rank, seg_start, tt=512)

    xs = _sc_scatter_rows(hp, dest, n_blocks * MOE_BLOCK)
    ys = _experts(seg_start, (padded // MOE_BLOCK).astype(I32), xs, w_exp_gate, w_exp_up,
                  w_exp_down)
    yg = _sc_gather_rows(ys, dest).reshape(TOP_K, t, d // 2)
    return _combine(yg, gates.T, x1s, gate2, final_g, seq, tc=256)


def kernel(x, c, ada_w, ada_b, norm1_g, w_in, lb_logits, hg_norm_g, w_branch_a, w_branch_b, w_out,
           norm2_g, w_router, router_bias, w_exp_gate, w_exp_up, w_exp_down, w_sh_gate, w_sh_up,
           w_sh_down, final_g):
    bsz, seq, d = x.shape
    depth = ada_w.shape[0]
    assert depth == 1, "the last layer's kernels also apply the final norm"
    lb_table = jnp.cumsum(jax.nn.softmax(lb_logits.astype(F32), axis=0), axis=0)
    out = _layer(x.reshape(bsz * seq, d), c, bsz, seq, lb_table[0], ada_w[0], ada_b[0], norm1_g[0],
                 w_in[0], hg_norm_g[0], w_branch_a[0], w_branch_b[0], w_out[0], norm2_g[0],
                 w_router[0], router_bias[0], w_exp_gate[0], w_exp_up[0], w_exp_down[0],
                 w_sh_gate[0], w_sh_up[0], w_sh_down[0], final_g)
    return out.reshape(bsz, seq, d)
```

```python
import functools

import jax
import jax.numpy as jnp
from jax import lax
from jax.experimental import pallas as pl
from jax.experimental.pallas import tpu as pltpu
from jax.experimental.pallas import tpu_sc as plsc

F32 = jnp.float32
BF16 = jnp.bfloat16
I32 = jnp.int32
U32 = jnp.uint32
HIGHEST = lax.Precision.HIGHEST

HG_HEADS = 4
HG_BLOCK = 16
HG_CHUNK = 32
HG_MILD_DECAY = -80.0
ATT_GROUPS = ((128, 1), (512, 4), (2048, 16))
ATT_HEADS_PER_GROUP = 4
ATT_HEAD_DIM = 64
TOP_K = 8
ROUTE_SCALE = 2.5
MOE_BLOCK = 256
RMS_EPS = 1e-6
BLOCK_DMA_PRIORITY = 1
SC_WINDOW = 128
EXPERT_GROUP = 2
EXPERT_IN_RING = 6
EXPERT_OUT_RING = 4

LANES = 128
VMEM_LIMIT_BYTES = 56 * 1024 * 1024


def _sigmoid(x):
    return 1.0 / (1.0 + jnp.exp(-x))


def _silu(x):
    return x * _sigmoid(x)


def _rms(x, g):
    return x * lax.rsqrt(jnp.mean(x * x, axis=-1, keepdims=True) + RMS_EPS) * g


def _pack_halves(x):
    n = x.shape[1] // 2
    bits = lax.bitcast_convert_type(x.astype(BF16).astype(F32), U32)
    return (bits[:, :n] >> 16) | (bits[:, n:] & jnp.uint32(0xFFFF0000))


def _unpack_halves(word):
    lo = lax.bitcast_convert_type(word << 16, F32)
    hi = lax.bitcast_convert_type(word & jnp.uint32(0xFFFF0000), F32)
    return lo, hi


def _params(n_axes=1):
    return pltpu.CompilerParams(
        dimension_semantics=("arbitrary",) * n_axes, vmem_limit_bytes=VMEM_LIMIT_BYTES)


def _ada_kernel(c_ref, w_ref, b_ref, o_ref):
    sc = _silu(c_ref[...])
    o_ref[...] = jnp.dot(sc, w_ref[...], preferred_element_type=F32, precision=HIGHEST) + b_ref[...]


def _ada(c, w, b):
    bsz, d = c.shape
    n = w.shape[1]
    return pl.pallas_call(
        _ada_kernel,
        out_shape=jax.ShapeDtypeStruct((bsz, n), F32),
        grid=(n // d,),
        in_specs=[pl.BlockSpec((bsz, d), lambda j: (0, 0)),
                  pl.BlockSpec((d, d), lambda j: (0, j)),
                  pl.BlockSpec((1, d), lambda j: (0, j))],
        out_specs=pl.BlockSpec((bsz, d), lambda j: (0, j)),
        compiler_params=_params(),
        name="ada_mod",
    )(c, w, b.reshape(1, n))


def _inproj_kernel(n_flat, flat_ranges, att_c0, x_ref, g_ref, sc_ref, sh_ref, w_ref, *refs):
    flat_refs, att_refs, scr = refs[:n_flat], refs[n_flat:-1], refs[-1]
    tm = x_ref.shape[0]
    h = _rms(x_ref[...], g_ref[...]) * (1.0 + sc_ref[0]) + sh_ref[0]
    hb = h.astype(BF16)
    for (c0, c1), o_ref in zip(flat_ranges, flat_refs):
        o_ref[...] = jnp.dot(hb, w_ref[:, c0:c1], preferred_element_type=F32).astype(o_ref.dtype)
    gw = ATT_HEADS_PER_GROUP * ATT_HEAD_DIM
    n_groups = len(ATT_GROUPS)
    for part in range(3):
        c0 = att_c0 + part * n_groups * gw
        res = jnp.dot(hb, w_ref[:, c0:c0 + n_groups * gw], preferred_element_type=F32)
        if part == 0:
            res = res * (ATT_HEAD_DIM ** -0.5)
        for g, (_, dil) in enumerate(ATT_GROUPS):
            o_ref = att_refs[g * 3 + part]
            sub = res[:, g * gw:(g + 1) * gw]
            if dil == 1:
                o_ref[0, 0] = sub.astype(BF16)
            else:
                for c in range(gw // LANES):
                    scr[c] = sub[:, c * LANES:(c + 1) * LANES]
                for r in range(dil):
                    o_ref[0, r] = jnp.concatenate(
                        [scr[c, pl.ds(r, tm // dil, stride=dil), :] for c in range(gw // LANES)],
                        axis=1).astype(BF16)


def _inproj(x2, g, scale, shift, w_bf16, bsz, seq, flat_segs, att_c0, tm):
    t, d = x2.shape
    gw = ATT_HEADS_PER_GROUP * ATT_HEAD_DIM
    n_per = seq // tm
    per_b = lambda i: (i // n_per, 0, 0)
    att_shapes, att_specs = [], []
    for _, dil in ATT_GROUPS:
        for _ in range(3):
            att_shapes.append(jax.ShapeDtypeStruct((bsz, dil, seq // dil, gw), BF16))
            att_specs.append(pl.BlockSpec((1, dil, tm // dil, gw),
                                          lambda i: (i // n_per, 0, i % n_per, 0)))
    outs = pl.pallas_call(
        functools.partial(_inproj_kernel, len(flat_segs),
                          tuple((c0, c0 + wdt) for c0, wdt, _ in flat_segs), att_c0),
        out_shape=[jax.ShapeDtypeStruct((t, wdt), dt) for _, wdt, dt in flat_segs] + att_shapes,
        grid=(t // tm,),
        in_specs=[pl.BlockSpec((tm, d), lambda i: (i, 0)),
                  pl.BlockSpec((1, d), lambda i: (0, 0)),
                  pl.BlockSpec((1, 1, d), per_b),
                  pl.BlockSpec((1, 1, d), per_b),
                  pl.BlockSpec(w_bf16.shape, lambda i: (0, 0))],
        out_specs=[pl.BlockSpec((tm, wdt), lambda i: (i, 0)) for _, wdt, _ in flat_segs]
        + att_specs,
        scratch_shapes=[pltpu.VMEM((gw // LANES, tm, LANES), F32)],
        compiler_params=_params(),
        name="in_proj",
    )(x2, g.reshape(1, d), scale, shift, w_bf16)
    return outs[:len(flat_segs)], outs[len(flat_segs):]


def _hgrn_kernel(ts, q_ref, f_ref, v_ref, gt_ref, lb_ref, ng_ref, o_ref, st_ref, b_ref):
    dk = q_ref.shape[1] // HG_HEADS
    n_chunks = ts // HG_CHUNK
    n_blk = HG_CHUNK // HG_BLOCK

    @pl.when(pl.program_id(1) == 0)
    def _():
        st_ref[...] = jnp.zeros_like(st_ref)

    row = lax.broadcasted_iota(I32, (LANES, LANES), 0)
    col = lax.broadcasted_iota(I32, (LANES, LANES), 1)
    same_chunk = (row // HG_CHUNK) == (col // HG_CHUNK)
    cum_mat = jnp.where(same_chunk & (col <= row), 1.0, 0.0).astype(BF16)

    def chunk_cumsum(x):
        out = []
        for r0 in range(0, ts, LANES):
            rest = x[r0:r0 + LANES]
            acc = None
            for _ in range(3):
                term = rest.astype(BF16)
                part = jnp.dot(cum_mat, term, preferred_element_type=F32)
                acc = part if acc is None else acc + part
                rest = rest - term.astype(F32)
            out.append(acc)
        return jnp.concatenate(out, axis=0)

    def forget(cs):
        lb = lb_ref[:, cs]
        return lb + (1.0 - lb) * _sigmoid(f_ref[:, cs])

    b_min = None
    for h in range(HG_HEADS):
        cs = slice(h * dk, (h + 1) * dk)
        b = chunk_cumsum(jnp.log(forget(cs)))
        b_ref[:, cs] = b
        m = jnp.min(b)
        b_min = m if b_min is None else jnp.minimum(b_min, m)
    mild = b_min >= HG_MILD_DECAY

    def finish(h, o, st):
        cs = slice(h * dk, (h + 1) * dk)
        st_ref[h] = st
        y = _rms(o, ng_ref[:, cs]) * _silu(gt_ref[:, cs].astype(F32))
        o_ref[:, cs] = y.astype(o_ref.dtype)

    @pl.when(mild)
    def _():
        span = 2 * HG_CHUNK
        causal = (lax.broadcasted_iota(I32, (span, span), 0)
                  >= lax.broadcasted_iota(I32, (span, span), 1))
        nt = lambda x, y: lax.dot_general(x, y, (((1,), (1,)), ((), ())),
                                          preferred_element_type=F32)
        for h in range(HG_HEADS):
            cs = slice(h * dk, (h + 1) * dk)
            v = v_ref[:, cs]
            b = b_ref[:, cs]
            q = q_ref[:, cs].astype(F32)
            k = 1.0 - forget(cs)
            st = st_ref[h]
            o_rows = []
            for r0 in range(0, ts, span):
                sl = slice(r0, r0 + span)
                b_first, b_second = b[r0:r0 + HG_CHUNK], b[r0 + HG_CHUNK:r0 + span]
                end_first = b_first[HG_CHUNK - 1:HG_CHUNK]
                end_second = b_second[HG_CHUNK - 1:HG_CHUNK]
                e = jnp.exp(jnp.concatenate([b_first - end_first, b_second], axis=0))
                qe = (q[sl] * e).astype(BF16)
                ke = k[sl] / e
                a = jnp.where(causal, nt(qe, ke.astype(BF16)), 0.0).astype(BF16)
                st_in = (st * jnp.exp(end_first)).astype(BF16)
                o_rows.append(jnp.dot(a, v[sl], preferred_element_type=F32) + nt(qe, st_in))
                kend = (ke * jnp.exp(end_second)).astype(BF16)
                vt = v[sl].astype(F32).T.astype(BF16)
                st = (st * jnp.exp(end_first + end_second)
                      + jnp.dot(vt, kend, preferred_element_type=F32))
            finish(h, jnp.concatenate(o_rows, axis=0), st)

    @pl.when(jnp.logical_not(mild))
    def _():
        _hgrn_steep(ts, dk, n_chunks, n_blk, q_ref, v_ref, b_ref, st_ref, forget, finish)


def _hgrn_steep(ts, dk, n_chunks, n_blk, q_ref, v_ref, b_ref, st_ref, forget, finish):
    t_in_blk = lax.broadcasted_iota(I32, (ts, dk), 0) % HG_BLOCK

    for h in range(HG_HEADS):
        cs = slice(h * dk, (h + 1) * dk)
        q = q_ref[:, cs].astype(F32)
        v = v_ref[:, cs].astype(F32)
        k = 1.0 - forget(cs)
        b = b_ref[:, cs]

        o = jnp.sum(q * k, axis=-1, keepdims=True) * v
        for d in range(1, HG_BLOCK):
            k_d = pltpu.roll(k, d, axis=0)
            b_d = pltpu.roll(b, d, axis=0)
            v_d = pltpu.roll(v, d, axis=0)
            w = jnp.sum(q * k_d * jnp.exp(jnp.minimum(b - b_d, 0.0)), axis=-1, keepdims=True)
            o = o + jnp.where(t_in_blk >= d, w * v_d, 0.0)

        st = st_ref[h]
        o_rows = []
        for c in range(n_chunks):
            r0 = c * HG_CHUNK
            bc = b[r0:r0 + HG_CHUNK]
            qc = q[r0:r0 + HG_CHUNK]
            kc = k[r0:r0 + HG_CHUNK]
            vc = v[r0:r0 + HG_CHUNK].astype(BF16)
            st_b = st.astype(BF16)
            for i in range(n_blk):
                i0 = i * HG_BLOCK
                if i == 0:
                    qt = qc[:HG_BLOCK] * jnp.exp(bc[:HG_BLOCK])
                    qs = qt
                else:
                    ref_row = bc[i0 - 1:i0]
                    qt = qc[i0:i0 + HG_BLOCK] * jnp.exp(bc[i0:i0 + HG_BLOCK] - ref_row)
                    qs = qt * jnp.exp(ref_row)
                oi = lax.dot_general(qs.astype(BF16), st_b, (((1,), (1,)), ((), ())),
                                     preferred_element_type=F32)
                if i > 0:
                    kh = kc[:i0] * jnp.exp(ref_row - bc[:i0])
                    a = lax.dot_general(qt.astype(BF16), kh.astype(BF16), (((1,), (1,)), ((), ())),
                                        preferred_element_type=F32)
                    oi = oi + jnp.dot(a.astype(BF16), vc[:i0], preferred_element_type=F32)
                o_rows.append(oi)
            b_end = bc[HG_CHUNK - 1:HG_CHUNK]
            kend = kc * jnp.exp(b_end - bc)
            vt = v[r0:r0 + HG_CHUNK].T.astype(BF16)
            st = st * jnp.exp(b_end) + jnp.dot(vt, kend.astype(BF16), preferred_element_type=F32)
        finish(h, o + jnp.concatenate(o_rows, axis=0), st)


def _hgrn(hq, hf, hi, hg, lb, ng, bsz, seq, ts):
    t, w = hq.shape
    dk = w // HG_HEADS
    n_s = seq // ts
    tile = lambda b, s: (b * n_s + s, 0)
    return pl.pallas_call(
        functools.partial(_hgrn_kernel, ts),
        out_shape=jax.ShapeDtypeStruct((t, w), BF16),
        grid=(bsz, n_s),
        in_specs=[pl.BlockSpec((ts, w), tile)] * 4
        + [pl.BlockSpec((1, w), lambda b, s: (0, 0))] * 2,
        out_specs=pl.BlockSpec((ts, w), tile),
        scratch_shapes=[pltpu.VMEM((HG_HEADS, dk, dk), F32), pltpu.VMEM((ts, w), F32)],
        compiler_params=_params(2),
        name="hgrn2",
    )(hq, hf, hi, hg, lb.reshape(1, w), ng.reshape(1, w))


def _attn_kernel(nk, nq, q_ref, kp_ref, kc_ref, vp_ref, vc_ref, o_ref, lse_ref):
    n = pl.program_id(2)
    e = ATT_HEAD_DIM
    i = lax.broadcasted_iota(I32, (nk, 2 * nk), 0)
    j = lax.broadcasted_iota(I32, (nk, 2 * nk), 1)
    band = (j >= i) & (j <= i + nk)
    kk = jnp.concatenate([kp_ref[0, 0], kc_ref[0, 0]], axis=0)
    vv = jnp.concatenate([vp_ref[0, 0], vc_ref[0, 0]], axis=0)
    for b in range(nq):
        valid = band & ((j >= nk) | (n * nq + b > 0))
        q = q_ref[0, 0, b * nk:(b + 1) * nk]
        kb = kk[b * nk:(b + 2) * nk]
        vb = vv[b * nk:(b + 2) * nk]
        for h in range(ATT_HEADS_PER_GROUP):
            cs = slice(h * e, (h + 1) * e)
            s = lax.dot_general(q[:, cs], kb[:, cs], (((1,), (1,)), ((), ())),
                                preferred_element_type=F32)
            s = jnp.where(valid, s, -jnp.inf)
            m = jnp.max(s, axis=-1, keepdims=True)
            p = jnp.exp(s - m)
            l = jnp.sum(p, axis=-1, keepdims=True)
            o = jnp.dot(p.astype(BF16), vb[:, cs], preferred_element_type=F32) / l
            o_ref[0, 0, b * nk:(b + 1) * nk, cs] = o
            lse_ref[0, 0, b * nk:(b + 1) * nk, cs] = jnp.broadcast_to(m + jnp.log(l), (nk, e))


def _attn_group(q, k, v, g, nq):
    window, dil = ATT_GROUPS[g]
    nk = window // dil
    bsz, _, ln, gw = q.shape
    nq = min(nq, ln // nk)
    assert ln % (nk * nq) == 0
    cur = pl.BlockSpec((1, 1, nq * nk, gw), lambda b, r, n: (b, r, n, 0))
    prev = pl.BlockSpec((1, 1, nk, gw), lambda b, r, n: (b, r, jnp.maximum(n * nq - 1, 0), 0))
    return pl.pallas_call(
        functools.partial(_attn_kernel, nk, nq),
        out_shape=[jax.ShapeDtypeStruct(q.shape, F32)] * 2,
        grid=(bsz, dil, ln // (nk * nq)),
        in_specs=[cur, prev, cur, prev, cur],
        out_specs=[cur, cur],
        compiler_params=_params(3),
        name=f"dilated_attn_g{g}",
    )(q, k, k, v, v)


def _token_major(ref, scr):
    dil, rows = ref.shape[1], ref.shape[2]
    if dil == 1:
        return ref[0, 0]
    n_col = scr.shape[0]
    for r in range(dil):
        for c in range(n_col):
            scr[c, pl.ds(r, rows, stride=dil), :] = ref[0, r, :, c * LANES:(c + 1) * LANES]
    return jnp.concatenate([scr[c] for c in range(n_col)], axis=1)


def _merge_kernel(ya_ref, o0_ref, o1_ref, o2_ref, l0_ref, l1_ref, l2_ref, ga_ref, gb_ref, x_ref,
                  g1_ref, sc2_ref, sh2_ref, g2_ref, n2_ref, wa_ref, wb_ref, wo_ref, wr_ref, wrl_ref,
                  wsg_ref, wsu_ref, wsd_ref, x1_ref, hp_ref, lg_ref, *scr):
    l0, l1, l2 = (_token_major(r, s) for r, s in zip((l0_ref, l1_ref, l2_ref), scr[:3]))
    o0, o1, o2 = (_token_major(r, s) for r, s in zip((o0_ref, o1_ref, o2_ref), scr[3:]))
    m = jnp.maximum(jnp.maximum(l0, l1), l2)
    e0, e1, e2 = jnp.exp(l0 - m), jnp.exp(l1 - m), jnp.exp(l2 - m)
    yb = (e0 * o0 + e1 * o1 + e2 * o2) / (e0 + e1 + e2)
    merged = (_sigmoid(ga_ref[...].astype(F32))
              * jnp.dot(ya_ref[...], wa_ref[...], preferred_element_type=F32)
              + _sigmoid(gb_ref[...].astype(F32))
              * jnp.dot(yb.astype(BF16), wb_ref[...], preferred_element_type=F32))
    x1 = x_ref[...] + g1_ref[0] * jnp.dot(merged.astype(BF16), wo_ref[...],
                                           preferred_element_type=F32)
    h2 = _rms(x1, n2_ref[...]) * (1.0 + sc2_ref[0]) + sh2_ref[0]
    hb = h2.astype(BF16)
    act = (_silu(jnp.dot(hb, wsg_ref[...], preferred_element_type=F32))
           * jnp.dot(hb, wsu_ref[...], preferred_element_type=F32))
    shared = jnp.dot(act.astype(BF16), wsd_ref[...], preferred_element_type=F32)
    x1_ref[...] = x1 + g2_ref[0] * shared
    hp_ref[...] = _pack_halves(h2)
    h_lo = (h2 - hb.astype(F32)).astype(BF16)
    nt = lambda a, b: lax.dot_general(a, b, (((1,), (1,)), ((), ())), preferred_element_type=F32)
    lg_ref[...] = nt(wr_ref[...], hb) + (nt(wr_ref[...], h_lo) + nt(wrl_ref[...], hb))


def _merge(ya, att, ga, gb, x2, gate1, scale2, shift2, gate2, norm2_g, wa, wb, wo, wr_t, wsg, wsu,
           wsd, seq, tm):
    t, d = x2.shape
    n_e = wr_t.shape[0]
    wr_hi = wr_t.astype(BF16)
    wr_lo = (wr_t - wr_hi.astype(F32)).astype(BF16)
    n_per = seq // tm
    per_b = lambda i: (i // n_per, 0, 0)
    rows = lambda wdt: pl.BlockSpec((tm, wdt), lambda i: (i, 0))
    full = lambda a: pl.BlockSpec(a.shape, lambda i: (0,) * a.ndim)
    vec = pl.BlockSpec((1, 1, d), per_b)
    (o0, l0), (o1, l1), (o2, l2) = att
    gw = o0.shape[3]
    by_residue = lambda a: pl.BlockSpec((1, a.shape[1], tm // a.shape[1], gw),
                                        lambda i: (i // n_per, 0, i % n_per, 0))
    att_in = (o0, o1, o2, l0, l1, l2)
    return pl.pallas_call(
        _merge_kernel,
        out_shape=[jax.ShapeDtypeStruct((t, d), F32),
                   jax.ShapeDtypeStruct((t, d // 2), U32),
                   jax.ShapeDtypeStruct((n_e, t), F32)],
        grid=(t // tm,),
        in_specs=[rows(ya.shape[1])] + [by_residue(a) for a in att_in] + [rows(d)] * 3
        + [vec, vec, vec, vec, pl.BlockSpec((1, d), lambda i: (0, 0))]
        + [full(a) for a in (wa, wb, wo, wr_hi, wr_lo, wsg, wsu, wsd)],
        out_specs=[rows(d), rows(d // 2), pl.BlockSpec((n_e, tm), lambda i: (0, i))],
        scratch_shapes=[pltpu.VMEM((gw // LANES, tm, LANES), F32)] * 6,
        compiler_params=_params(),
        name="merge_router",
    )(ya, *att_in, ga, gb, x2, gate1, scale2, shift2, gate2,
      norm2_g.reshape(1, d), wa, wb, wo, wr_hi, wr_lo, wsg, wsu, wsd)


def _topk_kernel(lg_ref, bias_ref, idx_ref, gate_ref, rank_ref, cnt_ref, carry_ref):
    n_e, tt = lg_ref.shape

    @pl.when(pl.program_id(0) == 0)
    def _():
        carry_ref[...] = jnp.zeros_like(carry_ref)

    scores = _sigmoid(lg_ref[...])
    sel = scores + bias_ref[...]
    eio = lax.broadcasted_iota(I32, (n_e, tt), 0)
    picked = jnp.zeros((n_e, tt), F32)
    idxs, vals = [], []
    for _ in range(TOP_K):
        m = jnp.max(sel, axis=0, keepdims=True)
        ik = jnp.min(jnp.where(sel == m, eio, n_e), axis=0, keepdims=True)
        hit = eio == ik
        vals.append(jnp.sum(jnp.where(hit, scores, 0.0), axis=0, keepdims=True))
        sel = jnp.where(hit, -jnp.inf, sel)
        picked = picked + jnp.where(hit, 1.0, 0.0)
        idxs.append(ik)
    denom = vals[0]
    for v in vals[1:]:
        denom = denom + v
    gate_ref[...] = jnp.concatenate([v / denom * ROUTE_SCALE for v in vals], axis=0)
    idx_ref[...] = jnp.concatenate(idxs, axis=0)

    upper = (lax.broadcasted_iota(I32, (tt, tt), 0) <= lax.broadcasted_iota(I32, (tt, tt), 1))
    incl = jnp.dot(picked.astype(BF16), jnp.where(upper, 1.0, 0.0).astype(BF16),
                   preferred_element_type=F32)
    before = incl - picked + carry_ref[...]
    rank_ref[...] = jnp.concatenate(
        [jnp.sum(jnp.where(eio == ik, before, 0.0), axis=0, keepdims=True) for ik in idxs],
        axis=0).astype(I32)
    carry_ref[...] = carry_ref[...] + jnp.sum(picked, axis=1, keepdims=True)
    cnt_ref[...] = jnp.broadcast_to(carry_ref[...], cnt_ref.shape).astype(I32)


def _topk(logits_t, bias, tt):
    n_e, t = logits_t.shape
    tok = pl.BlockSpec((TOP_K, tt), lambda i: (0, i))
    return pl.pallas_call(
        _topk_kernel,
        out_shape=[jax.ShapeDtypeStruct((TOP_K, t), I32), jax.ShapeDtypeStruct((TOP_K, t), F32),
                   jax.ShapeDtypeStruct((TOP_K, t), I32), jax.ShapeDtypeStruct((n_e, 128), I32)],
        grid=(t // tt,),
        in_specs=[pl.BlockSpec((n_e, tt), lambda i: (0, i)),
                  pl.BlockSpec((n_e, 1), lambda i: (0, 0))],
        out_specs=[tok, tok, tok, pl.BlockSpec((n_e, 128), lambda i: (0, 0))],
        scratch_shapes=[pltpu.VMEM((n_e, 1), F32)],
        compiler_params=_params(),
        name="router_topk",
    )(logits_t, bias.reshape(n_e, 1))


def _dest_kernel(idx_ref, rank_ref, start_ref, o_ref):
    k, tt = idx_ref.shape
    n_e = start_ref.shape[0]
    eio = lax.broadcasted_iota(I32, (n_e, tt), 0)
    start = start_ref[...]
    rows = [jnp.sum(jnp.where(eio == idx_ref[r:r + 1, :], start, 0), axis=0, keepdims=True)
            for r in range(k)]
    o_ref[...] = jnp.concatenate(rows, axis=0) + rank_ref[...]


def _dest(idx, rank, seg_start, tt):
    k, t = idx.shape
    n_e = seg_start.shape[0]
    tok = pl.BlockSpec((k, tt), lambda i: (0, i))
    return pl.pallas_call(
        _dest_kernel,
        out_shape=jax.ShapeDtypeStruct((k, t), I32),
        grid=(t // tt,),
        in_specs=[tok, tok, pl.BlockSpec((n_e, 1), lambda i: (0, 0))],
        out_specs=tok,
        compiler_params=_params(),
        name="moe_dest",
    )(idx, rank, seg_start.reshape(n_e, 1))


def _sc_mesh():
    return plsc.VectorSubcoreMesh(core_axis_name="core", subcore_axis_name="subcore")


def _sc_scatter_rows(rows, dest, n_out):
    k, t = dest.shape
    w = rows.shape[1]
    mesh = _sc_mesh()
    n_workers = mesh.num_cores * mesh.num_subcores
    win_per_worker = t // (SC_WINDOW * n_workers)
    assert win_per_worker * SC_WINDOW * n_workers == t

    @functools.partial(
        pl.kernel, out_type=jax.ShapeDtypeStruct((n_out, w), rows.dtype), mesh=mesh,
        scratch_types=[pltpu.VMEM((SC_WINDOW, w), rows.dtype)]
        + [pltpu.VMEM((1, SC_WINDOW), I32)] * k + [pltpu.SemaphoreType.DMA],
        name="moe_dispatch_sc")
    def run(rows_hbm, idx_hbm, out_hbm, rows_v, *rest):
        idx_v, sem = rest[:k], rest[k]
        worker = lax.axis_index("subcore") * mesh.num_cores + lax.axis_index("core")

        @pl.loop(0, win_per_worker)
        def _(j):
            t0 = pl.multiple_of((worker * win_per_worker + j) * SC_WINDOW, SC_WINDOW)
            pltpu.sync_copy(rows_hbm.at[pl.ds(t0, SC_WINDOW)], rows_v)
            for r in range(k):
                pltpu.sync_copy(idx_hbm.at[:, pl.ds(r * t + t0, SC_WINDOW)], idx_v[r])
            copies = [pltpu.async_copy(rows_v, out_hbm.at[idx_v[r].at[0]], sem) for r in range(k)]
            for c in copies:
                c.wait()

    return run(rows, dest.reshape(1, k * t))


def _sc_gather_rows(table, dest):
    k, t = dest.shape
    w = table.shape[1]
    mesh = _sc_mesh()
    n_workers = mesh.num_cores * mesh.num_subcores
    win_per_worker = (k * t) // (SC_WINDOW * n_workers)
    assert win_per_worker * SC_WINDOW * n_workers == k * t

    @functools.partial(
        pl.kernel, out_type=jax.ShapeDtypeStruct((k * t, w), table.dtype), mesh=mesh,
        scratch_types=[pltpu.VMEM((SC_WINDOW, w), table.dtype), pltpu.VMEM((1, SC_WINDOW), I32)],
        name="moe_gather_sc")
    def run(table_hbm, idx_hbm, out_hbm, rows_v, idx_v):
        worker = lax.axis_index("subcore") * mesh.num_cores + lax.axis_index("core")

        @pl.loop(0, win_per_worker)
        def _(j):
            p0 = pl.multiple_of((worker * win_per_worker + j) * SC_WINDOW, SC_WINDOW)
            pltpu.sync_copy(idx_hbm.at[:, pl.ds(p0, SC_WINDOW)], idx_v)
            pltpu.sync_copy(table_hbm.at[idx_v.at[0]], rows_v)
            pltpu.sync_copy(rows_v, out_hbm.at[pl.ds(p0, SC_WINDOW)])

    return run(table, dest.reshape(1, k * t))


def _expert_kernel(start_ref, nblk_ref, xs_ref, wg_ref, wu_ref, wd_ref, ys_ref,
                   xbuf, ybuf, wgb, wub, wdb, sem_in, sem_out):
    e = pl.program_id(0)
    n_e = pl.num_programs(0)
    nb = nblk_ref[e]
    g0 = start_ref[e] // MOE_BLOCK
    n_used = start_ref[n_e - 1] // MOE_BLOCK + nblk_ref[n_e - 1]
    n_in, n_out = xbuf.shape[0], ybuf.shape[0]

    def rows(g):
        return pl.ds(pl.multiple_of(g * MOE_BLOCK, MOE_BLOCK), MOE_BLOCK)

    def in_copy(g):
        slot = lax.rem(g, n_in)
        return pltpu.make_async_copy(xs_ref.at[rows(g), :], xbuf.at[slot], sem_in.at[slot])

    def out_copy(g):
        slot = lax.rem(g, n_out)
        return pltpu.make_async_copy(ybuf.at[slot], ys_ref.at[rows(g), :], sem_out.at[slot])

    look = n_in - EXPERT_GROUP

    @pl.when(e == 0)
    def _():
        for g in range(look):
            @pl.when(g < n_used)
            def _():
                in_copy(g).start(priority=BLOCK_DMA_PRIORITY)

    @pl.when(nb > 0)
    def _():
        wgb[...] = wg_ref[0].astype(BF16)
        wub[...] = wu_ref[0].astype(BF16)
        wdb[...] = wd_ref[0].astype(BF16)
        half = xbuf.shape[2]

        def swiglu(word):
            lo, hi = _unpack_halves(word)
            lo, hi = lo.astype(BF16), hi.astype(BF16)
            gate = (jnp.dot(lo, wgb[:half], preferred_element_type=F32)
                    + jnp.dot(hi, wgb[half:], preferred_element_type=F32))
            up = (jnp.dot(lo, wub[:half], preferred_element_type=F32)
                  + jnp.dot(hi, wub[half:], preferred_element_type=F32))
            act = (_silu(gate) * up).astype(BF16)
            return jnp.dot(act, wdb[...], preferred_element_type=F32)

        def process(g, m):
            for i in range(m):
                in_copy(g + i).wait()
            for i in range(m):
                @pl.when(g + look + i < n_used)
                def _():
                    in_copy(g + look + i).start(priority=BLOCK_DMA_PRIORITY)
            ys = [swiglu(xbuf[lax.rem(g + i, n_in)]) for i in range(m)]
            for i in range(m):
                @pl.when(g + i >= n_out)
                def _():
                    out_copy(g + i - n_out).wait()

                ybuf[lax.rem(g + i, n_out)] = _pack_halves(ys[i])
                out_copy(g + i).start(priority=BLOCK_DMA_PRIORITY)

        def group_body(p, carry):
            process(g0 + p * EXPERT_GROUP, EXPERT_GROUP)
            return carry

        lax.fori_loop(0, nb // EXPERT_GROUP, group_body, 0)
        for m in range(1, EXPERT_GROUP):
            @pl.when(lax.rem(nb, EXPERT_GROUP) == m)
            def _():
                process(g0 + nb - m, m)

    @pl.when(e == n_e - 1)
    def _():
        for i in range(n_out):
            @pl.when(n_used - 1 - i >= 0)
            def _():
                out_copy(n_used - 1 - i).wait()


def _experts(seg_start, seg_blocks, xs, wg, wu, wd):
    n_slots, half = xs.shape
    n_e, d, de = wg.shape
    return pl.pallas_call(
        _expert_kernel,
        out_shape=jax.ShapeDtypeStruct((n_slots, half), U32),
        grid_spec=pltpu.PrefetchScalarGridSpec(
            num_scalar_prefetch=2,
            grid=(n_e,),
            in_specs=[pl.BlockSpec(memory_space=pl.ANY),
                      pl.BlockSpec((1, d, de), lambda e, s, n: (e, 0, 0)),
                      pl.BlockSpec((1, d, de), lambda e, s, n: (e, 0, 0)),
                      pl.BlockSpec((1, de, d), lambda e, s, n: (e, 0, 0))],
            out_specs=pl.BlockSpec(memory_space=pl.ANY),
            scratch_shapes=[pltpu.VMEM((EXPERT_IN_RING, MOE_BLOCK, half), U32),
                            pltpu.VMEM((EXPERT_OUT_RING, MOE_BLOCK, half), U32),
                            pltpu.VMEM((d, de), BF16), pltpu.VMEM((d, de), BF16),
                            pltpu.VMEM((de, d), BF16),
                            pltpu.SemaphoreType.DMA((EXPERT_IN_RING,)),
                            pltpu.SemaphoreType.DMA((EXPERT_OUT_RING,))]),
        compiler_params=_params(),
        name="moe_experts",
    )(seg_start, seg_blocks, xs, wg, wu, wd)


def _combine_kernel(yg_ref, gt_ref, x_ref, g2_ref, fg_ref, o_ref):
    k = yg_ref.shape[0]
    gt = gt_ref[...]
    lo, hi = _unpack_halves(yg_ref[0])
    y_lo, y_hi = lo * gt[:, 0:1], hi * gt[:, 0:1]
    for r in range(1, k):
        lo, hi = _unpack_halves(yg_ref[r])
        y_lo, y_hi = y_lo + lo * gt[:, r:r + 1], y_hi + hi * gt[:, r:r + 1]
    y = jnp.concatenate([y_lo, y_hi], axis=1)
    o_ref[...] = _rms(x_ref[...] + g2_ref[0] * y, fg_ref[...])


def _combine(yg, gates_t, x1s, gate2, final_g, seq, tc):
    k, t, half = yg.shape
    d = x1s.shape[1]
    return pl.pallas_call(
        _combine_kernel,
        out_shape=jax.ShapeDtypeStruct((t, d), F32),
        grid=(t // tc,),
        in_specs=[pl.BlockSpec((k, tc, half), lambda i: (0, i, 0)),
                  pl.BlockSpec((tc, k), lambda i: (i, 0)),
                  pl.BlockSpec((tc, d), lambda i: (i, 0)),
                  pl.BlockSpec((1, 1, d), lambda i: ((i * tc) // seq, 0, 0)),
                  pl.BlockSpec((1, d), lambda i: (0, 0))],
        out_specs=pl.BlockSpec((tc, d), lambda i: (i, 0)),
        compiler_params=_params(),
        name="moe_combine",
    )(yg, gates_t, x1s, gate2, final_g.reshape(1, d))


def _layer(x2, c, bsz, seq, lb_row, ada_w, ada_b, norm1_g, w_in, hg_norm_g, w_branch_a, w_branch_b,
           w_out, norm2_g, w_router, router_bias, w_exp_gate, w_exp_up, w_exp_down, w_sh_gate,
           w_sh_up, w_sh_down, final_g):
    t, d = x2.shape
    n_e = w_router.shape[1]
    mod = _ada(c, ada_w, ada_b).reshape(bsz, 6, 1, d)
    shift1, scale1, gate1, shift2, scale2, gate2 = (mod[:, j] for j in range(6))

    hw = hg_norm_g.shape[0]
    aw = len(ATT_GROUPS) * ATT_HEADS_PER_GROUP * ATT_HEAD_DIM
    flat_segs = [(0, hw, BF16), (hw, hw, F32), (2 * hw, hw, BF16), (3 * hw, hw, BF16),
                 (4 * hw + 3 * aw, d, BF16), (4 * hw + 3 * aw + d, d, BF16)]
    (hq, hf, hi, hg, ga, gb), qkv = _inproj(
        x2, norm1_g, scale1, shift1, w_in.astype(BF16), bsz, seq, flat_segs, 4 * hw, tm=512)

    ya = _hgrn(hq, hf, hi, hg, lb_row, hg_norm_g, bsz, seq, ts=256)
    att = [_attn_group(*qkv[3 * g:3 * g + 3], g, nq=4) for g in range(len(ATT_GROUPS))]

    x1s, hp, logits_t = _merge(
        ya, att, ga, gb, x2, gate1, scale2, shift2, gate2, norm2_g, w_branch_a.astype(BF16),
        w_branch_b.astype(BF16), w_out.astype(BF16), w_router.T, w_sh_gate.astype(BF16),
        w_sh_up.astype(BF16), w_sh_down.astype(BF16), seq, tm=256)

    idx, gates, rank, cnt = _topk(logits_t, router_bias, tt=512)
    counts = cnt[:, 0]
    padded = (counts + MOE_BLOCK - 1) // MOE_BLOCK * MOE_BLOCK
    seg_start = (jnp.cumsum(padded) - padded).astype(I32)
    n_blocks = -(-(t * TOP_K) // MOE_BLOCK) + n_e
    dest = _dest(idx, rank, seg_start, tt=512)

    xs = _sc_scatter_rows(hp, dest, n_blocks * MOE_BLOCK)
    ys = _experts(seg_start, (padded // MOE_BLOCK).astype(I32), xs, w_exp_gate, w_exp_up,
                  w_exp_down)
    yg = _sc_gather_rows(ys, dest).reshape(TOP_K, t, d // 2)
    return _combine(yg, gates.T, x1s, gate2, final_g, seq, tc=256)


def kernel(x, c, ada_w, ada_b, norm1_g, w_in, lb_logits, hg_norm_g, w_branch_a, w_branch_b, w_out,
           norm2_g, w_router, router_bias, w_exp_gate, w_exp_up, w_exp_down, w_sh_gate, w_sh_up,
           w_sh_down, final_g):
    bsz, seq, d = x.shape
    depth = ada_w.shape[0]
    assert depth == 1, "the last layer's kernels also apply the final norm"
    lb_table = jnp.cumsum(jax.nn.softmax(lb_logits.astype(F32), axis=0), axis=0)
    out = _layer(x.reshape(bsz * seq, d), c, bsz, seq, lb_table[0], ada_w[0], ada_b[0], norm1_g[0],
                 w_in[0], hg_norm_g[0], w_branch_a[0], w_branch_b[0], w_out[0], norm2_g[0],
                 w_router[0], router_bias[0], w_exp_gate[0], w_exp_up[0], w_exp_down[0],
                 w_sh_gate[0], w_sh_up[0], w_sh_down[0], final_g)
    return out.reshape(bsz, seq, d)
```

```python
import functools

import jax
import jax.numpy as jnp
from jax import lax
from jax.experimental import pallas as pl
from jax.experimental.pallas import tpu as pltpu
from jax.experimental.pallas import tpu_sc as plsc

F32 = jnp.float32
BF16 = jnp.bfloat16
I32 = jnp.int32
U32 = jnp.uint32
HIGHEST = lax.Precision.HIGHEST

HG_HEADS = 4
HG_BLOCK = 16
HG_CHUNK = 32
HG_MILD_DECAY = -80.0
ATT_GROUPS = ((128, 1), (512, 4), (2048, 16))
ATT_HEADS_PER_GROUP = 4
ATT_HEAD_DIM = 64
TOP_K = 8
ROUTE_SCALE = 2.5
MOE_BLOCK = 256
RMS_EPS = 1e-6
BLOCK_DMA_PRIORITY = 1
SC_WINDOW = 128
EXPERT_GROUP = 4
EXPERT_IN_RING = 8
EXPERT_OUT_RING = 6

LANES = 128
VMEM_LIMIT_BYTES = 56 * 1024 * 1024


def _sigmoid(x):
    return 1.0 / (1.0 + jnp.exp(-x))


def _silu(x):
    return x * _sigmoid(x)


def _rms(x, g):
    return x * lax.rsqrt(jnp.mean(x * x, axis=-1, keepdims=True) + RMS_EPS) * g


def _pack_halves(x):
    n = x.shape[1] // 2
    bits = lax.bitcast_convert_type(x.astype(BF16).astype(F32), U32)
    return (bits[:, :n] >> 16) | (bits[:, n:] & jnp.uint32(0xFFFF0000))


def _unpack_halves(word):
    lo = lax.bitcast_convert_type(word << 16, F32)
    hi = lax.bitcast_convert_type(word & jnp.uint32(0xFFFF0000), F32)
    return lo, hi


def _params(n_axes=1):
    return pltpu.CompilerParams(
        dimension_semantics=("arbitrary",) * n_axes, vmem_limit_bytes=VMEM_LIMIT_BYTES)


def _ada_kernel(c_ref, w_ref, b_ref, o_ref):
    sc = _silu(c_ref[...])
    o_ref[...] = jnp.dot(sc, w_ref[...], preferred_element_type=F32, precision=HIGHEST) + b_ref[...]


def _ada(c, w, b):
    bsz, d = c.shape
    n = w.shape[1]
    return pl.pallas_call(
        _ada_kernel,
        out_shape=jax.ShapeDtypeStruct((bsz, n), F32),
        grid=(n // d,),
        in_specs=[pl.BlockSpec((bsz, d), lambda j: (0, 0)),
                  pl.BlockSpec((d, d), lambda j: (0, j)),
                  pl.BlockSpec((1, d), lambda j: (0, j))],
        out_specs=pl.BlockSpec((bsz, d), lambda j: (0, j)),
        compiler_params=_params(),
        name="ada_mod",
    )(c, w, b.reshape(1, n))


def _inproj_kernel(n_flat, flat_ranges, att_c0, x_ref, g_ref, sc_ref, sh_ref, w_ref, *refs):
    flat_refs, att_refs, scr = refs[:n_flat], refs[n_flat:-1], refs[-1]
    tm = x_ref.shape[0]
    h = _rms(x_ref[...], g_ref[...]) * (1.0 + sc_ref[0]) + sh_ref[0]
    hb = h.astype(BF16)
    for (c0, c1), o_ref in zip(flat_ranges, flat_refs):
        o_ref[...] = jnp.dot(hb, w_ref[:, c0:c1], preferred_element_type=F32).astype(o_ref.dtype)
    gw = ATT_HEADS_PER_GROUP * ATT_HEAD_DIM
    n_groups = len(ATT_GROUPS)
    for part in range(3):
        c0 = att_c0 + part * n_groups * gw
        res = jnp.dot(hb, w_ref[:, c0:c0 + n_groups * gw], preferred_element_type=F32)
        if part == 0:
            res = res * (ATT_HEAD_DIM ** -0.5)
        for g, (_, dil) in enumerate(ATT_GROUPS):
            o_ref = att_refs[g * 3 + part]
            sub = res[:, g * gw:(g + 1) * gw]
            if dil == 1:
                o_ref[0, 0] = sub.astype(BF16)
            else:
                for c in range(gw // LANES):
                    scr[c] = sub[:, c * LANES:(c + 1) * LANES]
                for r in range(dil):
                    o_ref[0, r] = jnp.concatenate(
                        [scr[c, pl.ds(r, tm // dil, stride=dil), :] for c in range(gw // LANES)],
                        axis=1).astype(BF16)


def _inproj(x2, g, scale, shift, w_bf16, bsz, seq, flat_segs, att_c0, tm):
    t, d = x2.shape
    gw = ATT_HEADS_PER_GROUP * ATT_HEAD_DIM
    n_per = seq // tm
    per_b = lambda i: (i // n_per, 0, 0)
    att_shapes, att_specs = [], []
    for _, dil in ATT_GROUPS:
        for _ in range(3):
            att_shapes.append(jax.ShapeDtypeStruct((bsz, dil, seq // dil, gw), BF16))
            att_specs.append(pl.BlockSpec((1, dil, tm // dil, gw),
                                          lambda i: (i // n_per, 0, i % n_per, 0)))
    outs = pl.pallas_call(
        functools.partial(_inproj_kernel, len(flat_segs),
                          tuple((c0, c0 + wdt) for c0, wdt, _ in flat_segs), att_c0),
        out_shape=[jax.ShapeDtypeStruct((t, wdt), dt) for _, wdt, dt in flat_segs] + att_shapes,
        grid=(t // tm,),
        in_specs=[pl.BlockSpec((tm, d), lambda i: (i, 0)),
                  pl.BlockSpec((1, d), lambda i: (0, 0)),
                  pl.BlockSpec((1, 1, d), per_b),
                  pl.BlockSpec((1, 1, d), per_b),
                  pl.BlockSpec(w_bf16.shape, lambda i: (0, 0))],
        out_specs=[pl.BlockSpec((tm, wdt), lambda i: (i, 0)) for _, wdt, _ in flat_segs]
        + att_specs,
        scratch_shapes=[pltpu.VMEM((gw // LANES, tm, LANES), F32)],
        compiler_params=_params(),
        name="in_proj",
    )(x2, g.reshape(1, d), scale, shift, w_bf16)
    return outs[:len(flat_segs)], outs[len(flat_segs):]


def _hgrn_kernel(ts, q_ref, f_ref, v_ref, gt_ref, lb_ref, ng_ref, o_ref, st_ref, b_ref):
    dk = q_ref.shape[1] // HG_HEADS
    n_chunks = ts // HG_CHUNK
    n_blk = HG_CHUNK // HG_BLOCK

    @pl.when(pl.program_id(1) == 0)
    def _():
        st_ref[...] = jnp.zeros_like(st_ref)

    row = lax.broadcasted_iota(I32, (LANES, LANES), 0)
    col = lax.broadcasted_iota(I32, (LANES, LANES), 1)
    same_chunk = (row // HG_CHUNK) == (col // HG_CHUNK)
    cum_mat = jnp.where(same_chunk & (col <= row), 1.0, 0.0).astype(BF16)

    def chunk_cumsum(x):
        out = []
        for r0 in range(0, ts, LANES):
            rest = x[r0:r0 + LANES]
            acc = None
            for _ in range(3):
                term = rest.astype(BF16)
                part = jnp.dot(cum_mat, term, preferred_element_type=F32)
                acc = part if acc is None else acc + part
                rest = rest - term.astype(F32)
            out.append(acc)
        return jnp.concatenate(out, axis=0)

    def forget(cs):
        lb = lb_ref[:, cs]
        return lb + (1.0 - lb) * _sigmoid(f_ref[:, cs])

    b_min = None
    for h in range(HG_HEADS):
        cs = slice(h * dk, (h + 1) * dk)
        b = chunk_cumsum(jnp.log(forget(cs)))
        b_ref[:, cs] = b
        m = jnp.min(b)
        b_min = m if b_min is None else jnp.minimum(b_min, m)
    mild = b_min >= HG_MILD_DECAY

    def finish(h, o, st):
        cs = slice(h * dk, (h + 1) * dk)
        st_ref[h] = st
        y = _rms(o, ng_ref[:, cs]) * _silu(gt_ref[:, cs].astype(F32))
        o_ref[:, cs] = y.astype(o_ref.dtype)

    @pl.when(mild)
    def _():
        span = 2 * HG_CHUNK
        causal = (lax.broadcasted_iota(I32, (span, span), 0)
                  >= lax.broadcasted_iota(I32, (span, span), 1))
        nt = lambda x, y: lax.dot_general(x, y, (((1,), (1,)), ((), ())),
                                          preferred_element_type=F32)
        for h in range(HG_HEADS):
            cs = slice(h * dk, (h + 1) * dk)
            v = v_ref[:, cs]
            b = b_ref[:, cs]
            q = q_ref[:, cs].astype(F32)
            k = 1.0 - forget(cs)
            st = st_ref[h]
            o_rows = []
            for r0 in range(0, ts, span):
                sl = slice(r0, r0 + span)
                b_first, b_second = b[r0:r0 + HG_CHUNK], b[r0 + HG_CHUNK:r0 + span]
                end_first = b_first[HG_CHUNK - 1:HG_CHUNK]
                end_second = b_second[HG_CHUNK - 1:HG_CHUNK]
                e = jnp.exp(jnp.concatenate([b_first - end_first, b_second], axis=0))
                qe = (q[sl] * e).astype(BF16)
                ke = k[sl] / e
                a = jnp.where(causal, nt(qe, ke.astype(BF16)), 0.0).astype(BF16)
                st_in = (st * jnp.exp(end_first)).astype(BF16)
                o_rows.append(jnp.dot(a, v[sl], preferred_element_type=F32) + nt(qe, st_in))
                kend = (ke * jnp.exp(end_second)).astype(BF16)
                vt = v[sl].astype(F32).T.astype(BF16)
                st = (st * jnp.exp(end_first + end_second)
                      + jnp.dot(vt, kend, preferred_element_type=F32))
            finish(h, jnp.concatenate(o_rows, axis=0), st)

    @pl.when(jnp.logical_not(mild))
    def _():
        _hgrn_steep(ts, dk, n_chunks, n_blk, q_ref, v_ref, b_ref, st_ref, forget, finish)


def _hgrn_steep(ts, dk, n_chunks, n_blk, q_ref, v_ref, b_ref, st_ref, forget, finish):
    t_in_blk = lax.broadcasted_iota(I32, (ts, dk), 0) % HG_BLOCK

    for h in range(HG_HEADS):
        cs = slice(h * dk, (h + 1) * dk)
        q = q_ref[:, cs].astype(F32)
        v = v_ref[:, cs].astype(F32)
        k = 1.0 - forget(cs)
        b = b_ref[:, cs]

        o = jnp.sum(q * k, axis=-1, keepdims=True) * v
        for d in range(1, HG_BLOCK):
            k_d = pltpu.roll(k, d, axis=0)
            b_d = pltpu.roll(b, d, axis=0)
            v_d = pltpu.roll(v, d, axis=0)
            w = jnp.sum(q * k_d * jnp.exp(jnp.minimum(b - b_d, 0.0)), axis=-1, keepdims=True)
            o = o + jnp.where(t_in_blk >= d, w * v_d, 0.0)

        st = st_ref[h]
        o_rows = []
        for c in range(n_chunks):
            r0 = c * HG_CHUNK
            bc = b[r0:r0 + HG_CHUNK]
            qc = q[r0:r0 + HG_CHUNK]
            kc = k[r0:r0 + HG_CHUNK]
            vc = v[r0:r0 + HG_CHUNK].astype(BF16)
            st_b = st.astype(BF16)
            for i in range(n_blk):
                i0 = i * HG_BLOCK
                if i == 0:
                    qt = qc[:HG_BLOCK] * jnp.exp(bc[:HG_BLOCK])
                    qs = qt
                else:
                    ref_row = bc[i0 - 1:i0]
                    qt = qc[i0:i0 + HG_BLOCK] * jnp.exp(bc[i0:i0 + HG_BLOCK] - ref_row)
                    qs = qt * jnp.exp(ref_row)
                oi = lax.dot_general(qs.astype(BF16), st_b, (((1,), (1,)), ((), ())),
                                     preferred_element_type=F32)
                if i > 0:
                    kh = kc[:i0] * jnp.exp(ref_row - bc[:i0])
                    a = lax.dot_general(qt.astype(BF16), kh.astype(BF16), (((1,), (1,)), ((), ())),
                                        preferred_element_type=F32)
                    oi = oi + jnp.dot(a.astype(BF16), vc[:i0], preferred_element_type=F32)
                o_rows.append(oi)
            b_end = bc[HG_CHUNK - 1:HG_CHUNK]
            kend = kc * jnp.exp(b_end - bc)
            vt = v[r0:r0 + HG_CHUNK].T.astype(BF16)
            st = st * jnp.exp(b_end) + jnp.dot(vt, kend.astype(BF16), preferred_element_type=F32)
        finish(h, o + jnp.concatenate(o_rows, axis=0), st)


def _hgrn(hq, hf, hi, hg, lb, ng, bsz, seq, ts):
    t, w = hq.shape
    dk = w // HG_HEADS
    n_s = seq // ts
    tile = lambda b, s: (b * n_s + s, 0)
    return pl.pallas_call(
        functools.partial(_hgrn_kernel, ts),
        out_shape=jax.ShapeDtypeStruct((t, w), BF16),
        grid=(bsz, n_s),
        in_specs=[pl.BlockSpec((ts, w), tile)] * 4
        + [pl.BlockSpec((1, w), lambda b, s: (0, 0))] * 2,
        out_specs=pl.BlockSpec((ts, w), tile),
        scratch_shapes=[pltpu.VMEM((HG_HEADS, dk, dk), F32), pltpu.VMEM((ts, w), F32)],
        compiler_params=_params(2),
        name="hgrn2",
    )(hq, hf, hi, hg, lb.reshape(1, w), ng.reshape(1, w))


def _attn_kernel(nk, nq, q_ref, kp_ref, kc_ref, vp_ref, vc_ref, o_ref, lse_ref):
    n = pl.program_id(2)
    e = ATT_HEAD_DIM
    i = lax.broadcasted_iota(I32, (nk, 2 * nk), 0)
    j = lax.broadcasted_iota(I32, (nk, 2 * nk), 1)
    band = (j >= i) & (j <= i + nk)
    kk = jnp.concatenate([kp_ref[0, 0], kc_ref[0, 0]], axis=0)
    vv = jnp.concatenate([vp_ref[0, 0], vc_ref[0, 0]], axis=0)
    first_head = lax.broadcasted_iota(I32, (nk, LANES), 1) < e
    zero = jnp.zeros((), q_ref.dtype)
    for b in range(nq):
        valid = band & ((j >= nk) | (n * nq + b > 0))
        rows = slice(b * nk, (b + 1) * nk)
        for c in range(0, ATT_HEADS_PER_GROUP * e, LANES):
            q = q_ref[0, 0, rows, c:c + LANES]
            kb = kk[b * nk:(b + 2) * nk, c:c + LANES]
            vb = vv[b * nk:(b + 2) * nk, c:c + LANES]
            outs, lses = [], []
            for keep in (first_head, jnp.logical_not(first_head)):
                s = lax.dot_general(jnp.where(keep, q, zero), kb, (((1,), (1,)), ((), ())),
                                    preferred_element_type=F32)
                s = jnp.where(valid, s, -jnp.inf)
                m = jnp.max(s, axis=-1, keepdims=True)
                p = jnp.exp(s - m)
                l = jnp.sum(p, axis=-1, keepdims=True)
                outs.append(jnp.dot(p.astype(BF16), vb, preferred_element_type=F32) / l)
                lses.append(m + jnp.log(l))
            o_ref[0, 0, rows, c:c + LANES] = jnp.where(first_head, outs[0], outs[1])
            lse_ref[0, 0, rows, c:c + LANES] = jnp.where(first_head, lses[0], lses[1])


def _attn_group(q, k, v, g, nq):
    window, dil = ATT_GROUPS[g]
    nk = window // dil
    bsz, _, ln, gw = q.shape
    nq = min(nq, ln // nk)
    assert ln % (nk * nq) == 0 and 2 * ATT_HEAD_DIM == LANES
    cur = pl.BlockSpec((1, 1, nq * nk, gw), lambda b, r, n: (b, r, n, 0))
    prev = pl.BlockSpec((1, 1, nk, gw), lambda b, r, n: (b, r, jnp.maximum(n * nq - 1, 0), 0))
    return pl.pallas_call(
        functools.partial(_attn_kernel, nk, nq),
        out_shape=[jax.ShapeDtypeStruct(q.shape, F32)] * 2,
        grid=(bsz, dil, ln // (nk * nq)),
        in_specs=[cur, prev, cur, prev, cur],
        out_specs=[cur, cur],
        compiler_params=_params(3),
        name=f"dilated_attn_g{g}",
    )(q, k, k, v, v)


def _token_major(ref, scr):
    dil, rows = ref.shape[1], ref.shape[2]
    if dil == 1:
        return ref[0, 0]
    n_col = scr.shape[0]
    for r in range(dil):
        for c in range(n_col):
            scr[c, pl.ds(r, rows, stride=dil), :] = ref[0, r, :, c * LANES:(c + 1) * LANES]
    return jnp.concatenate([scr[c] for c in range(n_col)], axis=1)


def _merge_kernel(ya_ref, o0_ref, o1_ref, o2_ref, l0_ref, l1_ref, l2_ref, ga_ref, gb_ref, x_ref,
                  g1_ref, sc2_ref, sh2_ref, g2_ref, n2_ref, wa_ref, wb_ref, wo_ref, wr_ref, wrl_ref,
                  wsg_ref, wsu_ref, wsd_ref, x1_ref, hp_ref, lg_ref, *scr):
    l0, l1, l2 = (_token_major(r, s) for r, s in zip((l0_ref, l1_ref, l2_ref), scr[:3]))
    o0, o1, o2 = (_token_major(r, s) for r, s in zip((o0_ref, o1_ref, o2_ref), scr[3:]))
    m = jnp.maximum(jnp.maximum(l0, l1), l2)
    e0, e1, e2 = jnp.exp(l0 - m), jnp.exp(l1 - m), jnp.exp(l2 - m)
    yb = (e0 * o0 + e1 * o1 + e2 * o2) / (e0 + e1 + e2)
    merged = (_sigmoid(ga_ref[...].astype(F32))
              * jnp.dot(ya_ref[...], wa_ref[...], preferred_element_type=F32)
              + _sigmoid(gb_ref[...].astype(F32))
              * jnp.dot(yb.astype(BF16), wb_ref[...], preferred_element_type=F32))
    x1 = x_ref[...] + g1_ref[0] * jnp.dot(merged.astype(BF16), wo_ref[...],
                                           preferred_element_type=F32)
    h2 = _rms(x1, n2_ref[...]) * (1.0 + sc2_ref[0]) + sh2_ref[0]
    hb = h2.astype(BF16)
    act = (_silu(jnp.dot(hb, wsg_ref[...], preferred_element_type=F32))
           * jnp.dot(hb, wsu_ref[...], preferred_element_type=F32))
    shared = jnp.dot(act.astype(BF16), wsd_ref[...], preferred_element_type=F32)
    x1_ref[...] = x1 + g2_ref[0] * shared
    hp_ref[...] = _pack_halves(h2)
    h_lo = (h2 - hb.astype(F32)).astype(BF16)
    nt = lambda a, b: lax.dot_general(a, b, (((1,), (1,)), ((), ())), preferred_element_type=F32)
    lg_ref[...] = nt(wr_ref[...], hb) + (nt(wr_ref[...], h_lo) + nt(wrl_ref[...], hb))


def _merge(ya, att, ga, gb, x2, gate1, scale2, shift2, gate2, norm2_g, wa, wb, wo, wr_t, wsg, wsu,
           wsd, seq, tm):
    t, d = x2.shape
    n_e = wr_t.shape[0]
    wr_hi = wr_t.astype(BF16)
    wr_lo = (wr_t - wr_hi.astype(F32)).astype(BF16)
    n_per = seq // tm
    per_b = lambda i: (i // n_per, 0, 0)
    rows = lambda wdt: pl.BlockSpec((tm, wdt), lambda i: (i, 0))
    full = lambda a: pl.BlockSpec(a.shape, lambda i: (0,) * a.ndim)
    vec = pl.BlockSpec((1, 1, d), per_b)
    (o0, l0), (o1, l1), (o2, l2) = att
    gw = o0.shape[3]
    by_residue = lambda a: pl.BlockSpec((1, a.shape[1], tm // a.shape[1], gw),
                                        lambda i: (i // n_per, 0, i % n_per, 0))
    att_in = (o0, o1, o2, l0, l1, l2)
    return pl.pallas_call(
        _merge_kernel,
        out_shape=[jax.ShapeDtypeStruct((t, d), F32),
                   jax.ShapeDtypeStruct((t, d // 2), U32),
                   jax.ShapeDtypeStruct((n_e, t), F32)],
        grid=(t // tm,),
        in_specs=[rows(ya.shape[1])] + [by_residue(a) for a in att_in] + [rows(d)] * 3
        + [vec, vec, vec, vec, pl.BlockSpec((1, d), lambda i: (0, 0))]
        + [full(a) for a in (wa, wb, wo, wr_hi, wr_lo, wsg, wsu, wsd)],
        out_specs=[rows(d), rows(d // 2), pl.BlockSpec((n_e, tm), lambda i: (0, i))],
        scratch_shapes=[pltpu.VMEM((gw // LANES, tm, LANES), F32)] * 6,
        compiler_params=_params(),
        name="merge_router",
    )(ya, *att_in, ga, gb, x2, gate1, scale2, shift2, gate2,
      norm2_g.reshape(1, d), wa, wb, wo, wr_hi, wr_lo, wsg, wsu, wsd)


def _topk_kernel(lg_ref, bias_ref, idx_ref, gate_ref, rank_ref, cnt_ref, carry_ref):
    n_e, tt = lg_ref.shape

    @pl.when(pl.program_id(0) == 0)
    def _():
        carry_ref[...] = jnp.zeros_like(carry_ref)

    scores = _sigmoid(lg_ref[...])
    sel = scores + bias_ref[...]
    eio = lax.broadcasted_iota(I32, (n_e, tt), 0)
    picked = jnp.zeros((n_e, tt), F32)
    idxs, vals = [], []
    for _ in range(TOP_K):
        m = jnp.max(sel, axis=0, keepdims=True)
        ik = jnp.min(jnp.where(sel == m, eio, n_e), axis=0, keepdims=True)
        hit = eio == ik
        vals.append(jnp.sum(jnp.where(hit, scores, 0.0), axis=0, keepdims=True))
        sel = jnp.where(hit, -jnp.inf, sel)
        picked = picked + jnp.where(hit, 1.0, 0.0)
        idxs.append(ik)
    denom = vals[0]
    for v in vals[1:]:
        denom = denom + v
    gate_ref[...] = jnp.concatenate([v / denom * ROUTE_SCALE for v in vals], axis=0)
    idx_ref[...] = jnp.concatenate(idxs, axis=0)

    upper = (lax.broadcasted_iota(I32, (tt, tt), 0) <= lax.broadcasted_iota(I32, (tt, tt), 1))
    incl = jnp.dot(picked.astype(BF16), jnp.where(upper, 1.0, 0.0).astype(BF16),
                   preferred_element_type=F32)
    before = incl - picked + carry_ref[...]
    rank_ref[...] = jnp.concatenate(
        [jnp.sum(jnp.where(eio == ik, before, 0.0), axis=0, keepdims=True) for ik in idxs],
        axis=0).astype(I32)
    carry_ref[...] = carry_ref[...] + jnp.sum(picked, axis=1, keepdims=True)
    cnt_ref[...] = jnp.broadcast_to(carry_ref[...], cnt_ref.shape).astype(I32)


def _topk(logits_t, bias, tt):
    n_e, t = logits_t.shape
    tok = pl.BlockSpec((TOP_K, tt), lambda i: (0, i))
    return pl.pallas_call(
        _topk_kernel,
        out_shape=[jax.ShapeDtypeStruct((TOP_K, t), I32), jax.ShapeDtypeStruct((TOP_K, t), F32),
                   jax.ShapeDtypeStruct((TOP_K, t), I32), jax.ShapeDtypeStruct((n_e, 128), I32)],
        grid=(t // tt,),
        in_specs=[pl.BlockSpec((n_e, tt), lambda i: (0, i)),
                  pl.BlockSpec((n_e, 1), lambda i: (0, 0))],
        out_specs=[tok, tok, tok, pl.BlockSpec((n_e, 128), lambda i: (0, 0))],
        scratch_shapes=[pltpu.VMEM((n_e, 1), F32)],
        compiler_params=_params(),
        name="router_topk",
    )(logits_t, bias.reshape(n_e, 1))


def _dest_kernel(idx_ref, rank_ref, start_ref, o_ref):
    k, tt = idx_ref.shape
    n_e = start_ref.shape[0]
    eio = lax.broadcasted_iota(I32, (n_e, tt), 0)
    start = start_ref[...]
    rows = [jnp.sum(jnp.where(eio == idx_ref[r:r + 1, :], start, 0), axis=0, keepdims=True)
            for r in range(k)]
    o_ref[...] = jnp.concatenate(rows, axis=0) + rank_ref[...]


def _dest(idx, rank, seg_start, tt):
    k, t = idx.shape
    n_e = seg_start.shape[0]
    tok = pl.BlockSpec((k, tt), lambda i: (0, i))
    return pl.pallas_call(
        _dest_kernel,
        out_shape=jax.ShapeDtypeStruct((k, t), I32),
        grid=(t // tt,),
        in_specs=[tok, tok, pl.BlockSpec((n_e, 1), lambda i: (0, 0))],
        out_specs=tok,
        compiler_params=_params(),
        name="moe_dest",
    )(idx, rank, seg_start.reshape(n_e, 1))


def _sc_mesh():
    return plsc.VectorSubcoreMesh(core_axis_name="core", subcore_axis_name="subcore")


def _sc_scatter_rows(rows, dest, n_out):
    k, t = dest.shape
    w = rows.shape[1]
    mesh = _sc_mesh()
    n_workers = mesh.num_cores * mesh.num_subcores
    win_per_worker = t // (SC_WINDOW * n_workers)
    assert win_per_worker * SC_WINDOW * n_workers == t

    @functools.partial(
        pl.kernel, out_type=jax.ShapeDtypeStruct((n_out, w), rows.dtype), mesh=mesh,
        scratch_types=[pltpu.VMEM((SC_WINDOW, w), rows.dtype)]
        + [pltpu.VMEM((1, SC_WINDOW), I32)] * k + [pltpu.SemaphoreType.DMA],
        name="moe_dispatch_sc")
    def run(rows_hbm, idx_hbm, out_hbm, rows_v, *rest):
        idx_v, sem = rest[:k], rest[k]
        worker = lax.axis_index("subcore") * mesh.num_cores + lax.axis_index("core")

        @pl.loop(0, win_per_worker)
        def _(j):
            t0 = pl.multiple_of((worker * win_per_worker + j) * SC_WINDOW, SC_WINDOW)
            pltpu.sync_copy(rows_hbm.at[pl.ds(t0, SC_WINDOW)], rows_v)
            for r in range(k):
                pltpu.sync_copy(idx_hbm.at[:, pl.ds(r * t + t0, SC_WINDOW)], idx_v[r])
            copies = [pltpu.async_copy(rows_v, out_hbm.at[idx_v[r].at[0]], sem) for r in range(k)]
            for c in copies:
                c.wait()

    return run(rows, dest.reshape(1, k * t))


def _sc_gather_rows(table, dest):
    k, t = dest.shape
    w = table.shape[1]
    mesh = _sc_mesh()
    n_workers = mesh.num_cores * mesh.num_subcores
    win_per_worker = (k * t) // (SC_WINDOW * n_workers)
    assert win_per_worker * SC_WINDOW * n_workers == k * t

    @functools.partial(
        pl.kernel, out_type=jax.ShapeDtypeStruct((k * t, w), table.dtype), mesh=mesh,
        scratch_types=[pltpu.VMEM((SC_WINDOW, w), table.dtype), pltpu.VMEM((1, SC_WINDOW), I32)],
        name="moe_gather_sc")
    def run(table_hbm, idx_hbm, out_hbm, rows_v, idx_v):
        worker = lax.axis_index("subcore") * mesh.num_cores + lax.axis_index("core")

        @pl.loop(0, win_per_worker)
        def _(j):
            p0 = pl.multiple_of((worker * win_per_worker + j) * SC_WINDOW, SC_WINDOW)
            pltpu.sync_copy(idx_hbm.at[:, pl.ds(p0, SC_WINDOW)], idx_v)
            pltpu.sync_copy(table_hbm.at[idx_v.at[0]], rows_v)
            pltpu.sync_copy(rows_v, out_hbm.at[pl.ds(p0, SC_WINDOW)])

    return run(table, dest.reshape(1, k * t))


def _expert_kernel(start_ref, nblk_ref, xs_ref, wg_ref, wu_ref, wd_ref, ys_ref,
                   xbuf, ybuf, wgb, wub, wdb, sem_in, sem_out):
    e = pl.program_id(0)
    n_e = pl.num_programs(0)
    nb = nblk_ref[e]
    g0 = start_ref[e] // MOE_BLOCK
    n_used = start_ref[n_e - 1] // MOE_BLOCK + nblk_ref[n_e - 1]
    n_in, n_out = xbuf.shape[0], ybuf.shape[0]

    def rows(g):
        return pl.ds(pl.multiple_of(g * MOE_BLOCK, MOE_BLOCK), MOE_BLOCK)

    def in_copy(g):
        slot = lax.rem(g, n_in)
        return pltpu.make_async_copy(xs_ref.at[rows(g), :], xbuf.at[slot], sem_in.at[slot])

    def out_copy(g):
        slot = lax.rem(g, n_out)
        return pltpu.make_async_copy(ybuf.at[slot], ys_ref.at[rows(g), :], sem_out.at[slot])

    look = n_in - EXPERT_GROUP

    @pl.when(e == 0)
    def _():
        for g in range(look):
            @pl.when(g < n_used)
            def _():
                in_copy(g).start(priority=BLOCK_DMA_PRIORITY)

    @pl.when(nb > 0)
    def _():
        wgb[...] = wg_ref[0].astype(BF16)
        wub[...] = wu_ref[0].astype(BF16)
        wdb[...] = wd_ref[0].astype(BF16)
        half = xbuf.shape[2]

        def swiglu(word):
            lo, hi = _unpack_halves(word)
            lo, hi = lo.astype(BF16), hi.astype(BF16)
            gate = (jnp.dot(lo, wgb[:half], preferred_element_type=F32)
                    + jnp.dot(hi, wgb[half:], preferred_element_type=F32))
            up = (jnp.dot(lo, wub[:half], preferred_element_type=F32)
                  + jnp.dot(hi, wub[half:], preferred_element_type=F32))
            act = (_silu(gate) * up).astype(BF16)
            return jnp.dot(act, wdb[...], preferred_element_type=F32)

        def process(g, m):
            for i in range(m):
                in_copy(g + i).wait()
            for i in range(m):
                @pl.when(g + look + i < n_used)
                def _():
                    in_copy(g + look + i).start(priority=BLOCK_DMA_PRIORITY)
            ys = [swiglu(xbuf[lax.rem(g + i, n_in)]) for i in range(m)]
            for i in range(m):
                @pl.when(g + i >= n_out)
                def _():
                    out_copy(g + i - n_out).wait()

                ybuf[lax.rem(g + i, n_out)] = _pack_halves(ys[i])
                out_copy(g + i).start(priority=BLOCK_DMA_PRIORITY)

        def group_body(p, carry):
            process(g0 + p * EXPERT_GROUP, EXPERT_GROUP)
            return carry

        lax.fori_loop(0, nb // EXPERT_GROUP, group_body, 0)
        for m in range(1, EXPERT_GROUP):
            @pl.when(lax.rem(nb, EXPERT_GROUP) == m)
            def _():
                process(g0 + nb - m, m)

    @pl.when(e == n_e - 1)
    def _():
        for i in range(n_out):
            @pl.when(n_used - 1 - i >= 0)
            def _():
                out_copy(n_used - 1 - i).wait()


def _experts(seg_start, seg_blocks, xs, wg, wu, wd):
    n_slots, half = xs.shape
    n_e, d, de = wg.shape
    return pl.pallas_call(
        _expert_kernel,
        out_shape=jax.ShapeDtypeStruct((n_slots, half), U32),
        grid_spec=pltpu.PrefetchScalarGridSpec(
            num_scalar_prefetch=2,
            grid=(n_e,),
            in_specs=[pl.BlockSpec(memory_space=pl.ANY),
                      pl.BlockSpec((1, d, de), lambda e, s, n: (e, 0, 0)),
                      pl.BlockSpec((1, d, de), lambda e, s, n: (e, 0, 0)),
                      pl.BlockSpec((1, de, d), lambda e, s, n: (e, 0, 0))],
            out_specs=pl.BlockSpec(memory_space=pl.ANY),
            scratch_shapes=[pltpu.VMEM((EXPERT_IN_RING, MOE_BLOCK, half), U32),
                            pltpu.VMEM((EXPERT_OUT_RING, MOE_BLOCK, half), U32),
                            pltpu.VMEM((d, de), BF16), pltpu.VMEM((d, de), BF16),
                            pltpu.VMEM((de, d), BF16),
                            pltpu.SemaphoreType.DMA((EXPERT_IN_RING,)),
                            pltpu.SemaphoreType.DMA((EXPERT_OUT_RING,))]),
        compiler_params=_params(),
        name="moe_experts",
    )(seg_start, seg_blocks, xs, wg, wu, wd)


def _combine_kernel(yg_ref, gt_ref, x_ref, g2_ref, fg_ref, o_ref):
    k = yg_ref.shape[0]
    gt = gt_ref[...]
    lo, hi = _unpack_halves(yg_ref[0])
    y_lo, y_hi = lo * gt[:, 0:1], hi * gt[:, 0:1]
    for r in range(1, k):
        lo, hi = _unpack_halves(yg_ref[r])
        y_lo, y_hi = y_lo + lo * gt[:, r:r + 1], y_hi + hi * gt[:, r:r + 1]
    y = jnp.concatenate([y_lo, y_hi], axis=1)
    o_ref[...] = _rms(x_ref[...] + g2_ref[0] * y, fg_ref[...])


def _combine(yg, gates_t, x1s, gate2, final_g, seq, tc):
    k, t, half = yg.shape
    d = x1s.shape[1]
    return pl.pallas_call(
        _combine_kernel,
        out_shape=jax.ShapeDtypeStruct((t, d), F32),
        grid=(t // tc,),
        in_specs=[pl.BlockSpec((k, tc, half), lambda i: (0, i, 0)),
                  pl.BlockSpec((tc, k), lambda i: (i, 0)),
                  pl.BlockSpec((tc, d), lambda i: (i, 0)),
                  pl.BlockSpec((1, 1, d), lambda i: ((i * tc) // seq, 0, 0)),
                  pl.BlockSpec((1, d), lambda i: (0, 0))],
        out_specs=pl.BlockSpec((tc, d), lambda i: (i, 0)),
        compiler_params=_params(),
        name="moe_combine",
    )(yg, gates_t, x1s, gate2, final_g.reshape(1, d))


def _layer(x2, c, bsz, seq, lb_row, ada_w, ada_b, norm1_g, w_in, hg_norm_g, w_branch_a, w_branch_b,
           w_out, norm2_g, w_router, router_bias, w_exp_gate, w_exp_up, w_exp_down, w_sh_gate,
           w_sh_up, w_sh_down, final_g):
    t, d = x2.shape
    n_e = w_router.shape[1]
    mod = _ada(c, ada_w, ada_b).reshape(bsz, 6, 1, d)
    shift1, scale1, gate1, shift2, scale2, gate2 = (mod[:, j] for j in range(6))

    hw = hg_norm_g.shape[0]
    aw = len(ATT_GROUPS) * ATT_HEADS_PER_GROUP * ATT_HEAD_DIM
    flat_segs = [(0, hw, BF16), (hw, hw, F32), (2 * hw, hw, BF16), (3 * hw, hw, BF16),
                 (4 * hw + 3 * aw, d, BF16), (4 * hw + 3 * aw + d, d, BF16)]
    (hq, hf, hi, hg, ga, gb), qkv = _inproj(
        x2, norm1_g, scale1, shift1, w_in.astype(BF16), bsz, seq, flat_segs, 4 * hw, tm=512)

    ya = _hgrn(hq, hf, hi, hg, lb_row, hg_norm_g, bsz, seq, ts=256)
    att = [_attn_group(*qkv[3 * g:3 * g + 3], g, nq=4) for g in range(len(ATT_GROUPS))]

    x1s, hp, logits_t = _merge(
        ya, att, ga, gb, x2, gate1, scale2, shift2, gate2, norm2_g, w_branch_a.astype(BF16),
        w_branch_b.astype(BF16), w_out.astype(BF16), w_router.T, w_sh_gate.astype(BF16),
        w_sh_up.astype(BF16), w_sh_down.astype(BF16), seq, tm=512)

    idx, gates, rank, cnt = _topk(logits_t, router_bias, tt=512)
    counts = cnt[:, 0]
    padded = (counts + MOE_BLOCK - 1) // MOE_BLOCK * MOE_BLOCK
    seg_start = (jnp.cumsum(padded) - padded).astype(I32)
    n_blocks = -(-(t * TOP_K) // MOE_BLOCK) + n_e
    dest = _dest(idx, rank, seg_start, tt=512)

    xs = _sc_scatter_rows(hp, dest, n_blocks * MOE_BLOCK)
    ys = _experts(seg_start, (padded // MOE_BLOCK).astype(I32), xs, w_exp_gate, w_exp_up,
                  w_exp_down)
    yg = _sc_gather_rows(ys, dest).reshape(TOP_K, t, d // 2)
    return _combine(yg, gates.T, x1s, gate2, final_g, seq, tc=256)


def kernel(x, c, ada_w, ada_b, norm1_g, w_in, lb_logits, hg_norm_g, w_branch_a, w_branch_b, w_out,
           norm2_g, w_router, router_bias, w_exp_gate, w_exp_up, w_exp_down, w_sh_gate, w_sh_up,
           w_sh_down, final_g):
    bsz, seq, d = x.shape
    depth = ada_w.shape[0]
    assert depth == 1, "the last layer's kernels also apply the final norm"
    lb_table = jnp.cumsum(jax.nn.softmax(lb_logits.astype(F32), axis=0), axis=0)
    out = _layer(x.reshape(bsz * seq, d), c, bsz, seq, lb_table[0], ada_w[0], ada_b[0], norm1_g[0],
                 w_in[0], hg_norm_g[0], w_branch_a[0], w_branch_b[0], w_out[0], norm2_g[0],
                 w_router[0], router_bias[0], w_exp_gate[0], w_exp_up[0], w_exp_down[0],
                 w_sh_gate[0], w_sh_up[0], w_sh_down[0], final_g)
    return out.reshape(bsz, seq, d)
```

```python
import functools

import jax
import jax.numpy as jnp
from jax import lax
from jax.experimental import pallas as pl
from jax.experimental.pallas import tpu as pltpu
from jax.experimental.pallas import tpu_sc as plsc

F32 = jnp.float32
BF16 = jnp.bfloat16
I32 = jnp.int32
U32 = jnp.uint32
HIGHEST = lax.Precision.HIGHEST

HG_HEADS = 4
HG_BLOCK = 16
HG_CHUNK = 32
HG_MILD_DECAY = -80.0
ATT_GROUPS = ((128, 1), (512, 4), (2048, 16))
ATT_HEADS_PER_GROUP = 4
ATT_HEAD_DIM = 64
TOP_K = 8
ROUTE_SCALE = 2.5
MOE_BLOCK = 256
RMS_EPS = 1e-6
BLOCK_DMA_PRIORITY = 1
SC_WINDOW = 128
COMBINE_PARTS = 2
EXPERT_GROUP = 4
EXPERT_IN_RING = 8
EXPERT_OUT_RING = 6

LANES = 128
VMEM_LIMIT_BYTES = 56 * 1024 * 1024


def _sigmoid(x):
    return 1.0 / (1.0 + jnp.exp(-x))


def _silu(x):
    return x * _sigmoid(x)


def _rms(x, g):
    return x * lax.rsqrt(jnp.mean(x * x, axis=-1, keepdims=True) + RMS_EPS) * g


def _pack_halves(x):
    n = x.shape[1] // 2
    bits = lax.bitcast_convert_type(x.astype(BF16).astype(F32), U32)
    return (bits[:, :n] >> 16) | (bits[:, n:] & jnp.uint32(0xFFFF0000))


def _unpack_halves(word):
    lo = lax.bitcast_convert_type(word << 16, F32)
    hi = lax.bitcast_convert_type(word & jnp.uint32(0xFFFF0000), F32)
    return lo, hi


def _params(n_axes=1):
    return pltpu.CompilerParams(
        dimension_semantics=("arbitrary",) * n_axes, vmem_limit_bytes=VMEM_LIMIT_BYTES)


def _ada_kernel(c_ref, w_ref, b_ref, o_ref):
    sc = _silu(c_ref[...])
    o_ref[...] = jnp.dot(sc, w_ref[...], preferred_element_type=F32, precision=HIGHEST) + b_ref[...]


def _ada(c, w, b):
    bsz, d = c.shape
    n = w.shape[1]
    return pl.pallas_call(
        _ada_kernel,
        out_shape=jax.ShapeDtypeStruct((bsz, n), F32),
        grid=(n // d,),
        in_specs=[pl.BlockSpec((bsz, d), lambda j: (0, 0)),
                  pl.BlockSpec((d, d), lambda j: (0, j)),
                  pl.BlockSpec((1, d), lambda j: (0, j))],
        out_specs=pl.BlockSpec((bsz, d), lambda j: (0, j)),
        compiler_params=_params(),
        name="ada_mod",
    )(c, w, b.reshape(1, n))


def _inproj_kernel(n_flat, flat_ranges, att_c0, x_ref, g_ref, sc_ref, sh_ref, w_ref, *refs):
    flat_refs, att_refs, scr = refs[:n_flat], refs[n_flat:-1], refs[-1]
    tm = x_ref.shape[0]
    h = _rms(x_ref[...], g_ref[...]) * (1.0 + sc_ref[0]) + sh_ref[0]
    hb = h.astype(BF16)
    for (c0, c1), o_ref in zip(flat_ranges, flat_refs):
        o_ref[...] = jnp.dot(hb, w_ref[:, c0:c1], preferred_element_type=F32).astype(o_ref.dtype)
    gw = ATT_HEADS_PER_GROUP * ATT_HEAD_DIM
    n_groups = len(ATT_GROUPS)
    for part in range(3):
        c0 = att_c0 + part * n_groups * gw
        res = jnp.dot(hb, w_ref[:, c0:c0 + n_groups * gw], preferred_element_type=F32)
        if part == 0:
            res = res * (ATT_HEAD_DIM ** -0.5)
        for g, (_, dil) in enumerate(ATT_GROUPS):
            o_ref = att_refs[g * 3 + part]
            sub = res[:, g * gw:(g + 1) * gw]
            if dil == 1:
                o_ref[0, 0] = sub.astype(BF16)
            else:
                for c in range(gw // LANES):
                    scr[c] = sub[:, c * LANES:(c + 1) * LANES]
                for r in range(dil):
                    o_ref[0, r] = jnp.concatenate(
                        [scr[c, pl.ds(r, tm // dil, stride=dil), :] for c in range(gw // LANES)],
                        axis=1).astype(BF16)


def _inproj(x2, g, scale, shift, w_bf16, bsz, seq, flat_segs, att_c0, tm):
    t, d = x2.shape
    gw = ATT_HEADS_PER_GROUP * ATT_HEAD_DIM
    n_per = seq // tm
    per_b = lambda i: (i // n_per, 0, 0)
    att_shapes, att_specs = [], []
    for _, dil in ATT_GROUPS:
        for _ in range(3):
            att_shapes.append(jax.ShapeDtypeStruct((bsz, dil, seq // dil, gw), BF16))
            att_specs.append(pl.BlockSpec((1, dil, tm // dil, gw),
                                          lambda i: (i // n_per, 0, i % n_per, 0)))
    outs = pl.pallas_call(
        functools.partial(_inproj_kernel, len(flat_segs),
                          tuple((c0, c0 + wdt) for c0, wdt, _ in flat_segs), att_c0),
        out_shape=[jax.ShapeDtypeStruct((t, wdt), dt) for _, wdt, dt in flat_segs] + att_shapes,
        grid=(t // tm,),
        in_specs=[pl.BlockSpec((tm, d), lambda i: (i, 0)),
                  pl.BlockSpec((1, d), lambda i: (0, 0)),
                  pl.BlockSpec((1, 1, d), per_b),
                  pl.BlockSpec((1, 1, d), per_b),
                  pl.BlockSpec(w_bf16.shape, lambda i: (0, 0))],
        out_specs=[pl.BlockSpec((tm, wdt), lambda i: (i, 0)) for _, wdt, _ in flat_segs]
        + att_specs,
        scratch_shapes=[pltpu.VMEM((gw // LANES, tm, LANES), F32)],
        compiler_params=_params(),
        name="in_proj",
    )(x2, g.reshape(1, d), scale, shift, w_bf16)
    return outs[:len(flat_segs)], outs[len(flat_segs):]


def _hgrn_kernel(ts, q_ref, f_ref, v_ref, gt_ref, lb_ref, ng_ref, o_ref, st_ref, b_ref):
    dk = q_ref.shape[1] // HG_HEADS
    n_chunks = ts // HG_CHUNK
    n_blk = HG_CHUNK // HG_BLOCK

    @pl.when(pl.program_id(1) == 0)
    def _():
        st_ref[...] = jnp.zeros_like(st_ref)

    row = lax.broadcasted_iota(I32, (LANES, LANES), 0)
    col = lax.broadcasted_iota(I32, (LANES, LANES), 1)
    same_chunk = (row // HG_CHUNK) == (col // HG_CHUNK)
    cum_mat = jnp.where(same_chunk & (col <= row), 1.0, 0.0).astype(BF16)

    def chunk_cumsum(x):
        out = []
        for r0 in range(0, ts, LANES):
            rest = x[r0:r0 + LANES]
            acc = None
            for _ in range(3):
                term = rest.astype(BF16)
                part = jnp.dot(cum_mat, term, preferred_element_type=F32)
                acc = part if acc is None else acc + part
                rest = rest - term.astype(F32)
            out.append(acc)
        return jnp.concatenate(out, axis=0)

    def forget(cs):
        lb = lb_ref[:, cs]
        return lb + (1.0 - lb) * _sigmoid(f_ref[:, cs])

    b_min = None
    for h in range(HG_HEADS):
        cs = slice(h * dk, (h + 1) * dk)
        b = chunk_cumsum(jnp.log(forget(cs)))
        b_ref[:, cs] = b
        m = jnp.min(b)
        b_min = m if b_min is None else jnp.minimum(b_min, m)
    mild = b_min >= HG_MILD_DECAY

    def finish(h, o, st):
        cs = slice(h * dk, (h + 1) * dk)
        st_ref[h] = st
        y = _rms(o, ng_ref[:, cs]) * _silu(gt_ref[:, cs].astype(F32))
        o_ref[:, cs] = y.astype(o_ref.dtype)

    @pl.when(mild)
    def _():
        span = 2 * HG_CHUNK
        causal = (lax.broadcasted_iota(I32, (span, span), 0)
                  >= lax.broadcasted_iota(I32, (span, span), 1))
        nt = lambda x, y: lax.dot_general(x, y, (((1,), (1,)), ((), ())),
                                          preferred_element_type=F32)
        for h in range(HG_HEADS):
            cs = slice(h * dk, (h + 1) * dk)
            v = v_ref[:, cs]
            b = b_ref[:, cs]
            q = q_ref[:, cs].astype(F32)
            k = 1.0 - forget(cs)
            st = st_ref[h]
            o_rows = []
            for r0 in range(0, ts, span):
                sl = slice(r0, r0 + span)
                b_first, b_second = b[r0:r0 + HG_CHUNK], b[r0 + HG_CHUNK:r0 + span]
                end_first = b_first[HG_CHUNK - 1:HG_CHUNK]
                end_second = b_second[HG_CHUNK - 1:HG_CHUNK]
                e = jnp.exp(jnp.concatenate([b_first - end_first, b_second], axis=0))
                qe = (q[sl] * e).astype(BF16)
                ke = k[sl] / e
                a = jnp.where(causal, nt(qe, ke.astype(BF16)), 0.0).astype(BF16)
                st_in = (st * jnp.exp(end_first)).astype(BF16)
                o_rows.append(jnp.dot(a, v[sl], preferred_element_type=F32) + nt(qe, st_in))
                kend = (ke * jnp.exp(end_second)).astype(BF16)
                vt = v[sl].astype(F32).T.astype(BF16)
                st = (st * jnp.exp(end_first + end_second)
                      + jnp.dot(vt, kend, preferred_element_type=F32))
            finish(h, jnp.concatenate(o_rows, axis=0), st)

    @pl.when(jnp.logical_not(mild))
    def _():
        _hgrn_steep(ts, dk, n_chunks, n_blk, q_ref, v_ref, b_ref, st_ref, forget, finish)


def _hgrn_steep(ts, dk, n_chunks, n_blk, q_ref, v_ref, b_ref, st_ref, forget, finish):
    t_in_blk = lax.broadcasted_iota(I32, (ts, dk), 0) % HG_BLOCK

    for h in range(HG_HEADS):
        cs = slice(h * dk, (h + 1) * dk)
        q = q_ref[:, cs].astype(F32)
        v = v_ref[:, cs].astype(F32)
        k = 1.0 - forget(cs)
        b = b_ref[:, cs]

        o = jnp.sum(q * k, axis=-1, keepdims=True) * v
        for d in range(1, HG_BLOCK):
            k_d = pltpu.roll(k, d, axis=0)
            b_d = pltpu.roll(b, d, axis=0)
            v_d = pltpu.roll(v, d, axis=0)
            w = jnp.sum(q * k_d * jnp.exp(jnp.minimum(b - b_d, 0.0)), axis=-1, keepdims=True)
            o = o + jnp.where(t_in_blk >= d, w * v_d, 0.0)

        st = st_ref[h]
        o_rows = []
        for c in range(n_chunks):
            r0 = c * HG_CHUNK
            bc = b[r0:r0 + HG_CHUNK]
            qc = q[r0:r0 + HG_CHUNK]
            kc = k[r0:r0 + HG_CHUNK]
            vc = v[r0:r0 + HG_CHUNK].astype(BF16)
            st_b = st.astype(BF16)
            for i in range(n_blk):
                i0 = i * HG_BLOCK
                if i == 0:
                    qt = qc[:HG_BLOCK] * jnp.exp(bc[:HG_BLOCK])
                    qs = qt
                else:
                    ref_row = bc[i0 - 1:i0]
                    qt = qc[i0:i0 + HG_BLOCK] * jnp.exp(bc[i0:i0 + HG_BLOCK] - ref_row)
                    qs = qt * jnp.exp(ref_row)
                oi = lax.dot_general(qs.astype(BF16), st_b, (((1,), (1,)), ((), ())),
                                     preferred_element_type=F32)
                if i > 0:
                    kh = kc[:i0] * jnp.exp(ref_row - bc[:i0])
                    a = lax.dot_general(qt.astype(BF16), kh.astype(BF16), (((1,), (1,)), ((), ())),
                                        preferred_element_type=F32)
                    oi = oi + jnp.dot(a.astype(BF16), vc[:i0], preferred_element_type=F32)
                o_rows.append(oi)
            b_end = bc[HG_CHUNK - 1:HG_CHUNK]
            kend = kc * jnp.exp(b_end - bc)
            vt = v[r0:r0 + HG_CHUNK].T.astype(BF16)
            st = st * jnp.exp(b_end) + jnp.dot(vt, kend.astype(BF16), preferred_element_type=F32)
        finish(h, o + jnp.concatenate(o_rows, axis=0), st)


def _hgrn(hq, hf, hi, hg, lb, ng, bsz, seq, ts):
    t, w = hq.shape
    dk = w // HG_HEADS
    n_s = seq // ts
    tile = lambda b, s: (b * n_s + s, 0)
    return pl.pallas_call(
        functools.partial(_hgrn_kernel, ts),
        out_shape=jax.ShapeDtypeStruct((t, w), BF16),
        grid=(bsz, n_s),
        in_specs=[pl.BlockSpec((ts, w), tile)] * 4
        + [pl.BlockSpec((1, w), lambda b, s: (0, 0))] * 2,
        out_specs=pl.BlockSpec((ts, w), tile),
        scratch_shapes=[pltpu.VMEM((HG_HEADS, dk, dk), F32), pltpu.VMEM((ts, w), F32)],
        compiler_params=_params(2),
        name="hgrn2",
    )(hq, hf, hi, hg, lb.reshape(1, w), ng.reshape(1, w))


def _attn_kernel(nk, nq, q_ref, kp_ref, kc_ref, vp_ref, vc_ref, o_ref, lse_ref):
    n = pl.program_id(2)
    e = ATT_HEAD_DIM
    i = lax.broadcasted_iota(I32, (nk, 2 * nk), 0)
    j = lax.broadcasted_iota(I32, (nk, 2 * nk), 1)
    band = (j >= i) & (j <= i + nk)
    kk = jnp.concatenate([kp_ref[0, 0], kc_ref[0, 0]], axis=0)
    vv = jnp.concatenate([vp_ref[0, 0], vc_ref[0, 0]], axis=0)
    first_head = lax.broadcasted_iota(I32, (nk, LANES), 1) < e
    zero = jnp.zeros((), q_ref.dtype)
    for b in range(nq):
        valid = band & ((j >= nk) | (n * nq + b > 0))
        rows = slice(b * nk, (b + 1) * nk)
        for c in range(0, ATT_HEADS_PER_GROUP * e, LANES):
            q = q_ref[0, 0, rows, c:c + LANES]
            kb = kk[b * nk:(b + 2) * nk, c:c + LANES]
            vb = vv[b * nk:(b + 2) * nk, c:c + LANES]
            outs, lses = [], []
            for keep in (first_head, jnp.logical_not(first_head)):
                s = lax.dot_general(jnp.where(keep, q, zero), kb, (((1,), (1,)), ((), ())),
                                    preferred_element_type=F32)
                s = jnp.where(valid, s, -jnp.inf)
                m = jnp.max(s, axis=-1, keepdims=True)
                p = jnp.exp(s - m)
                l = jnp.sum(p, axis=-1, keepdims=True)
                outs.append(jnp.dot(p.astype(BF16), vb, preferred_element_type=F32) / l)
                lses.append(m + jnp.log(l))
            o_ref[0, 0, rows, c:c + LANES] = jnp.where(first_head, outs[0], outs[1])
            lse_ref[0, 0, rows, c:c + LANES] = jnp.where(first_head, lses[0], lses[1])


def _attn_group(q, k, v, g, nq):
    window, dil = ATT_GROUPS[g]
    nk = window // dil
    bsz, _, ln, gw = q.shape
    nq = min(nq, ln // nk)
    assert ln % (nk * nq) == 0 and 2 * ATT_HEAD_DIM == LANES
    cur = pl.BlockSpec((1, 1, nq * nk, gw), lambda b, r, n: (b, r, n, 0))
    prev = pl.BlockSpec((1, 1, nk, gw), lambda b, r, n: (b, r, jnp.maximum(n * nq - 1, 0), 0))
    return pl.pallas_call(
        functools.partial(_attn_kernel, nk, nq),
        out_shape=[jax.ShapeDtypeStruct(q.shape, F32)] * 2,
        grid=(bsz, dil, ln // (nk * nq)),
        in_specs=[cur, prev, cur, prev, cur],
        out_specs=[cur, cur],
        compiler_params=_params(3),
        name=f"dilated_attn_g{g}",
    )(q, k, k, v, v)


def _token_major(ref, scr):
    dil, rows = ref.shape[1], ref.shape[2]
    if dil == 1:
        return ref[0, 0]
    n_col = scr.shape[0]
    for r in range(dil):
        for c in range(n_col):
            scr[c, pl.ds(r, rows, stride=dil), :] = ref[0, r, :, c * LANES:(c + 1) * LANES]
    return jnp.concatenate([scr[c] for c in range(n_col)], axis=1)


def _merge_kernel(ya_ref, o0_ref, o1_ref, o2_ref, l0_ref, l1_ref, l2_ref, ga_ref, gb_ref, x_ref,
                  g1_ref, sc2_ref, sh2_ref, g2_ref, n2_ref, wa_ref, wb_ref, wo_ref, wr_ref, wrl_ref,
                  wsg_ref, wsu_ref, wsd_ref, x1_ref, hp_ref, lg_ref, *scr):
    l0, l1, l2 = (_token_major(r, s) for r, s in zip((l0_ref, l1_ref, l2_ref), scr[:3]))
    o0, o1, o2 = (_token_major(r, s) for r, s in zip((o0_ref, o1_ref, o2_ref), scr[3:]))
    m = jnp.maximum(jnp.maximum(l0, l1), l2)
    e0, e1, e2 = jnp.exp(l0 - m), jnp.exp(l1 - m), jnp.exp(l2 - m)
    yb = (e0 * o0 + e1 * o1 + e2 * o2) / (e0 + e1 + e2)
    merged = (_sigmoid(ga_ref[...].astype(F32))
              * jnp.dot(ya_ref[...], wa_ref[...], preferred_element_type=F32)
              + _sigmoid(gb_ref[...].astype(F32))
              * jnp.dot(yb.astype(BF16), wb_ref[...], preferred_element_type=F32))
    x1 = x_ref[...] + g1_ref[0] * jnp.dot(merged.astype(BF16), wo_ref[...],
                                           preferred_element_type=F32)
    h2 = _rms(x1, n2_ref[...]) * (1.0 + sc2_ref[0]) + sh2_ref[0]
    hb = h2.astype(BF16)
    act = (_silu(jnp.dot(hb, wsg_ref[...], preferred_element_type=F32))
           * jnp.dot(hb, wsu_ref[...], preferred_element_type=F32))
    shared = jnp.dot(act.astype(BF16), wsd_ref[...], preferred_element_type=F32)
    x1_ref[...] = x1 + g2_ref[0] * shared
    hp_ref[...] = _pack_halves(h2)
    h_lo = (h2 - hb.astype(F32)).astype(BF16)
    nt = lambda a, b: lax.dot_general(a, b, (((1,), (1,)), ((), ())), preferred_element_type=F32)
    lg_ref[...] = nt(wr_ref[...], hb) + (nt(wr_ref[...], h_lo) + nt(wrl_ref[...], hb))


def _merge(ya, att, ga, gb, x2, gate1, scale2, shift2, gate2, norm2_g, wa, wb, wo, wr_t, wsg, wsu,
           wsd, seq, tm):
    t, d = x2.shape
    n_e = wr_t.shape[0]
    wr_hi = wr_t.astype(BF16)
    wr_lo = (wr_t - wr_hi.astype(F32)).astype(BF16)
    n_per = seq // tm
    per_b = lambda i: (i // n_per, 0, 0)
    rows = lambda wdt: pl.BlockSpec((tm, wdt), lambda i: (i, 0))
    full = lambda a: pl.BlockSpec(a.shape, lambda i: (0,) * a.ndim)
    vec = pl.BlockSpec((1, 1, d), per_b)
    (o0, l0), (o1, l1), (o2, l2) = att
    gw = o0.shape[3]
    by_residue = lambda a: pl.BlockSpec((1, a.shape[1], tm // a.shape[1], gw),
                                        lambda i: (i // n_per, 0, i % n_per, 0))
    att_in = (o0, o1, o2, l0, l1, l2)
    return pl.pallas_call(
        _merge_kernel,
        out_shape=[jax.ShapeDtypeStruct((t, d), F32),
                   jax.ShapeDtypeStruct((t, d // 2), U32),
                   jax.ShapeDtypeStruct((n_e, t), F32)],
        grid=(t // tm,),
        in_specs=[rows(ya.shape[1])] + [by_residue(a) for a in att_in] + [rows(d)] * 3
        + [vec, vec, vec, vec, pl.BlockSpec((1, d), lambda i: (0, 0))]
        + [full(a) for a in (wa, wb, wo, wr_hi, wr_lo, wsg, wsu, wsd)],
        out_specs=[rows(d), rows(d // 2), pl.BlockSpec((n_e, tm), lambda i: (0, i))],
        scratch_shapes=[pltpu.VMEM((gw // LANES, tm, LANES), F32)] * 6,
        compiler_params=_params(),
        name="merge_router",
    )(ya, *att_in, ga, gb, x2, gate1, scale2, shift2, gate2,
      norm2_g.reshape(1, d), wa, wb, wo, wr_hi, wr_lo, wsg, wsu, wsd)


def _topk_kernel(lg_ref, bias_ref, idx_ref, gate_ref, rank_ref, cnt_ref, carry_ref):
    n_e, tt = lg_ref.shape

    @pl.when(pl.program_id(0) == 0)
    def _():
        carry_ref[...] = jnp.zeros_like(carry_ref)

    scores = _sigmoid(lg_ref[...])
    sel = scores + bias_ref[...]
    eio = lax.broadcasted_iota(I32, (n_e, tt), 0)
    picked = jnp.zeros((n_e, tt), F32)
    idxs, vals = [], []
    for _ in range(TOP_K):
        m = jnp.max(sel, axis=0, keepdims=True)
        ik = jnp.min(jnp.where(sel == m, eio, n_e), axis=0, keepdims=True)
        hit = eio == ik
        vals.append(jnp.sum(jnp.where(hit, scores, 0.0), axis=0, keepdims=True))
        sel = jnp.where(hit, -jnp.inf, sel)
        picked = picked + jnp.where(hit, 1.0, 0.0)
        idxs.append(ik)
    denom = vals[0]
    for v in vals[1:]:
        denom = denom + v
    gate_ref[...] = jnp.concatenate([v / denom * ROUTE_SCALE for v in vals], axis=0)
    idx_ref[...] = jnp.concatenate(idxs, axis=0)

    upper = (lax.broadcasted_iota(I32, (tt, tt), 0) <= lax.broadcasted_iota(I32, (tt, tt), 1))
    incl = jnp.dot(picked.astype(BF16), jnp.where(upper, 1.0, 0.0).astype(BF16),
                   preferred_element_type=F32)
    before = incl - picked + carry_ref[...]
    rank_ref[...] = jnp.concatenate(
        [jnp.sum(jnp.where(eio == ik, before, 0.0), axis=0, keepdims=True) for ik in idxs],
        axis=0).astype(I32)
    carry_ref[...] = carry_ref[...] + jnp.sum(picked, axis=1, keepdims=True)
    cnt_ref[...] = jnp.broadcast_to(carry_ref[...], cnt_ref.shape).astype(I32)


def _topk(logits_t, bias, tt):
    n_e, t = logits_t.shape
    tok = pl.BlockSpec((TOP_K, tt), lambda i: (0, i))
    return pl.pallas_call(
        _topk_kernel,
        out_shape=[jax.ShapeDtypeStruct((TOP_K, t), I32), jax.ShapeDtypeStruct((TOP_K, t), F32),
                   jax.ShapeDtypeStruct((TOP_K, t), I32), jax.ShapeDtypeStruct((n_e, 128), I32)],
        grid=(t // tt,),
        in_specs=[pl.BlockSpec((n_e, tt), lambda i: (0, i)),
                  pl.BlockSpec((n_e, 1), lambda i: (0, 0))],
        out_specs=[tok, tok, tok, pl.BlockSpec((n_e, 128), lambda i: (0, 0))],
        scratch_shapes=[pltpu.VMEM((n_e, 1), F32)],
        compiler_params=_params(),
        name="router_topk",
    )(logits_t, bias.reshape(n_e, 1))


def _dest_kernel(idx_ref, rank_ref, start_ref, o_ref):
    k, tt = idx_ref.shape
    n_e = start_ref.shape[0]
    eio = lax.broadcasted_iota(I32, (n_e, tt), 0)
    start = start_ref[...]
    rows = [jnp.sum(jnp.where(eio == idx_ref[r:r + 1, :], start, 0), axis=0, keepdims=True)
            for r in range(k)]
    o_ref[...] = jnp.concatenate(rows, axis=0) + rank_ref[...]


def _dest(idx, rank, seg_start, tt):
    k, t = idx.shape
    n_e = seg_start.shape[0]
    tok = pl.BlockSpec((k, tt), lambda i: (0, i))
    return pl.pallas_call(
        _dest_kernel,
        out_shape=jax.ShapeDtypeStruct((k, t), I32),
        grid=(t // tt,),
        in_specs=[tok, tok, pl.BlockSpec((n_e, 1), lambda i: (0, 0))],
        out_specs=tok,
        compiler_params=_params(),
        name="moe_dest",
    )(idx, rank, seg_start.reshape(n_e, 1))


def _sc_mesh():
    return plsc.VectorSubcoreMesh(core_axis_name="core", subcore_axis_name="subcore")


def _sc_scatter_rows(rows, dest, n_out):
    k, t = dest.shape
    w = rows.shape[1]
    mesh = _sc_mesh()
    n_workers = mesh.num_cores * mesh.num_subcores
    win_per_worker = t // (SC_WINDOW * n_workers)
    assert win_per_worker * SC_WINDOW * n_workers == t

    @functools.partial(
        pl.kernel, out_type=jax.ShapeDtypeStruct((n_out, w), rows.dtype), mesh=mesh,
        scratch_types=[pltpu.VMEM((SC_WINDOW, w), rows.dtype)]
        + [pltpu.VMEM((1, SC_WINDOW), I32)] * k + [pltpu.SemaphoreType.DMA],
        name="moe_dispatch_sc")
    def run(rows_hbm, idx_hbm, out_hbm, rows_v, *rest):
        idx_v, sem = rest[:k], rest[k]
        worker = lax.axis_index("subcore") * mesh.num_cores + lax.axis_index("core")

        @pl.loop(0, win_per_worker)
        def _(j):
            t0 = pl.multiple_of((worker * win_per_worker + j) * SC_WINDOW, SC_WINDOW)
            pltpu.sync_copy(rows_hbm.at[pl.ds(t0, SC_WINDOW)], rows_v)
            for r in range(k):
                pltpu.sync_copy(idx_hbm.at[:, pl.ds(r * t + t0, SC_WINDOW)], idx_v[r])
            copies = [pltpu.async_copy(rows_v, out_hbm.at[idx_v[r].at[0]], sem) for r in range(k)]
            for c in copies:
                c.wait()

    return run(rows, dest.reshape(1, k * t))


def _sc_gather_rows(table, dest):
    k, t = dest.shape
    w = table.shape[1]
    mesh = _sc_mesh()
    n_workers = mesh.num_cores * mesh.num_subcores
    win_per_worker = (k * t) // (SC_WINDOW * n_workers)
    assert win_per_worker * SC_WINDOW * n_workers == k * t

    @functools.partial(
        pl.kernel, out_type=jax.ShapeDtypeStruct((k * t, w), table.dtype), mesh=mesh,
        scratch_types=[pltpu.VMEM((SC_WINDOW, w), table.dtype), pltpu.VMEM((1, SC_WINDOW), I32)],
        name="moe_gather_sc")
    def run(table_hbm, idx_hbm, out_hbm, rows_v, idx_v):
        worker = lax.axis_index("subcore") * mesh.num_cores + lax.axis_index("core")

        @pl.loop(0, win_per_worker)
        def _(j):
            p0 = pl.multiple_of((worker * win_per_worker + j) * SC_WINDOW, SC_WINDOW)
            pltpu.sync_copy(idx_hbm.at[:, pl.ds(p0, SC_WINDOW)], idx_v)
            pltpu.sync_copy(table_hbm.at[idx_v.at[0]], rows_v)
            pltpu.sync_copy(rows_v, out_hbm.at[pl.ds(p0, SC_WINDOW)])

    return run(table, dest.reshape(1, k * t))


def _expert_kernel(start_ref, nblk_ref, xs_ref, wg_ref, wu_ref, wd_ref, ys_ref,
                   xbuf, ybuf, wgb, wub, wdb, sem_in, sem_out):
    e = pl.program_id(0)
    n_e = pl.num_programs(0)
    nb = nblk_ref[e]
    g0 = start_ref[e] // MOE_BLOCK
    n_used = start_ref[n_e - 1] // MOE_BLOCK + nblk_ref[n_e - 1]
    n_in, n_out = xbuf.shape[0], ybuf.shape[0]

    def rows(g):
        return pl.ds(pl.multiple_of(g * MOE_BLOCK, MOE_BLOCK), MOE_BLOCK)

    def in_copy(g):
        slot = lax.rem(g, n_in)
        return pltpu.make_async_copy(xs_ref.at[rows(g), :], xbuf.at[slot], sem_in.at[slot])

    def out_copy(g):
        slot = lax.rem(g, n_out)
        return pltpu.make_async_copy(ybuf.at[slot], ys_ref.at[rows(g), :], sem_out.at[slot])

    look = n_in - EXPERT_GROUP

    @pl.when(e == 0)
    def _():
        for g in range(look):
            @pl.when(g < n_used)
            def _():
                in_copy(g).start(priority=BLOCK_DMA_PRIORITY)

    @pl.when(nb > 0)
    def _():
        wgb[...] = wg_ref[0].astype(BF16)
        wub[...] = wu_ref[0].astype(BF16)
        wdb[...] = wd_ref[0].astype(BF16)
        half = xbuf.shape[2]

        def swiglu(word):
            lo, hi = _unpack_halves(word)
            lo, hi = lo.astype(BF16), hi.astype(BF16)
            gate = (jnp.dot(lo, wgb[:half], preferred_element_type=F32)
                    + jnp.dot(hi, wgb[half:], preferred_element_type=F32))
            up = (jnp.dot(lo, wub[:half], preferred_element_type=F32)
                  + jnp.dot(hi, wub[half:], preferred_element_type=F32))
            act = (_silu(gate) * up).astype(BF16)
            return jnp.dot(act, wdb[...], preferred_element_type=F32)

        def process(g, m):
            for i in range(m):
                in_copy(g + i).wait()
            for i in range(m):
                @pl.when(g + look + i < n_used)
                def _():
                    in_copy(g + look + i).start(priority=BLOCK_DMA_PRIORITY)
            ys = [swiglu(xbuf[lax.rem(g + i, n_in)]) for i in range(m)]
            for i in range(m):
                @pl.when(g + i >= n_out)
                def _():
                    out_copy(g + i - n_out).wait()

                ybuf[lax.rem(g + i, n_out)] = _pack_halves(ys[i])
                out_copy(g + i).start(priority=BLOCK_DMA_PRIORITY)

        def group_body(p, carry):
            process(g0 + p * EXPERT_GROUP, EXPERT_GROUP)
            return carry

        lax.fori_loop(0, nb // EXPERT_GROUP, group_body, 0)
        for m in range(1, EXPERT_GROUP):
            @pl.when(lax.rem(nb, EXPERT_GROUP) == m)
            def _():
                process(g0 + nb - m, m)

    @pl.when(e == n_e - 1)
    def _():
        for i in range(n_out):
            @pl.when(n_used - 1 - i >= 0)
            def _():
                out_copy(n_used - 1 - i).wait()


def _experts(seg_start, seg_blocks, xs, wg, wu, wd):
    n_slots, half = xs.shape
    n_e, d, de = wg.shape
    return pl.pallas_call(
        _expert_kernel,
        out_shape=jax.ShapeDtypeStruct((n_slots, half), U32),
        grid_spec=pltpu.PrefetchScalarGridSpec(
            num_scalar_prefetch=2,
            grid=(n_e,),
            in_specs=[pl.BlockSpec(memory_space=pl.ANY),
                      pl.BlockSpec((1, d, de), lambda e, s, n: (e, 0, 0)),
                      pl.BlockSpec((1, d, de), lambda e, s, n: (e, 0, 0)),
                      pl.BlockSpec((1, de, d), lambda e, s, n: (e, 0, 0))],
            out_specs=pl.BlockSpec(memory_space=pl.ANY),
            scratch_shapes=[pltpu.VMEM((EXPERT_IN_RING, MOE_BLOCK, half), U32),
                            pltpu.VMEM((EXPERT_OUT_RING, MOE_BLOCK, half), U32),
                            pltpu.VMEM((d, de), BF16), pltpu.VMEM((d, de), BF16),
                            pltpu.VMEM((de, d), BF16),
                            pltpu.SemaphoreType.DMA((EXPERT_IN_RING,)),
                            pltpu.SemaphoreType.DMA((EXPERT_OUT_RING,))]),
        compiler_params=_params(),
        name="moe_experts",
    )(seg_start, seg_blocks, xs, wg, wu, wd)


def _combine_kernel(yg_ref, gt_ref, x_ref, g2_ref, fg_ref, o_ref):
    k = yg_ref.shape[0]
    gt = gt_ref[...]
    lo, hi = _unpack_halves(yg_ref[0])
    y_lo, y_hi = lo * gt[:, 0:1], hi * gt[:, 0:1]
    for r in range(1, k):
        lo, hi = _unpack_halves(yg_ref[r])
        y_lo, y_hi = y_lo + lo * gt[:, r:r + 1], y_hi + hi * gt[:, r:r + 1]
    y = jnp.concatenate([y_lo, y_hi], axis=1)
    o_ref[...] = _rms(x_ref[...] + g2_ref[0] * y, fg_ref[...])


def _combine_into_kernel(yg_ref, gt_ref, x_ref, g2_ref, fg_ref, prev_ref, o_ref):
    del prev_ref
    _combine_kernel(yg_ref, gt_ref, x_ref, g2_ref, fg_ref, o_ref)


def _combine(yg, tok0, gates_t, x1s, gate2, final_g, seq, tc, out_so_far=None):
    k, n, half = yg.shape
    t, d = x1s.shape
    b0 = tok0 // tc
    args = [yg, gates_t, x1s, gate2, final_g.reshape(1, d)]
    in_specs = [pl.BlockSpec((k, tc, half), lambda i: (0, i, 0)),
                pl.BlockSpec((tc, k), lambda i: (i + b0, 0)),
                pl.BlockSpec((tc, d), lambda i: (i + b0, 0)),
                pl.BlockSpec((1, 1, d), lambda i: (((i + b0) * tc) // seq, 0, 0)),
                pl.BlockSpec((1, d), lambda i: (0, 0))]
    aliases = {}
    kernel = _combine_kernel
    if out_so_far is not None:
        args.append(out_so_far)
        in_specs.append(pl.BlockSpec(memory_space=pl.ANY))
        aliases = {len(args) - 1: 0}
        kernel = _combine_into_kernel
    return pl.pallas_call(
        kernel,
        out_shape=jax.ShapeDtypeStruct((t, d), F32),
        grid=(n // tc,),
        in_specs=in_specs,
        out_specs=pl.BlockSpec((tc, d), lambda i: (i + b0, 0)),
        input_output_aliases=aliases,
        compiler_params=_params(),
        name="moe_combine",
    )(*args)


def _layer(x2, c, bsz, seq, lb_row, ada_w, ada_b, norm1_g, w_in, hg_norm_g, w_branch_a, w_branch_b,
           w_out, norm2_g, w_router, router_bias, w_exp_gate, w_exp_up, w_exp_down, w_sh_gate,
           w_sh_up, w_sh_down, final_g):
    t, d = x2.shape
    n_e = w_router.shape[1]
    mod = _ada(c, ada_w, ada_b).reshape(bsz, 6, 1, d)
    shift1, scale1, gate1, shift2, scale2, gate2 = (mod[:, j] for j in range(6))

    hw = hg_norm_g.shape[0]
    aw = len(ATT_GROUPS) * ATT_HEADS_PER_GROUP * ATT_HEAD_DIM
    flat_segs = [(0, hw, BF16), (hw, hw, F32), (2 * hw, hw, BF16), (3 * hw, hw, BF16),
                 (4 * hw + 3 * aw, d, BF16), (4 * hw + 3 * aw + d, d, BF16)]
    (hq, hf, hi, hg, ga, gb), qkv = _inproj(
        x2, norm1_g, scale1, shift1, w_in.astype(BF16), bsz, seq, flat_segs, 4 * hw, tm=512)

    ya = _hgrn(hq, hf, hi, hg, lb_row, hg_norm_g, bsz, seq, ts=256)
    att = [_attn_group(*qkv[3 * g:3 * g + 3], g, nq=4) for g in range(len(ATT_GROUPS))]

    x1s, hp, logits_t = _merge(
        ya, att, ga, gb, x2, gate1, scale2, shift2, gate2, norm2_g, w_branch_a.astype(BF16),
        w_branch_b.astype(BF16), w_out.astype(BF16), w_router.T, w_sh_gate.astype(BF16),
        w_sh_up.astype(BF16), w_sh_down.astype(BF16), seq, tm=512)

    idx, gates, rank, cnt = _topk(logits_t, router_bias, tt=512)
    counts = cnt[:, 0]
    padded = (counts + MOE_BLOCK - 1) // MOE_BLOCK * MOE_BLOCK
    seg_start = (jnp.cumsum(padded) - padded).astype(I32)
    n_blocks = -(-(t * TOP_K) // MOE_BLOCK) + n_e
    dest = _dest(idx, rank, seg_start, tt=512)

    xs = _sc_scatter_rows(hp, dest, n_blocks * MOE_BLOCK)
    ys = _experts(seg_start, (padded // MOE_BLOCK).astype(I32), xs, w_exp_gate, w_exp_up,
                  w_exp_down)
    out, n = None, t // COMBINE_PARTS
    for part in range(COMBINE_PARTS):
        yg = _sc_gather_rows(ys, dest[:, part * n:(part + 1) * n]).reshape(TOP_K, n, d // 2)
        out = _combine(yg, part * n, gates.T, x1s, gate2, final_g, seq, tc=256, out_so_far=out)
    return out


def kernel(x, c, ada_w, ada_b, norm1_g, w_in, lb_logits, hg_norm_g, w_branch_a, w_branch_b, w_out,
           norm2_g, w_router, router_bias, w_exp_gate, w_exp_up, w_exp_down, w_sh_gate, w_sh_up,
           w_sh_down, final_g):
    bsz, seq, d = x.shape
    depth = ada_w.shape[0]
    assert depth == 1, "the last layer's kernels also apply the final norm"
    lb_table = jnp.cumsum(jax.nn.softmax(lb_logits.astype(F32), axis=0), axis=0)
    out = _layer(x.reshape(bsz * seq, d), c, bsz, seq, lb_table[0], ada_w[0], ada_b[0], norm1_g[0],
                 w_in[0], hg_norm_g[0], w_branch_a[0], w_branch_b[0], w_out[0], norm2_g[0],
                 w_router[0], router_bias[0], w_exp_gate[0], w_exp_up[0], w_exp_down[0],
                 w_sh_gate[0], w_sh_up[0], w_sh_down[0], final_g)
    return out.reshape(bsz, seq, d)
```

```python
import functools

import jax
import jax.numpy as jnp
from jax import lax
from jax.experimental import pallas as pl
from jax.experimental.pallas import tpu as pltpu
from jax.experimental.pallas import tpu_sc as plsc

F32 = jnp.float32
BF16 = jnp.bfloat16
I32 = jnp.int32
U32 = jnp.uint32
HIGHEST = lax.Precision.HIGHEST

HG_HEADS = 4
HG_BLOCK = 16
HG_CHUNK = 32
HG_MILD_DECAY = -80.0
ATT_GROUPS = ((128, 1), (512, 4), (2048, 16))
ATT_HEADS_PER_GROUP = 4
ATT_HEAD_DIM = 64
TOP_K = 8
ROUTE_SCALE = 2.5
MOE_BLOCK = 256
RMS_EPS = 1e-6
N_DMA_QUEUES = 2
SC_WINDOW = 128
COMBINE_PARTS = 2
EXPERT_GROUP = 4
EXPERT_IN_RING = 8
EXPERT_OUT_RING = 6

LANES = 128
VMEM_LIMIT_BYTES = 56 * 1024 * 1024


def _sigmoid(x):
    return 1.0 / (1.0 + jnp.exp(-x))


def _silu(x):
    return x * _sigmoid(x)


def _rms(x, g):
    return x * lax.rsqrt(jnp.mean(x * x, axis=-1, keepdims=True) + RMS_EPS) * g


def _pack_halves(x):
    n = x.shape[1] // 2
    bits = lax.bitcast_convert_type(x.astype(BF16).astype(F32), U32)
    return (bits[:, :n] >> 16) | (bits[:, n:] & jnp.uint32(0xFFFF0000))


def _unpack_halves(word):
    lo = lax.bitcast_convert_type(word << 16, F32)
    hi = lax.bitcast_convert_type(word & jnp.uint32(0xFFFF0000), F32)
    return lo, hi


def _params(n_axes=1):
    return pltpu.CompilerParams(
        dimension_semantics=("arbitrary",) * n_axes, vmem_limit_bytes=VMEM_LIMIT_BYTES)


def _ada_kernel(c_ref, w_ref, b_ref, o_ref):
    sc = _silu(c_ref[...])
    o_ref[...] = jnp.dot(sc, w_ref[...], preferred_element_type=F32, precision=HIGHEST) + b_ref[...]


def _ada(c, w, b):
    bsz, d = c.shape
    n = w.shape[1]
    return pl.pallas_call(
        _ada_kernel,
        out_shape=jax.ShapeDtypeStruct((bsz, n), F32),
        grid=(n // d,),
        in_specs=[pl.BlockSpec((bsz, d), lambda j: (0, 0)),
                  pl.BlockSpec((d, d), lambda j: (0, j)),
                  pl.BlockSpec((1, d), lambda j: (0, j))],
        out_specs=pl.BlockSpec((bsz, d), lambda j: (0, j)),
        compiler_params=_params(),
        name="ada_mod",
    )(c, w, b.reshape(1, n))


def _inproj_kernel(n_flat, flat_ranges, att_c0, x_ref, g_ref, sc_ref, sh_ref, w_ref, *refs):
    flat_refs, att_refs, scr = refs[:n_flat], refs[n_flat:-1], refs[-1]
    tm = x_ref.shape[0]
    h = _rms(x_ref[...], g_ref[...]) * (1.0 + sc_ref[0]) + sh_ref[0]
    hb = h.astype(BF16)
    for (c0, c1), o_ref in zip(flat_ranges, flat_refs):
        o_ref[...] = jnp.dot(hb, w_ref[:, c0:c1], preferred_element_type=F32).astype(o_ref.dtype)
    gw = ATT_HEADS_PER_GROUP * ATT_HEAD_DIM
    n_groups = len(ATT_GROUPS)
    for part in range(3):
        c0 = att_c0 + part * n_groups * gw
        res = jnp.dot(hb, w_ref[:, c0:c0 + n_groups * gw], preferred_element_type=F32)
        if part == 0:
            res = res * (ATT_HEAD_DIM ** -0.5)
        for g, (_, dil) in enumerate(ATT_GROUPS):
            o_ref = att_refs[g * 3 + part]
            sub = res[:, g * gw:(g + 1) * gw]
            if dil == 1:
                o_ref[0, 0] = sub.astype(BF16)
            else:
                for c in range(gw // LANES):
                    scr[c] = sub[:, c * LANES:(c + 1) * LANES]
                for r in range(dil):
                    o_ref[0, r] = jnp.concatenate(
                        [scr[c, pl.ds(r, tm // dil, stride=dil), :] for c in range(gw // LANES)],
                        axis=1).astype(BF16)


def _inproj(x2, g, scale, shift, w_bf16, bsz, seq, flat_segs, att_c0, tm):
    t, d = x2.shape
    gw = ATT_HEADS_PER_GROUP * ATT_HEAD_DIM
    n_per = seq // tm
    per_b = lambda i: (i // n_per, 0, 0)
    att_shapes, att_specs = [], []
    for _, dil in ATT_GROUPS:
        for _ in range(3):
            att_shapes.append(jax.ShapeDtypeStruct((bsz, dil, seq // dil, gw), BF16))
            att_specs.append(pl.BlockSpec((1, dil, tm // dil, gw),
                                          lambda i: (i // n_per, 0, i % n_per, 0)))
    outs = pl.pallas_call(
        functools.partial(_inproj_kernel, len(flat_segs),
                          tuple((c0, c0 + wdt) for c0, wdt, _ in flat_segs), att_c0),
        out_shape=[jax.ShapeDtypeStruct((t, wdt), dt) for _, wdt, dt in flat_segs] + att_shapes,
        grid=(t // tm,),
        in_specs=[pl.BlockSpec((tm, d), lambda i: (i, 0)),
                  pl.BlockSpec((1, d), lambda i: (0, 0)),
                  pl.BlockSpec((1, 1, d), per_b),
                  pl.BlockSpec((1, 1, d), per_b),
                  pl.BlockSpec(w_bf16.shape, lambda i: (0, 0))],
        out_specs=[pl.BlockSpec((tm, wdt), lambda i: (i, 0)) for _, wdt, _ in flat_segs]
        + att_specs,
        scratch_shapes=[pltpu.VMEM((gw // LANES, tm, LANES), F32)],
        compiler_params=_params(),
        name="in_proj",
    )(x2, g.reshape(1, d), scale, shift, w_bf16)
    return outs[:len(flat_segs)], outs[len(flat_segs):]


def _hgrn_kernel(ts, q_ref, f_ref, v_ref, gt_ref, lb_ref, ng_ref, o_ref, st_ref, b_ref):
    dk = q_ref.shape[1] // HG_HEADS
    n_chunks = ts // HG_CHUNK
    n_blk = HG_CHUNK // HG_BLOCK

    @pl.when(pl.program_id(1) == 0)
    def _():
        st_ref[...] = jnp.zeros_like(st_ref)

    row = lax.broadcasted_iota(I32, (LANES, LANES), 0)
    col = lax.broadcasted_iota(I32, (LANES, LANES), 1)
    same_chunk = (row // HG_CHUNK) == (col // HG_CHUNK)
    cum_mat = jnp.where(same_chunk & (col <= row), 1.0, 0.0).astype(BF16)

    def chunk_cumsum(x):
        out = []
        for r0 in range(0, ts, LANES):
            rest = x[r0:r0 + LANES]
            acc = None
            for _ in range(3):
                term = rest.astype(BF16)
                part = jnp.dot(cum_mat, term, preferred_element_type=F32)
                acc = part if acc is None else acc + part
                rest = rest - term.astype(F32)
            out.append(acc)
        return jnp.concatenate(out, axis=0)

    def forget(cs):
        lb = lb_ref[:, cs]
        return lb + (1.0 - lb) * _sigmoid(f_ref[:, cs])

    b_min = None
    for h in range(HG_HEADS):
        cs = slice(h * dk, (h + 1) * dk)
        b = chunk_cumsum(jnp.log(forget(cs)))
        b_ref[:, cs] = b
        m = jnp.min(b)
        b_min = m if b_min is None else jnp.minimum(b_min, m)
    mild = b_min >= HG_MILD_DECAY

    def finish(h, o, st):
        cs = slice(h * dk, (h + 1) * dk)
        st_ref[h] = st
        y = _rms(o, ng_ref[:, cs]) * _silu(gt_ref[:, cs].astype(F32))
        o_ref[:, cs] = y.astype(o_ref.dtype)

    @pl.when(mild)
    def _():
        span = 2 * HG_CHUNK
        causal = (lax.broadcasted_iota(I32, (span, span), 0)
                  >= lax.broadcasted_iota(I32, (span, span), 1))
        nt = lambda x, y: lax.dot_general(x, y, (((1,), (1,)), ((), ())),
                                          preferred_element_type=F32)
        for h in range(HG_HEADS):
            cs = slice(h * dk, (h + 1) * dk)
            v = v_ref[:, cs]
            b = b_ref[:, cs]
            q = q_ref[:, cs].astype(F32)
            k = 1.0 - forget(cs)
            st = st_ref[h]
            o_rows = []
            for r0 in range(0, ts, span):
                sl = slice(r0, r0 + span)
                b_first, b_second = b[r0:r0 + HG_CHUNK], b[r0 + HG_CHUNK:r0 + span]
                end_first = b_first[HG_CHUNK - 1:HG_CHUNK]
                end_second = b_second[HG_CHUNK - 1:HG_CHUNK]
                e = jnp.exp(jnp.concatenate([b_first - end_first, b_second], axis=0))
                qe = (q[sl] * e).astype(BF16)
                ke = k[sl] / e
                a = jnp.where(causal, nt(qe, ke.astype(BF16)), 0.0).astype(BF16)
                st_in = (st * jnp.exp(end_first)).astype(BF16)
                o_rows.append(jnp.dot(a, v[sl], preferred_element_type=F32) + nt(qe, st_in))
                kend = (ke * jnp.exp(end_second)).astype(BF16)
                vt = v[sl].astype(F32).T.astype(BF16)
                st = (st * jnp.exp(end_first + end_second)
                      + jnp.dot(vt, kend, preferred_element_type=F32))
            finish(h, jnp.concatenate(o_rows, axis=0), st)

    @pl.when(jnp.logical_not(mild))
    def _():
        _hgrn_steep(ts, dk, n_chunks, n_blk, q_ref, v_ref, b_ref, st_ref, forget, finish)


def _hgrn_steep(ts, dk, n_chunks, n_blk, q_ref, v_ref, b_ref, st_ref, forget, finish):
    t_in_blk = lax.broadcasted_iota(I32, (ts, dk), 0) % HG_BLOCK

    for h in range(HG_HEADS):
        cs = slice(h * dk, (h + 1) * dk)
        q = q_ref[:, cs].astype(F32)
        v = v_ref[:, cs].astype(F32)
        k = 1.0 - forget(cs)
        b = b_ref[:, cs]

        o = jnp.sum(q * k, axis=-1, keepdims=True) * v
        for d in range(1, HG_BLOCK):
            k_d = pltpu.roll(k, d, axis=0)
            b_d = pltpu.roll(b, d, axis=0)
            v_d = pltpu.roll(v, d, axis=0)
            w = jnp.sum(q * k_d * jnp.exp(jnp.minimum(b - b_d, 0.0)), axis=-1, keepdims=True)
            o = o + jnp.where(t_in_blk >= d, w * v_d, 0.0)

        st = st_ref[h]
        o_rows = []
        for c in range(n_chunks):
            r0 = c * HG_CHUNK
            bc = b[r0:r0 + HG_CHUNK]
            qc = q[r0:r0 + HG_CHUNK]
            kc = k[r0:r0 + HG_CHUNK]
            vc = v[r0:r0 + HG_CHUNK].astype(BF16)
            st_b = st.astype(BF16)
            for i in range(n_blk):
                i0 = i * HG_BLOCK
                if i == 0:
                    qt = qc[:HG_BLOCK] * jnp.exp(bc[:HG_BLOCK])
                    qs = qt
                else:
                    ref_row = bc[i0 - 1:i0]
                    qt = qc[i0:i0 + HG_BLOCK] * jnp.exp(bc[i0:i0 + HG_BLOCK] - ref_row)
                    qs = qt * jnp.exp(ref_row)
                oi = lax.dot_general(qs.astype(BF16), st_b, (((1,), (1,)), ((), ())),
                                     preferred_element_type=F32)
                if i > 0:
                    kh = kc[:i0] * jnp.exp(ref_row - bc[:i0])
                    a = lax.dot_general(qt.astype(BF16), kh.astype(BF16), (((1,), (1,)), ((), ())),
                                        preferred_element_type=F32)
                    oi = oi + jnp.dot(a.astype(BF16), vc[:i0], preferred_element_type=F32)
                o_rows.append(oi)
            b_end = bc[HG_CHUNK - 1:HG_CHUNK]
            kend = kc * jnp.exp(b_end - bc)
            vt = v[r0:r0 + HG_CHUNK].T.astype(BF16)
            st = st * jnp.exp(b_end) + jnp.dot(vt, kend.astype(BF16), preferred_element_type=F32)
        finish(h, o + jnp.concatenate(o_rows, axis=0), st)


def _hgrn(hq, hf, hi, hg, lb, ng, bsz, seq, ts):
    t, w = hq.shape
    dk = w // HG_HEADS
    n_s = seq // ts
    tile = lambda b, s: (b * n_s + s, 0)
    return pl.pallas_call(
        functools.partial(_hgrn_kernel, ts),
        out_shape=jax.ShapeDtypeStruct((t, w), BF16),
        grid=(bsz, n_s),
        in_specs=[pl.BlockSpec((ts, w), tile)] * 4
        + [pl.BlockSpec((1, w), lambda b, s: (0, 0))] * 2,
        out_specs=pl.BlockSpec((ts, w), tile),
        scratch_shapes=[pltpu.VMEM((HG_HEADS, dk, dk), F32), pltpu.VMEM((ts, w), F32)],
        compiler_params=_params(2),
        name="hgrn2",
    )(hq, hf, hi, hg, lb.reshape(1, w), ng.reshape(1, w))


def _attn_kernel(nk, nq, q_ref, kp_ref, kc_ref, vp_ref, vc_ref, o_ref, lse_ref):
    n = pl.program_id(2)
    e = ATT_HEAD_DIM
    i = lax.broadcasted_iota(I32, (nk, 2 * nk), 0)
    j = lax.broadcasted_iota(I32, (nk, 2 * nk), 1)
    band = (j >= i) & (j <= i + nk)
    kk = jnp.concatenate([kp_ref[0, 0], kc_ref[0, 0]], axis=0)
    vv = jnp.concatenate([vp_ref[0, 0], vc_ref[0, 0]], axis=0)
    first_head = lax.broadcasted_iota(I32, (nk, LANES), 1) < e
    zero = jnp.zeros((), q_ref.dtype)
    for b in range(nq):
        valid = band & ((j >= nk) | (n * nq + b > 0))
        rows = slice(b * nk, (b + 1) * nk)
        for c in range(0, ATT_HEADS_PER_GROUP * e, LANES):
            q = q_ref[0, 0, rows, c:c + LANES]
            kb = kk[b * nk:(b + 2) * nk, c:c + LANES]
            vb = vv[b * nk:(b + 2) * nk, c:c + LANES]
            outs, lses = [], []
            for keep in (first_head, jnp.logical_not(first_head)):
                s = lax.dot_general(jnp.where(keep, q, zero), kb, (((1,), (1,)), ((), ())),
                                    preferred_element_type=F32)
                s = jnp.where(valid, s, -jnp.inf)
                m = jnp.max(s, axis=-1, keepdims=True)
                p = jnp.exp(s - m)
                l = jnp.sum(p, axis=-1, keepdims=True)
                outs.append(jnp.dot(p.astype(BF16), vb, preferred_element_type=F32) / l)
                lses.append(m + jnp.log(l))
            o_ref[0, 0, rows, c:c + LANES] = jnp.where(first_head, outs[0], outs[1])
            lse_ref[0, 0, rows, c:c + LANES] = jnp.where(first_head, lses[0], lses[1])


def _attn_group(q, k, v, g, nq):
    window, dil = ATT_GROUPS[g]
    nk = window // dil
    bsz, _, ln, gw = q.shape
    nq = min(nq, ln // nk)
    assert ln % (nk * nq) == 0 and 2 * ATT_HEAD_DIM == LANES
    cur = pl.BlockSpec((1, 1, nq * nk, gw), lambda b, r, n: (b, r, n, 0))
    prev = pl.BlockSpec((1, 1, nk, gw), lambda b, r, n: (b, r, jnp.maximum(n * nq - 1, 0), 0))
    return pl.pallas_call(
        functools.partial(_attn_kernel, nk, nq),
        out_shape=[jax.ShapeDtypeStruct(q.shape, F32)] * 2,
        grid=(bsz, dil, ln // (nk * nq)),
        in_specs=[cur, prev, cur, prev, cur],
        out_specs=[cur, cur],
        compiler_params=_params(3),
        name=f"dilated_attn_g{g}",
    )(q, k, k, v, v)


def _token_major(ref, scr):
    dil, rows = ref.shape[1], ref.shape[2]
    if dil == 1:
        return ref[0, 0]
    n_col = scr.shape[0]
    for r in range(dil):
        for c in range(n_col):
            scr[c, pl.ds(r, rows, stride=dil), :] = ref[0, r, :, c * LANES:(c + 1) * LANES]
    return jnp.concatenate([scr[c] for c in range(n_col)], axis=1)


def _merge_kernel(ya_ref, o0_ref, o1_ref, o2_ref, l0_ref, l1_ref, l2_ref, ga_ref, gb_ref, x_ref,
                  g1_ref, sc2_ref, sh2_ref, g2_ref, n2_ref, wa_ref, wb_ref, wo_ref, wr_ref, wrl_ref,
                  wsg_ref, wsu_ref, wsd_ref, x1_ref, hp_ref, lg_ref, *scr):
    l0, l1, l2 = (_token_major(r, s) for r, s in zip((l0_ref, l1_ref, l2_ref), scr[:3]))
    o0, o1, o2 = (_token_major(r, s) for r, s in zip((o0_ref, o1_ref, o2_ref), scr[3:]))
    m = jnp.maximum(jnp.maximum(l0, l1), l2)
    e0, e1, e2 = jnp.exp(l0 - m), jnp.exp(l1 - m), jnp.exp(l2 - m)
    yb = (e0 * o0 + e1 * o1 + e2 * o2) / (e0 + e1 + e2)
    merged = (_sigmoid(ga_ref[...].astype(F32))
              * jnp.dot(ya_ref[...], wa_ref[...], preferred_element_type=F32)
              + _sigmoid(gb_ref[...].astype(F32))
              * jnp.dot(yb.astype(BF16), wb_ref[...], preferred_element_type=F32))
    x1 = x_ref[...] + g1_ref[0] * jnp.dot(merged.astype(BF16), wo_ref[...],
                                           preferred_element_type=F32)
    h2 = _rms(x1, n2_ref[...]) * (1.0 + sc2_ref[0]) + sh2_ref[0]
    hb = h2.astype(BF16)
    act = (_silu(jnp.dot(hb, wsg_ref[...], preferred_element_type=F32))
           * jnp.dot(hb, wsu_ref[...], preferred_element_type=F32))
    shared = jnp.dot(act.astype(BF16), wsd_ref[...], preferred_element_type=F32)
    x1_ref[...] = x1 + g2_ref[0] * shared
    hp_ref[...] = _pack_halves(h2)
    h_lo = (h2 - hb.astype(F32)).astype(BF16)
    nt = lambda a, b: lax.dot_general(a, b, (((1,), (1,)), ((), ())), preferred_element_type=F32)
    lg_ref[...] = nt(wr_ref[...], hb) + (nt(wr_ref[...], h_lo) + nt(wrl_ref[...], hb))


def _merge(ya, att, ga, gb, x2, gate1, scale2, shift2, gate2, norm2_g, wa, wb, wo, wr_t, wsg, wsu,
           wsd, seq, tm):
    t, d = x2.shape
    n_e = wr_t.shape[0]
    wr_hi = wr_t.astype(BF16)
    wr_lo = (wr_t - wr_hi.astype(F32)).astype(BF16)
    n_per = seq // tm
    per_b = lambda i: (i // n_per, 0, 0)
    rows = lambda wdt: pl.BlockSpec((tm, wdt), lambda i: (i, 0))
    full = lambda a: pl.BlockSpec(a.shape, lambda i: (0,) * a.ndim)
    vec = pl.BlockSpec((1, 1, d), per_b)
    (o0, l0), (o1, l1), (o2, l2) = att
    gw = o0.shape[3]
    by_residue = lambda a: pl.BlockSpec((1, a.shape[1], tm // a.shape[1], gw),
                                        lambda i: (i // n_per, 0, i % n_per, 0))
    att_in = (o0, o1, o2, l0, l1, l2)
    return pl.pallas_call(
        _merge_kernel,
        out_shape=[jax.ShapeDtypeStruct((t, d), F32),
                   jax.ShapeDtypeStruct((t, d // 2), U32),
                   jax.ShapeDtypeStruct((n_e, t), F32)],
        grid=(t // tm,),
        in_specs=[rows(ya.shape[1])] + [by_residue(a) for a in att_in] + [rows(d)] * 3
        + [vec, vec, vec, vec, pl.BlockSpec((1, d), lambda i: (0, 0))]
        + [full(a) for a in (wa, wb, wo, wr_hi, wr_lo, wsg, wsu, wsd)],
        out_specs=[rows(d), rows(d // 2), pl.BlockSpec((n_e, tm), lambda i: (0, i))],
        scratch_shapes=[pltpu.VMEM((gw // LANES, tm, LANES), F32)] * 6,
        compiler_params=_params(),
        name="merge_router",
    )(ya, *att_in, ga, gb, x2, gate1, scale2, shift2, gate2,
      norm2_g.reshape(1, d), wa, wb, wo, wr_hi, wr_lo, wsg, wsu, wsd)


def _topk_kernel(lg_ref, bias_ref, idx_ref, gate_ref, rank_ref, cnt_ref, carry_ref):
    n_e, tt = lg_ref.shape

    @pl.when(pl.program_id(0) == 0)
    def _():
        carry_ref[...] = jnp.zeros_like(carry_ref)

    scores = _sigmoid(lg_ref[...])
    sel = scores + bias_ref[...]
    eio = lax.broadcasted_iota(I32, (n_e, tt), 0)
    picked = jnp.zeros((n_e, tt), F32)
    idxs, vals = [], []
    for _ in range(TOP_K):
        m = jnp.max(sel, axis=0, keepdims=True)
        ik = jnp.min(jnp.where(sel == m, eio, n_e), axis=0, keepdims=True)
        hit = eio == ik
        vals.append(jnp.sum(jnp.where(hit, scores, 0.0), axis=0, keepdims=True))
        sel = jnp.where(hit, -jnp.inf, sel)
        picked = picked + jnp.where(hit, 1.0, 0.0)
        idxs.append(ik)
    denom = vals[0]
    for v in vals[1:]:
        denom = denom + v
    gate_ref[...] = jnp.concatenate([v / denom * ROUTE_SCALE for v in vals], axis=0)
    idx_ref[...] = jnp.concatenate(idxs, axis=0)

    upper = (lax.broadcasted_iota(I32, (tt, tt), 0) <= lax.broadcasted_iota(I32, (tt, tt), 1))
    incl = jnp.dot(picked.astype(BF16), jnp.where(upper, 1.0, 0.0).astype(BF16),
                   preferred_element_type=F32)
    before = incl - picked + carry_ref[...]
    rank_ref[...] = jnp.concatenate(
        [jnp.sum(jnp.where(eio == ik, before, 0.0), axis=0, keepdims=True) for ik in idxs],
        axis=0).astype(I32)
    carry_ref[...] = carry_ref[...] + jnp.sum(picked, axis=1, keepdims=True)
    cnt_ref[...] = jnp.broadcast_to(carry_ref[...], cnt_ref.shape).astype(I32)


def _topk(logits_t, bias, tt):
    n_e, t = logits_t.shape
    tok = pl.BlockSpec((TOP_K, tt), lambda i: (0, i))
    return pl.pallas_call(
        _topk_kernel,
        out_shape=[jax.ShapeDtypeStruct((TOP_K, t), I32), jax.ShapeDtypeStruct((TOP_K, t), F32),
                   jax.ShapeDtypeStruct((TOP_K, t), I32), jax.ShapeDtypeStruct((n_e, 128), I32)],
        grid=(t // tt,),
        in_specs=[pl.BlockSpec((n_e, tt), lambda i: (0, i)),
                  pl.BlockSpec((n_e, 1), lambda i: (0, 0))],
        out_specs=[tok, tok, tok, pl.BlockSpec((n_e, 128), lambda i: (0, 0))],
        scratch_shapes=[pltpu.VMEM((n_e, 1), F32)],
        compiler_params=_params(),
        name="router_topk",
    )(logits_t, bias.reshape(n_e, 1))


def _dest_kernel(idx_ref, rank_ref, start_ref, o_ref):
    k, tt = idx_ref.shape
    n_e = start_ref.shape[0]
    eio = lax.broadcasted_iota(I32, (n_e, tt), 0)
    start = start_ref[...]
    rows = [jnp.sum(jnp.where(eio == idx_ref[r:r + 1, :], start, 0), axis=0, keepdims=True)
            for r in range(k)]
    o_ref[...] = jnp.concatenate(rows, axis=0) + rank_ref[...]


def _dest(idx, rank, seg_start, tt):
    k, t = idx.shape
    n_e = seg_start.shape[0]
    tok = pl.BlockSpec((k, tt), lambda i: (0, i))
    return pl.pallas_call(
        _dest_kernel,
        out_shape=jax.ShapeDtypeStruct((k, t), I32),
        grid=(t // tt,),
        in_specs=[tok, tok, pl.BlockSpec((n_e, 1), lambda i: (0, 0))],
        out_specs=tok,
        compiler_params=_params(),
        name="moe_dest",
    )(idx, rank, seg_start.reshape(n_e, 1))


def _sc_mesh():
    return plsc.VectorSubcoreMesh(core_axis_name="core", subcore_axis_name="subcore")


def _sc_scatter_rows(rows, dest, n_out):
    k, t = dest.shape
    w = rows.shape[1]
    mesh = _sc_mesh()
    n_workers = mesh.num_cores * mesh.num_subcores
    win_per_worker = t // (SC_WINDOW * n_workers)
    assert win_per_worker * SC_WINDOW * n_workers == t

    @functools.partial(
        pl.kernel, out_type=jax.ShapeDtypeStruct((n_out, w), rows.dtype), mesh=mesh,
        scratch_types=[pltpu.VMEM((SC_WINDOW, w), rows.dtype)]
        + [pltpu.VMEM((1, SC_WINDOW), I32)] * k + [pltpu.SemaphoreType.DMA],
        name="moe_dispatch_sc")
    def run(rows_hbm, idx_hbm, out_hbm, rows_v, *rest):
        idx_v, sem = rest[:k], rest[k]
        worker = lax.axis_index("subcore") * mesh.num_cores + lax.axis_index("core")

        @pl.loop(0, win_per_worker)
        def _(j):
            t0 = pl.multiple_of((worker * win_per_worker + j) * SC_WINDOW, SC_WINDOW)
            pltpu.sync_copy(rows_hbm.at[pl.ds(t0, SC_WINDOW)], rows_v)
            for r in range(k):
                pltpu.sync_copy(idx_hbm.at[:, pl.ds(r * t + t0, SC_WINDOW)], idx_v[r])
            copies = [pltpu.async_copy(rows_v, out_hbm.at[idx_v[r].at[0]], sem) for r in range(k)]
            for c in copies:
                c.wait()

    return run(rows, dest.reshape(1, k * t))


def _sc_gather_rows(table, dest):
    k, t = dest.shape
    w = table.shape[1]
    mesh = _sc_mesh()
    n_workers = mesh.num_cores * mesh.num_subcores
    win_per_worker = (k * t) // (SC_WINDOW * n_workers)
    assert win_per_worker * SC_WINDOW * n_workers == k * t

    @functools.partial(
        pl.kernel, out_type=jax.ShapeDtypeStruct((k * t, w), table.dtype), mesh=mesh,
        scratch_types=[pltpu.VMEM((SC_WINDOW, w), table.dtype), pltpu.VMEM((1, SC_WINDOW), I32)],
        name="moe_gather_sc")
    def run(table_hbm, idx_hbm, out_hbm, rows_v, idx_v):
        worker = lax.axis_index("subcore") * mesh.num_cores + lax.axis_index("core")

        @pl.loop(0, win_per_worker)
        def _(j):
            p0 = pl.multiple_of((worker * win_per_worker + j) * SC_WINDOW, SC_WINDOW)
            pltpu.sync_copy(idx_hbm.at[:, pl.ds(p0, SC_WINDOW)], idx_v)
            pltpu.sync_copy(table_hbm.at[idx_v.at[0]], rows_v)
            pltpu.sync_copy(rows_v, out_hbm.at[pl.ds(p0, SC_WINDOW)])

    return run(table, dest.reshape(1, k * t))


def _expert_kernel(start_ref, nblk_ref, xs_ref, wg_ref, wu_ref, wd_ref, ys_ref,
                   xbuf, ybuf, wgb, wub, wdb, sem_in, sem_out):
    e = pl.program_id(0)
    n_e = pl.num_programs(0)
    nb = nblk_ref[e]
    g0 = start_ref[e] // MOE_BLOCK
    n_used = start_ref[n_e - 1] // MOE_BLOCK + nblk_ref[n_e - 1]
    n_in, n_out = xbuf.shape[0], ybuf.shape[0]

    def rows(g):
        return pl.ds(pl.multiple_of(g * MOE_BLOCK, MOE_BLOCK), MOE_BLOCK)

    def in_copy(g):
        slot = lax.rem(g, n_in)
        return pltpu.make_async_copy(xs_ref.at[rows(g), :], xbuf.at[slot], sem_in.at[slot])

    def out_copy(g):
        slot = lax.rem(g, n_out)
        return pltpu.make_async_copy(ybuf.at[slot], ys_ref.at[rows(g), :], sem_out.at[slot])

    look = n_in - EXPERT_GROUP

    @pl.when(e == 0)
    def _():
        for g in range(look):
            @pl.when(g < n_used)
            def _():
                in_copy(g).start(priority=g % N_DMA_QUEUES)

    @pl.when(nb > 0)
    def _():
        wgb[...] = wg_ref[0].astype(BF16)
        wub[...] = wu_ref[0].astype(BF16)
        wdb[...] = wd_ref[0].astype(BF16)
        half = xbuf.shape[2]

        def swiglu(word):
            lo, hi = _unpack_halves(word)
            lo, hi = lo.astype(BF16), hi.astype(BF16)
            gate = (jnp.dot(lo, wgb[:half], preferred_element_type=F32)
                    + jnp.dot(hi, wgb[half:], preferred_element_type=F32))
            up = (jnp.dot(lo, wub[:half], preferred_element_type=F32)
                  + jnp.dot(hi, wub[half:], preferred_element_type=F32))
            act = (_silu(gate) * up).astype(BF16)
            return jnp.dot(act, wdb[...], preferred_element_type=F32)

        def process(g, m):
            for i in range(m):
                in_copy(g + i).wait()
            for i in range(m):
                @pl.when(g + look + i < n_used)
                def _():
                    in_copy(g + look + i).start(priority=i % N_DMA_QUEUES)
            ys = [swiglu(xbuf[lax.rem(g + i, n_in)]) for i in range(m)]
            for i in range(m):
                @pl.when(g + i >= n_out)
                def _():
                    out_copy(g + i - n_out).wait()

                ybuf[lax.rem(g + i, n_out)] = _pack_halves(ys[i])
                out_copy(g + i).start(priority=(i + 1) % N_DMA_QUEUES)

        def group_body(p, carry):
            process(g0 + p * EXPERT_GROUP, EXPERT_GROUP)
            return carry

        lax.fori_loop(0, nb // EXPERT_GROUP, group_body, 0)
        for m in range(1, EXPERT_GROUP):
            @pl.when(lax.rem(nb, EXPERT_GROUP) == m)
            def _():
                process(g0 + nb - m, m)

    @pl.when(e == n_e - 1)
    def _():
        for i in range(n_out):
            @pl.when(n_used - 1 - i >= 0)
            def _():
                out_copy(n_used - 1 - i).wait()


def _experts(seg_start, seg_blocks, xs, wg, wu, wd):
    n_slots, half = xs.shape
    n_e, d, de = wg.shape
    return pl.pallas_call(
        _expert_kernel,
        out_shape=jax.ShapeDtypeStruct((n_slots, half), U32),
        grid_spec=pltpu.PrefetchScalarGridSpec(
            num_scalar_prefetch=2,
            grid=(n_e,),
            in_specs=[pl.BlockSpec(memory_space=pl.ANY),
                      pl.BlockSpec((1, d, de), lambda e, s, n: (e, 0, 0)),
                      pl.BlockSpec((1, d, de), lambda e, s, n: (e, 0, 0)),
                      pl.BlockSpec((1, de, d), lambda e, s, n: (e, 0, 0))],
            out_specs=pl.BlockSpec(memory_space=pl.ANY),
            scratch_shapes=[pltpu.VMEM((EXPERT_IN_RING, MOE_BLOCK, half), U32),
                            pltpu.VMEM((EXPERT_OUT_RING, MOE_BLOCK, half), U32),
                            pltpu.VMEM((d, de), BF16), pltpu.VMEM((d, de), BF16),
                            pltpu.VMEM((de, d), BF16),
                            pltpu.SemaphoreType.DMA((EXPERT_IN_RING,)),
                            pltpu.SemaphoreType.DMA((EXPERT_OUT_RING,))]),
        compiler_params=_params(),
        name="moe_experts",
    )(seg_start, seg_blocks, xs, wg, wu, wd)


def _combine_kernel(yg_ref, gt_ref, x_ref, g2_ref, fg_ref, o_ref):
    k = yg_ref.shape[0]
    gt = gt_ref[...]
    lo, hi = _unpack_halves(yg_ref[0])
    y_lo, y_hi = lo * gt[:, 0:1], hi * gt[:, 0:1]
    for r in range(1, k):
        lo, hi = _unpack_halves(yg_ref[r])
        y_lo, y_hi = y_lo + lo * gt[:, r:r + 1], y_hi + hi * gt[:, r:r + 1]
    y = jnp.concatenate([y_lo, y_hi], axis=1)
    o_ref[...] = _rms(x_ref[...] + g2_ref[0] * y, fg_ref[...])


def _combine_into_kernel(yg_ref, gt_ref, x_ref, g2_ref, fg_ref, prev_ref, o_ref):
    del prev_ref
    _combine_kernel(yg_ref, gt_ref, x_ref, g2_ref, fg_ref, o_ref)


def _combine(yg, tok0, gates_t, x1s, gate2, final_g, seq, tc, out_so_far=None):
    k, n, half = yg.shape
    t, d = x1s.shape
    b0 = tok0 // tc
    args = [yg, gates_t, x1s, gate2, final_g.reshape(1, d)]
    in_specs = [pl.BlockSpec((k, tc, half), lambda i: (0, i, 0)),
                pl.BlockSpec((tc, k), lambda i: (i + b0, 0)),
                pl.BlockSpec((tc, d), lambda i: (i + b0, 0)),
                pl.BlockSpec((1, 1, d), lambda i: (((i + b0) * tc) // seq, 0, 0)),
                pl.BlockSpec((1, d), lambda i: (0, 0))]
    aliases = {}
    kernel = _combine_kernel
    if out_so_far is not None:
        args.append(out_so_far)
        in_specs.append(pl.BlockSpec(memory_space=pl.ANY))
        aliases = {len(args) - 1: 0}
        kernel = _combine_into_kernel
    return pl.pallas_call(
        kernel,
        out_shape=jax.ShapeDtypeStruct((t, d), F32),
        grid=(n // tc,),
        in_specs=in_specs,
        out_specs=pl.BlockSpec((tc, d), lambda i: (i + b0, 0)),
        input_output_aliases=aliases,
        compiler_params=_params(),
        name="moe_combine",
    )(*args)


def _layer(x2, c, bsz, seq, lb_row, ada_w, ada_b, norm1_g, w_in, hg_norm_g, w_branch_a, w_branch_b,
           w_out, norm2_g, w_router, router_bias, w_exp_gate, w_exp_up, w_exp_down, w_sh_gate,
           w_sh_up, w_sh_down, final_g):
    t, d = x2.shape
    n_e = w_router.shape[1]
    mod = _ada(c, ada_w, ada_b).reshape(bsz, 6, 1, d)
    shift1, scale1, gate1, shift2, scale2, gate2 = (mod[:, j] for j in range(6))

    hw = hg_norm_g.shape[0]
    aw = len(ATT_GROUPS) * ATT_HEADS_PER_GROUP * ATT_HEAD_DIM
    flat_segs = [(0, hw, BF16), (hw, hw, F32), (2 * hw, hw, BF16), (3 * hw, hw, BF16),
                 (4 * hw + 3 * aw, d, BF16), (4 * hw + 3 * aw + d, d, BF16)]
    (hq, hf, hi, hg, ga, gb), qkv = _inproj(
        x2, norm1_g, scale1, shift1, w_in.astype(BF16), bsz, seq, flat_segs, 4 * hw, tm=512)

    ya = _hgrn(hq, hf, hi, hg, lb_row, hg_norm_g, bsz, seq, ts=256)
    att = [_attn_group(*qkv[3 * g:3 * g + 3], g, nq=4) for g in range(len(ATT_GROUPS))]

    x1s, hp, logits_t = _merge(
        ya, att, ga, gb, x2, gate1, scale2, shift2, gate2, norm2_g, w_branch_a.astype(BF16),
        w_branch_b.astype(BF16), w_out.astype(BF16), w_router.T, w_sh_gate.astype(BF16),
        w_sh_up.astype(BF16), w_sh_down.astype(BF16), seq, tm=512)

    idx, gates, rank, cnt = _topk(logits_t, router_bias, tt=512)
    counts = cnt[:, 0]
    padded = (counts + MOE_BLOCK - 1) // MOE_BLOCK * MOE_BLOCK
    seg_start = (jnp.cumsum(padded) - padded).astype(I32)
    n_blocks = -(-(t * TOP_K) // MOE_BLOCK) + n_e
    dest = _dest(idx, rank, seg_start, tt=512)

    xs = _sc_scatter_rows(hp, dest, n_blocks * MOE_BLOCK)
    ys = _experts(seg_start, (padded // MOE_BLOCK).astype(I32), xs, w_exp_gate, w_exp_up,
                  w_exp_down)
    out, n = None, t // COMBINE_PARTS
    for part in range(COMBINE_PARTS):
        yg = _sc_gather_rows(ys, dest[:, part * n:(part + 1) * n]).reshape(TOP_K, n, d // 2)
        out = _combine(yg, part * n, gates.T, x1s, gate2, final_g, seq, tc=256, out_so_far=out)
    return out


def kernel(x, c, ada_w, ada_b, norm1_g, w_in, lb_logits, hg_norm_g, w_branch_a, w_branch_b, w_out,
           norm2_g, w_router, router_bias, w_exp_gate, w_exp_up, w_exp_down, w_sh_gate, w_sh_up,
           w_sh_down, final_g):
    bsz, seq, d = x.shape
    depth = ada_w.shape[0]
    assert depth == 1, "the last layer's kernels also apply the final norm"
    lb_table = jnp.cumsum(jax.nn.softmax(lb_logits.astype(F32), axis=0), axis=0)
    out = _layer(x.reshape(bsz * seq, d), c, bsz, seq, lb_table[0], ada_w[0], ada_b[0], norm1_g[0],
                 w_in[0], hg_norm_g[0], w_branch_a[0], w_branch_b[0], w_out[0], norm2_g[0],
                 w_router[0], router_bias[0], w_exp_gate[0], w_exp_up[0], w_exp_down[0],
                 w_sh_gate[0], w_sh_up[0], w_sh_down[0], final_g)
    return out.reshape(bsz, seq, d)
```

```python
import functools

import jax
import jax.numpy as jnp
from jax import lax
from jax.experimental import pallas as pl
from jax.experimental.pallas import tpu as pltpu
from jax.experimental.pallas import tpu_sc as plsc

F32 = jnp.float32
BF16 = jnp.bfloat16
I32 = jnp.int32
U32 = jnp.uint32
HIGHEST = lax.Precision.HIGHEST

HG_HEADS = 4
HG_BLOCK = 16
HG_CHUNK = 32
HG_MILD_DECAY = -80.0
ATT_GROUPS = ((128, 1), (512, 4), (2048, 16))
ATT_HEADS_PER_GROUP = 4
ATT_HEAD_DIM = 64
TOP_K = 8
ROUTE_SCALE = 2.5
MOE_BLOCK = 256
RMS_EPS = 1e-6
N_DMA_QUEUES = 2
SC_WINDOW = 128
COMBINE_PARTS = 2
EXPERT_WEIGHT_BUFFERS = 3
EXPERT_GROUP = 4
EXPERT_IN_RING = 8
EXPERT_OUT_RING = 6

LANES = 128
VMEM_LIMIT_BYTES = 56 * 1024 * 1024


def _sigmoid(x):
    return 1.0 / (1.0 + jnp.exp(-x))


def _silu(x):
    return x * _sigmoid(x)


def _rms(x, g):
    return x * lax.rsqrt(jnp.mean(x * x, axis=-1, keepdims=True) + RMS_EPS) * g


def _pack_halves(x):
    n = x.shape[1] // 2
    bits = lax.bitcast_convert_type(x.astype(BF16).astype(F32), U32)
    return (bits[:, :n] >> 16) | (bits[:, n:] & jnp.uint32(0xFFFF0000))


def _unpack_halves(word):
    lo = lax.bitcast_convert_type(word << 16, F32)
    hi = lax.bitcast_convert_type(word & jnp.uint32(0xFFFF0000), F32)
    return lo, hi


def _params(n_axes=1):
    return pltpu.CompilerParams(
        dimension_semantics=("arbitrary",) * n_axes, vmem_limit_bytes=VMEM_LIMIT_BYTES)


def _ada_kernel(c_ref, w_ref, b_ref, o_ref):
    sc = _silu(c_ref[...])
    o_ref[...] = jnp.dot(sc, w_ref[...], preferred_element_type=F32, precision=HIGHEST) + b_ref[...]


def _ada(c, w, b):
    bsz, d = c.shape
    n = w.shape[1]
    return pl.pallas_call(
        _ada_kernel,
        out_shape=jax.ShapeDtypeStruct((bsz, n), F32),
        grid=(n // d,),
        in_specs=[pl.BlockSpec((bsz, d), lambda j: (0, 0)),
                  pl.BlockSpec((d, d), lambda j: (0, j)),
                  pl.BlockSpec((1, d), lambda j: (0, j))],
        out_specs=pl.BlockSpec((bsz, d), lambda j: (0, j)),
        compiler_params=_params(),
        name="ada_mod",
    )(c, w, b.reshape(1, n))


def _inproj_kernel(n_flat, flat_ranges, att_c0, x_ref, g_ref, sc_ref, sh_ref, w_ref, *refs):
    flat_refs, att_refs, scr = refs[:n_flat], refs[n_flat:-1], refs[-1]
    tm = x_ref.shape[0]
    h = _rms(x_ref[...], g_ref[...]) * (1.0 + sc_ref[0]) + sh_ref[0]
    hb = h.astype(BF16)
    for (c0, c1), o_ref in zip(flat_ranges, flat_refs):
        o_ref[...] = jnp.dot(hb, w_ref[:, c0:c1], preferred_element_type=F32).astype(o_ref.dtype)
    gw = ATT_HEADS_PER_GROUP * ATT_HEAD_DIM
    n_groups = len(ATT_GROUPS)
    for part in range(3):
        c0 = att_c0 + part * n_groups * gw
        res = jnp.dot(hb, w_ref[:, c0:c0 + n_groups * gw], preferred_element_type=F32)
        if part == 0:
            res = res * (ATT_HEAD_DIM ** -0.5)
        for g, (_, dil) in enumerate(ATT_GROUPS):
            o_ref = att_refs[g * 3 + part]
            sub = res[:, g * gw:(g + 1) * gw]
            if dil == 1:
                o_ref[0, 0] = sub.astype(BF16)
            else:
                for c in range(gw // LANES):
                    scr[c] = sub[:, c * LANES:(c + 1) * LANES]
                for r in range(dil):
                    o_ref[0, r] = jnp.concatenate(
                        [scr[c, pl.ds(r, tm // dil, stride=dil), :] for c in range(gw // LANES)],
                        axis=1).astype(BF16)


def _inproj(x2, g, scale, shift, w_bf16, bsz, seq, flat_segs, att_c0, tm):
    t, d = x2.shape
    gw = ATT_HEADS_PER_GROUP * ATT_HEAD_DIM
    n_per = seq // tm
    per_b = lambda i: (i // n_per, 0, 0)
    att_shapes, att_specs = [], []
    for _, dil in ATT_GROUPS:
        for _ in range(3):
            att_shapes.append(jax.ShapeDtypeStruct((bsz, dil, seq // dil, gw), BF16))
            att_specs.append(pl.BlockSpec((1, dil, tm // dil, gw),
                                          lambda i: (i // n_per, 0, i % n_per, 0)))
    outs = pl.pallas_call(
        functools.partial(_inproj_kernel, len(flat_segs),
                          tuple((c0, c0 + wdt) for c0, wdt, _ in flat_segs), att_c0),
        out_shape=[jax.ShapeDtypeStruct((t, wdt), dt) for _, wdt, dt in flat_segs] + att_shapes,
        grid=(t // tm,),
        in_specs=[pl.BlockSpec((tm, d), lambda i: (i, 0)),
                  pl.BlockSpec((1, d), lambda i: (0, 0)),
                  pl.BlockSpec((1, 1, d), per_b),
                  pl.BlockSpec((1, 1, d), per_b),
                  pl.BlockSpec(w_bf16.shape, lambda i: (0, 0))],
        out_specs=[pl.BlockSpec((tm, wdt), lambda i: (i, 0)) for _, wdt, _ in flat_segs]
        + att_specs,
        scratch_shapes=[pltpu.VMEM((gw // LANES, tm, LANES), F32)],
        compiler_params=_params(),
        name="in_proj",
    )(x2, g.reshape(1, d), scale, shift, w_bf16)
    return outs[:len(flat_segs)], outs[len(flat_segs):]


def _hgrn_kernel(ts, q_ref, f_ref, v_ref, gt_ref, lb_ref, ng_ref, o_ref, st_ref, b_ref):
    dk = q_ref.shape[1] // HG_HEADS
    n_chunks = ts // HG_CHUNK
    n_blk = HG_CHUNK // HG_BLOCK

    @pl.when(pl.program_id(1) == 0)
    def _():
        st_ref[...] = jnp.zeros_like(st_ref)

    row = lax.broadcasted_iota(I32, (LANES, LANES), 0)
    col = lax.broadcasted_iota(I32, (LANES, LANES), 1)
    same_chunk = (row // HG_CHUNK) == (col // HG_CHUNK)
    cum_mat = jnp.where(same_chunk & (col <= row), 1.0, 0.0).astype(BF16)

    def chunk_cumsum(x):
        out = []
        for r0 in range(0, ts, LANES):
            rest = x[r0:r0 + LANES]
            acc = None
            for _ in range(3):
                term = rest.astype(BF16)
                part = jnp.dot(cum_mat, term, preferred_element_type=F32)
                acc = part if acc is None else acc + part
                rest = rest - term.astype(F32)
            out.append(acc)
        return jnp.concatenate(out, axis=0)

    def forget(cs):
        lb = lb_ref[:, cs]
        return lb + (1.0 - lb) * _sigmoid(f_ref[:, cs])

    b_min = None
    for h in range(HG_HEADS):
        cs = slice(h * dk, (h + 1) * dk)
        b = chunk_cumsum(jnp.log(forget(cs)))
        b_ref[:, cs] = b
        m = jnp.min(b)
        b_min = m if b_min is None else jnp.minimum(b_min, m)
    mild = b_min >= HG_MILD_DECAY

    def finish(h, o, st):
        cs = slice(h * dk, (h + 1) * dk)
        st_ref[h] = st
        y = _rms(o, ng_ref[:, cs]) * _silu(gt_ref[:, cs].astype(F32))
        o_ref[:, cs] = y.astype(o_ref.dtype)

    @pl.when(mild)
    def _():
        span = 2 * HG_CHUNK
        causal = (lax.broadcasted_iota(I32, (span, span), 0)
                  >= lax.broadcasted_iota(I32, (span, span), 1))
        nt = lambda x, y: lax.dot_general(x, y, (((1,), (1,)), ((), ())),
                                          preferred_element_type=F32)
        for h in range(HG_HEADS):
            cs = slice(h * dk, (h + 1) * dk)
            v = v_ref[:, cs]
            b = b_ref[:, cs]
            q = q_ref[:, cs].astype(F32)
            k = 1.0 - forget(cs)
            st = st_ref[h]
            o_rows = []
            for r0 in range(0, ts, span):
                sl = slice(r0, r0 + span)
                b_first, b_second = b[r0:r0 + HG_CHUNK], b[r0 + HG_CHUNK:r0 + span]
                end_first = b_first[HG_CHUNK - 1:HG_CHUNK]
                end_second = b_second[HG_CHUNK - 1:HG_CHUNK]
                e = jnp.exp(jnp.concatenate([b_first - end_first, b_second], axis=0))
                qe = (q[sl] * e).astype(BF16)
                ke = k[sl] / e
                a = jnp.where(causal, nt(qe, ke.astype(BF16)), 0.0).astype(BF16)
                st_in = (st * jnp.exp(end_first)).astype(BF16)
                o_rows.append(jnp.dot(a, v[sl], preferred_element_type=F32) + nt(qe, st_in))
                kend = (ke * jnp.exp(end_second)).astype(BF16)
                vt = v[sl].astype(F32).T.astype(BF16)
                st = (st * jnp.exp(end_first + end_second)
                      + jnp.dot(vt, kend, preferred_element_type=F32))
            finish(h, jnp.concatenate(o_rows, axis=0), st)

    @pl.when(jnp.logical_not(mild))
    def _():
        _hgrn_steep(ts, dk, n_chunks, n_blk, q_ref, v_ref, b_ref, st_ref, forget, finish)


def _hgrn_steep(ts, dk, n_chunks, n_blk, q_ref, v_ref, b_ref, st_ref, forget, finish):
    t_in_blk = lax.broadcasted_iota(I32, (ts, dk), 0) % HG_BLOCK

    for h in range(HG_HEADS):
        cs = slice(h * dk, (h + 1) * dk)
        q = q_ref[:, cs].astype(F32)
        v = v_ref[:, cs].astype(F32)
        k = 1.0 - forget(cs)
        b = b_ref[:, cs]

        o = jnp.sum(q * k, axis=-1, keepdims=True) * v
        for d in range(1, HG_BLOCK):
            k_d = pltpu.roll(k, d, axis=0)
            b_d = pltpu.roll(b, d, axis=0)
            v_d = pltpu.roll(v, d, axis=0)
            w = jnp.sum(q * k_d * jnp.exp(jnp.minimum(b - b_d, 0.0)), axis=-1, keepdims=True)
            o = o + jnp.where(t_in_blk >= d, w * v_d, 0.0)

        st = st_ref[h]
        o_rows = []
        for c in range(n_chunks):
            r0 = c * HG_CHUNK
            bc = b[r0:r0 + HG_CHUNK]
            qc = q[r0:r0 + HG_CHUNK]
            kc = k[r0:r0 + HG_CHUNK]
            vc = v[r0:r0 + HG_CHUNK].astype(BF16)
            st_b = st.astype(BF16)
            for i in range(n_blk):
                i0 = i * HG_BLOCK
                if i == 0:
                    qt = qc[:HG_BLOCK] * jnp.exp(bc[:HG_BLOCK])
                    qs = qt
                else:
                    ref_row = bc[i0 - 1:i0]
                    qt = qc[i0:i0 + HG_BLOCK] * jnp.exp(bc[i0:i0 + HG_BLOCK] - ref_row)
                    qs = qt * jnp.exp(ref_row)
                oi = lax.dot_general(qs.astype(BF16), st_b, (((1,), (1,)), ((), ())),
                                     preferred_element_type=F32)
                if i > 0:
                    kh = kc[:i0] * jnp.exp(ref_row - bc[:i0])
                    a = lax.dot_general(qt.astype(BF16), kh.astype(BF16), (((1,), (1,)), ((), ())),
                                        preferred_element_type=F32)
                    oi = oi + jnp.dot(a.astype(BF16), vc[:i0], preferred_element_type=F32)
                o_rows.append(oi)
            b_end = bc[HG_CHUNK - 1:HG_CHUNK]
            kend = kc * jnp.exp(b_end - bc)
            vt = v[r0:r0 + HG_CHUNK].T.astype(BF16)
            st = st * jnp.exp(b_end) + jnp.dot(vt, kend.astype(BF16), preferred_element_type=F32)
        finish(h, o + jnp.concatenate(o_rows, axis=0), st)


def _hgrn(hq, hf, hi, hg, lb, ng, bsz, seq, ts):
    t, w = hq.shape
    dk = w // HG_HEADS
    n_s = seq // ts
    tile = lambda b, s: (b * n_s + s, 0)
    return pl.pallas_call(
        functools.partial(_hgrn_kernel, ts),
        out_shape=jax.ShapeDtypeStruct((t, w), BF16),
        grid=(bsz, n_s),
        in_specs=[pl.BlockSpec((ts, w), tile)] * 4
        + [pl.BlockSpec((1, w), lambda b, s: (0, 0))] * 2,
        out_specs=pl.BlockSpec((ts, w), tile),
        scratch_shapes=[pltpu.VMEM((HG_HEADS, dk, dk), F32), pltpu.VMEM((ts, w), F32)],
        compiler_params=_params(2),
        name="hgrn2",
    )(hq, hf, hi, hg, lb.reshape(1, w), ng.reshape(1, w))


def _attn_kernel(nk, nq, q_ref, kp_ref, kc_ref, vp_ref, vc_ref, o_ref, lse_ref):
    n = pl.program_id(2)
    e = ATT_HEAD_DIM
    i = lax.broadcasted_iota(I32, (nk, 2 * nk), 0)
    j = lax.broadcasted_iota(I32, (nk, 2 * nk), 1)
    band = (j >= i) & (j <= i + nk)
    kk = jnp.concatenate([kp_ref[0, 0], kc_ref[0, 0]], axis=0)
    vv = jnp.concatenate([vp_ref[0, 0], vc_ref[0, 0]], axis=0)
    first_head = lax.broadcasted_iota(I32, (nk, LANES), 1) < e
    zero = jnp.zeros((), q_ref.dtype)
    for b in range(nq):
        valid = band & ((j >= nk) | (n * nq + b > 0))
        rows = slice(b * nk, (b + 1) * nk)
        for c in range(0, ATT_HEADS_PER_GROUP * e, LANES):
            q = q_ref[0, 0, rows, c:c + LANES]
            kb = kk[b * nk:(b + 2) * nk, c:c + LANES]
            vb = vv[b * nk:(b + 2) * nk, c:c + LANES]
            outs, lses = [], []
            for keep in (first_head, jnp.logical_not(first_head)):
                s = lax.dot_general(jnp.where(keep, q, zero), kb, (((1,), (1,)), ((), ())),
                                    preferred_element_type=F32)
                s = jnp.where(valid, s, -jnp.inf)
                m = jnp.max(s, axis=-1, keepdims=True)
                p = jnp.exp(s - m)
                l = jnp.sum(p, axis=-1, keepdims=True)
                outs.append(jnp.dot(p.astype(BF16), vb, preferred_element_type=F32) / l)
                lses.append(m + jnp.log(l))
            o_ref[0, 0, rows, c:c + LANES] = jnp.where(first_head, outs[0], outs[1])
            lse_ref[0, 0, rows, c:c + LANES] = jnp.where(first_head, lses[0], lses[1])


def _attn_group(q, k, v, g, nq):
    window, dil = ATT_GROUPS[g]
    nk = window // dil
    bsz, _, ln, gw = q.shape
    nq = min(nq, ln // nk)
    assert ln % (nk * nq) == 0 and 2 * ATT_HEAD_DIM == LANES
    cur = pl.BlockSpec((1, 1, nq * nk, gw), lambda b, r, n: (b, r, n, 0))
    prev = pl.BlockSpec((1, 1, nk, gw), lambda b, r, n: (b, r, jnp.maximum(n * nq - 1, 0), 0))
    return pl.pallas_call(
        functools.partial(_attn_kernel, nk, nq),
        out_shape=[jax.ShapeDtypeStruct(q.shape, F32)] * 2,
        grid=(bsz, dil, ln // (nk * nq)),
        in_specs=[cur, prev, cur, prev, cur],
        out_specs=[cur, cur],
        compiler_params=_params(3),
        name=f"dilated_attn_g{g}",
    )(q, k, k, v, v)


def _token_major(ref, scr):
    dil, rows = ref.shape[1], ref.shape[2]
    if dil == 1:
        return ref[0, 0]
    n_col = scr.shape[0]
    for r in range(dil):
        for c in range(n_col):
            scr[c, pl.ds(r, rows, stride=dil), :] = ref[0, r, :, c * LANES:(c + 1) * LANES]
    return jnp.concatenate([scr[c] for c in range(n_col)], axis=1)


def _merge_kernel(ya_ref, o0_ref, o1_ref, o2_ref, l0_ref, l1_ref, l2_ref, ga_ref, gb_ref, x_ref,
                  g1_ref, sc2_ref, sh2_ref, g2_ref, n2_ref, wa_ref, wb_ref, wo_ref, wr_ref, wrl_ref,
                  wsg_ref, wsu_ref, wsd_ref, x1_ref, hp_ref, lg_ref, *scr):
    l0, l1, l2 = (_token_major(r, s) for r, s in zip((l0_ref, l1_ref, l2_ref), scr[:3]))
    o0, o1, o2 = (_token_major(r, s) for r, s in zip((o0_ref, o1_ref, o2_ref), scr[3:]))
    m = jnp.maximum(jnp.maximum(l0, l1), l2)
    e0, e1, e2 = jnp.exp(l0 - m), jnp.exp(l1 - m), jnp.exp(l2 - m)
    yb = (e0 * o0 + e1 * o1 + e2 * o2) / (e0 + e1 + e2)
    merged = (_sigmoid(ga_ref[...].astype(F32))
              * jnp.dot(ya_ref[...], wa_ref[...], preferred_element_type=F32)
              + _sigmoid(gb_ref[...].astype(F32))
              * jnp.dot(yb.astype(BF16), wb_ref[...], preferred_element_type=F32))
    x1 = x_ref[...] + g1_ref[0] * jnp.dot(merged.astype(BF16), wo_ref[...],
                                           preferred_element_type=F32)
    h2 = _rms(x1, n2_ref[...]) * (1.0 + sc2_ref[0]) + sh2_ref[0]
    hb = h2.astype(BF16)
    act = (_silu(jnp.dot(hb, wsg_ref[...], preferred_element_type=F32))
           * jnp.dot(hb, wsu_ref[...], preferred_element_type=F32))
    shared = jnp.dot(act.astype(BF16), wsd_ref[...], preferred_element_type=F32)
    x1_ref[...] = x1 + g2_ref[0] * shared
    hp_ref[...] = _pack_halves(h2)
    h_lo = (h2 - hb.astype(F32)).astype(BF16)
    nt = lambda a, b: lax.dot_general(a, b, (((1,), (1,)), ((), ())), preferred_element_type=F32)
    lg_ref[...] = nt(wr_ref[...], hb) + (nt(wr_ref[...], h_lo) + nt(wrl_ref[...], hb))


def _merge(ya, att, ga, gb, x2, gate1, scale2, shift2, gate2, norm2_g, wa, wb, wo, wr_t, wsg, wsu,
           wsd, seq, tm):
    t, d = x2.shape
    n_e = wr_t.shape[0]
    wr_hi = wr_t.astype(BF16)
    wr_lo = (wr_t - wr_hi.astype(F32)).astype(BF16)
    n_per = seq // tm
    per_b = lambda i: (i // n_per, 0, 0)
    rows = lambda wdt: pl.BlockSpec((tm, wdt), lambda i: (i, 0))
    full = lambda a: pl.BlockSpec(a.shape, lambda i: (0,) * a.ndim)
    vec = pl.BlockSpec((1, 1, d), per_b)
    (o0, l0), (o1, l1), (o2, l2) = att
    gw = o0.shape[3]
    by_residue = lambda a: pl.BlockSpec((1, a.shape[1], tm // a.shape[1], gw),
                                        lambda i: (i // n_per, 0, i % n_per, 0))
    att_in = (o0, o1, o2, l0, l1, l2)
    return pl.pallas_call(
        _merge_kernel,
        out_shape=[jax.ShapeDtypeStruct((t, d), F32),
                   jax.ShapeDtypeStruct((t, d // 2), U32),
                   jax.ShapeDtypeStruct((n_e, t), F32)],
        grid=(t // tm,),
        in_specs=[rows(ya.shape[1])] + [by_residue(a) for a in att_in] + [rows(d)] * 3
        + [vec, vec, vec, vec, pl.BlockSpec((1, d), lambda i: (0, 0))]
        + [full(a) for a in (wa, wb, wo, wr_hi, wr_lo, wsg, wsu, wsd)],
        out_specs=[rows(d), rows(d // 2), pl.BlockSpec((n_e, tm), lambda i: (0, i))],
        scratch_shapes=[pltpu.VMEM((gw // LANES, tm, LANES), F32)] * 6,
        compiler_params=_params(),
        name="merge_router",
    )(ya, *att_in, ga, gb, x2, gate1, scale2, shift2, gate2,
      norm2_g.reshape(1, d), wa, wb, wo, wr_hi, wr_lo, wsg, wsu, wsd)


def _topk_kernel(lg_ref, bias_ref, idx_ref, gate_ref, rank_ref, cnt_ref, carry_ref):
    n_e, tt = lg_ref.shape

    @pl.when(pl.program_id(0) == 0)
    def _():
        carry_ref[...] = jnp.zeros_like(carry_ref)

    scores = _sigmoid(lg_ref[...])
    sel = scores + bias_ref[...]
    eio = lax.broadcasted_iota(I32, (n_e, tt), 0)
    picked = jnp.zeros((n_e, tt), F32)
    idxs, vals = [], []
    for _ in range(TOP_K):
        m = jnp.max(sel, axis=0, keepdims=True)
        ik = jnp.min(jnp.where(sel == m, eio, n_e), axis=0, keepdims=True)
        hit = eio == ik
        vals.append(jnp.sum(jnp.where(hit, scores, 0.0), axis=0, keepdims=True))
        sel = jnp.where(hit, -jnp.inf, sel)
        picked = picked + jnp.where(hit, 1.0, 0.0)
        idxs.append(ik)
    denom = vals[0]
    for v in vals[1:]:
        denom = denom + v
    gate_ref[...] = jnp.concatenate([v / denom * ROUTE_SCALE for v in vals], axis=0)
    idx_ref[...] = jnp.concatenate(idxs, axis=0)

    upper = (lax.broadcasted_iota(I32, (tt, tt), 0) <= lax.broadcasted_iota(I32, (tt, tt), 1))
    incl = jnp.dot(picked.astype(BF16), jnp.where(upper, 1.0, 0.0).astype(BF16),
                   preferred_element_type=F32)
    before = incl - picked + carry_ref[...]
    rank_ref[...] = jnp.concatenate(
        [jnp.sum(jnp.where(eio == ik, before, 0.0), axis=0, keepdims=True) for ik in idxs],
        axis=0).astype(I32)
    carry_ref[...] = carry_ref[...] + jnp.sum(picked, axis=1, keepdims=True)
    cnt_ref[...] = jnp.broadcast_to(carry_ref[...], cnt_ref.shape).astype(I32)


def _topk(logits_t, bias, tt):
    n_e, t = logits_t.shape
    tok = pl.BlockSpec((TOP_K, tt), lambda i: (0, i))
    return pl.pallas_call(
        _topk_kernel,
        out_shape=[jax.ShapeDtypeStruct((TOP_K, t), I32), jax.ShapeDtypeStruct((TOP_K, t), F32),
                   jax.ShapeDtypeStruct((TOP_K, t), I32), jax.ShapeDtypeStruct((n_e, 128), I32)],
        grid=(t // tt,),
        in_specs=[pl.BlockSpec((n_e, tt), lambda i: (0, i)),
                  pl.BlockSpec((n_e, 1), lambda i: (0, 0))],
        out_specs=[tok, tok, tok, pl.BlockSpec((n_e, 128), lambda i: (0, 0))],
        scratch_shapes=[pltpu.VMEM((n_e, 1), F32)],
        compiler_params=_params(),
        name="router_topk",
    )(logits_t, bias.reshape(n_e, 1))


def _dest_kernel(idx_ref, rank_ref, start_ref, o_ref):
    k, tt = idx_ref.shape
    n_e = start_ref.shape[0]
    eio = lax.broadcasted_iota(I32, (n_e, tt), 0)
    start = start_ref[...]
    rows = [jnp.sum(jnp.where(eio == idx_ref[r:r + 1, :], start, 0), axis=0, keepdims=True)
            for r in range(k)]
    o_ref[...] = jnp.concatenate(rows, axis=0) + rank_ref[...]


def _dest(idx, rank, seg_start, tt):
    k, t = idx.shape
    n_e = seg_start.shape[0]
    tok = pl.BlockSpec((k, tt), lambda i: (0, i))
    return pl.pallas_call(
        _dest_kernel,
        out_shape=jax.ShapeDtypeStruct((k, t), I32),
        grid=(t // tt,),
        in_specs=[tok, tok, pl.BlockSpec((n_e, 1), lambda i: (0, 0))],
        out_specs=tok,
        compiler_params=_params(),
        name="moe_dest",
    )(idx, rank, seg_start.reshape(n_e, 1))


def _sc_mesh():
    return plsc.VectorSubcoreMesh(core_axis_name="core", subcore_axis_name="subcore")


def _sc_scatter_rows(rows, dest, n_out):
    k, t = dest.shape
    w = rows.shape[1]
    mesh = _sc_mesh()
    n_workers = mesh.num_cores * mesh.num_subcores
    win_per_worker = t // (SC_WINDOW * n_workers)
    assert win_per_worker * SC_WINDOW * n_workers == t

    @functools.partial(
        pl.kernel, out_type=jax.ShapeDtypeStruct((n_out, w), rows.dtype), mesh=mesh,
        scratch_types=[pltpu.VMEM((SC_WINDOW, w), rows.dtype)]
        + [pltpu.VMEM((1, SC_WINDOW), I32)] * k + [pltpu.SemaphoreType.DMA],
        name="moe_dispatch_sc")
    def run(rows_hbm, idx_hbm, out_hbm, rows_v, *rest):
        idx_v, sem = rest[:k], rest[k]
        worker = lax.axis_index("subcore") * mesh.num_cores + lax.axis_index("core")

        @pl.loop(0, win_per_worker)
        def _(j):
            t0 = pl.multiple_of((worker * win_per_worker + j) * SC_WINDOW, SC_WINDOW)
            pltpu.sync_copy(rows_hbm.at[pl.ds(t0, SC_WINDOW)], rows_v)
            for r in range(k):
                pltpu.sync_copy(idx_hbm.at[:, pl.ds(r * t + t0, SC_WINDOW)], idx_v[r])
            copies = [pltpu.async_copy(rows_v, out_hbm.at[idx_v[r].at[0]], sem) for r in range(k)]
            for c in copies:
                c.wait()

    return run(rows, dest.reshape(1, k * t))


def _sc_gather_rows(table, dest):
    k, t = dest.shape
    w = table.shape[1]
    mesh = _sc_mesh()
    n_workers = mesh.num_cores * mesh.num_subcores
    win_per_worker = (k * t) // (SC_WINDOW * n_workers)
    assert win_per_worker * SC_WINDOW * n_workers == k * t

    @functools.partial(
        pl.kernel, out_type=jax.ShapeDtypeStruct((k * t, w), table.dtype), mesh=mesh,
        scratch_types=[pltpu.VMEM((SC_WINDOW, w), table.dtype), pltpu.VMEM((1, SC_WINDOW), I32)],
        name="moe_gather_sc")
    def run(table_hbm, idx_hbm, out_hbm, rows_v, idx_v):
        worker = lax.axis_index("subcore") * mesh.num_cores + lax.axis_index("core")

        @pl.loop(0, win_per_worker)
        def _(j):
            p0 = pl.multiple_of((worker * win_per_worker + j) * SC_WINDOW, SC_WINDOW)
            pltpu.sync_copy(idx_hbm.at[:, pl.ds(p0, SC_WINDOW)], idx_v)
            pltpu.sync_copy(table_hbm.at[idx_v.at[0]], rows_v)
            pltpu.sync_copy(rows_v, out_hbm.at[pl.ds(p0, SC_WINDOW)])

    return run(table, dest.reshape(1, k * t))


def _expert_kernel(start_ref, nblk_ref, xs_ref, wg_ref, wu_ref, wd_ref, ys_ref,
                   xbuf, ybuf, wgb, wub, wdb, wbuf_g, wbuf_u, wbuf_d, sem_in, sem_out, sem_w):
    wbuf = (wbuf_g, wbuf_u, wbuf_d)
    e = pl.program_id(0)
    n_e = pl.num_programs(0)
    nb = nblk_ref[e]
    g0 = start_ref[e] // MOE_BLOCK
    n_used = start_ref[n_e - 1] // MOE_BLOCK + nblk_ref[n_e - 1]
    n_in, n_out = xbuf.shape[0], ybuf.shape[0]

    def rows(g):
        return pl.ds(pl.multiple_of(g * MOE_BLOCK, MOE_BLOCK), MOE_BLOCK)

    def in_copy(g):
        slot = lax.rem(g, n_in)
        return pltpu.make_async_copy(xs_ref.at[rows(g), :], xbuf.at[slot], sem_in.at[slot])

    def out_copy(g):
        slot = lax.rem(g, n_out)
        return pltpu.make_async_copy(ybuf.at[slot], ys_ref.at[rows(g), :], sem_out.at[slot])

    look = n_in - EXPERT_GROUP

    @pl.when(e == 0)
    def _():
        for g in range(look):
            @pl.when(g < n_used)
            def _():
                in_copy(g).start(priority=g % N_DMA_QUEUES)

    n_w = wbuf[0].shape[0]

    def weight_copies(ex):
        slot = lax.rem(ex, n_w)
        return [pltpu.make_async_copy(src.at[ex], buf.at[slot], sem_w.at[slot])
                for src, buf in zip((wg_ref, wu_ref, wd_ref), wbuf)]

    @pl.when(e == 0)
    def _():
        for ex in range(min(n_w, wg_ref.shape[0])):
            for c in weight_copies(ex):
                c.start()

    for c in weight_copies(e):
        c.wait()
    w_slot = lax.rem(e, n_w)

    @pl.when(nb > 0)
    def _():
        wgb[...] = wbuf[0][w_slot].astype(BF16)
        wub[...] = wbuf[1][w_slot].astype(BF16)
        wdb[...] = wbuf[2][w_slot].astype(BF16)

    @pl.when(e + n_w < n_e)
    def _():
        for c in weight_copies(e + n_w):
            c.start()

    @pl.when(nb > 0)
    def _():
        half = xbuf.shape[2]

        def swiglu(word):
            lo, hi = _unpack_halves(word)
            lo, hi = lo.astype(BF16), hi.astype(BF16)
            gate = (jnp.dot(lo, wgb[:half], preferred_element_type=F32)
                    + jnp.dot(hi, wgb[half:], preferred_element_type=F32))
            up = (jnp.dot(lo, wub[:half], preferred_element_type=F32)
                  + jnp.dot(hi, wub[half:], preferred_element_type=F32))
            act = (_silu(gate) * up).astype(BF16)
            return jnp.dot(act, wdb[...], preferred_element_type=F32)

        def process(g, m):
            for i in range(m):
                in_copy(g + i).wait()
            for i in range(m):
                @pl.when(g + look + i < n_used)
                def _():
                    in_copy(g + look + i).start(priority=i % N_DMA_QUEUES)
            ys = [swiglu(xbuf[lax.rem(g + i, n_in)]) for i in range(m)]
            for i in range(m):
                @pl.when(g + i >= n_out)
                def _():
                    out_copy(g + i - n_out).wait()

                ybuf[lax.rem(g + i, n_out)] = _pack_halves(ys[i])
                out_copy(g + i).start(priority=(i + 1) % N_DMA_QUEUES)

        def group_body(p, carry):
            process(g0 + p * EXPERT_GROUP, EXPERT_GROUP)
            return carry

        lax.fori_loop(0, nb // EXPERT_GROUP, group_body, 0)
        for m in range(1, EXPERT_GROUP):
            @pl.when(lax.rem(nb, EXPERT_GROUP) == m)
            def _():
                process(g0 + nb - m, m)

    @pl.when(e == n_e - 1)
    def _():
        for i in range(n_out):
            @pl.when(n_used - 1 - i >= 0)
            def _():
                out_copy(n_used - 1 - i).wait()


def _experts(seg_start, seg_blocks, xs, wg, wu, wd):
    n_slots, half = xs.shape
    n_e, d, de = wg.shape
    n_w = EXPERT_WEIGHT_BUFFERS
    return pl.pallas_call(
        _expert_kernel,
        out_shape=jax.ShapeDtypeStruct((n_slots, half), U32),
        grid_spec=pltpu.PrefetchScalarGridSpec(
            num_scalar_prefetch=2,
            grid=(n_e,),
            in_specs=[pl.BlockSpec(memory_space=pl.ANY)] * 4,
            out_specs=pl.BlockSpec(memory_space=pl.ANY),
            scratch_shapes=[pltpu.VMEM((EXPERT_IN_RING, MOE_BLOCK, half), U32),
                            pltpu.VMEM((EXPERT_OUT_RING, MOE_BLOCK, half), U32),
                            pltpu.VMEM((d, de), BF16), pltpu.VMEM((d, de), BF16),
                            pltpu.VMEM((de, d), BF16),
                            pltpu.VMEM((n_w, d, de), F32), pltpu.VMEM((n_w, d, de), F32),
                            pltpu.VMEM((n_w, de, d), F32),
                            pltpu.SemaphoreType.DMA((EXPERT_IN_RING,)),
                            pltpu.SemaphoreType.DMA((EXPERT_OUT_RING,)),
                            pltpu.SemaphoreType.DMA((n_w,))]),
        compiler_params=_params(),
        name="moe_experts",
    )(seg_start, seg_blocks, xs, wg, wu, wd)


def _combine_kernel(yg_ref, gt_ref, x_ref, g2_ref, fg_ref, o_ref):
    k = yg_ref.shape[0]
    gt = gt_ref[...]
    lo, hi = _unpack_halves(yg_ref[0])
    y_lo, y_hi = lo * gt[:, 0:1], hi * gt[:, 0:1]
    for r in range(1, k):
        lo, hi = _unpack_halves(yg_ref[r])
        y_lo, y_hi = y_lo + lo * gt[:, r:r + 1], y_hi + hi * gt[:, r:r + 1]
    y = jnp.concatenate([y_lo, y_hi], axis=1)
    o_ref[...] = _rms(x_ref[...] + g2_ref[0] * y, fg_ref[...])


def _combine_into_kernel(yg_ref, gt_ref, x_ref, g2_ref, fg_ref, prev_ref, o_ref):
    del prev_ref
    _combine_kernel(yg_ref, gt_ref, x_ref, g2_ref, fg_ref, o_ref)


def _combine(yg, tok0, gates_t, x1s, gate2, final_g, seq, tc, out_so_far=None):
    k, n, half = yg.shape
    t, d = x1s.shape
    b0 = tok0 // tc
    args = [yg, gates_t, x1s, gate2, final_g.reshape(1, d)]
    in_specs = [pl.BlockSpec((k, tc, half), lambda i: (0, i, 0)),
                pl.BlockSpec((tc, k), lambda i: (i + b0, 0)),
                pl.BlockSpec((tc, d), lambda i: (i + b0, 0)),
                pl.BlockSpec((1, 1, d), lambda i: (((i + b0) * tc) // seq, 0, 0)),
                pl.BlockSpec((1, d), lambda i: (0, 0))]
    aliases = {}
    kernel = _combine_kernel
    if out_so_far is not None:
        args.append(out_so_far)
        in_specs.append(pl.BlockSpec(memory_space=pl.ANY))
        aliases = {len(args) - 1: 0}
        kernel = _combine_into_kernel
    return pl.pallas_call(
        kernel,
        out_shape=jax.ShapeDtypeStruct((t, d), F32),
        grid=(n // tc,),
        in_specs=in_specs,
        out_specs=pl.BlockSpec((tc, d), lambda i: (i + b0, 0)),
        input_output_aliases=aliases,
        compiler_params=_params(),
        name="moe_combine",
    )(*args)


def _layer(x2, c, bsz, seq, lb_row, ada_w, ada_b, norm1_g, w_in, hg_norm_g, w_branch_a, w_branch_b,
           w_out, norm2_g, w_router, router_bias, w_exp_gate, w_exp_up, w_exp_down, w_sh_gate,
           w_sh_up, w_sh_down, final_g):
    t, d = x2.shape
    n_e = w_router.shape[1]
    mod = _ada(c, ada_w, ada_b).reshape(bsz, 6, 1, d)
    shift1, scale1, gate1, shift2, scale2, gate2 = (mod[:, j] for j in range(6))

    hw = hg_norm_g.shape[0]
    aw = len(ATT_GROUPS) * ATT_HEADS_PER_GROUP * ATT_HEAD_DIM
    flat_segs = [(0, hw, BF16), (hw, hw, F32), (2 * hw, hw, BF16), (3 * hw, hw, BF16),
                 (4 * hw + 3 * aw, d, BF16), (4 * hw + 3 * aw + d, d, BF16)]
    (hq, hf, hi, hg, ga, gb), qkv = _inproj(
        x2, norm1_g, scale1, shift1, w_in.astype(BF16), bsz, seq, flat_segs, 4 * hw, tm=512)

    ya = _hgrn(hq, hf, hi, hg, lb_row, hg_norm_g, bsz, seq, ts=256)
    att = [_attn_group(*qkv[3 * g:3 * g + 3], g, nq=4) for g in range(len(ATT_GROUPS))]

    x1s, hp, logits_t = _merge(
        ya, att, ga, gb, x2, gate1, scale2, shift2, gate2, norm2_g, w_branch_a.astype(BF16),
        w_branch_b.astype(BF16), w_out.astype(BF16), w_router.T, w_sh_gate.astype(BF16),
        w_sh_up.astype(BF16), w_sh_down.astype(BF16), seq, tm=512)

    idx, gates, rank, cnt = _topk(logits_t, router_bias, tt=512)
    counts = cnt[:, 0]
    padded = (counts + MOE_BLOCK - 1) // MOE_BLOCK * MOE_BLOCK
    seg_start = (jnp.cumsum(padded) - padded).astype(I32)
    n_blocks = -(-(t * TOP_K) // MOE_BLOCK) + n_e
    dest = _dest(idx, rank, seg_start, tt=512)

    xs = _sc_scatter_rows(hp, dest, n_blocks * MOE_BLOCK)
    ys = _experts(seg_start, (padded // MOE_BLOCK).astype(I32), xs, w_exp_gate, w_exp_up,
                  w_exp_down)
    out, n = None, t // COMBINE_PARTS
    for part in range(COMBINE_PARTS):
        yg = _sc_gather_rows(ys, dest[:, part * n:(part + 1) * n]).reshape(TOP_K, n, d // 2)
        out = _combine(yg, part * n, gates.T, x1s, gate2, final_g, seq, tc=256, out_so_far=out)
    return out


def kernel(x, c, ada_w, ada_b, norm1_g, w_in, lb_logits, hg_norm_g, w_branch_a, w_branch_b, w_out,
           norm2_g, w_router, router_bias, w_exp_gate, w_exp_up, w_exp_down, w_sh_gate, w_sh_up,
           w_sh_down, final_g):
    bsz, seq, d = x.shape
    depth = ada_w.shape[0]
    assert depth == 1, "the last layer's kernels also apply the final norm"
    lb_table = jnp.cumsum(jax.nn.softmax(lb_logits.astype(F32), axis=0), axis=0)
    out = _layer(x.reshape(bsz * seq, d), c, bsz, seq, lb_table[0], ada_w[0], ada_b[0], norm1_g[0],
                 w_in[0], hg_norm_g[0], w_branch_a[0], w_branch_b[0], w_out[0], norm2_g[0],
                 w_router[0], router_bias[0], w_exp_gate[0], w_exp_up[0], w_exp_down[0],
                 w_sh_gate[0], w_sh_up[0], w_sh_down[0], final_g)
    return out.reshape(bsz, seq, d)
```

```python
import functools

import jax
import jax.numpy as jnp
from jax import lax
from jax.experimental import pallas as pl
from jax.experimental.pallas import tpu as pltpu
from jax.experimental.pallas import tpu_sc as plsc

F32 = jnp.float32
BF16 = jnp.bfloat16
I32 = jnp.int32
U32 = jnp.uint32
HIGHEST = lax.Precision.HIGHEST

HG_HEADS = 4
HG_BLOCK = 16
HG_CHUNK = 32
HG_MILD_DECAY = -80.0
ATT_GROUPS = ((128, 1), (512, 4), (2048, 16))
ATT_HEADS_PER_GROUP = 4
ATT_HEAD_DIM = 64
TOP_K = 8
ROUTE_SCALE = 2.5
MOE_BLOCK = 256
RMS_EPS = 1e-6
N_DMA_QUEUES = 2
SC_WINDOW = 128
COMBINE_PARTS = 2
EXPERT_WEIGHT_BUFFERS = 3
EXPERT_GROUP = 4
EXPERT_IN_RING = 8
EXPERT_OUT_RING = 6

LANES = 128
VMEM_LIMIT_BYTES = 56 * 1024 * 1024


def _sigmoid(x):
    return 1.0 / (1.0 + jnp.exp(-x))


def _silu(x):
    return x * _sigmoid(x)


def _rms(x, g):
    return x * lax.rsqrt(jnp.mean(x * x, axis=-1, keepdims=True) + RMS_EPS) * g


def _pack_halves(x):
    n = x.shape[1] // 2
    bits = lax.bitcast_convert_type(x.astype(BF16).astype(F32), U32)
    return (bits[:, :n] >> 16) | (bits[:, n:] & jnp.uint32(0xFFFF0000))


def _unpack_halves(word):
    lo = lax.bitcast_convert_type(word << 16, F32)
    hi = lax.bitcast_convert_type(word & jnp.uint32(0xFFFF0000), F32)
    return lo, hi


def _params(n_axes=1):
    return pltpu.CompilerParams(
        dimension_semantics=("arbitrary",) * n_axes, vmem_limit_bytes=VMEM_LIMIT_BYTES)


def _ada_kernel(c_ref, w_ref, b_ref, o_ref):
    sc = _silu(c_ref[...])
    o_ref[...] = jnp.dot(sc, w_ref[...], preferred_element_type=F32, precision=HIGHEST) + b_ref[...]


def _ada(c, w, b):
    bsz, d = c.shape
    n = w.shape[1]
    return pl.pallas_call(
        _ada_kernel,
        out_shape=jax.ShapeDtypeStruct((bsz, n), F32),
        grid=(n // d,),
        in_specs=[pl.BlockSpec((bsz, d), lambda j: (0, 0)),
                  pl.BlockSpec((d, d), lambda j: (0, j)),
                  pl.BlockSpec((1, d), lambda j: (0, j))],
        out_specs=pl.BlockSpec((bsz, d), lambda j: (0, j)),
        compiler_params=_params(),
        name="ada_mod",
    )(c, w, b.reshape(1, n))


def _inproj_kernel(n_flat, flat_ranges, att_c0, x_ref, g_ref, sc_ref, sh_ref, w_ref, *refs):
    flat_refs, att_refs, scr = refs[:n_flat], refs[n_flat:-1], refs[-1]
    tm = x_ref.shape[0]
    h = _rms(x_ref[...], g_ref[...]) * (1.0 + sc_ref[0]) + sh_ref[0]
    hb = h.astype(BF16)
    for (c0, c1), o_ref in zip(flat_ranges, flat_refs):
        o_ref[...] = jnp.dot(hb, w_ref[:, c0:c1], preferred_element_type=F32).astype(o_ref.dtype)
    gw = ATT_HEADS_PER_GROUP * ATT_HEAD_DIM
    n_groups = len(ATT_GROUPS)
    for part in range(3):
        c0 = att_c0 + part * n_groups * gw
        res = jnp.dot(hb, w_ref[:, c0:c0 + n_groups * gw], preferred_element_type=F32)
        if part == 0:
            res = res * (ATT_HEAD_DIM ** -0.5)
        for g, (_, dil) in enumerate(ATT_GROUPS):
            o_ref = att_refs[g * 3 + part]
            sub = res[:, g * gw:(g + 1) * gw]
            if dil == 1:
                o_ref[0, 0] = sub.astype(BF16)
            else:
                for c in range(gw // LANES):
                    scr[c] = sub[:, c * LANES:(c + 1) * LANES]
                for r in range(dil):
                    o_ref[0, r] = jnp.concatenate(
                        [scr[c, pl.ds(r, tm // dil, stride=dil), :] for c in range(gw // LANES)],
                        axis=1).astype(BF16)


def _inproj(x2, g, scale, shift, w_bf16, bsz, seq, flat_segs, att_c0, tm):
    t, d = x2.shape
    gw = ATT_HEADS_PER_GROUP * ATT_HEAD_DIM
    n_per = seq // tm
    per_b = lambda i: (i // n_per, 0, 0)
    att_shapes, att_specs = [], []
    for _, dil in ATT_GROUPS:
        for _ in range(3):
            att_shapes.append(jax.ShapeDtypeStruct((bsz, dil, seq // dil, gw), BF16))
            att_specs.append(pl.BlockSpec((1, dil, tm // dil, gw),
                                          lambda i: (i // n_per, 0, i % n_per, 0)))
    outs = pl.pallas_call(
        functools.partial(_inproj_kernel, len(flat_segs),
                          tuple((c0, c0 + wdt) for c0, wdt, _ in flat_segs), att_c0),
        out_shape=[jax.ShapeDtypeStruct((t, wdt), dt) for _, wdt, dt in flat_segs] + att_shapes,
        grid=(t // tm,),
        in_specs=[pl.BlockSpec((tm, d), lambda i: (i, 0)),
                  pl.BlockSpec((1, d), lambda i: (0, 0)),
                  pl.BlockSpec((1, 1, d), per_b),
                  pl.BlockSpec((1, 1, d), per_b),
                  pl.BlockSpec(w_bf16.shape, lambda i: (0, 0))],
        out_specs=[pl.BlockSpec((tm, wdt), lambda i: (i, 0)) for _, wdt, _ in flat_segs]
        + att_specs,
        scratch_shapes=[pltpu.VMEM((gw // LANES, tm, LANES), F32)],
        compiler_params=_params(),
        name="in_proj",
    )(x2, g.reshape(1, d), scale, shift, w_bf16)
    return outs[:len(flat_segs)], outs[len(flat_segs):]


def _hgrn_kernel(ts, q_ref, f_ref, v_ref, gt_ref, lb_ref, ng_ref, o_ref, st_ref, b_ref):
    dk = q_ref.shape[1] // HG_HEADS
    n_chunks = ts // HG_CHUNK
    n_blk = HG_CHUNK // HG_BLOCK

    @pl.when(pl.program_id(1) == 0)
    def _():
        st_ref[...] = jnp.zeros_like(st_ref)

    row = lax.broadcasted_iota(I32, (LANES, LANES), 0)
    col = lax.broadcasted_iota(I32, (LANES, LANES), 1)
    same_chunk = (row // HG_CHUNK) == (col // HG_CHUNK)
    cum_mat = jnp.where(same_chunk & (col <= row), 1.0, 0.0).astype(BF16)

    def chunk_cumsum(x):
        out = []
        for r0 in range(0, ts, LANES):
            rest = x[r0:r0 + LANES]
            acc = None
            for _ in range(3):
                term = rest.astype(BF16)
                part = jnp.dot(cum_mat, term, preferred_element_type=F32)
                acc = part if acc is None else acc + part
                rest = rest - term.astype(F32)
            out.append(acc)
        return jnp.concatenate(out, axis=0)

    def forget(cs):
        lb = lb_ref[:, cs]
        return lb + (1.0 - lb) * _sigmoid(f_ref[:, cs])

    b_min = None
    for h in range(HG_HEADS):
        cs = slice(h * dk, (h + 1) * dk)
        b = chunk_cumsum(jnp.log(forget(cs)))
        b_ref[:, cs] = b
        m = jnp.min(b)
        b_min = m if b_min is None else jnp.minimum(b_min, m)
    mild = b_min >= HG_MILD_DECAY

    def finish(h, o, st):
        cs = slice(h * dk, (h + 1) * dk)
        st_ref[h] = st
        y = _rms(o, ng_ref[:, cs]) * _silu(gt_ref[:, cs].astype(F32))
        o_ref[:, cs] = y.astype(o_ref.dtype)

    @pl.when(mild)
    def _():
        span = 2 * HG_CHUNK
        causal = (lax.broadcasted_iota(I32, (span, span), 0)
                  >= lax.broadcasted_iota(I32, (span, span), 1))
        nt = lambda x, y: lax.dot_general(x, y, (((1,), (1,)), ((), ())),
                                          preferred_element_type=F32)
        for h in range(HG_HEADS):
            cs = slice(h * dk, (h + 1) * dk)
            v = v_ref[:, cs]
            b = b_ref[:, cs]
            q = q_ref[:, cs].astype(F32)
            k = 1.0 - forget(cs)
            st = st_ref[h]
            o_rows = []
            for r0 in range(0, ts, span):
                sl = slice(r0, r0 + span)
                b_first, b_second = b[r0:r0 + HG_CHUNK], b[r0 + HG_CHUNK:r0 + span]
                end_first = b_first[HG_CHUNK - 1:HG_CHUNK]
                end_second = b_second[HG_CHUNK - 1:HG_CHUNK]
                e = jnp.exp(jnp.concatenate([b_first - end_first, b_second], axis=0))
                qe = (q[sl] * e).astype(BF16)
                ke = k[sl] / e
                a = jnp.where(causal, nt(qe, ke.astype(BF16)), 0.0).astype(BF16)
                st_in = (st * jnp.exp(end_first)).astype(BF16)
                o_rows.append(jnp.dot(a, v[sl], preferred_element_type=F32) + nt(qe, st_in))
                kend = (ke * jnp.exp(end_second)).astype(BF16)
                vt = v[sl].astype(F32).T.astype(BF16)
                st = (st * jnp.exp(end_first + end_second)
                      + jnp.dot(vt, kend, preferred_element_type=F32))
            finish(h, jnp.concatenate(o_rows, axis=0), st)

    @pl.when(jnp.logical_not(mild))
    def _():
        _hgrn_steep(ts, dk, n_chunks, n_blk, q_ref, v_ref, b_ref, st_ref, forget, finish)


def _hgrn_steep(ts, dk, n_chunks, n_blk, q_ref, v_ref, b_ref, st_ref, forget, finish):
    t_in_blk = lax.broadcasted_iota(I32, (ts, dk), 0) % HG_BLOCK

    for h in range(HG_HEADS):
        cs = slice(h * dk, (h + 1) * dk)
        q = q_ref[:, cs].astype(F32)
        v = v_ref[:, cs].astype(F32)
        k = 1.0 - forget(cs)
        b = b_ref[:, cs]

        o = jnp.sum(q * k, axis=-1, keepdims=True) * v
        for d in range(1, HG_BLOCK):
            k_d = pltpu.roll(k, d, axis=0)
            b_d = pltpu.roll(b, d, axis=0)
            v_d = pltpu.roll(v, d, axis=0)
            w = jnp.sum(q * k_d * jnp.exp(jnp.minimum(b - b_d, 0.0)), axis=-1, keepdims=True)
            o = o + jnp.where(t_in_blk >= d, w * v_d, 0.0)

        st = st_ref[h]
        o_rows = []
        for c in range(n_chunks):
            r0 = c * HG_CHUNK
            bc = b[r0:r0 + HG_CHUNK]
            qc = q[r0:r0 + HG_CHUNK]
            kc = k[r0:r0 + HG_CHUNK]
            vc = v[r0:r0 + HG_CHUNK].astype(BF16)
            st_b = st.astype(BF16)
            for i in range(n_blk):
                i0 = i * HG_BLOCK
                if i == 0:
                    qt = qc[:HG_BLOCK] * jnp.exp(bc[:HG_BLOCK])
                    qs = qt
                else:
                    ref_row = bc[i0 - 1:i0]
                    qt = qc[i0:i0 + HG_BLOCK] * jnp.exp(bc[i0:i0 + HG_BLOCK] - ref_row)
                    qs = qt * jnp.exp(ref_row)
                oi = lax.dot_general(qs.astype(BF16), st_b, (((1,), (1,)), ((), ())),
                                     preferred_element_type=F32)
                if i > 0:
                    kh = kc[:i0] * jnp.exp(ref_row - bc[:i0])
                    a = lax.dot_general(qt.astype(BF16), kh.astype(BF16), (((1,), (1,)), ((), ())),
                                        preferred_element_type=F32)
                    oi = oi + jnp.dot(a.astype(BF16), vc[:i0], preferred_element_type=F32)
                o_rows.append(oi)
            b_end = bc[HG_CHUNK - 1:HG_CHUNK]
            kend = kc * jnp.exp(b_end - bc)
            vt = v[r0:r0 + HG_CHUNK].T.astype(BF16)
            st = st * jnp.exp(b_end) + jnp.dot(vt, kend.astype(BF16), preferred_element_type=F32)
        finish(h, o + jnp.concatenate(o_rows, axis=0), st)


def _hgrn(hq, hf, hi, hg, lb, ng, bsz, seq, ts):
    t, w = hq.shape
    dk = w // HG_HEADS
    n_s = seq // ts
    tile = lambda b, s: (b * n_s + s, 0)
    return pl.pallas_call(
        functools.partial(_hgrn_kernel, ts),
        out_shape=jax.ShapeDtypeStruct((t, w), BF16),
        grid=(bsz, n_s),
        in_specs=[pl.BlockSpec((ts, w), tile)] * 4
        + [pl.BlockSpec((1, w), lambda b, s: (0, 0))] * 2,
        out_specs=pl.BlockSpec((ts, w), tile),
        scratch_shapes=[pltpu.VMEM((HG_HEADS, dk, dk), F32), pltpu.VMEM((ts, w), F32)],
        compiler_params=_params(2),
        name="hgrn2",
    )(hq, hf, hi, hg, lb.reshape(1, w), ng.reshape(1, w))


def _attn_kernel(nk, nq, q_ref, kp_ref, kc_ref, vp_ref, vc_ref, o_ref, lse_ref):
    n = pl.program_id(2)
    e = ATT_HEAD_DIM
    i = lax.broadcasted_iota(I32, (nk, 2 * nk), 0)
    j = lax.broadcasted_iota(I32, (nk, 2 * nk), 1)
    band = (j >= i) & (j <= i + nk)
    kk = jnp.concatenate([kp_ref[0, 0], kc_ref[0, 0]], axis=0)
    vv = jnp.concatenate([vp_ref[0, 0], vc_ref[0, 0]], axis=0)
    first_head = lax.broadcasted_iota(I32, (nk, LANES), 1) < e
    zero = jnp.zeros((), q_ref.dtype)
    for b in range(nq):
        valid = band & ((j >= nk) | (n * nq + b > 0))
        rows = slice(b * nk, (b + 1) * nk)
        for c in range(0, ATT_HEADS_PER_GROUP * e, LANES):
            q = q_ref[0, 0, rows, c:c + LANES]
            kb = kk[b * nk:(b + 2) * nk, c:c + LANES]
            vb = vv[b * nk:(b + 2) * nk, c:c + LANES]
            outs, lses = [], []
            for keep in (first_head, jnp.logical_not(first_head)):
                s = lax.dot_general(jnp.where(keep, q, zero), kb, (((1,), (1,)), ((), ())),
                                    preferred_element_type=F32)
                s = jnp.where(valid, s, -jnp.inf)
                m = jnp.max(s, axis=-1, keepdims=True)
                p = jnp.exp(s - m)
                l = jnp.sum(p, axis=-1, keepdims=True)
                outs.append(jnp.dot(p.astype(BF16), vb, preferred_element_type=F32) / l)
                lses.append(m + jnp.log(l))
            o_ref[0, 0, rows, c:c + LANES] = jnp.where(first_head, outs[0], outs[1])
            lse_ref[0, 0, rows, c:c + LANES] = jnp.where(first_head, lses[0], lses[1])


def _attn_group(q, k, v, g, nq):
    window, dil = ATT_GROUPS[g]
    nk = window // dil
    bsz, _, ln, gw = q.shape
    nq = min(nq, ln // nk)
    assert ln % (nk * nq) == 0 and 2 * ATT_HEAD_DIM == LANES
    cur = pl.BlockSpec((1, 1, nq * nk, gw), lambda b, r, n: (b, r, n, 0))
    prev = pl.BlockSpec((1, 1, nk, gw), lambda b, r, n: (b, r, jnp.maximum(n * nq - 1, 0), 0))
    return pl.pallas_call(
        functools.partial(_attn_kernel, nk, nq),
        out_shape=[jax.ShapeDtypeStruct(q.shape, F32)] * 2,
        grid=(bsz, dil, ln // (nk * nq)),
        in_specs=[cur, prev, cur, prev, cur],
        out_specs=[cur, cur],
        compiler_params=_params(3),
        name=f"dilated_attn_g{g}",
    )(q, k, k, v, v)


def _token_major(ref, scr):
    dil, rows = ref.shape[1], ref.shape[2]
    if dil == 1:
        return ref[0, 0]
    n_col = scr.shape[0]
    for r in range(dil):
        for c in range(n_col):
            scr[c, pl.ds(r, rows, stride=dil), :] = ref[0, r, :, c * LANES:(c + 1) * LANES]
    return jnp.concatenate([scr[c] for c in range(n_col)], axis=1)


def _merge_kernel(ya_ref, o0_ref, o1_ref, o2_ref, l0_ref, l1_ref, l2_ref, ga_ref, gb_ref, x_ref,
                  g1_ref, sc2_ref, sh2_ref, g2_ref, n2_ref, wa_ref, wb_ref, wo_ref, wr_ref, wrl_ref,
                  wsg_ref, wsu_ref, wsd_ref, bias_ref, x1_ref, hp_ref, idx_ref, gate_ref, rank_ref,
                  cnt_ref, carry_ref, lg_ref, *scr):
    step = pl.program_id(0)

    @pl.when(step == 0)
    def _():
        carry_ref[...] = jnp.zeros_like(carry_ref)
        lg_ref[...] = jnp.zeros_like(lg_ref)

    _route(lg_ref[...], jnp.where(step > 0, 1.0, 0.0), bias_ref, idx_ref, gate_ref, rank_ref,
           cnt_ref, carry_ref)

    l0, l1, l2 = (_token_major(r, s) for r, s in zip((l0_ref, l1_ref, l2_ref), scr[:3]))
    o0, o1, o2 = (_token_major(r, s) for r, s in zip((o0_ref, o1_ref, o2_ref), scr[3:]))
    m = jnp.maximum(jnp.maximum(l0, l1), l2)
    e0, e1, e2 = jnp.exp(l0 - m), jnp.exp(l1 - m), jnp.exp(l2 - m)
    yb = (e0 * o0 + e1 * o1 + e2 * o2) / (e0 + e1 + e2)
    merged = (_sigmoid(ga_ref[...].astype(F32))
              * jnp.dot(ya_ref[...], wa_ref[...], preferred_element_type=F32)
              + _sigmoid(gb_ref[...].astype(F32))
              * jnp.dot(yb.astype(BF16), wb_ref[...], preferred_element_type=F32))
    x1 = x_ref[...] + g1_ref[0] * jnp.dot(merged.astype(BF16), wo_ref[...],
                                           preferred_element_type=F32)
    h2 = _rms(x1, n2_ref[...]) * (1.0 + sc2_ref[0]) + sh2_ref[0]
    hb = h2.astype(BF16)
    act = (_silu(jnp.dot(hb, wsg_ref[...], preferred_element_type=F32))
           * jnp.dot(hb, wsu_ref[...], preferred_element_type=F32))
    shared = jnp.dot(act.astype(BF16), wsd_ref[...], preferred_element_type=F32)
    x1_ref[...] = x1 + g2_ref[0] * shared
    hp_ref[...] = _pack_halves(h2)
    h_lo = (h2 - hb.astype(F32)).astype(BF16)
    nt = lambda a, b: lax.dot_general(a, b, (((1,), (1,)), ((), ())), preferred_element_type=F32)
    lg_ref[...] = nt(wr_ref[...], hb) + (nt(wr_ref[...], h_lo) + nt(wrl_ref[...], hb))


def _merge(ya, att, ga, gb, x2, gate1, scale2, shift2, gate2, norm2_g, wa, wb, wo, wr_t, wsg, wsu,
           wsd, router_bias, seq, tm):
    t, d = x2.shape
    n_e = wr_t.shape[0]
    wr_hi = wr_t.astype(BF16)
    wr_lo = (wr_t - wr_hi.astype(F32)).astype(BF16)
    n_per = seq // tm
    n_tiles = t // tm
    tile = lambda i: jnp.minimum(i, n_tiles - 1)
    per_b = lambda i: (tile(i) // n_per, 0, 0)
    rows = lambda wdt: pl.BlockSpec((tm, wdt), lambda i: (tile(i), 0))
    full = lambda a: pl.BlockSpec(a.shape, lambda i: (0,) * a.ndim)
    vec = pl.BlockSpec((1, 1, d), per_b)
    (o0, l0), (o1, l1), (o2, l2) = att
    gw = o0.shape[3]
    by_residue = lambda a: pl.BlockSpec((1, a.shape[1], tm // a.shape[1], gw),
                                        lambda i: (tile(i) // n_per, 0, tile(i) % n_per, 0))
    att_in = (o0, o1, o2, l0, l1, l2)
    bias_col = router_bias.reshape(n_e, 1)
    tok = pl.BlockSpec((TOP_K, tm), lambda i: (0, jnp.maximum(i - 1, 0)))
    return pl.pallas_call(
        _merge_kernel,
        out_shape=[jax.ShapeDtypeStruct((t, d), F32),
                   jax.ShapeDtypeStruct((t, d // 2), U32),
                   jax.ShapeDtypeStruct((TOP_K, t), I32), jax.ShapeDtypeStruct((TOP_K, t), F32),
                   jax.ShapeDtypeStruct((TOP_K, t), I32), jax.ShapeDtypeStruct((n_e, LANES), I32)],
        grid=(n_tiles + 1,),
        in_specs=[rows(ya.shape[1])] + [by_residue(a) for a in att_in] + [rows(d)] * 3
        + [vec, vec, vec, vec, pl.BlockSpec((1, d), lambda i: (0, 0))]
        + [full(a) for a in (wa, wb, wo, wr_hi, wr_lo, wsg, wsu, wsd, bias_col)],
        out_specs=[rows(d), rows(d // 2), tok, tok, tok,
                   pl.BlockSpec((n_e, LANES), lambda i: (0, 0))],
        scratch_shapes=[pltpu.VMEM((n_e, 1), F32), pltpu.VMEM((n_e, tm), F32)]
        + [pltpu.VMEM((gw // LANES, tm, LANES), F32)] * 6,
        compiler_params=_params(),
        name="merge_router",
    )(ya, *att_in, ga, gb, x2, gate1, scale2, shift2, gate2,
      norm2_g.reshape(1, d), wa, wb, wo, wr_hi, wr_lo, wsg, wsu, wsd, bias_col)


def _route(logits, live, bias_ref, idx_ref, gate_ref, rank_ref, cnt_ref, carry_ref):
    n_e, tt = logits.shape
    scores = _sigmoid(logits)
    sel = scores + bias_ref[...]
    eio = lax.broadcasted_iota(I32, (n_e, tt), 0)
    picked = jnp.zeros((n_e, tt), F32)
    idxs, vals = [], []
    for _ in range(TOP_K):
        m = jnp.max(sel, axis=0, keepdims=True)
        ik = jnp.min(jnp.where(sel == m, eio, n_e), axis=0, keepdims=True)
        hit = eio == ik
        vals.append(jnp.sum(jnp.where(hit, scores, 0.0), axis=0, keepdims=True))
        sel = jnp.where(hit, -jnp.inf, sel)
        picked = picked + jnp.where(hit, 1.0, 0.0)
        idxs.append(ik)
    denom = vals[0]
    for v in vals[1:]:
        denom = denom + v
    gate_ref[...] = jnp.concatenate([v / denom * ROUTE_SCALE for v in vals], axis=0)
    idx_ref[...] = jnp.concatenate(idxs, axis=0)

    upper = (lax.broadcasted_iota(I32, (tt, tt), 0) <= lax.broadcasted_iota(I32, (tt, tt), 1))
    incl = jnp.dot(picked.astype(BF16), jnp.where(upper, 1.0, 0.0).astype(BF16),
                   preferred_element_type=F32)
    before = incl - picked + carry_ref[...]
    rank_ref[...] = jnp.concatenate(
        [jnp.sum(jnp.where(eio == ik, before, 0.0), axis=0, keepdims=True) for ik in idxs],
        axis=0).astype(I32)
    carry_ref[...] = carry_ref[...] + jnp.sum(picked, axis=1, keepdims=True) * live
    cnt_ref[...] = jnp.broadcast_to(carry_ref[...], cnt_ref.shape).astype(I32)


def _dest_kernel(idx_ref, rank_ref, start_ref, o_ref):
    k, tt = idx_ref.shape
    n_e = start_ref.shape[0]
    eio = lax.broadcasted_iota(I32, (n_e, tt), 0)
    start = start_ref[...]
    rows = [jnp.sum(jnp.where(eio == idx_ref[r:r + 1, :], start, 0), axis=0, keepdims=True)
            for r in range(k)]
    o_ref[...] = jnp.concatenate(rows, axis=0) + rank_ref[...]


def _dest(idx, rank, seg_start, tt):
    k, t = idx.shape
    n_e = seg_start.shape[0]
    tok = pl.BlockSpec((k, tt), lambda i: (0, i))
    return pl.pallas_call(
        _dest_kernel,
        out_shape=jax.ShapeDtypeStruct((k, t), I32),
        grid=(t // tt,),
        in_specs=[tok, tok, pl.BlockSpec((n_e, 1), lambda i: (0, 0))],
        out_specs=tok,
        compiler_params=_params(),
        name="moe_dest",
    )(idx, rank, seg_start.reshape(n_e, 1))


def _sc_mesh():
    return plsc.VectorSubcoreMesh(core_axis_name="core", subcore_axis_name="subcore")


def _sc_scatter_rows(rows, dest, n_out):
    k, t = dest.shape
    w = rows.shape[1]
    mesh = _sc_mesh()
    n_workers = mesh.num_cores * mesh.num_subcores
    win_per_worker = t // (SC_WINDOW * n_workers)
    assert win_per_worker * SC_WINDOW * n_workers == t

    @functools.partial(
        pl.kernel, out_type=jax.ShapeDtypeStruct((n_out, w), rows.dtype), mesh=mesh,
        scratch_types=[pltpu.VMEM((SC_WINDOW, w), rows.dtype)]
        + [pltpu.VMEM((1, SC_WINDOW), I32)] * k + [pltpu.SemaphoreType.DMA],
        name="moe_dispatch_sc")
    def run(rows_hbm, idx_hbm, out_hbm, rows_v, *rest):
        idx_v, sem = rest[:k], rest[k]
        worker = lax.axis_index("subcore") * mesh.num_cores + lax.axis_index("core")

        @pl.loop(0, win_per_worker)
        def _(j):
            t0 = pl.multiple_of((worker * win_per_worker + j) * SC_WINDOW, SC_WINDOW)
            pltpu.sync_copy(rows_hbm.at[pl.ds(t0, SC_WINDOW)], rows_v)
            for r in range(k):
                pltpu.sync_copy(idx_hbm.at[:, pl.ds(r * t + t0, SC_WINDOW)], idx_v[r])
            copies = [pltpu.async_copy(rows_v, out_hbm.at[idx_v[r].at[0]], sem) for r in range(k)]
            for c in copies:
                c.wait()

    return run(rows, dest.reshape(1, k * t))


def _sc_gather_rows(table, dest):
    k, t = dest.shape
    w = table.shape[1]
    mesh = _sc_mesh()
    n_workers = mesh.num_cores * mesh.num_subcores
    win_per_worker = (k * t) // (SC_WINDOW * n_workers)
    assert win_per_worker * SC_WINDOW * n_workers == k * t

    @functools.partial(
        pl.kernel, out_type=jax.ShapeDtypeStruct((k * t, w), table.dtype), mesh=mesh,
        scratch_types=[pltpu.VMEM((SC_WINDOW, w), table.dtype), pltpu.VMEM((1, SC_WINDOW), I32)],
        name="moe_gather_sc")
    def run(table_hbm, idx_hbm, out_hbm, rows_v, idx_v):
        worker = lax.axis_index("subcore") * mesh.num_cores + lax.axis_index("core")

        @pl.loop(0, win_per_worker)
        def _(j):
            p0 = pl.multiple_of((worker * win_per_worker + j) * SC_WINDOW, SC_WINDOW)
            pltpu.sync_copy(idx_hbm.at[:, pl.ds(p0, SC_WINDOW)], idx_v)
            pltpu.sync_copy(table_hbm.at[idx_v.at[0]], rows_v)
            pltpu.sync_copy(rows_v, out_hbm.at[pl.ds(p0, SC_WINDOW)])

    return run(table, dest.reshape(1, k * t))


def _expert_kernel(start_ref, nblk_ref, xs_ref, wg_ref, wu_ref, wd_ref, ys_ref,
                   xbuf, ybuf, wgb, wub, wdb, wbuf_g, wbuf_u, wbuf_d, sem_in, sem_out, sem_w):
    wbuf = (wbuf_g, wbuf_u, wbuf_d)
    e = pl.program_id(0)
    n_e = pl.num_programs(0)
    nb = nblk_ref[e]
    g0 = start_ref[e] // MOE_BLOCK
    n_used = start_ref[n_e - 1] // MOE_BLOCK + nblk_ref[n_e - 1]
    n_in, n_out = xbuf.shape[0], ybuf.shape[0]

    def rows(g):
        return pl.ds(pl.multiple_of(g * MOE_BLOCK, MOE_BLOCK), MOE_BLOCK)

    def in_copy(g):
        slot = lax.rem(g, n_in)
        return pltpu.make_async_copy(xs_ref.at[rows(g), :], xbuf.at[slot], sem_in.at[slot])

    def out_copy(g):
        slot = lax.rem(g, n_out)
        return pltpu.make_async_copy(ybuf.at[slot], ys_ref.at[rows(g), :], sem_out.at[slot])

    look = n_in - EXPERT_GROUP

    @pl.when(e == 0)
    def _():
        for g in range(look):
            @pl.when(g < n_used)
            def _():
                in_copy(g).start(priority=g % N_DMA_QUEUES)

    n_w = wbuf[0].shape[0]

    def weight_copies(ex):
        slot = lax.rem(ex, n_w)
        return [pltpu.make_async_copy(src.at[ex], buf.at[slot], sem_w.at[slot])
                for src, buf in zip((wg_ref, wu_ref, wd_ref), wbuf)]

    @pl.when(e == 0)
    def _():
        for ex in range(min(n_w, wg_ref.shape[0])):
            for c in weight_copies(ex):
                c.start()

    for c in weight_copies(e):
        c.wait()
    w_slot = lax.rem(e, n_w)

    @pl.when(nb > 0)
    def _():
        wgb[...] = wbuf[0][w_slot].astype(BF16)
        wub[...] = wbuf[1][w_slot].astype(BF16)
        wdb[...] = wbuf[2][w_slot].astype(BF16)

    @pl.when(e + n_w < n_e)
    def _():
        for c in weight_copies(e + n_w):
            c.start()

    @pl.when(nb > 0)
    def _():
        half = xbuf.shape[2]

        def swiglu(word):
            lo, hi = _unpack_halves(word)
            lo, hi = lo.astype(BF16), hi.astype(BF16)
            gate = (jnp.dot(lo, wgb[:half], preferred_element_type=F32)
                    + jnp.dot(hi, wgb[half:], preferred_element_type=F32))
            up = (jnp.dot(lo, wub[:half], preferred_element_type=F32)
                  + jnp.dot(hi, wub[half:], preferred_element_type=F32))
            act = (_silu(gate) * up).astype(BF16)
            return jnp.dot(act, wdb[...], preferred_element_type=F32)

        def process(g, m):
            for i in range(m):
                in_copy(g + i).wait()
            for i in range(m):
                @pl.when(g + look + i < n_used)
                def _():
                    in_copy(g + look + i).start(priority=i % N_DMA_QUEUES)
            ys = [swiglu(xbuf[lax.rem(g + i, n_in)]) for i in range(m)]
            for i in range(m):
                @pl.when(g + i >= n_out)
                def _():
                    out_copy(g + i - n_out).wait()

                ybuf[lax.rem(g + i, n_out)] = _pack_halves(ys[i])
                out_copy(g + i).start(priority=(i + 1) % N_DMA_QUEUES)

        def group_body(p, carry):
            process(g0 + p * EXPERT_GROUP, EXPERT_GROUP)
            return carry

        lax.fori_loop(0, nb // EXPERT_GROUP, group_body, 0)
        for m in range(1, EXPERT_GROUP):
            @pl.when(lax.rem(nb, EXPERT_GROUP) == m)
            def _():
                process(g0 + nb - m, m)

    @pl.when(e == n_e - 1)
    def _():
        for i in range(n_out):
            @pl.when(n_used - 1 - i >= 0)
            def _():
                out_copy(n_used - 1 - i).wait()


def _experts(seg_start, seg_blocks, xs, wg, wu, wd):
    n_slots, half = xs.shape
    n_e, d, de = wg.shape
    n_w = EXPERT_WEIGHT_BUFFERS
    return pl.pallas_call(
        _expert_kernel,
        out_shape=jax.ShapeDtypeStruct((n_slots, half), U32),
        grid_spec=pltpu.PrefetchScalarGridSpec(
            num_scalar_prefetch=2,
            grid=(n_e,),
            in_specs=[pl.BlockSpec(memory_space=pl.ANY)] * 4,
            out_specs=pl.BlockSpec(memory_space=pl.ANY),
            scratch_shapes=[pltpu.VMEM((EXPERT_IN_RING, MOE_BLOCK, half), U32),
                            pltpu.VMEM((EXPERT_OUT_RING, MOE_BLOCK, half), U32),
                            pltpu.VMEM((d, de), BF16), pltpu.VMEM((d, de), BF16),
                            pltpu.VMEM((de, d), BF16),
                            pltpu.VMEM((n_w, d, de), F32), pltpu.VMEM((n_w, d, de), F32),
                            pltpu.VMEM((n_w, de, d), F32),
                            pltpu.SemaphoreType.DMA((EXPERT_IN_RING,)),
                            pltpu.SemaphoreType.DMA((EXPERT_OUT_RING,)),
                            pltpu.SemaphoreType.DMA((n_w,))]),
        compiler_params=_params(),
        name="moe_experts",
    )(seg_start, seg_blocks, xs, wg, wu, wd)


def _combine_kernel(yg_ref, gt_ref, x_ref, g2_ref, fg_ref, o_ref):
    k = yg_ref.shape[0]
    gt = gt_ref[...]
    lo, hi = _unpack_halves(yg_ref[0])
    y_lo, y_hi = lo * gt[:, 0:1], hi * gt[:, 0:1]
    for r in range(1, k):
        lo, hi = _unpack_halves(yg_ref[r])
        y_lo, y_hi = y_lo + lo * gt[:, r:r + 1], y_hi + hi * gt[:, r:r + 1]
    y = jnp.concatenate([y_lo, y_hi], axis=1)
    o_ref[...] = _rms(x_ref[...] + g2_ref[0] * y, fg_ref[...])


def _combine_into_kernel(yg_ref, gt_ref, x_ref, g2_ref, fg_ref, prev_ref, o_ref):
    del prev_ref
    _combine_kernel(yg_ref, gt_ref, x_ref, g2_ref, fg_ref, o_ref)


def _combine(yg, tok0, gates_t, x1s, gate2, final_g, seq, tc, out_so_far=None):
    k, n, half = yg.shape
    t, d = x1s.shape
    b0 = tok0 // tc
    args = [yg, gates_t, x1s, gate2, final_g.reshape(1, d)]
    in_specs = [pl.BlockSpec((k, tc, half), lambda i: (0, i, 0)),
                pl.BlockSpec((tc, k), lambda i: (i + b0, 0)),
                pl.BlockSpec((tc, d), lambda i: (i + b0, 0)),
                pl.BlockSpec((1, 1, d), lambda i: (((i + b0) * tc) // seq, 0, 0)),
                pl.BlockSpec((1, d), lambda i: (0, 0))]
    aliases = {}
    kernel = _combine_kernel
    if out_so_far is not None:
        args.append(out_so_far)
        in_specs.append(pl.BlockSpec(memory_space=pl.ANY))
        aliases = {len(args) - 1: 0}
        kernel = _combine_into_kernel
    return pl.pallas_call(
        kernel,
        out_shape=jax.ShapeDtypeStruct((t, d), F32),
        grid=(n // tc,),
        in_specs=in_specs,
        out_specs=pl.BlockSpec((tc, d), lambda i: (i + b0, 0)),
        input_output_aliases=aliases,
        compiler_params=_params(),
        name="moe_combine",
    )(*args)


def _layer(x2, c, bsz, seq, lb_row, ada_w, ada_b, norm1_g, w_in, hg_norm_g, w_branch_a, w_branch_b,
           w_out, norm2_g, w_router, router_bias, w_exp_gate, w_exp_up, w_exp_down, w_sh_gate,
           w_sh_up, w_sh_down, final_g):
    t, d = x2.shape
    n_e = w_router.shape[1]
    mod = _ada(c, ada_w, ada_b).reshape(bsz, 6, 1, d)
    shift1, scale1, gate1, shift2, scale2, gate2 = (mod[:, j] for j in range(6))

    hw = hg_norm_g.shape[0]
    aw = len(ATT_GROUPS) * ATT_HEADS_PER_GROUP * ATT_HEAD_DIM
    flat_segs = [(0, hw, BF16), (hw, hw, F32), (2 * hw, hw, BF16), (3 * hw, hw, BF16),
                 (4 * hw + 3 * aw, d, BF16), (4 * hw + 3 * aw + d, d, BF16)]
    (hq, hf, hi, hg, ga, gb), qkv = _inproj(
        x2, norm1_g, scale1, shift1, w_in.astype(BF16), bsz, seq, flat_segs, 4 * hw, tm=512)

    ya = _hgrn(hq, hf, hi, hg, lb_row, hg_norm_g, bsz, seq, ts=256)
    att = [_attn_group(*qkv[3 * g:3 * g + 3], g, nq=4) for g in range(len(ATT_GROUPS))]

    x1s, hp, idx, gates, rank, cnt = _merge(
        ya, att, ga, gb, x2, gate1, scale2, shift2, gate2, norm2_g, w_branch_a.astype(BF16),
        w_branch_b.astype(BF16), w_out.astype(BF16), w_router.T, w_sh_gate.astype(BF16),
        w_sh_up.astype(BF16), w_sh_down.astype(BF16), router_bias, seq, tm=512)
    counts = cnt[:, 0]
    padded = (counts + MOE_BLOCK - 1) // MOE_BLOCK * MOE_BLOCK
    seg_start = (jnp.cumsum(padded) - padded).astype(I32)
    n_blocks = -(-(t * TOP_K) // MOE_BLOCK) + n_e
    dest = _dest(idx, rank, seg_start, tt=512)

    xs = _sc_scatter_rows(hp, dest, n_blocks * MOE_BLOCK)
    ys = _experts(seg_start, (padded // MOE_BLOCK).astype(I32), xs, w_exp_gate, w_exp_up,
                  w_exp_down)
    out, n = None, t // COMBINE_PARTS
    for part in range(COMBINE_PARTS):
        yg = _sc_gather_rows(ys, dest[:, part * n:(part + 1) * n]).reshape(TOP_K, n, d // 2)
        out = _combine(yg, part * n, gates.T, x1s, gate2, final_g, seq, tc=256, out_so_far=out)
    return out


def kernel(x, c, ada_w, ada_b, norm1_g, w_in, lb_logits, hg_norm_g, w_branch_a, w_branch_b, w_out,
           norm2_g, w_router, router_bias, w_exp_gate, w_exp_up, w_exp_down, w_sh_gate, w_sh_up,
           w_sh_down, final_g):
    bsz, seq, d = x.shape
    depth = ada_w.shape[0]
    assert depth == 1, "the last layer's kernels also apply the final norm"
    lb_table = jnp.cumsum(jax.nn.softmax(lb_logits.astype(F32), axis=0), axis=0)
    out = _layer(x.reshape(bsz * seq, d), c, bsz, seq, lb_table[0], ada_w[0], ada_b[0], norm1_g[0],
                 w_in[0], hg_norm_g[0], w_branch_a[0], w_branch_b[0], w_out[0], norm2_g[0],
                 w_router[0], router_bias[0], w_exp_gate[0], w_exp_up[0], w_exp_down[0],
                 w_sh_gate[0], w_sh_up[0], w_sh_down[0], final_g)
    return out.reshape(bsz, seq, d)
```

```python
import functools

import jax
import jax.numpy as jnp
from jax import lax
from jax.experimental import pallas as pl
from jax.experimental.pallas import tpu as pltpu
from jax.experimental.pallas import tpu_sc as plsc

F32 = jnp.float32
BF16 = jnp.bfloat16
I32 = jnp.int32
U32 = jnp.uint32
HIGHEST = lax.Precision.HIGHEST

HG_HEADS = 4
HG_BLOCK = 16
HG_CHUNK = 32
HG_MILD_DECAY = -80.0
ATT_GROUPS = ((128, 1), (512, 4), (2048, 16))
ATT_HEADS_PER_GROUP = 4
ATT_HEAD_DIM = 64
TOP_K = 8
ROUTE_SCALE = 2.5
MOE_BLOCK = 256
RMS_EPS = 1e-6
N_DMA_QUEUES = 2
SC_WINDOW = 128
COMBINE_PARTS = 2
EXPERT_WEIGHT_BUFFERS = 3
EXPERT_GROUP = 4
EXPERT_IN_RING = 8
EXPERT_OUT_RING = 6

LANES = 128
VMEM_LIMIT_BYTES = 56 * 1024 * 1024


def _sigmoid(x):
    return 1.0 / (1.0 + jnp.exp(-x))


def _silu(x):
    return x * _sigmoid(x)


def _rms(x, g):
    return x * lax.rsqrt(jnp.mean(x * x, axis=-1, keepdims=True) + RMS_EPS) * g


def _pack_halves(x):
    n = x.shape[1] // 2
    bits = lax.bitcast_convert_type(x.astype(BF16).astype(F32), U32)
    return (bits[:, :n] >> 16) | (bits[:, n:] & jnp.uint32(0xFFFF0000))


def _unpack_halves(word):
    lo = lax.bitcast_convert_type(word << 16, F32)
    hi = lax.bitcast_convert_type(word & jnp.uint32(0xFFFF0000), F32)
    return lo, hi


def _params(n_axes=1):
    return pltpu.CompilerParams(
        dimension_semantics=("arbitrary",) * n_axes, vmem_limit_bytes=VMEM_LIMIT_BYTES)


def _ada_kernel(c_ref, w_ref, b_ref, o_ref):
    sc = _silu(c_ref[...])
    o_ref[...] = jnp.dot(sc, w_ref[...], preferred_element_type=F32, precision=HIGHEST) + b_ref[...]


def _ada(c, w, b):
    bsz, d = c.shape
    n = w.shape[1]
    return pl.pallas_call(
        _ada_kernel,
        out_shape=jax.ShapeDtypeStruct((bsz, n), F32),
        grid=(n // d,),
        in_specs=[pl.BlockSpec((bsz, d), lambda j: (0, 0)),
                  pl.BlockSpec((d, d), lambda j: (0, j)),
                  pl.BlockSpec((1, d), lambda j: (0, j))],
        out_specs=pl.BlockSpec((bsz, d), lambda j: (0, j)),
        compiler_params=_params(),
        name="ada_mod",
    )(c, w, b.reshape(1, n))


def _inproj_kernel(n_flat, flat_ranges, att_c0, x_ref, g_ref, sc_ref, sh_ref, w_ref, *refs):
    flat_refs, att_refs, scr = refs[:n_flat], refs[n_flat:-1], refs[-1]
    tm = x_ref.shape[0]
    h = _rms(x_ref[...], g_ref[...]) * (1.0 + sc_ref[0]) + sh_ref[0]
    hb = h.astype(BF16)
    for (c0, c1), o_ref in zip(flat_ranges, flat_refs):
        o_ref[...] = jnp.dot(hb, w_ref[:, c0:c1], preferred_element_type=F32).astype(o_ref.dtype)
    gw = ATT_HEADS_PER_GROUP * ATT_HEAD_DIM
    n_groups = len(ATT_GROUPS)
    for part in range(3):
        c0 = att_c0 + part * n_groups * gw
        res = jnp.dot(hb, w_ref[:, c0:c0 + n_groups * gw], preferred_element_type=F32)
        if part == 0:
            res = res * (ATT_HEAD_DIM ** -0.5)
        for g, (_, dil) in enumerate(ATT_GROUPS):
            o_ref = att_refs[g * 3 + part]
            sub = res[:, g * gw:(g + 1) * gw]
            if dil == 1:
                o_ref[0, 0] = sub.astype(BF16)
            else:
                for c in range(gw // LANES):
                    scr[c] = sub[:, c * LANES:(c + 1) * LANES]
                for r in range(dil):
                    o_ref[0, r] = jnp.concatenate(
                        [scr[c, pl.ds(r, tm // dil, stride=dil), :] for c in range(gw // LANES)],
                        axis=1).astype(BF16)


def _inproj(x2, g, scale, shift, w_bf16, bsz, seq, flat_segs, att_c0, tm):
    t, d = x2.shape
    gw = ATT_HEADS_PER_GROUP * ATT_HEAD_DIM
    n_per = seq // tm
    per_b = lambda i: (i // n_per, 0, 0)
    att_shapes, att_specs = [], []
    for _, dil in ATT_GROUPS:
        for _ in range(3):
            att_shapes.append(jax.ShapeDtypeStruct((bsz, dil, seq // dil, gw), BF16))
            att_specs.append(pl.BlockSpec((1, dil, tm // dil, gw),
                                          lambda i: (i // n_per, 0, i % n_per, 0)))
    outs = pl.pallas_call(
        functools.partial(_inproj_kernel, len(flat_segs),
                          tuple((c0, c0 + wdt) for c0, wdt, _ in flat_segs), att_c0),
        out_shape=[jax.ShapeDtypeStruct((t, wdt), dt) for _, wdt, dt in flat_segs] + att_shapes,
        grid=(t // tm,),
        in_specs=[pl.BlockSpec((tm, d), lambda i: (i, 0)),
                  pl.BlockSpec((1, d), lambda i: (0, 0)),
                  pl.BlockSpec((1, 1, d), per_b),
                  pl.BlockSpec((1, 1, d), per_b),
                  pl.BlockSpec(w_bf16.shape, lambda i: (0, 0))],
        out_specs=[pl.BlockSpec((tm, wdt), lambda i: (i, 0)) for _, wdt, _ in flat_segs]
        + att_specs,
        scratch_shapes=[pltpu.VMEM((gw // LANES, tm, LANES), F32)],
        compiler_params=_params(),
        name="in_proj",
    )(x2, g.reshape(1, d), scale, shift, w_bf16)
    return outs[:len(flat_segs)], outs[len(flat_segs):]


def _hgrn_kernel(ts, q_ref, f_ref, v_ref, gt_ref, lb_ref, ng_ref, o_ref, st_ref, b_ref):
    dk = q_ref.shape[1] // HG_HEADS
    n_chunks = ts // HG_CHUNK
    n_blk = HG_CHUNK // HG_BLOCK

    @pl.when(pl.program_id(1) == 0)
    def _():
        st_ref[...] = jnp.zeros_like(st_ref)

    row = lax.broadcasted_iota(I32, (LANES, LANES), 0)
    col = lax.broadcasted_iota(I32, (LANES, LANES), 1)
    same_chunk = (row // HG_CHUNK) == (col // HG_CHUNK)
    cum_mat = jnp.where(same_chunk & (col <= row), 1.0, 0.0).astype(BF16)

    def chunk_cumsum(x):
        out = []
        for r0 in range(0, ts, LANES):
            rest = x[r0:r0 + LANES]
            acc = None
            for _ in range(3):
                term = rest.astype(BF16)
                part = jnp.dot(cum_mat, term, preferred_element_type=F32)
                acc = part if acc is None else acc + part
                rest = rest - term.astype(F32)
            out.append(acc)
        return jnp.concatenate(out, axis=0)

    def forget(cs):
        lb = lb_ref[:, cs]
        return lb + (1.0 - lb) * _sigmoid(f_ref[:, cs])

    b_min = None
    for h in range(HG_HEADS):
        cs = slice(h * dk, (h + 1) * dk)
        b = chunk_cumsum(jnp.log(forget(cs)))
        b_ref[:, cs] = b
        m = jnp.min(b)
        b_min = m if b_min is None else jnp.minimum(b_min, m)
    mild = b_min >= HG_MILD_DECAY

    def finish(h, o, st):
        cs = slice(h * dk, (h + 1) * dk)
        st_ref[h] = st
        y = _rms(o, ng_ref[:, cs]) * _silu(gt_ref[:, cs].astype(F32))
        o_ref[:, cs] = y.astype(o_ref.dtype)

    @pl.when(mild)
    def _():
        span = 2 * HG_CHUNK
        causal = (lax.broadcasted_iota(I32, (span, span), 0)
                  >= lax.broadcasted_iota(I32, (span, span), 1))
        nt = lambda x, y: lax.dot_general(x, y, (((1,), (1,)), ((), ())),
                                          preferred_element_type=F32)
        for h in range(HG_HEADS):
            cs = slice(h * dk, (h + 1) * dk)
            v = v_ref[:, cs]
            b = b_ref[:, cs]
            q = q_ref[:, cs].astype(F32)
            k = 1.0 - forget(cs)
            st = st_ref[h]
            o_rows = []
            for r0 in range(0, ts, span):
                sl = slice(r0, r0 + span)
                b_first, b_second = b[r0:r0 + HG_CHUNK], b[r0 + HG_CHUNK:r0 + span]
                end_first = b_first[HG_CHUNK - 1:HG_CHUNK]
                end_second = b_second[HG_CHUNK - 1:HG_CHUNK]
                e = jnp.exp(jnp.concatenate([b_first - end_first, b_second], axis=0))
                qe = (q[sl] * e).astype(BF16)
                ke = k[sl] / e
                a = jnp.where(causal, nt(qe, ke.astype(BF16)), 0.0).astype(BF16)
                st_in = (st * jnp.exp(end_first)).astype(BF16)
                o_rows.append(jnp.dot(a, v[sl], preferred_element_type=F32) + nt(qe, st_in))
                kend = (ke * jnp.exp(end_second)).astype(BF16)
                vt = v[sl].astype(F32).T.astype(BF16)
                st = (st * jnp.exp(end_first + end_second)
                      + jnp.dot(vt, kend, preferred_element_type=F32))
            finish(h, jnp.concatenate(o_rows, axis=0), st)

    @pl.when(jnp.logical_not(mild))
    def _():
        _hgrn_steep(ts, dk, n_chunks, n_blk, q_ref, v_ref, b_ref, st_ref, forget, finish)


def _hgrn_steep(ts, dk, n_chunks, n_blk, q_ref, v_ref, b_ref, st_ref, forget, finish):
    t_in_blk = lax.broadcasted_iota(I32, (ts, dk), 0) % HG_BLOCK

    for h in range(HG_HEADS):
        cs = slice(h * dk, (h + 1) * dk)
        q = q_ref[:, cs].astype(F32)
        v = v_ref[:, cs].astype(F32)
        k = 1.0 - forget(cs)
        b = b_ref[:, cs]

        o = jnp.sum(q * k, axis=-1, keepdims=True) * v
        for d in range(1, HG_BLOCK):
            k_d = pltpu.roll(k, d, axis=0)
            b_d = pltpu.roll(b, d, axis=0)
            v_d = pltpu.roll(v, d, axis=0)
            w = jnp.sum(q * k_d * jnp.exp(jnp.minimum(b - b_d, 0.0)), axis=-1, keepdims=True)
            o = o + jnp.where(t_in_blk >= d, w * v_d, 0.0)

        st = st_ref[h]
        o_rows = []
        for c in range(n_chunks):
            r0 = c * HG_CHUNK
            bc = b[r0:r0 + HG_CHUNK]
            qc = q[r0:r0 + HG_CHUNK]
            kc = k[r0:r0 + HG_CHUNK]
            vc = v[r0:r0 + HG_CHUNK].astype(BF16)
            st_b = st.astype(BF16)
            for i in range(n_blk):
                i0 = i * HG_BLOCK
                if i == 0:
                    qt = qc[:HG_BLOCK] * jnp.exp(bc[:HG_BLOCK])
                    qs = qt
                else:
                    ref_row = bc[i0 - 1:i0]
                    qt = qc[i0:i0 + HG_BLOCK] * jnp.exp(bc[i0:i0 + HG_BLOCK] - ref_row)
                    qs = qt * jnp.exp(ref_row)
                oi = lax.dot_general(qs.astype(BF16), st_b, (((1,), (1,)), ((), ())),
                                     preferred_element_type=F32)
                if i > 0:
                    kh = kc[:i0] * jnp.exp(ref_row - bc[:i0])
                    a = lax.dot_general(qt.astype(BF16), kh.astype(BF16), (((1,), (1,)), ((), ())),
                                        preferred_element_type=F32)
                    oi = oi + jnp.dot(a.astype(BF16), vc[:i0], preferred_element_type=F32)
                o_rows.append(oi)
            b_end = bc[HG_CHUNK - 1:HG_CHUNK]
            kend = kc * jnp.exp(b_end - bc)
            vt = v[r0:r0 + HG_CHUNK].T.astype(BF16)
            st = st * jnp.exp(b_end) + jnp.dot(vt, kend.astype(BF16), preferred_element_type=F32)
        finish(h, o + jnp.concatenate(o_rows, axis=0), st)


def _hgrn(hq, hf, hi, hg, lb, ng, bsz, seq, ts):
    t, w = hq.shape
    dk = w // HG_HEADS
    n_s = seq // ts
    tile = lambda b, s: (b * n_s + s, 0)
    return pl.pallas_call(
        functools.partial(_hgrn_kernel, ts),
        out_shape=jax.ShapeDtypeStruct((t, w), BF16),
        grid=(bsz, n_s),
        in_specs=[pl.BlockSpec((ts, w), tile)] * 4
        + [pl.BlockSpec((1, w), lambda b, s: (0, 0))] * 2,
        out_specs=pl.BlockSpec((ts, w), tile),
        scratch_shapes=[pltpu.VMEM((HG_HEADS, dk, dk), F32), pltpu.VMEM((ts, w), F32)],
        compiler_params=_params(2),
        name="hgrn2",
    )(hq, hf, hi, hg, lb.reshape(1, w), ng.reshape(1, w))


def _attn_kernel(nk, nq, nr, q_ref, kp_ref, kc_ref, vp_ref, vc_ref, o_ref, lse_ref):
    n = pl.program_id(2)
    e = ATT_HEAD_DIM
    i = lax.broadcasted_iota(I32, (nk, 2 * nk), 0)
    j = lax.broadcasted_iota(I32, (nk, 2 * nk), 1)
    band = (j >= i) & (j <= i + nk)
    first_head = lax.broadcasted_iota(I32, (nk, LANES), 1) < e
    zero = jnp.zeros((), q_ref.dtype)
    for r in range(nr):
        kk = jnp.concatenate([kp_ref[0, r], kc_ref[0, r]], axis=0)
        vv = jnp.concatenate([vp_ref[0, r], vc_ref[0, r]], axis=0)
        for b in range(nq):
            valid = band & ((j >= nk) | (n * nq + b > 0))
            rows = slice(b * nk, (b + 1) * nk)
            for c in range(0, ATT_HEADS_PER_GROUP * e, LANES):
                q = q_ref[0, r, rows, c:c + LANES]
                kb = kk[b * nk:(b + 2) * nk, c:c + LANES]
                vb = vv[b * nk:(b + 2) * nk, c:c + LANES]
                outs, lses = [], []
                for keep in (first_head, jnp.logical_not(first_head)):
                    s = lax.dot_general(jnp.where(keep, q, zero), kb, (((1,), (1,)), ((), ())),
                                        preferred_element_type=F32)
                    s = jnp.where(valid, s, -jnp.inf)
                    m = jnp.max(s, axis=-1, keepdims=True)
                    p = jnp.exp(s - m)
                    l = jnp.sum(p, axis=-1, keepdims=True)
                    outs.append(jnp.dot(p.astype(BF16), vb, preferred_element_type=F32) / l)
                    lses.append(m + jnp.log(l))
                o_ref[0, r, rows, c:c + LANES] = jnp.where(first_head, outs[0], outs[1])
                lse_ref[0, r, rows, c:c + LANES] = jnp.where(first_head, lses[0], lses[1])


def _attn_group(q, k, v, g, blocks_per_step):
    window, dil = ATT_GROUPS[g]
    nk = window // dil
    bsz, _, ln, gw = q.shape
    nq = min(blocks_per_step, ln // nk)
    nr = min(blocks_per_step // nq, dil)
    assert ln % (nk * nq) == 0 and dil % nr == 0 and 2 * ATT_HEAD_DIM == LANES
    cur = pl.BlockSpec((1, nr, nq * nk, gw), lambda b, r, n: (b, r, n, 0))
    prev = pl.BlockSpec((1, nr, nk, gw), lambda b, r, n: (b, r, jnp.maximum(n * nq - 1, 0), 0))
    return pl.pallas_call(
        functools.partial(_attn_kernel, nk, nq, nr),
        out_shape=[jax.ShapeDtypeStruct(q.shape, F32)] * 2,
        grid=(bsz, dil // nr, ln // (nk * nq)),
        in_specs=[cur, prev, cur, prev, cur],
        out_specs=[cur, cur],
        compiler_params=_params(3),
        name=f"dilated_attn_g{g}",
    )(q, k, k, v, v)


def _token_major(ref, scr):
    dil, rows = ref.shape[1], ref.shape[2]
    if dil == 1:
        return ref[0, 0]
    n_col = scr.shape[0]
    for r in range(dil):
        for c in range(n_col):
            scr[c, pl.ds(r, rows, stride=dil), :] = ref[0, r, :, c * LANES:(c + 1) * LANES]
    return jnp.concatenate([scr[c] for c in range(n_col)], axis=1)


def _merge_kernel(ya_ref, o0_ref, o1_ref, o2_ref, l0_ref, l1_ref, l2_ref, ga_ref, gb_ref, x_ref,
                  g1_ref, sc2_ref, sh2_ref, g2_ref, n2_ref, wa_ref, wb_ref, wo_ref, wr_ref, wrl_ref,
                  wsg_ref, wsu_ref, wsd_ref, bias_ref, x1_ref, hp_ref, idx_ref, gate_ref, rank_ref,
                  cnt_ref, carry_ref, lg_ref, *scr):
    step = pl.program_id(0)

    @pl.when(step == 0)
    def _():
        carry_ref[...] = jnp.zeros_like(carry_ref)
        lg_ref[...] = jnp.zeros_like(lg_ref)

    _route(lg_ref[...], jnp.where(step > 0, 1.0, 0.0), bias_ref, idx_ref, gate_ref, rank_ref,
           cnt_ref, carry_ref)

    l0, l1, l2 = (_token_major(r, s) for r, s in zip((l0_ref, l1_ref, l2_ref), scr[:3]))
    o0, o1, o2 = (_token_major(r, s) for r, s in zip((o0_ref, o1_ref, o2_ref), scr[3:]))
    m = jnp.maximum(jnp.maximum(l0, l1), l2)
    e0, e1, e2 = jnp.exp(l0 - m), jnp.exp(l1 - m), jnp.exp(l2 - m)
    yb = (e0 * o0 + e1 * o1 + e2 * o2) / (e0 + e1 + e2)
    merged = (_sigmoid(ga_ref[...].astype(F32))
              * jnp.dot(ya_ref[...], wa_ref[...], preferred_element_type=F32)
              + _sigmoid(gb_ref[...].astype(F32))
              * jnp.dot(yb.astype(BF16), wb_ref[...], preferred_element_type=F32))
    x1 = x_ref[...] + g1_ref[0] * jnp.dot(merged.astype(BF16), wo_ref[...],
                                           preferred_element_type=F32)
    h2 = _rms(x1, n2_ref[...]) * (1.0 + sc2_ref[0]) + sh2_ref[0]
    hb = h2.astype(BF16)
    act = (_silu(jnp.dot(hb, wsg_ref[...], preferred_element_type=F32))
           * jnp.dot(hb, wsu_ref[...], preferred_element_type=F32))
    shared = jnp.dot(act.astype(BF16), wsd_ref[...], preferred_element_type=F32)
    x1_ref[...] = x1 + g2_ref[0] * shared
    hp_ref[...] = _pack_halves(h2)
    h_lo = (h2 - hb.astype(F32)).astype(BF16)
    nt = lambda a, b: lax.dot_general(a, b, (((1,), (1,)), ((), ())), preferred_element_type=F32)
    lg_ref[...] = nt(wr_ref[...], hb) + (nt(wr_ref[...], h_lo) + nt(wrl_ref[...], hb))


def _merge(ya, att, ga, gb, x2, gate1, scale2, shift2, gate2, norm2_g, wa, wb, wo, wr_t, wsg, wsu,
           wsd, router_bias, seq, tm):
    t, d = x2.shape
    n_e = wr_t.shape[0]
    wr_hi = wr_t.astype(BF16)
    wr_lo = (wr_t - wr_hi.astype(F32)).astype(BF16)
    n_per = seq // tm
    n_tiles = t // tm
    tile = lambda i: jnp.minimum(i, n_tiles - 1)
    per_b = lambda i: (tile(i) // n_per, 0, 0)
    rows = lambda wdt: pl.BlockSpec((tm, wdt), lambda i: (tile(i), 0))
    full = lambda a: pl.BlockSpec(a.shape, lambda i: (0,) * a.ndim)
    vec = pl.BlockSpec((1, 1, d), per_b)
    (o0, l0), (o1, l1), (o2, l2) = att
    gw = o0.shape[3]
    by_residue = lambda a: pl.BlockSpec((1, a.shape[1], tm // a.shape[1], gw),
                                        lambda i: (tile(i) // n_per, 0, tile(i) % n_per, 0))
    att_in = (o0, o1, o2, l0, l1, l2)
    bias_col = router_bias.reshape(n_e, 1)
    tok = pl.BlockSpec((TOP_K, tm), lambda i: (0, jnp.maximum(i - 1, 0)))
    return pl.pallas_call(
        _merge_kernel,
        out_shape=[jax.ShapeDtypeStruct((t, d), F32),
                   jax.ShapeDtypeStruct((t, d // 2), U32),
                   jax.ShapeDtypeStruct((TOP_K, t), I32), jax.ShapeDtypeStruct((TOP_K, t), F32),
                   jax.ShapeDtypeStruct((TOP_K, t), I32), jax.ShapeDtypeStruct((n_e, LANES), I32)],
        grid=(n_tiles + 1,),
        in_specs=[rows(ya.shape[1])] + [by_residue(a) for a in att_in] + [rows(d)] * 3
        + [vec, vec, vec, vec, pl.BlockSpec((1, d), lambda i: (0, 0))]
        + [full(a) for a in (wa, wb, wo, wr_hi, wr_lo, wsg, wsu, wsd, bias_col)],
        out_specs=[rows(d), rows(d // 2), tok, tok, tok,
                   pl.BlockSpec((n_e, LANES), lambda i: (0, 0))],
        scratch_shapes=[pltpu.VMEM((n_e, 1), F32), pltpu.VMEM((n_e, tm), F32)]
        + [pltpu.VMEM((gw // LANES, tm, LANES), F32)] * 6,
        compiler_params=_params(),
        name="merge_router",
    )(ya, *att_in, ga, gb, x2, gate1, scale2, shift2, gate2,
      norm2_g.reshape(1, d), wa, wb, wo, wr_hi, wr_lo, wsg, wsu, wsd, bias_col)


def _route(logits, live, bias_ref, idx_ref, gate_ref, rank_ref, cnt_ref, carry_ref):
    n_e, tt = logits.shape
    scores = _sigmoid(logits)
    sel = scores + bias_ref[...]
    eio = lax.broadcasted_iota(I32, (n_e, tt), 0)
    picked = jnp.zeros((n_e, tt), F32)
    idxs, vals = [], []
    for _ in range(TOP_K):
        m = jnp.max(sel, axis=0, keepdims=True)
        ik = jnp.min(jnp.where(sel == m, eio, n_e), axis=0, keepdims=True)
        hit = eio == ik
        vals.append(jnp.sum(jnp.where(hit, scores, 0.0), axis=0, keepdims=True))
        sel = jnp.where(hit, -jnp.inf, sel)
        picked = picked + jnp.where(hit, 1.0, 0.0)
        idxs.append(ik)
    denom = vals[0]
    for v in vals[1:]:
        denom = denom + v
    gate_ref[...] = jnp.concatenate([v / denom * ROUTE_SCALE for v in vals], axis=0)
    idx_ref[...] = jnp.concatenate(idxs, axis=0)

    upper = (lax.broadcasted_iota(I32, (tt, tt), 0) <= lax.broadcasted_iota(I32, (tt, tt), 1))
    incl = jnp.dot(picked.astype(BF16), jnp.where(upper, 1.0, 0.0).astype(BF16),
                   preferred_element_type=F32)
    before = incl - picked + carry_ref[...]
    rank_ref[...] = jnp.concatenate(
        [jnp.sum(jnp.where(eio == ik, before, 0.0), axis=0, keepdims=True) for ik in idxs],
        axis=0).astype(I32)
    carry_ref[...] = carry_ref[...] + jnp.sum(picked, axis=1, keepdims=True) * live
    cnt_ref[...] = jnp.broadcast_to(carry_ref[...], cnt_ref.shape).astype(I32)


def _dest_kernel(idx_ref, rank_ref, start_ref, o_ref):
    k, tt = idx_ref.shape
    n_e = start_ref.shape[0]
    eio = lax.broadcasted_iota(I32, (n_e, tt), 0)
    start = start_ref[...]
    rows = [jnp.sum(jnp.where(eio == idx_ref[r:r + 1, :], start, 0), axis=0, keepdims=True)
            for r in range(k)]
    o_ref[...] = jnp.concatenate(rows, axis=0) + rank_ref[...]


def _dest(idx, rank, seg_start, tt):
    k, t = idx.shape
    n_e = seg_start.shape[0]
    tok = pl.BlockSpec((k, tt), lambda i: (0, i))
    return pl.pallas_call(
        _dest_kernel,
        out_shape=jax.ShapeDtypeStruct((k, t), I32),
        grid=(t // tt,),
        in_specs=[tok, tok, pl.BlockSpec((n_e, 1), lambda i: (0, 0))],
        out_specs=tok,
        compiler_params=_params(),
        name="moe_dest",
    )(idx, rank, seg_start.reshape(n_e, 1))


def _sc_mesh():
    return plsc.VectorSubcoreMesh(core_axis_name="core", subcore_axis_name="subcore")


def _sc_scatter_rows(rows, dest, n_out):
    k, t = dest.shape
    w = rows.shape[1]
    mesh = _sc_mesh()
    n_workers = mesh.num_cores * mesh.num_subcores
    win_per_worker = t // (SC_WINDOW * n_workers)
    assert win_per_worker * SC_WINDOW * n_workers == t

    @functools.partial(
        pl.kernel, out_type=jax.ShapeDtypeStruct((n_out, w), rows.dtype), mesh=mesh,
        scratch_types=[pltpu.VMEM((SC_WINDOW, w), rows.dtype)]
        + [pltpu.VMEM((1, SC_WINDOW), I32)] * k + [pltpu.SemaphoreType.DMA],
        name="moe_dispatch_sc")
    def run(rows_hbm, idx_hbm, out_hbm, rows_v, *rest):
        idx_v, sem = rest[:k], rest[k]
        worker = lax.axis_index("subcore") * mesh.num_cores + lax.axis_index("core")

        @pl.loop(0, win_per_worker)
        def _(j):
            t0 = pl.multiple_of((worker * win_per_worker + j) * SC_WINDOW, SC_WINDOW)
            pltpu.sync_copy(rows_hbm.at[pl.ds(t0, SC_WINDOW)], rows_v)
            for r in range(k):
                pltpu.sync_copy(idx_hbm.at[:, pl.ds(r * t + t0, SC_WINDOW)], idx_v[r])
            copies = [pltpu.async_copy(rows_v, out_hbm.at[idx_v[r].at[0]], sem) for r in range(k)]
            for c in copies:
                c.wait()

    return run(rows, dest.reshape(1, k * t))


def _sc_gather_rows(table, dest):
    k, t = dest.shape
    w = table.shape[1]
    mesh = _sc_mesh()
    n_workers = mesh.num_cores * mesh.num_subcores
    win_per_worker = (k * t) // (SC_WINDOW * n_workers)
    assert win_per_worker * SC_WINDOW * n_workers == k * t

    @functools.partial(
        pl.kernel, out_type=jax.ShapeDtypeStruct((k * t, w), table.dtype), mesh=mesh,
        scratch_types=[pltpu.VMEM((SC_WINDOW, w), table.dtype), pltpu.VMEM((1, SC_WINDOW), I32)],
        name="moe_gather_sc")
    def run(table_hbm, idx_hbm, out_hbm, rows_v, idx_v):
        worker = lax.axis_index("subcore") * mesh.num_cores + lax.axis_index("core")

        @pl.loop(0, win_per_worker)
        def _(j):
            p0 = pl.multiple_of((worker * win_per_worker + j) * SC_WINDOW, SC_WINDOW)
            pltpu.sync_copy(idx_hbm.at[:, pl.ds(p0, SC_WINDOW)], idx_v)
            pltpu.sync_copy(table_hbm.at[idx_v.at[0]], rows_v)
            pltpu.sync_copy(rows_v, out_hbm.at[pl.ds(p0, SC_WINDOW)])

    return run(table, dest.reshape(1, k * t))


def _expert_kernel(start_ref, nblk_ref, xs_ref, wg_ref, wu_ref, wd_ref, ys_ref,
                   xbuf, ybuf, wgb, wub, wdb, wbuf_g, wbuf_u, wbuf_d, sem_in, sem_out, sem_w):
    wbuf = (wbuf_g, wbuf_u, wbuf_d)
    e = pl.program_id(0)
    n_e = pl.num_programs(0)
    nb = nblk_ref[e]
    g0 = start_ref[e] // MOE_BLOCK
    n_used = start_ref[n_e - 1] // MOE_BLOCK + nblk_ref[n_e - 1]
    n_in, n_out = xbuf.shape[0], ybuf.shape[0]

    def rows(g):
        return pl.ds(pl.multiple_of(g * MOE_BLOCK, MOE_BLOCK), MOE_BLOCK)

    def in_copy(g):
        slot = lax.rem(g, n_in)
        return pltpu.make_async_copy(xs_ref.at[rows(g), :], xbuf.at[slot], sem_in.at[slot])

    def out_copy(g):
        slot = lax.rem(g, n_out)
        return pltpu.make_async_copy(ybuf.at[slot], ys_ref.at[rows(g), :], sem_out.at[slot])

    look = n_in - EXPERT_GROUP

    @pl.when(e == 0)
    def _():
        for g in range(look):
            @pl.when(g < n_used)
            def _():
                in_copy(g).start(priority=g % N_DMA_QUEUES)

    n_w = wbuf[0].shape[0]

    def weight_copies(ex):
        slot = lax.rem(ex, n_w)
        return [pltpu.make_async_copy(src.at[ex], buf.at[slot], sem_w.at[slot])
                for src, buf in zip((wg_ref, wu_ref, wd_ref), wbuf)]

    @pl.when(e == 0)
    def _():
        for ex in range(min(n_w, wg_ref.shape[0])):
            for c in weight_copies(ex):
                c.start()

    for c in weight_copies(e):
        c.wait()
    w_slot = lax.rem(e, n_w)

    @pl.when(nb > 0)
    def _():
        wgb[...] = wbuf[0][w_slot].astype(BF16)
        wub[...] = wbuf[1][w_slot].astype(BF16)
        wdb[...] = wbuf[2][w_slot].astype(BF16)

    @pl.when(e + n_w < n_e)
    def _():
        for c in weight_copies(e + n_w):
            c.start()

    @pl.when(nb > 0)
    def _():
        def swiglu(word):
            lo, hi = _unpack_halves(word)
            x = jnp.concatenate([lo.astype(BF16), hi.astype(BF16)], axis=1)
            gate = jnp.dot(x, wgb[...], preferred_element_type=F32)
            up = jnp.dot(x, wub[...], preferred_element_type=F32)
            act = (_silu(gate) * up).astype(BF16)
            return jnp.dot(act, wdb[...], preferred_element_type=F32)

        def process(g, m):
            for i in range(m):
                in_copy(g + i).wait()
            for i in range(m):
                @pl.when(g + look + i < n_used)
                def _():
                    in_copy(g + look + i).start(priority=i % N_DMA_QUEUES)
            ys = [swiglu(xbuf[lax.rem(g + i, n_in)]) for i in range(m)]
            for i in range(m):
                @pl.when(g + i >= n_out)
                def _():
                    out_copy(g + i - n_out).wait()

                ybuf[lax.rem(g + i, n_out)] = _pack_halves(ys[i])
                out_copy(g + i).start(priority=(i + 1) % N_DMA_QUEUES)

        def group_body(p, carry):
            process(g0 + p * EXPERT_GROUP, EXPERT_GROUP)
            return carry

        lax.fori_loop(0, nb // EXPERT_GROUP, group_body, 0)
        for m in range(1, EXPERT_GROUP):
            @pl.when(lax.rem(nb, EXPERT_GROUP) == m)
            def _():
                process(g0 + nb - m, m)

    @pl.when(e == n_e - 1)
    def _():
        for i in range(n_out):
            @pl.when(n_used - 1 - i >= 0)
            def _():
                out_copy(n_used - 1 - i).wait()


def _experts(seg_start, seg_blocks, xs, wg, wu, wd):
    n_slots, half = xs.shape
    n_e, d, de = wg.shape
    n_w = EXPERT_WEIGHT_BUFFERS
    return pl.pallas_call(
        _expert_kernel,
        out_shape=jax.ShapeDtypeStruct((n_slots, half), U32),
        grid_spec=pltpu.PrefetchScalarGridSpec(
            num_scalar_prefetch=2,
            grid=(n_e,),
            in_specs=[pl.BlockSpec(memory_space=pl.ANY)] * 4,
            out_specs=pl.BlockSpec(memory_space=pl.ANY),
            scratch_shapes=[pltpu.VMEM((EXPERT_IN_RING, MOE_BLOCK, half), U32),
                            pltpu.VMEM((EXPERT_OUT_RING, MOE_BLOCK, half), U32),
                            pltpu.VMEM((d, de), BF16), pltpu.VMEM((d, de), BF16),
                            pltpu.VMEM((de, d), BF16),
                            pltpu.VMEM((n_w, d, de), F32), pltpu.VMEM((n_w, d, de), F32),
                            pltpu.VMEM((n_w, de, d), F32),
                            pltpu.SemaphoreType.DMA((EXPERT_IN_RING,)),
                            pltpu.SemaphoreType.DMA((EXPERT_OUT_RING,)),
                            pltpu.SemaphoreType.DMA((n_w,))]),
        compiler_params=_params(),
        name="moe_experts",
    )(seg_start, seg_blocks, xs, wg, wu, wd)


def _combine_kernel(yg_ref, gt_ref, x_ref, g2_ref, fg_ref, o_ref):
    k = yg_ref.shape[0]
    gt = gt_ref[...]
    lo, hi = _unpack_halves(yg_ref[0])
    y_lo, y_hi = lo * gt[:, 0:1], hi * gt[:, 0:1]
    for r in range(1, k):
        lo, hi = _unpack_halves(yg_ref[r])
        y_lo, y_hi = y_lo + lo * gt[:, r:r + 1], y_hi + hi * gt[:, r:r + 1]
    y = jnp.concatenate([y_lo, y_hi], axis=1)
    o_ref[...] = _rms(x_ref[...] + g2_ref[0] * y, fg_ref[...])


def _combine_into_kernel(yg_ref, gt_ref, x_ref, g2_ref, fg_ref, prev_ref, o_ref):
    del prev_ref
    _combine_kernel(yg_ref, gt_ref, x_ref, g2_ref, fg_ref, o_ref)


def _combine(yg, tok0, gates_t, x1s, gate2, final_g, seq, tc, out_so_far=None):
    k, n, half = yg.shape
    t, d = x1s.shape
    b0 = tok0 // tc
    args = [yg, gates_t, x1s, gate2, final_g.reshape(1, d)]
    in_specs = [pl.BlockSpec((k, tc, half), lambda i: (0, i, 0)),
                pl.BlockSpec((tc, k), lambda i: (i + b0, 0)),
                pl.BlockSpec((tc, d), lambda i: (i + b0, 0)),
                pl.BlockSpec((1, 1, d), lambda i: (((i + b0) * tc) // seq, 0, 0)),
                pl.BlockSpec((1, d), lambda i: (0, 0))]
    aliases = {}
    kernel = _combine_kernel
    if out_so_far is not None:
        args.append(out_so_far)
        in_specs.append(pl.BlockSpec(memory_space=pl.ANY))
        aliases = {len(args) - 1: 0}
        kernel = _combine_into_kernel
    return pl.pallas_call(
        kernel,
        out_shape=jax.ShapeDtypeStruct((t, d), F32),
        grid=(n // tc,),
        in_specs=in_specs,
        out_specs=pl.BlockSpec((tc, d), lambda i: (i + b0, 0)),
        input_output_aliases=aliases,
        compiler_params=_params(),
        name="moe_combine",
    )(*args)


def _layer(x2, c, bsz, seq, lb_row, ada_w, ada_b, norm1_g, w_in, hg_norm_g, w_branch_a, w_branch_b,
           w_out, norm2_g, w_router, router_bias, w_exp_gate, w_exp_up, w_exp_down, w_sh_gate,
           w_sh_up, w_sh_down, final_g):
    t, d = x2.shape
    n_e = w_router.shape[1]
    mod = _ada(c, ada_w, ada_b).reshape(bsz, 6, 1, d)
    shift1, scale1, gate1, shift2, scale2, gate2 = (mod[:, j] for j in range(6))

    hw = hg_norm_g.shape[0]
    aw = len(ATT_GROUPS) * ATT_HEADS_PER_GROUP * ATT_HEAD_DIM
    flat_segs = [(0, hw, BF16), (hw, hw, F32), (2 * hw, hw, BF16), (3 * hw, hw, BF16),
                 (4 * hw + 3 * aw, d, BF16), (4 * hw + 3 * aw + d, d, BF16)]
    (hq, hf, hi, hg, ga, gb), qkv = _inproj(
        x2, norm1_g, scale1, shift1, w_in.astype(BF16), bsz, seq, flat_segs, 4 * hw, tm=512)

    ya = _hgrn(hq, hf, hi, hg, lb_row, hg_norm_g, bsz, seq, ts=512)
    att = [_attn_group(*qkv[3 * g:3 * g + 3], g, blocks_per_step=8)
           for g in range(len(ATT_GROUPS))]

    x1s, hp, idx, gates, rank, cnt = _merge(
        ya, att, ga, gb, x2, gate1, scale2, shift2, gate2, norm2_g, w_branch_a.astype(BF16),
        w_branch_b.astype(BF16), w_out.astype(BF16), w_router.T, w_sh_gate.astype(BF16),
        w_sh_up.astype(BF16), w_sh_down.astype(BF16), router_bias, seq, tm=512)
    counts = cnt[:, 0]
    padded = (counts + MOE_BLOCK - 1) // MOE_BLOCK * MOE_BLOCK
    seg_start = (jnp.cumsum(padded) - padded).astype(I32)
    n_blocks = -(-(t * TOP_K) // MOE_BLOCK) + n_e
    dest = _dest(idx, rank, seg_start, tt=512)

    xs = _sc_scatter_rows(hp, dest, n_blocks * MOE_BLOCK)
    ys = _experts(seg_start, (padded // MOE_BLOCK).astype(I32), xs, w_exp_gate, w_exp_up,
                  w_exp_down)
    out, n = None, t // COMBINE_PARTS
    for part in range(COMBINE_PARTS):
        yg = _sc_gather_rows(ys, dest[:, part * n:(part + 1) * n]).reshape(TOP_K, n, d // 2)
        out = _combine(yg, part * n, gates.T, x1s, gate2, final_g, seq, tc=256, out_so_far=out)
    return out


def kernel(x, c, ada_w, ada_b, norm1_g, w_in, lb_logits, hg_norm_g, w_branch_a, w_branch_b, w_out,
           norm2_g, w_router, router_bias, w_exp_gate, w_exp_up, w_exp_down, w_sh_gate, w_sh_up,
           w_sh_down, final_g):
    bsz, seq, d = x.shape
    depth = ada_w.shape[0]
    assert depth == 1, "the last layer's kernels also apply the final norm"
    lb_table = jnp.cumsum(jax.nn.softmax(lb_logits.astype(F32), axis=0), axis=0)
    out = _layer(x.reshape(bsz * seq, d), c, bsz, seq, lb_table[0], ada_w[0], ada_b[0], norm1_g[0],
                 w_in[0], hg_norm_g[0], w_branch_a[0], w_branch_b[0], w_out[0], norm2_g[0],
                 w_router[0], router_bias[0], w_exp_gate[0], w_exp_up[0], w_exp_down[0],
                 w_sh_gate[0], w_sh_up[0], w_sh_down[0], final_g)
    return out.reshape(bsz, seq, d)
```

```python
import functools

import jax
import jax.numpy as jnp
from jax import lax
from jax.experimental import pallas as pl
from jax.experimental.pallas import tpu as pltpu
from jax.experimental.pallas import tpu_sc as plsc

F32 = jnp.float32
BF16 = jnp.bfloat16
I32 = jnp.int32
U32 = jnp.uint32
HIGHEST = lax.Precision.HIGHEST

HG_HEADS = 4
HG_BLOCK = 16
HG_CHUNK = 32
HG_MILD_DECAY = -80.0
ATT_GROUPS = ((128, 1), (512, 4), (2048, 16))
ATT_HEADS_PER_GROUP = 4
ATT_HEAD_DIM = 64
TOP_K = 8
ROUTE_SCALE = 2.5
MOE_BLOCK = 256
RMS_EPS = 1e-6
N_DMA_QUEUES = 2
SC_WINDOW = 128
COMBINE_PARTS = 4
EXPERT_WEIGHT_BUFFERS = 3
EXPERT_GROUP = 4
EXPERT_IN_RING = 8
EXPERT_OUT_RING = 6

LANES = 128
VMEM_LIMIT_BYTES = 56 * 1024 * 1024

IN_PROJ_TILE = 512
HGRN_TILE = 512
ATT_BLOCKS_PER_STEP = 8
MERGE_TILE = 512
DEST_TILE = 512
COMBINE_TILE = 256


def _sigmoid(x):
    return 1.0 / (1.0 + jnp.exp(-x))


def _silu(x):
    return x * _sigmoid(x)


def _rms(x, g):
    return x * lax.rsqrt(jnp.mean(x * x, axis=-1, keepdims=True) + RMS_EPS) * g


def _pack_halves(x):
    n = x.shape[1] // 2
    bits = lax.bitcast_convert_type(x.astype(BF16).astype(F32), U32)
    return (bits[:, :n] >> 16) | (bits[:, n:] & jnp.uint32(0xFFFF0000))


def _unpack_halves(word):
    lo = lax.bitcast_convert_type(word << 16, F32)
    hi = lax.bitcast_convert_type(word & jnp.uint32(0xFFFF0000), F32)
    return lo, hi


def _params(n_axes=1):
    return pltpu.CompilerParams(
        dimension_semantics=("arbitrary",) * n_axes, vmem_limit_bytes=VMEM_LIMIT_BYTES)


def _ada_kernel(c_ref, w_ref, b_ref, o_ref):
    sc = _silu(c_ref[...])
    o_ref[...] = jnp.dot(sc, w_ref[...], preferred_element_type=F32, precision=HIGHEST) + b_ref[...]


def _ada(c, w, b):
    bsz, d = c.shape
    n = w.shape[1]
    return pl.pallas_call(
        _ada_kernel,
        out_shape=jax.ShapeDtypeStruct((bsz, n), F32),
        grid=(n // d,),
        in_specs=[pl.BlockSpec((bsz, d), lambda j: (0, 0)),
                  pl.BlockSpec((d, d), lambda j: (0, j)),
                  pl.BlockSpec((1, d), lambda j: (0, j))],
        out_specs=pl.BlockSpec((bsz, d), lambda j: (0, j)),
        compiler_params=_params(),
        name="ada_mod",
    )(c, w, b.reshape(1, n))


def _inproj_kernel(n_flat, flat_ranges, att_c0, x_ref, g_ref, sc_ref, sh_ref, w_ref, *refs):
    flat_refs, att_refs, scr = refs[:n_flat], refs[n_flat:-1], refs[-1]
    tm = x_ref.shape[0]
    h = _rms(x_ref[...], g_ref[...]) * (1.0 + sc_ref[0]) + sh_ref[0]
    hb = h.astype(BF16)
    for (c0, c1), o_ref in zip(flat_ranges, flat_refs):
        o_ref[...] = jnp.dot(hb, w_ref[:, c0:c1], preferred_element_type=F32).astype(o_ref.dtype)
    gw = ATT_HEADS_PER_GROUP * ATT_HEAD_DIM
    n_groups = len(ATT_GROUPS)
    for part in range(3):
        c0 = att_c0 + part * n_groups * gw
        res = jnp.dot(hb, w_ref[:, c0:c0 + n_groups * gw], preferred_element_type=F32)
        if part == 0:
            res = res * (ATT_HEAD_DIM ** -0.5)
        for g, (_, dil) in enumerate(ATT_GROUPS):
            o_ref = att_refs[g * 3 + part]
            sub = res[:, g * gw:(g + 1) * gw]
            if dil == 1:
                o_ref[0, 0] = sub.astype(BF16)
            else:
                for c in range(gw // LANES):
                    scr[c] = sub[:, c * LANES:(c + 1) * LANES]
                for r in range(dil):
                    o_ref[0, r] = jnp.concatenate(
                        [scr[c, pl.ds(r, tm // dil, stride=dil), :] for c in range(gw // LANES)],
                        axis=1).astype(BF16)


def _inproj(x2, g, scale, shift, w_bf16, bsz, seq, flat_segs, att_c0, tm):
    t, d = x2.shape
    gw = ATT_HEADS_PER_GROUP * ATT_HEAD_DIM
    n_per = seq // tm
    per_b = lambda i: (i // n_per, 0, 0)
    att_shapes, att_specs = [], []
    for _, dil in ATT_GROUPS:
        for _ in range(3):
            att_shapes.append(jax.ShapeDtypeStruct((bsz, dil, seq // dil, gw), BF16))
            att_specs.append(pl.BlockSpec((1, dil, tm // dil, gw),
                                          lambda i: (i // n_per, 0, i % n_per, 0)))
    outs = pl.pallas_call(
        functools.partial(_inproj_kernel, len(flat_segs),
                          tuple((c0, c0 + wdt) for c0, wdt, _ in flat_segs), att_c0),
        out_shape=[jax.ShapeDtypeStruct((t, wdt), dt) for _, wdt, dt in flat_segs] + att_shapes,
        grid=(t // tm,),
        in_specs=[pl.BlockSpec((tm, d), lambda i: (i, 0)),
                  pl.BlockSpec((1, d), lambda i: (0, 0)),
                  pl.BlockSpec((1, 1, d), per_b),
                  pl.BlockSpec((1, 1, d), per_b),
                  pl.BlockSpec(w_bf16.shape, lambda i: (0, 0))],
        out_specs=[pl.BlockSpec((tm, wdt), lambda i: (i, 0)) for _, wdt, _ in flat_segs]
        + att_specs,
        scratch_shapes=[pltpu.VMEM((gw // LANES, tm, LANES), F32)],
        compiler_params=_params(),
        name="in_proj",
    )(x2, g.reshape(1, d), scale, shift, w_bf16)
    return outs[:len(flat_segs)], outs[len(flat_segs):]


def _hgrn_kernel(ts, q_ref, f_ref, v_ref, gt_ref, lb_ref, ng_ref, o_ref, st_ref, b_ref):
    dk = q_ref.shape[1] // HG_HEADS
    n_chunks = ts // HG_CHUNK
    n_blk = HG_CHUNK // HG_BLOCK

    @pl.when(pl.program_id(1) == 0)
    def _():
        st_ref[...] = jnp.zeros_like(st_ref)

    row = lax.broadcasted_iota(I32, (LANES, LANES), 0)
    col = lax.broadcasted_iota(I32, (LANES, LANES), 1)
    same_chunk = (row // HG_CHUNK) == (col // HG_CHUNK)
    cum_mat = jnp.where(same_chunk & (col <= row), 1.0, 0.0).astype(BF16)

    def chunk_cumsum(x):
        out = []
        for r0 in range(0, ts, LANES):
            rest = x[r0:r0 + LANES]
            acc = None
            for _ in range(3):
                term = rest.astype(BF16)
                part = jnp.dot(cum_mat, term, preferred_element_type=F32)
                acc = part if acc is None else acc + part
                rest = rest - term.astype(F32)
            out.append(acc)
        return jnp.concatenate(out, axis=0)

    def forget(cs):
        lb = lb_ref[:, cs]
        return lb + (1.0 - lb) * _sigmoid(f_ref[:, cs])

    b_min = None
    for h in range(HG_HEADS):
        cs = slice(h * dk, (h + 1) * dk)
        b = chunk_cumsum(jnp.log(forget(cs)))
        b_ref[:, cs] = b
        m = jnp.min(b)
        b_min = m if b_min is None else jnp.minimum(b_min, m)
    mild = b_min >= HG_MILD_DECAY

    def finish(h, o, st):
        cs = slice(h * dk, (h + 1) * dk)
        st_ref[h] = st
        y = _rms(o, ng_ref[:, cs]) * _silu(gt_ref[:, cs].astype(F32))
        o_ref[:, cs] = y.astype(o_ref.dtype)

    @pl.when(mild)
    def _():
        span = 2 * HG_CHUNK
        causal = (lax.broadcasted_iota(I32, (span, span), 0)
                  >= lax.broadcasted_iota(I32, (span, span), 1))
        nt = lambda x, y: lax.dot_general(x, y, (((1,), (1,)), ((), ())),
                                          preferred_element_type=F32)
        for h in range(HG_HEADS):
            cs = slice(h * dk, (h + 1) * dk)
            v = v_ref[:, cs]
            b = b_ref[:, cs]
            q = q_ref[:, cs].astype(F32)
            k = 1.0 - forget(cs)
            st = st_ref[h]
            o_rows = []
            for r0 in range(0, ts, span):
                sl = slice(r0, r0 + span)
                b_first, b_second = b[r0:r0 + HG_CHUNK], b[r0 + HG_CHUNK:r0 + span]
                end_first = b_first[HG_CHUNK - 1:HG_CHUNK]
                end_second = b_second[HG_CHUNK - 1:HG_CHUNK]
                e = jnp.exp(jnp.concatenate([b_first - end_first, b_second], axis=0))
                qe = (q[sl] * e).astype(BF16)
                ke = k[sl] / e
                a = jnp.where(causal, nt(qe, ke.astype(BF16)), 0.0).astype(BF16)
                st_in = (st * jnp.exp(end_first)).astype(BF16)
                o_rows.append(jnp.dot(a, v[sl], preferred_element_type=F32) + nt(qe, st_in))
                kend = (ke * jnp.exp(end_second)).astype(BF16)
                vt = v[sl].astype(F32).T.astype(BF16)
                st = (st * jnp.exp(end_first + end_second)
                      + jnp.dot(vt, kend, preferred_element_type=F32))
            finish(h, jnp.concatenate(o_rows, axis=0), st)

    @pl.when(jnp.logical_not(mild))
    def _():
        _hgrn_steep(ts, dk, n_chunks, n_blk, q_ref, v_ref, b_ref, st_ref, forget, finish)


def _hgrn_steep(ts, dk, n_chunks, n_blk, q_ref, v_ref, b_ref, st_ref, forget, finish):
    t_in_blk = lax.broadcasted_iota(I32, (ts, dk), 0) % HG_BLOCK

    for h in range(HG_HEADS):
        cs = slice(h * dk, (h + 1) * dk)
        q = q_ref[:, cs].astype(F32)
        v = v_ref[:, cs].astype(F32)
        k = 1.0 - forget(cs)
        b = b_ref[:, cs]

        o = jnp.sum(q * k, axis=-1, keepdims=True) * v
        for d in range(1, HG_BLOCK):
            k_d = pltpu.roll(k, d, axis=0)
            b_d = pltpu.roll(b, d, axis=0)
            v_d = pltpu.roll(v, d, axis=0)
            w = jnp.sum(q * k_d * jnp.exp(jnp.minimum(b - b_d, 0.0)), axis=-1, keepdims=True)
            o = o + jnp.where(t_in_blk >= d, w * v_d, 0.0)

        st = st_ref[h]
        o_rows = []
        for c in range(n_chunks):
            r0 = c * HG_CHUNK
            bc = b[r0:r0 + HG_CHUNK]
            qc = q[r0:r0 + HG_CHUNK]
            kc = k[r0:r0 + HG_CHUNK]
            vc = v[r0:r0 + HG_CHUNK].astype(BF16)
            st_b = st.astype(BF16)
            for i in range(n_blk):
                i0 = i * HG_BLOCK
                if i == 0:
                    qt = qc[:HG_BLOCK] * jnp.exp(bc[:HG_BLOCK])
                    qs = qt
                else:
                    ref_row = bc[i0 - 1:i0]
                    qt = qc[i0:i0 + HG_BLOCK] * jnp.exp(bc[i0:i0 + HG_BLOCK] - ref_row)
                    qs = qt * jnp.exp(ref_row)
                oi = lax.dot_general(qs.astype(BF16), st_b, (((1,), (1,)), ((), ())),
                                     preferred_element_type=F32)
                if i > 0:
                    kh = kc[:i0] * jnp.exp(ref_row - bc[:i0])
                    a = lax.dot_general(qt.astype(BF16), kh.astype(BF16), (((1,), (1,)), ((), ())),
                                        preferred_element_type=F32)
                    oi = oi + jnp.dot(a.astype(BF16), vc[:i0], preferred_element_type=F32)
                o_rows.append(oi)
            b_end = bc[HG_CHUNK - 1:HG_CHUNK]
            kend = kc * jnp.exp(b_end - bc)
            vt = v[r0:r0 + HG_CHUNK].T.astype(BF16)
            st = st * jnp.exp(b_end) + jnp.dot(vt, kend.astype(BF16), preferred_element_type=F32)
        finish(h, o + jnp.concatenate(o_rows, axis=0), st)


def _hgrn(hq, hf, hi, hg, lb, ng, bsz, seq, ts):
    t, w = hq.shape
    dk = w // HG_HEADS
    n_s = seq // ts
    tile = lambda b, s: (b * n_s + s, 0)
    return pl.pallas_call(
        functools.partial(_hgrn_kernel, ts),
        out_shape=jax.ShapeDtypeStruct((t, w), BF16),
        grid=(bsz, n_s),
        in_specs=[pl.BlockSpec((ts, w), tile)] * 4
        + [pl.BlockSpec((1, w), lambda b, s: (0, 0))] * 2,
        out_specs=pl.BlockSpec((ts, w), tile),
        scratch_shapes=[pltpu.VMEM((HG_HEADS, dk, dk), F32), pltpu.VMEM((ts, w), F32)],
        compiler_params=_params(2),
        name="hgrn2",
    )(hq, hf, hi, hg, lb.reshape(1, w), ng.reshape(1, w))


def _attn_kernel(nk, nq, nr, q_ref, kp_ref, kc_ref, vp_ref, vc_ref, o_ref, lse_ref):
    n = pl.program_id(2)
    e = ATT_HEAD_DIM
    i = lax.broadcasted_iota(I32, (nk, 2 * nk), 0)
    j = lax.broadcasted_iota(I32, (nk, 2 * nk), 1)
    band = (j >= i) & (j <= i + nk)
    first_head = lax.broadcasted_iota(I32, (nk, LANES), 1) < e
    zero = jnp.zeros((), q_ref.dtype)
    for r in range(nr):
        kk = jnp.concatenate([kp_ref[0, r], kc_ref[0, r]], axis=0)
        vv = jnp.concatenate([vp_ref[0, r], vc_ref[0, r]], axis=0)
        for b in range(nq):
            valid = band & ((j >= nk) | (n * nq + b > 0))
            rows = slice(b * nk, (b + 1) * nk)
            for c in range(0, ATT_HEADS_PER_GROUP * e, LANES):
                q = q_ref[0, r, rows, c:c + LANES]
                kb = kk[b * nk:(b + 2) * nk, c:c + LANES]
                vb = vv[b * nk:(b + 2) * nk, c:c + LANES]
                outs, lses = [], []
                for keep in (first_head, jnp.logical_not(first_head)):
                    s = lax.dot_general(jnp.where(keep, q, zero), kb, (((1,), (1,)), ((), ())),
                                        preferred_element_type=F32)
                    s = jnp.where(valid, s, -jnp.inf)
                    m = jnp.max(s, axis=-1, keepdims=True)
                    p = jnp.exp(s - m)
                    l = jnp.sum(p, axis=-1, keepdims=True)
                    outs.append(jnp.dot(p.astype(BF16), vb, preferred_element_type=F32) / l)
                    lses.append(m + jnp.log(l))
                o_ref[0, r, rows, c:c + LANES] = jnp.where(first_head, outs[0], outs[1])
                lse_ref[0, r, rows, c:c + LANES] = jnp.where(first_head, lses[0], lses[1])


def _attn_group(q, k, v, g, blocks_per_step):
    window, dil = ATT_GROUPS[g]
    nk = window // dil
    bsz, _, ln, gw = q.shape
    nq = min(blocks_per_step, ln // nk)
    nr = min(blocks_per_step // nq, dil)
    assert ln % (nk * nq) == 0 and dil % nr == 0 and 2 * ATT_HEAD_DIM == LANES
    cur = pl.BlockSpec((1, nr, nq * nk, gw), lambda b, r, n: (b, r, n, 0))
    prev = pl.BlockSpec((1, nr, nk, gw), lambda b, r, n: (b, r, jnp.maximum(n * nq - 1, 0), 0))
    return pl.pallas_call(
        functools.partial(_attn_kernel, nk, nq, nr),
        out_shape=[jax.ShapeDtypeStruct(q.shape, F32)] * 2,
        grid=(bsz, dil // nr, ln // (nk * nq)),
        in_specs=[cur, prev, cur, prev, cur],
        out_specs=[cur, cur],
        compiler_params=_params(3),
        name=f"dilated_attn_g{g}",
    )(q, k, k, v, v)


def _token_major(ref, scr):
    dil, rows = ref.shape[1], ref.shape[2]
    if dil == 1:
        return ref[0, 0]
    n_col = scr.shape[0]
    for r in range(dil):
        for c in range(n_col):
            scr[c, pl.ds(r, rows, stride=dil), :] = ref[0, r, :, c * LANES:(c + 1) * LANES]
    return jnp.concatenate([scr[c] for c in range(n_col)], axis=1)


def _merge_kernel(ya_ref, o0_ref, o1_ref, o2_ref, l0_ref, l1_ref, l2_ref, ga_ref, gb_ref, x_ref,
                  g1_ref, sc2_ref, sh2_ref, g2_ref, n2_ref, wa_ref, wb_ref, wo_ref, wr_ref, wrl_ref,
                  wsg_ref, wsu_ref, wsd_ref, bias_ref, x1_ref, hp_ref, idx_ref, gate_ref, rank_ref,
                  cnt_ref, carry_ref, lg_ref, *scr):
    step = pl.program_id(0)

    @pl.when(step == 0)
    def _():
        carry_ref[...] = jnp.zeros_like(carry_ref)
        lg_ref[...] = jnp.zeros_like(lg_ref)

    _route(lg_ref[...], jnp.where(step > 0, 1.0, 0.0), bias_ref, idx_ref, gate_ref, rank_ref,
           cnt_ref, carry_ref)

    l0, l1, l2 = (_token_major(r, s) for r, s in zip((l0_ref, l1_ref, l2_ref), scr[:3]))
    o0, o1, o2 = (_token_major(r, s) for r, s in zip((o0_ref, o1_ref, o2_ref), scr[3:]))
    m = jnp.maximum(jnp.maximum(l0, l1), l2)
    e0, e1, e2 = jnp.exp(l0 - m), jnp.exp(l1 - m), jnp.exp(l2 - m)
    yb = (e0 * o0 + e1 * o1 + e2 * o2) / (e0 + e1 + e2)
    merged = (_sigmoid(ga_ref[...].astype(F32))
              * jnp.dot(ya_ref[...], wa_ref[...], preferred_element_type=F32)
              + _sigmoid(gb_ref[...].astype(F32))
              * jnp.dot(yb.astype(BF16), wb_ref[...], preferred_element_type=F32))
    x1 = x_ref[...] + g1_ref[0] * jnp.dot(merged.astype(BF16), wo_ref[...],
                                           preferred_element_type=F32)
    h2 = _rms(x1, n2_ref[...]) * (1.0 + sc2_ref[0]) + sh2_ref[0]
    hb = h2.astype(BF16)
    act = (_silu(jnp.dot(hb, wsg_ref[...], preferred_element_type=F32))
           * jnp.dot(hb, wsu_ref[...], preferred_element_type=F32))
    shared = jnp.dot(act.astype(BF16), wsd_ref[...], preferred_element_type=F32)
    x1_ref[...] = x1 + g2_ref[0] * shared
    hp_ref[...] = _pack_halves(h2)
    h_lo = (h2 - hb.astype(F32)).astype(BF16)
    nt = lambda a, b: lax.dot_general(a, b, (((1,), (1,)), ((), ())), preferred_element_type=F32)
    lg_ref[...] = nt(wr_ref[...], hb) + (nt(wr_ref[...], h_lo) + nt(wrl_ref[...], hb))


def _merge(ya, att, ga, gb, x2, gate1, scale2, shift2, gate2, norm2_g, wa, wb, wo, wr_t, wsg, wsu,
           wsd, router_bias, seq, tm):
    t, d = x2.shape
    n_e = wr_t.shape[0]
    wr_hi = wr_t.astype(BF16)
    wr_lo = (wr_t - wr_hi.astype(F32)).astype(BF16)
    n_per = seq // tm
    n_tiles = t // tm
    tile = lambda i: jnp.minimum(i, n_tiles - 1)
    per_b = lambda i: (tile(i) // n_per, 0, 0)
    rows = lambda wdt: pl.BlockSpec((tm, wdt), lambda i: (tile(i), 0))
    full = lambda a: pl.BlockSpec(a.shape, lambda i: (0,) * a.ndim)
    vec = pl.BlockSpec((1, 1, d), per_b)
    (o0, l0), (o1, l1), (o2, l2) = att
    gw = o0.shape[3]
    by_residue = lambda a: pl.BlockSpec((1, a.shape[1], tm // a.shape[1], gw),
                                        lambda i: (tile(i) // n_per, 0, tile(i) % n_per, 0))
    att_in = (o0, o1, o2, l0, l1, l2)
    bias_col = router_bias.reshape(n_e, 1)
    tok = pl.BlockSpec((TOP_K, tm), lambda i: (0, jnp.maximum(i - 1, 0)))
    return pl.pallas_call(
        _merge_kernel,
        out_shape=[jax.ShapeDtypeStruct((t, d), F32),
                   jax.ShapeDtypeStruct((t, d // 2), U32),
                   jax.ShapeDtypeStruct((TOP_K, t), I32), jax.ShapeDtypeStruct((TOP_K, t), F32),
                   jax.ShapeDtypeStruct((TOP_K, t), I32), jax.ShapeDtypeStruct((n_e, LANES), I32)],
        grid=(n_tiles + 1,),
        in_specs=[rows(ya.shape[1])] + [by_residue(a) for a in att_in] + [rows(d)] * 3
        + [vec, vec, vec, vec, pl.BlockSpec((1, d), lambda i: (0, 0))]
        + [full(a) for a in (wa, wb, wo, wr_hi, wr_lo, wsg, wsu, wsd, bias_col)],
        out_specs=[rows(d), rows(d // 2), tok, tok, tok,
                   pl.BlockSpec((n_e, LANES), lambda i: (0, 0))],
        scratch_shapes=[pltpu.VMEM((n_e, 1), F32), pltpu.VMEM((n_e, tm), F32)]
        + [pltpu.VMEM((gw // LANES, tm, LANES), F32)] * 6,
        compiler_params=_params(),
        name="merge_router",
    )(ya, *att_in, ga, gb, x2, gate1, scale2, shift2, gate2,
      norm2_g.reshape(1, d), wa, wb, wo, wr_hi, wr_lo, wsg, wsu, wsd, bias_col)


def _route(logits, live, bias_ref, idx_ref, gate_ref, rank_ref, cnt_ref, carry_ref):
    n_e, tt = logits.shape
    scores = _sigmoid(logits)
    sel = scores + bias_ref[...]
    eio = lax.broadcasted_iota(I32, (n_e, tt), 0)
    picked = jnp.zeros((n_e, tt), F32)
    idxs, vals = [], []
    for _ in range(TOP_K):
        m = jnp.max(sel, axis=0, keepdims=True)
        ik = jnp.min(jnp.where(sel == m, eio, n_e), axis=0, keepdims=True)
        hit = eio == ik
        vals.append(jnp.sum(jnp.where(hit, scores, 0.0), axis=0, keepdims=True))
        sel = jnp.where(hit, -jnp.inf, sel)
        picked = picked + jnp.where(hit, 1.0, 0.0)
        idxs.append(ik)
    denom = vals[0]
    for v in vals[1:]:
        denom = denom + v
    gate_ref[...] = jnp.concatenate([v / denom * ROUTE_SCALE for v in vals], axis=0)
    idx_ref[...] = jnp.concatenate(idxs, axis=0)

    upper = (lax.broadcasted_iota(I32, (tt, tt), 0) <= lax.broadcasted_iota(I32, (tt, tt), 1))
    incl = jnp.dot(picked.astype(BF16), jnp.where(upper, 1.0, 0.0).astype(BF16),
                   preferred_element_type=F32)
    before = incl - picked + carry_ref[...]
    rank_ref[...] = jnp.concatenate(
        [jnp.sum(jnp.where(eio == ik, before, 0.0), axis=0, keepdims=True) for ik in idxs],
        axis=0).astype(I32)
    carry_ref[...] = carry_ref[...] + jnp.sum(picked, axis=1, keepdims=True) * live
    cnt_ref[...] = jnp.broadcast_to(carry_ref[...], cnt_ref.shape).astype(I32)


def _dest_kernel(idx_ref, rank_ref, start_ref, o_ref):
    k, tt = idx_ref.shape
    n_e = start_ref.shape[0]
    eio = lax.broadcasted_iota(I32, (n_e, tt), 0)
    start = start_ref[...]
    rows = [jnp.sum(jnp.where(eio == idx_ref[r:r + 1, :], start, 0), axis=0, keepdims=True)
            for r in range(k)]
    o_ref[...] = jnp.concatenate(rows, axis=0) + rank_ref[...]


def _dest(idx, rank, seg_start, tt):
    k, t = idx.shape
    n_e = seg_start.shape[0]
    tok = pl.BlockSpec((k, tt), lambda i: (0, i))
    return pl.pallas_call(
        _dest_kernel,
        out_shape=jax.ShapeDtypeStruct((k, t), I32),
        grid=(t // tt,),
        in_specs=[tok, tok, pl.BlockSpec((n_e, 1), lambda i: (0, 0))],
        out_specs=tok,
        compiler_params=_params(),
        name="moe_dest",
    )(idx, rank, seg_start.reshape(n_e, 1))


def _sc_mesh():
    return plsc.VectorSubcoreMesh(core_axis_name="core", subcore_axis_name="subcore")


def _sc_scatter_rows(rows, dest, n_out):
    k, t = dest.shape
    w = rows.shape[1]
    mesh = _sc_mesh()
    n_workers = mesh.num_cores * mesh.num_subcores
    win_per_worker = t // (SC_WINDOW * n_workers)
    assert win_per_worker * SC_WINDOW * n_workers == t

    @functools.partial(
        pl.kernel, out_type=jax.ShapeDtypeStruct((n_out, w), rows.dtype), mesh=mesh,
        scratch_types=[pltpu.VMEM((SC_WINDOW, w), rows.dtype)]
        + [pltpu.VMEM((1, SC_WINDOW), I32)] * k + [pltpu.SemaphoreType.DMA],
        name="moe_dispatch_sc")
    def run(rows_hbm, idx_hbm, out_hbm, rows_v, *rest):
        idx_v, sem = rest[:k], rest[k]
        worker = lax.axis_index("subcore") * mesh.num_cores + lax.axis_index("core")

        @pl.loop(0, win_per_worker)
        def _(j):
            t0 = pl.multiple_of((worker * win_per_worker + j) * SC_WINDOW, SC_WINDOW)
            pltpu.sync_copy(rows_hbm.at[pl.ds(t0, SC_WINDOW)], rows_v)
            for r in range(k):
                pltpu.sync_copy(idx_hbm.at[:, pl.ds(r * t + t0, SC_WINDOW)], idx_v[r])
            copies = [pltpu.async_copy(rows_v, out_hbm.at[idx_v[r].at[0]], sem) for r in range(k)]
            for c in copies:
                c.wait()

    return run(rows, dest.reshape(1, k * t))


def _sc_gather_rows(table, dest):
    k, t = dest.shape
    w = table.shape[1]
    mesh = _sc_mesh()
    n_workers = mesh.num_cores * mesh.num_subcores
    win_per_worker = (k * t) // (SC_WINDOW * n_workers)
    assert win_per_worker * SC_WINDOW * n_workers == k * t

    @functools.partial(
        pl.kernel, out_type=jax.ShapeDtypeStruct((k * t, w), table.dtype), mesh=mesh,
        scratch_types=[pltpu.VMEM((SC_WINDOW, w), table.dtype), pltpu.VMEM((1, SC_WINDOW), I32)],
        name="moe_gather_sc")
    def run(table_hbm, idx_hbm, out_hbm, rows_v, idx_v):
        worker = lax.axis_index("subcore") * mesh.num_cores + lax.axis_index("core")

        @pl.loop(0, win_per_worker)
        def _(j):
            p0 = pl.multiple_of((worker * win_per_worker + j) * SC_WINDOW, SC_WINDOW)
            pltpu.sync_copy(idx_hbm.at[:, pl.ds(p0, SC_WINDOW)], idx_v)
            pltpu.sync_copy(table_hbm.at[idx_v.at[0]], rows_v)
            pltpu.sync_copy(rows_v, out_hbm.at[pl.ds(p0, SC_WINDOW)])

    return run(table, dest.reshape(1, k * t))


def _expert_kernel(start_ref, nblk_ref, xs_ref, wg_ref, wu_ref, wd_ref, ys_ref,
                   xbuf, ybuf, wgb, wub, wdb, wbuf_g, wbuf_u, wbuf_d, sem_in, sem_out, sem_w):
    wbuf = (wbuf_g, wbuf_u, wbuf_d)
    e = pl.program_id(0)
    n_e = pl.num_programs(0)
    nb = nblk_ref[e]
    g0 = start_ref[e] // MOE_BLOCK
    n_used = start_ref[n_e - 1] // MOE_BLOCK + nblk_ref[n_e - 1]
    n_in, n_out = xbuf.shape[0], ybuf.shape[0]

    def rows(g):
        return pl.ds(pl.multiple_of(g * MOE_BLOCK, MOE_BLOCK), MOE_BLOCK)

    def in_copy(g):
        slot = lax.rem(g, n_in)
        return pltpu.make_async_copy(xs_ref.at[rows(g), :], xbuf.at[slot], sem_in.at[slot])

    def out_copy(g):
        slot = lax.rem(g, n_out)
        return pltpu.make_async_copy(ybuf.at[slot], ys_ref.at[rows(g), :], sem_out.at[slot])

    look = n_in - EXPERT_GROUP

    @pl.when(e == 0)
    def _():
        for g in range(look):
            @pl.when(g < n_used)
            def _():
                in_copy(g).start(priority=g % N_DMA_QUEUES)

    n_w = wbuf[0].shape[0]

    def weight_copies(ex):
        slot = lax.rem(ex, n_w)
        return [pltpu.make_async_copy(src.at[ex], buf.at[slot], sem_w.at[slot])
                for src, buf in zip((wg_ref, wu_ref, wd_ref), wbuf)]

    @pl.when(e == 0)
    def _():
        for ex in range(min(n_w, wg_ref.shape[0])):
            for c in weight_copies(ex):
                c.start()

    for c in weight_copies(e):
        c.wait()
    w_slot = lax.rem(e, n_w)

    @pl.when(nb > 0)
    def _():
        wgb[...] = wbuf[0][w_slot].astype(BF16)
        wub[...] = wbuf[1][w_slot].astype(BF16)
        wdb[...] = wbuf[2][w_slot].astype(BF16)

    @pl.when(e + n_w < n_e)
    def _():
        for c in weight_copies(e + n_w):
            c.start()

    @pl.when(nb > 0)
    def _():
        def swiglu(word):
            lo, hi = _unpack_halves(word)
            x = jnp.concatenate([lo.astype(BF16), hi.astype(BF16)], axis=1)
            gate = jnp.dot(x, wgb[...], preferred_element_type=F32)
            up = jnp.dot(x, wub[...], preferred_element_type=F32)
            act = (_silu(gate) * up).astype(BF16)
            return jnp.dot(act, wdb[...], preferred_element_type=F32)

        def process(g, m):
            for i in range(m):
                in_copy(g + i).wait()
            for i in range(m):
                @pl.when(g + look + i < n_used)
                def _():
                    in_copy(g + look + i).start(priority=i % N_DMA_QUEUES)
            ys = [swiglu(xbuf[lax.rem(g + i, n_in)]) for i in range(m)]
            for i in range(m):
                @pl.when(g + i >= n_out)
                def _():
                    out_copy(g + i - n_out).wait()

                ybuf[lax.rem(g + i, n_out)] = _pack_halves(ys[i])
                out_copy(g + i).start(priority=(i + 1) % N_DMA_QUEUES)

        def group_body(p, carry):
            process(g0 + p * EXPERT_GROUP, EXPERT_GROUP)
            return carry

        lax.fori_loop(0, nb // EXPERT_GROUP, group_body, 0)
        for m in range(1, EXPERT_GROUP):
            @pl.when(lax.rem(nb, EXPERT_GROUP) == m)
            def _():
                process(g0 + nb - m, m)

    @pl.when(e == n_e - 1)
    def _():
        for i in range(n_out):
            @pl.when(n_used - 1 - i >= 0)
            def _():
                out_copy(n_used - 1 - i).wait()


def _experts(seg_start, seg_blocks, xs, wg, wu, wd):
    n_slots, half = xs.shape
    n_e, d, de = wg.shape
    n_w = EXPERT_WEIGHT_BUFFERS
    return pl.pallas_call(
        _expert_kernel,
        out_shape=jax.ShapeDtypeStruct((n_slots, half), U32),
        grid_spec=pltpu.PrefetchScalarGridSpec(
            num_scalar_prefetch=2,
            grid=(n_e,),
            in_specs=[pl.BlockSpec(memory_space=pl.ANY)] * 4,
            out_specs=pl.BlockSpec(memory_space=pl.ANY),
            scratch_shapes=[pltpu.VMEM((EXPERT_IN_RING, MOE_BLOCK, half), U32),
                            pltpu.VMEM((EXPERT_OUT_RING, MOE_BLOCK, half), U32),
                            pltpu.VMEM((d, de), BF16), pltpu.VMEM((d, de), BF16),
                            pltpu.VMEM((de, d), BF16),
                            pltpu.VMEM((n_w, d, de), F32), pltpu.VMEM((n_w, d, de), F32),
                            pltpu.VMEM((n_w, de, d), F32),
                            pltpu.SemaphoreType.DMA((EXPERT_IN_RING,)),
                            pltpu.SemaphoreType.DMA((EXPERT_OUT_RING,)),
                            pltpu.SemaphoreType.DMA((n_w,))]),
        compiler_params=_params(),
        name="moe_experts",
    )(seg_start, seg_blocks, xs, wg, wu, wd)


def _combine_kernel(yg_ref, gt_ref, x_ref, g2_ref, fg_ref, o_ref):
    k = yg_ref.shape[0]
    gt = gt_ref[...]
    lo, hi = _unpack_halves(yg_ref[0])
    y_lo, y_hi = lo * gt[:, 0:1], hi * gt[:, 0:1]
    for r in range(1, k):
        lo, hi = _unpack_halves(yg_ref[r])
        y_lo, y_hi = y_lo + lo * gt[:, r:r + 1], y_hi + hi * gt[:, r:r + 1]
    y = jnp.concatenate([y_lo, y_hi], axis=1)
    o_ref[...] = _rms(x_ref[...] + g2_ref[0] * y, fg_ref[...])


def _combine_into_kernel(yg_ref, gt_ref, x_ref, g2_ref, fg_ref, prev_ref, o_ref):
    del prev_ref
    _combine_kernel(yg_ref, gt_ref, x_ref, g2_ref, fg_ref, o_ref)


def _combine(yg, tok0, gates_t, x1s, gate2, final_g, seq, tc, out_so_far=None):
    k, n, half = yg.shape
    t, d = x1s.shape
    b0 = tok0 // tc
    args = [yg, gates_t, x1s, gate2, final_g.reshape(1, d)]
    in_specs = [pl.BlockSpec((k, tc, half), lambda i: (0, i, 0)),
                pl.BlockSpec((tc, k), lambda i: (i + b0, 0)),
                pl.BlockSpec((tc, d), lambda i: (i + b0, 0)),
                pl.BlockSpec((1, 1, d), lambda i: (((i + b0) * tc) // seq, 0, 0)),
                pl.BlockSpec((1, d), lambda i: (0, 0))]
    aliases = {}
    kernel = _combine_kernel
    if out_so_far is not None:
        args.append(out_so_far)
        in_specs.append(pl.BlockSpec(memory_space=pl.ANY))
        aliases = {len(args) - 1: 0}
        kernel = _combine_into_kernel
    return pl.pallas_call(
        kernel,
        out_shape=jax.ShapeDtypeStruct((t, d), F32),
        grid=(n // tc,),
        in_specs=in_specs,
        out_specs=pl.BlockSpec((tc, d), lambda i: (i + b0, 0)),
        input_output_aliases=aliases,
        compiler_params=_params(),
        name="moe_combine",
    )(*args)


def _layer(x2, c, bsz, seq, lb_row, ada_w, ada_b, norm1_g, w_in, hg_norm_g, w_branch_a, w_branch_b,
           w_out, norm2_g, w_router, router_bias, w_exp_gate, w_exp_up, w_exp_down, w_sh_gate,
           w_sh_up, w_sh_down, final_g):
    t, d = x2.shape
    n_e = w_router.shape[1]
    mod = _ada(c, ada_w, ada_b).reshape(bsz, 6, 1, d)
    shift1, scale1, gate1, shift2, scale2, gate2 = (mod[:, j] for j in range(6))

    hw = hg_norm_g.shape[0]
    aw = len(ATT_GROUPS) * ATT_HEADS_PER_GROUP * ATT_HEAD_DIM
    flat_segs = [(0, hw, BF16), (hw, hw, F32), (2 * hw, hw, BF16), (3 * hw, hw, BF16),
                 (4 * hw + 3 * aw, d, BF16), (4 * hw + 3 * aw + d, d, BF16)]
    (hq, hf, hi, hg, ga, gb), qkv = _inproj(
        x2, norm1_g, scale1, shift1, w_in.astype(BF16), bsz, seq, flat_segs, 4 * hw,
        tm=IN_PROJ_TILE)

    ya = _hgrn(hq, hf, hi, hg, lb_row, hg_norm_g, bsz, seq, ts=HGRN_TILE)
    att = [_attn_group(*qkv[3 * g:3 * g + 3], g, blocks_per_step=ATT_BLOCKS_PER_STEP)
           for g in range(len(ATT_GROUPS))]

    x1s, hp, idx, gates, rank, cnt = _merge(
        ya, att, ga, gb, x2, gate1, scale2, shift2, gate2, norm2_g, w_branch_a.astype(BF16),
        w_branch_b.astype(BF16), w_out.astype(BF16), w_router.T, w_sh_gate.astype(BF16),
        w_sh_up.astype(BF16), w_sh_down.astype(BF16), router_bias, seq, tm=MERGE_TILE)
    counts = cnt[:, 0]
    padded = (counts + MOE_BLOCK - 1) // MOE_BLOCK * MOE_BLOCK
    seg_start = (jnp.cumsum(padded) - padded).astype(I32)
    n_blocks = -(-(t * TOP_K) // MOE_BLOCK) + n_e
    dest = _dest(idx, rank, seg_start, tt=DEST_TILE)

    xs = _sc_scatter_rows(hp, dest, n_blocks * MOE_BLOCK)
    ys = _experts(seg_start, (padded // MOE_BLOCK).astype(I32), xs, w_exp_gate, w_exp_up,
                  w_exp_down)
    out, n = None, t // COMBINE_PARTS
    for part in range(COMBINE_PARTS):
        yg = _sc_gather_rows(ys, dest[:, part * n:(part + 1) * n]).reshape(TOP_K, n, d // 2)
        out = _combine(yg, part * n, gates.T, x1s, gate2, final_g, seq, tc=COMBINE_TILE,
                       out_so_far=out)
    return out


def kernel(x, c, ada_w, ada_b, norm1_g, w_in, lb_logits, hg_norm_g, w_branch_a, w_branch_b, w_out,
           norm2_g, w_router, router_bias, w_exp_gate, w_exp_up, w_exp_down, w_sh_gate, w_sh_up,
           w_sh_down, final_g):
    bsz, seq, d = x.shape
    depth = ada_w.shape[0]
    assert depth == 1, "the last layer's kernels also apply the final norm"
    lb_table = jnp.cumsum(jax.nn.softmax(lb_logits.astype(F32), axis=0), axis=0)
    out = _layer(x.reshape(bsz * seq, d), c, bsz, seq, lb_table[0], ada_w[0], ada_b[0], norm1_g[0],
                 w_in[0], hg_norm_g[0], w_branch_a[0], w_branch_b[0], w_out[0], norm2_g[0],
                 w_router[0], router_bias[0], w_exp_gate[0], w_exp_up[0], w_exp_down[0],
                 w_sh_gate[0], w_sh_up[0], w_sh_down[0], final_g)
    return out.reshape(bsz, seq, d)
```

```python
import functools

import jax
import jax.numpy as jnp
from jax import lax
from jax.experimental import pallas as pl
from jax.experimental.pallas import tpu as pltpu
from jax.experimental.pallas import tpu_sc as plsc

F32 = jnp.float32
BF16 = jnp.bfloat16
I32 = jnp.int32
U32 = jnp.uint32
HIGHEST = lax.Precision.HIGHEST

HG_HEADS = 4
HG_BLOCK = 16
HG_CHUNK = 32
HG_MILD_DECAY = -80.0
ATT_GROUPS = ((128, 1), (512, 4), (2048, 16))
ATT_HEADS_PER_GROUP = 4
ATT_HEAD_DIM = 64
TOP_K = 8
ROUTE_SCALE = 2.5
MOE_BLOCK = 256
RMS_EPS = 1e-6
N_DMA_QUEUES = 2
SC_WINDOW = 128
COMBINE_PARTS = 8
EXPERT_WEIGHT_BUFFERS = 3
EXPERT_GROUP = 2
EXPERT_IN_RING = 6
EXPERT_OUT_RING = 4

LANES = 128
VMEM_LIMIT_BYTES = 56 * 1024 * 1024

IN_PROJ_TILE = 512
HGRN_TILE = 512
ATT_BLOCKS_PER_STEP = 8
MERGE_TILE = 512
DEST_TILE = 512
COMBINE_TILE = 256


def _sigmoid(x):
    return 1.0 / (1.0 + jnp.exp(-x))


def _silu(x):
    return x * _sigmoid(x)


def _rms(x, g):
    return x * lax.rsqrt(jnp.mean(x * x, axis=-1, keepdims=True) + RMS_EPS) * g


def _pack_halves(x):
    n = x.shape[1] // 2
    bits = lax.bitcast_convert_type(x.astype(BF16).astype(F32), U32)
    return (bits[:, :n] >> 16) | (bits[:, n:] & jnp.uint32(0xFFFF0000))


def _unpack_halves(word):
    lo = lax.bitcast_convert_type(word << 16, F32)
    hi = lax.bitcast_convert_type(word & jnp.uint32(0xFFFF0000), F32)
    return lo, hi


def _params(n_axes=1):
    return pltpu.CompilerParams(
        dimension_semantics=("arbitrary",) * n_axes, vmem_limit_bytes=VMEM_LIMIT_BYTES)


def _ada_kernel(c_ref, w_ref, b_ref, o_ref):
    sc = _silu(c_ref[...])
    o_ref[...] = jnp.dot(sc, w_ref[...], preferred_element_type=F32, precision=HIGHEST) + b_ref[...]


def _ada(c, w, b):
    bsz, d = c.shape
    n = w.shape[1]
    return pl.pallas_call(
        _ada_kernel,
        out_shape=jax.ShapeDtypeStruct((bsz, n), F32),
        grid=(n // d,),
        in_specs=[pl.BlockSpec((bsz, d), lambda j: (0, 0)),
                  pl.BlockSpec((d, d), lambda j: (0, j)),
                  pl.BlockSpec((1, d), lambda j: (0, j))],
        out_specs=pl.BlockSpec((bsz, d), lambda j: (0, j)),
        compiler_params=_params(),
        name="ada_mod",
    )(c, w, b.reshape(1, n))


def _inproj_kernel(n_flat, flat_ranges, att_c0, x_ref, g_ref, sc_ref, sh_ref, w_ref, *refs):
    flat_refs, att_refs, scr = refs[:n_flat], refs[n_flat:-1], refs[-1]
    tm = x_ref.shape[0]
    h = _rms(x_ref[...], g_ref[...]) * (1.0 + sc_ref[0]) + sh_ref[0]
    hb = h.astype(BF16)
    for (c0, c1), o_ref in zip(flat_ranges, flat_refs):
        o_ref[...] = jnp.dot(hb, w_ref[:, c0:c1], preferred_element_type=F32).astype(o_ref.dtype)
    gw = ATT_HEADS_PER_GROUP * ATT_HEAD_DIM
    n_groups = len(ATT_GROUPS)
    for part in range(3):
        c0 = att_c0 + part * n_groups * gw
        res = jnp.dot(hb, w_ref[:, c0:c0 + n_groups * gw], preferred_element_type=F32)
        if part == 0:
            res = res * (ATT_HEAD_DIM ** -0.5)
        for g, (_, dil) in enumerate(ATT_GROUPS):
            o_ref = att_refs[g * 3 + part]
            sub = res[:, g * gw:(g + 1) * gw]
            if dil == 1:
                o_ref[0, 0] = sub.astype(BF16)
            else:
                for c in range(gw // LANES):
                    scr[c] = sub[:, c * LANES:(c + 1) * LANES]
                for r in range(dil):
                    o_ref[0, r] = jnp.concatenate(
                        [scr[c, pl.ds(r, tm // dil, stride=dil), :] for c in range(gw // LANES)],
                        axis=1).astype(BF16)


def _inproj(x2, g, scale, shift, w_bf16, bsz, seq, flat_segs, att_c0, tm):
    t, d = x2.shape
    gw = ATT_HEADS_PER_GROUP * ATT_HEAD_DIM
    n_per = seq // tm
    per_b = lambda i: (i // n_per, 0, 0)
    att_shapes, att_specs = [], []
    for _, dil in ATT_GROUPS:
        for _ in range(3):
            att_shapes.append(jax.ShapeDtypeStruct((bsz, dil, seq // dil, gw), BF16))
            att_specs.append(pl.BlockSpec((1, dil, tm // dil, gw),
                                          lambda i: (i // n_per, 0, i % n_per, 0)))
    outs = pl.pallas_call(
        functools.partial(_inproj_kernel, len(flat_segs),
                          tuple((c0, c0 + wdt) for c0, wdt, _ in flat_segs), att_c0),
        out_shape=[jax.ShapeDtypeStruct((t, wdt), dt) for _, wdt, dt in flat_segs] + att_shapes,
        grid=(t // tm,),
        in_specs=[pl.BlockSpec((tm, d), lambda i: (i, 0)),
                  pl.BlockSpec((1, d), lambda i: (0, 0)),
                  pl.BlockSpec((1, 1, d), per_b),
                  pl.BlockSpec((1, 1, d), per_b),
                  pl.BlockSpec(w_bf16.shape, lambda i: (0, 0))],
        out_specs=[pl.BlockSpec((tm, wdt), lambda i: (i, 0)) for _, wdt, _ in flat_segs]
        + att_specs,
        scratch_shapes=[pltpu.VMEM((gw // LANES, tm, LANES), F32)],
        compiler_params=_params(),
        name="in_proj",
    )(x2, g.reshape(1, d), scale, shift, w_bf16)
    return outs[:len(flat_segs)], outs[len(flat_segs):]


def _hgrn_kernel(ts, q_ref, f_ref, v_ref, gt_ref, lb_ref, ng_ref, o_ref, st_ref, b_ref):
    dk = q_ref.shape[1] // HG_HEADS
    n_chunks = ts // HG_CHUNK
    n_blk = HG_CHUNK // HG_BLOCK

    @pl.when(pl.program_id(1) == 0)
    def _():
        st_ref[...] = jnp.zeros_like(st_ref)

    row = lax.broadcasted_iota(I32, (LANES, LANES), 0)
    col = lax.broadcasted_iota(I32, (LANES, LANES), 1)
    same_chunk = (row // HG_CHUNK) == (col // HG_CHUNK)
    cum_mat = jnp.where(same_chunk & (col <= row), 1.0, 0.0).astype(BF16)

    def chunk_cumsum(x):
        out = []
        for r0 in range(0, ts, LANES):
            rest = x[r0:r0 + LANES]
            acc = None
            for _ in range(3):
                term = rest.astype(BF16)
                part = jnp.dot(cum_mat, term, preferred_element_type=F32)
                acc = part if acc is None else acc + part
                rest = rest - term.astype(F32)
            out.append(acc)
        return jnp.concatenate(out, axis=0)

    def forget(cs):
        lb = lb_ref[:, cs]
        return lb + (1.0 - lb) * _sigmoid(f_ref[:, cs])

    b_min = None
    for h in range(HG_HEADS):
        cs = slice(h * dk, (h + 1) * dk)
        b = chunk_cumsum(jnp.log(forget(cs)))
        b_ref[:, cs] = b
        m = jnp.min(b)
        b_min = m if b_min is None else jnp.minimum(b_min, m)
    mild = b_min >= HG_MILD_DECAY

    def finish(h, o, st):
        cs = slice(h * dk, (h + 1) * dk)
        st_ref[h] = st
        y = _rms(o, ng_ref[:, cs]) * _silu(gt_ref[:, cs].astype(F32))
        o_ref[:, cs] = y.astype(o_ref.dtype)

    @pl.when(mild)
    def _():
        span = 2 * HG_CHUNK
        causal = (lax.broadcasted_iota(I32, (span, span), 0)
                  >= lax.broadcasted_iota(I32, (span, span), 1))
        nt = lambda x, y: lax.dot_general(x, y, (((1,), (1,)), ((), ())),
                                          preferred_element_type=F32)
        for h in range(HG_HEADS):
            cs = slice(h * dk, (h + 1) * dk)
            v = v_ref[:, cs]
            b = b_ref[:, cs]
            q = q_ref[:, cs].astype(F32)
            k = 1.0 - forget(cs)
            st = st_ref[h]
            o_rows = []
            for r0 in range(0, ts, span):
                sl = slice(r0, r0 + span)
                b_first, b_second = b[r0:r0 + HG_CHUNK], b[r0 + HG_CHUNK:r0 + span]
                end_first = b_first[HG_CHUNK - 1:HG_CHUNK]
                end_second = b_second[HG_CHUNK - 1:HG_CHUNK]
                e = jnp.exp(jnp.concatenate([b_first - end_first, b_second], axis=0))
                qe = (q[sl] * e).astype(BF16)
                ke = k[sl] / e
                a = jnp.where(causal, nt(qe, ke.astype(BF16)), 0.0).astype(BF16)
                st_in = (st * jnp.exp(end_first)).astype(BF16)
                o_rows.append(jnp.dot(a, v[sl], preferred_element_type=F32) + nt(qe, st_in))
                kend = (ke * jnp.exp(end_second)).astype(BF16)
                vt = v[sl].astype(F32).T.astype(BF16)
                st = (st * jnp.exp(end_first + end_second)
                      + jnp.dot(vt, kend, preferred_element_type=F32))
            finish(h, jnp.concatenate(o_rows, axis=0), st)

    @pl.when(jnp.logical_not(mild))
    def _():
        _hgrn_steep(ts, dk, n_chunks, n_blk, q_ref, v_ref, b_ref, st_ref, forget, finish)


def _hgrn_steep(ts, dk, n_chunks, n_blk, q_ref, v_ref, b_ref, st_ref, forget, finish):
    t_in_blk = lax.broadcasted_iota(I32, (ts, dk), 0) % HG_BLOCK

    for h in range(HG_HEADS):
        cs = slice(h * dk, (h + 1) * dk)
        q = q_ref[:, cs].astype(F32)
        v = v_ref[:, cs].astype(F32)
        k = 1.0 - forget(cs)
        b = b_ref[:, cs]

        o = jnp.sum(q * k, axis=-1, keepdims=True) * v
        for d in range(1, HG_BLOCK):
            k_d = pltpu.roll(k, d, axis=0)
            b_d = pltpu.roll(b, d, axis=0)
            v_d = pltpu.roll(v, d, axis=0)
            w = jnp.sum(q * k_d * jnp.exp(jnp.minimum(b - b_d, 0.0)), axis=-1, keepdims=True)
            o = o + jnp.where(t_in_blk >= d, w * v_d, 0.0)

        st = st_ref[h]
        o_rows = []
        for c in range(n_chunks):
            r0 = c * HG_CHUNK
            bc = b[r0:r0 + HG_CHUNK]
            qc = q[r0:r0 + HG_CHUNK]
            kc = k[r0:r0 + HG_CHUNK]
            vc = v[r0:r0 + HG_CHUNK].astype(BF16)
            st_b = st.astype(BF16)
            for i in range(n_blk):
                i0 = i * HG_BLOCK
                if i == 0:
                    qt = qc[:HG_BLOCK] * jnp.exp(bc[:HG_BLOCK])
                    qs = qt
                else:
                    ref_row = bc[i0 - 1:i0]
                    qt = qc[i0:i0 + HG_BLOCK] * jnp.exp(bc[i0:i0 + HG_BLOCK] - ref_row)
                    qs = qt * jnp.exp(ref_row)
                oi = lax.dot_general(qs.astype(BF16), st_b, (((1,), (1,)), ((), ())),
                                     preferred_element_type=F32)
                if i > 0:
                    kh = kc[:i0] * jnp.exp(ref_row - bc[:i0])
                    a = lax.dot_general(qt.astype(BF16), kh.astype(BF16), (((1,), (1,)), ((), ())),
                                        preferred_element_type=F32)
                    oi = oi + jnp.dot(a.astype(BF16), vc[:i0], preferred_element_type=F32)
                o_rows.append(oi)
            b_end = bc[HG_CHUNK - 1:HG_CHUNK]
            kend = kc * jnp.exp(b_end - bc)
            vt = v[r0:r0 + HG_CHUNK].T.astype(BF16)
            st = st * jnp.exp(b_end) + jnp.dot(vt, kend.astype(BF16), preferred_element_type=F32)
        finish(h, o + jnp.concatenate(o_rows, axis=0), st)


def _hgrn(hq, hf, hi, hg, lb, ng, bsz, seq, ts):
    t, w = hq.shape
    dk = w // HG_HEADS
    n_s = seq // ts
    tile = lambda b, s: (b * n_s + s, 0)
    return pl.pallas_call(
        functools.partial(_hgrn_kernel, ts),
        out_shape=jax.ShapeDtypeStruct((t, w), BF16),
        grid=(bsz, n_s),
        in_specs=[pl.BlockSpec((ts, w), tile)] * 4
        + [pl.BlockSpec((1, w), lambda b, s: (0, 0))] * 2,
        out_specs=pl.BlockSpec((ts, w), tile),
        scratch_shapes=[pltpu.VMEM((HG_HEADS, dk, dk), F32), pltpu.VMEM((ts, w), F32)],
        compiler_params=_params(2),
        name="hgrn2",
    )(hq, hf, hi, hg, lb.reshape(1, w), ng.reshape(1, w))


def _attn_kernel(nk, nq, nr, q_ref, kp_ref, kc_ref, vp_ref, vc_ref, o_ref, lse_ref):
    n = pl.program_id(2)
    e = ATT_HEAD_DIM
    i = lax.broadcasted_iota(I32, (nk, 2 * nk), 0)
    j = lax.broadcasted_iota(I32, (nk, 2 * nk), 1)
    band = (j >= i) & (j <= i + nk)
    first_head = lax.broadcasted_iota(I32, (nk, LANES), 1) < e
    zero = jnp.zeros((), q_ref.dtype)
    for r in range(nr):
        kk = jnp.concatenate([kp_ref[0, r], kc_ref[0, r]], axis=0)
        vv = jnp.concatenate([vp_ref[0, r], vc_ref[0, r]], axis=0)
        for b in range(nq):
            valid = band & ((j >= nk) | (n * nq + b > 0))
            rows = slice(b * nk, (b + 1) * nk)
            for c in range(0, ATT_HEADS_PER_GROUP * e, LANES):
                q = q_ref[0, r, rows, c:c + LANES]
                kb = kk[b * nk:(b + 2) * nk, c:c + LANES]
                vb = vv[b * nk:(b + 2) * nk, c:c + LANES]
                outs, lses = [], []
                for keep in (first_head, jnp.logical_not(first_head)):
                    s = lax.dot_general(jnp.where(keep, q, zero), kb, (((1,), (1,)), ((), ())),
                                        preferred_element_type=F32)
                    s = jnp.where(valid, s, -jnp.inf)
                    m = jnp.max(s, axis=-1, keepdims=True)
                    p = jnp.exp(s - m)
                    l = jnp.sum(p, axis=-1, keepdims=True)
                    outs.append(jnp.dot(p.astype(BF16), vb, preferred_element_type=F32) / l)
                    lses.append(m + jnp.log(l))
                o_ref[0, r, rows, c:c + LANES] = jnp.where(first_head, outs[0], outs[1])
                lse_ref[0, r, rows, c:c + LANES] = jnp.where(first_head, lses[0], lses[1])


def _attn_group(q, k, v, g, blocks_per_step):
    window, dil = ATT_GROUPS[g]
    nk = window // dil
    bsz, _, ln, gw = q.shape
    nq = min(blocks_per_step, ln // nk)
    nr = min(blocks_per_step // nq, dil)
    assert ln % (nk * nq) == 0 and dil % nr == 0 and 2 * ATT_HEAD_DIM == LANES
    cur = pl.BlockSpec((1, nr, nq * nk, gw), lambda b, r, n: (b, r, n, 0))
    prev = pl.BlockSpec((1, nr, nk, gw), lambda b, r, n: (b, r, jnp.maximum(n * nq - 1, 0), 0))
    return pl.pallas_call(
        functools.partial(_attn_kernel, nk, nq, nr),
        out_shape=[jax.ShapeDtypeStruct(q.shape, F32)] * 2,
        grid=(bsz, dil // nr, ln // (nk * nq)),
        in_specs=[cur, prev, cur, prev, cur],
        out_specs=[cur, cur],
        compiler_params=_params(3),
        name=f"dilated_attn_g{g}",
    )(q, k, k, v, v)


def _token_major(ref, scr):
    dil, rows = ref.shape[1], ref.shape[2]
    if dil == 1:
        return ref[0, 0]
    n_col = scr.shape[0]
    for r in range(dil):
        for c in range(n_col):
            scr[c, pl.ds(r, rows, stride=dil), :] = ref[0, r, :, c * LANES:(c + 1) * LANES]
    return jnp.concatenate([scr[c] for c in range(n_col)], axis=1)


def _merge_kernel(ya_ref, o0_ref, o1_ref, o2_ref, l0_ref, l1_ref, l2_ref, ga_ref, gb_ref, x_ref,
                  g1_ref, sc2_ref, sh2_ref, g2_ref, n2_ref, wa_ref, wb_ref, wo_ref, wr_ref, wrl_ref,
                  wsg_ref, wsu_ref, wsd_ref, bias_ref, x1_ref, hp_ref, idx_ref, gate_ref, rank_ref,
                  cnt_ref, carry_ref, lg_ref, *scr):
    step = pl.program_id(0)

    @pl.when(step == 0)
    def _():
        carry_ref[...] = jnp.zeros_like(carry_ref)
        lg_ref[...] = jnp.zeros_like(lg_ref)

    _route(lg_ref[...], jnp.where(step > 0, 1.0, 0.0), bias_ref, idx_ref, gate_ref, rank_ref,
           cnt_ref, carry_ref)

    l0, l1, l2 = (_token_major(r, s) for r, s in zip((l0_ref, l1_ref, l2_ref), scr[:3]))
    o0, o1, o2 = (_token_major(r, s) for r, s in zip((o0_ref, o1_ref, o2_ref), scr[3:]))
    m = jnp.maximum(jnp.maximum(l0, l1), l2)
    e0, e1, e2 = jnp.exp(l0 - m), jnp.exp(l1 - m), jnp.exp(l2 - m)
    yb = (e0 * o0 + e1 * o1 + e2 * o2) / (e0 + e1 + e2)
    merged = (_sigmoid(ga_ref[...].astype(F32))
              * jnp.dot(ya_ref[...], wa_ref[...], preferred_element_type=F32)
              + _sigmoid(gb_ref[...].astype(F32))
              * jnp.dot(yb.astype(BF16), wb_ref[...], preferred_element_type=F32))
    x1 = x_ref[...] + g1_ref[0] * jnp.dot(merged.astype(BF16), wo_ref[...],
                                           preferred_element_type=F32)
    h2 = _rms(x1, n2_ref[...]) * (1.0 + sc2_ref[0]) + sh2_ref[0]
    hb = h2.astype(BF16)
    act = (_silu(jnp.dot(hb, wsg_ref[...], preferred_element_type=F32))
           * jnp.dot(hb, wsu_ref[...], preferred_element_type=F32))
    shared = jnp.dot(act.astype(BF16), wsd_ref[...], preferred_element_type=F32)
    x1_ref[...] = x1 + g2_ref[0] * shared
    hp_ref[...] = _pack_halves(h2)
    h_lo = (h2 - hb.astype(F32)).astype(BF16)
    nt = lambda a, b: lax.dot_general(a, b, (((1,), (1,)), ((), ())), preferred_element_type=F32)
    lg_ref[...] = nt(wr_ref[...], hb) + (nt(wr_ref[...], h_lo) + nt(wrl_ref[...], hb))


def _merge(ya, att, ga, gb, x2, gate1, scale2, shift2, gate2, norm2_g, wa, wb, wo, wr_t, wsg, wsu,
           wsd, router_bias, seq, tm):
    t, d = x2.shape
    n_e = wr_t.shape[0]
    wr_hi = wr_t.astype(BF16)
    wr_lo = (wr_t - wr_hi.astype(F32)).astype(BF16)
    n_per = seq // tm
    n_tiles = t // tm
    tile = lambda i: jnp.minimum(i, n_tiles - 1)
    per_b = lambda i: (tile(i) // n_per, 0, 0)
    rows = lambda wdt: pl.BlockSpec((tm, wdt), lambda i: (tile(i), 0))
    full = lambda a: pl.BlockSpec(a.shape, lambda i: (0,) * a.ndim)
    vec = pl.BlockSpec((1, 1, d), per_b)
    (o0, l0), (o1, l1), (o2, l2) = att
    gw = o0.shape[3]
    by_residue = lambda a: pl.BlockSpec((1, a.shape[1], tm // a.shape[1], gw),
                                        lambda i: (tile(i) // n_per, 0, tile(i) % n_per, 0))
    att_in = (o0, o1, o2, l0, l1, l2)
    bias_col = router_bias.reshape(n_e, 1)
    tok = pl.BlockSpec((TOP_K, tm), lambda i: (0, jnp.maximum(i - 1, 0)))
    return pl.pallas_call(
        _merge_kernel,
        out_shape=[jax.ShapeDtypeStruct((t, d), F32),
                   jax.ShapeDtypeStruct((t, d // 2), U32),
                   jax.ShapeDtypeStruct((TOP_K, t), I32), jax.ShapeDtypeStruct((TOP_K, t), F32),
                   jax.ShapeDtypeStruct((TOP_K, t), I32), jax.ShapeDtypeStruct((n_e, LANES), I32)],
        grid=(n_tiles + 1,),
        in_specs=[rows(ya.shape[1])] + [by_residue(a) for a in att_in] + [rows(d)] * 3
        + [vec, vec, vec, vec, pl.BlockSpec((1, d), lambda i: (0, 0))]
        + [full(a) for a in (wa, wb, wo, wr_hi, wr_lo, wsg, wsu, wsd, bias_col)],
        out_specs=[rows(d), rows(d // 2), tok, tok, tok,
                   pl.BlockSpec((n_e, LANES), lambda i: (0, 0))],
        scratch_shapes=[pltpu.VMEM((n_e, 1), F32), pltpu.VMEM((n_e, tm), F32)]
        + [pltpu.VMEM((gw // LANES, tm, LANES), F32)] * 6,
        compiler_params=_params(),
        name="merge_router",
    )(ya, *att_in, ga, gb, x2, gate1, scale2, shift2, gate2,
      norm2_g.reshape(1, d), wa, wb, wo, wr_hi, wr_lo, wsg, wsu, wsd, bias_col)


def _route(logits, live, bias_ref, idx_ref, gate_ref, rank_ref, cnt_ref, carry_ref):
    n_e, tt = logits.shape
    scores = _sigmoid(logits)
    sel = scores + bias_ref[...]
    eio = lax.broadcasted_iota(I32, (n_e, tt), 0)
    picked = jnp.zeros((n_e, tt), F32)
    idxs, vals = [], []
    for _ in range(TOP_K):
        m = jnp.max(sel, axis=0, keepdims=True)
        ik = jnp.min(jnp.where(sel == m, eio, n_e), axis=0, keepdims=True)
        hit = eio == ik
        vals.append(jnp.sum(jnp.where(hit, scores, 0.0), axis=0, keepdims=True))
        sel = jnp.where(hit, -jnp.inf, sel)
        picked = picked + jnp.where(hit, 1.0, 0.0)
        idxs.append(ik)
    denom = vals[0]
    for v in vals[1:]:
        denom = denom + v
    gate_ref[...] = jnp.concatenate([v / denom * ROUTE_SCALE for v in vals], axis=0)
    idx_ref[...] = jnp.concatenate(idxs, axis=0)

    upper = (lax.broadcasted_iota(I32, (tt, tt), 0) <= lax.broadcasted_iota(I32, (tt, tt), 1))
    incl = jnp.dot(picked.astype(BF16), jnp.where(upper, 1.0, 0.0).astype(BF16),
                   preferred_element_type=F32)
    before = incl - picked + carry_ref[...]
    rank_ref[...] = jnp.concatenate(
        [jnp.sum(jnp.where(eio == ik, before, 0.0), axis=0, keepdims=True) for ik in idxs],
        axis=0).astype(I32)
    carry_ref[...] = carry_ref[...] + jnp.sum(picked, axis=1, keepdims=True) * live
    cnt_ref[...] = jnp.broadcast_to(carry_ref[...], cnt_ref.shape).astype(I32)


def _dest_kernel(idx_ref, rank_ref, start_ref, o_ref):
    k, tt = idx_ref.shape
    n_e = start_ref.shape[0]
    eio = lax.broadcasted_iota(I32, (n_e, tt), 0)
    start = start_ref[...]
    rows = [jnp.sum(jnp.where(eio == idx_ref[r:r + 1, :], start, 0), axis=0, keepdims=True)
            for r in range(k)]
    o_ref[...] = jnp.concatenate(rows, axis=0) + rank_ref[...]


def _dest(idx, rank, seg_start, tt):
    k, t = idx.shape
    n_e = seg_start.shape[0]
    tok = pl.BlockSpec((k, tt), lambda i: (0, i))
    return pl.pallas_call(
        _dest_kernel,
        out_shape=jax.ShapeDtypeStruct((k, t), I32),
        grid=(t // tt,),
        in_specs=[tok, tok, pl.BlockSpec((n_e, 1), lambda i: (0, 0))],
        out_specs=tok,
        compiler_params=_params(),
        name="moe_dest",
    )(idx, rank, seg_start.reshape(n_e, 1))


def _sc_mesh():
    return plsc.VectorSubcoreMesh(core_axis_name="core", subcore_axis_name="subcore")


def _sc_scatter_rows(rows, dest, n_out):
    k, t = dest.shape
    w = rows.shape[1]
    mesh = _sc_mesh()
    n_workers = mesh.num_cores * mesh.num_subcores
    win_per_worker = t // (SC_WINDOW * n_workers)
    assert win_per_worker * SC_WINDOW * n_workers == t

    @functools.partial(
        pl.kernel, out_type=jax.ShapeDtypeStruct((n_out, w), rows.dtype), mesh=mesh,
        scratch_types=[pltpu.VMEM((SC_WINDOW, w), rows.dtype)]
        + [pltpu.VMEM((1, SC_WINDOW), I32)] * k + [pltpu.SemaphoreType.DMA],
        name="moe_dispatch_sc")
    def run(rows_hbm, idx_hbm, out_hbm, rows_v, *rest):
        idx_v, sem = rest[:k], rest[k]
        worker = lax.axis_index("subcore") * mesh.num_cores + lax.axis_index("core")

        @pl.loop(0, win_per_worker)
        def _(j):
            t0 = pl.multiple_of((worker * win_per_worker + j) * SC_WINDOW, SC_WINDOW)
            pltpu.sync_copy(rows_hbm.at[pl.ds(t0, SC_WINDOW)], rows_v)
            for r in range(k):
                pltpu.sync_copy(idx_hbm.at[:, pl.ds(r * t + t0, SC_WINDOW)], idx_v[r])
            copies = [pltpu.async_copy(rows_v, out_hbm.at[idx_v[r].at[0]], sem) for r in range(k)]
            for c in copies:
                c.wait()

    return run(rows, dest.reshape(1, k * t))


def _sc_gather_rows(table, dest):
    k, t = dest.shape
    w = table.shape[1]
    mesh = _sc_mesh()
    n_workers = mesh.num_cores * mesh.num_subcores
    win_per_worker = (k * t) // (SC_WINDOW * n_workers)
    assert win_per_worker * SC_WINDOW * n_workers == k * t

    @functools.partial(
        pl.kernel, out_type=jax.ShapeDtypeStruct((k * t, w), table.dtype), mesh=mesh,
        scratch_types=[pltpu.VMEM((SC_WINDOW, w), table.dtype), pltpu.VMEM((1, SC_WINDOW), I32)],
        name="moe_gather_sc")
    def run(table_hbm, idx_hbm, out_hbm, rows_v, idx_v):
        worker = lax.axis_index("subcore") * mesh.num_cores + lax.axis_index("core")

        @pl.loop(0, win_per_worker)
        def _(j):
            p0 = pl.multiple_of((worker * win_per_worker + j) * SC_WINDOW, SC_WINDOW)
            pltpu.sync_copy(idx_hbm.at[:, pl.ds(p0, SC_WINDOW)], idx_v)
            pltpu.sync_copy(table_hbm.at[idx_v.at[0]], rows_v)
            pltpu.sync_copy(rows_v, out_hbm.at[pl.ds(p0, SC_WINDOW)])

    return run(table, dest.reshape(1, k * t))


def _expert_kernel(start_ref, nblk_ref, xs_ref, wg_ref, wu_ref, wd_ref, ys_ref,
                   xbuf, ybuf, wgb, wub, wdb, wbuf_g, wbuf_u, wbuf_d, sem_in, sem_out, sem_w):
    wbuf = (wbuf_g, wbuf_u, wbuf_d)
    e = pl.program_id(0)
    n_e = pl.num_programs(0)
    nb = nblk_ref[e]
    g0 = start_ref[e] // MOE_BLOCK
    n_used = start_ref[n_e - 1] // MOE_BLOCK + nblk_ref[n_e - 1]
    n_in, n_out = xbuf.shape[0], ybuf.shape[0]

    def rows(g):
        return pl.ds(pl.multiple_of(g * MOE_BLOCK, MOE_BLOCK), MOE_BLOCK)

    def in_copy(g):
        slot = lax.rem(g, n_in)
        return pltpu.make_async_copy(xs_ref.at[rows(g), :], xbuf.at[slot], sem_in.at[slot])

    def out_copy(g):
        slot = lax.rem(g, n_out)
        return pltpu.make_async_copy(ybuf.at[slot], ys_ref.at[rows(g), :], sem_out.at[slot])

    look = n_in - EXPERT_GROUP

    @pl.when(e == 0)
    def _():
        for g in range(look):
            @pl.when(g < n_used)
            def _():
                in_copy(g).start(priority=g % N_DMA_QUEUES)

    n_w = wbuf[0].shape[0]

    def weight_copies(ex):
        slot = lax.rem(ex, n_w)
        return [pltpu.make_async_copy(src.at[ex], buf.at[slot], sem_w.at[slot])
                for src, buf in zip((wg_ref, wu_ref, wd_ref), wbuf)]

    @pl.when(e == 0)
    def _():
        for ex in range(min(n_w, wg_ref.shape[0])):
            for c in weight_copies(ex):
                c.start()

    for c in weight_copies(e):
        c.wait()
    w_slot = lax.rem(e, n_w)

    @pl.when(nb > 0)
    def _():
        wgb[...] = wbuf[0][w_slot].astype(BF16)
        wub[...] = wbuf[1][w_slot].astype(BF16)
        wdb[...] = wbuf[2][w_slot].astype(BF16)

    @pl.when(e + n_w < n_e)
    def _():
        for c in weight_copies(e + n_w):
            c.start()

    @pl.when(nb > 0)
    def _():
        def swiglu(word):
            lo, hi = _unpack_halves(word)
            x = jnp.concatenate([lo.astype(BF16), hi.astype(BF16)], axis=1)
            gate = jnp.dot(x, wgb[...], preferred_element_type=F32)
            up = jnp.dot(x, wub[...], preferred_element_type=F32)
            act = (_silu(gate) * up).astype(BF16)
            return jnp.dot(act, wdb[...], preferred_element_type=F32)

        def process(g, m):
            for i in range(m):
                in_copy(g + i).wait()
            for i in range(m):
                @pl.when(g + look + i < n_used)
                def _():
                    in_copy(g + look + i).start(priority=i % N_DMA_QUEUES)
            ys = [swiglu(xbuf[lax.rem(g + i, n_in)]) for i in range(m)]
            for i in range(m):
                @pl.when(g + i >= n_out)
                def _():
                    out_copy(g + i - n_out).wait()

                ybuf[lax.rem(g + i, n_out)] = _pack_halves(ys[i])
                out_copy(g + i).start(priority=(i + 1) % N_DMA_QUEUES)

        def group_body(p, carry):
            process(g0 + p * EXPERT_GROUP, EXPERT_GROUP)
            return carry

        lax.fori_loop(0, nb // EXPERT_GROUP, group_body, 0)
        for m in range(1, EXPERT_GROUP):
            @pl.when(lax.rem(nb, EXPERT_GROUP) == m)
            def _():
                process(g0 + nb - m, m)

    @pl.when(e == n_e - 1)
    def _():
        for i in range(n_out):
            @pl.when(n_used - 1 - i >= 0)
            def _():
                out_copy(n_used - 1 - i).wait()


def _experts(seg_start, seg_blocks, xs, wg, wu, wd):
    n_slots, half = xs.shape
    n_e, d, de = wg.shape
    n_w = EXPERT_WEIGHT_BUFFERS
    return pl.pallas_call(
        _expert_kernel,
        out_shape=jax.ShapeDtypeStruct((n_slots, half), U32),
        grid_spec=pltpu.PrefetchScalarGridSpec(
            num_scalar_prefetch=2,
            grid=(n_e,),
            in_specs=[pl.BlockSpec(memory_space=pl.ANY)] * 4,
            out_specs=pl.BlockSpec(memory_space=pl.ANY),
            scratch_shapes=[pltpu.VMEM((EXPERT_IN_RING, MOE_BLOCK, half), U32),
                            pltpu.VMEM((EXPERT_OUT_RING, MOE_BLOCK, half), U32),
                            pltpu.VMEM((d, de), BF16), pltpu.VMEM((d, de), BF16),
                            pltpu.VMEM((de, d), BF16),
                            pltpu.VMEM((n_w, d, de), F32), pltpu.VMEM((n_w, d, de), F32),
                            pltpu.VMEM((n_w, de, d), F32),
                            pltpu.SemaphoreType.DMA((EXPERT_IN_RING,)),
                            pltpu.SemaphoreType.DMA((EXPERT_OUT_RING,)),
                            pltpu.SemaphoreType.DMA((n_w,))]),
        compiler_params=_params(),
        name="moe_experts",
    )(seg_start, seg_blocks, xs, wg, wu, wd)


def _combine_kernel(yg_ref, gt_ref, x_ref, g2_ref, fg_ref, o_ref):
    k = yg_ref.shape[0]
    gt = gt_ref[...]
    lo, hi = _unpack_halves(yg_ref[0])
    y_lo, y_hi = lo * gt[:, 0:1], hi * gt[:, 0:1]
    for r in range(1, k):
        lo, hi = _unpack_halves(yg_ref[r])
        y_lo, y_hi = y_lo + lo * gt[:, r:r + 1], y_hi + hi * gt[:, r:r + 1]
    y = jnp.concatenate([y_lo, y_hi], axis=1)
    o_ref[...] = _rms(x_ref[...] + g2_ref[0] * y, fg_ref[...])


def _combine_into_kernel(yg_ref, gt_ref, x_ref, g2_ref, fg_ref, prev_ref, o_ref):
    del prev_ref
    _combine_kernel(yg_ref, gt_ref, x_ref, g2_ref, fg_ref, o_ref)


def _combine(yg, tok0, gates_t, x1s, gate2, final_g, seq, tc, out_so_far=None):
    k, n, half = yg.shape
    t, d = x1s.shape
    b0 = tok0 // tc
    args = [yg, gates_t, x1s, gate2, final_g.reshape(1, d)]
    in_specs = [pl.BlockSpec((k, tc, half), lambda i: (0, i, 0)),
                pl.BlockSpec((tc, k), lambda i: (i + b0, 0)),
                pl.BlockSpec((tc, d), lambda i: (i + b0, 0)),
                pl.BlockSpec((1, 1, d), lambda i: (((i + b0) * tc) // seq, 0, 0)),
                pl.BlockSpec((1, d), lambda i: (0, 0))]
    aliases = {}
    kernel = _combine_kernel
    if out_so_far is not None:
        args.append(out_so_far)
        in_specs.append(pl.BlockSpec(memory_space=pl.ANY))
        aliases = {len(args) - 1: 0}
        kernel = _combine_into_kernel
    return pl.pallas_call(
        kernel,
        out_shape=jax.ShapeDtypeStruct((t, d), F32),
        grid=(n // tc,),
        in_specs=in_specs,
        out_specs=pl.BlockSpec((tc, d), lambda i: (i + b0, 0)),
        input_output_aliases=aliases,
        compiler_params=_params(),
        name="moe_combine",
    )(*args)


def _layer(x2, c, bsz, seq, lb_row, ada_w, ada_b, norm1_g, w_in, hg_norm_g, w_branch_a, w_branch_b,
           w_out, norm2_g, w_router, router_bias, w_exp_gate, w_exp_up, w_exp_down, w_sh_gate,
           w_sh_up, w_sh_down, final_g):
    t, d = x2.shape
    n_e = w_router.shape[1]
    mod = _ada(c, ada_w, ada_b).reshape(bsz, 6, 1, d)
    shift1, scale1, gate1, shift2, scale2, gate2 = (mod[:, j] for j in range(6))

    hw = hg_norm_g.shape[0]
    aw = len(ATT_GROUPS) * ATT_HEADS_PER_GROUP * ATT_HEAD_DIM
    flat_segs = [(0, hw, BF16), (hw, hw, F32), (2 * hw, hw, BF16), (3 * hw, hw, BF16),
                 (4 * hw + 3 * aw, d, BF16), (4 * hw + 3 * aw + d, d, BF16)]
    (hq, hf, hi, hg, ga, gb), qkv = _inproj(
        x2, norm1_g, scale1, shift1, w_in.astype(BF16), bsz, seq, flat_segs, 4 * hw,
        tm=IN_PROJ_TILE)

    ya = _hgrn(hq, hf, hi, hg, lb_row, hg_norm_g, bsz, seq, ts=HGRN_TILE)
    att = [_attn_group(*qkv[3 * g:3 * g + 3], g, blocks_per_step=ATT_BLOCKS_PER_STEP)
           for g in range(len(ATT_GROUPS))]

    x1s, hp, idx, gates, rank, cnt = _merge(
        ya, att, ga, gb, x2, gate1, scale2, shift2, gate2, norm2_g, w_branch_a.astype(BF16),
        w_branch_b.astype(BF16), w_out.astype(BF16), w_router.T, w_sh_gate.astype(BF16),
        w_sh_up.astype(BF16), w_sh_down.astype(BF16), router_bias, seq, tm=MERGE_TILE)
    counts = cnt[:, 0]
    padded = (counts + MOE_BLOCK - 1) // MOE_BLOCK * MOE_BLOCK
    seg_start = (jnp.cumsum(padded) - padded).astype(I32)
    n_blocks = -(-(t * TOP_K) // MOE_BLOCK) + n_e
    dest = _dest(idx, rank, seg_start, tt=DEST_TILE)

    xs = _sc_scatter_rows(hp, dest, n_blocks * MOE_BLOCK)
    ys = _experts(seg_start, (padded // MOE_BLOCK).astype(I32), xs, w_exp_gate, w_exp_up,
                  w_exp_down)
    out, n = None, t // COMBINE_PARTS
    for part in range(COMBINE_PARTS):
        yg = _sc_gather_rows(ys, dest[:, part * n:(part + 1) * n]).reshape(TOP_K, n, d // 2)
        out = _combine(yg, part * n, gates.T, x1s, gate2, final_g, seq, tc=COMBINE_TILE,
                       out_so_far=out)
    return out


def kernel(x, c, ada_w, ada_b, norm1_g, w_in, lb_logits, hg_norm_g, w_branch_a, w_branch_b, w_out,
           norm2_g, w_router, router_bias, w_exp_gate, w_exp_up, w_exp_down, w_sh_gate, w_sh_up,
           w_sh_down, final_g):
    bsz, seq, d = x.shape
    depth = ada_w.shape[0]
    assert depth == 1, "the last layer's kernels also apply the final norm"
    lb_table = jnp.cumsum(jax.nn.softmax(lb_logits.astype(F32), axis=0), axis=0)
    out = _layer(x.reshape(bsz * seq, d), c, bsz, seq, lb_table[0], ada_w[0], ada_b[0], norm1_g[0],
                 w_in[0], hg_norm_g[0], w_branch_a[0], w_branch_b[0], w_out[0], norm2_g[0],
                 w_router[0], router_bias[0], w_exp_gate[0], w_exp_up[0], w_exp_down[0],
                 w_sh_gate[0], w_sh_up[0], w_sh_down[0], final_g)
    return out.reshape(bsz, seq, d)
```

```python
import functools

import jax
import jax.numpy as jnp
from jax import lax
from jax.experimental import pallas as pl
from jax.experimental.pallas import tpu as pltpu
from jax.experimental.pallas import tpu_sc as plsc

F32 = jnp.float32
BF16 = jnp.bfloat16
I32 = jnp.int32
U32 = jnp.uint32
HIGHEST = lax.Precision.HIGHEST

HG_HEADS = 4
HG_BLOCK = 16
HG_CHUNK = 32
HG_MILD_DECAY = -80.0
ATT_GROUPS = ((128, 1), (512, 4), (2048, 16))
ATT_HEADS_PER_GROUP = 4
ATT_HEAD_DIM = 64
TOP_K = 8
ROUTE_SCALE = 2.5
MOE_BLOCK = 256
RMS_EPS = 1e-6
N_DMA_QUEUES = 2
SC_WINDOW = 128
COMBINE_PARTS = 8
EXPERT_WEIGHT_BUFFERS = 3
EXPERT_GROUP = 4
EXPERT_IN_RING = 8
EXPERT_OUT_RING = 6

LANES = 128
VMEM_LIMIT_BYTES = 56 * 1024 * 1024

IN_PROJ_TILE = 512
HGRN_TILE = 512
ATT_BLOCKS_PER_STEP = 8
MERGE_TILE = 512
DEST_TILE = 512
COMBINE_TILE = 256


def _sigmoid(x):
    return 1.0 / (1.0 + jnp.exp(-x))


def _silu(x):
    return x * _sigmoid(x)


def _rms(x, g):
    return x * lax.rsqrt(jnp.mean(x * x, axis=-1, keepdims=True) + RMS_EPS) * g


def _pack_halves(x):
    n = x.shape[1] // 2
    bits = lax.bitcast_convert_type(x.astype(BF16).astype(F32), U32)
    return (bits[:, :n] >> 16) | (bits[:, n:] & jnp.uint32(0xFFFF0000))


def _unpack_halves(word):
    lo = lax.bitcast_convert_type(word << 16, F32)
    hi = lax.bitcast_convert_type(word & jnp.uint32(0xFFFF0000), F32)
    return lo, hi


def _params(n_axes=1):
    return pltpu.CompilerParams(
        dimension_semantics=("arbitrary",) * n_axes, vmem_limit_bytes=VMEM_LIMIT_BYTES)


def _ada_kernel(c_ref, w_ref, b_ref, o_ref):
    sc = _silu(c_ref[...])
    o_ref[...] = jnp.dot(sc, w_ref[...], preferred_element_type=F32, precision=HIGHEST) + b_ref[...]


def _ada(c, w, b):
    bsz, d = c.shape
    n = w.shape[1]
    return pl.pallas_call(
        _ada_kernel,
        out_shape=jax.ShapeDtypeStruct((bsz, n), F32),
        grid=(n // d,),
        in_specs=[pl.BlockSpec((bsz, d), lambda j: (0, 0)),
                  pl.BlockSpec((d, d), lambda j: (0, j)),
                  pl.BlockSpec((1, d), lambda j: (0, j))],
        out_specs=pl.BlockSpec((bsz, d), lambda j: (0, j)),
        compiler_params=_params(),
        name="ada_mod",
    )(c, w, b.reshape(1, n))


def _inproj_kernel(n_flat, flat_ranges, att_c0, x_ref, g_ref, sc_ref, sh_ref, w_ref, *refs):
    flat_refs, att_refs, scr = refs[:n_flat], refs[n_flat:-1], refs[-1]
    tm = x_ref.shape[0]
    h = _rms(x_ref[...], g_ref[...]) * (1.0 + sc_ref[0]) + sh_ref[0]
    hb = h.astype(BF16)
    for (c0, c1), o_ref in zip(flat_ranges, flat_refs):
        o_ref[...] = jnp.dot(hb, w_ref[:, c0:c1], preferred_element_type=F32).astype(o_ref.dtype)
    gw = ATT_HEADS_PER_GROUP * ATT_HEAD_DIM
    n_groups = len(ATT_GROUPS)
    for part in range(3):
        c0 = att_c0 + part * n_groups * gw
        res = jnp.dot(hb, w_ref[:, c0:c0 + n_groups * gw], preferred_element_type=F32)
        if part == 0:
            res = res * (ATT_HEAD_DIM ** -0.5)
        for g, (_, dil) in enumerate(ATT_GROUPS):
            o_ref = att_refs[g * 3 + part]
            sub = res[:, g * gw:(g + 1) * gw]
            if dil == 1:
                o_ref[0, 0] = sub.astype(BF16)
            else:
                for c in range(gw // LANES):
                    scr[c] = sub[:, c * LANES:(c + 1) * LANES]
                for r in range(dil):
                    o_ref[0, r] = jnp.concatenate(
                        [scr[c, pl.ds(r, tm // dil, stride=dil), :] for c in range(gw // LANES)],
                        axis=1).astype(BF16)


def _inproj(x2, g, scale, shift, w_bf16, bsz, seq, flat_segs, att_c0, tm):
    t, d = x2.shape
    gw = ATT_HEADS_PER_GROUP * ATT_HEAD_DIM
    n_per = seq // tm
    per_b = lambda i: (i // n_per, 0, 0)
    att_shapes, att_specs = [], []
    for _, dil in ATT_GROUPS:
        for _ in range(3):
            att_shapes.append(jax.ShapeDtypeStruct((bsz, dil, seq // dil, gw), BF16))
            att_specs.append(pl.BlockSpec((1, dil, tm // dil, gw),
                                          lambda i: (i // n_per, 0, i % n_per, 0)))
    outs = pl.pallas_call(
        functools.partial(_inproj_kernel, len(flat_segs),
                          tuple((c0, c0 + wdt) for c0, wdt, _ in flat_segs), att_c0),
        out_shape=[jax.ShapeDtypeStruct((t, wdt), dt) for _, wdt, dt in flat_segs] + att_shapes,
        grid=(t // tm,),
        in_specs=[pl.BlockSpec((tm, d), lambda i: (i, 0)),
                  pl.BlockSpec((1, d), lambda i: (0, 0)),
                  pl.BlockSpec((1, 1, d), per_b),
                  pl.BlockSpec((1, 1, d), per_b),
                  pl.BlockSpec(w_bf16.shape, lambda i: (0, 0))],
        out_specs=[pl.BlockSpec((tm, wdt), lambda i: (i, 0)) for _, wdt, _ in flat_segs]
        + att_specs,
        scratch_shapes=[pltpu.VMEM((gw // LANES, tm, LANES), F32)],
        compiler_params=_params(),
        name="in_proj",
    )(x2, g.reshape(1, d), scale, shift, w_bf16)
    return outs[:len(flat_segs)], outs[len(flat_segs):]


def _hgrn_kernel(ts, q_ref, f_ref, v_ref, gt_ref, lb_ref, ng_ref, o_ref, st_ref, b_ref):
    dk = q_ref.shape[1] // HG_HEADS
    n_chunks = ts // HG_CHUNK
    n_blk = HG_CHUNK // HG_BLOCK

    @pl.when(pl.program_id(1) == 0)
    def _():
        st_ref[...] = jnp.zeros_like(st_ref)

    row = lax.broadcasted_iota(I32, (LANES, LANES), 0)
    col = lax.broadcasted_iota(I32, (LANES, LANES), 1)
    same_chunk = (row // HG_CHUNK) == (col // HG_CHUNK)
    cum_mat = jnp.where(same_chunk & (col <= row), 1.0, 0.0).astype(BF16)

    def chunk_cumsum(x):
        out = []
        for r0 in range(0, ts, LANES):
            rest = x[r0:r0 + LANES]
            acc = None
            for _ in range(3):
                term = rest.astype(BF16)
                part = jnp.dot(cum_mat, term, preferred_element_type=F32)
                acc = part if acc is None else acc + part
                rest = rest - term.astype(F32)
            out.append(acc)
        return jnp.concatenate(out, axis=0)

    def forget(cs):
        lb = lb_ref[:, cs]
        return lb + (1.0 - lb) * _sigmoid(f_ref[:, cs])

    b_min = None
    for h in range(HG_HEADS):
        cs = slice(h * dk, (h + 1) * dk)
        b = chunk_cumsum(jnp.log(forget(cs)))
        b_ref[:, cs] = b
        m = jnp.min(b)
        b_min = m if b_min is None else jnp.minimum(b_min, m)
    mild = b_min >= HG_MILD_DECAY

    def finish(h, o, st):
        cs = slice(h * dk, (h + 1) * dk)
        st_ref[h] = st
        y = _rms(o, ng_ref[:, cs]) * _silu(gt_ref[:, cs].astype(F32))
        o_ref[:, cs] = y.astype(o_ref.dtype)

    @pl.when(mild)
    def _():
        span = 2 * HG_CHUNK
        causal = (lax.broadcasted_iota(I32, (span, span), 0)
                  >= lax.broadcasted_iota(I32, (span, span), 1))
        nt = lambda x, y: lax.dot_general(x, y, (((1,), (1,)), ((), ())),
                                          preferred_element_type=F32)
        for h in range(HG_HEADS):
            cs = slice(h * dk, (h + 1) * dk)
            v = v_ref[:, cs]
            b = b_ref[:, cs]
            q = q_ref[:, cs].astype(F32)
            k = 1.0 - forget(cs)
            st = st_ref[h]
            o_rows = []
            for r0 in range(0, ts, span):
                sl = slice(r0, r0 + span)
                b_first, b_second = b[r0:r0 + HG_CHUNK], b[r0 + HG_CHUNK:r0 + span]
                end_first = b_first[HG_CHUNK - 1:HG_CHUNK]
                end_second = b_second[HG_CHUNK - 1:HG_CHUNK]
                e = jnp.exp(jnp.concatenate([b_first - end_first, b_second], axis=0))
                qe = (q[sl] * e).astype(BF16)
                ke = k[sl] / e
                a = jnp.where(causal, nt(qe, ke.astype(BF16)), 0.0).astype(BF16)
                st_in = (st * jnp.exp(end_first)).astype(BF16)
                o_rows.append(jnp.dot(a, v[sl], preferred_element_type=F32) + nt(qe, st_in))
                kend = (ke * jnp.exp(end_second)).astype(BF16)
                vt = v[sl].astype(F32).T.astype(BF16)
                st = (st * jnp.exp(end_first + end_second)
                      + jnp.dot(vt, kend, preferred_element_type=F32))
            finish(h, jnp.concatenate(o_rows, axis=0), st)

    @pl.when(jnp.logical_not(mild))
    def _():
        _hgrn_steep(ts, dk, n_chunks, n_blk, q_ref, v_ref, b_ref, st_ref, forget, finish)


def _hgrn_steep(ts, dk, n_chunks, n_blk, q_ref, v_ref, b_ref, st_ref, forget, finish):
    t_in_blk = lax.broadcasted_iota(I32, (ts, dk), 0) % HG_BLOCK

    for h in range(HG_HEADS):
        cs = slice(h * dk, (h + 1) * dk)
        q = q_ref[:, cs].astype(F32)
        v = v_ref[:, cs].astype(F32)
        k = 1.0 - forget(cs)
        b = b_ref[:, cs]

        o = jnp.sum(q * k, axis=-1, keepdims=True) * v
        for d in range(1, HG_BLOCK):
            k_d = pltpu.roll(k, d, axis=0)
            b_d = pltpu.roll(b, d, axis=0)
            v_d = pltpu.roll(v, d, axis=0)
            w = jnp.sum(q * k_d * jnp.exp(jnp.minimum(b - b_d, 0.0)), axis=-1, keepdims=True)
            o = o + jnp.where(t_in_blk >= d, w * v_d, 0.0)

        st = st_ref[h]
        o_rows = []
        for c in range(n_chunks):
            r0 = c * HG_CHUNK
            bc = b[r0:r0 + HG_CHUNK]
            qc = q[r0:r0 + HG_CHUNK]
            kc = k[r0:r0 + HG_CHUNK]
            vc = v[r0:r0 + HG_CHUNK].astype(BF16)
            st_b = st.astype(BF16)
            for i in range(n_blk):
                i0 = i * HG_BLOCK
                if i == 0:
                    qt = qc[:HG_BLOCK] * jnp.exp(bc[:HG_BLOCK])
                    qs = qt
                else:
                    ref_row = bc[i0 - 1:i0]
                    qt = qc[i0:i0 + HG_BLOCK] * jnp.exp(bc[i0:i0 + HG_BLOCK] - ref_row)
                    qs = qt * jnp.exp(ref_row)
                oi = lax.dot_general(qs.astype(BF16), st_b, (((1,), (1,)), ((), ())),
                                     preferred_element_type=F32)
                if i > 0:
                    kh = kc[:i0] * jnp.exp(ref_row - bc[:i0])
                    a = lax.dot_general(qt.astype(BF16), kh.astype(BF16), (((1,), (1,)), ((), ())),
                                        preferred_element_type=F32)
                    oi = oi + jnp.dot(a.astype(BF16), vc[:i0], preferred_element_type=F32)
                o_rows.append(oi)
            b_end = bc[HG_CHUNK - 1:HG_CHUNK]
            kend = kc * jnp.exp(b_end - bc)
            vt = v[r0:r0 + HG_CHUNK].T.astype(BF16)
            st = st * jnp.exp(b_end) + jnp.dot(vt, kend.astype(BF16), preferred_element_type=F32)
        finish(h, o + jnp.concatenate(o_rows, axis=0), st)


def _hgrn(hq, hf, hi, hg, lb, ng, bsz, seq, ts):
    t, w = hq.shape
    dk = w // HG_HEADS
    n_s = seq // ts
    tile = lambda b, s: (b * n_s + s, 0)
    return pl.pallas_call(
        functools.partial(_hgrn_kernel, ts),
        out_shape=jax.ShapeDtypeStruct((t, w), BF16),
        grid=(bsz, n_s),
        in_specs=[pl.BlockSpec((ts, w), tile)] * 4
        + [pl.BlockSpec((1, w), lambda b, s: (0, 0))] * 2,
        out_specs=pl.BlockSpec((ts, w), tile),
        scratch_shapes=[pltpu.VMEM((HG_HEADS, dk, dk), F32), pltpu.VMEM((ts, w), F32)],
        compiler_params=_params(2),
        name="hgrn2",
    )(hq, hf, hi, hg, lb.reshape(1, w), ng.reshape(1, w))


def _attn_kernel(nk, nq, nr, q_ref, kp_ref, kc_ref, vp_ref, vc_ref, o_ref, lse_ref):
    n = pl.program_id(2)
    e = ATT_HEAD_DIM
    i = lax.broadcasted_iota(I32, (nk, 2 * nk), 0)
    j = lax.broadcasted_iota(I32, (nk, 2 * nk), 1)
    band = (j >= i) & (j <= i + nk)
    first_head = lax.broadcasted_iota(I32, (nk, LANES), 1) < e
    zero = jnp.zeros((), q_ref.dtype)
    for r in range(nr):
        kk = jnp.concatenate([kp_ref[0, r], kc_ref[0, r]], axis=0)
        vv = jnp.concatenate([vp_ref[0, r], vc_ref[0, r]], axis=0)
        for b in range(nq):
            valid = band & ((j >= nk) | (n * nq + b > 0))
            rows = slice(b * nk, (b + 1) * nk)
            for c in range(0, ATT_HEADS_PER_GROUP * e, LANES):
                q = q_ref[0, r, rows, c:c + LANES]
                kb = kk[b * nk:(b + 2) * nk, c:c + LANES]
                vb = vv[b * nk:(b + 2) * nk, c:c + LANES]
                outs, lses = [], []
                for keep in (first_head, jnp.logical_not(first_head)):
                    s = lax.dot_general(jnp.where(keep, q, zero), kb, (((1,), (1,)), ((), ())),
                                        preferred_element_type=F32)
                    s = jnp.where(valid, s, -jnp.inf)
                    m = jnp.max(s, axis=-1, keepdims=True)
                    p = jnp.exp(s - m)
                    l = jnp.sum(p, axis=-1, keepdims=True)
                    outs.append(jnp.dot(p.astype(BF16), vb, preferred_element_type=F32) / l)
                    lses.append(m + jnp.log(l))
                o_ref[0, r, rows, c:c + LANES] = jnp.where(first_head, outs[0], outs[1])
                lse_ref[0, r, rows, c:c + LANES] = jnp.where(first_head, lses[0], lses[1])


def _attn_group(q, k, v, g, blocks_per_step):
    window, dil = ATT_GROUPS[g]
    nk = window // dil
    bsz, _, ln, gw = q.shape
    nq = min(blocks_per_step, ln // nk)
    nr = min(blocks_per_step // nq, dil)
    assert ln % (nk * nq) == 0 and dil % nr == 0 and 2 * ATT_HEAD_DIM == LANES
    cur = pl.BlockSpec((1, nr, nq * nk, gw), lambda b, r, n: (b, r, n, 0))
    prev = pl.BlockSpec((1, nr, nk, gw), lambda b, r, n: (b, r, jnp.maximum(n * nq - 1, 0), 0))
    return pl.pallas_call(
        functools.partial(_attn_kernel, nk, nq, nr),
        out_shape=[jax.ShapeDtypeStruct(q.shape, F32)] * 2,
        grid=(bsz, dil // nr, ln // (nk * nq)),
        in_specs=[cur, prev, cur, prev, cur],
        out_specs=[cur, cur],
        compiler_params=_params(3),
        name=f"dilated_attn_g{g}",
    )(q, k, k, v, v)


def _token_major(ref, scr):
    dil, rows = ref.shape[1], ref.shape[2]
    if dil == 1:
        return ref[0, 0]
    n_col = scr.shape[0]
    for r in range(dil):
        for c in range(n_col):
            scr[c, pl.ds(r, rows, stride=dil), :] = ref[0, r, :, c * LANES:(c + 1) * LANES]
    return jnp.concatenate([scr[c] for c in range(n_col)], axis=1)


def _merge_kernel(ya_ref, o0_ref, o1_ref, o2_ref, l0_ref, l1_ref, l2_ref, ga_ref, gb_ref, x_ref,
                  g1_ref, sc2_ref, sh2_ref, g2_ref, n2_ref, wa_ref, wb_ref, wo_ref, wr_ref, wrl_ref,
                  wsg_ref, wsu_ref, wsd_ref, bias_ref, x1_ref, hp_ref, idx_ref, gate_ref, rank_ref,
                  cnt_ref, carry_ref, lg_ref, *scr):
    step = pl.program_id(0)

    @pl.when(step == 0)
    def _():
        carry_ref[...] = jnp.zeros_like(carry_ref)
        lg_ref[...] = jnp.zeros_like(lg_ref)

    _route(lg_ref[...], jnp.where(step > 0, 1.0, 0.0), bias_ref, idx_ref, gate_ref, rank_ref,
           cnt_ref, carry_ref)

    l0, l1, l2 = (_token_major(r, s) for r, s in zip((l0_ref, l1_ref, l2_ref), scr[:3]))
    o0, o1, o2 = (_token_major(r, s) for r, s in zip((o0_ref, o1_ref, o2_ref), scr[3:]))
    m = jnp.maximum(jnp.maximum(l0, l1), l2)
    e0, e1, e2 = jnp.exp(l0 - m), jnp.exp(l1 - m), jnp.exp(l2 - m)
    yb = (e0 * o0 + e1 * o1 + e2 * o2) / (e0 + e1 + e2)
    merged = (_sigmoid(ga_ref[...].astype(F32))
              * jnp.dot(ya_ref[...], wa_ref[...], preferred_element_type=F32)
              + _sigmoid(gb_ref[...].astype(F32))
              * jnp.dot(yb.astype(BF16), wb_ref[...], preferred_element_type=F32))
    x1 = x_ref[...] + g1_ref[0] * jnp.dot(merged.astype(BF16), wo_ref[...],
                                           preferred_element_type=F32)
    h2 = _rms(x1, n2_ref[...]) * (1.0 + sc2_ref[0]) + sh2_ref[0]
    hb = h2.astype(BF16)
    act = (_silu(jnp.dot(hb, wsg_ref[...], preferred_element_type=F32))
           * jnp.dot(hb, wsu_ref[...], preferred_element_type=F32))
    shared = jnp.dot(act.astype(BF16), wsd_ref[...], preferred_element_type=F32)
    x1_ref[...] = x1 + g2_ref[0] * shared
    hp_ref[...] = _pack_halves(h2)
    h_lo = (h2 - hb.astype(F32)).astype(BF16)
    nt = lambda a, b: lax.dot_general(a, b, (((1,), (1,)), ((), ())), preferred_element_type=F32)
    lg_ref[...] = nt(wr_ref[...], hb) + (nt(wr_ref[...], h_lo) + nt(wrl_ref[...], hb))


def _merge(ya, att, ga, gb, x2, gate1, scale2, shift2, gate2, norm2_g, wa, wb, wo, wr_t, wsg, wsu,
           wsd, router_bias, seq, tm):
    t, d = x2.shape
    n_e = wr_t.shape[0]
    wr_hi = wr_t.astype(BF16)
    wr_lo = (wr_t - wr_hi.astype(F32)).astype(BF16)
    n_per = seq // tm
    n_tiles = t // tm
    tile = lambda i: jnp.minimum(i, n_tiles - 1)
    per_b = lambda i: (tile(i) // n_per, 0, 0)
    rows = lambda wdt: pl.BlockSpec((tm, wdt), lambda i: (tile(i), 0))
    full = lambda a: pl.BlockSpec(a.shape, lambda i: (0,) * a.ndim)
    vec = pl.BlockSpec((1, 1, d), per_b)
    (o0, l0), (o1, l1), (o2, l2) = att
    gw = o0.shape[3]
    by_residue = lambda a: pl.BlockSpec((1, a.shape[1], tm // a.shape[1], gw),
                                        lambda i: (tile(i) // n_per, 0, tile(i) % n_per, 0))
    att_in = (o0, o1, o2, l0, l1, l2)
    bias_col = router_bias.reshape(n_e, 1)
    tok = pl.BlockSpec((TOP_K, tm), lambda i: (0, jnp.maximum(i - 1, 0)))
    return pl.pallas_call(
        _merge_kernel,
        out_shape=[jax.ShapeDtypeStruct((t, d), F32),
                   jax.ShapeDtypeStruct((t, d // 2), U32),
                   jax.ShapeDtypeStruct((TOP_K, t), I32), jax.ShapeDtypeStruct((TOP_K, t), F32),
                   jax.ShapeDtypeStruct((TOP_K, t), I32), jax.ShapeDtypeStruct((n_e, LANES), I32)],
        grid=(n_tiles + 1,),
        in_specs=[rows(ya.shape[1])] + [by_residue(a) for a in att_in] + [rows(d)] * 3
        + [vec, vec, vec, vec, pl.BlockSpec((1, d), lambda i: (0, 0))]
        + [full(a) for a in (wa, wb, wo, wr_hi, wr_lo, wsg, wsu, wsd, bias_col)],
        out_specs=[rows(d), rows(d // 2), tok, tok, tok,
                   pl.BlockSpec((n_e, LANES), lambda i: (0, 0))],
        scratch_shapes=[pltpu.VMEM((n_e, 1), F32), pltpu.VMEM((n_e, tm), F32)]
        + [pltpu.VMEM((gw // LANES, tm, LANES), F32)] * 6,
        compiler_params=_params(),
        name="merge_router",
    )(ya, *att_in, ga, gb, x2, gate1, scale2, shift2, gate2,
      norm2_g.reshape(1, d), wa, wb, wo, wr_hi, wr_lo, wsg, wsu, wsd, bias_col)


def _route(logits, live, bias_ref, idx_ref, gate_ref, rank_ref, cnt_ref, carry_ref):
    n_e, tt = logits.shape
    scores = _sigmoid(logits)
    sel = scores + bias_ref[...]
    eio = lax.broadcasted_iota(I32, (n_e, tt), 0)
    picked = jnp.zeros((n_e, tt), F32)
    idxs, vals = [], []
    for _ in range(TOP_K):
        m = jnp.max(sel, axis=0, keepdims=True)
        ik = jnp.min(jnp.where(sel == m, eio, n_e), axis=0, keepdims=True)
        hit = eio == ik
        vals.append(jnp.sum(jnp.where(hit, scores, 0.0), axis=0, keepdims=True))
        sel = jnp.where(hit, -jnp.inf, sel)
        picked = picked + jnp.where(hit, 1.0, 0.0)
        idxs.append(ik)
    denom = vals[0]
    for v in vals[1:]:
        denom = denom + v
    gate_ref[...] = jnp.concatenate([v / denom * ROUTE_SCALE for v in vals], axis=0)
    idx_ref[...] = jnp.concatenate(idxs, axis=0)

    upper = (lax.broadcasted_iota(I32, (tt, tt), 0) <= lax.broadcasted_iota(I32, (tt, tt), 1))
    incl = jnp.dot(picked.astype(BF16), jnp.where(upper, 1.0, 0.0).astype(BF16),
                   preferred_element_type=F32)
    before = incl - picked + carry_ref[...]
    rank_ref[...] = jnp.concatenate(
        [jnp.sum(jnp.where(eio == ik, before, 0.0), axis=0, keepdims=True) for ik in idxs],
        axis=0).astype(I32)
    carry_ref[...] = carry_ref[...] + jnp.sum(picked, axis=1, keepdims=True) * live
    cnt_ref[...] = jnp.broadcast_to(carry_ref[...], cnt_ref.shape).astype(I32)


def _dest_kernel(idx_ref, rank_ref, start_ref, o_ref):
    k, tt = idx_ref.shape
    n_e = start_ref.shape[0]
    eio = lax.broadcasted_iota(I32, (n_e, tt), 0)
    start = start_ref[...]
    rows = [jnp.sum(jnp.where(eio == idx_ref[r:r + 1, :], start, 0), axis=0, keepdims=True)
            for r in range(k)]
    o_ref[...] = jnp.concatenate(rows, axis=0) + rank_ref[...]


def _dest(idx, rank, seg_start, tt):
    k, t = idx.shape
    n_e = seg_start.shape[0]
    tok = pl.BlockSpec((k, tt), lambda i: (0, i))
    return pl.pallas_call(
        _dest_kernel,
        out_shape=jax.ShapeDtypeStruct((k, t), I32),
        grid=(t // tt,),
        in_specs=[tok, tok, pl.BlockSpec((n_e, 1), lambda i: (0, 0))],
        out_specs=tok,
        compiler_params=_params(),
        name="moe_dest",
    )(idx, rank, seg_start.reshape(n_e, 1))


def _sc_mesh():
    return plsc.VectorSubcoreMesh(core_axis_name="core", subcore_axis_name="subcore")


def _sc_scatter_rows(rows, dest, n_out):
    k, t = dest.shape
    w = rows.shape[1]
    mesh = _sc_mesh()
    n_workers = mesh.num_cores * mesh.num_subcores
    win_per_worker = t // (SC_WINDOW * n_workers)
    assert win_per_worker * SC_WINDOW * n_workers == t

    @functools.partial(
        pl.kernel, out_type=jax.ShapeDtypeStruct((n_out, w), rows.dtype), mesh=mesh,
        scratch_types=[pltpu.VMEM((SC_WINDOW, w), rows.dtype)]
        + [pltpu.VMEM((1, SC_WINDOW), I32)] * k + [pltpu.SemaphoreType.DMA],
        name="moe_dispatch_sc")
    def run(rows_hbm, idx_hbm, out_hbm, rows_v, *rest):
        idx_v, sem = rest[:k], rest[k]
        worker = lax.axis_index("subcore") * mesh.num_cores + lax.axis_index("core")

        @pl.loop(0, win_per_worker)
        def _(j):
            t0 = pl.multiple_of((worker * win_per_worker + j) * SC_WINDOW, SC_WINDOW)
            pltpu.sync_copy(rows_hbm.at[pl.ds(t0, SC_WINDOW)], rows_v)
            for r in range(k):
                pltpu.sync_copy(idx_hbm.at[:, pl.ds(r * t + t0, SC_WINDOW)], idx_v[r])
            copies = [pltpu.async_copy(rows_v, out_hbm.at[idx_v[r].at[0]], sem) for r in range(k)]
            for c in copies:
                c.wait()

    return run(rows, dest.reshape(1, k * t))


def _sc_gather_rows(table, dest):
    k, t = dest.shape
    w = table.shape[1]
    mesh = _sc_mesh()
    n_workers = mesh.num_cores * mesh.num_subcores
    win_per_worker = (k * t) // (SC_WINDOW * n_workers)
    assert win_per_worker * SC_WINDOW * n_workers == k * t

    @functools.partial(
        pl.kernel, out_type=jax.ShapeDtypeStruct((k * t, w), table.dtype), mesh=mesh,
        scratch_types=[pltpu.VMEM((SC_WINDOW, w), table.dtype), pltpu.VMEM((1, SC_WINDOW), I32)],
        name="moe_gather_sc")
    def run(table_hbm, idx_hbm, out_hbm, rows_v, idx_v):
        worker = lax.axis_index("subcore") * mesh.num_cores + lax.axis_index("core")

        @pl.loop(0, win_per_worker)
        def _(j):
            p0 = pl.multiple_of((worker * win_per_worker + j) * SC_WINDOW, SC_WINDOW)
            pltpu.sync_copy(idx_hbm.at[:, pl.ds(p0, SC_WINDOW)], idx_v)
            pltpu.sync_copy(table_hbm.at[idx_v.at[0]], rows_v)
            pltpu.sync_copy(rows_v, out_hbm.at[pl.ds(p0, SC_WINDOW)])

    return run(table, dest.reshape(1, k * t))


def _expert_kernel(start_ref, nblk_ref, xs_ref, wg_ref, wu_ref, wd_ref, ys_ref,
                   xbuf, ybuf, wgb, wub, wdb, wbuf_g, wbuf_u, wbuf_d, sem_in, sem_out, sem_w):
    wbuf = (wbuf_g, wbuf_u, wbuf_d)
    e = pl.program_id(0)
    n_e = pl.num_programs(0)
    nb = nblk_ref[e]
    g0 = start_ref[e] // MOE_BLOCK
    n_used = start_ref[n_e - 1] // MOE_BLOCK + nblk_ref[n_e - 1]
    n_in, n_out = xbuf.shape[0], ybuf.shape[0]

    def rows(g):
        return pl.ds(pl.multiple_of(g * MOE_BLOCK, MOE_BLOCK), MOE_BLOCK)

    def in_copy(g):
        slot = lax.rem(g, n_in)
        return pltpu.make_async_copy(xs_ref.at[rows(g), :], xbuf.at[slot], sem_in.at[slot])

    def out_copy(g):
        slot = lax.rem(g, n_out)
        return pltpu.make_async_copy(ybuf.at[slot], ys_ref.at[rows(g), :], sem_out.at[slot])

    look = n_in - EXPERT_GROUP

    @pl.when(e == 0)
    def _():
        for g in range(look):
            @pl.when(g < n_used)
            def _():
                in_copy(g).start(priority=g % N_DMA_QUEUES)

    n_w = wbuf[0].shape[0]

    def weight_copies(ex):
        slot = lax.rem(ex, n_w)
        return [pltpu.make_async_copy(src.at[ex], buf.at[slot], sem_w.at[slot])
                for src, buf in zip((wg_ref, wu_ref, wd_ref), wbuf)]

    @pl.when(e == 0)
    def _():
        for ex in range(min(n_w, wg_ref.shape[0])):
            for c in weight_copies(ex):
                c.start()

    for c in weight_copies(e):
        c.wait()
    w_slot = lax.rem(e, n_w)

    @pl.when(nb > 0)
    def _():
        wgb[...] = wbuf[0][w_slot].astype(BF16)
        wub[...] = wbuf[1][w_slot].astype(BF16)
        wdb[...] = wbuf[2][w_slot].astype(BF16)

    @pl.when(e + n_w < n_e)
    def _():
        for c in weight_copies(e + n_w):
            c.start()

    @pl.when(nb > 0)
    def _():
        def swiglu(word):
            lo, hi = _unpack_halves(word)
            x = jnp.concatenate([lo.astype(BF16), hi.astype(BF16)], axis=1)
            gate = jnp.dot(x, wgb[...], preferred_element_type=F32)
            up = jnp.dot(x, wub[...], preferred_element_type=F32)
            act = (_silu(gate) * up).astype(BF16)
            return jnp.dot(act, wdb[...], preferred_element_type=F32)

        def process(g, m):
            for i in range(m):
                in_copy(g + i).wait()
            for i in range(m):
                @pl.when(g + look + i < n_used)
                def _():
                    in_copy(g + look + i).start(priority=i % N_DMA_QUEUES)
            y_all = swiglu(jnp.concatenate([xbuf[lax.rem(g + i, n_in)] for i in range(m)], axis=0))
            ys = [y_all[i * MOE_BLOCK:(i + 1) * MOE_BLOCK] for i in range(m)]
            for i in range(m):
                @pl.when(g + i >= n_out)
                def _():
                    out_copy(g + i - n_out).wait()

                ybuf[lax.rem(g + i, n_out)] = _pack_halves(ys[i])
                out_copy(g + i).start(priority=(i + 1) % N_DMA_QUEUES)

        def group_body(p, carry):
            process(g0 + p * EXPERT_GROUP, EXPERT_GROUP)
            return carry

        lax.fori_loop(0, nb // EXPERT_GROUP, group_body, 0)
        for m in range(1, EXPERT_GROUP):
            @pl.when(lax.rem(nb, EXPERT_GROUP) == m)
            def _():
                process(g0 + nb - m, m)

    @pl.when(e == n_e - 1)
    def _():
        for i in range(n_out):
            @pl.when(n_used - 1 - i >= 0)
            def _():
                out_copy(n_used - 1 - i).wait()


def _experts(seg_start, seg_blocks, xs, wg, wu, wd):
    n_slots, half = xs.shape
    n_e, d, de = wg.shape
    n_w = EXPERT_WEIGHT_BUFFERS
    return pl.pallas_call(
        _expert_kernel,
        out_shape=jax.ShapeDtypeStruct((n_slots, half), U32),
        grid_spec=pltpu.PrefetchScalarGridSpec(
            num_scalar_prefetch=2,
            grid=(n_e,),
            in_specs=[pl.BlockSpec(memory_space=pl.ANY)] * 4,
            out_specs=pl.BlockSpec(memory_space=pl.ANY),
            scratch_shapes=[pltpu.VMEM((EXPERT_IN_RING, MOE_BLOCK, half), U32),
                            pltpu.VMEM((EXPERT_OUT_RING, MOE_BLOCK, half), U32),
                            pltpu.VMEM((d, de), BF16), pltpu.VMEM((d, de), BF16),
                            pltpu.VMEM((de, d), BF16),
                            pltpu.VMEM((n_w, d, de), F32), pltpu.VMEM((n_w, d, de), F32),
                            pltpu.VMEM((n_w, de, d), F32),
                            pltpu.SemaphoreType.DMA((EXPERT_IN_RING,)),
                            pltpu.SemaphoreType.DMA((EXPERT_OUT_RING,)),
                            pltpu.SemaphoreType.DMA((n_w,))]),
        compiler_params=_params(),
        name="moe_experts",
    )(seg_start, seg_blocks, xs, wg, wu, wd)


def _combine_kernel(yg_ref, gt_ref, x_ref, g2_ref, fg_ref, o_ref):
    k = yg_ref.shape[0]
    gt = gt_ref[...]
    lo, hi = _unpack_halves(yg_ref[0])
    y_lo, y_hi = lo * gt[:, 0:1], hi * gt[:, 0:1]
    for r in range(1, k):
        lo, hi = _unpack_halves(yg_ref[r])
        y_lo, y_hi = y_lo + lo * gt[:, r:r + 1], y_hi + hi * gt[:, r:r + 1]
    y = jnp.concatenate([y_lo, y_hi], axis=1)
    o_ref[...] = _rms(x_ref[...] + g2_ref[0] * y, fg_ref[...])


def _combine_into_kernel(yg_ref, gt_ref, x_ref, g2_ref, fg_ref, prev_ref, o_ref):
    del prev_ref
    _combine_kernel(yg_ref, gt_ref, x_ref, g2_ref, fg_ref, o_ref)


def _combine(yg, tok0, gates_t, x1s, gate2, final_g, seq, tc, out_so_far=None):
    k, n, half = yg.shape
    t, d = x1s.shape
    b0 = tok0 // tc
    args = [yg, gates_t, x1s, gate2, final_g.reshape(1, d)]
    in_specs = [pl.BlockSpec((k, tc, half), lambda i: (0, i, 0)),
                pl.BlockSpec((tc, k), lambda i: (i + b0, 0)),
                pl.BlockSpec((tc, d), lambda i: (i + b0, 0)),
                pl.BlockSpec((1, 1, d), lambda i: (((i + b0) * tc) // seq, 0, 0)),
                pl.BlockSpec((1, d), lambda i: (0, 0))]
    aliases = {}
    kernel = _combine_kernel
    if out_so_far is not None:
        args.append(out_so_far)
        in_specs.append(pl.BlockSpec(memory_space=pl.ANY))
        aliases = {len(args) - 1: 0}
        kernel = _combine_into_kernel
    return pl.pallas_call(
        kernel,
        out_shape=jax.ShapeDtypeStruct((t, d), F32),
        grid=(n // tc,),
        in_specs=in_specs,
        out_specs=pl.BlockSpec((tc, d), lambda i: (i + b0, 0)),
        input_output_aliases=aliases,
        compiler_params=_params(),
        name="moe_combine",
    )(*args)


def _layer(x2, c, bsz, seq, lb_row, ada_w, ada_b, norm1_g, w_in, hg_norm_g, w_branch_a, w_branch_b,
           w_out, norm2_g, w_router, router_bias, w_exp_gate, w_exp_up, w_exp_down, w_sh_gate,
           w_sh_up, w_sh_down, final_g):
    t, d = x2.shape
    n_e = w_router.shape[1]
    mod = _ada(c, ada_w, ada_b).reshape(bsz, 6, 1, d)
    shift1, scale1, gate1, shift2, scale2, gate2 = (mod[:, j] for j in range(6))

    hw = hg_norm_g.shape[0]
    aw = len(ATT_GROUPS) * ATT_HEADS_PER_GROUP * ATT_HEAD_DIM
    flat_segs = [(0, hw, BF16), (hw, hw, F32), (2 * hw, hw, BF16), (3 * hw, hw, BF16),
                 (4 * hw + 3 * aw, d, BF16), (4 * hw + 3 * aw + d, d, BF16)]
    (hq, hf, hi, hg, ga, gb), qkv = _inproj(
        x2, norm1_g, scale1, shift1, w_in.astype(BF16), bsz, seq, flat_segs, 4 * hw,
        tm=IN_PROJ_TILE)

    ya = _hgrn(hq, hf, hi, hg, lb_row, hg_norm_g, bsz, seq, ts=HGRN_TILE)
    att = [_attn_group(*qkv[3 * g:3 * g + 3], g, blocks_per_step=ATT_BLOCKS_PER_STEP)
           for g in range(len(ATT_GROUPS))]

    x1s, hp, idx, gates, rank, cnt = _merge(
        ya, att, ga, gb, x2, gate1, scale2, shift2, gate2, norm2_g, w_branch_a.astype(BF16),
        w_branch_b.astype(BF16), w_out.astype(BF16), w_router.T, w_sh_gate.astype(BF16),
        w_sh_up.astype(BF16), w_sh_down.astype(BF16), router_bias, seq, tm=MERGE_TILE)
    counts = cnt[:, 0]
    padded = (counts + MOE_BLOCK - 1) // MOE_BLOCK * MOE_BLOCK
    seg_start = (jnp.cumsum(padded) - padded).astype(I32)
    n_blocks = -(-(t * TOP_K) // MOE_BLOCK) + n_e
    dest = _dest(idx, rank, seg_start, tt=DEST_TILE)

    xs = _sc_scatter_rows(hp, dest, n_blocks * MOE_BLOCK)
    ys = _experts(seg_start, (padded // MOE_BLOCK).astype(I32), xs, w_exp_gate, w_exp_up,
                  w_exp_down)
    out, n = None, t // COMBINE_PARTS
    for part in range(COMBINE_PARTS):
        yg = _sc_gather_rows(ys, dest[:, part * n:(part + 1) * n]).reshape(TOP_K, n, d // 2)
        out = _combine(yg, part * n, gates.T, x1s, gate2, final_g, seq, tc=COMBINE_TILE,
                       out_so_far=out)
    return out


def kernel(x, c, ada_w, ada_b, norm1_g, w_in, lb_logits, hg_norm_g, w_branch_a, w_branch_b, w_out,
           norm2_g, w_router, router_bias, w_exp_gate, w_exp_up, w_exp_down, w_sh_gate, w_sh_up,
           w_sh_down, final_g):
    bsz, seq, d = x.shape
    depth = ada_w.shape[0]
    assert depth == 1, "the last layer's kernels also apply the final norm"
    lb_table = jnp.cumsum(jax.nn.softmax(lb_logits.astype(F32), axis=0), axis=0)
    out = _layer(x.reshape(bsz * seq, d), c, bsz, seq, lb_table[0], ada_w[0], ada_b[0], norm1_g[0],
                 w_in[0], hg_norm_g[0], w_branch_a[0], w_branch_b[0], w_out[0], norm2_g[0],
                 w_router[0], router_bias[0], w_exp_gate[0], w_exp_up[0], w_exp_down[0],
                 w_sh_gate[0], w_sh_up[0], w_sh_down[0], final_g)
    return out.reshape(bsz, seq, d)
```

```python
import functools

import jax
import jax.numpy as jnp
from jax import lax
from jax.experimental import pallas as pl
from jax.experimental.pallas import tpu as pltpu
from jax.experimental.pallas import tpu_sc as plsc

F32 = jnp.float32
BF16 = jnp.bfloat16
I32 = jnp.int32
U32 = jnp.uint32
HIGHEST = lax.Precision.HIGHEST

HG_HEADS = 4
HG_BLOCK = 16
HG_CHUNK = 32
HG_MILD_DECAY = -80.0
ATT_GROUPS = ((128, 1), (512, 4), (2048, 16))
ATT_HEADS_PER_GROUP = 4
ATT_HEAD_DIM = 64
TOP_K = 8
ROUTE_SCALE = 2.5
MOE_BLOCK = 256
RMS_EPS = 1e-6
N_DMA_QUEUES = 2
SC_WINDOW = 128
COMBINE_PARTS = 8
EXPERT_WEIGHT_BUFFERS = 3
EXPERT_GROUP = 4
EXPERT_IN_RING = 8
EXPERT_OUT_RING = 6

LANES = 128
VMEM_LIMIT_BYTES = 56 * 1024 * 1024

IN_PROJ_TILE = 512
HGRN_TILE = 512
ATT_BLOCKS_PER_STEP = 8
MERGE_TILE = 512
DEST_TILE = 2048
COMBINE_TILE = 512


def _sigmoid(x):
    return 1.0 / (1.0 + jnp.exp(-x))


def _silu(x):
    return x * _sigmoid(x)


def _rms(x, g):
    return x * lax.rsqrt(jnp.mean(x * x, axis=-1, keepdims=True) + RMS_EPS) * g


def _pack_halves(x):
    n = x.shape[1] // 2
    bits = lax.bitcast_convert_type(x.astype(BF16).astype(F32), U32)
    return (bits[:, :n] >> 16) | (bits[:, n:] & jnp.uint32(0xFFFF0000))


def _unpack_halves(word):
    lo = lax.bitcast_convert_type(word << 16, F32)
    hi = lax.bitcast_convert_type(word & jnp.uint32(0xFFFF0000), F32)
    return lo, hi


def _params(n_axes=1):
    return pltpu.CompilerParams(
        dimension_semantics=("arbitrary",) * n_axes, vmem_limit_bytes=VMEM_LIMIT_BYTES)


def _ada_kernel(c_ref, w_ref, b_ref, o_ref):
    sc = _silu(c_ref[...])
    o_ref[...] = jnp.dot(sc, w_ref[...], preferred_element_type=F32, precision=HIGHEST) + b_ref[...]


def _ada(c, w, b):
    bsz, d = c.shape
    n = w.shape[1]
    return pl.pallas_call(
        _ada_kernel,
        out_shape=jax.ShapeDtypeStruct((bsz, n), F32),
        grid=(n // d,),
        in_specs=[pl.BlockSpec((bsz, d), lambda j: (0, 0)),
                  pl.BlockSpec((d, d), lambda j: (0, j)),
                  pl.BlockSpec((1, d), lambda j: (0, j))],
        out_specs=pl.BlockSpec((bsz, d), lambda j: (0, j)),
        compiler_params=_params(),
        name="ada_mod",
    )(c, w, b.reshape(1, n))


def _inproj_kernel(n_flat, flat_ranges, att_c0, x_ref, g_ref, sc_ref, sh_ref, w_ref, *refs):
    flat_refs, att_refs, scr = refs[:n_flat], refs[n_flat:-1], refs[-1]
    tm = x_ref.shape[0]
    h = _rms(x_ref[...], g_ref[...]) * (1.0 + sc_ref[0]) + sh_ref[0]
    hb = h.astype(BF16)
    for (c0, c1), o_ref in zip(flat_ranges, flat_refs):
        o_ref[...] = jnp.dot(hb, w_ref[:, c0:c1], preferred_element_type=F32).astype(o_ref.dtype)
    gw = ATT_HEADS_PER_GROUP * ATT_HEAD_DIM
    n_groups = len(ATT_GROUPS)
    for part in range(3):
        c0 = att_c0 + part * n_groups * gw
        res = jnp.dot(hb, w_ref[:, c0:c0 + n_groups * gw], preferred_element_type=F32)
        if part == 0:
            res = res * (ATT_HEAD_DIM ** -0.5)
        for g, (_, dil) in enumerate(ATT_GROUPS):
            o_ref = att_refs[g * 3 + part]
            sub = res[:, g * gw:(g + 1) * gw]
            if dil == 1:
                o_ref[0, 0] = sub.astype(BF16)
            else:
                for c in range(gw // LANES):
                    scr[c] = sub[:, c * LANES:(c + 1) * LANES]
                for r in range(dil):
                    o_ref[0, r] = jnp.concatenate(
                        [scr[c, pl.ds(r, tm // dil, stride=dil), :] for c in range(gw // LANES)],
                        axis=1).astype(BF16)


def _inproj(x2, g, scale, shift, w_bf16, bsz, seq, flat_segs, att_c0, tm):
    t, d = x2.shape
    gw = ATT_HEADS_PER_GROUP * ATT_HEAD_DIM
    n_per = seq // tm
    per_b = lambda i: (i // n_per, 0, 0)
    att_shapes, att_specs = [], []
    for _, dil in ATT_GROUPS:
        for _ in range(3):
            att_shapes.append(jax.ShapeDtypeStruct((bsz, dil, seq // dil, gw), BF16))
            att_specs.append(pl.BlockSpec((1, dil, tm // dil, gw),
                                          lambda i: (i // n_per, 0, i % n_per, 0)))
    outs = pl.pallas_call(
        functools.partial(_inproj_kernel, len(flat_segs),
                          tuple((c0, c0 + wdt) for c0, wdt, _ in flat_segs), att_c0),
        out_shape=[jax.ShapeDtypeStruct((t, wdt), dt) for _, wdt, dt in flat_segs] + att_shapes,
        grid=(t // tm,),
        in_specs=[pl.BlockSpec((tm, d), lambda i: (i, 0)),
                  pl.BlockSpec((1, d), lambda i: (0, 0)),
                  pl.BlockSpec((1, 1, d), per_b),
                  pl.BlockSpec((1, 1, d), per_b),
                  pl.BlockSpec(w_bf16.shape, lambda i: (0, 0))],
        out_specs=[pl.BlockSpec((tm, wdt), lambda i: (i, 0)) for _, wdt, _ in flat_segs]
        + att_specs,
        scratch_shapes=[pltpu.VMEM((gw // LANES, tm, LANES), F32)],
        compiler_params=_params(),
        name="in_proj",
    )(x2, g.reshape(1, d), scale, shift, w_bf16)
    return outs[:len(flat_segs)], outs[len(flat_segs):]


def _hgrn_kernel(ts, q_ref, f_ref, v_ref, gt_ref, lb_ref, ng_ref, o_ref, st_ref, b_ref):
    dk = q_ref.shape[1] // HG_HEADS
    n_chunks = ts // HG_CHUNK
    n_blk = HG_CHUNK // HG_BLOCK

    @pl.when(pl.program_id(1) == 0)
    def _():
        st_ref[...] = jnp.zeros_like(st_ref)

    row = lax.broadcasted_iota(I32, (LANES, LANES), 0)
    col = lax.broadcasted_iota(I32, (LANES, LANES), 1)
    same_chunk = (row // HG_CHUNK) == (col // HG_CHUNK)
    cum_mat = jnp.where(same_chunk & (col <= row), 1.0, 0.0).astype(BF16)

    def chunk_cumsum(x):
        out = []
        for r0 in range(0, ts, LANES):
            rest = x[r0:r0 + LANES]
            acc = None
            for _ in range(3):
                term = rest.astype(BF16)
                part = jnp.dot(cum_mat, term, preferred_element_type=F32)
                acc = part if acc is None else acc + part
                rest = rest - term.astype(F32)
            out.append(acc)
        return jnp.concatenate(out, axis=0)

    def forget(cs):
        lb = lb_ref[:, cs]
        return lb + (1.0 - lb) * _sigmoid(f_ref[:, cs])

    b_min = None
    for h in range(HG_HEADS):
        cs = slice(h * dk, (h + 1) * dk)
        b = chunk_cumsum(jnp.log(forget(cs)))
        b_ref[:, cs] = b
        m = jnp.min(b)
        b_min = m if b_min is None else jnp.minimum(b_min, m)
    mild = b_min >= HG_MILD_DECAY

    def finish(h, o, st):
        cs = slice(h * dk, (h + 1) * dk)
        st_ref[h] = st
        y = _rms(o, ng_ref[:, cs]) * _silu(gt_ref[:, cs].astype(F32))
        o_ref[:, cs] = y.astype(o_ref.dtype)

    @pl.when(mild)
    def _():
        span = 2 * HG_CHUNK
        causal = (lax.broadcasted_iota(I32, (span, span), 0)
                  >= lax.broadcasted_iota(I32, (span, span), 1))
        nt = lambda x, y: lax.dot_general(x, y, (((1,), (1,)), ((), ())),
                                          preferred_element_type=F32)
        for h in range(HG_HEADS):
            cs = slice(h * dk, (h + 1) * dk)
            v = v_ref[:, cs]
            b = b_ref[:, cs]
            q = q_ref[:, cs].astype(F32)
            k = 1.0 - forget(cs)
            st = st_ref[h]
            o_rows = []
            for r0 in range(0, ts, span):
                sl = slice(r0, r0 + span)
                b_first, b_second = b[r0:r0 + HG_CHUNK], b[r0 + HG_CHUNK:r0 + span]
                end_first = b_first[HG_CHUNK - 1:HG_CHUNK]
                end_second = b_second[HG_CHUNK - 1:HG_CHUNK]
                e = jnp.exp(jnp.concatenate([b_first - end_first, b_second], axis=0))
                qe = (q[sl] * e).astype(BF16)
                ke = k[sl] / e
                a = jnp.where(causal, nt(qe, ke.astype(BF16)), 0.0).astype(BF16)
                st_in = (st * jnp.exp(end_first)).astype(BF16)
                o_rows.append(jnp.dot(a, v[sl], preferred_element_type=F32) + nt(qe, st_in))
                kend = (ke * jnp.exp(end_second)).astype(BF16)
                vt = v[sl].astype(F32).T.astype(BF16)
                st = (st * jnp.exp(end_first + end_second)
                      + jnp.dot(vt, kend, preferred_element_type=F32))
            finish(h, jnp.concatenate(o_rows, axis=0), st)

    @pl.when(jnp.logical_not(mild))
    def _():
        _hgrn_steep(ts, dk, n_chunks, n_blk, q_ref, v_ref, b_ref, st_ref, forget, finish)


def _hgrn_steep(ts, dk, n_chunks, n_blk, q_ref, v_ref, b_ref, st_ref, forget, finish):
    t_in_blk = lax.broadcasted_iota(I32, (ts, dk), 0) % HG_BLOCK

    for h in range(HG_HEADS):
        cs = slice(h * dk, (h + 1) * dk)
        q = q_ref[:, cs].astype(F32)
        v = v_ref[:, cs].astype(F32)
        k = 1.0 - forget(cs)
        b = b_ref[:, cs]

        o = jnp.sum(q * k, axis=-1, keepdims=True) * v
        for d in range(1, HG_BLOCK):
            k_d = pltpu.roll(k, d, axis=0)
            b_d = pltpu.roll(b, d, axis=0)
            v_d = pltpu.roll(v, d, axis=0)
            w = jnp.sum(q * k_d * jnp.exp(jnp.minimum(b - b_d, 0.0)), axis=-1, keepdims=True)
            o = o + jnp.where(t_in_blk >= d, w * v_d, 0.0)

        st = st_ref[h]
        o_rows = []
        for c in range(n_chunks):
            r0 = c * HG_CHUNK
            bc = b[r0:r0 + HG_CHUNK]
            qc = q[r0:r0 + HG_CHUNK]
            kc = k[r0:r0 + HG_CHUNK]
            vc = v[r0:r0 + HG_CHUNK].astype(BF16)
            st_b = st.astype(BF16)
            for i in range(n_blk):
                i0 = i * HG_BLOCK
                if i == 0:
                    qt = qc[:HG_BLOCK] * jnp.exp(bc[:HG_BLOCK])
                    qs = qt
                else:
                    ref_row = bc[i0 - 1:i0]
                    qt = qc[i0:i0 + HG_BLOCK] * jnp.exp(bc[i0:i0 + HG_BLOCK] - ref_row)
                    qs = qt * jnp.exp(ref_row)
                oi = lax.dot_general(qs.astype(BF16), st_b, (((1,), (1,)), ((), ())),
                                     preferred_element_type=F32)
                if i > 0:
                    kh = kc[:i0] * jnp.exp(ref_row - bc[:i0])
                    a = lax.dot_general(qt.astype(BF16), kh.astype(BF16), (((1,), (1,)), ((), ())),
                                        preferred_element_type=F32)
                    oi = oi + jnp.dot(a.astype(BF16), vc[:i0], preferred_element_type=F32)
                o_rows.append(oi)
            b_end = bc[HG_CHUNK - 1:HG_CHUNK]
            kend = kc * jnp.exp(b_end - bc)
            vt = v[r0:r0 + HG_CHUNK].T.astype(BF16)
            st = st * jnp.exp(b_end) + jnp.dot(vt, kend.astype(BF16), preferred_element_type=F32)
        finish(h, o + jnp.concatenate(o_rows, axis=0), st)


def _hgrn(hq, hf, hi, hg, lb, ng, bsz, seq, ts):
    t, w = hq.shape
    dk = w // HG_HEADS
    n_s = seq // ts
    tile = lambda b, s: (b * n_s + s, 0)
    return pl.pallas_call(
        functools.partial(_hgrn_kernel, ts),
        out_shape=jax.ShapeDtypeStruct((t, w), BF16),
        grid=(bsz, n_s),
        in_specs=[pl.BlockSpec((ts, w), tile)] * 4
        + [pl.BlockSpec((1, w), lambda b, s: (0, 0))] * 2,
        out_specs=pl.BlockSpec((ts, w), tile),
        scratch_shapes=[pltpu.VMEM((HG_HEADS, dk, dk), F32), pltpu.VMEM((ts, w), F32)],
        compiler_params=_params(2),
        name="hgrn2",
    )(hq, hf, hi, hg, lb.reshape(1, w), ng.reshape(1, w))


def _attn_kernel(nk, nq, nr, q_ref, kp_ref, kc_ref, vp_ref, vc_ref, o_ref, lse_ref):
    n = pl.program_id(2)
    e = ATT_HEAD_DIM
    i = lax.broadcasted_iota(I32, (nk, 2 * nk), 0)
    j = lax.broadcasted_iota(I32, (nk, 2 * nk), 1)
    band = (j >= i) & (j <= i + nk)
    first_head = lax.broadcasted_iota(I32, (nk, LANES), 1) < e
    zero = jnp.zeros((), q_ref.dtype)
    for r in range(nr):
        kk = jnp.concatenate([kp_ref[0, r], kc_ref[0, r]], axis=0)
        vv = jnp.concatenate([vp_ref[0, r], vc_ref[0, r]], axis=0)
        for b in range(nq):
            valid = band & ((j >= nk) | (n * nq + b > 0))
            rows = slice(b * nk, (b + 1) * nk)
            for c in range(0, ATT_HEADS_PER_GROUP * e, LANES):
                q = q_ref[0, r, rows, c:c + LANES]
                kb = kk[b * nk:(b + 2) * nk, c:c + LANES]
                vb = vv[b * nk:(b + 2) * nk, c:c + LANES]
                outs, lses = [], []
                for keep in (first_head, jnp.logical_not(first_head)):
                    s = lax.dot_general(jnp.where(keep, q, zero), kb, (((1,), (1,)), ((), ())),
                                        preferred_element_type=F32)
                    s = jnp.where(valid, s, -jnp.inf)
                    m = jnp.max(s, axis=-1, keepdims=True)
                    p = jnp.exp(s - m)
                    l = jnp.sum(p, axis=-1, keepdims=True)
                    outs.append(jnp.dot(p.astype(BF16), vb, preferred_element_type=F32) / l)
                    lses.append(m + jnp.log(l))
                o_ref[0, r, rows, c:c + LANES] = jnp.where(first_head, outs[0], outs[1])
                lse_ref[0, r, rows, c:c + LANES] = jnp.where(first_head, lses[0], lses[1])


def _attn_group(q, k, v, g, blocks_per_step):
    window, dil = ATT_GROUPS[g]
    nk = window // dil
    bsz, _, ln, gw = q.shape
    nq = min(blocks_per_step, ln // nk)
    nr = min(blocks_per_step // nq, dil)
    assert ln % (nk * nq) == 0 and dil % nr == 0 and 2 * ATT_HEAD_DIM == LANES
    cur = pl.BlockSpec((1, nr, nq * nk, gw), lambda b, r, n: (b, r, n, 0))
    prev = pl.BlockSpec((1, nr, nk, gw), lambda b, r, n: (b, r, jnp.maximum(n * nq - 1, 0), 0))
    return pl.pallas_call(
        functools.partial(_attn_kernel, nk, nq, nr),
        out_shape=[jax.ShapeDtypeStruct(q.shape, F32)] * 2,
        grid=(bsz, dil // nr, ln // (nk * nq)),
        in_specs=[cur, prev, cur, prev, cur],
        out_specs=[cur, cur],
        compiler_params=_params(3),
        name=f"dilated_attn_g{g}",
    )(q, k, k, v, v)


def _token_major(ref, scr):
    dil, rows = ref.shape[1], ref.shape[2]
    if dil == 1:
        return ref[0, 0]
    n_col = scr.shape[0]
    for r in range(dil):
        for c in range(n_col):
            scr[c, pl.ds(r, rows, stride=dil), :] = ref[0, r, :, c * LANES:(c + 1) * LANES]
    return jnp.concatenate([scr[c] for c in range(n_col)], axis=1)


def _merge_kernel(ya_ref, o0_ref, o1_ref, o2_ref, l0_ref, l1_ref, l2_ref, ga_ref, gb_ref, x_ref,
                  g1_ref, sc2_ref, sh2_ref, g2_ref, n2_ref, wa_ref, wb_ref, wo_ref, wr_ref, wrl_ref,
                  wsg_ref, wsu_ref, wsd_ref, bias_ref, x1_ref, hp_ref, idx_ref, gate_ref, rank_ref,
                  cnt_ref, carry_ref, lg_ref, *scr):
    step = pl.program_id(0)

    @pl.when(step == 0)
    def _():
        carry_ref[...] = jnp.zeros_like(carry_ref)
        lg_ref[...] = jnp.zeros_like(lg_ref)

    _route(lg_ref[...], jnp.where(step > 0, 1.0, 0.0), bias_ref, idx_ref, gate_ref, rank_ref,
           cnt_ref, carry_ref)

    l0, l1, l2 = (_token_major(r, s) for r, s in zip((l0_ref, l1_ref, l2_ref), scr[:3]))
    o0, o1, o2 = (_token_major(r, s) for r, s in zip((o0_ref, o1_ref, o2_ref), scr[3:]))
    m = jnp.maximum(jnp.maximum(l0, l1), l2)
    e0, e1, e2 = jnp.exp(l0 - m), jnp.exp(l1 - m), jnp.exp(l2 - m)
    yb = (e0 * o0 + e1 * o1 + e2 * o2) / (e0 + e1 + e2)
    merged = (_sigmoid(ga_ref[...].astype(F32))
              * jnp.dot(ya_ref[...], wa_ref[...], preferred_element_type=F32)
              + _sigmoid(gb_ref[...].astype(F32))
              * jnp.dot(yb.astype(BF16), wb_ref[...], preferred_element_type=F32))
    x1 = x_ref[...] + g1_ref[0] * jnp.dot(merged.astype(BF16), wo_ref[...],
                                           preferred_element_type=F32)
    h2 = _rms(x1, n2_ref[...]) * (1.0 + sc2_ref[0]) + sh2_ref[0]
    hb = h2.astype(BF16)
    act = (_silu(jnp.dot(hb, wsg_ref[...], preferred_element_type=F32))
           * jnp.dot(hb, wsu_ref[...], preferred_element_type=F32))
    shared = jnp.dot(act.astype(BF16), wsd_ref[...], preferred_element_type=F32)
    x1_ref[...] = x1 + g2_ref[0] * shared
    hp_ref[...] = _pack_halves(h2)
    h_lo = (h2 - hb.astype(F32)).astype(BF16)
    nt = lambda a, b: lax.dot_general(a, b, (((1,), (1,)), ((), ())), preferred_element_type=F32)
    lg_ref[...] = nt(wr_ref[...], hb) + (nt(wr_ref[...], h_lo) + nt(wrl_ref[...], hb))


def _merge(ya, att, ga, gb, x2, gate1, scale2, shift2, gate2, norm2_g, wa, wb, wo, wr_t, wsg, wsu,
           wsd, router_bias, seq, tm):
    t, d = x2.shape
    n_e = wr_t.shape[0]
    wr_hi = wr_t.astype(BF16)
    wr_lo = (wr_t - wr_hi.astype(F32)).astype(BF16)
    n_per = seq // tm
    n_tiles = t // tm
    tile = lambda i: jnp.minimum(i, n_tiles - 1)
    per_b = lambda i: (tile(i) // n_per, 0, 0)
    rows = lambda wdt: pl.BlockSpec((tm, wdt), lambda i: (tile(i), 0))
    full = lambda a: pl.BlockSpec(a.shape, lambda i: (0,) * a.ndim)
    vec = pl.BlockSpec((1, 1, d), per_b)
    (o0, l0), (o1, l1), (o2, l2) = att
    gw = o0.shape[3]
    by_residue = lambda a: pl.BlockSpec((1, a.shape[1], tm // a.shape[1], gw),
                                        lambda i: (tile(i) // n_per, 0, tile(i) % n_per, 0))
    att_in = (o0, o1, o2, l0, l1, l2)
    bias_col = router_bias.reshape(n_e, 1)
    tok = pl.BlockSpec((TOP_K, tm), lambda i: (0, jnp.maximum(i - 1, 0)))
    return pl.pallas_call(
        _merge_kernel,
        out_shape=[jax.ShapeDtypeStruct((t, d), F32),
                   jax.ShapeDtypeStruct((t, d // 2), U32),
                   jax.ShapeDtypeStruct((TOP_K, t), I32), jax.ShapeDtypeStruct((TOP_K, t), F32),
                   jax.ShapeDtypeStruct((TOP_K, t), I32), jax.ShapeDtypeStruct((n_e, LANES), I32)],
        grid=(n_tiles + 1,),
        in_specs=[rows(ya.shape[1])] + [by_residue(a) for a in att_in] + [rows(d)] * 3
        + [vec, vec, vec, vec, pl.BlockSpec((1, d), lambda i: (0, 0))]
        + [full(a) for a in (wa, wb, wo, wr_hi, wr_lo, wsg, wsu, wsd, bias_col)],
        out_specs=[rows(d), rows(d // 2), tok, tok, tok,
                   pl.BlockSpec((n_e, LANES), lambda i: (0, 0))],
        scratch_shapes=[pltpu.VMEM((n_e, 1), F32), pltpu.VMEM((n_e, tm), F32)]
        + [pltpu.VMEM((gw // LANES, tm, LANES), F32)] * 6,
        compiler_params=_params(),
        name="merge_router",
    )(ya, *att_in, ga, gb, x2, gate1, scale2, shift2, gate2,
      norm2_g.reshape(1, d), wa, wb, wo, wr_hi, wr_lo, wsg, wsu, wsd, bias_col)


def _route(logits, live, bias_ref, idx_ref, gate_ref, rank_ref, cnt_ref, carry_ref):
    n_e, tt = logits.shape
    scores = _sigmoid(logits)
    sel = scores + bias_ref[...]
    eio = lax.broadcasted_iota(I32, (n_e, tt), 0)
    picked = jnp.zeros((n_e, tt), F32)
    idxs, vals = [], []
    for _ in range(TOP_K):
        m = jnp.max(sel, axis=0, keepdims=True)
        ik = jnp.min(jnp.where(sel == m, eio, n_e), axis=0, keepdims=True)
        hit = eio == ik
        vals.append(jnp.sum(jnp.where(hit, scores, 0.0), axis=0, keepdims=True))
        sel = jnp.where(hit, -jnp.inf, sel)
        picked = picked + jnp.where(hit, 1.0, 0.0)
        idxs.append(ik)
    denom = vals[0]
    for v in vals[1:]:
        denom = denom + v
    gate_ref[...] = jnp.concatenate([v / denom * ROUTE_SCALE for v in vals], axis=0)
    idx_ref[...] = jnp.concatenate(idxs, axis=0)

    upper = (lax.broadcasted_iota(I32, (tt, tt), 0) <= lax.broadcasted_iota(I32, (tt, tt), 1))
    incl = jnp.dot(picked.astype(BF16), jnp.where(upper, 1.0, 0.0).astype(BF16),
                   preferred_element_type=F32)
    before = incl - picked + carry_ref[...]
    rank_ref[...] = jnp.concatenate(
        [jnp.sum(jnp.where(eio == ik, before, 0.0), axis=0, keepdims=True) for ik in idxs],
        axis=0).astype(I32)
    carry_ref[...] = carry_ref[...] + jnp.sum(picked, axis=1, keepdims=True) * live
    cnt_ref[...] = jnp.broadcast_to(carry_ref[...], cnt_ref.shape).astype(I32)


def _dest_kernel(idx_ref, rank_ref, start_ref, o_ref):
    k, tt = idx_ref.shape
    n_e = start_ref.shape[0]
    eio = lax.broadcasted_iota(I32, (n_e, tt), 0)
    start = start_ref[...]
    rows = [jnp.sum(jnp.where(eio == idx_ref[r:r + 1, :], start, 0), axis=0, keepdims=True)
            for r in range(k)]
    o_ref[...] = jnp.concatenate(rows, axis=0) + rank_ref[...]


def _dest(idx, rank, seg_start, tt):
    k, t = idx.shape
    n_e = seg_start.shape[0]
    tok = pl.BlockSpec((k, tt), lambda i: (0, i))
    return pl.pallas_call(
        _dest_kernel,
        out_shape=jax.ShapeDtypeStruct((k, t), I32),
        grid=(t // tt,),
        in_specs=[tok, tok, pl.BlockSpec((n_e, 1), lambda i: (0, 0))],
        out_specs=tok,
        compiler_params=_params(),
        name="moe_dest",
    )(idx, rank, seg_start.reshape(n_e, 1))


def _sc_mesh():
    return plsc.VectorSubcoreMesh(core_axis_name="core", subcore_axis_name="subcore")


def _sc_scatter_rows(rows, dest, n_out):
    k, t = dest.shape
    w = rows.shape[1]
    mesh = _sc_mesh()
    n_workers = mesh.num_cores * mesh.num_subcores
    win_per_worker = t // (SC_WINDOW * n_workers)
    assert win_per_worker * SC_WINDOW * n_workers == t

    @functools.partial(
        pl.kernel, out_type=jax.ShapeDtypeStruct((n_out, w), rows.dtype), mesh=mesh,
        scratch_types=[pltpu.VMEM((SC_WINDOW, w), rows.dtype)]
        + [pltpu.VMEM((1, SC_WINDOW), I32)] * k + [pltpu.SemaphoreType.DMA],
        name="moe_dispatch_sc")
    def run(rows_hbm, idx_hbm, out_hbm, rows_v, *rest):
        idx_v, sem = rest[:k], rest[k]
        worker = lax.axis_index("subcore") * mesh.num_cores + lax.axis_index("core")

        @pl.loop(0, win_per_worker)
        def _(j):
            t0 = pl.multiple_of((worker * win_per_worker + j) * SC_WINDOW, SC_WINDOW)
            pltpu.sync_copy(rows_hbm.at[pl.ds(t0, SC_WINDOW)], rows_v)
            for r in range(k):
                pltpu.sync_copy(idx_hbm.at[:, pl.ds(r * t + t0, SC_WINDOW)], idx_v[r])
            copies = [pltpu.async_copy(rows_v, out_hbm.at[idx_v[r].at[0]], sem) for r in range(k)]
            for c in copies:
                c.wait()

    return run(rows, dest.reshape(1, k * t))


def _sc_gather_rows(table, dest):
    k, t = dest.shape
    w = table.shape[1]
    mesh = _sc_mesh()
    n_workers = mesh.num_cores * mesh.num_subcores
    win_per_worker = (k * t) // (SC_WINDOW * n_workers)
    assert win_per_worker * SC_WINDOW * n_workers == k * t

    @functools.partial(
        pl.kernel, out_type=jax.ShapeDtypeStruct((k * t, w), table.dtype), mesh=mesh,
        scratch_types=[pltpu.VMEM((SC_WINDOW, w), table.dtype), pltpu.VMEM((1, SC_WINDOW), I32)],
        name="moe_gather_sc")
    def run(table_hbm, idx_hbm, out_hbm, rows_v, idx_v):
        worker = lax.axis_index("subcore") * mesh.num_cores + lax.axis_index("core")

        @pl.loop(0, win_per_worker)
        def _(j):
            p0 = pl.multiple_of((worker * win_per_worker + j) * SC_WINDOW, SC_WINDOW)
            pltpu.sync_copy(idx_hbm.at[:, pl.ds(p0, SC_WINDOW)], idx_v)
            pltpu.sync_copy(table_hbm.at[idx_v.at[0]], rows_v)
            pltpu.sync_copy(rows_v, out_hbm.at[pl.ds(p0, SC_WINDOW)])

    return run(table, dest.reshape(1, k * t))


def _expert_kernel(start_ref, nblk_ref, xs_ref, wg_ref, wu_ref, wd_ref, ys_ref,
                   xbuf, ybuf, wgb, wub, wdb, wbuf_g, wbuf_u, wbuf_d, sem_in, sem_out, sem_w):
    wbuf = (wbuf_g, wbuf_u, wbuf_d)
    e = pl.program_id(0)
    n_e = pl.num_programs(0)
    nb = nblk_ref[e]
    g0 = start_ref[e] // MOE_BLOCK
    n_used = start_ref[n_e - 1] // MOE_BLOCK + nblk_ref[n_e - 1]
    n_in, n_out = xbuf.shape[0], ybuf.shape[0]

    def rows(g):
        return pl.ds(pl.multiple_of(g * MOE_BLOCK, MOE_BLOCK), MOE_BLOCK)

    def in_copy(g):
        slot = lax.rem(g, n_in)
        return pltpu.make_async_copy(xs_ref.at[rows(g), :], xbuf.at[slot], sem_in.at[slot])

    def out_copy(g):
        slot = lax.rem(g, n_out)
        return pltpu.make_async_copy(ybuf.at[slot], ys_ref.at[rows(g), :], sem_out.at[slot])

    look = n_in - EXPERT_GROUP

    @pl.when(e == 0)
    def _():
        for g in range(look):
            @pl.when(g < n_used)
            def _():
                in_copy(g).start(priority=g % N_DMA_QUEUES)

    n_w = wbuf[0].shape[0]

    def weight_copies(ex):
        slot = lax.rem(ex, n_w)
        return [pltpu.make_async_copy(src.at[ex], buf.at[slot], sem_w.at[slot])
                for src, buf in zip((wg_ref, wu_ref, wd_ref), wbuf)]

    @pl.when(e == 0)
    def _():
        for ex in range(min(n_w, wg_ref.shape[0])):
            for c in weight_copies(ex):
                c.start()

    for c in weight_copies(e):
        c.wait()
    w_slot = lax.rem(e, n_w)

    @pl.when(nb > 0)
    def _():
        wgb[...] = wbuf[0][w_slot].astype(BF16)
        wub[...] = wbuf[1][w_slot].astype(BF16)
        wdb[...] = wbuf[2][w_slot].astype(BF16)

    @pl.when(e + n_w < n_e)
    def _():
        for c in weight_copies(e + n_w):
            c.start()

    @pl.when(nb > 0)
    def _():
        def swiglu(word):
            lo, hi = _unpack_halves(word)
            x = jnp.concatenate([lo.astype(BF16), hi.astype(BF16)], axis=1)
            gate = jnp.dot(x, wgb[...], preferred_element_type=F32)
            up = jnp.dot(x, wub[...], preferred_element_type=F32)
            act = (_silu(gate) * up).astype(BF16)
            return jnp.dot(act, wdb[...], preferred_element_type=F32)

        def process(g, m):
            for i in range(m):
                in_copy(g + i).wait()
            for i in range(m):
                @pl.when(g + look + i < n_used)
                def _():
                    in_copy(g + look + i).start(priority=i % N_DMA_QUEUES)
            y_all = swiglu(jnp.concatenate([xbuf[lax.rem(g + i, n_in)] for i in range(m)], axis=0))
            ys = [y_all[i * MOE_BLOCK:(i + 1) * MOE_BLOCK] for i in range(m)]
            for i in range(m):
                @pl.when(g + i >= n_out)
                def _():
                    out_copy(g + i - n_out).wait()

                ybuf[lax.rem(g + i, n_out)] = _pack_halves(ys[i])
                out_copy(g + i).start(priority=(i + 1) % N_DMA_QUEUES)

        def group_body(p, carry):
            process(g0 + p * EXPERT_GROUP, EXPERT_GROUP)
            return carry

        lax.fori_loop(0, nb // EXPERT_GROUP, group_body, 0)
        for m in range(1, EXPERT_GROUP):
            @pl.when(lax.rem(nb, EXPERT_GROUP) == m)
            def _():
                process(g0 + nb - m, m)

    @pl.when(e == n_e - 1)
    def _():
        for i in range(n_out):
            @pl.when(n_used - 1 - i >= 0)
            def _():
                out_copy(n_used - 1 - i).wait()


def _experts(seg_start, seg_blocks, xs, wg, wu, wd):
    n_slots, half = xs.shape
    n_e, d, de = wg.shape
    n_w = EXPERT_WEIGHT_BUFFERS
    return pl.pallas_call(
        _expert_kernel,
        out_shape=jax.ShapeDtypeStruct((n_slots, half), U32),
        grid_spec=pltpu.PrefetchScalarGridSpec(
            num_scalar_prefetch=2,
            grid=(n_e,),
            in_specs=[pl.BlockSpec(memory_space=pl.ANY)] * 4,
            out_specs=pl.BlockSpec(memory_space=pl.ANY),
            scratch_shapes=[pltpu.VMEM((EXPERT_IN_RING, MOE_BLOCK, half), U32),
                            pltpu.VMEM((EXPERT_OUT_RING, MOE_BLOCK, half), U32),
                            pltpu.VMEM((d, de), BF16), pltpu.VMEM((d, de), BF16),
                            pltpu.VMEM((de, d), BF16),
                            pltpu.VMEM((n_w, d, de), F32), pltpu.VMEM((n_w, d, de), F32),
                            pltpu.VMEM((n_w, de, d), F32),
                            pltpu.SemaphoreType.DMA((EXPERT_IN_RING,)),
                            pltpu.SemaphoreType.DMA((EXPERT_OUT_RING,)),
                            pltpu.SemaphoreType.DMA((n_w,))]),
        compiler_params=_params(),
        name="moe_experts",
    )(seg_start, seg_blocks, xs, wg, wu, wd)


def _combine_kernel(yg_ref, gt_ref, x_ref, g2_ref, fg_ref, o_ref):
    k = yg_ref.shape[0]
    gt = gt_ref[...]
    lo, hi = _unpack_halves(yg_ref[0])
    y_lo, y_hi = lo * gt[:, 0:1], hi * gt[:, 0:1]
    for r in range(1, k):
        lo, hi = _unpack_halves(yg_ref[r])
        y_lo, y_hi = y_lo + lo * gt[:, r:r + 1], y_hi + hi * gt[:, r:r + 1]
    y = jnp.concatenate([y_lo, y_hi], axis=1)
    o_ref[...] = _rms(x_ref[...] + g2_ref[0] * y, fg_ref[...])


def _combine_into_kernel(yg_ref, gt_ref, x_ref, g2_ref, fg_ref, prev_ref, o_ref):
    del prev_ref
    _combine_kernel(yg_ref, gt_ref, x_ref, g2_ref, fg_ref, o_ref)


def _combine(yg, tok0, gates_t, x1s, gate2, final_g, seq, tc, out_so_far=None):
    k, n, half = yg.shape
    t, d = x1s.shape
    b0 = tok0 // tc
    args = [yg, gates_t, x1s, gate2, final_g.reshape(1, d)]
    in_specs = [pl.BlockSpec((k, tc, half), lambda i: (0, i, 0)),
                pl.BlockSpec((tc, k), lambda i: (i + b0, 0)),
                pl.BlockSpec((tc, d), lambda i: (i + b0, 0)),
                pl.BlockSpec((1, 1, d), lambda i: (((i + b0) * tc) // seq, 0, 0)),
                pl.BlockSpec((1, d), lambda i: (0, 0))]
    aliases = {}
    kernel = _combine_kernel
    if out_so_far is not None:
        args.append(out_so_far)
        in_specs.append(pl.BlockSpec(memory_space=pl.ANY))
        aliases = {len(args) - 1: 0}
        kernel = _combine_into_kernel
    return pl.pallas_call(
        kernel,
        out_shape=jax.ShapeDtypeStruct((t, d), F32),
        grid=(n // tc,),
        in_specs=in_specs,
        out_specs=pl.BlockSpec((tc, d), lambda i: (i + b0, 0)),
        input_output_aliases=aliases,
        compiler_params=_params(),
        name="moe_combine",
    )(*args)


def _layer(x2, c, bsz, seq, lb_row, ada_w, ada_b, norm1_g, w_in, hg_norm_g, w_branch_a, w_branch_b,
           w_out, norm2_g, w_router, router_bias, w_exp_gate, w_exp_up, w_exp_down, w_sh_gate,
           w_sh_up, w_sh_down, final_g):
    t, d = x2.shape
    n_e = w_router.shape[1]
    mod = _ada(c, ada_w, ada_b).reshape(bsz, 6, 1, d)
    shift1, scale1, gate1, shift2, scale2, gate2 = (mod[:, j] for j in range(6))

    hw = hg_norm_g.shape[0]
    aw = len(ATT_GROUPS) * ATT_HEADS_PER_GROUP * ATT_HEAD_DIM
    flat_segs = [(0, hw, BF16), (hw, hw, F32), (2 * hw, hw, BF16), (3 * hw, hw, BF16),
                 (4 * hw + 3 * aw, d, BF16), (4 * hw + 3 * aw + d, d, BF16)]
    (hq, hf, hi, hg, ga, gb), qkv = _inproj(
        x2, norm1_g, scale1, shift1, w_in.astype(BF16), bsz, seq, flat_segs, 4 * hw,
        tm=IN_PROJ_TILE)

    ya = _hgrn(hq, hf, hi, hg, lb_row, hg_norm_g, bsz, seq, ts=HGRN_TILE)
    att = [_attn_group(*qkv[3 * g:3 * g + 3], g, blocks_per_step=ATT_BLOCKS_PER_STEP)
           for g in range(len(ATT_GROUPS))]

    x1s, hp, idx, gates, rank, cnt = _merge(
        ya, att, ga, gb, x2, gate1, scale2, shift2, gate2, norm2_g, w_branch_a.astype(BF16),
        w_branch_b.astype(BF16), w_out.astype(BF16), w_router.T, w_sh_gate.astype(BF16),
        w_sh_up.astype(BF16), w_sh_down.astype(BF16), router_bias, seq, tm=MERGE_TILE)
    counts = cnt[:, 0]
    padded = (counts + MOE_BLOCK - 1) // MOE_BLOCK * MOE_BLOCK
    seg_start = (jnp.cumsum(padded) - padded).astype(I32)
    n_blocks = -(-(t * TOP_K) // MOE_BLOCK) + n_e
    dest = _dest(idx, rank, seg_start, tt=DEST_TILE)

    xs = _sc_scatter_rows(hp, dest, n_blocks * MOE_BLOCK)
    ys = _experts(seg_start, (padded // MOE_BLOCK).astype(I32), xs, w_exp_gate, w_exp_up,
                  w_exp_down)
    out, n = None, t // COMBINE_PARTS
    for part in range(COMBINE_PARTS):
        yg = _sc_gather_rows(ys, dest[:, part * n:(part + 1) * n]).reshape(TOP_K, n, d // 2)
        out = _combine(yg, part * n, gates.T, x1s, gate2, final_g, seq, tc=COMBINE_TILE,
                       out_so_far=out)
    return out


def kernel(x, c, ada_w, ada_b, norm1_g, w_in, lb_logits, hg_norm_g, w_branch_a, w_branch_b, w_out,
           norm2_g, w_router, router_bias, w_exp_gate, w_exp_up, w_exp_down, w_sh_gate, w_sh_up,
           w_sh_down, final_g):
    bsz, seq, d = x.shape
    depth = ada_w.shape[0]
    assert depth == 1, "the last layer's kernels also apply the final norm"
    lb_table = jnp.cumsum(jax.nn.softmax(lb_logits.astype(F32), axis=0), axis=0)
    out = _layer(x.reshape(bsz * seq, d), c, bsz, seq, lb_table[0], ada_w[0], ada_b[0], norm1_g[0],
                 w_in[0], hg_norm_g[0], w_branch_a[0], w_branch_b[0], w_out[0], norm2_g[0],
                 w_router[0], router_bias[0], w_exp_gate[0], w_exp_up[0], w_exp_down[0],
                 w_sh_gate[0], w_sh_up[0], w_sh_down[0], final_g)
    return out.reshape(bsz, seq, d)
```

```python
import functools

import jax
import jax.numpy as jnp
from jax import lax
from jax.experimental import pallas as pl
from jax.experimental.pallas import tpu as pltpu
from jax.experimental.pallas import tpu_sc as plsc

F32 = jnp.float32
BF16 = jnp.bfloat16
I32 = jnp.int32
U32 = jnp.uint32
HIGHEST = lax.Precision.HIGHEST

HG_HEADS = 4
HG_BLOCK = 16
HG_CHUNK = 32
HG_MILD_DECAY = -80.0
ATT_GROUPS = ((128, 1), (512, 4), (2048, 16))
ATT_HEADS_PER_GROUP = 4
ATT_HEAD_DIM = 64
TOP_K = 8
ROUTE_SCALE = 2.5
MOE_BLOCK = 256
RMS_EPS = 1e-6
N_DMA_QUEUES = 2
SC_WINDOW = 128
COMBINE_PARTS = 8
EXPERT_WEIGHT_BUFFERS = 3
EXPERT_GROUP = 4
EXPERT_IN_RING = 8
EXPERT_OUT_RING = 6

LANES = 128
VMEM_LIMIT_BYTES = 56 * 1024 * 1024

IN_PROJ_TILE = 512
HGRN_TILE = 1024
ATT_BLOCKS_PER_STEP = 16
MERGE_TILE = 512
DEST_TILE = 2048
COMBINE_TILE = 512


def _sigmoid(x):
    return 1.0 / (1.0 + jnp.exp(-x))


def _silu(x):
    return x * _sigmoid(x)


def _rms(x, g):
    return x * lax.rsqrt(jnp.mean(x * x, axis=-1, keepdims=True) + RMS_EPS) * g


def _pack_halves(x):
    n = x.shape[1] // 2
    bits = lax.bitcast_convert_type(x.astype(BF16).astype(F32), U32)
    return (bits[:, :n] >> 16) | (bits[:, n:] & jnp.uint32(0xFFFF0000))


def _unpack_halves(word):
    lo = lax.bitcast_convert_type(word << 16, F32)
    hi = lax.bitcast_convert_type(word & jnp.uint32(0xFFFF0000), F32)
    return lo, hi


def _params(n_axes=1):
    return pltpu.CompilerParams(
        dimension_semantics=("arbitrary",) * n_axes, vmem_limit_bytes=VMEM_LIMIT_BYTES)


def _ada_kernel(c_ref, w_ref, b_ref, o_ref):
    sc = _silu(c_ref[...])
    o_ref[...] = jnp.dot(sc, w_ref[...], preferred_element_type=F32, precision=HIGHEST) + b_ref[...]


def _ada(c, w, b):
    bsz, d = c.shape
    n = w.shape[1]
    return pl.pallas_call(
        _ada_kernel,
        out_shape=jax.ShapeDtypeStruct((bsz, n), F32),
        grid=(n // d,),
        in_specs=[pl.BlockSpec((bsz, d), lambda j: (0, 0)),
                  pl.BlockSpec((d, d), lambda j: (0, j)),
                  pl.BlockSpec((1, d), lambda j: (0, j))],
        out_specs=pl.BlockSpec((bsz, d), lambda j: (0, j)),
        compiler_params=_params(),
        name="ada_mod",
    )(c, w, b.reshape(1, n))


def _inproj_kernel(n_flat, flat_ranges, att_c0, x_ref, g_ref, sc_ref, sh_ref, w_ref, *refs):
    flat_refs, att_refs, scr = refs[:n_flat], refs[n_flat:-1], refs[-1]
    tm = x_ref.shape[0]
    h = _rms(x_ref[...], g_ref[...]) * (1.0 + sc_ref[0]) + sh_ref[0]
    hb = h.astype(BF16)
    for (c0, c1), o_ref in zip(flat_ranges, flat_refs):
        o_ref[...] = jnp.dot(hb, w_ref[:, c0:c1], preferred_element_type=F32).astype(o_ref.dtype)
    gw = ATT_HEADS_PER_GROUP * ATT_HEAD_DIM
    n_groups = len(ATT_GROUPS)
    for part in range(3):
        c0 = att_c0 + part * n_groups * gw
        res = jnp.dot(hb, w_ref[:, c0:c0 + n_groups * gw], preferred_element_type=F32)
        if part == 0:
            res = res * (ATT_HEAD_DIM ** -0.5)
        for g, (_, dil) in enumerate(ATT_GROUPS):
            o_ref = att_refs[g * 3 + part]
            sub = res[:, g * gw:(g + 1) * gw]
            if dil == 1:
                o_ref[0, 0] = sub.astype(BF16)
            else:
                for c in range(gw // LANES):
                    scr[c] = sub[:, c * LANES:(c + 1) * LANES]
                for r in range(dil):
                    o_ref[0, r] = jnp.concatenate(
                        [scr[c, pl.ds(r, tm // dil, stride=dil), :] for c in range(gw // LANES)],
                        axis=1).astype(BF16)


def _inproj(x2, g, scale, shift, w_bf16, bsz, seq, flat_segs, att_c0, tm):
    t, d = x2.shape
    gw = ATT_HEADS_PER_GROUP * ATT_HEAD_DIM
    n_per = seq // tm
    per_b = lambda i: (i // n_per, 0, 0)
    att_shapes, att_specs = [], []
    for _, dil in ATT_GROUPS:
        for _ in range(3):
            att_shapes.append(jax.ShapeDtypeStruct((bsz, dil, seq // dil, gw), BF16))
            att_specs.append(pl.BlockSpec((1, dil, tm // dil, gw),
                                          lambda i: (i // n_per, 0, i % n_per, 0)))
    outs = pl.pallas_call(
        functools.partial(_inproj_kernel, len(flat_segs),
                          tuple((c0, c0 + wdt) for c0, wdt, _ in flat_segs), att_c0),
        out_shape=[jax.ShapeDtypeStruct((t, wdt), dt) for _, wdt, dt in flat_segs] + att_shapes,
        grid=(t // tm,),
        in_specs=[pl.BlockSpec((tm, d), lambda i: (i, 0)),
                  pl.BlockSpec((1, d), lambda i: (0, 0)),
                  pl.BlockSpec((1, 1, d), per_b),
                  pl.BlockSpec((1, 1, d), per_b),
                  pl.BlockSpec(w_bf16.shape, lambda i: (0, 0))],
        out_specs=[pl.BlockSpec((tm, wdt), lambda i: (i, 0)) for _, wdt, _ in flat_segs]
        + att_specs,
        scratch_shapes=[pltpu.VMEM((gw // LANES, tm, LANES), F32)],
        compiler_params=_params(),
        name="in_proj",
    )(x2, g.reshape(1, d), scale, shift, w_bf16)
    return outs[:len(flat_segs)], outs[len(flat_segs):]


def _hgrn_kernel(ts, q_ref, f_ref, v_ref, gt_ref, lb_ref, ng_ref, o_ref, st_ref, b_ref):
    dk = q_ref.shape[1] // HG_HEADS
    n_chunks = ts // HG_CHUNK
    n_blk = HG_CHUNK // HG_BLOCK

    @pl.when(pl.program_id(1) == 0)
    def _():
        st_ref[...] = jnp.zeros_like(st_ref)

    row = lax.broadcasted_iota(I32, (LANES, LANES), 0)
    col = lax.broadcasted_iota(I32, (LANES, LANES), 1)
    same_chunk = (row // HG_CHUNK) == (col // HG_CHUNK)
    cum_mat = jnp.where(same_chunk & (col <= row), 1.0, 0.0).astype(BF16)

    def chunk_cumsum(x):
        out = []
        for r0 in range(0, ts, LANES):
            rest = x[r0:r0 + LANES]
            acc = None
            for _ in range(3):
                term = rest.astype(BF16)
                part = jnp.dot(cum_mat, term, preferred_element_type=F32)
                acc = part if acc is None else acc + part
                rest = rest - term.astype(F32)
            out.append(acc)
        return jnp.concatenate(out, axis=0)

    def forget(cs):
        lb = lb_ref[:, cs]
        return lb + (1.0 - lb) * _sigmoid(f_ref[:, cs])

    b_min = None
    for h in range(HG_HEADS):
        cs = slice(h * dk, (h + 1) * dk)
        b = chunk_cumsum(jnp.log(forget(cs)))
        b_ref[:, cs] = b
        m = jnp.min(b)
        b_min = m if b_min is None else jnp.minimum(b_min, m)
    mild = b_min >= HG_MILD_DECAY

    def finish(h, o, st):
        cs = slice(h * dk, (h + 1) * dk)
        st_ref[h] = st
        y = _rms(o, ng_ref[:, cs]) * _silu(gt_ref[:, cs].astype(F32))
        o_ref[:, cs] = y.astype(o_ref.dtype)

    @pl.when(mild)
    def _():
        span = 2 * HG_CHUNK
        causal = (lax.broadcasted_iota(I32, (span, span), 0)
                  >= lax.broadcasted_iota(I32, (span, span), 1))
        nt = lambda x, y: lax.dot_general(x, y, (((1,), (1,)), ((), ())),
                                          preferred_element_type=F32)
        for h in range(HG_HEADS):
            cs = slice(h * dk, (h + 1) * dk)
            v = v_ref[:, cs]
            b = b_ref[:, cs]
            q = q_ref[:, cs].astype(F32)
            k = 1.0 - forget(cs)
            st = st_ref[h]
            o_rows = []
            for r0 in range(0, ts, span):
                sl = slice(r0, r0 + span)
                b_first, b_second = b[r0:r0 + HG_CHUNK], b[r0 + HG_CHUNK:r0 + span]
                end_first = b_first[HG_CHUNK - 1:HG_CHUNK]
                end_second = b_second[HG_CHUNK - 1:HG_CHUNK]
                e = jnp.exp(jnp.concatenate([b_first - end_first, b_second], axis=0))
                qe = (q[sl] * e).astype(BF16)
                ke = k[sl] / e
                a = jnp.where(causal, nt(qe, ke.astype(BF16)), 0.0).astype(BF16)
                st_in = (st * jnp.exp(end_first)).astype(BF16)
                o_rows.append(jnp.dot(a, v[sl], preferred_element_type=F32) + nt(qe, st_in))
                kend = (ke * jnp.exp(end_second)).astype(BF16)
                vt = v[sl].astype(F32).T.astype(BF16)
                st = (st * jnp.exp(end_first + end_second)
                      + jnp.dot(vt, kend, preferred_element_type=F32))
            finish(h, jnp.concatenate(o_rows, axis=0), st)

    @pl.when(jnp.logical_not(mild))
    def _():
        _hgrn_steep(ts, dk, n_chunks, n_blk, q_ref, v_ref, b_ref, st_ref, forget, finish)


def _hgrn_steep(ts, dk, n_chunks, n_blk, q_ref, v_ref, b_ref, st_ref, forget, finish):
    t_in_blk = lax.broadcasted_iota(I32, (ts, dk), 0) % HG_BLOCK

    for h in range(HG_HEADS):
        cs = slice(h * dk, (h + 1) * dk)
        q = q_ref[:, cs].astype(F32)
        v = v_ref[:, cs].astype(F32)
        k = 1.0 - forget(cs)
        b = b_ref[:, cs]

        o = jnp.sum(q * k, axis=-1, keepdims=True) * v
        for d in range(1, HG_BLOCK):
            k_d = pltpu.roll(k, d, axis=0)
            b_d = pltpu.roll(b, d, axis=0)
            v_d = pltpu.roll(v, d, axis=0)
            w = jnp.sum(q * k_d * jnp.exp(jnp.minimum(b - b_d, 0.0)), axis=-1, keepdims=True)
            o = o + jnp.where(t_in_blk >= d, w * v_d, 0.0)

        st = st_ref[h]
        o_rows = []
        for c in range(n_chunks):
            r0 = c * HG_CHUNK
            bc = b[r0:r0 + HG_CHUNK]
            qc = q[r0:r0 + HG_CHUNK]
            kc = k[r0:r0 + HG_CHUNK]
            vc = v[r0:r0 + HG_CHUNK].astype(BF16)
            st_b = st.astype(BF16)
            for i in range(n_blk):
                i0 = i * HG_BLOCK
                if i == 0:
                    qt = qc[:HG_BLOCK] * jnp.exp(bc[:HG_BLOCK])
                    qs = qt
                else:
                    ref_row = bc[i0 - 1:i0]
                    qt = qc[i0:i0 + HG_BLOCK] * jnp.exp(bc[i0:i0 + HG_BLOCK] - ref_row)
                    qs = qt * jnp.exp(ref_row)
                oi = lax.dot_general(qs.astype(BF16), st_b, (((1,), (1,)), ((), ())),
                                     preferred_element_type=F32)
                if i > 0:
                    kh = kc[:i0] * jnp.exp(ref_row - bc[:i0])
                    a = lax.dot_general(qt.astype(BF16), kh.astype(BF16), (((1,), (1,)), ((), ())),
                                        preferred_element_type=F32)
                    oi = oi + jnp.dot(a.astype(BF16), vc[:i0], preferred_element_type=F32)
                o_rows.append(oi)
            b_end = bc[HG_CHUNK - 1:HG_CHUNK]
            kend = kc * jnp.exp(b_end - bc)
            vt = v[r0:r0 + HG_CHUNK].T.astype(BF16)
            st = st * jnp.exp(b_end) + jnp.dot(vt, kend.astype(BF16), preferred_element_type=F32)
        finish(h, o + jnp.concatenate(o_rows, axis=0), st)


def _hgrn(hq, hf, hi, hg, lb, ng, bsz, seq, ts):
    t, w = hq.shape
    dk = w // HG_HEADS
    n_s = seq // ts
    tile = lambda b, s: (b * n_s + s, 0)
    return pl.pallas_call(
        functools.partial(_hgrn_kernel, ts),
        out_shape=jax.ShapeDtypeStruct((t, w), BF16),
        grid=(bsz, n_s),
        in_specs=[pl.BlockSpec((ts, w), tile)] * 4
        + [pl.BlockSpec((1, w), lambda b, s: (0, 0))] * 2,
        out_specs=pl.BlockSpec((ts, w), tile),
        scratch_shapes=[pltpu.VMEM((HG_HEADS, dk, dk), F32), pltpu.VMEM((ts, w), F32)],
        compiler_params=_params(2),
        name="hgrn2",
    )(hq, hf, hi, hg, lb.reshape(1, w), ng.reshape(1, w))


def _attn_kernel(nk, nq, nr, q_ref, kp_ref, kc_ref, vp_ref, vc_ref, o_ref, lse_ref):
    n = pl.program_id(2)
    e = ATT_HEAD_DIM
    i = lax.broadcasted_iota(I32, (nk, 2 * nk), 0)
    j = lax.broadcasted_iota(I32, (nk, 2 * nk), 1)
    band = (j >= i) & (j <= i + nk)
    first_head = lax.broadcasted_iota(I32, (nk, LANES), 1) < e
    zero = jnp.zeros((), q_ref.dtype)
    for r in range(nr):
        kk = jnp.concatenate([kp_ref[0, r], kc_ref[0, r]], axis=0)
        vv = jnp.concatenate([vp_ref[0, r], vc_ref[0, r]], axis=0)
        for b in range(nq):
            valid = band & ((j >= nk) | (n * nq + b > 0))
            rows = slice(b * nk, (b + 1) * nk)
            for c in range(0, ATT_HEADS_PER_GROUP * e, LANES):
                q = q_ref[0, r, rows, c:c + LANES]
                kb = kk[b * nk:(b + 2) * nk, c:c + LANES]
                vb = vv[b * nk:(b + 2) * nk, c:c + LANES]
                outs, lses = [], []
                for keep in (first_head, jnp.logical_not(first_head)):
                    s = lax.dot_general(jnp.where(keep, q, zero), kb, (((1,), (1,)), ((), ())),
                                        preferred_element_type=F32)
                    s = jnp.where(valid, s, -jnp.inf)
                    m = jnp.max(s, axis=-1, keepdims=True)
                    p = jnp.exp(s - m)
                    l = jnp.sum(p, axis=-1, keepdims=True)
                    outs.append(jnp.dot(p.astype(BF16), vb, preferred_element_type=F32) / l)
                    lses.append(m + jnp.log(l))
                o_ref[0, r, rows, c:c + LANES] = jnp.where(first_head, outs[0], outs[1])
                lse_ref[0, r, rows, c:c + LANES] = jnp.where(first_head, lses[0], lses[1])


def _attn_group(q, k, v, g, blocks_per_step):
    window, dil = ATT_GROUPS[g]
    nk = window // dil
    bsz, _, ln, gw = q.shape
    nq = min(blocks_per_step, ln // nk)
    nr = min(blocks_per_step // nq, dil)
    assert ln % (nk * nq) == 0 and dil % nr == 0 and 2 * ATT_HEAD_DIM == LANES
    cur = pl.BlockSpec((1, nr, nq * nk, gw), lambda b, r, n: (b, r, n, 0))
    prev = pl.BlockSpec((1, nr, nk, gw), lambda b, r, n: (b, r, jnp.maximum(n * nq - 1, 0), 0))
    return pl.pallas_call(
        functools.partial(_attn_kernel, nk, nq, nr),
        out_shape=[jax.ShapeDtypeStruct(q.shape, F32)] * 2,
        grid=(bsz, dil // nr, ln // (nk * nq)),
        in_specs=[cur, prev, cur, prev, cur],
        out_specs=[cur, cur],
        compiler_params=_params(3),
        name=f"dilated_attn_g{g}",
    )(q, k, k, v, v)


def _token_major(ref, scr):
    dil, rows = ref.shape[1], ref.shape[2]
    if dil == 1:
        return ref[0, 0]
    n_col = scr.shape[0]
    for r in range(dil):
        for c in range(n_col):
            scr[c, pl.ds(r, rows, stride=dil), :] = ref[0, r, :, c * LANES:(c + 1) * LANES]
    return jnp.concatenate([scr[c] for c in range(n_col)], axis=1)


def _merge_kernel(ya_ref, o0_ref, o1_ref, o2_ref, l0_ref, l1_ref, l2_ref, ga_ref, gb_ref, x_ref,
                  g1_ref, sc2_ref, sh2_ref, g2_ref, n2_ref, wa_ref, wb_ref, wo_ref, wr_ref, wrl_ref,
                  wsg_ref, wsu_ref, wsd_ref, bias_ref, x1_ref, hp_ref, idx_ref, gate_ref, rank_ref,
                  cnt_ref, carry_ref, lg_ref, *scr):
    step = pl.program_id(0)

    @pl.when(step == 0)
    def _():
        carry_ref[...] = jnp.zeros_like(carry_ref)
        lg_ref[...] = jnp.zeros_like(lg_ref)

    _route(lg_ref[...], jnp.where(step > 0, 1.0, 0.0), bias_ref, idx_ref, gate_ref, rank_ref,
           cnt_ref, carry_ref)

    l0, l1, l2 = (_token_major(r, s) for r, s in zip((l0_ref, l1_ref, l2_ref), scr[:3]))
    o0, o1, o2 = (_token_major(r, s) for r, s in zip((o0_ref, o1_ref, o2_ref), scr[3:]))
    m = jnp.maximum(jnp.maximum(l0, l1), l2)
    e0, e1, e2 = jnp.exp(l0 - m), jnp.exp(l1 - m), jnp.exp(l2 - m)
    yb = (e0 * o0 + e1 * o1 + e2 * o2) / (e0 + e1 + e2)
    merged = (_sigmoid(ga_ref[...].astype(F32))
              * jnp.dot(ya_ref[...], wa_ref[...], preferred_element_type=F32)
              + _sigmoid(gb_ref[...].astype(F32))
              * jnp.dot(yb.astype(BF16), wb_ref[...], preferred_element_type=F32))
    x1 = x_ref[...] + g1_ref[0] * jnp.dot(merged.astype(BF16), wo_ref[...],
                                           preferred_element_type=F32)
    h2 = _rms(x1, n2_ref[...]) * (1.0 + sc2_ref[0]) + sh2_ref[0]
    hb = h2.astype(BF16)
    act = (_silu(jnp.dot(hb, wsg_ref[...], preferred_element_type=F32))
           * jnp.dot(hb, wsu_ref[...], preferred_element_type=F32))
    shared = jnp.dot(act.astype(BF16), wsd_ref[...], preferred_element_type=F32)
    x1_ref[...] = x1 + g2_ref[0] * shared
    hp_ref[...] = _pack_halves(h2)
    h_lo = (h2 - hb.astype(F32)).astype(BF16)
    nt = lambda a, b: lax.dot_general(a, b, (((1,), (1,)), ((), ())), preferred_element_type=F32)
    lg_ref[...] = nt(wr_ref[...], hb) + (nt(wr_ref[...], h_lo) + nt(wrl_ref[...], hb))


def _merge(ya, att, ga, gb, x2, gate1, scale2, shift2, gate2, norm2_g, wa, wb, wo, wr_t, wsg, wsu,
           wsd, router_bias, seq, tm):
    t, d = x2.shape
    n_e = wr_t.shape[0]
    wr_hi = wr_t.astype(BF16)
    wr_lo = (wr_t - wr_hi.astype(F32)).astype(BF16)
    n_per = seq // tm
    n_tiles = t // tm
    tile = lambda i: jnp.minimum(i, n_tiles - 1)
    per_b = lambda i: (tile(i) // n_per, 0, 0)
    rows = lambda wdt: pl.BlockSpec((tm, wdt), lambda i: (tile(i), 0))
    full = lambda a: pl.BlockSpec(a.shape, lambda i: (0,) * a.ndim)
    vec = pl.BlockSpec((1, 1, d), per_b)
    (o0, l0), (o1, l1), (o2, l2) = att
    gw = o0.shape[3]
    by_residue = lambda a: pl.BlockSpec((1, a.shape[1], tm // a.shape[1], gw),
                                        lambda i: (tile(i) // n_per, 0, tile(i) % n_per, 0))
    att_in = (o0, o1, o2, l0, l1, l2)
    bias_col = router_bias.reshape(n_e, 1)
    tok = pl.BlockSpec((TOP_K, tm), lambda i: (0, jnp.maximum(i - 1, 0)))
    return pl.pallas_call(
        _merge_kernel,
        out_shape=[jax.ShapeDtypeStruct((t, d), F32),
                   jax.ShapeDtypeStruct((t, d // 2), U32),
                   jax.ShapeDtypeStruct((TOP_K, t), I32), jax.ShapeDtypeStruct((TOP_K, t), F32),
                   jax.ShapeDtypeStruct((TOP_K, t), I32), jax.ShapeDtypeStruct((n_e, LANES), I32)],
        grid=(n_tiles + 1,),
        in_specs=[rows(ya.shape[1])] + [by_residue(a) for a in att_in] + [rows(d)] * 3
        + [vec, vec, vec, vec, pl.BlockSpec((1, d), lambda i: (0, 0))]
        + [full(a) for a in (wa, wb, wo, wr_hi, wr_lo, wsg, wsu, wsd, bias_col)],
        out_specs=[rows(d), rows(d // 2), tok, tok, tok,
                   pl.BlockSpec((n_e, LANES), lambda i: (0, 0))],
        scratch_shapes=[pltpu.VMEM((n_e, 1), F32), pltpu.VMEM((n_e, tm), F32)]
        + [pltpu.VMEM((gw // LANES, tm, LANES), F32)] * 6,
        compiler_params=_params(),
        name="merge_router",
    )(ya, *att_in, ga, gb, x2, gate1, scale2, shift2, gate2,
      norm2_g.reshape(1, d), wa, wb, wo, wr_hi, wr_lo, wsg, wsu, wsd, bias_col)


def _route(logits, live, bias_ref, idx_ref, gate_ref, rank_ref, cnt_ref, carry_ref):
    n_e, tt = logits.shape
    scores = _sigmoid(logits)
    sel = scores + bias_ref[...]
    eio = lax.broadcasted_iota(I32, (n_e, tt), 0)
    picked = jnp.zeros((n_e, tt), F32)
    idxs, vals = [], []
    for _ in range(TOP_K):
        m = jnp.max(sel, axis=0, keepdims=True)
        ik = jnp.min(jnp.where(sel == m, eio, n_e), axis=0, keepdims=True)
        hit = eio == ik
        vals.append(jnp.sum(jnp.where(hit, scores, 0.0), axis=0, keepdims=True))
        sel = jnp.where(hit, -jnp.inf, sel)
        picked = picked + jnp.where(hit, 1.0, 0.0)
        idxs.append(ik)
    denom = vals[0]
    for v in vals[1:]:
        denom = denom + v
    gate_ref[...] = jnp.concatenate([v / denom * ROUTE_SCALE for v in vals], axis=0)
    idx_ref[...] = jnp.concatenate(idxs, axis=0)

    upper = (lax.broadcasted_iota(I32, (tt, tt), 0) <= lax.broadcasted_iota(I32, (tt, tt), 1))
    incl = jnp.dot(picked.astype(BF16), jnp.where(upper, 1.0, 0.0).astype(BF16),
                   preferred_element_type=F32)
    before = incl - picked + carry_ref[...]
    rank_ref[...] = jnp.concatenate(
        [jnp.sum(jnp.where(eio == ik, before, 0.0), axis=0, keepdims=True) for ik in idxs],
        axis=0).astype(I32)
    carry_ref[...] = carry_ref[...] + jnp.sum(picked, axis=1, keepdims=True) * live
    cnt_ref[...] = jnp.broadcast_to(carry_ref[...], cnt_ref.shape).astype(I32)


def _dest_kernel(idx_ref, rank_ref, start_ref, o_ref):
    k, tt = idx_ref.shape
    n_e = start_ref.shape[0]
    eio = lax.broadcasted_iota(I32, (n_e, tt), 0)
    start = start_ref[...]
    rows = [jnp.sum(jnp.where(eio == idx_ref[r:r + 1, :], start, 0), axis=0, keepdims=True)
            for r in range(k)]
    o_ref[...] = jnp.concatenate(rows, axis=0) + rank_ref[...]


def _dest(idx, rank, seg_start, tt):
    k, t = idx.shape
    n_e = seg_start.shape[0]
    tok = pl.BlockSpec((k, tt), lambda i: (0, i))
    return pl.pallas_call(
        _dest_kernel,
        out_shape=jax.ShapeDtypeStruct((k, t), I32),
        grid=(t // tt,),
        in_specs=[tok, tok, pl.BlockSpec((n_e, 1), lambda i: (0, 0))],
        out_specs=tok,
        compiler_params=_params(),
        name="moe_dest",
    )(idx, rank, seg_start.reshape(n_e, 1))


def _sc_mesh():
    return plsc.VectorSubcoreMesh(core_axis_name="core", subcore_axis_name="subcore")


def _sc_scatter_rows(rows, dest, n_out):
    k, t = dest.shape
    w = rows.shape[1]
    mesh = _sc_mesh()
    n_workers = mesh.num_cores * mesh.num_subcores
    win_per_worker = t // (SC_WINDOW * n_workers)
    assert win_per_worker * SC_WINDOW * n_workers == t

    @functools.partial(
        pl.kernel, out_type=jax.ShapeDtypeStruct((n_out, w), rows.dtype), mesh=mesh,
        scratch_types=[pltpu.VMEM((SC_WINDOW, w), rows.dtype)]
        + [pltpu.VMEM((1, SC_WINDOW), I32)] * k + [pltpu.SemaphoreType.DMA],
        name="moe_dispatch_sc")
    def run(rows_hbm, idx_hbm, out_hbm, rows_v, *rest):
        idx_v, sem = rest[:k], rest[k]
        worker = lax.axis_index("subcore") * mesh.num_cores + lax.axis_index("core")

        @pl.loop(0, win_per_worker)
        def _(j):
            t0 = pl.multiple_of((worker * win_per_worker + j) * SC_WINDOW, SC_WINDOW)
            pltpu.sync_copy(rows_hbm.at[pl.ds(t0, SC_WINDOW)], rows_v)
            for r in range(k):
                pltpu.sync_copy(idx_hbm.at[:, pl.ds(r * t + t0, SC_WINDOW)], idx_v[r])
            copies = [pltpu.async_copy(rows_v, out_hbm.at[idx_v[r].at[0]], sem) for r in range(k)]
            for c in copies:
                c.wait()

    return run(rows, dest.reshape(1, k * t))


def _sc_gather_rows(table, dest):
    k, t = dest.shape
    w = table.shape[1]
    mesh = _sc_mesh()
    n_workers = mesh.num_cores * mesh.num_subcores
    win_per_worker = (k * t) // (SC_WINDOW * n_workers)
    assert win_per_worker * SC_WINDOW * n_workers == k * t

    @functools.partial(
        pl.kernel, out_type=jax.ShapeDtypeStruct((k * t, w), table.dtype), mesh=mesh,
        scratch_types=[pltpu.VMEM((SC_WINDOW, w), table.dtype), pltpu.VMEM((1, SC_WINDOW), I32)],
        name="moe_gather_sc")
    def run(table_hbm, idx_hbm, out_hbm, rows_v, idx_v):
        worker = lax.axis_index("subcore") * mesh.num_cores + lax.axis_index("core")

        @pl.loop(0, win_per_worker)
        def _(j):
            p0 = pl.multiple_of((worker * win_per_worker + j) * SC_WINDOW, SC_WINDOW)
            pltpu.sync_copy(idx_hbm.at[:, pl.ds(p0, SC_WINDOW)], idx_v)
            pltpu.sync_copy(table_hbm.at[idx_v.at[0]], rows_v)
            pltpu.sync_copy(rows_v, out_hbm.at[pl.ds(p0, SC_WINDOW)])

    return run(table, dest.reshape(1, k * t))


def _expert_kernel(start_ref, nblk_ref, xs_ref, wg_ref, wu_ref, wd_ref, ys_ref,
                   xbuf, ybuf, wgb, wub, wdb, wbuf_g, wbuf_u, wbuf_d, sem_in, sem_out, sem_w):
    wbuf = (wbuf_g, wbuf_u, wbuf_d)
    e = pl.program_id(0)
    n_e = pl.num_programs(0)
    nb = nblk_ref[e]
    g0 = start_ref[e] // MOE_BLOCK
    n_used = start_ref[n_e - 1] // MOE_BLOCK + nblk_ref[n_e - 1]
    n_in, n_out = xbuf.shape[0], ybuf.shape[0]

    def rows(g):
        return pl.ds(pl.multiple_of(g * MOE_BLOCK, MOE_BLOCK), MOE_BLOCK)

    def in_copy(g):
        slot = lax.rem(g, n_in)
        return pltpu.make_async_copy(xs_ref.at[rows(g), :], xbuf.at[slot], sem_in.at[slot])

    def out_copy(g):
        slot = lax.rem(g, n_out)
        return pltpu.make_async_copy(ybuf.at[slot], ys_ref.at[rows(g), :], sem_out.at[slot])

    look = n_in - EXPERT_GROUP

    @pl.when(e == 0)
    def _():
        for g in range(look):
            @pl.when(g < n_used)
            def _():
                in_copy(g).start(priority=g % N_DMA_QUEUES)

    n_w = wbuf[0].shape[0]

    def weight_copies(ex):
        slot = lax.rem(ex, n_w)
        return [pltpu.make_async_copy(src.at[ex], buf.at[slot], sem_w.at[slot])
                for src, buf in zip((wg_ref, wu_ref, wd_ref), wbuf)]

    @pl.when(e == 0)
    def _():
        for ex in range(min(n_w, wg_ref.shape[0])):
            for c in weight_copies(ex):
                c.start()

    for c in weight_copies(e):
        c.wait()
    w_slot = lax.rem(e, n_w)

    @pl.when(nb > 0)
    def _():
        wgb[...] = wbuf[0][w_slot].astype(BF16)
        wub[...] = wbuf[1][w_slot].astype(BF16)
        wdb[...] = wbuf[2][w_slot].astype(BF16)

    @pl.when(e + n_w < n_e)
    def _():
        for c in weight_copies(e + n_w):
            c.start()

    @pl.when(nb > 0)
    def _():
        def swiglu(word):
            lo, hi = _unpack_halves(word)
            x = jnp.concatenate([lo.astype(BF16), hi.astype(BF16)], axis=1)
            gate = jnp.dot(x, wgb[...], preferred_element_type=F32)
            up = jnp.dot(x, wub[...], preferred_element_type=F32)
            act = (_silu(gate) * up).astype(BF16)
            return jnp.dot(act, wdb[...], preferred_element_type=F32)

        def process(g, m):
            for i in range(m):
                in_copy(g + i).wait()
            for i in range(m):
                @pl.when(g + look + i < n_used)
                def _():
                    in_copy(g + look + i).start(priority=i % N_DMA_QUEUES)
            y_all = swiglu(jnp.concatenate([xbuf[lax.rem(g + i, n_in)] for i in range(m)], axis=0))
            ys = [y_all[i * MOE_BLOCK:(i + 1) * MOE_BLOCK] for i in range(m)]
            for i in range(m):
                @pl.when(g + i >= n_out)
                def _():
                    out_copy(g + i - n_out).wait()

                ybuf[lax.rem(g + i, n_out)] = _pack_halves(ys[i])
                out_copy(g + i).start(priority=(i + 1) % N_DMA_QUEUES)

        def group_body(p, carry):
            process(g0 + p * EXPERT_GROUP, EXPERT_GROUP)
            return carry

        lax.fori_loop(0, nb // EXPERT_GROUP, group_body, 0)
        for m in range(1, EXPERT_GROUP):
            @pl.when(lax.rem(nb, EXPERT_GROUP) == m)
            def _():
                process(g0 + nb - m, m)

    @pl.when(e == n_e - 1)
    def _():
        for i in range(n_out):
            @pl.when(n_used - 1 - i >= 0)
            def _():
                out_copy(n_used - 1 - i).wait()


def _experts(seg_start, seg_blocks, xs, wg, wu, wd):
    n_slots, half = xs.shape
    n_e, d, de = wg.shape
    n_w = EXPERT_WEIGHT_BUFFERS
    return pl.pallas_call(
        _expert_kernel,
        out_shape=jax.ShapeDtypeStruct((n_slots, half), U32),
        grid_spec=pltpu.PrefetchScalarGridSpec(
            num_scalar_prefetch=2,
            grid=(n_e,),
            in_specs=[pl.BlockSpec(memory_space=pl.ANY)] * 4,
            out_specs=pl.BlockSpec(memory_space=pl.ANY),
            scratch_shapes=[pltpu.VMEM((EXPERT_IN_RING, MOE_BLOCK, half), U32),
                            pltpu.VMEM((EXPERT_OUT_RING, MOE_BLOCK, half), U32),
                            pltpu.VMEM((d, de), BF16), pltpu.VMEM((d, de), BF16),
                            pltpu.VMEM((de, d), BF16),
                            pltpu.VMEM((n_w, d, de), F32), pltpu.VMEM((n_w, d, de), F32),
                            pltpu.VMEM((n_w, de, d), F32),
                            pltpu.SemaphoreType.DMA((EXPERT_IN_RING,)),
                            pltpu.SemaphoreType.DMA((EXPERT_OUT_RING,)),
                            pltpu.SemaphoreType.DMA((n_w,))]),
        compiler_params=_params(),
        name="moe_experts",
    )(seg_start, seg_blocks, xs, wg, wu, wd)


def _combine_kernel(yg_ref, gt_ref, x_ref, g2_ref, fg_ref, o_ref):
    k = yg_ref.shape[0]
    gt = gt_ref[...]
    lo, hi = _unpack_halves(yg_ref[0])
    y_lo, y_hi = lo * gt[:, 0:1], hi * gt[:, 0:1]
    for r in range(1, k):
        lo, hi = _unpack_halves(yg_ref[r])
        y_lo, y_hi = y_lo + lo * gt[:, r:r + 1], y_hi + hi * gt[:, r:r + 1]
    y = jnp.concatenate([y_lo, y_hi], axis=1)
    o_ref[...] = _rms(x_ref[...] + g2_ref[0] * y, fg_ref[...])


def _combine_into_kernel(yg_ref, gt_ref, x_ref, g2_ref, fg_ref, prev_ref, o_ref):
    del prev_ref
    _combine_kernel(yg_ref, gt_ref, x_ref, g2_ref, fg_ref, o_ref)


def _combine(yg, tok0, gates_t, x1s, gate2, final_g, seq, tc, out_so_far=None):
    k, n, half = yg.shape
    t, d = x1s.shape
    b0 = tok0 // tc
    args = [yg, gates_t, x1s, gate2, final_g.reshape(1, d)]
    in_specs = [pl.BlockSpec((k, tc, half), lambda i: (0, i, 0)),
                pl.BlockSpec((tc, k), lambda i: (i + b0, 0)),
                pl.BlockSpec((tc, d), lambda i: (i + b0, 0)),
                pl.BlockSpec((1, 1, d), lambda i: (((i + b0) * tc) // seq, 0, 0)),
                pl.BlockSpec((1, d), lambda i: (0, 0))]
    aliases = {}
    kernel = _combine_kernel
    if out_so_far is not None:
        args.append(out_so_far)
        in_specs.append(pl.BlockSpec(memory_space=pl.ANY))
        aliases = {len(args) - 1: 0}
        kernel = _combine_into_kernel
    return pl.pallas_call(
        kernel,
        out_shape=jax.ShapeDtypeStruct((t, d), F32),
        grid=(n // tc,),
        in_specs=in_specs,
        out_specs=pl.BlockSpec((tc, d), lambda i: (i + b0, 0)),
        input_output_aliases=aliases,
        compiler_params=_params(),
        name="moe_combine",
    )(*args)


def _layer(x2, c, bsz, seq, lb_row, ada_w, ada_b, norm1_g, w_in, hg_norm_g, w_branch_a, w_branch_b,
           w_out, norm2_g, w_router, router_bias, w_exp_gate, w_exp_up, w_exp_down, w_sh_gate,
           w_sh_up, w_sh_down, final_g):
    t, d = x2.shape
    n_e = w_router.shape[1]
    mod = _ada(c, ada_w, ada_b).reshape(bsz, 6, 1, d)
    shift1, scale1, gate1, shift2, scale2, gate2 = (mod[:, j] for j in range(6))

    hw = hg_norm_g.shape[0]
    aw = len(ATT_GROUPS) * ATT_HEADS_PER_GROUP * ATT_HEAD_DIM
    flat_segs = [(0, hw, BF16), (hw, hw, F32), (2 * hw, hw, BF16), (3 * hw, hw, BF16),
                 (4 * hw + 3 * aw, d, BF16), (4 * hw + 3 * aw + d, d, BF16)]
    (hq, hf, hi, hg, ga, gb), qkv = _inproj(
        x2, norm1_g, scale1, shift1, w_in.astype(BF16), bsz, seq, flat_segs, 4 * hw,
        tm=IN_PROJ_TILE)

    ya = _hgrn(hq, hf, hi, hg, lb_row, hg_norm_g, bsz, seq, ts=HGRN_TILE)
    att = [_attn_group(*qkv[3 * g:3 * g + 3], g, blocks_per_step=ATT_BLOCKS_PER_STEP)
           for g in range(len(ATT_GROUPS))]

    x1s, hp, idx, gates, rank, cnt = _merge(
        ya, att, ga, gb, x2, gate1, scale2, shift2, gate2, norm2_g, w_branch_a.astype(BF16),
        w_branch_b.astype(BF16), w_out.astype(BF16), w_router.T, w_sh_gate.astype(BF16),
        w_sh_up.astype(BF16), w_sh_down.astype(BF16), router_bias, seq, tm=MERGE_TILE)
    counts = cnt[:, 0]
    padded = (counts + MOE_BLOCK - 1) // MOE_BLOCK * MOE_BLOCK
    seg_start = (jnp.cumsum(padded) - padded).astype(I32)
    n_blocks = -(-(t * TOP_K) // MOE_BLOCK) + n_e
    dest = _dest(idx, rank, seg_start, tt=DEST_TILE)

    xs = _sc_scatter_rows(hp, dest, n_blocks * MOE_BLOCK)
    ys = _experts(seg_start, (padded // MOE_BLOCK).astype(I32), xs, w_exp_gate, w_exp_up,
                  w_exp_down)
    out, n = None, t // COMBINE_PARTS
    for part in range(COMBINE_PARTS):
        yg = _sc_gather_rows(ys, dest[:, part * n:(part + 1) * n]).reshape(TOP_K, n, d // 2)
        out = _combine(yg, part * n, gates.T, x1s, gate2, final_g, seq, tc=COMBINE_TILE,
                       out_so_far=out)
    return out


def kernel(x, c, ada_w, ada_b, norm1_g, w_in, lb_logits, hg_norm_g, w_branch_a, w_branch_b, w_out,
           norm2_g, w_router, router_bias, w_exp_gate, w_exp_up, w_exp_down, w_sh_gate, w_sh_up,
           w_sh_down, final_g):
    bsz, seq, d = x.shape
    depth = ada_w.shape[0]
    assert depth == 1, "the last layer's kernels also apply the final norm"
    lb_table = jnp.cumsum(jax.nn.softmax(lb_logits.astype(F32), axis=0), axis=0)
    out = _layer(x.reshape(bsz * seq, d), c, bsz, seq, lb_table[0], ada_w[0], ada_b[0], norm1_g[0],
                 w_in[0], hg_norm_g[0], w_branch_a[0], w_branch_b[0], w_out[0], norm2_g[0],
                 w_router[0], router_bias[0], w_exp_gate[0], w_exp_up[0], w_exp_down[0],
                 w_sh_gate[0], w_sh_up[0], w_sh_down[0], final_g)
    return out.reshape(bsz, seq, d)
```

```python
import functools

import jax
import jax.numpy as jnp
from jax import lax
from jax.experimental import pallas as pl
from jax.experimental.pallas import tpu as pltpu
from jax.experimental.pallas import tpu_sc as plsc

F32 = jnp.float32
BF16 = jnp.bfloat16
I32 = jnp.int32
U32 = jnp.uint32
HIGHEST = lax.Precision.HIGHEST

HG_HEADS = 4
HG_BLOCK = 16
HG_CHUNK = 32
HG_MILD_DECAY = -80.0
ATT_GROUPS = ((128, 1), (512, 4), (2048, 16))
ATT_HEADS_PER_GROUP = 4
ATT_HEAD_DIM = 64
TOP_K = 8
ROUTE_SCALE = 2.5
MOE_BLOCK = 256
RMS_EPS = 1e-6
N_DMA_QUEUES = 2
SC_WINDOW = 128
COMBINE_PARTS = 8
EXPERT_WEIGHT_BUFFERS = 3
EXPERT_GROUP = 4
EXPERT_IN_RING = 8
EXPERT_OUT_RING = 6

LANES = 128
VMEM_LIMIT_BYTES = 56 * 1024 * 1024

IN_PROJ_TILE = 512
HGRN_TILE = 512
ATT_BLOCKS_PER_STEP = 16
MERGE_TILE = 512
DEST_TILE = 2048
COMBINE_TILE = 512


def _sigmoid(x):
    return 1.0 / (1.0 + jnp.exp(-x))


def _silu(x):
    return x * _sigmoid(x)


def _rms(x, g):
    return x * lax.rsqrt(jnp.mean(x * x, axis=-1, keepdims=True) + RMS_EPS) * g


def _pack_halves(x):
    n = x.shape[1] // 2
    bits = lax.bitcast_convert_type(x.astype(BF16).astype(F32), U32)
    return (bits[:, :n] >> 16) | (bits[:, n:] & jnp.uint32(0xFFFF0000))


def _unpack_halves(word):
    lo = lax.bitcast_convert_type(word << 16, F32)
    hi = lax.bitcast_convert_type(word & jnp.uint32(0xFFFF0000), F32)
    return lo, hi


def _params(n_axes=1):
    return pltpu.CompilerParams(
        dimension_semantics=("arbitrary",) * n_axes, vmem_limit_bytes=VMEM_LIMIT_BYTES)


def _ada_kernel(c_ref, w_ref, b_ref, o_ref):
    sc = _silu(c_ref[...])
    o_ref[...] = jnp.dot(sc, w_ref[...], preferred_element_type=F32, precision=HIGHEST) + b_ref[...]


def _ada(c, w, b):
    bsz, d = c.shape
    n = w.shape[1]
    return pl.pallas_call(
        _ada_kernel,
        out_shape=jax.ShapeDtypeStruct((bsz, n), F32),
        grid=(n // d,),
        in_specs=[pl.BlockSpec((bsz, d), lambda j: (0, 0)),
                  pl.BlockSpec((d, d), lambda j: (0, j)),
                  pl.BlockSpec((1, d), lambda j: (0, j))],
        out_specs=pl.BlockSpec((bsz, d), lambda j: (0, j)),
        compiler_params=_params(),
        name="ada_mod",
    )(c, w, b.reshape(1, n))


def _inproj_kernel(n_flat, flat_ranges, att_c0, x_ref, g_ref, sc_ref, sh_ref, w_ref, *refs):
    flat_refs, att_refs, scr = refs[:n_flat], refs[n_flat:-1], refs[-1]
    tm = x_ref.shape[0]
    h = _rms(x_ref[...], g_ref[...]) * (1.0 + sc_ref[0]) + sh_ref[0]
    hb = h.astype(BF16)
    for (c0, c1), o_ref in zip(flat_ranges, flat_refs):
        o_ref[...] = jnp.dot(hb, w_ref[:, c0:c1], preferred_element_type=F32).astype(o_ref.dtype)
    gw = ATT_HEADS_PER_GROUP * ATT_HEAD_DIM
    n_groups = len(ATT_GROUPS)
    for part in range(3):
        c0 = att_c0 + part * n_groups * gw
        res = jnp.dot(hb, w_ref[:, c0:c0 + n_groups * gw], preferred_element_type=F32)
        if part == 0:
            res = res * (ATT_HEAD_DIM ** -0.5)
        for g, (_, dil) in enumerate(ATT_GROUPS):
            o_ref = att_refs[g * 3 + part]
            sub = res[:, g * gw:(g + 1) * gw]
            if dil == 1:
                o_ref[0, 0] = sub.astype(BF16)
            else:
                for c in range(gw // LANES):
                    scr[c] = sub[:, c * LANES:(c + 1) * LANES]
                for r in range(dil):
                    o_ref[0, r] = jnp.concatenate(
                        [scr[c, pl.ds(r, tm // dil, stride=dil), :] for c in range(gw // LANES)],
                        axis=1).astype(BF16)


def _inproj(x2, g, scale, shift, w_bf16, bsz, seq, flat_segs, att_c0, tm):
    t, d = x2.shape
    gw = ATT_HEADS_PER_GROUP * ATT_HEAD_DIM
    n_per = seq // tm
    per_b = lambda i: (i // n_per, 0, 0)
    att_shapes, att_specs = [], []
    for _, dil in ATT_GROUPS:
        for _ in range(3):
            att_shapes.append(jax.ShapeDtypeStruct((bsz, dil, seq // dil, gw), BF16))
            att_specs.append(pl.BlockSpec((1, dil, tm // dil, gw),
                                          lambda i: (i // n_per, 0, i % n_per, 0)))
    outs = pl.pallas_call(
        functools.partial(_inproj_kernel, len(flat_segs),
                          tuple((c0, c0 + wdt) for c0, wdt, _ in flat_segs), att_c0),
        out_shape=[jax.ShapeDtypeStruct((t, wdt), dt) for _, wdt, dt in flat_segs] + att_shapes,
        grid=(t // tm,),
        in_specs=[pl.BlockSpec((tm, d), lambda i: (i, 0)),
                  pl.BlockSpec((1, d), lambda i: (0, 0)),
                  pl.BlockSpec((1, 1, d), per_b),
                  pl.BlockSpec((1, 1, d), per_b),
                  pl.BlockSpec(w_bf16.shape, lambda i: (0, 0))],
        out_specs=[pl.BlockSpec((tm, wdt), lambda i: (i, 0)) for _, wdt, _ in flat_segs]
        + att_specs,
        scratch_shapes=[pltpu.VMEM((gw // LANES, tm, LANES), F32)],
        compiler_params=_params(),
        name="in_proj",
    )(x2, g.reshape(1, d), scale, shift, w_bf16)
    return outs[:len(flat_segs)], outs[len(flat_segs):]


def _hgrn_kernel(ts, q_ref, f_ref, v_ref, gt_ref, lb_ref, ng_ref, o_ref, st_ref, b_ref):
    dk = q_ref.shape[1] // HG_HEADS
    n_chunks = ts // HG_CHUNK
    n_blk = HG_CHUNK // HG_BLOCK

    @pl.when(pl.program_id(1) == 0)
    def _():
        st_ref[...] = jnp.zeros_like(st_ref)

    row = lax.broadcasted_iota(I32, (LANES, LANES), 0)
    col = lax.broadcasted_iota(I32, (LANES, LANES), 1)
    same_chunk = (row // HG_CHUNK) == (col // HG_CHUNK)
    cum_mat = jnp.where(same_chunk & (col <= row), 1.0, 0.0).astype(BF16)

    def chunk_cumsum(x):
        out = []
        for r0 in range(0, ts, LANES):
            rest = x[r0:r0 + LANES]
            acc = None
            for _ in range(3):
                term = rest.astype(BF16)
                part = jnp.dot(cum_mat, term, preferred_element_type=F32)
                acc = part if acc is None else acc + part
                rest = rest - term.astype(F32)
            out.append(acc)
        return jnp.concatenate(out, axis=0)

    def forget(cs):
        lb = lb_ref[:, cs]
        return lb + (1.0 - lb) * _sigmoid(f_ref[:, cs])

    b_min = None
    for h in range(HG_HEADS):
        cs = slice(h * dk, (h + 1) * dk)
        b = chunk_cumsum(jnp.log(forget(cs)))
        b_ref[:, cs] = b
        m = jnp.min(b)
        b_min = m if b_min is None else jnp.minimum(b_min, m)
    mild = b_min >= HG_MILD_DECAY

    def finish(h, o, st):
        cs = slice(h * dk, (h + 1) * dk)
        st_ref[h] = st
        y = _rms(o, ng_ref[:, cs]) * _silu(gt_ref[:, cs].astype(F32))
        o_ref[:, cs] = y.astype(o_ref.dtype)

    @pl.when(mild)
    def _():
        span = 2 * HG_CHUNK
        causal = (lax.broadcasted_iota(I32, (span, span), 0)
                  >= lax.broadcasted_iota(I32, (span, span), 1))
        nt = lambda x, y: lax.dot_general(x, y, (((1,), (1,)), ((), ())),
                                          preferred_element_type=F32)
        for h in range(HG_HEADS):
            cs = slice(h * dk, (h + 1) * dk)
            v = v_ref[:, cs]
            b = b_ref[:, cs]
            q = q_ref[:, cs].astype(F32)
            k = 1.0 - forget(cs)
            st = st_ref[h]
            o_rows = []
            for r0 in range(0, ts, span):
                sl = slice(r0, r0 + span)
                b_first, b_second = b[r0:r0 + HG_CHUNK], b[r0 + HG_CHUNK:r0 + span]
                end_first = b_first[HG_CHUNK - 1:HG_CHUNK]
                end_second = b_second[HG_CHUNK - 1:HG_CHUNK]
                e = jnp.exp(jnp.concatenate([b_first - end_first, b_second], axis=0))
                qe = (q[sl] * e).astype(BF16)
                ke = k[sl] / e
                a = jnp.where(causal, nt(qe, ke.astype(BF16)), 0.0).astype(BF16)
                st_in = (st * jnp.exp(end_first)).astype(BF16)
                o_rows.append(jnp.dot(a, v[sl], preferred_element_type=F32) + nt(qe, st_in))
                kend = (ke * jnp.exp(end_second)).astype(BF16)
                vt = v[sl].astype(F32).T.astype(BF16)
                st = (st * jnp.exp(end_first + end_second)
                      + jnp.dot(vt, kend, preferred_element_type=F32))
            finish(h, jnp.concatenate(o_rows, axis=0), st)

    @pl.when(jnp.logical_not(mild))
    def _():
        _hgrn_steep(ts, dk, n_chunks, n_blk, q_ref, v_ref, b_ref, st_ref, forget, finish)


def _hgrn_steep(ts, dk, n_chunks, n_blk, q_ref, v_ref, b_ref, st_ref, forget, finish):
    t_in_blk = lax.broadcasted_iota(I32, (ts, dk), 0) % HG_BLOCK

    for h in range(HG_HEADS):
        cs = slice(h * dk, (h + 1) * dk)
        q = q_ref[:, cs].astype(F32)
        v = v_ref[:, cs].astype(F32)
        k = 1.0 - forget(cs)
        b = b_ref[:, cs]

        o = jnp.sum(q * k, axis=-1, keepdims=True) * v
        for d in range(1, HG_BLOCK):
            k_d = pltpu.roll(k, d, axis=0)
            b_d = pltpu.roll(b, d, axis=0)
            v_d = pltpu.roll(v, d, axis=0)
            w = jnp.sum(q * k_d * jnp.exp(jnp.minimum(b - b_d, 0.0)), axis=-1, keepdims=True)
            o = o + jnp.where(t_in_blk >= d, w * v_d, 0.0)

        st = st_ref[h]
        o_rows = []
        for c in range(n_chunks):
            r0 = c * HG_CHUNK
            bc = b[r0:r0 + HG_CHUNK]
            qc = q[r0:r0 + HG_CHUNK]
            kc = k[r0:r0 + HG_CHUNK]
            vc = v[r0:r0 + HG_CHUNK].astype(BF16)
            st_b = st.astype(BF16)
            for i in range(n_blk):
                i0 = i * HG_BLOCK
                if i == 0:
                    qt = qc[:HG_BLOCK] * jnp.exp(bc[:HG_BLOCK])
                    qs = qt
                else:
                    ref_row = bc[i0 - 1:i0]
                    qt = qc[i0:i0 + HG_BLOCK] * jnp.exp(bc[i0:i0 + HG_BLOCK] - ref_row)
                    qs = qt * jnp.exp(ref_row)
                oi = lax.dot_general(qs.astype(BF16), st_b, (((1,), (1,)), ((), ())),
                                     preferred_element_type=F32)
                if i > 0:
                    kh = kc[:i0] * jnp.exp(ref_row - bc[:i0])
                    a = lax.dot_general(qt.astype(BF16), kh.astype(BF16), (((1,), (1,)), ((), ())),
                                        preferred_element_type=F32)
                    oi = oi + jnp.dot(a.astype(BF16), vc[:i0], preferred_element_type=F32)
                o_rows.append(oi)
            b_end = bc[HG_CHUNK - 1:HG_CHUNK]
            kend = kc * jnp.exp(b_end - bc)
            vt = v[r0:r0 + HG_CHUNK].T.astype(BF16)
            st = st * jnp.exp(b_end) + jnp.dot(vt, kend.astype(BF16), preferred_element_type=F32)
        finish(h, o + jnp.concatenate(o_rows, axis=0), st)


def _hgrn(hq, hf, hi, hg, lb, ng, bsz, seq, ts):
    t, w = hq.shape
    dk = w // HG_HEADS
    n_s = seq // ts
    tile = lambda b, s: (b * n_s + s, 0)
    return pl.pallas_call(
        functools.partial(_hgrn_kernel, ts),
        out_shape=jax.ShapeDtypeStruct((t, w), BF16),
        grid=(bsz, n_s),
        in_specs=[pl.BlockSpec((ts, w), tile)] * 4
        + [pl.BlockSpec((1, w), lambda b, s: (0, 0))] * 2,
        out_specs=pl.BlockSpec((ts, w), tile),
        scratch_shapes=[pltpu.VMEM((HG_HEADS, dk, dk), F32), pltpu.VMEM((ts, w), F32)],
        compiler_params=_params(2),
        name="hgrn2",
    )(hq, hf, hi, hg, lb.reshape(1, w), ng.reshape(1, w))


def _attn_kernel(nk, nq, nr, q_ref, kp_ref, kc_ref, vp_ref, vc_ref, o_ref, lse_ref):
    n = pl.program_id(2)
    e = ATT_HEAD_DIM
    i = lax.broadcasted_iota(I32, (nk, 2 * nk), 0)
    j = lax.broadcasted_iota(I32, (nk, 2 * nk), 1)
    band = (j >= i) & (j <= i + nk)
    first_head = lax.broadcasted_iota(I32, (nk, LANES), 1) < e
    zero = jnp.zeros((), q_ref.dtype)
    for r in range(nr):
        kk = jnp.concatenate([kp_ref[0, r], kc_ref[0, r]], axis=0)
        vv = jnp.concatenate([vp_ref[0, r], vc_ref[0, r]], axis=0)
        for b in range(nq):
            valid = band & ((j >= nk) | (n * nq + b > 0))
            rows = slice(b * nk, (b + 1) * nk)
            for c in range(0, ATT_HEADS_PER_GROUP * e, LANES):
                q = q_ref[0, r, rows, c:c + LANES]
                kb = kk[b * nk:(b + 2) * nk, c:c + LANES]
                vb = vv[b * nk:(b + 2) * nk, c:c + LANES]
                outs, lses = [], []
                for keep in (first_head, jnp.logical_not(first_head)):
                    s = lax.dot_general(jnp.where(keep, q, zero), kb, (((1,), (1,)), ((), ())),
                                        preferred_element_type=F32)
                    s = jnp.where(valid, s, -jnp.inf)
                    m = jnp.max(s, axis=-1, keepdims=True)
                    p = jnp.exp(s - m)
                    l = jnp.sum(p, axis=-1, keepdims=True)
                    outs.append(jnp.dot(p.astype(BF16), vb, preferred_element_type=F32) / l)
                    lses.append(m + jnp.log(l))
                o_ref[0, r, rows, c:c + LANES] = jnp.where(first_head, outs[0], outs[1])
                lse_ref[0, r, rows, c:c + LANES] = jnp.where(first_head, lses[0], lses[1])


def _attn_group(q, k, v, g, blocks_per_step):
    window, dil = ATT_GROUPS[g]
    nk = window // dil
    bsz, _, ln, gw = q.shape
    nq = min(blocks_per_step, ln // nk)
    nr = min(blocks_per_step // nq, dil)
    assert ln % (nk * nq) == 0 and dil % nr == 0 and 2 * ATT_HEAD_DIM == LANES
    cur = pl.BlockSpec((1, nr, nq * nk, gw), lambda b, r, n: (b, r, n, 0))
    prev = pl.BlockSpec((1, nr, nk, gw), lambda b, r, n: (b, r, jnp.maximum(n * nq - 1, 0), 0))
    return pl.pallas_call(
        functools.partial(_attn_kernel, nk, nq, nr),
        out_shape=[jax.ShapeDtypeStruct(q.shape, F32)] * 2,
        grid=(bsz, dil // nr, ln // (nk * nq)),
        in_specs=[cur, prev, cur, prev, cur],
        out_specs=[cur, cur],
        compiler_params=_params(3),
        name=f"dilated_attn_g{g}",
    )(q, k, k, v, v)


def _token_major(ref, scr):
    dil, rows = ref.shape[1], ref.shape[2]
    if dil == 1:
        return ref[0, 0]
    n_col = scr.shape[0]
    for r in range(dil):
        for c in range(n_col):
            scr[c, pl.ds(r, rows, stride=dil), :] = ref[0, r, :, c * LANES:(c + 1) * LANES]
    return jnp.concatenate([scr[c] for c in range(n_col)], axis=1)


def _merge_kernel(ya_ref, o0_ref, o1_ref, o2_ref, l0_ref, l1_ref, l2_ref, ga_ref, gb_ref, x_ref,
                  g1_ref, sc2_ref, sh2_ref, g2_ref, n2_ref, wa_ref, wb_ref, wo_ref, wr_ref, wrl_ref,
                  wsg_ref, wsu_ref, wsd_ref, bias_ref, x1_ref, hp_ref, idx_ref, gate_ref, rank_ref,
                  cnt_ref, carry_ref, lg_ref, *scr):
    step = pl.program_id(0)

    @pl.when(step == 0)
    def _():
        carry_ref[...] = jnp.zeros_like(carry_ref)
        lg_ref[...] = jnp.zeros_like(lg_ref)

    _route(lg_ref[...], jnp.where(step > 0, 1.0, 0.0), bias_ref, idx_ref, gate_ref, rank_ref,
           cnt_ref, carry_ref)

    l0, l1, l2 = (_token_major(r, s) for r, s in zip((l0_ref, l1_ref, l2_ref), scr[:3]))
    o0, o1, o2 = (_token_major(r, s) for r, s in zip((o0_ref, o1_ref, o2_ref), scr[3:]))
    m = jnp.maximum(jnp.maximum(l0, l1), l2)
    e0, e1, e2 = jnp.exp(l0 - m), jnp.exp(l1 - m), jnp.exp(l2 - m)
    yb = (e0 * o0 + e1 * o1 + e2 * o2) / (e0 + e1 + e2)
    merged = (_sigmoid(ga_ref[...].astype(F32))
              * jnp.dot(ya_ref[...], wa_ref[...], preferred_element_type=F32)
              + _sigmoid(gb_ref[...].astype(F32))
              * jnp.dot(yb.astype(BF16), wb_ref[...], preferred_element_type=F32))
    x1 = x_ref[...] + g1_ref[0] * jnp.dot(merged.astype(BF16), wo_ref[...],
                                           preferred_element_type=F32)
    h2 = _rms(x1, n2_ref[...]) * (1.0 + sc2_ref[0]) + sh2_ref[0]
    hb = h2.astype(BF16)
    act = (_silu(jnp.dot(hb, wsg_ref[...], preferred_element_type=F32))
           * jnp.dot(hb, wsu_ref[...], preferred_element_type=F32))
    shared = jnp.dot(act.astype(BF16), wsd_ref[...], preferred_element_type=F32)
    x1_ref[...] = x1 + g2_ref[0] * shared
    hp_ref[...] = _pack_halves(h2)
    h_lo = (h2 - hb.astype(F32)).astype(BF16)
    nt = lambda a, b: lax.dot_general(a, b, (((1,), (1,)), ((), ())), preferred_element_type=F32)
    lg_ref[...] = nt(wr_ref[...], hb) + (nt(wr_ref[...], h_lo) + nt(wrl_ref[...], hb))


def _merge(ya, att, ga, gb, x2, gate1, scale2, shift2, gate2, norm2_g, wa, wb, wo, wr_t, wsg, wsu,
           wsd, router_bias, seq, tm):
    t, d = x2.shape
    n_e = wr_t.shape[0]
    wr_hi = wr_t.astype(BF16)
    wr_lo = (wr_t - wr_hi.astype(F32)).astype(BF16)
    n_per = seq // tm
    n_tiles = t // tm
    tile = lambda i: jnp.minimum(i, n_tiles - 1)
    per_b = lambda i: (tile(i) // n_per, 0, 0)
    rows = lambda wdt: pl.BlockSpec((tm, wdt), lambda i: (tile(i), 0))
    full = lambda a: pl.BlockSpec(a.shape, lambda i: (0,) * a.ndim)
    vec = pl.BlockSpec((1, 1, d), per_b)
    (o0, l0), (o1, l1), (o2, l2) = att
    gw = o0.shape[3]
    by_residue = lambda a: pl.BlockSpec((1, a.shape[1], tm // a.shape[1], gw),
                                        lambda i: (tile(i) // n_per, 0, tile(i) % n_per, 0))
    att_in = (o0, o1, o2, l0, l1, l2)
    bias_col = router_bias.reshape(n_e, 1)
    tok = pl.BlockSpec((TOP_K, tm), lambda i: (0, jnp.maximum(i - 1, 0)))
    return pl.pallas_call(
        _merge_kernel,
        out_shape=[jax.ShapeDtypeStruct((t, d), F32),
                   jax.ShapeDtypeStruct((t, d // 2), U32),
                   jax.ShapeDtypeStruct((TOP_K, t), I32), jax.ShapeDtypeStruct((TOP_K, t), F32),
                   jax.ShapeDtypeStruct((TOP_K, t), I32), jax.ShapeDtypeStruct((n_e, LANES), I32)],
        grid=(n_tiles + 1,),
        in_specs=[rows(ya.shape[1])] + [by_residue(a) for a in att_in] + [rows(d)] * 3
        + [vec, vec, vec, vec, pl.BlockSpec((1, d), lambda i: (0, 0))]
        + [full(a) for a in (wa, wb, wo, wr_hi, wr_lo, wsg, wsu, wsd, bias_col)],
        out_specs=[rows(d), rows(d // 2), tok, tok, tok,
                   pl.BlockSpec((n_e, LANES), lambda i: (0, 0))],
        scratch_shapes=[pltpu.VMEM((n_e, 1), F32), pltpu.VMEM((n_e, tm), F32)]
        + [pltpu.VMEM((gw // LANES, tm, LANES), F32)] * 6,
        compiler_params=_params(),
        name="merge_router",
    )(ya, *att_in, ga, gb, x2, gate1, scale2, shift2, gate2,
      norm2_g.reshape(1, d), wa, wb, wo, wr_hi, wr_lo, wsg, wsu, wsd, bias_col)


def _route(logits, live, bias_ref, idx_ref, gate_ref, rank_ref, cnt_ref, carry_ref):
    n_e, tt = logits.shape
    scores = _sigmoid(logits)
    sel = scores + bias_ref[...]
    eio = lax.broadcasted_iota(I32, (n_e, tt), 0)
    picked = jnp.zeros((n_e, tt), F32)
    idxs, vals = [], []
    for _ in range(TOP_K):
        m = jnp.max(sel, axis=0, keepdims=True)
        ik = jnp.min(jnp.where(sel == m, eio, n_e), axis=0, keepdims=True)
        hit = eio == ik
        vals.append(jnp.sum(jnp.where(hit, scores, 0.0), axis=0, keepdims=True))
        sel = jnp.where(hit, -jnp.inf, sel)
        picked = picked + jnp.where(hit, 1.0, 0.0)
        idxs.append(ik)
    denom = vals[0]
    for v in vals[1:]:
        denom = denom + v
    gate_ref[...] = jnp.concatenate([v / denom * ROUTE_SCALE for v in vals], axis=0)
    idx_ref[...] = jnp.concatenate(idxs, axis=0)

    upper = (lax.broadcasted_iota(I32, (tt, tt), 0) <= lax.broadcasted_iota(I32, (tt, tt), 1))
    incl = jnp.dot(picked.astype(BF16), jnp.where(upper, 1.0, 0.0).astype(BF16),
                   preferred_element_type=F32)
    before = incl - picked + carry_ref[...]
    rank_ref[...] = jnp.concatenate(
        [jnp.sum(jnp.where(eio == ik, before, 0.0), axis=0, keepdims=True) for ik in idxs],
        axis=0).astype(I32)
    carry_ref[...] = carry_ref[...] + jnp.sum(picked, axis=1, keepdims=True) * live
    cnt_ref[...] = jnp.broadcast_to(carry_ref[...], cnt_ref.shape).astype(I32)


def _dest_kernel(idx_ref, rank_ref, start_ref, o_ref):
    k, tt = idx_ref.shape
    n_e = start_ref.shape[0]
    eio = lax.broadcasted_iota(I32, (n_e, tt), 0)
    start = start_ref[...]
    rows = [jnp.sum(jnp.where(eio == idx_ref[r:r + 1, :], start, 0), axis=0, keepdims=True)
            for r in range(k)]
    o_ref[...] = jnp.concatenate(rows, axis=0) + rank_ref[...]


def _dest(idx, rank, seg_start, tt):
    k, t = idx.shape
    n_e = seg_start.shape[0]
    tok = pl.BlockSpec((k, tt), lambda i: (0, i))
    return pl.pallas_call(
        _dest_kernel,
        out_shape=jax.ShapeDtypeStruct((k, t), I32),
        grid=(t // tt,),
        in_specs=[tok, tok, pl.BlockSpec((n_e, 1), lambda i: (0, 0))],
        out_specs=tok,
        compiler_params=_params(),
        name="moe_dest",
    )(idx, rank, seg_start.reshape(n_e, 1))


def _sc_mesh():
    return plsc.VectorSubcoreMesh(core_axis_name="core", subcore_axis_name="subcore")


def _sc_scatter_rows(rows, dest, n_out):
    k, t = dest.shape
    w = rows.shape[1]
    mesh = _sc_mesh()
    n_workers = mesh.num_cores * mesh.num_subcores
    win_per_worker = t // (SC_WINDOW * n_workers)
    assert win_per_worker * SC_WINDOW * n_workers == t

    @functools.partial(
        pl.kernel, out_type=jax.ShapeDtypeStruct((n_out, w), rows.dtype), mesh=mesh,
        scratch_types=[pltpu.VMEM((SC_WINDOW, w), rows.dtype)]
        + [pltpu.VMEM((1, SC_WINDOW), I32)] * k + [pltpu.SemaphoreType.DMA],
        name="moe_dispatch_sc")
    def run(rows_hbm, idx_hbm, out_hbm, rows_v, *rest):
        idx_v, sem = rest[:k], rest[k]
        worker = lax.axis_index("subcore") * mesh.num_cores + lax.axis_index("core")

        @pl.loop(0, win_per_worker)
        def _(j):
            t0 = pl.multiple_of((worker * win_per_worker + j) * SC_WINDOW, SC_WINDOW)
            pltpu.sync_copy(rows_hbm.at[pl.ds(t0, SC_WINDOW)], rows_v)
            for r in range(k):
                pltpu.sync_copy(idx_hbm.at[:, pl.ds(r * t + t0, SC_WINDOW)], idx_v[r])
            copies = [pltpu.async_copy(rows_v, out_hbm.at[idx_v[r].at[0]], sem) for r in range(k)]
            for c in copies:
                c.wait()

    return run(rows, dest.reshape(1, k * t))


def _sc_gather_rows(table, dest):
    k, t = dest.shape
    w = table.shape[1]
    mesh = _sc_mesh()
    n_workers = mesh.num_cores * mesh.num_subcores
    win_per_worker = (k * t) // (SC_WINDOW * n_workers)
    assert win_per_worker * SC_WINDOW * n_workers == k * t

    @functools.partial(
        pl.kernel, out_type=jax.ShapeDtypeStruct((k * t, w), table.dtype), mesh=mesh,
        scratch_types=[pltpu.VMEM((SC_WINDOW, w), table.dtype), pltpu.VMEM((1, SC_WINDOW), I32)],
        name="moe_gather_sc")
    def run(table_hbm, idx_hbm, out_hbm, rows_v, idx_v):
        worker = lax.axis_index("subcore") * mesh.num_cores + lax.axis_index("core")

        @pl.loop(0, win_per_worker)
        def _(j):
            p0 = pl.multiple_of((worker * win_per_worker + j) * SC_WINDOW, SC_WINDOW)
            pltpu.sync_copy(idx_hbm.at[:, pl.ds(p0, SC_WINDOW)], idx_v)
            pltpu.sync_copy(table_hbm.at[idx_v.at[0]], rows_v)
            pltpu.sync_copy(rows_v, out_hbm.at[pl.ds(p0, SC_WINDOW)])

    return run(table, dest.reshape(1, k * t))


def _expert_kernel(start_ref, nblk_ref, xs_ref, wg_ref, wu_ref, wd_ref, ys_ref,
                   xbuf, ybuf, wgb, wub, wdb, wbuf_g, wbuf_u, wbuf_d, sem_in, sem_out, sem_w):
    wbuf = (wbuf_g, wbuf_u, wbuf_d)
    e = pl.program_id(0)
    n_e = pl.num_programs(0)
    nb = nblk_ref[e]
    g0 = start_ref[e] // MOE_BLOCK
    n_used = start_ref[n_e - 1] // MOE_BLOCK + nblk_ref[n_e - 1]
    n_in, n_out = xbuf.shape[0], ybuf.shape[0]

    def rows(g):
        return pl.ds(pl.multiple_of(g * MOE_BLOCK, MOE_BLOCK), MOE_BLOCK)

    def in_copy(g):
        slot = lax.rem(g, n_in)
        return pltpu.make_async_copy(xs_ref.at[rows(g), :], xbuf.at[slot], sem_in.at[slot])

    def out_copy(g):
        slot = lax.rem(g, n_out)
        return pltpu.make_async_copy(ybuf.at[slot], ys_ref.at[rows(g), :], sem_out.at[slot])

    look = n_in - EXPERT_GROUP

    @pl.when(e == 0)
    def _():
        for g in range(look):
            @pl.when(g < n_used)
            def _():
                in_copy(g).start(priority=g % N_DMA_QUEUES)

    n_w = wbuf[0].shape[0]

    def weight_copies(ex):
        slot = lax.rem(ex, n_w)
        return [pltpu.make_async_copy(src.at[ex], buf.at[slot], sem_w.at[slot])
                for src, buf in zip((wg_ref, wu_ref, wd_ref), wbuf)]

    @pl.when(e == 0)
    def _():
        for ex in range(min(n_w, wg_ref.shape[0])):
            for c in weight_copies(ex):
                c.start()

    for c in weight_copies(e):
        c.wait()
    w_slot = lax.rem(e, n_w)

    @pl.when(nb > 0)
    def _():
        wgb[...] = wbuf[0][w_slot].astype(BF16)
        wub[...] = wbuf[1][w_slot].astype(BF16)
        wdb[...] = wbuf[2][w_slot].astype(BF16)

    @pl.when(e + n_w < n_e)
    def _():
        for c in weight_copies(e + n_w):
            c.start()

    @pl.when(nb > 0)
    def _():
        def swiglu(word):
            lo, hi = _unpack_halves(word)
            x = jnp.concatenate([lo.astype(BF16), hi.astype(BF16)], axis=1)
            gate = jnp.dot(x, wgb[...], preferred_element_type=F32)
            up = jnp.dot(x, wub[...], preferred_element_type=F32)
            act = (_silu(gate) * up).astype(BF16)
            return jnp.dot(act, wdb[...], preferred_element_type=F32)

        def process(g, m):
            for i in range(m):
                in_copy(g + i).wait()
            for i in range(m):
                @pl.when(g + look + i < n_used)
                def _():
                    in_copy(g + look + i).start(priority=i % N_DMA_QUEUES)
            y_all = swiglu(jnp.concatenate([xbuf[lax.rem(g + i, n_in)] for i in range(m)], axis=0))
            ys = [y_all[i * MOE_BLOCK:(i + 1) * MOE_BLOCK] for i in range(m)]
            for i in range(m):
                @pl.when(g + i >= n_out)
                def _():
                    out_copy(g + i - n_out).wait()

                ybuf[lax.rem(g + i, n_out)] = _pack_halves(ys[i])
                out_copy(g + i).start(priority=(i + 1) % N_DMA_QUEUES)

        def group_body(p, carry):
            process(g0 + p * EXPERT_GROUP, EXPERT_GROUP)
            return carry

        lax.fori_loop(0, nb // EXPERT_GROUP, group_body, 0)
        for m in range(1, EXPERT_GROUP):
            @pl.when(lax.rem(nb, EXPERT_GROUP) == m)
            def _():
                process(g0 + nb - m, m)

    @pl.when(e == n_e - 1)
    def _():
        for i in range(n_out):
            @pl.when(n_used - 1 - i >= 0)
            def _():
                out_copy(n_used - 1 - i).wait()


def _experts(seg_start, seg_blocks, xs, wg, wu, wd):
    n_slots, half = xs.shape
    n_e, d, de = wg.shape
    n_w = EXPERT_WEIGHT_BUFFERS
    return pl.pallas_call(
        _expert_kernel,
        out_shape=jax.ShapeDtypeStruct((n_slots, half), U32),
        grid_spec=pltpu.PrefetchScalarGridSpec(
            num_scalar_prefetch=2,
            grid=(n_e,),
            in_specs=[pl.BlockSpec(memory_space=pl.ANY)] * 4,
            out_specs=pl.BlockSpec(memory_space=pl.ANY),
            scratch_shapes=[pltpu.VMEM((EXPERT_IN_RING, MOE_BLOCK, half), U32),
                            pltpu.VMEM((EXPERT_OUT_RING, MOE_BLOCK, half), U32),
                            pltpu.VMEM((d, de), BF16), pltpu.VMEM((d, de), BF16),
                            pltpu.VMEM((de, d), BF16),
                            pltpu.VMEM((n_w, d, de), F32), pltpu.VMEM((n_w, d, de), F32),
                            pltpu.VMEM((n_w, de, d), F32),
                            pltpu.SemaphoreType.DMA((EXPERT_IN_RING,)),
                            pltpu.SemaphoreType.DMA((EXPERT_OUT_RING,)),
                            pltpu.SemaphoreType.DMA((n_w,))]),
        compiler_params=_params(),
        name="moe_experts",
    )(seg_start, seg_blocks, xs, wg, wu, wd)


def _combine_kernel(yg_ref, gt_ref, x_ref, g2_ref, fg_ref, o_ref):
    k = yg_ref.shape[0]
    gt = gt_ref[...]
    lo, hi = _unpack_halves(yg_ref[0])
    y_lo, y_hi = lo * gt[:, 0:1], hi * gt[:, 0:1]
    for r in range(1, k):
        lo, hi = _unpack_halves(yg_ref[r])
        y_lo, y_hi = y_lo + lo * gt[:, r:r + 1], y_hi + hi * gt[:, r:r + 1]
    y = jnp.concatenate([y_lo, y_hi], axis=1)
    o_ref[...] = _rms(x_ref[...] + g2_ref[0] * y, fg_ref[...])


def _combine_into_kernel(yg_ref, gt_ref, x_ref, g2_ref, fg_ref, prev_ref, o_ref):
    del prev_ref
    _combine_kernel(yg_ref, gt_ref, x_ref, g2_ref, fg_ref, o_ref)


def _combine(yg, tok0, gates_t, x1s, gate2, final_g, seq, tc, out_so_far=None):
    k, n, half = yg.shape
    t, d = x1s.shape
    assert n % tc == 0 and tok0 % tc == 0
    b0 = tok0 // tc
    args = [yg, gates_t, x1s, gate2, final_g.reshape(1, d)]
    in_specs = [pl.BlockSpec((k, tc, half), lambda i: (0, i, 0)),
                pl.BlockSpec((tc, k), lambda i: (i + b0, 0)),
                pl.BlockSpec((tc, d), lambda i: (i + b0, 0)),
                pl.BlockSpec((1, 1, d), lambda i: (((i + b0) * tc) // seq, 0, 0)),
                pl.BlockSpec((1, d), lambda i: (0, 0))]
    aliases = {}
    kernel = _combine_kernel
    if out_so_far is not None:
        args.append(out_so_far)
        in_specs.append(pl.BlockSpec(memory_space=pl.ANY))
        aliases = {len(args) - 1: 0}
        kernel = _combine_into_kernel
    return pl.pallas_call(
        kernel,
        out_shape=jax.ShapeDtypeStruct((t, d), F32),
        grid=(n // tc,),
        in_specs=in_specs,
        out_specs=pl.BlockSpec((tc, d), lambda i: (i + b0, 0)),
        input_output_aliases=aliases,
        compiler_params=_params(),
        name="moe_combine",
    )(*args)


def _layer(x2, c, bsz, seq, lb_row, ada_w, ada_b, norm1_g, w_in, hg_norm_g, w_branch_a, w_branch_b,
           w_out, norm2_g, w_router, router_bias, w_exp_gate, w_exp_up, w_exp_down, w_sh_gate,
           w_sh_up, w_sh_down, final_g):
    t, d = x2.shape
    n_e = w_router.shape[1]
    mod = _ada(c, ada_w, ada_b).reshape(bsz, 6, 1, d)
    shift1, scale1, gate1, shift2, scale2, gate2 = (mod[:, j] for j in range(6))

    hw = hg_norm_g.shape[0]
    aw = len(ATT_GROUPS) * ATT_HEADS_PER_GROUP * ATT_HEAD_DIM
    flat_segs = [(0, hw, BF16), (hw, hw, F32), (2 * hw, hw, BF16), (3 * hw, hw, BF16),
                 (4 * hw + 3 * aw, d, BF16), (4 * hw + 3 * aw + d, d, BF16)]
    (hq, hf, hi, hg, ga, gb), qkv = _inproj(
        x2, norm1_g, scale1, shift1, w_in.astype(BF16), bsz, seq, flat_segs, 4 * hw,
        tm=IN_PROJ_TILE)

    ya = _hgrn(hq, hf, hi, hg, lb_row, hg_norm_g, bsz, seq, ts=HGRN_TILE)
    att = [_attn_group(*qkv[3 * g:3 * g + 3], g, blocks_per_step=ATT_BLOCKS_PER_STEP)
           for g in range(len(ATT_GROUPS))]

    x1s, hp, idx, gates, rank, cnt = _merge(
        ya, att, ga, gb, x2, gate1, scale2, shift2, gate2, norm2_g, w_branch_a.astype(BF16),
        w_branch_b.astype(BF16), w_out.astype(BF16), w_router.T, w_sh_gate.astype(BF16),
        w_sh_up.astype(BF16), w_sh_down.astype(BF16), router_bias, seq, tm=MERGE_TILE)
    counts = cnt[:, 0]
    padded = (counts + MOE_BLOCK - 1) // MOE_BLOCK * MOE_BLOCK
    seg_start = (jnp.cumsum(padded) - padded).astype(I32)
    n_blocks = -(-(t * TOP_K) // MOE_BLOCK) + n_e
    dest = _dest(idx, rank, seg_start, tt=DEST_TILE)

    xs = _sc_scatter_rows(hp, dest, n_blocks * MOE_BLOCK)
    ys = _experts(seg_start, (padded // MOE_BLOCK).astype(I32), xs, w_exp_gate, w_exp_up,
                  w_exp_down)
    out, n = None, t // COMBINE_PARTS
    for part in range(COMBINE_PARTS):
        yg = _sc_gather_rows(ys, dest[:, part * n:(part + 1) * n]).reshape(TOP_K, n, d // 2)
        out = _combine(yg, part * n, gates.T, x1s, gate2, final_g, seq, tc=COMBINE_TILE,
                       out_so_far=out)
    return out


def kernel(x, c, ada_w, ada_b, norm1_g, w_in, lb_logits, hg_norm_g, w_branch_a, w_branch_b, w_out,
           norm2_g, w_router, router_bias, w_exp_gate, w_exp_up, w_exp_down, w_sh_gate, w_sh_up,
           w_sh_down, final_g):
    bsz, seq, d = x.shape
    depth = ada_w.shape[0]
    assert depth == 1, "the last layer's kernels also apply the final norm"
    lb_table = jnp.cumsum(jax.nn.softmax(lb_logits.astype(F32), axis=0), axis=0)
    out = _layer(x.reshape(bsz * seq, d), c, bsz, seq, lb_table[0], ada_w[0], ada_b[0], norm1_g[0],
                 w_in[0], hg_norm_g[0], w_branch_a[0], w_branch_b[0], w_out[0], norm2_g[0],
                 w_router[0], router_bias[0], w_exp_gate[0], w_exp_up[0], w_exp_down[0],
                 w_sh_gate[0], w_sh_up[0], w_sh_down[0], final_g)
    return out.reshape(bsz, seq, d)
```

```python
import functools

import jax
import jax.numpy as jnp
from jax import lax
from jax.experimental import pallas as pl
from jax.experimental.pallas import tpu as pltpu
from jax.experimental.pallas import tpu_sc as plsc

F32 = jnp.float32
BF16 = jnp.bfloat16
I32 = jnp.int32
U32 = jnp.uint32
HIGHEST = lax.Precision.HIGHEST

HG_HEADS = 4
HG_BLOCK = 16
HG_CHUNK = 32
HG_MILD_DECAY = -80.0
ATT_GROUPS = ((128, 1), (512, 4), (2048, 16))
ATT_HEADS_PER_GROUP = 4
ATT_HEAD_DIM = 64
TOP_K = 8
ROUTE_SCALE = 2.5
MOE_BLOCK = 256
RMS_EPS = 1e-6
N_DMA_QUEUES = 2
SC_WINDOW = 128
COMBINE_PARTS = 8
EXPERT_WEIGHT_BUFFERS = 3
EXPERT_GROUP = 4
EXPERT_IN_RING = 8
EXPERT_OUT_RING = 6

LANES = 128
VMEM_LIMIT_BYTES = 56 * 1024 * 1024

IN_PROJ_TILE = 512
HGRN_TILE = 512
ATT_BLOCKS_PER_STEP = 16
MERGE_TILE = 512
DEST_TILE = 2048
COMBINE_TILE = 512


def _sigmoid(x):
    return 1.0 / (1.0 + jnp.exp(-x))


def _silu(x):
    return x * _sigmoid(x)


def _rms(x, g):
    return x * lax.rsqrt(jnp.mean(x * x, axis=-1, keepdims=True) + RMS_EPS) * g


def _pack_halves(x):
    n = x.shape[1] // 2
    bits = lax.bitcast_convert_type(x.astype(BF16).astype(F32), U32)
    return (bits[:, :n] >> 16) | (bits[:, n:] & jnp.uint32(0xFFFF0000))


def _unpack_halves(word):
    lo = lax.bitcast_convert_type(word << 16, F32)
    hi = lax.bitcast_convert_type(word & jnp.uint32(0xFFFF0000), F32)
    return lo, hi


def _params(n_axes=1):
    return pltpu.CompilerParams(
        dimension_semantics=("arbitrary",) * n_axes, vmem_limit_bytes=VMEM_LIMIT_BYTES)


def _ada_kernel(c_ref, w_ref, b_ref, o_ref):
    sc = _silu(c_ref[...])
    o_ref[...] = jnp.dot(sc, w_ref[...], preferred_element_type=F32, precision=HIGHEST) + b_ref[...]


def _ada(c, w, b):
    bsz, d = c.shape
    n = w.shape[1]
    return pl.pallas_call(
        _ada_kernel,
        out_shape=jax.ShapeDtypeStruct((bsz, n), F32),
        grid=(n // d,),
        in_specs=[pl.BlockSpec((bsz, d), lambda j: (0, 0)),
                  pl.BlockSpec((d, d), lambda j: (0, j)),
                  pl.BlockSpec((1, d), lambda j: (0, j))],
        out_specs=pl.BlockSpec((bsz, d), lambda j: (0, j)),
        compiler_params=_params(),
        name="ada_mod",
    )(c, w, b.reshape(1, n))


def _inproj_kernel(n_flat, flat_ranges, att_c0, x_ref, g_ref, sc_ref, sh_ref, w_ref, *refs):
    flat_refs, att_refs, scr = refs[:n_flat], refs[n_flat:-1], refs[-1]
    tm = x_ref.shape[0]
    h = _rms(x_ref[...], g_ref[...]) * (1.0 + sc_ref[0]) + sh_ref[0]
    hb = h.astype(BF16)
    for (c0, c1), o_ref in zip(flat_ranges, flat_refs):
        o_ref[...] = jnp.dot(hb, w_ref[:, c0:c1], preferred_element_type=F32).astype(o_ref.dtype)
    gw = ATT_HEADS_PER_GROUP * ATT_HEAD_DIM
    n_groups = len(ATT_GROUPS)
    for part in range(3):
        c0 = att_c0 + part * n_groups * gw
        res = jnp.dot(hb, w_ref[:, c0:c0 + n_groups * gw], preferred_element_type=F32)
        if part == 0:
            res = res * (ATT_HEAD_DIM ** -0.5)
        for g, (_, dil) in enumerate(ATT_GROUPS):
            o_ref = att_refs[g * 3 + part]
            sub = res[:, g * gw:(g + 1) * gw]
            if dil == 1:
                o_ref[0, 0] = sub.astype(BF16)
            else:
                for c in range(gw // LANES):
                    scr[c] = sub[:, c * LANES:(c + 1) * LANES]
                for r in range(dil):
                    o_ref[0, r] = jnp.concatenate(
                        [scr[c, pl.ds(r, tm // dil, stride=dil), :] for c in range(gw // LANES)],
                        axis=1).astype(BF16)


def _inproj(x2, g, scale, shift, w_bf16, bsz, seq, flat_segs, att_c0, tm):
    t, d = x2.shape
    gw = ATT_HEADS_PER_GROUP * ATT_HEAD_DIM
    n_per = seq // tm
    per_b = lambda i: (i // n_per, 0, 0)
    att_shapes, att_specs = [], []
    for _, dil in ATT_GROUPS:
        for _ in range(3):
            att_shapes.append(jax.ShapeDtypeStruct((bsz, dil, seq // dil, gw), BF16))
            att_specs.append(pl.BlockSpec((1, dil, tm // dil, gw),
                                          lambda i: (i // n_per, 0, i % n_per, 0)))
    outs = pl.pallas_call(
        functools.partial(_inproj_kernel, len(flat_segs),
                          tuple((c0, c0 + wdt) for c0, wdt, _ in flat_segs), att_c0),
        out_shape=[jax.ShapeDtypeStruct((t, wdt), dt) for _, wdt, dt in flat_segs] + att_shapes,
        grid=(t // tm,),
        in_specs=[pl.BlockSpec((tm, d), lambda i: (i, 0)),
                  pl.BlockSpec((1, d), lambda i: (0, 0)),
                  pl.BlockSpec((1, 1, d), per_b),
                  pl.BlockSpec((1, 1, d), per_b),
                  pl.BlockSpec(w_bf16.shape, lambda i: (0, 0))],
        out_specs=[pl.BlockSpec((tm, wdt), lambda i: (i, 0)) for _, wdt, _ in flat_segs]
        + att_specs,
        scratch_shapes=[pltpu.VMEM((gw // LANES, tm, LANES), F32)],
        compiler_params=_params(),
        name="in_proj",
    )(x2, g.reshape(1, d), scale, shift, w_bf16)
    return outs[:len(flat_segs)], outs[len(flat_segs):]


def _hgrn_kernel(ts, q_ref, f_ref, v_ref, gt_ref, lb_ref, ng_ref, o_ref, st_ref, b_ref):
    dk = q_ref.shape[1] // HG_HEADS
    n_chunks = ts // HG_CHUNK
    n_blk = HG_CHUNK // HG_BLOCK

    @pl.when(pl.program_id(1) == 0)
    def _():
        st_ref[...] = jnp.zeros_like(st_ref)

    row = lax.broadcasted_iota(I32, (LANES, LANES), 0)
    col = lax.broadcasted_iota(I32, (LANES, LANES), 1)
    same_chunk = (row // HG_CHUNK) == (col // HG_CHUNK)
    cum_mat = jnp.where(same_chunk & (col <= row), 1.0, 0.0).astype(BF16)

    def chunk_cumsum(x):
        out = []
        for r0 in range(0, ts, LANES):
            rest = x[r0:r0 + LANES]
            acc = None
            for _ in range(3):
                term = rest.astype(BF16)
                part = jnp.dot(cum_mat, term, preferred_element_type=F32)
                acc = part if acc is None else acc + part
                rest = rest - term.astype(F32)
            out.append(acc)
        return jnp.concatenate(out, axis=0)

    def forget(cs):
        lb = lb_ref[:, cs]
        return lb + (1.0 - lb) * _sigmoid(f_ref[:, cs])

    b_min = None
    for h in range(HG_HEADS):
        cs = slice(h * dk, (h + 1) * dk)
        b = chunk_cumsum(jnp.log(forget(cs)))
        b_ref[:, cs] = b
        m = jnp.min(b)
        b_min = m if b_min is None else jnp.minimum(b_min, m)
    mild = b_min >= HG_MILD_DECAY

    def finish(h, o, st):
        cs = slice(h * dk, (h + 1) * dk)
        st_ref[h] = st
        y = _rms(o, ng_ref[:, cs]) * _silu(gt_ref[:, cs].astype(F32))
        o_ref[:, cs] = y.astype(o_ref.dtype)

    @pl.when(mild)
    def _():
        span = 2 * HG_CHUNK
        causal = (lax.broadcasted_iota(I32, (span, span), 0)
                  >= lax.broadcasted_iota(I32, (span, span), 1))
        nt = lambda x, y: lax.dot_general(x, y, (((1,), (1,)), ((), ())),
                                          preferred_element_type=F32)
        for h in range(HG_HEADS):
            cs = slice(h * dk, (h + 1) * dk)
            v = v_ref[:, cs]
            b = b_ref[:, cs]
            q = q_ref[:, cs].astype(F32)
            k = 1.0 - forget(cs)
            st = st_ref[h]
            o_rows = []
            for r0 in range(0, ts, span):
                sl = slice(r0, r0 + span)
                b_first, b_second = b[r0:r0 + HG_CHUNK], b[r0 + HG_CHUNK:r0 + span]
                end_first = b_first[HG_CHUNK - 1:HG_CHUNK]
                end_second = b_second[HG_CHUNK - 1:HG_CHUNK]
                e = jnp.exp(jnp.concatenate([b_first - end_first, b_second], axis=0))
                qe = (q[sl] * e).astype(BF16)
                ke = k[sl] / e
                a = jnp.where(causal, nt(qe, ke.astype(BF16)), 0.0).astype(BF16)
                st_in = (st * jnp.exp(end_first)).astype(BF16)
                o_rows.append(jnp.dot(a, v[sl], preferred_element_type=F32) + nt(qe, st_in))
                kend = (ke * jnp.exp(end_second)).astype(BF16)
                vt = v[sl].astype(F32).T.astype(BF16)
                st = (st * jnp.exp(end_first + end_second)
                      + jnp.dot(vt, kend, preferred_element_type=F32))
            finish(h, jnp.concatenate(o_rows, axis=0), st)

    @pl.when(jnp.logical_not(mild))
    def _():
        _hgrn_steep(ts, dk, n_chunks, n_blk, q_ref, v_ref, b_ref, st_ref, forget, finish)


def _hgrn_steep(ts, dk, n_chunks, n_blk, q_ref, v_ref, b_ref, st_ref, forget, finish):
    t_in_blk = lax.broadcasted_iota(I32, (ts, dk), 0) % HG_BLOCK

    for h in range(HG_HEADS):
        cs = slice(h * dk, (h + 1) * dk)
        q = q_ref[:, cs].astype(F32)
        v = v_ref[:, cs].astype(F32)
        k = 1.0 - forget(cs)
        b = b_ref[:, cs]

        o = jnp.sum(q * k, axis=-1, keepdims=True) * v
        for d in range(1, HG_BLOCK):
            k_d = pltpu.roll(k, d, axis=0)
            b_d = pltpu.roll(b, d, axis=0)
            v_d = pltpu.roll(v, d, axis=0)
            w = jnp.sum(q * k_d * jnp.exp(jnp.minimum(b - b_d, 0.0)), axis=-1, keepdims=True)
            o = o + jnp.where(t_in_blk >= d, w * v_d, 0.0)

        st = st_ref[h]
        o_rows = []
        for c in range(n_chunks):
            r0 = c * HG_CHUNK
            bc = b[r0:r0 + HG_CHUNK]
            qc = q[r0:r0 + HG_CHUNK]
            kc = k[r0:r0 + HG_CHUNK]
            vc = v[r0:r0 + HG_CHUNK].astype(BF16)
            st_b = st.astype(BF16)
            for i in range(n_blk):
                i0 = i * HG_BLOCK
                if i == 0:
                    qt = qc[:HG_BLOCK] * jnp.exp(bc[:HG_BLOCK])
                    qs = qt
                else:
                    ref_row = bc[i0 - 1:i0]
                    qt = qc[i0:i0 + HG_BLOCK] * jnp.exp(bc[i0:i0 + HG_BLOCK] - ref_row)
                    qs = qt * jnp.exp(ref_row)
                oi = lax.dot_general(qs.astype(BF16), st_b, (((1,), (1,)), ((), ())),
                                     preferred_element_type=F32)
                if i > 0:
                    kh = kc[:i0] * jnp.exp(ref_row - bc[:i0])
                    a = lax.dot_general(qt.astype(BF16), kh.astype(BF16), (((1,), (1,)), ((), ())),
                                        preferred_element_type=F32)
                    oi = oi + jnp.dot(a.astype(BF16), vc[:i0], preferred_element_type=F32)
                o_rows.append(oi)
            b_end = bc[HG_CHUNK - 1:HG_CHUNK]
            kend = kc * jnp.exp(b_end - bc)
            vt = v[r0:r0 + HG_CHUNK].T.astype(BF16)
            st = st * jnp.exp(b_end) + jnp.dot(vt, kend.astype(BF16), preferred_element_type=F32)
        finish(h, o + jnp.concatenate(o_rows, axis=0), st)


def _hgrn(hq, hf, hi, hg, lb, ng, bsz, seq, ts):
    t, w = hq.shape
    dk = w // HG_HEADS
    n_s = seq // ts
    tile = lambda b, s: (b * n_s + s, 0)
    return pl.pallas_call(
        functools.partial(_hgrn_kernel, ts),
        out_shape=jax.ShapeDtypeStruct((t, w), BF16),
        grid=(bsz, n_s),
        in_specs=[pl.BlockSpec((ts, w), tile)] * 4
        + [pl.BlockSpec((1, w), lambda b, s: (0, 0))] * 2,
        out_specs=pl.BlockSpec((ts, w), tile),
        scratch_shapes=[pltpu.VMEM((HG_HEADS, dk, dk), F32), pltpu.VMEM((ts, w), F32)],
        compiler_params=_params(2),
        name="hgrn2",
    )(hq, hf, hi, hg, lb.reshape(1, w), ng.reshape(1, w))


def _attn_kernel(nk, nq, nr, q_ref, kp_ref, kc_ref, vp_ref, vc_ref, o_ref, lse_ref):
    n = pl.program_id(2)
    e = ATT_HEAD_DIM
    i = lax.broadcasted_iota(I32, (nk, 2 * nk), 0)
    j = lax.broadcasted_iota(I32, (nk, 2 * nk), 1)
    band = (j >= i) & (j <= i + nk)
    first_head = lax.broadcasted_iota(I32, (nk, LANES), 1) < e
    zero = jnp.zeros((), q_ref.dtype)
    for r in range(nr):
        kk = jnp.concatenate([kp_ref[0, r], kc_ref[0, r]], axis=0)
        vv = jnp.concatenate([vp_ref[0, r], vc_ref[0, r]], axis=0)
        for b in range(nq):
            valid = band & ((j >= nk) | (n * nq + b > 0))
            rows = slice(b * nk, (b + 1) * nk)
            for c in range(0, ATT_HEADS_PER_GROUP * e, LANES):
                q = q_ref[0, r, rows, c:c + LANES]
                kb = kk[b * nk:(b + 2) * nk, c:c + LANES]
                vb = vv[b * nk:(b + 2) * nk, c:c + LANES]
                outs, lses = [], []
                for keep in (first_head, jnp.logical_not(first_head)):
                    s = lax.dot_general(jnp.where(keep, q, zero), kb, (((1,), (1,)), ((), ())),
                                        preferred_element_type=F32)
                    s = jnp.where(valid, s, -jnp.inf)
                    m = jnp.max(s, axis=-1, keepdims=True)
                    p = jnp.exp(s - m)
                    l = jnp.sum(p, axis=-1, keepdims=True)
                    outs.append(jnp.dot(p.astype(BF16), vb, preferred_element_type=F32) / l)
                    lses.append(m + jnp.log(l))
                o_ref[0, r, rows, c:c + LANES] = jnp.where(first_head, outs[0], outs[1])
                lse_ref[0, r, rows, c:c + LANES] = jnp.where(first_head, lses[0], lses[1])


def _attn_group(q, k, v, g, blocks_per_step):
    window, dil = ATT_GROUPS[g]
    nk = window // dil
    bsz, _, ln, gw = q.shape
    nq = min(blocks_per_step, ln // nk)
    nr = min(blocks_per_step // nq, dil)
    assert ln % (nk * nq) == 0 and dil % nr == 0 and 2 * ATT_HEAD_DIM == LANES
    cur = pl.BlockSpec((1, nr, nq * nk, gw), lambda b, r, n: (b, r, n, 0))
    prev = pl.BlockSpec((1, nr, nk, gw), lambda b, r, n: (b, r, jnp.maximum(n * nq - 1, 0), 0))
    return pl.pallas_call(
        functools.partial(_attn_kernel, nk, nq, nr),
        out_shape=[jax.ShapeDtypeStruct(q.shape, F32)] * 2,
        grid=(bsz, dil // nr, ln // (nk * nq)),
        in_specs=[cur, prev, cur, prev, cur],
        out_specs=[cur, cur],
        compiler_params=_params(3),
        name=f"dilated_attn_g{g}",
    )(q, k, k, v, v)


def _token_major(ref, scr):
    dil, rows = ref.shape[1], ref.shape[2]
    if dil == 1:
        return ref[0, 0]
    n_col = scr.shape[0]
    for r in range(dil):
        for c in range(n_col):
            scr[c, pl.ds(r, rows, stride=dil), :] = ref[0, r, :, c * LANES:(c + 1) * LANES]
    return jnp.concatenate([scr[c] for c in range(n_col)], axis=1)


def _merge_kernel(ya_ref, o0_ref, o1_ref, o2_ref, l0_ref, l1_ref, l2_ref, ga_ref, gb_ref, x_ref,
                  g1_ref, sc2_ref, sh2_ref, g2_ref, n2_ref, wa_ref, wb_ref, wo_ref, wr_ref, wrl_ref,
                  wsg_ref, wsu_ref, wsd_ref, bias_ref, x1_ref, hp_ref, idx_ref, gate_ref, rank_ref,
                  cnt_ref, carry_ref, lg_ref, *scr):
    step = pl.program_id(0)

    @pl.when(step == 0)
    def _():
        carry_ref[...] = jnp.zeros_like(carry_ref)
        lg_ref[...] = jnp.zeros_like(lg_ref)

    _route(lg_ref[...], jnp.where(step > 0, 1.0, 0.0), bias_ref, idx_ref, gate_ref, rank_ref,
           cnt_ref, carry_ref)

    l0, l1, l2 = (_token_major(r, s) for r, s in zip((l0_ref, l1_ref, l2_ref), scr[:3]))
    o0, o1, o2 = (_token_major(r, s) for r, s in zip((o0_ref, o1_ref, o2_ref), scr[3:]))
    m = jnp.maximum(jnp.maximum(l0, l1), l2)
    e0, e1, e2 = jnp.exp(l0 - m), jnp.exp(l1 - m), jnp.exp(l2 - m)
    yb = (e0 * o0 + e1 * o1 + e2 * o2) / (e0 + e1 + e2)
    merged = (_sigmoid(ga_ref[...].astype(F32))
              * jnp.dot(ya_ref[...], wa_ref[...], preferred_element_type=F32)
              + _sigmoid(gb_ref[...].astype(F32))
              * jnp.dot(yb.astype(BF16), wb_ref[...], preferred_element_type=F32))
    x1 = x_ref[...] + g1_ref[0] * jnp.dot(merged.astype(BF16), wo_ref[...],
                                           preferred_element_type=F32)
    h2 = _rms(x1, n2_ref[...]) * (1.0 + sc2_ref[0]) + sh2_ref[0]
    hb = h2.astype(BF16)
    act = (_silu(jnp.dot(hb, wsg_ref[...], preferred_element_type=F32))
           * jnp.dot(hb, wsu_ref[...], preferred_element_type=F32))
    shared = jnp.dot(act.astype(BF16), wsd_ref[...], preferred_element_type=F32)
    x1_ref[...] = x1 + g2_ref[0] * shared
    hp_ref[...] = _pack_halves(h2)
    h_lo = (h2 - hb.astype(F32)).astype(BF16)
    nt = lambda a, b: lax.dot_general(a, b, (((1,), (1,)), ((), ())), preferred_element_type=F32)
    lg_ref[...] = nt(wr_ref[...], hb) + (nt(wr_ref[...], h_lo) + nt(wrl_ref[...], hb))


def _merge(ya, att, ga, gb, x2, gate1, scale2, shift2, gate2, norm2_g, wa, wb, wo, wr_t, wsg, wsu,
           wsd, router_bias, seq, tm):
    t, d = x2.shape
    n_e = wr_t.shape[0]
    wr_hi = wr_t.astype(BF16)
    wr_lo = (wr_t - wr_hi.astype(F32)).astype(BF16)
    n_per = seq // tm
    n_tiles = t // tm
    tile = lambda i: jnp.minimum(i, n_tiles - 1)
    per_b = lambda i: (tile(i) // n_per, 0, 0)
    rows = lambda wdt: pl.BlockSpec((tm, wdt), lambda i: (tile(i), 0))
    full = lambda a: pl.BlockSpec(a.shape, lambda i: (0,) * a.ndim)
    vec = pl.BlockSpec((1, 1, d), per_b)
    (o0, l0), (o1, l1), (o2, l2) = att
    gw = o0.shape[3]
    by_residue = lambda a: pl.BlockSpec((1, a.shape[1], tm // a.shape[1], gw),
                                        lambda i: (tile(i) // n_per, 0, tile(i) % n_per, 0))
    att_in = (o0, o1, o2, l0, l1, l2)
    bias_col = router_bias.reshape(n_e, 1)
    tok = pl.BlockSpec((TOP_K, tm), lambda i: (0, jnp.maximum(i - 1, 0)))
    return pl.pallas_call(
        _merge_kernel,
        out_shape=[jax.ShapeDtypeStruct((t, d), F32),
                   jax.ShapeDtypeStruct((t, d // 2), U32),
                   jax.ShapeDtypeStruct((TOP_K, t), I32), jax.ShapeDtypeStruct((TOP_K, t), F32),
                   jax.ShapeDtypeStruct((TOP_K, t), I32), jax.ShapeDtypeStruct((n_e, LANES), I32)],
        grid=(n_tiles + 1,),
        in_specs=[rows(ya.shape[1])] + [by_residue(a) for a in att_in] + [rows(d)] * 3
        + [vec, vec, vec, vec, pl.BlockSpec((1, d), lambda i: (0, 0))]
        + [full(a) for a in (wa, wb, wo, wr_hi, wr_lo, wsg, wsu, wsd, bias_col)],
        out_specs=[rows(d), rows(d // 2), tok, tok, tok,
                   pl.BlockSpec((n_e, LANES), lambda i: (0, 0))],
        scratch_shapes=[pltpu.VMEM((n_e, 1), F32), pltpu.VMEM((n_e, tm), F32)]
        + [pltpu.VMEM((gw // LANES, tm, LANES), F32)] * 6,
        compiler_params=_params(),
        name="merge_router",
    )(ya, *att_in, ga, gb, x2, gate1, scale2, shift2, gate2,
      norm2_g.reshape(1, d), wa, wb, wo, wr_hi, wr_lo, wsg, wsu, wsd, bias_col)


def _route(logits, live, bias_ref, idx_ref, gate_ref, rank_ref, cnt_ref, carry_ref):
    n_e, tt = logits.shape
    scores = _sigmoid(logits)
    sel = scores + bias_ref[...]
    eio = lax.broadcasted_iota(I32, (n_e, tt), 0)
    picked = jnp.zeros((n_e, tt), F32)
    idxs, vals = [], []
    for _ in range(TOP_K):
        m = jnp.max(sel, axis=0, keepdims=True)
        ik = jnp.min(jnp.where(sel == m, eio, n_e), axis=0, keepdims=True)
        hit = eio == ik
        vals.append(jnp.sum(jnp.where(hit, scores, 0.0), axis=0, keepdims=True))
        sel = jnp.where(hit, -jnp.inf, sel)
        picked = picked + jnp.where(hit, 1.0, 0.0)
        idxs.append(ik)
    denom = vals[0]
    for v in vals[1:]:
        denom = denom + v
    gate_ref[...] = jnp.concatenate([v / denom * ROUTE_SCALE for v in vals], axis=0)
    idx_ref[...] = jnp.concatenate(idxs, axis=0)

    upper = (lax.broadcasted_iota(I32, (tt, tt), 0) <= lax.broadcasted_iota(I32, (tt, tt), 1))
    incl = jnp.dot(picked.astype(BF16), jnp.where(upper, 1.0, 0.0).astype(BF16),
                   preferred_element_type=F32)
    before = incl - picked + carry_ref[...]
    rank_ref[...] = jnp.concatenate(
        [jnp.sum(jnp.where(eio == ik, before, 0.0), axis=0, keepdims=True) for ik in idxs],
        axis=0).astype(I32)
    carry_ref[...] = carry_ref[...] + jnp.sum(picked, axis=1, keepdims=True) * live
    cnt_ref[...] = jnp.broadcast_to(carry_ref[...], cnt_ref.shape).astype(I32)


def _dest_kernel(idx_ref, rank_ref, start_ref, o_ref):
    k, tt = idx_ref.shape
    n_e = start_ref.shape[0]
    eio = lax.broadcasted_iota(I32, (n_e, tt), 0)
    start = start_ref[...]
    rows = [jnp.sum(jnp.where(eio == idx_ref[r:r + 1, :], start, 0), axis=0, keepdims=True)
            for r in range(k)]
    o_ref[...] = jnp.concatenate(rows, axis=0) + rank_ref[...]


def _dest(idx, rank, seg_start, tt):
    k, t = idx.shape
    n_e = seg_start.shape[0]
    tok = pl.BlockSpec((k, tt), lambda i: (0, i))
    return pl.pallas_call(
        _dest_kernel,
        out_shape=jax.ShapeDtypeStruct((k, t), I32),
        grid=(t // tt,),
        in_specs=[tok, tok, pl.BlockSpec((n_e, 1), lambda i: (0, 0))],
        out_specs=tok,
        compiler_params=_params(),
        name="moe_dest",
    )(idx, rank, seg_start.reshape(n_e, 1))


def _sc_mesh():
    return plsc.VectorSubcoreMesh(core_axis_name="core", subcore_axis_name="subcore")


def _sc_scatter_rows(rows, dest, n_out):
    k, t = dest.shape
    w = rows.shape[1]
    mesh = _sc_mesh()
    n_workers = mesh.num_cores * mesh.num_subcores
    win_per_worker = t // (SC_WINDOW * n_workers)
    assert win_per_worker * SC_WINDOW * n_workers == t

    @functools.partial(
        pl.kernel, out_type=jax.ShapeDtypeStruct((n_out, w), rows.dtype), mesh=mesh,
        scratch_types=[pltpu.VMEM((SC_WINDOW, w), rows.dtype)]
        + [pltpu.VMEM((1, SC_WINDOW), I32)] * k + [pltpu.SemaphoreType.DMA],
        name="moe_dispatch_sc")
    def run(rows_hbm, idx_hbm, out_hbm, rows_v, *rest):
        idx_v, sem = rest[:k], rest[k]
        worker = lax.axis_index("subcore") * mesh.num_cores + lax.axis_index("core")

        @pl.loop(0, win_per_worker)
        def _(j):
            t0 = pl.multiple_of((worker * win_per_worker + j) * SC_WINDOW, SC_WINDOW)
            loads = [pltpu.async_copy(rows_hbm.at[pl.ds(t0, SC_WINDOW)], rows_v, sem)]
            loads += [pltpu.async_copy(idx_hbm.at[:, pl.ds(r * t + t0, SC_WINDOW)], idx_v[r], sem)
                      for r in range(k)]
            for c in loads:
                c.wait()
            copies = [pltpu.async_copy(rows_v, out_hbm.at[idx_v[r].at[0]], sem) for r in range(k)]
            for c in copies:
                c.wait()

    return run(rows, dest.reshape(1, k * t))


def _sc_gather_rows(table, dest):
    k, t = dest.shape
    w = table.shape[1]
    mesh = _sc_mesh()
    n_workers = mesh.num_cores * mesh.num_subcores
    win_per_worker = (k * t) // (SC_WINDOW * n_workers)
    assert win_per_worker * SC_WINDOW * n_workers == k * t and win_per_worker % 2 == 0

    @functools.partial(
        pl.kernel, out_type=jax.ShapeDtypeStruct((k * t, w), table.dtype), mesh=mesh,
        scratch_types=[pltpu.VMEM((SC_WINDOW, w), table.dtype), pltpu.VMEM((1, SC_WINDOW), I32),
                       pltpu.VMEM((1, SC_WINDOW), I32), pltpu.SemaphoreType.DMA],
        name="moe_gather_sc")
    def run(table_hbm, idx_hbm, out_hbm, rows_v, idx_a, idx_b, sem):
        worker = lax.axis_index("subcore") * mesh.num_cores + lax.axis_index("core")

        def window(j):
            return pl.ds(pl.multiple_of((worker * win_per_worker + j) * SC_WINDOW, SC_WINDOW),
                         SC_WINDOW)

        def idx_load(j, buf):
            return pltpu.make_async_copy(idx_hbm.at[:, window(j)], buf, sem)

        idx_load(0, idx_a).start()

        @pl.loop(0, win_per_worker, step=2)
        def _(j0):
            for b, (cur, nxt) in enumerate(((idx_a, idx_b), (idx_b, idx_a))):
                j = j0 + b
                idx_load(j, cur).wait()

                @pl.when(j + 1 < win_per_worker)
                def _():
                    idx_load(j + 1, nxt).start()

                pltpu.sync_copy(table_hbm.at[cur.at[0]], rows_v)
                pltpu.sync_copy(rows_v, out_hbm.at[window(j)])

    return run(table, dest.reshape(1, k * t))


def _expert_kernel(start_ref, nblk_ref, xs_ref, wg_ref, wu_ref, wd_ref, ys_ref,
                   xbuf, ybuf, wgb, wub, wdb, wbuf_g, wbuf_u, wbuf_d, sem_in, sem_out, sem_w):
    wbuf = (wbuf_g, wbuf_u, wbuf_d)
    e = pl.program_id(0)
    n_e = pl.num_programs(0)
    nb = nblk_ref[e]
    g0 = start_ref[e] // MOE_BLOCK
    n_used = start_ref[n_e - 1] // MOE_BLOCK + nblk_ref[n_e - 1]
    n_in, n_out = xbuf.shape[0], ybuf.shape[0]

    def rows(g):
        return pl.ds(pl.multiple_of(g * MOE_BLOCK, MOE_BLOCK), MOE_BLOCK)

    def in_copy(g):
        slot = lax.rem(g, n_in)
        return pltpu.make_async_copy(xs_ref.at[rows(g), :], xbuf.at[slot], sem_in.at[slot])

    def out_copy(g):
        slot = lax.rem(g, n_out)
        return pltpu.make_async_copy(ybuf.at[slot], ys_ref.at[rows(g), :], sem_out.at[slot])

    look = n_in - EXPERT_GROUP

    @pl.when(e == 0)
    def _():
        for g in range(look):
            @pl.when(g < n_used)
            def _():
                in_copy(g).start(priority=g % N_DMA_QUEUES)

    n_w = wbuf[0].shape[0]

    def weight_copies(ex):
        slot = lax.rem(ex, n_w)
        return [pltpu.make_async_copy(src.at[ex], buf.at[slot], sem_w.at[slot])
                for src, buf in zip((wg_ref, wu_ref, wd_ref), wbuf)]

    @pl.when(e == 0)
    def _():
        for ex in range(min(n_w, wg_ref.shape[0])):
            for c in weight_copies(ex):
                c.start()

    for c in weight_copies(e):
        c.wait()
    w_slot = lax.rem(e, n_w)

    @pl.when(nb > 0)
    def _():
        wgb[...] = wbuf[0][w_slot].astype(BF16)
        wub[...] = wbuf[1][w_slot].astype(BF16)
        wdb[...] = wbuf[2][w_slot].astype(BF16)

    @pl.when(e + n_w < n_e)
    def _():
        for c in weight_copies(e + n_w):
            c.start()

    @pl.when(nb > 0)
    def _():
        def swiglu(word):
            lo, hi = _unpack_halves(word)
            x = jnp.concatenate([lo.astype(BF16), hi.astype(BF16)], axis=1)
            gate = jnp.dot(x, wgb[...], preferred_element_type=F32)
            up = jnp.dot(x, wub[...], preferred_element_type=F32)
            act = (_silu(gate) * up).astype(BF16)
            return jnp.dot(act, wdb[...], preferred_element_type=F32)

        def process(g, m):
            for i in range(m):
                in_copy(g + i).wait()
            for i in range(m):
                @pl.when(g + look + i < n_used)
                def _():
                    in_copy(g + look + i).start(priority=i % N_DMA_QUEUES)
            y_all = swiglu(jnp.concatenate([xbuf[lax.rem(g + i, n_in)] for i in range(m)], axis=0))
            ys = [y_all[i * MOE_BLOCK:(i + 1) * MOE_BLOCK] for i in range(m)]
            for i in range(m):
                @pl.when(g + i >= n_out)
                def _():
                    out_copy(g + i - n_out).wait()

                ybuf[lax.rem(g + i, n_out)] = _pack_halves(ys[i])
                out_copy(g + i).start(priority=(i + 1) % N_DMA_QUEUES)

        def group_body(p, carry):
            process(g0 + p * EXPERT_GROUP, EXPERT_GROUP)
            return carry

        lax.fori_loop(0, nb // EXPERT_GROUP, group_body, 0)
        for m in range(1, EXPERT_GROUP):
            @pl.when(lax.rem(nb, EXPERT_GROUP) == m)
            def _():
                process(g0 + nb - m, m)

    @pl.when(e == n_e - 1)
    def _():
        for i in range(n_out):
            @pl.when(n_used - 1 - i >= 0)
            def _():
                out_copy(n_used - 1 - i).wait()


def _experts(seg_start, seg_blocks, xs, wg, wu, wd):
    n_slots, half = xs.shape
    n_e, d, de = wg.shape
    n_w = EXPERT_WEIGHT_BUFFERS
    return pl.pallas_call(
        _expert_kernel,
        out_shape=jax.ShapeDtypeStruct((n_slots, half), U32),
        grid_spec=pltpu.PrefetchScalarGridSpec(
            num_scalar_prefetch=2,
            grid=(n_e,),
            in_specs=[pl.BlockSpec(memory_space=pl.ANY)] * 4,
            out_specs=pl.BlockSpec(memory_space=pl.ANY),
            scratch_shapes=[pltpu.VMEM((EXPERT_IN_RING, MOE_BLOCK, half), U32),
                            pltpu.VMEM((EXPERT_OUT_RING, MOE_BLOCK, half), U32),
                            pltpu.VMEM((d, de), BF16), pltpu.VMEM((d, de), BF16),
                            pltpu.VMEM((de, d), BF16),
                            pltpu.VMEM((n_w, d, de), F32), pltpu.VMEM((n_w, d, de), F32),
                            pltpu.VMEM((n_w, de, d), F32),
                            pltpu.SemaphoreType.DMA((EXPERT_IN_RING,)),
                            pltpu.SemaphoreType.DMA((EXPERT_OUT_RING,)),
                            pltpu.SemaphoreType.DMA((n_w,))]),
        compiler_params=_params(),
        name="moe_experts",
    )(seg_start, seg_blocks, xs, wg, wu, wd)


def _combine_kernel(yg_ref, gt_ref, x_ref, g2_ref, fg_ref, o_ref):
    k = yg_ref.shape[0]
    gt = gt_ref[...]
    lo, hi = _unpack_halves(yg_ref[0])
    y_lo, y_hi = lo * gt[:, 0:1], hi * gt[:, 0:1]
    for r in range(1, k):
        lo, hi = _unpack_halves(yg_ref[r])
        y_lo, y_hi = y_lo + lo * gt[:, r:r + 1], y_hi + hi * gt[:, r:r + 1]
    y = jnp.concatenate([y_lo, y_hi], axis=1)
    o_ref[...] = _rms(x_ref[...] + g2_ref[0] * y, fg_ref[...])


def _combine_into_kernel(yg_ref, gt_ref, x_ref, g2_ref, fg_ref, prev_ref, o_ref):
    del prev_ref
    _combine_kernel(yg_ref, gt_ref, x_ref, g2_ref, fg_ref, o_ref)


def _combine(yg, tok0, gates_t, x1s, gate2, final_g, seq, tc, out_so_far=None):
    k, n, half = yg.shape
    t, d = x1s.shape
    assert n % tc == 0 and tok0 % tc == 0
    b0 = tok0 // tc
    args = [yg, gates_t, x1s, gate2, final_g.reshape(1, d)]
    in_specs = [pl.BlockSpec((k, tc, half), lambda i: (0, i, 0)),
                pl.BlockSpec((tc, k), lambda i: (i + b0, 0)),
                pl.BlockSpec((tc, d), lambda i: (i + b0, 0)),
                pl.BlockSpec((1, 1, d), lambda i: (((i + b0) * tc) // seq, 0, 0)),
                pl.BlockSpec((1, d), lambda i: (0, 0))]
    aliases = {}
    kernel = _combine_kernel
    if out_so_far is not None:
        args.append(out_so_far)
        in_specs.append(pl.BlockSpec(memory_space=pl.ANY))
        aliases = {len(args) - 1: 0}
        kernel = _combine_into_kernel
    return pl.pallas_call(
        kernel,
        out_shape=jax.ShapeDtypeStruct((t, d), F32),
        grid=(n // tc,),
        in_specs=in_specs,
        out_specs=pl.BlockSpec((tc, d), lambda i: (i + b0, 0)),
        input_output_aliases=aliases,
        compiler_params=_params(),
        name="moe_combine",
    )(*args)


def _layer(x2, c, bsz, seq, lb_row, ada_w, ada_b, norm1_g, w_in, hg_norm_g, w_branch_a, w_branch_b,
           w_out, norm2_g, w_router, router_bias, w_exp_gate, w_exp_up, w_exp_down, w_sh_gate,
           w_sh_up, w_sh_down, final_g):
    t, d = x2.shape
    n_e = w_router.shape[1]
    mod = _ada(c, ada_w, ada_b).reshape(bsz, 6, 1, d)
    shift1, scale1, gate1, shift2, scale2, gate2 = (mod[:, j] for j in range(6))

    hw = hg_norm_g.shape[0]
    aw = len(ATT_GROUPS) * ATT_HEADS_PER_GROUP * ATT_HEAD_DIM
    flat_segs = [(0, hw, BF16), (hw, hw, F32), (2 * hw, hw, BF16), (3 * hw, hw, BF16),
                 (4 * hw + 3 * aw, d, BF16), (4 * hw + 3 * aw + d, d, BF16)]
    (hq, hf, hi, hg, ga, gb), qkv = _inproj(
        x2, norm1_g, scale1, shift1, w_in.astype(BF16), bsz, seq, flat_segs, 4 * hw,
        tm=IN_PROJ_TILE)

    ya = _hgrn(hq, hf, hi, hg, lb_row, hg_norm_g, bsz, seq, ts=HGRN_TILE)
    att = [_attn_group(*qkv[3 * g:3 * g + 3], g, blocks_per_step=ATT_BLOCKS_PER_STEP)
           for g in range(len(ATT_GROUPS))]

    x1s, hp, idx, gates, rank, cnt = _merge(
        ya, att, ga, gb, x2, gate1, scale2, shift2, gate2, norm2_g, w_branch_a.astype(BF16),
        w_branch_b.astype(BF16), w_out.astype(BF16), w_router.T, w_sh_gate.astype(BF16),
        w_sh_up.astype(BF16), w_sh_down.astype(BF16), router_bias, seq, tm=MERGE_TILE)
    counts = cnt[:, 0]
    padded = (counts + MOE_BLOCK - 1) // MOE_BLOCK * MOE_BLOCK
    seg_start = (jnp.cumsum(padded) - padded).astype(I32)
    n_blocks = -(-(t * TOP_K) // MOE_BLOCK) + n_e
    dest = _dest(idx, rank, seg_start, tt=DEST_TILE)

    xs = _sc_scatter_rows(hp, dest, n_blocks * MOE_BLOCK)
    ys = _experts(seg_start, (padded // MOE_BLOCK).astype(I32), xs, w_exp_gate, w_exp_up,
                  w_exp_down)
    out, n = None, t // COMBINE_PARTS
    for part in range(COMBINE_PARTS):
        yg = _sc_gather_rows(ys, dest[:, part * n:(part + 1) * n]).reshape(TOP_K, n, d // 2)
        out = _combine(yg, part * n, gates.T, x1s, gate2, final_g, seq, tc=COMBINE_TILE,
                       out_so_far=out)
    return out


def kernel(x, c, ada_w, ada_b, norm1_g, w_in, lb_logits, hg_norm_g, w_branch_a, w_branch_b, w_out,
           norm2_g, w_router, router_bias, w_exp_gate, w_exp_up, w_exp_down, w_sh_gate, w_sh_up,
           w_sh_down, final_g):
    bsz, seq, d = x.shape
    depth = ada_w.shape[0]
    assert depth == 1, "the last layer's kernels also apply the final norm"
    lb_table = jnp.cumsum(jax.nn.softmax(lb_logits.astype(F32), axis=0), axis=0)
    out = _layer(x.reshape(bsz * seq, d), c, bsz, seq, lb_table[0], ada_w[0], ada_b[0], norm1_g[0],
                 w_in[0], hg_norm_g[0], w_branch_a[0], w_branch_b[0], w_out[0], norm2_g[0],
                 w_router[0], router_bias[0], w_exp_gate[0], w_exp_up[0], w_exp_down[0],
                 w_sh_gate[0], w_sh_up[0], w_sh_down[0], final_g)
    return out.reshape(bsz, seq, d)
```

```python
import functools

import jax
import jax.numpy as jnp
from jax import lax
from jax.experimental import pallas as pl
from jax.experimental.pallas import tpu as pltpu
from jax.experimental.pallas import tpu_sc as plsc

F32 = jnp.float32
BF16 = jnp.bfloat16
I32 = jnp.int32
U32 = jnp.uint32
HIGHEST = lax.Precision.HIGHEST

HG_HEADS = 4
HG_BLOCK = 16
HG_CHUNK = 32
HG_MILD_DECAY = -80.0
ATT_GROUPS = ((128, 1), (512, 4), (2048, 16))
ATT_HEADS_PER_GROUP = 4
ATT_HEAD_DIM = 64
TOP_K = 8
ROUTE_SCALE = 2.5
MOE_BLOCK = 256
RMS_EPS = 1e-6
N_DMA_QUEUES = 2
SC_WINDOW = 128
COMBINE_PARTS = 16
EXPERT_WEIGHT_BUFFERS = 3
EXPERT_GROUP = 4
EXPERT_IN_RING = 8
EXPERT_OUT_RING = 6

LANES = 128
VMEM_LIMIT_BYTES = 56 * 1024 * 1024

IN_PROJ_TILE = 1024
HGRN_TILE = 512
ATT_BLOCKS_PER_STEP = 16
MERGE_TILE = 512
DEST_TILE = 2048
COMBINE_TILE = 512


def _sigmoid(x):
    return 1.0 / (1.0 + jnp.exp(-x))


def _silu(x):
    return x * _sigmoid(x)


def _rms(x, g):
    return x * lax.rsqrt(jnp.mean(x * x, axis=-1, keepdims=True) + RMS_EPS) * g


def _pack_halves(x):
    n = x.shape[1] // 2
    bits = lax.bitcast_convert_type(x.astype(BF16).astype(F32), U32)
    return (bits[:, :n] >> 16) | (bits[:, n:] & jnp.uint32(0xFFFF0000))


def _unpack_halves(word):
    lo = lax.bitcast_convert_type(word << 16, F32)
    hi = lax.bitcast_convert_type(word & jnp.uint32(0xFFFF0000), F32)
    return lo, hi


def _params(n_axes=1):
    return pltpu.CompilerParams(
        dimension_semantics=("arbitrary",) * n_axes, vmem_limit_bytes=VMEM_LIMIT_BYTES)


def _ada_kernel(c_ref, w_ref, b_ref, o_ref):
    sc = _silu(c_ref[...])
    o_ref[...] = jnp.dot(sc, w_ref[...], preferred_element_type=F32, precision=HIGHEST) + b_ref[...]


def _ada(c, w, b):
    bsz, d = c.shape
    n = w.shape[1]
    return pl.pallas_call(
        _ada_kernel,
        out_shape=jax.ShapeDtypeStruct((bsz, n), F32),
        grid=(n // d,),
        in_specs=[pl.BlockSpec((bsz, d), lambda j: (0, 0)),
                  pl.BlockSpec((d, d), lambda j: (0, j)),
                  pl.BlockSpec((1, d), lambda j: (0, j))],
        out_specs=pl.BlockSpec((bsz, d), lambda j: (0, j)),
        compiler_params=_params(),
        name="ada_mod",
    )(c, w, b.reshape(1, n))


def _inproj_kernel(n_flat, flat_ranges, att_c0, x_ref, g_ref, sc_ref, sh_ref, w_ref, *refs):
    flat_refs, att_refs, scr = refs[:n_flat], refs[n_flat:-1], refs[-1]
    tm = x_ref.shape[0]
    h = _rms(x_ref[...], g_ref[...]) * (1.0 + sc_ref[0]) + sh_ref[0]
    hb = h.astype(BF16)
    for (c0, c1), o_ref in zip(flat_ranges, flat_refs):
        o_ref[...] = jnp.dot(hb, w_ref[:, c0:c1], preferred_element_type=F32).astype(o_ref.dtype)
    gw = ATT_HEADS_PER_GROUP * ATT_HEAD_DIM
    n_groups = len(ATT_GROUPS)
    for part in range(3):
        c0 = att_c0 + part * n_groups * gw
        res = jnp.dot(hb, w_ref[:, c0:c0 + n_groups * gw], preferred_element_type=F32)
        if part == 0:
            res = res * (ATT_HEAD_DIM ** -0.5)
        for g, (_, dil) in enumerate(ATT_GROUPS):
            o_ref = att_refs[g * 3 + part]
            sub = res[:, g * gw:(g + 1) * gw]
            if dil == 1:
                o_ref[0, 0] = sub.astype(BF16)
            else:
                for c in range(gw // LANES):
                    scr[c] = sub[:, c * LANES:(c + 1) * LANES]
                for r in range(dil):
                    o_ref[0, r] = jnp.concatenate(
                        [scr[c, pl.ds(r, tm // dil, stride=dil), :] for c in range(gw // LANES)],
                        axis=1).astype(BF16)


def _inproj(x2, g, scale, shift, w_bf16, bsz, seq, flat_segs, att_c0, tm):
    t, d = x2.shape
    gw = ATT_HEADS_PER_GROUP * ATT_HEAD_DIM
    n_per = seq // tm
    per_b = lambda i: (i // n_per, 0, 0)
    att_shapes, att_specs = [], []
    for _, dil in ATT_GROUPS:
        for _ in range(3):
            att_shapes.append(jax.ShapeDtypeStruct((bsz, dil, seq // dil, gw), BF16))
            att_specs.append(pl.BlockSpec((1, dil, tm // dil, gw),
                                          lambda i: (i // n_per, 0, i % n_per, 0)))
    outs = pl.pallas_call(
        functools.partial(_inproj_kernel, len(flat_segs),
                          tuple((c0, c0 + wdt) for c0, wdt, _ in flat_segs), att_c0),
        out_shape=[jax.ShapeDtypeStruct((t, wdt), dt) for _, wdt, dt in flat_segs] + att_shapes,
        grid=(t // tm,),
        in_specs=[pl.BlockSpec((tm, d), lambda i: (i, 0)),
                  pl.BlockSpec((1, d), lambda i: (0, 0)),
                  pl.BlockSpec((1, 1, d), per_b),
                  pl.BlockSpec((1, 1, d), per_b),
                  pl.BlockSpec(w_bf16.shape, lambda i: (0, 0), pipeline_mode=pl.Buffered(1))],
        out_specs=[pl.BlockSpec((tm, wdt), lambda i: (i, 0)) for _, wdt, _ in flat_segs]
        + att_specs,
        scratch_shapes=[pltpu.VMEM((gw // LANES, tm, LANES), F32)],
        compiler_params=_params(),
        name="in_proj",
    )(x2, g.reshape(1, d), scale, shift, w_bf16)
    return outs[:len(flat_segs)], outs[len(flat_segs):]


def _hgrn_kernel(ts, q_ref, f_ref, v_ref, gt_ref, lb_ref, ng_ref, o_ref, st_ref, b_ref):
    dk = q_ref.shape[1] // HG_HEADS
    n_chunks = ts // HG_CHUNK
    n_blk = HG_CHUNK // HG_BLOCK

    @pl.when(pl.program_id(1) == 0)
    def _():
        st_ref[...] = jnp.zeros_like(st_ref)

    row = lax.broadcasted_iota(I32, (LANES, LANES), 0)
    col = lax.broadcasted_iota(I32, (LANES, LANES), 1)
    same_chunk = (row // HG_CHUNK) == (col // HG_CHUNK)
    cum_mat = jnp.where(same_chunk & (col <= row), 1.0, 0.0).astype(BF16)

    def chunk_cumsum(x):
        out = []
        for r0 in range(0, ts, LANES):
            rest = x[r0:r0 + LANES]
            acc = None
            for _ in range(3):
                term = rest.astype(BF16)
                part = jnp.dot(cum_mat, term, preferred_element_type=F32)
                acc = part if acc is None else acc + part
                rest = rest - term.astype(F32)
            out.append(acc)
        return jnp.concatenate(out, axis=0)

    def forget(cs):
        lb = lb_ref[:, cs]
        return lb + (1.0 - lb) * _sigmoid(f_ref[:, cs])

    b_min = None
    for h in range(HG_HEADS):
        cs = slice(h * dk, (h + 1) * dk)
        b = chunk_cumsum(jnp.log(forget(cs)))
        b_ref[:, cs] = b
        m = jnp.min(b)
        b_min = m if b_min is None else jnp.minimum(b_min, m)
    mild = b_min >= HG_MILD_DECAY

    def finish(h, o, st):
        cs = slice(h * dk, (h + 1) * dk)
        st_ref[h] = st
        y = _rms(o, ng_ref[:, cs]) * _silu(gt_ref[:, cs].astype(F32))
        o_ref[:, cs] = y.astype(o_ref.dtype)

    @pl.when(mild)
    def _():
        span = 2 * HG_CHUNK
        causal = (lax.broadcasted_iota(I32, (span, span), 0)
                  >= lax.broadcasted_iota(I32, (span, span), 1))
        nt = lambda x, y: lax.dot_general(x, y, (((1,), (1,)), ((), ())),
                                          preferred_element_type=F32)
        for h in range(HG_HEADS):
            cs = slice(h * dk, (h + 1) * dk)
            v = v_ref[:, cs]
            b = b_ref[:, cs]
            q = q_ref[:, cs].astype(F32)
            k = 1.0 - forget(cs)
            st = st_ref[h]
            o_rows = []
            for r0 in range(0, ts, span):
                sl = slice(r0, r0 + span)
                b_first, b_second = b[r0:r0 + HG_CHUNK], b[r0 + HG_CHUNK:r0 + span]
                end_first = b_first[HG_CHUNK - 1:HG_CHUNK]
                end_second = b_second[HG_CHUNK - 1:HG_CHUNK]
                e = jnp.exp(jnp.concatenate([b_first - end_first, b_second], axis=0))
                qe = (q[sl] * e).astype(BF16)
                ke = k[sl] / e
                a = jnp.where(causal, nt(qe, ke.astype(BF16)), 0.0).astype(BF16)
                st_in = (st * jnp.exp(end_first)).astype(BF16)
                o_rows.append(jnp.dot(a, v[sl], preferred_element_type=F32) + nt(qe, st_in))
                kend = (ke * jnp.exp(end_second)).astype(BF16)
                vt = v[sl].astype(F32).T.astype(BF16)
                st = (st * jnp.exp(end_first + end_second)
                      + jnp.dot(vt, kend, preferred_element_type=F32))
            finish(h, jnp.concatenate(o_rows, axis=0), st)

    @pl.when(jnp.logical_not(mild))
    def _():
        _hgrn_steep(ts, dk, n_chunks, n_blk, q_ref, v_ref, b_ref, st_ref, forget, finish)


def _hgrn_steep(ts, dk, n_chunks, n_blk, q_ref, v_ref, b_ref, st_ref, forget, finish):
    t_in_blk = lax.broadcasted_iota(I32, (ts, dk), 0) % HG_BLOCK

    for h in range(HG_HEADS):
        cs = slice(h * dk, (h + 1) * dk)
        q = q_ref[:, cs].astype(F32)
        v = v_ref[:, cs].astype(F32)
        k = 1.0 - forget(cs)
        b = b_ref[:, cs]

        o = jnp.sum(q * k, axis=-1, keepdims=True) * v
        for d in range(1, HG_BLOCK):
            k_d = pltpu.roll(k, d, axis=0)
            b_d = pltpu.roll(b, d, axis=0)
            v_d = pltpu.roll(v, d, axis=0)
            w = jnp.sum(q * k_d * jnp.exp(jnp.minimum(b - b_d, 0.0)), axis=-1, keepdims=True)
            o = o + jnp.where(t_in_blk >= d, w * v_d, 0.0)

        st = st_ref[h]
        o_rows = []
        for c in range(n_chunks):
            r0 = c * HG_CHUNK
            bc = b[r0:r0 + HG_CHUNK]
            qc = q[r0:r0 + HG_CHUNK]
            kc = k[r0:r0 + HG_CHUNK]
            vc = v[r0:r0 + HG_CHUNK].astype(BF16)
            st_b = st.astype(BF16)
            for i in range(n_blk):
                i0 = i * HG_BLOCK
                if i == 0:
                    qt = qc[:HG_BLOCK] * jnp.exp(bc[:HG_BLOCK])
                    qs = qt
                else:
                    ref_row = bc[i0 - 1:i0]
                    qt = qc[i0:i0 + HG_BLOCK] * jnp.exp(bc[i0:i0 + HG_BLOCK] - ref_row)
                    qs = qt * jnp.exp(ref_row)
                oi = lax.dot_general(qs.astype(BF16), st_b, (((1,), (1,)), ((), ())),
                                     preferred_element_type=F32)
                if i > 0:
                    kh = kc[:i0] * jnp.exp(ref_row - bc[:i0])
                    a = lax.dot_general(qt.astype(BF16), kh.astype(BF16), (((1,), (1,)), ((), ())),
                                        preferred_element_type=F32)
                    oi = oi + jnp.dot(a.astype(BF16), vc[:i0], preferred_element_type=F32)
                o_rows.append(oi)
            b_end = bc[HG_CHUNK - 1:HG_CHUNK]
            kend = kc * jnp.exp(b_end - bc)
            vt = v[r0:r0 + HG_CHUNK].T.astype(BF16)
            st = st * jnp.exp(b_end) + jnp.dot(vt, kend.astype(BF16), preferred_element_type=F32)
        finish(h, o + jnp.concatenate(o_rows, axis=0), st)


def _hgrn(hq, hf, hi, hg, lb, ng, bsz, seq, ts):
    t, w = hq.shape
    dk = w // HG_HEADS
    n_s = seq // ts
    tile = lambda b, s: (b * n_s + s, 0)
    return pl.pallas_call(
        functools.partial(_hgrn_kernel, ts),
        out_shape=jax.ShapeDtypeStruct((t, w), BF16),
        grid=(bsz, n_s),
        in_specs=[pl.BlockSpec((ts, w), tile)] * 4
        + [pl.BlockSpec((1, w), lambda b, s: (0, 0))] * 2,
        out_specs=pl.BlockSpec((ts, w), tile),
        scratch_shapes=[pltpu.VMEM((HG_HEADS, dk, dk), F32), pltpu.VMEM((ts, w), F32)],
        compiler_params=_params(2),
        name="hgrn2",
    )(hq, hf, hi, hg, lb.reshape(1, w), ng.reshape(1, w))


def _attn_kernel(nk, nq, nr, q_ref, kp_ref, kc_ref, vp_ref, vc_ref, o_ref, lse_ref):
    n = pl.program_id(2)
    e = ATT_HEAD_DIM
    i = lax.broadcasted_iota(I32, (nk, 2 * nk), 0)
    j = lax.broadcasted_iota(I32, (nk, 2 * nk), 1)
    band = (j >= i) & (j <= i + nk)
    first_head = lax.broadcasted_iota(I32, (nk, LANES), 1) < e
    zero = jnp.zeros((), q_ref.dtype)
    for r in range(nr):
        kk = jnp.concatenate([kp_ref[0, r], kc_ref[0, r]], axis=0)
        vv = jnp.concatenate([vp_ref[0, r], vc_ref[0, r]], axis=0)
        for b in range(nq):
            valid = band & ((j >= nk) | (n * nq + b > 0))
            rows = slice(b * nk, (b + 1) * nk)
            for c in range(0, ATT_HEADS_PER_GROUP * e, LANES):
                q = q_ref[0, r, rows, c:c + LANES]
                kb = kk[b * nk:(b + 2) * nk, c:c + LANES]
                vb = vv[b * nk:(b + 2) * nk, c:c + LANES]
                outs, lses = [], []
                for keep in (first_head, jnp.logical_not(first_head)):
                    s = lax.dot_general(jnp.where(keep, q, zero), kb, (((1,), (1,)), ((), ())),
                                        preferred_element_type=F32)
                    s = jnp.where(valid, s, -jnp.inf)
                    m = jnp.max(s, axis=-1, keepdims=True)
                    p = jnp.exp(s - m)
                    l = jnp.sum(p, axis=-1, keepdims=True)
                    outs.append(jnp.dot(p.astype(BF16), vb, preferred_element_type=F32) / l)
                    lses.append(m + jnp.log(l))
                o_ref[0, r, rows, c:c + LANES] = jnp.where(first_head, outs[0], outs[1])
                lse_ref[0, r, rows, c:c + LANES] = jnp.where(first_head, lses[0], lses[1])


def _attn_group(q, k, v, g, blocks_per_step):
    window, dil = ATT_GROUPS[g]
    nk = window // dil
    bsz, _, ln, gw = q.shape
    nq = min(blocks_per_step, ln // nk)
    nr = min(blocks_per_step // nq, dil)
    assert ln % (nk * nq) == 0 and dil % nr == 0 and 2 * ATT_HEAD_DIM == LANES
    cur = pl.BlockSpec((1, nr, nq * nk, gw), lambda b, r, n: (b, r, n, 0))
    prev = pl.BlockSpec((1, nr, nk, gw), lambda b, r, n: (b, r, jnp.maximum(n * nq - 1, 0), 0))
    return pl.pallas_call(
        functools.partial(_attn_kernel, nk, nq, nr),
        out_shape=[jax.ShapeDtypeStruct(q.shape, F32)] * 2,
        grid=(bsz, dil // nr, ln // (nk * nq)),
        in_specs=[cur, prev, cur, prev, cur],
        out_specs=[cur, cur],
        compiler_params=_params(3),
        name=f"dilated_attn_g{g}",
    )(q, k, k, v, v)


def _token_major(ref, scr):
    dil, rows = ref.shape[1], ref.shape[2]
    if dil == 1:
        return ref[0, 0]
    n_col = scr.shape[0]
    for r in range(dil):
        for c in range(n_col):
            scr[c, pl.ds(r, rows, stride=dil), :] = ref[0, r, :, c * LANES:(c + 1) * LANES]
    return jnp.concatenate([scr[c] for c in range(n_col)], axis=1)


def _merge_kernel(ya_ref, o0_ref, o1_ref, o2_ref, l0_ref, l1_ref, l2_ref, ga_ref, gb_ref, x_ref,
                  g1_ref, sc2_ref, sh2_ref, g2_ref, n2_ref, wa_ref, wb_ref, wo_ref, wr_ref, wrl_ref,
                  wsg_ref, wsu_ref, wsd_ref, bias_ref, x1_ref, hp_ref, idx_ref, gate_ref, rank_ref,
                  cnt_ref, carry_ref, lg_ref, *scr):
    step = pl.program_id(0)

    @pl.when(step == 0)
    def _():
        carry_ref[...] = jnp.zeros_like(carry_ref)
        lg_ref[...] = jnp.zeros_like(lg_ref)

    _route(lg_ref[...], jnp.where(step > 0, 1.0, 0.0), bias_ref, idx_ref, gate_ref, rank_ref,
           cnt_ref, carry_ref)

    l0, l1, l2 = (_token_major(r, s) for r, s in zip((l0_ref, l1_ref, l2_ref), scr[:3]))
    o0, o1, o2 = (_token_major(r, s) for r, s in zip((o0_ref, o1_ref, o2_ref), scr[3:]))
    m = jnp.maximum(jnp.maximum(l0, l1), l2)
    e0, e1, e2 = jnp.exp(l0 - m), jnp.exp(l1 - m), jnp.exp(l2 - m)
    yb = (e0 * o0 + e1 * o1 + e2 * o2) / (e0 + e1 + e2)
    merged = (_sigmoid(ga_ref[...].astype(F32))
              * jnp.dot(ya_ref[...], wa_ref[...], preferred_element_type=F32)
              + _sigmoid(gb_ref[...].astype(F32))
              * jnp.dot(yb.astype(BF16), wb_ref[...], preferred_element_type=F32))
    x1 = x_ref[...] + g1_ref[0] * jnp.dot(merged.astype(BF16), wo_ref[...],
                                           preferred_element_type=F32)
    h2 = _rms(x1, n2_ref[...]) * (1.0 + sc2_ref[0]) + sh2_ref[0]
    hb = h2.astype(BF16)
    act = (_silu(jnp.dot(hb, wsg_ref[...], preferred_element_type=F32))
           * jnp.dot(hb, wsu_ref[...], preferred_element_type=F32))
    shared = jnp.dot(act.astype(BF16), wsd_ref[...], preferred_element_type=F32)
    x1_ref[...] = x1 + g2_ref[0] * shared
    hp_ref[...] = _pack_halves(h2)
    h_lo = (h2 - hb.astype(F32)).astype(BF16)
    nt = lambda a, b: lax.dot_general(a, b, (((1,), (1,)), ((), ())), preferred_element_type=F32)
    lg_ref[...] = nt(wr_ref[...], hb) + (nt(wr_ref[...], h_lo) + nt(wrl_ref[...], hb))


def _merge(ya, att, ga, gb, x2, gate1, scale2, shift2, gate2, norm2_g, wa, wb, wo, wr_t, wsg, wsu,
           wsd, router_bias, seq, tm):
    t, d = x2.shape
    n_e = wr_t.shape[0]
    wr_hi = wr_t.astype(BF16)
    wr_lo = (wr_t - wr_hi.astype(F32)).astype(BF16)
    n_per = seq // tm
    n_tiles = t // tm
    tile = lambda i: jnp.minimum(i, n_tiles - 1)
    per_b = lambda i: (tile(i) // n_per, 0, 0)
    rows = lambda wdt: pl.BlockSpec((tm, wdt), lambda i: (tile(i), 0))
    full = lambda a: pl.BlockSpec(a.shape, lambda i: (0,) * a.ndim)
    vec = pl.BlockSpec((1, 1, d), per_b)
    (o0, l0), (o1, l1), (o2, l2) = att
    gw = o0.shape[3]
    by_residue = lambda a: pl.BlockSpec((1, a.shape[1], tm // a.shape[1], gw),
                                        lambda i: (tile(i) // n_per, 0, tile(i) % n_per, 0))
    att_in = (o0, o1, o2, l0, l1, l2)
    bias_col = router_bias.reshape(n_e, 1)
    tok = pl.BlockSpec((TOP_K, tm), lambda i: (0, jnp.maximum(i - 1, 0)))
    return pl.pallas_call(
        _merge_kernel,
        out_shape=[jax.ShapeDtypeStruct((t, d), F32),
                   jax.ShapeDtypeStruct((t, d // 2), U32),
                   jax.ShapeDtypeStruct((TOP_K, t), I32), jax.ShapeDtypeStruct((TOP_K, t), F32),
                   jax.ShapeDtypeStruct((TOP_K, t), I32), jax.ShapeDtypeStruct((n_e, LANES), I32)],
        grid=(n_tiles + 1,),
        in_specs=[rows(ya.shape[1])] + [by_residue(a) for a in att_in] + [rows(d)] * 3
        + [vec, vec, vec, vec, pl.BlockSpec((1, d), lambda i: (0, 0))]
        + [full(a) for a in (wa, wb, wo, wr_hi, wr_lo, wsg, wsu, wsd, bias_col)],
        out_specs=[rows(d), rows(d // 2), tok, tok, tok,
                   pl.BlockSpec((n_e, LANES), lambda i: (0, 0))],
        scratch_shapes=[pltpu.VMEM((n_e, 1), F32), pltpu.VMEM((n_e, tm), F32)]
        + [pltpu.VMEM((gw // LANES, tm, LANES), F32)] * 6,
        compiler_params=_params(),
        name="merge_router",
    )(ya, *att_in, ga, gb, x2, gate1, scale2, shift2, gate2,
      norm2_g.reshape(1, d), wa, wb, wo, wr_hi, wr_lo, wsg, wsu, wsd, bias_col)


def _route(logits, live, bias_ref, idx_ref, gate_ref, rank_ref, cnt_ref, carry_ref):
    n_e, tt = logits.shape
    scores = _sigmoid(logits)
    sel = scores + bias_ref[...]
    eio = lax.broadcasted_iota(I32, (n_e, tt), 0)
    picked = jnp.zeros((n_e, tt), F32)
    idxs, vals = [], []
    for _ in range(TOP_K):
        m = jnp.max(sel, axis=0, keepdims=True)
        ik = jnp.min(jnp.where(sel == m, eio, n_e), axis=0, keepdims=True)
        hit = eio == ik
        vals.append(jnp.sum(jnp.where(hit, scores, 0.0), axis=0, keepdims=True))
        sel = jnp.where(hit, -jnp.inf, sel)
        picked = picked + jnp.where(hit, 1.0, 0.0)
        idxs.append(ik)
    denom = vals[0]
    for v in vals[1:]:
        denom = denom + v
    gate_ref[...] = jnp.concatenate([v / denom * ROUTE_SCALE for v in vals], axis=0)
    idx_ref[...] = jnp.concatenate(idxs, axis=0)

    upper = (lax.broadcasted_iota(I32, (tt, tt), 0) <= lax.broadcasted_iota(I32, (tt, tt), 1))
    incl = jnp.dot(picked.astype(BF16), jnp.where(upper, 1.0, 0.0).astype(BF16),
                   preferred_element_type=F32)
    before = incl - picked + carry_ref[...]
    rank_ref[...] = jnp.concatenate(
        [jnp.sum(jnp.where(eio == ik, before, 0.0), axis=0, keepdims=True) for ik in idxs],
        axis=0).astype(I32)
    carry_ref[...] = carry_ref[...] + jnp.sum(picked, axis=1, keepdims=True) * live
    cnt_ref[...] = jnp.broadcast_to(carry_ref[...], cnt_ref.shape).astype(I32)


def _dest_kernel(idx_ref, rank_ref, start_ref, o_ref):
    k, tt = idx_ref.shape
    n_e = start_ref.shape[0]
    eio = lax.broadcasted_iota(I32, (n_e, tt), 0)
    start = start_ref[...]
    rows = [jnp.sum(jnp.where(eio == idx_ref[r:r + 1, :], start, 0), axis=0, keepdims=True)
            for r in range(k)]
    o_ref[...] = jnp.concatenate(rows, axis=0) + rank_ref[...]


def _dest(idx, rank, seg_start, tt):
    k, t = idx.shape
    n_e = seg_start.shape[0]
    tok = pl.BlockSpec((k, tt), lambda i: (0, i))
    return pl.pallas_call(
        _dest_kernel,
        out_shape=jax.ShapeDtypeStruct((k, t), I32),
        grid=(t // tt,),
        in_specs=[tok, tok, pl.BlockSpec((n_e, 1), lambda i: (0, 0))],
        out_specs=tok,
        compiler_params=_params(),
        name="moe_dest",
    )(idx, rank, seg_start.reshape(n_e, 1))


def _sc_mesh():
    return plsc.VectorSubcoreMesh(core_axis_name="core", subcore_axis_name="subcore")


def _sc_scatter_rows(rows, dest, n_out):
    k, t = dest.shape
    w = rows.shape[1]
    mesh = _sc_mesh()
    n_workers = mesh.num_cores * mesh.num_subcores
    win_per_worker = t // (SC_WINDOW * n_workers)
    assert win_per_worker * SC_WINDOW * n_workers == t

    @functools.partial(
        pl.kernel, out_type=jax.ShapeDtypeStruct((n_out, w), rows.dtype), mesh=mesh,
        scratch_types=[pltpu.VMEM((SC_WINDOW, w), rows.dtype)]
        + [pltpu.VMEM((1, SC_WINDOW), I32)] * k + [pltpu.SemaphoreType.DMA],
        name="moe_dispatch_sc")
    def run(rows_hbm, idx_hbm, out_hbm, rows_v, *rest):
        idx_v, sem = rest[:k], rest[k]
        worker = lax.axis_index("subcore") * mesh.num_cores + lax.axis_index("core")

        @pl.loop(0, win_per_worker)
        def _(j):
            t0 = pl.multiple_of((worker * win_per_worker + j) * SC_WINDOW, SC_WINDOW)
            loads = [pltpu.async_copy(rows_hbm.at[pl.ds(t0, SC_WINDOW)], rows_v, sem)]
            loads += [pltpu.async_copy(idx_hbm.at[:, pl.ds(r * t + t0, SC_WINDOW)], idx_v[r], sem)
                      for r in range(k)]
            for c in loads:
                c.wait()
            copies = [pltpu.async_copy(rows_v, out_hbm.at[idx_v[r].at[0]], sem) for r in range(k)]
            for c in copies:
                c.wait()

    return run(rows, dest.reshape(1, k * t))


def _sc_gather_rows(table, dest):
    k, t = dest.shape
    w = table.shape[1]
    mesh = _sc_mesh()
    n_workers = mesh.num_cores * mesh.num_subcores
    win_per_worker = (k * t) // (SC_WINDOW * n_workers)
    assert win_per_worker * SC_WINDOW * n_workers == k * t and win_per_worker % 2 == 0

    @functools.partial(
        pl.kernel, out_type=jax.ShapeDtypeStruct((k * t, w), table.dtype), mesh=mesh,
        scratch_types=[pltpu.VMEM((SC_WINDOW, w), table.dtype), pltpu.VMEM((1, SC_WINDOW), I32),
                       pltpu.VMEM((1, SC_WINDOW), I32), pltpu.SemaphoreType.DMA],
        name="moe_gather_sc")
    def run(table_hbm, idx_hbm, out_hbm, rows_v, idx_a, idx_b, sem):
        worker = lax.axis_index("subcore") * mesh.num_cores + lax.axis_index("core")

        def window(j):
            return pl.ds(pl.multiple_of((worker * win_per_worker + j) * SC_WINDOW, SC_WINDOW),
                         SC_WINDOW)

        def idx_load(j, buf):
            return pltpu.make_async_copy(idx_hbm.at[:, window(j)], buf, sem)

        idx_load(0, idx_a).start()

        @pl.loop(0, win_per_worker, step=2)
        def _(j0):
            for b, (cur, nxt) in enumerate(((idx_a, idx_b), (idx_b, idx_a))):
                j = j0 + b
                idx_load(j, cur).wait()

                @pl.when(j + 1 < win_per_worker)
                def _():
                    idx_load(j + 1, nxt).start()

                pltpu.sync_copy(table_hbm.at[cur.at[0]], rows_v)
                pltpu.sync_copy(rows_v, out_hbm.at[window(j)])

    return run(table, dest.reshape(1, k * t))


def _expert_kernel(start_ref, nblk_ref, xs_ref, wg_ref, wu_ref, wd_ref, ys_ref,
                   xbuf, ybuf, wgb, wub, wdb, wbuf_g, wbuf_u, wbuf_d, sem_in, sem_out, sem_w):
    wbuf = (wbuf_g, wbuf_u, wbuf_d)
    e = pl.program_id(0)
    n_e = pl.num_programs(0)
    nb = nblk_ref[e]
    g0 = start_ref[e] // MOE_BLOCK
    n_used = start_ref[n_e - 1] // MOE_BLOCK + nblk_ref[n_e - 1]
    n_in, n_out = xbuf.shape[0], ybuf.shape[0]

    def rows(g):
        return pl.ds(pl.multiple_of(g * MOE_BLOCK, MOE_BLOCK), MOE_BLOCK)

    def in_copy(g):
        slot = lax.rem(g, n_in)
        return pltpu.make_async_copy(xs_ref.at[rows(g), :], xbuf.at[slot], sem_in.at[slot])

    def out_copy(g):
        slot = lax.rem(g, n_out)
        return pltpu.make_async_copy(ybuf.at[slot], ys_ref.at[rows(g), :], sem_out.at[slot])

    look = n_in - EXPERT_GROUP

    @pl.when(e == 0)
    def _():
        for g in range(look):
            @pl.when(g < n_used)
            def _():
                in_copy(g).start(priority=g % N_DMA_QUEUES)

    n_w = wbuf[0].shape[0]

    def weight_copies(ex):
        slot = lax.rem(ex, n_w)
        return [pltpu.make_async_copy(src.at[ex], buf.at[slot], sem_w.at[slot])
                for src, buf in zip((wg_ref, wu_ref, wd_ref), wbuf)]

    @pl.when(e == 0)
    def _():
        for ex in range(min(n_w, wg_ref.shape[0])):
            for c in weight_copies(ex):
                c.start()

    for c in weight_copies(e):
        c.wait()
    w_slot = lax.rem(e, n_w)

    @pl.when(nb > 0)
    def _():
        wgb[...] = wbuf[0][w_slot].astype(BF16)
        wub[...] = wbuf[1][w_slot].astype(BF16)
        wdb[...] = wbuf[2][w_slot].astype(BF16)

    @pl.when(e + n_w < n_e)
    def _():
        for c in weight_copies(e + n_w):
            c.start()

    @pl.when(nb > 0)
    def _():
        def swiglu(word):
            lo, hi = _unpack_halves(word)
            x = jnp.concatenate([lo.astype(BF16), hi.astype(BF16)], axis=1)
            gate = jnp.dot(x, wgb[...], preferred_element_type=F32)
            up = jnp.dot(x, wub[...], preferred_element_type=F32)
            act = (_silu(gate) * up).astype(BF16)
            return jnp.dot(act, wdb[...], preferred_element_type=F32)

        def process(g, m):
            for i in range(m):
                in_copy(g + i).wait()
            for i in range(m):
                @pl.when(g + look + i < n_used)
                def _():
                    in_copy(g + look + i).start(priority=i % N_DMA_QUEUES)
            y_all = swiglu(jnp.concatenate([xbuf[lax.rem(g + i, n_in)] for i in range(m)], axis=0))
            ys = [y_all[i * MOE_BLOCK:(i + 1) * MOE_BLOCK] for i in range(m)]
            for i in range(m):
                @pl.when(g + i >= n_out)
                def _():
                    out_copy(g + i - n_out).wait()

                ybuf[lax.rem(g + i, n_out)] = _pack_halves(ys[i])
                out_copy(g + i).start(priority=(i + 1) % N_DMA_QUEUES)

        def group_body(p, carry):
            process(g0 + p * EXPERT_GROUP, EXPERT_GROUP)
            return carry

        lax.fori_loop(0, nb // EXPERT_GROUP, group_body, 0)
        for m in range(1, EXPERT_GROUP):
            @pl.when(lax.rem(nb, EXPERT_GROUP) == m)
            def _():
                process(g0 + nb - m, m)

    @pl.when(e == n_e - 1)
    def _():
        for i in range(n_out):
            @pl.when(n_used - 1 - i >= 0)
            def _():
                out_copy(n_used - 1 - i).wait()


def _experts(seg_start, seg_blocks, xs, wg, wu, wd):
    n_slots, half = xs.shape
    n_e, d, de = wg.shape
    n_w = EXPERT_WEIGHT_BUFFERS
    return pl.pallas_call(
        _expert_kernel,
        out_shape=jax.ShapeDtypeStruct((n_slots, half), U32),
        grid_spec=pltpu.PrefetchScalarGridSpec(
            num_scalar_prefetch=2,
            grid=(n_e,),
            in_specs=[pl.BlockSpec(memory_space=pl.ANY)] * 4,
            out_specs=pl.BlockSpec(memory_space=pl.ANY),
            scratch_shapes=[pltpu.VMEM((EXPERT_IN_RING, MOE_BLOCK, half), U32),
                            pltpu.VMEM((EXPERT_OUT_RING, MOE_BLOCK, half), U32),
                            pltpu.VMEM((d, de), BF16), pltpu.VMEM((d, de), BF16),
                            pltpu.VMEM((de, d), BF16),
                            pltpu.VMEM((n_w, d, de), F32), pltpu.VMEM((n_w, d, de), F32),
                            pltpu.VMEM((n_w, de, d), F32),
                            pltpu.SemaphoreType.DMA((EXPERT_IN_RING,)),
                            pltpu.SemaphoreType.DMA((EXPERT_OUT_RING,)),
                            pltpu.SemaphoreType.DMA((n_w,))]),
        compiler_params=_params(),
        name="moe_experts",
    )(seg_start, seg_blocks, xs, wg, wu, wd)


def _combine_kernel(yg_ref, gt_ref, x_ref, g2_ref, fg_ref, o_ref):
    k = yg_ref.shape[0]
    gt = gt_ref[...]
    lo, hi = _unpack_halves(yg_ref[0])
    y_lo, y_hi = lo * gt[:, 0:1], hi * gt[:, 0:1]
    for r in range(1, k):
        lo, hi = _unpack_halves(yg_ref[r])
        y_lo, y_hi = y_lo + lo * gt[:, r:r + 1], y_hi + hi * gt[:, r:r + 1]
    y = jnp.concatenate([y_lo, y_hi], axis=1)
    o_ref[...] = _rms(x_ref[...] + g2_ref[0] * y, fg_ref[...])


def _combine_into_kernel(yg_ref, gt_ref, x_ref, g2_ref, fg_ref, prev_ref, o_ref):
    del prev_ref
    _combine_kernel(yg_ref, gt_ref, x_ref, g2_ref, fg_ref, o_ref)


def _combine(yg, tok0, gates_t, x1s, gate2, final_g, seq, tc, out_so_far=None):
    k, n, half = yg.shape
    t, d = x1s.shape
    assert n % tc == 0 and tok0 % tc == 0
    b0 = tok0 // tc
    args = [yg, gates_t, x1s, gate2, final_g.reshape(1, d)]
    in_specs = [pl.BlockSpec((k, tc, half), lambda i: (0, i, 0)),
                pl.BlockSpec((tc, k), lambda i: (i + b0, 0)),
                pl.BlockSpec((tc, d), lambda i: (i + b0, 0)),
                pl.BlockSpec((1, 1, d), lambda i: (((i + b0) * tc) // seq, 0, 0)),
                pl.BlockSpec((1, d), lambda i: (0, 0))]
    aliases = {}
    kernel = _combine_kernel
    if out_so_far is not None:
        args.append(out_so_far)
        in_specs.append(pl.BlockSpec(memory_space=pl.ANY))
        aliases = {len(args) - 1: 0}
        kernel = _combine_into_kernel
    return pl.pallas_call(
        kernel,
        out_shape=jax.ShapeDtypeStruct((t, d), F32),
        grid=(n // tc,),
        in_specs=in_specs,
        out_specs=pl.BlockSpec((tc, d), lambda i: (i + b0, 0)),
        input_output_aliases=aliases,
        compiler_params=_params(),
        name="moe_combine",
    )(*args)


def _layer(x2, c, bsz, seq, lb_row, ada_w, ada_b, norm1_g, w_in, hg_norm_g, w_branch_a, w_branch_b,
           w_out, norm2_g, w_router, router_bias, w_exp_gate, w_exp_up, w_exp_down, w_sh_gate,
           w_sh_up, w_sh_down, final_g):
    t, d = x2.shape
    n_e = w_router.shape[1]
    mod = _ada(c, ada_w, ada_b).reshape(bsz, 6, 1, d)
    shift1, scale1, gate1, shift2, scale2, gate2 = (mod[:, j] for j in range(6))

    hw = hg_norm_g.shape[0]
    aw = len(ATT_GROUPS) * ATT_HEADS_PER_GROUP * ATT_HEAD_DIM
    flat_segs = [(0, hw, BF16), (hw, hw, F32), (2 * hw, hw, BF16), (3 * hw, hw, BF16),
                 (4 * hw + 3 * aw, d, BF16), (4 * hw + 3 * aw + d, d, BF16)]
    (hq, hf, hi, hg, ga, gb), qkv = _inproj(
        x2, norm1_g, scale1, shift1, w_in.astype(BF16), bsz, seq, flat_segs, 4 * hw,
        tm=IN_PROJ_TILE)

    ya = _hgrn(hq, hf, hi, hg, lb_row, hg_norm_g, bsz, seq, ts=HGRN_TILE)
    att = [_attn_group(*qkv[3 * g:3 * g + 3], g, blocks_per_step=ATT_BLOCKS_PER_STEP)
           for g in range(len(ATT_GROUPS))]

    x1s, hp, idx, gates, rank, cnt = _merge(
        ya, att, ga, gb, x2, gate1, scale2, shift2, gate2, norm2_g, w_branch_a.astype(BF16),
        w_branch_b.astype(BF16), w_out.astype(BF16), w_router.T, w_sh_gate.astype(BF16),
        w_sh_up.astype(BF16), w_sh_down.astype(BF16), router_bias, seq, tm=MERGE_TILE)
    counts = cnt[:, 0]
    padded = (counts + MOE_BLOCK - 1) // MOE_BLOCK * MOE_BLOCK
    seg_start = (jnp.cumsum(padded) - padded).astype(I32)
    n_blocks = -(-(t * TOP_K) // MOE_BLOCK) + n_e
    dest = _dest(idx, rank, seg_start, tt=DEST_TILE)

    xs = _sc_scatter_rows(hp, dest, n_blocks * MOE_BLOCK)
    ys = _experts(seg_start, (padded // MOE_BLOCK).astype(I32), xs, w_exp_gate, w_exp_up,
                  w_exp_down)
    out, n = None, t // COMBINE_PARTS
    for part in range(COMBINE_PARTS):
        yg = _sc_gather_rows(ys, dest[:, part * n:(part + 1) * n]).reshape(TOP_K, n, d // 2)
        out = _combine(yg, part * n, gates.T, x1s, gate2, final_g, seq, tc=COMBINE_TILE,
                       out_so_far=out)
    return out


def kernel(x, c, ada_w, ada_b, norm1_g, w_in, lb_logits, hg_norm_g, w_branch_a, w_branch_b, w_out,
           norm2_g, w_router, router_bias, w_exp_gate, w_exp_up, w_exp_down, w_sh_gate, w_sh_up,
           w_sh_down, final_g):
    bsz, seq, d = x.shape
    depth = ada_w.shape[0]
    assert depth == 1, "the last layer's kernels also apply the final norm"
    lb_table = jnp.cumsum(jax.nn.softmax(lb_logits.astype(F32), axis=0), axis=0)
    out = _layer(x.reshape(bsz * seq, d), c, bsz, seq, lb_table[0], ada_w[0], ada_b[0], norm1_g[0],
                 w_in[0], hg_norm_g[0], w_branch_a[0], w_branch_b[0], w_out[0], norm2_g[0],
                 w_router[0], router_bias[0], w_exp_gate[0], w_exp_up[0], w_exp_down[0],
                 w_sh_gate[0], w_sh_up[0], w_sh_down[0], final_g)
    return out.reshape(bsz, seq, d)
```

```python
import functools

import jax
import jax.numpy as jnp
from jax import lax
from jax.experimental import pallas as pl
from jax.experimental.pallas import tpu as pltpu
from jax.experimental.pallas import tpu_sc as plsc

F32 = jnp.float32
BF16 = jnp.bfloat16
I32 = jnp.int32
U32 = jnp.uint32
HIGHEST = lax.Precision.HIGHEST

HG_HEADS = 4
HG_BLOCK = 16
HG_CHUNK = 32
HG_MILD_DECAY = -80.0
ATT_GROUPS = ((128, 1), (512, 4), (2048, 16))
ATT_HEADS_PER_GROUP = 4
ATT_HEAD_DIM = 64
TOP_K = 8
ROUTE_SCALE = 2.5
MOE_BLOCK = 256
RMS_EPS = 1e-6
N_DMA_QUEUES = 2
SC_WINDOW = 128
COMBINE_PARTS = 8
EXPERT_WEIGHT_BUFFERS = 3
EXPERT_GROUP = 4
EXPERT_IN_RING = 8
EXPERT_OUT_RING = 6

LANES = 128
VMEM_LIMIT_BYTES = 56 * 1024 * 1024

IN_PROJ_TILE = 1024
HGRN_TILE = 512
ATT_BLOCKS_PER_STEP = 16
MERGE_TILE = 512
DEST_TILE = 2048
COMBINE_TILE = 512


def _sigmoid(x):
    return 1.0 / (1.0 + jnp.exp(-x))


def _silu(x):
    return x * _sigmoid(x)


def _rms(x, g):
    return x * lax.rsqrt(jnp.mean(x * x, axis=-1, keepdims=True) + RMS_EPS) * g


def _pack_halves(x):
    n = x.shape[1] // 2
    bits = lax.bitcast_convert_type(x.astype(BF16).astype(F32), U32)
    return (bits[:, :n] >> 16) | (bits[:, n:] & jnp.uint32(0xFFFF0000))


def _unpack_halves(word):
    lo = lax.bitcast_convert_type(word << 16, F32)
    hi = lax.bitcast_convert_type(word & jnp.uint32(0xFFFF0000), F32)
    return lo, hi


def _params(n_axes=1):
    return pltpu.CompilerParams(
        dimension_semantics=("arbitrary",) * n_axes, vmem_limit_bytes=VMEM_LIMIT_BYTES)


def _ada_kernel(c_ref, w_ref, b_ref, o_ref):
    sc = _silu(c_ref[...])
    o_ref[...] = jnp.dot(sc, w_ref[...], preferred_element_type=F32, precision=HIGHEST) + b_ref[...]


def _ada(c, w, b):
    bsz, d = c.shape
    n = w.shape[1]
    return pl.pallas_call(
        _ada_kernel,
        out_shape=jax.ShapeDtypeStruct((bsz, n), F32),
        grid=(n // d,),
        in_specs=[pl.BlockSpec((bsz, d), lambda j: (0, 0)),
                  pl.BlockSpec((d, d), lambda j: (0, j)),
                  pl.BlockSpec((1, d), lambda j: (0, j))],
        out_specs=pl.BlockSpec((bsz, d), lambda j: (0, j)),
        compiler_params=_params(),
        name="ada_mod",
    )(c, w, b.reshape(1, n))


def _inproj_kernel(n_flat, flat_ranges, att_c0, x_ref, g_ref, sc_ref, sh_ref, w_ref, *refs):
    flat_refs, att_refs, scr = refs[:n_flat], refs[n_flat:-1], refs[-1]
    tm = x_ref.shape[0]
    h = _rms(x_ref[...], g_ref[...]) * (1.0 + sc_ref[0]) + sh_ref[0]
    hb = h.astype(BF16)
    for (c0, c1), o_ref in zip(flat_ranges, flat_refs):
        o_ref[...] = jnp.dot(hb, w_ref[:, c0:c1], preferred_element_type=F32).astype(o_ref.dtype)
    gw = ATT_HEADS_PER_GROUP * ATT_HEAD_DIM
    n_groups = len(ATT_GROUPS)
    for part in range(3):
        c0 = att_c0 + part * n_groups * gw
        res = jnp.dot(hb, w_ref[:, c0:c0 + n_groups * gw], preferred_element_type=F32)
        if part == 0:
            res = res * (ATT_HEAD_DIM ** -0.5)
        for g, (_, dil) in enumerate(ATT_GROUPS):
            o_ref = att_refs[g * 3 + part]
            sub = res[:, g * gw:(g + 1) * gw]
            if dil == 1:
                o_ref[0, 0] = sub.astype(BF16)
            else:
                for c in range(gw // LANES):
                    scr[c] = sub[:, c * LANES:(c + 1) * LANES]
                for r in range(dil):
                    o_ref[0, r] = jnp.concatenate(
                        [scr[c, pl.ds(r, tm // dil, stride=dil), :] for c in range(gw // LANES)],
                        axis=1).astype(BF16)


def _inproj(x2, g, scale, shift, w_bf16, bsz, seq, flat_segs, att_c0, tm):
    t, d = x2.shape
    gw = ATT_HEADS_PER_GROUP * ATT_HEAD_DIM
    n_per = seq // tm
    per_b = lambda i: (i // n_per, 0, 0)
    att_shapes, att_specs = [], []
    for _, dil in ATT_GROUPS:
        for _ in range(3):
            att_shapes.append(jax.ShapeDtypeStruct((bsz, dil, seq // dil, gw), BF16))
            att_specs.append(pl.BlockSpec((1, dil, tm // dil, gw),
                                          lambda i: (i // n_per, 0, i % n_per, 0)))
    outs = pl.pallas_call(
        functools.partial(_inproj_kernel, len(flat_segs),
                          tuple((c0, c0 + wdt) for c0, wdt, _ in flat_segs), att_c0),
        out_shape=[jax.ShapeDtypeStruct((t, wdt), dt) for _, wdt, dt in flat_segs] + att_shapes,
        grid=(t // tm,),
        in_specs=[pl.BlockSpec((tm, d), lambda i: (i, 0)),
                  pl.BlockSpec((1, d), lambda i: (0, 0)),
                  pl.BlockSpec((1, 1, d), per_b),
                  pl.BlockSpec((1, 1, d), per_b),
                  pl.BlockSpec(w_bf16.shape, lambda i: (0, 0), pipeline_mode=pl.Buffered(1))],
        out_specs=[pl.BlockSpec((tm, wdt), lambda i: (i, 0)) for _, wdt, _ in flat_segs]
        + att_specs,
        scratch_shapes=[pltpu.VMEM((gw // LANES, tm, LANES), F32)],
        compiler_params=_params(),
        name="in_proj",
    )(x2, g.reshape(1, d), scale, shift, w_bf16)
    return outs[:len(flat_segs)], outs[len(flat_segs):]


def _hgrn_kernel(ts, q_ref, f_ref, v_ref, gt_ref, lb_ref, ng_ref, o_ref, st_ref, b_ref):
    dk = q_ref.shape[1] // HG_HEADS
    n_chunks = ts // HG_CHUNK
    n_blk = HG_CHUNK // HG_BLOCK

    @pl.when(pl.program_id(1) == 0)
    def _():
        st_ref[...] = jnp.zeros_like(st_ref)

    row = lax.broadcasted_iota(I32, (LANES, LANES), 0)
    col = lax.broadcasted_iota(I32, (LANES, LANES), 1)
    same_chunk = (row // HG_CHUNK) == (col // HG_CHUNK)
    cum_mat = jnp.where(same_chunk & (col <= row), 1.0, 0.0).astype(BF16)

    def chunk_cumsum(x):
        out = []
        for r0 in range(0, ts, LANES):
            rest = x[r0:r0 + LANES]
            acc = None
            for _ in range(3):
                term = rest.astype(BF16)
                part = jnp.dot(cum_mat, term, preferred_element_type=F32)
                acc = part if acc is None else acc + part
                rest = rest - term.astype(F32)
            out.append(acc)
        return jnp.concatenate(out, axis=0)

    def forget(cs):
        lb = lb_ref[:, cs]
        return lb + (1.0 - lb) * _sigmoid(f_ref[:, cs])

    b_min = None
    for h in range(HG_HEADS):
        cs = slice(h * dk, (h + 1) * dk)
        b = chunk_cumsum(jnp.log(forget(cs)))
        b_ref[:, cs] = b
        m = jnp.min(b)
        b_min = m if b_min is None else jnp.minimum(b_min, m)
    mild = b_min >= HG_MILD_DECAY

    def finish(h, o, st):
        cs = slice(h * dk, (h + 1) * dk)
        st_ref[h] = st
        y = _rms(o, ng_ref[:, cs]) * _silu(gt_ref[:, cs].astype(F32))
        o_ref[:, cs] = y.astype(o_ref.dtype)

    @pl.when(mild)
    def _():
        span = 2 * HG_CHUNK
        causal = (lax.broadcasted_iota(I32, (span, span), 0)
                  >= lax.broadcasted_iota(I32, (span, span), 1))
        nt = lambda x, y: lax.dot_general(x, y, (((1,), (1,)), ((), ())),
                                          preferred_element_type=F32)
        for h in range(HG_HEADS):
            cs = slice(h * dk, (h + 1) * dk)
            v = v_ref[:, cs]
            b = b_ref[:, cs]
            q = q_ref[:, cs].astype(F32)
            k = 1.0 - forget(cs)
            st = st_ref[h]
            o_rows = []
            for r0 in range(0, ts, span):
                sl = slice(r0, r0 + span)
                b_first, b_second = b[r0:r0 + HG_CHUNK], b[r0 + HG_CHUNK:r0 + span]
                end_first = b_first[HG_CHUNK - 1:HG_CHUNK]
                end_second = b_second[HG_CHUNK - 1:HG_CHUNK]
                e = jnp.exp(jnp.concatenate([b_first - end_first, b_second], axis=0))
                qe = (q[sl] * e).astype(BF16)
                ke = k[sl] / e
                a = jnp.where(causal, nt(qe, ke.astype(BF16)), 0.0).astype(BF16)
                st_in = (st * jnp.exp(end_first)).astype(BF16)
                o_rows.append(jnp.dot(a, v[sl], preferred_element_type=F32) + nt(qe, st_in))
                kend = (ke * jnp.exp(end_second)).astype(BF16)
                vt = v[sl].astype(F32).T.astype(BF16)
                st = (st * jnp.exp(end_first + end_second)
                      + jnp.dot(vt, kend, preferred_element_type=F32))
            finish(h, jnp.concatenate(o_rows, axis=0), st)

    @pl.when(jnp.logical_not(mild))
    def _():
        _hgrn_steep(ts, dk, n_chunks, n_blk, q_ref, v_ref, b_ref, st_ref, forget, finish)


def _hgrn_steep(ts, dk, n_chunks, n_blk, q_ref, v_ref, b_ref, st_ref, forget, finish):
    t_in_blk = lax.broadcasted_iota(I32, (ts, dk), 0) % HG_BLOCK

    for h in range(HG_HEADS):
        cs = slice(h * dk, (h + 1) * dk)
        q = q_ref[:, cs].astype(F32)
        v = v_ref[:, cs].astype(F32)
        k = 1.0 - forget(cs)
        b = b_ref[:, cs]

        o = jnp.sum(q * k, axis=-1, keepdims=True) * v
        for d in range(1, HG_BLOCK):
            k_d = pltpu.roll(k, d, axis=0)
            b_d = pltpu.roll(b, d, axis=0)
            v_d = pltpu.roll(v, d, axis=0)
            w = jnp.sum(q * k_d * jnp.exp(jnp.minimum(b - b_d, 0.0)), axis=-1, keepdims=True)
            o = o + jnp.where(t_in_blk >= d, w * v_d, 0.0)

        st = st_ref[h]
        o_rows = []
        for c in range(n_chunks):
            r0 = c * HG_CHUNK
            bc = b[r0:r0 + HG_CHUNK]
            qc = q[r0:r0 + HG_CHUNK]
            kc = k[r0:r0 + HG_CHUNK]
            vc = v[r0:r0 + HG_CHUNK].astype(BF16)
            st_b = st.astype(BF16)
            for i in range(n_blk):
                i0 = i * HG_BLOCK
                if i == 0:
                    qt = qc[:HG_BLOCK] * jnp.exp(bc[:HG_BLOCK])
                    qs = qt
                else:
                    ref_row = bc[i0 - 1:i0]
                    qt = qc[i0:i0 + HG_BLOCK] * jnp.exp(bc[i0:i0 + HG_BLOCK] - ref_row)
                    qs = qt * jnp.exp(ref_row)
                oi = lax.dot_general(qs.astype(BF16), st_b, (((1,), (1,)), ((), ())),
                                     preferred_element_type=F32)
                if i > 0:
                    kh = kc[:i0] * jnp.exp(ref_row - bc[:i0])
                    a = lax.dot_general(qt.astype(BF16), kh.astype(BF16), (((1,), (1,)), ((), ())),
                                        preferred_element_type=F32)
                    oi = oi + jnp.dot(a.astype(BF16), vc[:i0], preferred_element_type=F32)
                o_rows.append(oi)
            b_end = bc[HG_CHUNK - 1:HG_CHUNK]
            kend = kc * jnp.exp(b_end - bc)
            vt = v[r0:r0 + HG_CHUNK].T.astype(BF16)
            st = st * jnp.exp(b_end) + jnp.dot(vt, kend.astype(BF16), preferred_element_type=F32)
        finish(h, o + jnp.concatenate(o_rows, axis=0), st)


def _hgrn(hq, hf, hi, hg, lb, ng, bsz, seq, ts):
    t, w = hq.shape
    dk = w // HG_HEADS
    n_s = seq // ts
    tile = lambda b, s: (b * n_s + s, 0)
    return pl.pallas_call(
        functools.partial(_hgrn_kernel, ts),
        out_shape=jax.ShapeDtypeStruct((t, w), BF16),
        grid=(bsz, n_s),
        in_specs=[pl.BlockSpec((ts, w), tile)] * 4
        + [pl.BlockSpec((1, w), lambda b, s: (0, 0))] * 2,
        out_specs=pl.BlockSpec((ts, w), tile),
        scratch_shapes=[pltpu.VMEM((HG_HEADS, dk, dk), F32), pltpu.VMEM((ts, w), F32)],
        compiler_params=_params(2),
        name="hgrn2",
    )(hq, hf, hi, hg, lb.reshape(1, w), ng.reshape(1, w))


def _attn_kernel(nk, nq, nr, q_ref, kp_ref, kc_ref, vp_ref, vc_ref, o_ref, lse_ref):
    n = pl.program_id(2)
    e = ATT_HEAD_DIM
    i = lax.broadcasted_iota(I32, (nk, 2 * nk), 0)
    j = lax.broadcasted_iota(I32, (nk, 2 * nk), 1)
    band = (j >= i) & (j <= i + nk)
    first_head = lax.broadcasted_iota(I32, (nk, LANES), 1) < e
    zero = jnp.zeros((), q_ref.dtype)
    for r in range(nr):
        kk = jnp.concatenate([kp_ref[0, r], kc_ref[0, r]], axis=0)
        vv = jnp.concatenate([vp_ref[0, r], vc_ref[0, r]], axis=0)
        for b in range(nq):
            valid = band & ((j >= nk) | (n * nq + b > 0))
            rows = slice(b * nk, (b + 1) * nk)
            for c in range(0, ATT_HEADS_PER_GROUP * e, LANES):
                q = q_ref[0, r, rows, c:c + LANES]
                kb = kk[b * nk:(b + 2) * nk, c:c + LANES]
                vb = vv[b * nk:(b + 2) * nk, c:c + LANES]
                outs, lses = [], []
                for keep in (first_head, jnp.logical_not(first_head)):
                    s = lax.dot_general(jnp.where(keep, q, zero), kb, (((1,), (1,)), ((), ())),
                                        preferred_element_type=F32)
                    s = jnp.where(valid, s, -jnp.inf)
                    m = jnp.max(s, axis=-1, keepdims=True)
                    p = jnp.exp(s - m)
                    l = jnp.sum(p, axis=-1, keepdims=True)
                    outs.append(jnp.dot(p.astype(BF16), vb, preferred_element_type=F32) / l)
                    lses.append(m + jnp.log(l))
                o_ref[0, r, rows, c:c + LANES] = jnp.where(first_head, outs[0], outs[1])
                lse_ref[0, r, rows, c:c + LANES] = jnp.where(first_head, lses[0], lses[1])


def _attn_group(q, k, v, g, blocks_per_step):
    window, dil = ATT_GROUPS[g]
    nk = window // dil
    bsz, _, ln, gw = q.shape
    nq = min(blocks_per_step, ln // nk)
    nr = min(blocks_per_step // nq, dil)
    assert ln % (nk * nq) == 0 and dil % nr == 0 and 2 * ATT_HEAD_DIM == LANES
    cur = pl.BlockSpec((1, nr, nq * nk, gw), lambda b, r, n: (b, r, n, 0))
    prev = pl.BlockSpec((1, nr, nk, gw), lambda b, r, n: (b, r, jnp.maximum(n * nq - 1, 0), 0))
    return pl.pallas_call(
        functools.partial(_attn_kernel, nk, nq, nr),
        out_shape=[jax.ShapeDtypeStruct(q.shape, F32)] * 2,
        grid=(bsz, dil // nr, ln // (nk * nq)),
        in_specs=[cur, prev, cur, prev, cur],
        out_specs=[cur, cur],
        compiler_params=_params(3),
        name=f"dilated_attn_g{g}",
    )(q, k, k, v, v)


def _token_major(ref, scr):
    dil, rows = ref.shape[1], ref.shape[2]
    if dil == 1:
        return ref[0, 0]
    n_col = scr.shape[0]
    for r in range(dil):
        for c in range(n_col):
            scr[c, pl.ds(r, rows, stride=dil), :] = ref[0, r, :, c * LANES:(c + 1) * LANES]
    return jnp.concatenate([scr[c] for c in range(n_col)], axis=1)


def _merge_kernel(ya_ref, o0_ref, o1_ref, o2_ref, l0_ref, l1_ref, l2_ref, ga_ref, gb_ref, x_ref,
                  g1_ref, sc2_ref, sh2_ref, g2_ref, n2_ref, wa_ref, wb_ref, wo_ref, wr_ref, wrl_ref,
                  wsg_ref, wsu_ref, wsd_ref, bias_ref, x1_ref, hp_ref, idx_ref, gate_ref, rank_ref,
                  cnt_ref, carry_ref, lg_ref, *scr):
    step = pl.program_id(0)

    @pl.when(step == 0)
    def _():
        carry_ref[...] = jnp.zeros_like(carry_ref)
        lg_ref[...] = jnp.zeros_like(lg_ref)

    _route(lg_ref[...], jnp.where(step > 0, 1.0, 0.0), bias_ref, idx_ref, gate_ref, rank_ref,
           cnt_ref, carry_ref)

    l0, l1, l2 = (_token_major(r, s) for r, s in zip((l0_ref, l1_ref, l2_ref), scr[:3]))
    o0, o1, o2 = (_token_major(r, s) for r, s in zip((o0_ref, o1_ref, o2_ref), scr[3:]))
    m = jnp.maximum(jnp.maximum(l0, l1), l2)
    e0, e1, e2 = jnp.exp(l0 - m), jnp.exp(l1 - m), jnp.exp(l2 - m)
    yb = (e0 * o0 + e1 * o1 + e2 * o2) / (e0 + e1 + e2)
    merged = (_sigmoid(ga_ref[...].astype(F32))
              * jnp.dot(ya_ref[...], wa_ref[...], preferred_element_type=F32)
              + _sigmoid(gb_ref[...].astype(F32))
              * jnp.dot(yb.astype(BF16), wb_ref[...], preferred_element_type=F32))
    x1 = x_ref[...] + g1_ref[0] * jnp.dot(merged.astype(BF16), wo_ref[...],
                                           preferred_element_type=F32)
    h2 = _rms(x1, n2_ref[...]) * (1.0 + sc2_ref[0]) + sh2_ref[0]
    hb = h2.astype(BF16)
    act = (_silu(jnp.dot(hb, wsg_ref[...], preferred_element_type=F32))
           * jnp.dot(hb, wsu_ref[...], preferred_element_type=F32))
    shared = jnp.dot(act.astype(BF16), wsd_ref[...], preferred_element_type=F32)
    x1_ref[...] = x1 + g2_ref[0] * shared
    hp_ref[...] = _pack_halves(h2)
    h_lo = (h2 - hb.astype(F32)).astype(BF16)
    nt = lambda a, b: lax.dot_general(a, b, (((1,), (1,)), ((), ())), preferred_element_type=F32)
    lg_ref[...] = nt(wr_ref[...], hb) + (nt(wr_ref[...], h_lo) + nt(wrl_ref[...], hb))


def _merge(ya, att, ga, gb, x2, gate1, scale2, shift2, gate2, norm2_g, wa, wb, wo, wr_t, wsg, wsu,
           wsd, router_bias, seq, tm):
    t, d = x2.shape
    n_e = wr_t.shape[0]
    wr_hi = wr_t.astype(BF16)
    wr_lo = (wr_t - wr_hi.astype(F32)).astype(BF16)
    n_per = seq // tm
    n_tiles = t // tm
    tile = lambda i: jnp.minimum(i, n_tiles - 1)
    per_b = lambda i: (tile(i) // n_per, 0, 0)
    rows = lambda wdt: pl.BlockSpec((tm, wdt), lambda i: (tile(i), 0))
    full = lambda a: pl.BlockSpec(a.shape, lambda i: (0,) * a.ndim)
    vec = pl.BlockSpec((1, 1, d), per_b)
    (o0, l0), (o1, l1), (o2, l2) = att
    gw = o0.shape[3]
    by_residue = lambda a: pl.BlockSpec((1, a.shape[1], tm // a.shape[1], gw),
                                        lambda i: (tile(i) // n_per, 0, tile(i) % n_per, 0))
    att_in = (o0, o1, o2, l0, l1, l2)
    bias_col = router_bias.reshape(n_e, 1)
    tok = pl.BlockSpec((TOP_K, tm), lambda i: (0, jnp.maximum(i - 1, 0)))
    return pl.pallas_call(
        _merge_kernel,
        out_shape=[jax.ShapeDtypeStruct((t, d), F32),
                   jax.ShapeDtypeStruct((t, d // 2), U32),
                   jax.ShapeDtypeStruct((TOP_K, t), I32), jax.ShapeDtypeStruct((TOP_K, t), F32),
                   jax.ShapeDtypeStruct((TOP_K, t), I32), jax.ShapeDtypeStruct((n_e, LANES), I32)],
        grid=(n_tiles + 1,),
        in_specs=[rows(ya.shape[1])] + [by_residue(a) for a in att_in] + [rows(d)] * 3
        + [vec, vec, vec, vec, pl.BlockSpec((1, d), lambda i: (0, 0))]
        + [full(a) for a in (wa, wb, wo, wr_hi, wr_lo, wsg, wsu, wsd, bias_col)],
        out_specs=[rows(d), rows(d // 2), tok, tok, tok,
                   pl.BlockSpec((n_e, LANES), lambda i: (0, 0))],
        scratch_shapes=[pltpu.VMEM((n_e, 1), F32), pltpu.VMEM((n_e, tm), F32)]
        + [pltpu.VMEM((gw // LANES, tm, LANES), F32)] * 6,
        compiler_params=_params(),
        name="merge_router",
    )(ya, *att_in, ga, gb, x2, gate1, scale2, shift2, gate2,
      norm2_g.reshape(1, d), wa, wb, wo, wr_hi, wr_lo, wsg, wsu, wsd, bias_col)


def _route(logits, live, bias_ref, idx_ref, gate_ref, rank_ref, cnt_ref, carry_ref):
    n_e, tt = logits.shape
    scores = _sigmoid(logits)
    sel = scores + bias_ref[...]
    eio = lax.broadcasted_iota(I32, (n_e, tt), 0)
    picked = jnp.zeros((n_e, tt), F32)
    idxs, vals = [], []
    for _ in range(TOP_K):
        m = jnp.max(sel, axis=0, keepdims=True)
        ik = jnp.min(jnp.where(sel == m, eio, n_e), axis=0, keepdims=True)
        hit = eio == ik
        vals.append(jnp.sum(jnp.where(hit, scores, 0.0), axis=0, keepdims=True))
        sel = jnp.where(hit, -jnp.inf, sel)
        picked = picked + jnp.where(hit, 1.0, 0.0)
        idxs.append(ik)
    denom = vals[0]
    for v in vals[1:]:
        denom = denom + v
    gate_ref[...] = jnp.concatenate([v / denom * ROUTE_SCALE for v in vals], axis=0)
    idx_ref[...] = jnp.concatenate(idxs, axis=0)

    upper = (lax.broadcasted_iota(I32, (tt, tt), 0) <= lax.broadcasted_iota(I32, (tt, tt), 1))
    incl = jnp.dot(picked.astype(BF16), jnp.where(upper, 1.0, 0.0).astype(BF16),
                   preferred_element_type=F32)
    before = incl - picked + carry_ref[...]
    rank_ref[...] = jnp.concatenate(
        [jnp.sum(jnp.where(eio == ik, before, 0.0), axis=0, keepdims=True) for ik in idxs],
        axis=0).astype(I32)
    carry_ref[...] = carry_ref[...] + jnp.sum(picked, axis=1, keepdims=True) * live
    cnt_ref[...] = jnp.broadcast_to(carry_ref[...], cnt_ref.shape).astype(I32)


def _dest_kernel(idx_ref, rank_ref, start_ref, o_ref):
    k, tt = idx_ref.shape
    n_e = start_ref.shape[0]
    eio = lax.broadcasted_iota(I32, (n_e, tt), 0)
    start = start_ref[...]
    rows = [jnp.sum(jnp.where(eio == idx_ref[r:r + 1, :], start, 0), axis=0, keepdims=True)
            for r in range(k)]
    o_ref[...] = jnp.concatenate(rows, axis=0) + rank_ref[...]


def _dest(idx, rank, seg_start, tt):
    k, t = idx.shape
    n_e = seg_start.shape[0]
    tok = pl.BlockSpec((k, tt), lambda i: (0, i))
    return pl.pallas_call(
        _dest_kernel,
        out_shape=jax.ShapeDtypeStruct((k, t), I32),
        grid=(t // tt,),
        in_specs=[tok, tok, pl.BlockSpec((n_e, 1), lambda i: (0, 0))],
        out_specs=tok,
        compiler_params=_params(),
        name="moe_dest",
    )(idx, rank, seg_start.reshape(n_e, 1))


def _sc_mesh():
    return plsc.VectorSubcoreMesh(core_axis_name="core", subcore_axis_name="subcore")


def _sc_scatter_rows(rows, dest, n_out):
    k, t = dest.shape
    w = rows.shape[1]
    mesh = _sc_mesh()
    n_workers = mesh.num_cores * mesh.num_subcores
    win_per_worker = t // (SC_WINDOW * n_workers)
    assert win_per_worker * SC_WINDOW * n_workers == t

    @functools.partial(
        pl.kernel, out_type=jax.ShapeDtypeStruct((n_out, w), rows.dtype), mesh=mesh,
        scratch_types=[pltpu.VMEM((SC_WINDOW, w), rows.dtype)]
        + [pltpu.VMEM((1, SC_WINDOW), I32)] * k + [pltpu.SemaphoreType.DMA],
        name="moe_dispatch_sc")
    def run(rows_hbm, idx_hbm, out_hbm, rows_v, *rest):
        idx_v, sem = rest[:k], rest[k]
        worker = lax.axis_index("subcore") * mesh.num_cores + lax.axis_index("core")

        @pl.loop(0, win_per_worker)
        def _(j):
            t0 = pl.multiple_of((worker * win_per_worker + j) * SC_WINDOW, SC_WINDOW)
            loads = [pltpu.async_copy(rows_hbm.at[pl.ds(t0, SC_WINDOW)], rows_v, sem)]
            loads += [pltpu.async_copy(idx_hbm.at[:, pl.ds(r * t + t0, SC_WINDOW)], idx_v[r], sem)
                      for r in range(k)]
            for c in loads:
                c.wait()
            copies = [pltpu.async_copy(rows_v, out_hbm.at[idx_v[r].at[0]], sem) for r in range(k)]
            for c in copies:
                c.wait()

    return run(rows, dest.reshape(1, k * t))


def _sc_gather_rows(table, dest):
    k, t = dest.shape
    w = table.shape[1]
    mesh = _sc_mesh()
    n_workers = mesh.num_cores * mesh.num_subcores
    win_per_worker = (k * t) // (SC_WINDOW * n_workers)
    assert win_per_worker * SC_WINDOW * n_workers == k * t and win_per_worker % 2 == 0

    @functools.partial(
        pl.kernel, out_type=jax.ShapeDtypeStruct((k * t, w), table.dtype), mesh=mesh,
        scratch_types=[pltpu.VMEM((SC_WINDOW, w), table.dtype), pltpu.VMEM((1, SC_WINDOW), I32),
                       pltpu.VMEM((1, SC_WINDOW), I32), pltpu.SemaphoreType.DMA],
        name="moe_gather_sc")
    def run(table_hbm, idx_hbm, out_hbm, rows_v, idx_a, idx_b, sem):
        worker = lax.axis_index("subcore") * mesh.num_cores + lax.axis_index("core")

        def window(j):
            return pl.ds(pl.multiple_of((worker * win_per_worker + j) * SC_WINDOW, SC_WINDOW),
                         SC_WINDOW)

        def idx_load(j, buf):
            return pltpu.make_async_copy(idx_hbm.at[:, window(j)], buf, sem)

        idx_load(0, idx_a).start()

        @pl.loop(0, win_per_worker, step=2)
        def _(j0):
            for b, (cur, nxt) in enumerate(((idx_a, idx_b), (idx_b, idx_a))):
                j = j0 + b
                idx_load(j, cur).wait()

                @pl.when(j + 1 < win_per_worker)
                def _():
                    idx_load(j + 1, nxt).start()

                pltpu.sync_copy(table_hbm.at[cur.at[0]], rows_v)
                pltpu.sync_copy(rows_v, out_hbm.at[window(j)])

    return run(table, dest.reshape(1, k * t))


def _expert_kernel(start_ref, nblk_ref, xs_ref, wg_ref, wu_ref, wd_ref, ys_ref,
                   xbuf, ybuf, wgb, wub, wdb, wbuf_g, wbuf_u, wbuf_d, sem_in, sem_out, sem_w):
    wbuf = (wbuf_g, wbuf_u, wbuf_d)
    e = pl.program_id(0)
    n_e = pl.num_programs(0)
    nb = nblk_ref[e]
    g0 = start_ref[e] // MOE_BLOCK
    n_used = start_ref[n_e - 1] // MOE_BLOCK + nblk_ref[n_e - 1]
    n_in, n_out = xbuf.shape[0], ybuf.shape[0]

    def rows(g):
        return pl.ds(pl.multiple_of(g * MOE_BLOCK, MOE_BLOCK), MOE_BLOCK)

    def in_copy(g):
        slot = lax.rem(g, n_in)
        return pltpu.make_async_copy(xs_ref.at[rows(g), :], xbuf.at[slot], sem_in.at[slot])

    def out_copy(g):
        slot = lax.rem(g, n_out)
        return pltpu.make_async_copy(ybuf.at[slot], ys_ref.at[rows(g), :], sem_out.at[slot])

    look = n_in - EXPERT_GROUP

    @pl.when(e == 0)
    def _():
        for g in range(look):
            @pl.when(g < n_used)
            def _():
                in_copy(g).start(priority=g % N_DMA_QUEUES)

    n_w = wbuf[0].shape[0]

    def weight_copies(ex):
        slot = lax.rem(ex, n_w)
        return [pltpu.make_async_copy(src.at[ex], buf.at[slot], sem_w.at[slot])
                for src, buf in zip((wg_ref, wu_ref, wd_ref), wbuf)]

    @pl.when(e == 0)
    def _():
        for ex in range(min(n_w, wg_ref.shape[0])):
            for c in weight_copies(ex):
                c.start()

    for c in weight_copies(e):
        c.wait()
    w_slot = lax.rem(e, n_w)

    @pl.when(nb > 0)
    def _():
        wgb[...] = wbuf[0][w_slot].astype(BF16)
        wub[...] = wbuf[1][w_slot].astype(BF16)
        wdb[...] = wbuf[2][w_slot].astype(BF16)

    @pl.when(e + n_w < n_e)
    def _():
        for c in weight_copies(e + n_w):
            c.start()

    @pl.when(nb > 0)
    def _():
        def swiglu(word):
            lo, hi = _unpack_halves(word)
            x = jnp.concatenate([lo.astype(BF16), hi.astype(BF16)], axis=1)
            gate = jnp.dot(x, wgb[...], preferred_element_type=F32)
            up = jnp.dot(x, wub[...], preferred_element_type=F32)
            act = (_silu(gate) * up).astype(BF16)
            return jnp.dot(act, wdb[...], preferred_element_type=F32)

        def process(g, m):
            for i in range(m):
                in_copy(g + i).wait()
            for i in range(m):
                @pl.when(g + look + i < n_used)
                def _():
                    in_copy(g + look + i).start(priority=i % N_DMA_QUEUES)
            y_all = swiglu(jnp.concatenate([xbuf[lax.rem(g + i, n_in)] for i in range(m)], axis=0))
            ys = [y_all[i * MOE_BLOCK:(i + 1) * MOE_BLOCK] for i in range(m)]
            for i in range(m):
                @pl.when(g + i >= n_out)
                def _():
                    out_copy(g + i - n_out).wait()

                ybuf[lax.rem(g + i, n_out)] = _pack_halves(ys[i])
                out_copy(g + i).start(priority=(i + 1) % N_DMA_QUEUES)

        def group_body(p, carry):
            process(g0 + p * EXPERT_GROUP, EXPERT_GROUP)
            return carry

        lax.fori_loop(0, nb // EXPERT_GROUP, group_body, 0)
        for m in range(1, EXPERT_GROUP):
            @pl.when(lax.rem(nb, EXPERT_GROUP) == m)
            def _():
                process(g0 + nb - m, m)

    @pl.when(e == n_e - 1)
    def _():
        for i in range(n_out):
            @pl.when(n_used - 1 - i >= 0)
            def _():
                out_copy(n_used - 1 - i).wait()


def _experts(seg_start, seg_blocks, xs, wg, wu, wd):
    n_slots, half = xs.shape
    n_e, d, de = wg.shape
    n_w = EXPERT_WEIGHT_BUFFERS
    return pl.pallas_call(
        _expert_kernel,
        out_shape=jax.ShapeDtypeStruct((n_slots, half), U32),
        grid_spec=pltpu.PrefetchScalarGridSpec(
            num_scalar_prefetch=2,
            grid=(n_e,),
            in_specs=[pl.BlockSpec(memory_space=pl.ANY)] * 4,
            out_specs=pl.BlockSpec(memory_space=pl.ANY),
            scratch_shapes=[pltpu.VMEM((EXPERT_IN_RING, MOE_BLOCK, half), U32),
                            pltpu.VMEM((EXPERT_OUT_RING, MOE_BLOCK, half), U32),
                            pltpu.VMEM((d, de), BF16), pltpu.VMEM((d, de), BF16),
                            pltpu.VMEM((de, d), BF16),
                            pltpu.VMEM((n_w, d, de), F32), pltpu.VMEM((n_w, d, de), F32),
                            pltpu.VMEM((n_w, de, d), F32),
                            pltpu.SemaphoreType.DMA((EXPERT_IN_RING,)),
                            pltpu.SemaphoreType.DMA((EXPERT_OUT_RING,)),
                            pltpu.SemaphoreType.DMA((n_w,))]),
        compiler_params=_params(),
        name="moe_experts",
    )(seg_start, seg_blocks, xs, wg, wu, wd)


def _combine_kernel(yg_ref, gt_ref, x_ref, g2_ref, fg_ref, o_ref):
    k = yg_ref.shape[0]
    gt = gt_ref[...]
    lo, hi = _unpack_halves(yg_ref[0])
    y_lo, y_hi = lo * gt[:, 0:1], hi * gt[:, 0:1]
    for r in range(1, k):
        lo, hi = _unpack_halves(yg_ref[r])
        y_lo, y_hi = y_lo + lo * gt[:, r:r + 1], y_hi + hi * gt[:, r:r + 1]
    y = jnp.concatenate([y_lo, y_hi], axis=1)
    o_ref[...] = _rms(x_ref[...] + g2_ref[0] * y, fg_ref[...])


def _combine_into_kernel(yg_ref, gt_ref, x_ref, g2_ref, fg_ref, prev_ref, o_ref):
    del prev_ref
    _combine_kernel(yg_ref, gt_ref, x_ref, g2_ref, fg_ref, o_ref)


def _combine(yg, tok0, gates_t, x1s, gate2, final_g, seq, tc, out_so_far=None):
    k, n, half = yg.shape
    t, d = x1s.shape
    assert n % tc == 0 and tok0 % tc == 0
    b0 = tok0 // tc
    args = [yg, gates_t, x1s, gate2, final_g.reshape(1, d)]
    in_specs = [pl.BlockSpec((k, tc, half), lambda i: (0, i, 0)),
                pl.BlockSpec((tc, k), lambda i: (i + b0, 0)),
                pl.BlockSpec((tc, d), lambda i: (i + b0, 0)),
                pl.BlockSpec((1, 1, d), lambda i: (((i + b0) * tc) // seq, 0, 0)),
                pl.BlockSpec((1, d), lambda i: (0, 0))]
    aliases = {}
    kernel = _combine_kernel
    if out_so_far is not None:
        args.append(out_so_far)
        in_specs.append(pl.BlockSpec(memory_space=pl.ANY))
        aliases = {len(args) - 1: 0}
        kernel = _combine_into_kernel
    return pl.pallas_call(
        kernel,
        out_shape=jax.ShapeDtypeStruct((t, d), F32),
        grid=(n // tc,),
        in_specs=in_specs,
        out_specs=pl.BlockSpec((tc, d), lambda i: (i + b0, 0)),
        input_output_aliases=aliases,
        compiler_params=_params(),
        name="moe_combine",
    )(*args)


def _layer(x2, c, bsz, seq, lb_row, ada_w, ada_b, norm1_g, w_in, hg_norm_g, w_branch_a, w_branch_b,
           w_out, norm2_g, w_router, router_bias, w_exp_gate, w_exp_up, w_exp_down, w_sh_gate,
           w_sh_up, w_sh_down, final_g):
    t, d = x2.shape
    n_e = w_router.shape[1]
    mod = _ada(c, ada_w, ada_b).reshape(bsz, 6, 1, d)
    shift1, scale1, gate1, shift2, scale2, gate2 = (mod[:, j] for j in range(6))

    hw = hg_norm_g.shape[0]
    aw = len(ATT_GROUPS) * ATT_HEADS_PER_GROUP * ATT_HEAD_DIM
    flat_segs = [(0, hw, BF16), (hw, hw, F32), (2 * hw, hw, BF16), (3 * hw, hw, BF16),
                 (4 * hw + 3 * aw, d, BF16), (4 * hw + 3 * aw + d, d, BF16)]
    (hq, hf, hi, hg, ga, gb), qkv = _inproj(
        x2, norm1_g, scale1, shift1, w_in.astype(BF16), bsz, seq, flat_segs, 4 * hw,
        tm=IN_PROJ_TILE)

    ya = _hgrn(hq, hf, hi, hg, lb_row, hg_norm_g, bsz, seq, ts=HGRN_TILE)
    att = [_attn_group(*qkv[3 * g:3 * g + 3], g, blocks_per_step=ATT_BLOCKS_PER_STEP)
           for g in range(len(ATT_GROUPS))]

    x1s, hp, idx, gates, rank, cnt = _merge(
        ya, att, ga, gb, x2, gate1, scale2, shift2, gate2, norm2_g, w_branch_a.astype(BF16),
        w_branch_b.astype(BF16), w_out.astype(BF16), w_router.T, w_sh_gate.astype(BF16),
        w_sh_up.astype(BF16), w_sh_down.astype(BF16), router_bias, seq, tm=MERGE_TILE)
    counts = cnt[:, 0]
    padded = (counts + MOE_BLOCK - 1) // MOE_BLOCK * MOE_BLOCK
    seg_start = (jnp.cumsum(padded) - padded).astype(I32)
    n_blocks = -(-(t * TOP_K) // MOE_BLOCK) + n_e
    dest = _dest(idx, rank, seg_start, tt=DEST_TILE)

    xs = _sc_scatter_rows(hp, dest, n_blocks * MOE_BLOCK)
    ys = _experts(seg_start, (padded // MOE_BLOCK).astype(I32), xs, w_exp_gate, w_exp_up,
                  w_exp_down)
    out, n = None, t // COMBINE_PARTS
    for part in range(COMBINE_PARTS):
        yg = _sc_gather_rows(ys, dest[:, part * n:(part + 1) * n]).reshape(TOP_K, n, d // 2)
        out = _combine(yg, part * n, gates.T, x1s, gate2, final_g, seq, tc=COMBINE_TILE,
                       out_so_far=out)
    return out


def kernel(x, c, ada_w, ada_b, norm1_g, w_in, lb_logits, hg_norm_g, w_branch_a, w_branch_b, w_out,
           norm2_g, w_router, router_bias, w_exp_gate, w_exp_up, w_exp_down, w_sh_gate, w_sh_up,
           w_sh_down, final_g):
    bsz, seq, d = x.shape
    depth = ada_w.shape[0]
    assert depth == 1, "the last layer's kernels also apply the final norm"
    lb_table = jnp.cumsum(jax.nn.softmax(lb_logits.astype(F32), axis=0), axis=0)
    out = _layer(x.reshape(bsz * seq, d), c, bsz, seq, lb_table[0], ada_w[0], ada_b[0], norm1_g[0],
                 w_in[0], hg_norm_g[0], w_branch_a[0], w_branch_b[0], w_out[0], norm2_g[0],
                 w_router[0], router_bias[0], w_exp_gate[0], w_exp_up[0], w_exp_down[0],
                 w_sh_gate[0], w_sh_up[0], w_sh_down[0], final_g)
    return out.reshape(bsz, seq, d)
```

```python
import functools

import jax
import jax.numpy as jnp
from jax import lax
from jax.experimental import pallas as pl
from jax.experimental.pallas import tpu as pltpu
from jax.experimental.pallas import tpu_sc as plsc

F32 = jnp.float32
BF16 = jnp.bfloat16
I32 = jnp.int32
U32 = jnp.uint32
HIGHEST = lax.Precision.HIGHEST

HG_HEADS = 4
HG_BLOCK = 16
HG_CHUNK = 32
HG_MILD_DECAY = -80.0
ATT_GROUPS = ((128, 1), (512, 4), (2048, 16))
ATT_HEADS_PER_GROUP = 4
ATT_HEAD_DIM = 64
TOP_K = 8
ROUTE_SCALE = 2.5
MOE_BLOCK = 256
RMS_EPS = 1e-6
N_DMA_QUEUES = 2
SC_WINDOW = 128
COMBINE_PARTS = 8
EXPERT_WEIGHT_BUFFERS = 3
EXPERT_GROUP = 4
EXPERT_IN_RING = 8
EXPERT_OUT_RING = 6

LANES = 128
VMEM_LIMIT_BYTES = 56 * 1024 * 1024

IN_PROJ_TILE = 1024
HGRN_TILE = 512
ATT_BLOCKS_PER_STEP = 16
MERGE_TILE = 512
DEST_TILE = 2048
COMBINE_TILE = 512


def _sigmoid(x):
    return 1.0 / (1.0 + jnp.exp(-x))


def _silu(x):
    return x * _sigmoid(x)


def _rms(x, g):
    return x * lax.rsqrt(jnp.mean(x * x, axis=-1, keepdims=True) + RMS_EPS) * g


def _pack_halves(x):
    n = x.shape[1] // 2
    bits = lax.bitcast_convert_type(x.astype(BF16).astype(F32), U32)
    return (bits[:, :n] >> 16) | (bits[:, n:] & jnp.uint32(0xFFFF0000))


def _unpack_halves(word):
    lo = lax.bitcast_convert_type(word << 16, F32)
    hi = lax.bitcast_convert_type(word & jnp.uint32(0xFFFF0000), F32)
    return lo, hi


def _params(n_axes=1):
    return pltpu.CompilerParams(
        dimension_semantics=("arbitrary",) * n_axes, vmem_limit_bytes=VMEM_LIMIT_BYTES)


def _ada_kernel(c_ref, w_ref, b_ref, o_ref):
    sc = _silu(c_ref[...])
    o_ref[...] = jnp.dot(sc, w_ref[...], preferred_element_type=F32, precision=HIGHEST) + b_ref[...]


def _ada(c, w, b):
    bsz, d = c.shape
    n = w.shape[1]
    return pl.pallas_call(
        _ada_kernel,
        out_shape=jax.ShapeDtypeStruct((bsz, n), F32),
        grid=(n // d,),
        in_specs=[pl.BlockSpec((bsz, d), lambda j: (0, 0)),
                  pl.BlockSpec((d, d), lambda j: (0, j)),
                  pl.BlockSpec((1, d), lambda j: (0, j))],
        out_specs=pl.BlockSpec((bsz, d), lambda j: (0, j)),
        compiler_params=_params(),
        name="ada_mod",
    )(c, w, b.reshape(1, n))


def _inproj_kernel(n_flat, flat_ranges, att_c0, x_ref, g_ref, sc_ref, sh_ref, w_ref, *refs):
    flat_refs, att_refs, scr = refs[:n_flat], refs[n_flat:-1], refs[-1]
    tm = x_ref.shape[0]
    h = _rms(x_ref[...], g_ref[...]) * (1.0 + sc_ref[0]) + sh_ref[0]
    hb = h.astype(BF16)
    for (c0, c1), o_ref in zip(flat_ranges, flat_refs):
        o_ref[...] = jnp.dot(hb, w_ref[:, c0:c1], preferred_element_type=F32).astype(o_ref.dtype)
    gw = ATT_HEADS_PER_GROUP * ATT_HEAD_DIM
    n_groups = len(ATT_GROUPS)
    for part in range(3):
        c0 = att_c0 + part * n_groups * gw
        res = jnp.dot(hb, w_ref[:, c0:c0 + n_groups * gw], preferred_element_type=F32)
        if part == 0:
            res = res * (ATT_HEAD_DIM ** -0.5)
        for g, (_, dil) in enumerate(ATT_GROUPS):
            o_ref = att_refs[g * 3 + part]
            sub = res[:, g * gw:(g + 1) * gw]
            if dil == 1:
                o_ref[0, 0] = sub.astype(BF16)
            else:
                for c in range(gw // LANES):
                    scr[c] = sub[:, c * LANES:(c + 1) * LANES]
                for r in range(dil):
                    o_ref[0, r] = jnp.concatenate(
                        [scr[c, pl.ds(r, tm // dil, stride=dil), :] for c in range(gw // LANES)],
                        axis=1).astype(BF16)


def _inproj(x2, g, scale, shift, w_bf16, bsz, seq, flat_segs, att_c0, tm):
    t, d = x2.shape
    gw = ATT_HEADS_PER_GROUP * ATT_HEAD_DIM
    n_per = seq // tm
    per_b = lambda i: (i // n_per, 0, 0)
    att_shapes, att_specs = [], []
    for _, dil in ATT_GROUPS:
        for _ in range(3):
            att_shapes.append(jax.ShapeDtypeStruct((bsz, dil, seq // dil, gw), BF16))
            att_specs.append(pl.BlockSpec((1, dil, tm // dil, gw),
                                          lambda i: (i // n_per, 0, i % n_per, 0)))
    outs = pl.pallas_call(
        functools.partial(_inproj_kernel, len(flat_segs),
                          tuple((c0, c0 + wdt) for c0, wdt, _ in flat_segs), att_c0),
        out_shape=[jax.ShapeDtypeStruct((t, wdt), dt) for _, wdt, dt in flat_segs] + att_shapes,
        grid=(t // tm,),
        in_specs=[pl.BlockSpec((tm, d), lambda i: (i, 0)),
                  pl.BlockSpec((1, d), lambda i: (0, 0)),
                  pl.BlockSpec((1, 1, d), per_b),
                  pl.BlockSpec((1, 1, d), per_b),
                  pl.BlockSpec(w_bf16.shape, lambda i: (0, 0), pipeline_mode=pl.Buffered(1))],
        out_specs=[pl.BlockSpec((tm, wdt), lambda i: (i, 0)) for _, wdt, _ in flat_segs]
        + att_specs,
        scratch_shapes=[pltpu.VMEM((gw // LANES, tm, LANES), F32)],
        compiler_params=_params(),
        name="in_proj",
    )(x2, g.reshape(1, d), scale, shift, w_bf16)
    return outs[:len(flat_segs)], outs[len(flat_segs):]


def _hgrn_kernel(ts, q_ref, f_ref, v_ref, gt_ref, lb_ref, ng_ref, o_ref, st_ref, b_ref):
    dk = q_ref.shape[1] // HG_HEADS
    n_chunks = ts // HG_CHUNK
    n_blk = HG_CHUNK // HG_BLOCK

    @pl.when(pl.program_id(1) == 0)
    def _():
        st_ref[...] = jnp.zeros_like(st_ref)

    row = lax.broadcasted_iota(I32, (LANES, LANES), 0)
    col = lax.broadcasted_iota(I32, (LANES, LANES), 1)
    same_chunk = (row // HG_CHUNK) == (col // HG_CHUNK)
    cum_mat = jnp.where(same_chunk & (col <= row), 1.0, 0.0).astype(BF16)

    def chunk_cumsum(x):
        out = []
        for r0 in range(0, ts, LANES):
            rest = x[r0:r0 + LANES]
            acc = None
            for _ in range(3):
                term = rest.astype(BF16)
                part = jnp.dot(cum_mat, term, preferred_element_type=F32)
                acc = part if acc is None else acc + part
                rest = rest - term.astype(F32)
            out.append(acc)
        return jnp.concatenate(out, axis=0)

    def forget(cs):
        lb = lb_ref[:, cs]
        return lb + (1.0 - lb) * _sigmoid(f_ref[:, cs])

    b_min = None
    for h in range(HG_HEADS):
        cs = slice(h * dk, (h + 1) * dk)
        b = chunk_cumsum(jnp.log(forget(cs)))
        b_ref[:, cs] = b
        m = jnp.min(b)
        b_min = m if b_min is None else jnp.minimum(b_min, m)
    mild = b_min >= HG_MILD_DECAY

    def finish(h, o, st):
        cs = slice(h * dk, (h + 1) * dk)
        st_ref[h] = st
        y = _rms(o, ng_ref[:, cs]) * _silu(gt_ref[:, cs].astype(F32))
        o_ref[:, cs] = y.astype(o_ref.dtype)

    @pl.when(mild)
    def _():
        span = 2 * HG_CHUNK
        causal = (lax.broadcasted_iota(I32, (span, span), 0)
                  >= lax.broadcasted_iota(I32, (span, span), 1))
        nt = lambda x, y: lax.dot_general(x, y, (((1,), (1,)), ((), ())),
                                          preferred_element_type=F32)
        for h in range(HG_HEADS):
            cs = slice(h * dk, (h + 1) * dk)
            v = v_ref[:, cs]
            b = b_ref[:, cs]
            q = q_ref[:, cs].astype(F32)
            k = 1.0 - forget(cs)
            st = st_ref[h]
            o_rows = []
            for r0 in range(0, ts, span):
                sl = slice(r0, r0 + span)
                b_first, b_second = b[r0:r0 + HG_CHUNK], b[r0 + HG_CHUNK:r0 + span]
                end_first = b_first[HG_CHUNK - 1:HG_CHUNK]
                end_second = b_second[HG_CHUNK - 1:HG_CHUNK]
                e = jnp.exp(jnp.concatenate([b_first - end_first, b_second], axis=0))
                qe = (q[sl] * e).astype(BF16)
                ke = k[sl] / e
                a = jnp.where(causal, nt(qe, ke.astype(BF16)), 0.0).astype(BF16)
                st_in = (st * jnp.exp(end_first)).astype(BF16)
                o_rows.append(jnp.dot(a, v[sl], preferred_element_type=F32) + nt(qe, st_in))
                kend = (ke * jnp.exp(end_second)).astype(BF16)
                vt = v[sl].astype(F32).T.astype(BF16)
                st = (st * jnp.exp(end_first + end_second)
                      + jnp.dot(vt, kend, preferred_element_type=F32))
            finish(h, jnp.concatenate(o_rows, axis=0), st)

    @pl.when(jnp.logical_not(mild))
    def _():
        _hgrn_steep(ts, dk, n_chunks, n_blk, q_ref, v_ref, b_ref, st_ref, forget, finish)


def _hgrn_steep(ts, dk, n_chunks, n_blk, q_ref, v_ref, b_ref, st_ref, forget, finish):
    t_in_blk = lax.broadcasted_iota(I32, (ts, dk), 0) % HG_BLOCK

    for h in range(HG_HEADS):
        cs = slice(h * dk, (h + 1) * dk)
        q = q_ref[:, cs].astype(F32)
        v = v_ref[:, cs].astype(F32)
        k = 1.0 - forget(cs)
        b = b_ref[:, cs]

        o = jnp.sum(q * k, axis=-1, keepdims=True) * v
        for d in range(1, HG_BLOCK):
            k_d = pltpu.roll(k, d, axis=0)
            b_d = pltpu.roll(b, d, axis=0)
            v_d = pltpu.roll(v, d, axis=0)
            w = jnp.sum(q * k_d * jnp.exp(jnp.minimum(b - b_d, 0.0)), axis=-1, keepdims=True)
            o = o + jnp.where(t_in_blk >= d, w * v_d, 0.0)

        st = st_ref[h]
        o_rows = []
        for c in range(n_chunks):
            r0 = c * HG_CHUNK
            bc = b[r0:r0 + HG_CHUNK]
            qc = q[r0:r0 + HG_CHUNK]
            kc = k[r0:r0 + HG_CHUNK]
            vc = v[r0:r0 + HG_CHUNK].astype(BF16)
            st_b = st.astype(BF16)
            for i in range(n_blk):
                i0 = i * HG_BLOCK
                if i == 0:
                    qt = qc[:HG_BLOCK] * jnp.exp(bc[:HG_BLOCK])
                    qs = qt
                else:
                    ref_row = bc[i0 - 1:i0]
                    qt = qc[i0:i0 + HG_BLOCK] * jnp.exp(bc[i0:i0 + HG_BLOCK] - ref_row)
                    qs = qt * jnp.exp(ref_row)
                oi = lax.dot_general(qs.astype(BF16), st_b, (((1,), (1,)), ((), ())),
                                     preferred_element_type=F32)
                if i > 0:
                    kh = kc[:i0] * jnp.exp(ref_row - bc[:i0])
                    a = lax.dot_general(qt.astype(BF16), kh.astype(BF16), (((1,), (1,)), ((), ())),
                                        preferred_element_type=F32)
                    oi = oi + jnp.dot(a.astype(BF16), vc[:i0], preferred_element_type=F32)
                o_rows.append(oi)
            b_end = bc[HG_CHUNK - 1:HG_CHUNK]
            kend = kc * jnp.exp(b_end - bc)
            vt = v[r0:r0 + HG_CHUNK].T.astype(BF16)
            st = st * jnp.exp(b_end) + jnp.dot(vt, kend.astype(BF16), preferred_element_type=F32)
        finish(h, o + jnp.concatenate(o_rows, axis=0), st)


def _hgrn(hq, hf, hi, hg, lb, ng, bsz, seq, ts):
    t, w = hq.shape
    dk = w // HG_HEADS
    n_s = seq // ts
    tile = lambda b, s: (b * n_s + s, 0)
    return pl.pallas_call(
        functools.partial(_hgrn_kernel, ts),
        out_shape=jax.ShapeDtypeStruct((t, w), BF16),
        grid=(bsz, n_s),
        in_specs=[pl.BlockSpec((ts, w), tile)] * 4
        + [pl.BlockSpec((1, w), lambda b, s: (0, 0))] * 2,
        out_specs=pl.BlockSpec((ts, w), tile),
        scratch_shapes=[pltpu.VMEM((HG_HEADS, dk, dk), F32), pltpu.VMEM((ts, w), F32)],
        compiler_params=_params(2),
        name="hgrn2",
    )(hq, hf, hi, hg, lb.reshape(1, w), ng.reshape(1, w))


def _attn_kernel(nk, nq, nr, q_ref, kp_ref, kc_ref, vp_ref, vc_ref, o_ref, lse_ref):
    n = pl.program_id(2)
    e = ATT_HEAD_DIM
    i = lax.broadcasted_iota(I32, (nk, 2 * nk), 0)
    j = lax.broadcasted_iota(I32, (nk, 2 * nk), 1)
    band = (j >= i) & (j <= i + nk)
    first_head = lax.broadcasted_iota(I32, (nk, LANES), 1) < e
    zero = jnp.zeros((), q_ref.dtype)
    for r in range(nr):
        kk = jnp.concatenate([kp_ref[0, r], kc_ref[0, r]], axis=0)
        vv = jnp.concatenate([vp_ref[0, r], vc_ref[0, r]], axis=0)
        for b in range(nq):
            valid = band & ((j >= nk) | (n * nq + b > 0))
            rows = slice(b * nk, (b + 1) * nk)
            for c in range(0, ATT_HEADS_PER_GROUP * e, LANES):
                q = q_ref[0, r, rows, c:c + LANES]
                kb = kk[b * nk:(b + 2) * nk, c:c + LANES]
                vb = vv[b * nk:(b + 2) * nk, c:c + LANES]
                outs, lses = [], []
                for keep in (first_head, jnp.logical_not(first_head)):
                    s = lax.dot_general(jnp.where(keep, q, zero), kb, (((1,), (1,)), ((), ())),
                                        preferred_element_type=F32)
                    s = jnp.where(valid, s, -jnp.inf)
                    m = jnp.max(s, axis=-1, keepdims=True)
                    p = jnp.exp(s - m)
                    l = jnp.sum(p, axis=-1, keepdims=True)
                    outs.append(jnp.dot(p.astype(BF16), vb, preferred_element_type=F32) / l)
                    lses.append(m + jnp.log(l))
                o_ref[0, r, rows, c:c + LANES] = jnp.where(first_head, outs[0], outs[1])
                lse_ref[0, r, rows, c:c + LANES] = jnp.where(first_head, lses[0], lses[1])


def _attn_group(q, k, v, g, blocks_per_step):
    window, dil = ATT_GROUPS[g]
    nk = window // dil
    bsz, _, ln, gw = q.shape
    nq = min(blocks_per_step, ln // nk)
    nr = min(blocks_per_step // nq, dil)
    assert ln % (nk * nq) == 0 and dil % nr == 0 and 2 * ATT_HEAD_DIM == LANES
    cur = pl.BlockSpec((1, nr, nq * nk, gw), lambda b, r, n: (b, r, n, 0))
    prev = pl.BlockSpec((1, nr, nk, gw), lambda b, r, n: (b, r, jnp.maximum(n * nq - 1, 0), 0))
    return pl.pallas_call(
        functools.partial(_attn_kernel, nk, nq, nr),
        out_shape=[jax.ShapeDtypeStruct(q.shape, F32)] * 2,
        grid=(bsz, dil // nr, ln // (nk * nq)),
        in_specs=[cur, prev, cur, prev, cur],
        out_specs=[cur, cur],
        compiler_params=_params(3),
        name=f"dilated_attn_g{g}",
    )(q, k, k, v, v)


def _token_major(ref, scr):
    dil, rows = ref.shape[1], ref.shape[2]
    if dil == 1:
        return ref[0, 0]
    n_col = scr.shape[0]
    for r in range(dil):
        for c in range(n_col):
            scr[c, pl.ds(r, rows, stride=dil), :] = ref[0, r, :, c * LANES:(c + 1) * LANES]
    return jnp.concatenate([scr[c] for c in range(n_col)], axis=1)


def _merge_kernel(ya_ref, o0_ref, o1_ref, o2_ref, l0_ref, l1_ref, l2_ref, ga_ref, gb_ref, x_ref,
                  g1_ref, sc2_ref, sh2_ref, g2_ref, n2_ref, wa_ref, wb_ref, wo_ref, wr_ref, wrl_ref,
                  wsg_ref, wsu_ref, wsd_ref, bias_ref, x1_ref, hp_ref, idx_ref, gate_ref, rank_ref,
                  cnt_ref, carry_ref, lg_ref, *scr):
    step = pl.program_id(0)

    @pl.when(step == 0)
    def _():
        carry_ref[...] = jnp.zeros_like(carry_ref)
        lg_ref[...] = jnp.zeros_like(lg_ref)

    _route(lg_ref[...], jnp.where(step > 0, 1.0, 0.0), bias_ref, idx_ref, gate_ref, rank_ref,
           cnt_ref, carry_ref)

    l0, l1, l2 = (_token_major(r, s) for r, s in zip((l0_ref, l1_ref, l2_ref), scr[:3]))
    o0, o1, o2 = (_token_major(r, s) for r, s in zip((o0_ref, o1_ref, o2_ref), scr[3:]))
    m = jnp.maximum(jnp.maximum(l0, l1), l2)
    e0, e1, e2 = jnp.exp(l0 - m), jnp.exp(l1 - m), jnp.exp(l2 - m)
    yb = (e0 * o0 + e1 * o1 + e2 * o2) / (e0 + e1 + e2)
    merged = (_sigmoid(ga_ref[...].astype(F32))
              * jnp.dot(ya_ref[...], wa_ref[...], preferred_element_type=F32)
              + _sigmoid(gb_ref[...].astype(F32))
              * jnp.dot(yb.astype(BF16), wb_ref[...], preferred_element_type=F32))
    x1 = x_ref[...] + g1_ref[0] * jnp.dot(merged.astype(BF16), wo_ref[...],
                                           preferred_element_type=F32)
    h2 = _rms(x1, n2_ref[...]) * (1.0 + sc2_ref[0]) + sh2_ref[0]
    hb = h2.astype(BF16)
    act = (_silu(jnp.dot(hb, wsg_ref[...], preferred_element_type=F32))
           * jnp.dot(hb, wsu_ref[...], preferred_element_type=F32))
    shared = jnp.dot(act.astype(BF16), wsd_ref[...], preferred_element_type=F32)
    x1_ref[...] = x1 + g2_ref[0] * shared
    hp_ref[...] = _pack_halves(h2)
    h_lo = (h2 - hb.astype(F32)).astype(BF16)
    nt = lambda a, b: lax.dot_general(a, b, (((1,), (1,)), ((), ())), preferred_element_type=F32)
    lg_ref[...] = nt(wr_ref[...], hb) + (nt(wr_ref[...], h_lo) + nt(wrl_ref[...], hb))


def _merge(ya, att, ga, gb, x2, gate1, scale2, shift2, gate2, norm2_g, wa, wb, wo, wr_t, wsg, wsu,
           wsd, router_bias, seq, tm):
    t, d = x2.shape
    n_e = wr_t.shape[0]
    wr_hi = wr_t.astype(BF16)
    wr_lo = (wr_t - wr_hi.astype(F32)).astype(BF16)
    n_per = seq // tm
    n_tiles = t // tm
    tile = lambda i: jnp.minimum(i, n_tiles - 1)
    per_b = lambda i: (tile(i) // n_per, 0, 0)
    rows = lambda wdt: pl.BlockSpec((tm, wdt), lambda i: (tile(i), 0))
    full = lambda a: pl.BlockSpec(a.shape, lambda i: (0,) * a.ndim)
    vec = pl.BlockSpec((1, 1, d), per_b)
    (o0, l0), (o1, l1), (o2, l2) = att
    gw = o0.shape[3]
    by_residue = lambda a: pl.BlockSpec((1, a.shape[1], tm // a.shape[1], gw),
                                        lambda i: (tile(i) // n_per, 0, tile(i) % n_per, 0))
    att_in = (o0, o1, o2, l0, l1, l2)
    bias_col = router_bias.reshape(n_e, 1)
    tok = pl.BlockSpec((TOP_K, tm), lambda i: (0, jnp.maximum(i - 1, 0)))
    return pl.pallas_call(
        _merge_kernel,
        out_shape=[jax.ShapeDtypeStruct((t, d), F32),
                   jax.ShapeDtypeStruct((t, d // 2), U32),
                   jax.ShapeDtypeStruct((TOP_K, t), I32), jax.ShapeDtypeStruct((TOP_K, t), F32),
                   jax.ShapeDtypeStruct((TOP_K, t), I32), jax.ShapeDtypeStruct((n_e, LANES), I32)],
        grid=(n_tiles + 1,),
        in_specs=[rows(ya.shape[1])] + [by_residue(a) for a in att_in] + [rows(d)] * 3
        + [vec, vec, vec, vec, pl.BlockSpec((1, d), lambda i: (0, 0))]
        + [full(a) for a in (wa, wb, wo, wr_hi, wr_lo, wsg, wsu, wsd, bias_col)],
        out_specs=[rows(d), rows(d // 2), tok, tok, tok,
                   pl.BlockSpec((n_e, LANES), lambda i: (0, 0))],
        scratch_shapes=[pltpu.VMEM((n_e, 1), F32), pltpu.VMEM((n_e, tm), F32)]
        + [pltpu.VMEM((gw // LANES, tm, LANES), F32)] * 6,
        compiler_params=_params(),
        name="merge_router",
    )(ya, *att_in, ga, gb, x2, gate1, scale2, shift2, gate2,
      norm2_g.reshape(1, d), wa, wb, wo, wr_hi, wr_lo, wsg, wsu, wsd, bias_col)


def _route(logits, live, bias_ref, idx_ref, gate_ref, rank_ref, cnt_ref, carry_ref):
    n_e, tt = logits.shape
    scores = _sigmoid(logits)
    sel = scores + bias_ref[...]
    eio = lax.broadcasted_iota(I32, (n_e, tt), 0)
    picked = jnp.zeros((n_e, tt), F32)
    idxs, vals = [], []
    for _ in range(TOP_K):
        m = jnp.max(sel, axis=0, keepdims=True)
        ik = jnp.min(jnp.where(sel == m, eio, n_e), axis=0, keepdims=True)
        hit = eio == ik
        vals.append(jnp.sum(jnp.where(hit, scores, 0.0), axis=0, keepdims=True))
        sel = jnp.where(hit, -jnp.inf, sel)
        picked = picked + jnp.where(hit, 1.0, 0.0)
        idxs.append(ik)
    denom = vals[0]
    for v in vals[1:]:
        denom = denom + v
    gate_ref[...] = jnp.concatenate([v / denom * ROUTE_SCALE for v in vals], axis=0)
    idx_ref[...] = jnp.concatenate(idxs, axis=0)

    upper = (lax.broadcasted_iota(I32, (tt, tt), 0) <= lax.broadcasted_iota(I32, (tt, tt), 1))
    incl = jnp.dot(picked.astype(BF16), jnp.where(upper, 1.0, 0.0).astype(BF16),
                   preferred_element_type=F32)
    before = incl - picked + carry_ref[...]
    rank_ref[...] = jnp.concatenate(
        [jnp.sum(jnp.where(eio == ik, before, 0.0), axis=0, keepdims=True) for ik in idxs],
        axis=0).astype(I32)
    carry_ref[...] = carry_ref[...] + jnp.sum(picked, axis=1, keepdims=True) * live
    cnt_ref[...] = jnp.broadcast_to(carry_ref[...], cnt_ref.shape).astype(I32)


def _dest_kernel(idx_ref, rank_ref, start_ref, o_ref):
    k, tt = idx_ref.shape
    n_e = start_ref.shape[0]
    eio = lax.broadcasted_iota(I32, (n_e, tt), 0)
    start = start_ref[...]
    rows = [jnp.sum(jnp.where(eio == idx_ref[r:r + 1, :], start, 0), axis=0, keepdims=True)
            for r in range(k)]
    o_ref[...] = jnp.concatenate(rows, axis=0) + rank_ref[...]


def _dest(idx, rank, seg_start, tt):
    k, t = idx.shape
    n_e = seg_start.shape[0]
    tok = pl.BlockSpec((k, tt), lambda i: (0, i))
    return pl.pallas_call(
        _dest_kernel,
        out_shape=jax.ShapeDtypeStruct((k, t), I32),
        grid=(t // tt,),
        in_specs=[tok, tok, pl.BlockSpec((n_e, 1), lambda i: (0, 0))],
        out_specs=tok,
        compiler_params=_params(),
        name="moe_dest",
    )(idx, rank, seg_start.reshape(n_e, 1))


def _sc_mesh():
    return plsc.VectorSubcoreMesh(core_axis_name="core", subcore_axis_name="subcore")


def _sc_scatter_rows(rows, dest, n_out):
    k, t = dest.shape
    w = rows.shape[1]
    mesh = _sc_mesh()
    n_workers = mesh.num_cores * mesh.num_subcores
    win_per_worker = t // (SC_WINDOW * n_workers)
    assert win_per_worker * SC_WINDOW * n_workers == t

    @functools.partial(
        pl.kernel, out_type=jax.ShapeDtypeStruct((n_out, w), rows.dtype), mesh=mesh,
        scratch_types=[pltpu.VMEM((SC_WINDOW, w), rows.dtype)]
        + [pltpu.VMEM((1, SC_WINDOW), I32)] * k + [pltpu.SemaphoreType.DMA],
        name="moe_dispatch_sc")
    def run(rows_hbm, idx_hbm, out_hbm, rows_v, *rest):
        idx_v, sem = rest[:k], rest[k]
        worker = lax.axis_index("subcore") * mesh.num_cores + lax.axis_index("core")

        @pl.loop(0, win_per_worker)
        def _(j):
            t0 = pl.multiple_of((worker * win_per_worker + j) * SC_WINDOW, SC_WINDOW)
            loads = [pltpu.async_copy(rows_hbm.at[pl.ds(t0, SC_WINDOW)], rows_v, sem)]
            loads += [pltpu.async_copy(idx_hbm.at[:, pl.ds(r * t + t0, SC_WINDOW)], idx_v[r], sem)
                      for r in range(k)]
            for c in loads:
                c.wait()
            copies = [pltpu.async_copy(rows_v, out_hbm.at[idx_v[r].at[0]], sem) for r in range(k)]
            for c in copies:
                c.wait()

    return run(rows, dest.reshape(1, k * t))


def _sc_gather_rows(table, dest):
    k, t = dest.shape
    w = table.shape[1]
    mesh = _sc_mesh()
    n_workers = mesh.num_cores * mesh.num_subcores
    win_per_worker = (k * t) // (SC_WINDOW * n_workers)
    assert win_per_worker * SC_WINDOW * n_workers == k * t and win_per_worker % 2 == 0

    @functools.partial(
        pl.kernel, out_type=jax.ShapeDtypeStruct((k * t, w), table.dtype), mesh=mesh,
        scratch_types=[pltpu.VMEM((SC_WINDOW, w), table.dtype), pltpu.VMEM((1, SC_WINDOW), I32),
                       pltpu.VMEM((1, SC_WINDOW), I32), pltpu.SemaphoreType.DMA],
        name="moe_gather_sc")
    def run(table_hbm, idx_hbm, out_hbm, rows_v, idx_a, idx_b, sem):
        worker = lax.axis_index("subcore") * mesh.num_cores + lax.axis_index("core")

        def window(j):
            return pl.ds(pl.multiple_of((worker * win_per_worker + j) * SC_WINDOW, SC_WINDOW),
                         SC_WINDOW)

        def idx_load(j, buf):
            return pltpu.make_async_copy(idx_hbm.at[:, window(j)], buf, sem)

        idx_load(0, idx_a).start()

        @pl.loop(0, win_per_worker, step=2)
        def _(j0):
            for b, (cur, nxt) in enumerate(((idx_a, idx_b), (idx_b, idx_a))):
                j = j0 + b
                idx_load(j, cur).wait()

                @pl.when(j + 1 < win_per_worker)
                def _():
                    idx_load(j + 1, nxt).start()

                pltpu.sync_copy(table_hbm.at[cur.at[0]], rows_v)
                pltpu.sync_copy(rows_v, out_hbm.at[window(j)])

    return run(table, dest.reshape(1, k * t))


def _expert_kernel(start_ref, nblk_ref, xs_ref, wg_ref, wu_ref, wd_ref, ys_ref,
                   xbuf, ybuf, wgb, wub, wdb, wbuf_g, wbuf_u, wbuf_d, sem_in, sem_out, sem_w):
    wbuf = (wbuf_g, wbuf_u, wbuf_d)
    e = pl.program_id(0)
    n_e = pl.num_programs(0)
    nb = nblk_ref[e]
    g0 = start_ref[e] // MOE_BLOCK
    n_used = start_ref[n_e - 1] // MOE_BLOCK + nblk_ref[n_e - 1]
    n_in, n_out = xbuf.shape[0], ybuf.shape[0]

    def rows(g):
        return pl.ds(pl.multiple_of(g * MOE_BLOCK, MOE_BLOCK), MOE_BLOCK)

    def in_copy(g):
        slot = lax.rem(g, n_in)
        return pltpu.make_async_copy(xs_ref.at[rows(g), :], xbuf.at[slot], sem_in.at[slot])

    def out_copy(g):
        slot = lax.rem(g, n_out)
        return pltpu.make_async_copy(ybuf.at[slot], ys_ref.at[rows(g), :], sem_out.at[slot])

    look = n_in - EXPERT_GROUP

    @pl.when(e == 0)
    def _():
        for g in range(look):
            @pl.when(g < n_used)
            def _():
                in_copy(g).start(priority=g % N_DMA_QUEUES)

    n_w = wbuf[0].shape[0]

    def weight_copies(ex):
        slot = lax.rem(ex, n_w)
        return [pltpu.make_async_copy(src.at[ex], buf.at[slot], sem_w.at[slot])
                for src, buf in zip((wg_ref, wu_ref, wd_ref), wbuf)]

    @pl.when(e == 0)
    def _():
        for ex in range(min(n_w, wg_ref.shape[0])):
            for c in weight_copies(ex):
                c.start()

    for c in weight_copies(e):
        c.wait()
    w_slot = lax.rem(e, n_w)

    @pl.when(nb > 0)
    def _():
        wgb[...] = wbuf[0][w_slot].astype(BF16)
        wub[...] = wbuf[1][w_slot].astype(BF16)
        wdb[...] = wbuf[2][w_slot].astype(BF16)

    @pl.when(e + n_w < n_e)
    def _():
        for c in weight_copies(e + n_w):
            c.start()

    @pl.when(nb > 0)
    def _():
        def swiglu(word):
            lo, hi = _unpack_halves(word)
            x = jnp.concatenate([lo.astype(BF16), hi.astype(BF16)], axis=1)
            gate = jnp.dot(x, wgb[...], preferred_element_type=F32)
            up = jnp.dot(x, wub[...], preferred_element_type=F32)
            act = (_silu(gate) * up).astype(BF16)
            return jnp.dot(act, wdb[...], preferred_element_type=F32)

        def process(g, m):
            for i in range(m):
                in_copy(g + i).wait()
            for i in range(m):
                @pl.when(g + look + i < n_used)
                def _():
                    in_copy(g + look + i).start(priority=i % N_DMA_QUEUES)
            y_all = swiglu(jnp.concatenate([xbuf[lax.rem(g + i, n_in)] for i in range(m)], axis=0))
            ys = [y_all[i * MOE_BLOCK:(i + 1) * MOE_BLOCK] for i in range(m)]
            for i in range(m):
                @pl.when(g + i >= n_out)
                def _():
                    out_copy(g + i - n_out).wait()

                ybuf[lax.rem(g + i, n_out)] = _pack_halves(ys[i])
                out_copy(g + i).start(priority=(i + 1) % N_DMA_QUEUES)

        def group_body(p, carry):
            process(g0 + p * EXPERT_GROUP, EXPERT_GROUP)
            return carry

        lax.fori_loop(0, nb // EXPERT_GROUP, group_body, 0)
        for m in range(1, EXPERT_GROUP):
            @pl.when(lax.rem(nb, EXPERT_GROUP) == m)
            def _():
                process(g0 + nb - m, m)

    @pl.when(e == n_e - 1)
    def _():
        for i in range(n_out):
            @pl.when(n_used - 1 - i >= 0)
            def _():
                out_copy(n_used - 1 - i).wait()


def _experts(seg_start, seg_blocks, xs, wg, wu, wd):
    n_slots, half = xs.shape
    n_e, d, de = wg.shape
    n_w = EXPERT_WEIGHT_BUFFERS
    return pl.pallas_call(
        _expert_kernel,
        out_shape=jax.ShapeDtypeStruct((n_slots, half), U32),
        grid_spec=pltpu.PrefetchScalarGridSpec(
            num_scalar_prefetch=2,
            grid=(n_e,),
            in_specs=[pl.BlockSpec(memory_space=pl.ANY)] * 4,
            out_specs=pl.BlockSpec(memory_space=pl.ANY),
            scratch_shapes=[pltpu.VMEM((EXPERT_IN_RING, MOE_BLOCK, half), U32),
                            pltpu.VMEM((EXPERT_OUT_RING, MOE_BLOCK, half), U32),
                            pltpu.VMEM((d, de), BF16), pltpu.VMEM((d, de), BF16),
                            pltpu.VMEM((de, d), BF16),
                            pltpu.VMEM((n_w, d, de), F32), pltpu.VMEM((n_w, d, de), F32),
                            pltpu.VMEM((n_w, de, d), F32),
                            pltpu.SemaphoreType.DMA((EXPERT_IN_RING,)),
                            pltpu.SemaphoreType.DMA((EXPERT_OUT_RING,)),
                            pltpu.SemaphoreType.DMA((n_w,))]),
        compiler_params=_params(),
        name="moe_experts",
    )(seg_start, seg_blocks, xs, wg, wu, wd)


def _combine_kernel(yg_ref, gt_ref, x_ref, g2_ref, fg_ref, o_ref):
    k, tc = gt_ref.shape
    eye = jnp.where(lax.broadcasted_iota(I32, (tc, tc), 0) == lax.broadcasted_iota(I32, (tc, tc), 1),
                    1.0, 0.0).astype(BF16)
    rest, gt = gt_ref[...], None
    for _ in range(3):
        term = rest.astype(BF16)
        part = lax.dot_general(eye, term, (((1,), (1,)), ((), ())), preferred_element_type=F32)
        gt = part if gt is None else gt + part
        rest = rest - term.astype(F32)
    lo, hi = _unpack_halves(yg_ref[0])
    y_lo, y_hi = lo * gt[:, 0:1], hi * gt[:, 0:1]
    for r in range(1, k):
        lo, hi = _unpack_halves(yg_ref[r])
        y_lo, y_hi = y_lo + lo * gt[:, r:r + 1], y_hi + hi * gt[:, r:r + 1]
    y = jnp.concatenate([y_lo, y_hi], axis=1)
    o_ref[...] = _rms(x_ref[...] + g2_ref[0] * y, fg_ref[...])


def _combine_into_kernel(yg_ref, gt_ref, x_ref, g2_ref, fg_ref, prev_ref, o_ref):
    del prev_ref
    _combine_kernel(yg_ref, gt_ref, x_ref, g2_ref, fg_ref, o_ref)


def _combine(yg, tok0, gates, x1s, gate2, final_g, seq, tc, out_so_far=None):
    k, n, half = yg.shape
    t, d = x1s.shape
    assert n % tc == 0 and tok0 % tc == 0
    b0 = tok0 // tc
    args = [yg, gates, x1s, gate2, final_g.reshape(1, d)]
    in_specs = [pl.BlockSpec((k, tc, half), lambda i: (0, i, 0)),
                pl.BlockSpec((k, tc), lambda i: (0, i + b0)),
                pl.BlockSpec((tc, d), lambda i: (i + b0, 0)),
                pl.BlockSpec((1, 1, d), lambda i: (((i + b0) * tc) // seq, 0, 0)),
                pl.BlockSpec((1, d), lambda i: (0, 0))]
    aliases = {}
    kernel = _combine_kernel
    if out_so_far is not None:
        args.append(out_so_far)
        in_specs.append(pl.BlockSpec(memory_space=pl.ANY))
        aliases = {len(args) - 1: 0}
        kernel = _combine_into_kernel
    return pl.pallas_call(
        kernel,
        out_shape=jax.ShapeDtypeStruct((t, d), F32),
        grid=(n // tc,),
        in_specs=in_specs,
        out_specs=pl.BlockSpec((tc, d), lambda i: (i + b0, 0)),
        input_output_aliases=aliases,
        compiler_params=_params(),
        name="moe_combine",
    )(*args)


def _layer(x2, c, bsz, seq, lb_row, ada_w, ada_b, norm1_g, w_in, hg_norm_g, w_branch_a, w_branch_b,
           w_out, norm2_g, w_router, router_bias, w_exp_gate, w_exp_up, w_exp_down, w_sh_gate,
           w_sh_up, w_sh_down, final_g):
    t, d = x2.shape
    n_e = w_router.shape[1]
    mod = _ada(c, ada_w, ada_b).reshape(bsz, 6, 1, d)
    shift1, scale1, gate1, shift2, scale2, gate2 = (mod[:, j] for j in range(6))

    hw = hg_norm_g.shape[0]
    aw = len(ATT_GROUPS) * ATT_HEADS_PER_GROUP * ATT_HEAD_DIM
    flat_segs = [(0, hw, BF16), (hw, hw, F32), (2 * hw, hw, BF16), (3 * hw, hw, BF16),
                 (4 * hw + 3 * aw, d, BF16), (4 * hw + 3 * aw + d, d, BF16)]
    (hq, hf, hi, hg, ga, gb), qkv = _inproj(
        x2, norm1_g, scale1, shift1, w_in.astype(BF16), bsz, seq, flat_segs, 4 * hw,
        tm=IN_PROJ_TILE)

    ya = _hgrn(hq, hf, hi, hg, lb_row, hg_norm_g, bsz, seq, ts=HGRN_TILE)
    att = [_attn_group(*qkv[3 * g:3 * g + 3], g, blocks_per_step=ATT_BLOCKS_PER_STEP)
           for g in range(len(ATT_GROUPS))]

    x1s, hp, idx, gates, rank, cnt = _merge(
        ya, att, ga, gb, x2, gate1, scale2, shift2, gate2, norm2_g, w_branch_a.astype(BF16),
        w_branch_b.astype(BF16), w_out.astype(BF16), w_router.T, w_sh_gate.astype(BF16),
        w_sh_up.astype(BF16), w_sh_down.astype(BF16), router_bias, seq, tm=MERGE_TILE)
    counts = cnt[:, 0]
    padded = (counts + MOE_BLOCK - 1) // MOE_BLOCK * MOE_BLOCK
    seg_start = (jnp.cumsum(padded) - padded).astype(I32)
    n_blocks = -(-(t * TOP_K) // MOE_BLOCK) + n_e
    dest = _dest(idx, rank, seg_start, tt=DEST_TILE)

    xs = _sc_scatter_rows(hp, dest, n_blocks * MOE_BLOCK)
    ys = _experts(seg_start, (padded // MOE_BLOCK).astype(I32), xs, w_exp_gate, w_exp_up,
                  w_exp_down)
    out, n = None, t // COMBINE_PARTS
    for part in range(COMBINE_PARTS):
        yg = _sc_gather_rows(ys, dest[:, part * n:(part + 1) * n]).reshape(TOP_K, n, d // 2)
        out = _combine(yg, part * n, gates, x1s, gate2, final_g, seq, tc=COMBINE_TILE,
                       out_so_far=out)
    return out


def kernel(x, c, ada_w, ada_b, norm1_g, w_in, lb_logits, hg_norm_g, w_branch_a, w_branch_b, w_out,
           norm2_g, w_router, router_bias, w_exp_gate, w_exp_up, w_exp_down, w_sh_gate, w_sh_up,
           w_sh_down, final_g):
    bsz, seq, d = x.shape
    depth = ada_w.shape[0]
    assert depth == 1, "the last layer's kernels also apply the final norm"
    lb_table = jnp.cumsum(jax.nn.softmax(lb_logits.astype(F32), axis=0), axis=0)
    out = _layer(x.reshape(bsz * seq, d), c, bsz, seq, lb_table[0], ada_w[0], ada_b[0], norm1_g[0],
                 w_in[0], hg_norm_g[0], w_branch_a[0], w_branch_b[0], w_out[0], norm2_g[0],
                 w_router[0], router_bias[0], w_exp_gate[0], w_exp_up[0], w_exp_down[0],
                 w_sh_gate[0], w_sh_up[0], w_sh_down[0], final_g)
    return out.reshape(bsz, seq, d)
```

```python
import functools

import jax
import jax.numpy as jnp
from jax import lax
from jax.experimental import pallas as pl
from jax.experimental.pallas import tpu as pltpu
from jax.experimental.pallas import tpu_sc as plsc

F32 = jnp.float32
BF16 = jnp.bfloat16
I32 = jnp.int32
U32 = jnp.uint32
HIGHEST = lax.Precision.HIGHEST

HG_HEADS = 4
HG_BLOCK = 16
HG_CHUNK = 32
HG_MILD_DECAY = -80.0
ATT_GROUPS = ((128, 1), (512, 4), (2048, 16))
ATT_HEADS_PER_GROUP = 4
ATT_HEAD_DIM = 64
TOP_K = 8
ROUTE_SCALE = 2.5
MOE_BLOCK = 256
RMS_EPS = 1e-6
N_DMA_QUEUES = 2
SC_WINDOW = 128
COMBINE_PARTS = 8
EXPERT_WEIGHT_BUFFERS = 3
EXPERT_GROUP = 4
EXPERT_IN_RING = 8
EXPERT_OUT_RING = 6

LANES = 128
VMEM_LIMIT_BYTES = 56 * 1024 * 1024

IN_PROJ_TILE = 1024
HGRN_TILE = 512
ATT_BLOCKS_PER_STEP = 16
MERGE_TILE = 512
DEST_TILE = 2048
COMBINE_TILE = 512


def _sigmoid(x):
    return 1.0 / (1.0 + jnp.exp(-x))


def _silu(x):
    return x * _sigmoid(x)


def _rms(x, g):
    return x * lax.rsqrt(jnp.mean(x * x, axis=-1, keepdims=True) + RMS_EPS) * g


def _pack_halves(x):
    n = x.shape[1] // 2
    bits = lax.bitcast_convert_type(x.astype(BF16).astype(F32), U32)
    return (bits[:, :n] >> 16) | (bits[:, n:] & jnp.uint32(0xFFFF0000))


def _unpack_halves(word):
    lo = lax.bitcast_convert_type(word << 16, F32)
    hi = lax.bitcast_convert_type(word & jnp.uint32(0xFFFF0000), F32)
    return lo, hi


def _params(n_axes=1):
    return pltpu.CompilerParams(
        dimension_semantics=("arbitrary",) * n_axes, vmem_limit_bytes=VMEM_LIMIT_BYTES)


def _ada_kernel(c_ref, w_ref, b_ref, o_ref):
    sc = _silu(c_ref[...])
    o_ref[...] = jnp.dot(sc, w_ref[...], preferred_element_type=F32, precision=HIGHEST) + b_ref[...]


def _ada(c, w, b):
    bsz, d = c.shape
    n = w.shape[1]
    return pl.pallas_call(
        _ada_kernel,
        out_shape=jax.ShapeDtypeStruct((bsz, n), F32),
        grid=(n // d,),
        in_specs=[pl.BlockSpec((bsz, d), lambda j: (0, 0)),
                  pl.BlockSpec((d, d), lambda j: (0, j)),
                  pl.BlockSpec((1, d), lambda j: (0, j))],
        out_specs=pl.BlockSpec((bsz, d), lambda j: (0, j)),
        compiler_params=_params(),
        name="ada_mod",
    )(c, w, b.reshape(1, n))


def _inproj_kernel(n_flat, flat_ranges, att_c0, x_ref, g_ref, sc_ref, sh_ref, w_ref, *refs):
    flat_refs, att_refs, scr = refs[:n_flat], refs[n_flat:-1], refs[-1]
    tm = x_ref.shape[0]
    h = _rms(x_ref[...], g_ref[...]) * (1.0 + sc_ref[0]) + sh_ref[0]
    hb = h.astype(BF16)
    for (c0, c1), o_ref in zip(flat_ranges, flat_refs):
        o_ref[...] = jnp.dot(hb, w_ref[:, c0:c1], preferred_element_type=F32).astype(o_ref.dtype)
    gw = ATT_HEADS_PER_GROUP * ATT_HEAD_DIM
    n_groups = len(ATT_GROUPS)
    for part in range(3):
        c0 = att_c0 + part * n_groups * gw
        res = jnp.dot(hb, w_ref[:, c0:c0 + n_groups * gw], preferred_element_type=F32)
        if part == 0:
            res = res * (ATT_HEAD_DIM ** -0.5)
        for g, (_, dil) in enumerate(ATT_GROUPS):
            o_ref = att_refs[g * 3 + part]
            sub = res[:, g * gw:(g + 1) * gw]
            if dil == 1:
                o_ref[0, 0] = sub.astype(BF16)
            else:
                for c in range(gw // LANES):
                    scr[c] = sub[:, c * LANES:(c + 1) * LANES]
                for r in range(dil):
                    o_ref[0, r] = jnp.concatenate(
                        [scr[c, pl.ds(r, tm // dil, stride=dil), :] for c in range(gw // LANES)],
                        axis=1).astype(BF16)


def _inproj(x2, g, scale, shift, w_bf16, bsz, seq, flat_segs, att_c0, tm):
    t, d = x2.shape
    gw = ATT_HEADS_PER_GROUP * ATT_HEAD_DIM
    n_per = seq // tm
    per_b = lambda i: (i // n_per, 0, 0)
    att_shapes, att_specs = [], []
    for _, dil in ATT_GROUPS:
        for _ in range(3):
            att_shapes.append(jax.ShapeDtypeStruct((bsz, dil, seq // dil, gw), BF16))
            att_specs.append(pl.BlockSpec((1, dil, tm // dil, gw),
                                          lambda i: (i // n_per, 0, i % n_per, 0)))
    outs = pl.pallas_call(
        functools.partial(_inproj_kernel, len(flat_segs),
                          tuple((c0, c0 + wdt) for c0, wdt, _ in flat_segs), att_c0),
        out_shape=[jax.ShapeDtypeStruct((t, wdt), dt) for _, wdt, dt in flat_segs] + att_shapes,
        grid=(t // tm,),
        in_specs=[pl.BlockSpec((tm, d), lambda i: (i, 0)),
                  pl.BlockSpec((1, d), lambda i: (0, 0)),
                  pl.BlockSpec((1, 1, d), per_b),
                  pl.BlockSpec((1, 1, d), per_b),
                  pl.BlockSpec(w_bf16.shape, lambda i: (0, 0), pipeline_mode=pl.Buffered(1))],
        out_specs=[pl.BlockSpec((tm, wdt), lambda i: (i, 0)) for _, wdt, _ in flat_segs]
        + att_specs,
        scratch_shapes=[pltpu.VMEM((gw // LANES, tm, LANES), F32)],
        compiler_params=_params(),
        name="in_proj",
    )(x2, g.reshape(1, d), scale, shift, w_bf16)
    return outs[:len(flat_segs)], outs[len(flat_segs):]


def _hgrn_kernel(ts, q_ref, f_ref, v_ref, gt_ref, lb_ref, ng_ref, o_ref, st_ref, b_ref):
    dk = q_ref.shape[1] // HG_HEADS
    n_chunks = ts // HG_CHUNK
    n_blk = HG_CHUNK // HG_BLOCK

    @pl.when(pl.program_id(1) == 0)
    def _():
        st_ref[...] = jnp.zeros_like(st_ref)

    row = lax.broadcasted_iota(I32, (LANES, LANES), 0)
    col = lax.broadcasted_iota(I32, (LANES, LANES), 1)
    same_chunk = (row // HG_CHUNK) == (col // HG_CHUNK)
    cum_mat = jnp.where(same_chunk & (col <= row), 1.0, 0.0).astype(BF16)

    def chunk_cumsum(x):
        out = []
        for r0 in range(0, ts, LANES):
            rest = x[r0:r0 + LANES]
            acc = None
            for _ in range(3):
                term = rest.astype(BF16)
                part = jnp.dot(cum_mat, term, preferred_element_type=F32)
                acc = part if acc is None else acc + part
                rest = rest - term.astype(F32)
            out.append(acc)
        return jnp.concatenate(out, axis=0)

    def forget(cs):
        lb = lb_ref[:, cs]
        return lb + (1.0 - lb) * _sigmoid(f_ref[:, cs])

    b_min = None
    for h in range(HG_HEADS):
        cs = slice(h * dk, (h + 1) * dk)
        b = chunk_cumsum(jnp.log(forget(cs)))
        b_ref[:, cs] = b
        m = jnp.min(b)
        b_min = m if b_min is None else jnp.minimum(b_min, m)
    mild = b_min >= HG_MILD_DECAY

    def finish(h, o, st):
        cs = slice(h * dk, (h + 1) * dk)
        st_ref[h] = st
        y = _rms(o, ng_ref[:, cs]) * _silu(gt_ref[:, cs].astype(F32))
        o_ref[:, cs] = y.astype(o_ref.dtype)

    @pl.when(mild)
    def _():
        span = 2 * HG_CHUNK
        causal = (lax.broadcasted_iota(I32, (span, span), 0)
                  >= lax.broadcasted_iota(I32, (span, span), 1))
        nt = lambda x, y: lax.dot_general(x, y, (((1,), (1,)), ((), ())),
                                          preferred_element_type=F32)
        for h in range(HG_HEADS):
            cs = slice(h * dk, (h + 1) * dk)
            v = v_ref[:, cs]
            b = b_ref[:, cs]
            q = q_ref[:, cs].astype(F32)
            k = 1.0 - forget(cs)
            st = st_ref[h]
            o_rows = []
            for r0 in range(0, ts, span):
                sl = slice(r0, r0 + span)
                b_first, b_second = b[r0:r0 + HG_CHUNK], b[r0 + HG_CHUNK:r0 + span]
                end_first = b_first[HG_CHUNK - 1:HG_CHUNK]
                end_second = b_second[HG_CHUNK - 1:HG_CHUNK]
                e = jnp.exp(jnp.concatenate([b_first - end_first, b_second], axis=0))
                qe = (q[sl] * e).astype(BF16)
                ke = k[sl] / e
                a = jnp.where(causal, nt(qe, ke.astype(BF16)), 0.0).astype(BF16)
                st_in = (st * jnp.exp(end_first)).astype(BF16)
                o_rows.append(jnp.dot(a, v[sl], preferred_element_type=F32) + nt(qe, st_in))
                kend = (ke * jnp.exp(end_second)).astype(BF16)
                vt = v[sl].astype(F32).T.astype(BF16)
                st = (st * jnp.exp(end_first + end_second)
                      + jnp.dot(vt, kend, preferred_element_type=F32))
            finish(h, jnp.concatenate(o_rows, axis=0), st)

    @pl.when(jnp.logical_not(mild))
    def _():
        _hgrn_steep(ts, dk, n_chunks, n_blk, q_ref, v_ref, b_ref, st_ref, forget, finish)


def _hgrn_steep(ts, dk, n_chunks, n_blk, q_ref, v_ref, b_ref, st_ref, forget, finish):
    t_in_blk = lax.broadcasted_iota(I32, (ts, dk), 0) % HG_BLOCK

    for h in range(HG_HEADS):
        cs = slice(h * dk, (h + 1) * dk)
        q = q_ref[:, cs].astype(F32)
        v = v_ref[:, cs].astype(F32)
        k = 1.0 - forget(cs)
        b = b_ref[:, cs]

        o = jnp.sum(q * k, axis=-1, keepdims=True) * v
        for d in range(1, HG_BLOCK):
            k_d = pltpu.roll(k, d, axis=0)
            b_d = pltpu.roll(b, d, axis=0)
            v_d = pltpu.roll(v, d, axis=0)
            w = jnp.sum(q * k_d * jnp.exp(jnp.minimum(b - b_d, 0.0)), axis=-1, keepdims=True)
            o = o + jnp.where(t_in_blk >= d, w * v_d, 0.0)

        st = st_ref[h]
        o_rows = []
        for c in range(n_chunks):
            r0 = c * HG_CHUNK
            bc = b[r0:r0 + HG_CHUNK]
            qc = q[r0:r0 + HG_CHUNK]
            kc = k[r0:r0 + HG_CHUNK]
            vc = v[r0:r0 + HG_CHUNK].astype(BF16)
            st_b = st.astype(BF16)
            for i in range(n_blk):
                i0 = i * HG_BLOCK
                if i == 0:
                    qt = qc[:HG_BLOCK] * jnp.exp(bc[:HG_BLOCK])
                    qs = qt
                else:
                    ref_row = bc[i0 - 1:i0]
                    qt = qc[i0:i0 + HG_BLOCK] * jnp.exp(bc[i0:i0 + HG_BLOCK] - ref_row)
                    qs = qt * jnp.exp(ref_row)
                oi = lax.dot_general(qs.astype(BF16), st_b, (((1,), (1,)), ((), ())),
                                     preferred_element_type=F32)
                if i > 0:
                    kh = kc[:i0] * jnp.exp(ref_row - bc[:i0])
                    a = lax.dot_general(qt.astype(BF16), kh.astype(BF16), (((1,), (1,)), ((), ())),
                                        preferred_element_type=F32)
                    oi = oi + jnp.dot(a.astype(BF16), vc[:i0], preferred_element_type=F32)
                o_rows.append(oi)
            b_end = bc[HG_CHUNK - 1:HG_CHUNK]
            kend = kc * jnp.exp(b_end - bc)
            vt = v[r0:r0 + HG_CHUNK].T.astype(BF16)
            st = st * jnp.exp(b_end) + jnp.dot(vt, kend.astype(BF16), preferred_element_type=F32)
        finish(h, o + jnp.concatenate(o_rows, axis=0), st)


def _hgrn(hq, hf, hi, hg, lb, ng, bsz, seq, ts):
    t, w = hq.shape
    dk = w // HG_HEADS
    n_s = seq // ts
    tile = lambda b, s: (b * n_s + s, 0)
    return pl.pallas_call(
        functools.partial(_hgrn_kernel, ts),
        out_shape=jax.ShapeDtypeStruct((t, w), BF16),
        grid=(bsz, n_s),
        in_specs=[pl.BlockSpec((ts, w), tile)] * 4
        + [pl.BlockSpec((1, w), lambda b, s: (0, 0))] * 2,
        out_specs=pl.BlockSpec((ts, w), tile),
        scratch_shapes=[pltpu.VMEM((HG_HEADS, dk, dk), F32), pltpu.VMEM((ts, w), F32)],
        compiler_params=_params(2),
        name="hgrn2",
    )(hq, hf, hi, hg, lb.reshape(1, w), ng.reshape(1, w))


def _attn_kernel(nk, nq, nr, q_ref, kp_ref, kc_ref, vp_ref, vc_ref, o_ref, lse_ref):
    n = pl.program_id(2)
    e = ATT_HEAD_DIM
    i = lax.broadcasted_iota(I32, (nk, 2 * nk), 0)
    j = lax.broadcasted_iota(I32, (nk, 2 * nk), 1)
    band = (j >= i) & (j <= i + nk)
    first_head = lax.broadcasted_iota(I32, (nk, LANES), 1) < e
    zero = jnp.zeros((), q_ref.dtype)
    for r in range(nr):
        kk = jnp.concatenate([kp_ref[0, r], kc_ref[0, r]], axis=0)
        vv = jnp.concatenate([vp_ref[0, r], vc_ref[0, r]], axis=0)
        for b in range(nq):
            valid = band & ((j >= nk) | (n * nq + b > 0))
            rows = slice(b * nk, (b + 1) * nk)
            for c in range(0, ATT_HEADS_PER_GROUP * e, LANES):
                q = q_ref[0, r, rows, c:c + LANES]
                kb = kk[b * nk:(b + 2) * nk, c:c + LANES]
                vb = vv[b * nk:(b + 2) * nk, c:c + LANES]
                outs, lses = [], []
                for keep in (first_head, jnp.logical_not(first_head)):
                    s = lax.dot_general(jnp.where(keep, q, zero), kb, (((1,), (1,)), ((), ())),
                                        preferred_element_type=F32)
                    s = jnp.where(valid, s, -jnp.inf)
                    m = jnp.max(s, axis=-1, keepdims=True)
                    p = jnp.exp(s - m)
                    l = jnp.sum(p, axis=-1, keepdims=True)
                    outs.append(jnp.dot(p.astype(BF16), vb, preferred_element_type=F32) / l)
                    lses.append(m + jnp.log(l))
                o_ref[0, r, rows, c:c + LANES] = jnp.where(first_head, outs[0], outs[1])
                lse_ref[0, r, rows, c:c + LANES] = jnp.where(first_head, lses[0], lses[1])


def _attn_group(q, k, v, g, blocks_per_step):
    window, dil = ATT_GROUPS[g]
    nk = window // dil
    bsz, _, ln, gw = q.shape
    nq = min(blocks_per_step, ln // nk)
    nr = min(blocks_per_step // nq, dil)
    assert ln % (nk * nq) == 0 and dil % nr == 0 and 2 * ATT_HEAD_DIM == LANES
    cur = pl.BlockSpec((1, nr, nq * nk, gw), lambda b, r, n: (b, r, n, 0))
    prev = pl.BlockSpec((1, nr, nk, gw), lambda b, r, n: (b, r, jnp.maximum(n * nq - 1, 0), 0))
    return pl.pallas_call(
        functools.partial(_attn_kernel, nk, nq, nr),
        out_shape=[jax.ShapeDtypeStruct(q.shape, F32)] * 2,
        grid=(bsz, dil // nr, ln // (nk * nq)),
        in_specs=[cur, prev, cur, prev, cur],
        out_specs=[cur, cur],
        compiler_params=_params(3),
        name=f"dilated_attn_g{g}",
    )(q, k, k, v, v)


def _token_major(ref, scr):
    dil, rows = ref.shape[1], ref.shape[2]
    if dil == 1:
        return ref[0, 0]
    n_col = scr.shape[0]
    for r in range(dil):
        for c in range(n_col):
            scr[c, pl.ds(r, rows, stride=dil), :] = ref[0, r, :, c * LANES:(c + 1) * LANES]
    return jnp.concatenate([scr[c] for c in range(n_col)], axis=1)


def _merge_kernel(ya_ref, o0_ref, o1_ref, o2_ref, l0_ref, l1_ref, l2_ref, ga_ref, gb_ref, x_ref,
                  g1_ref, sc2_ref, sh2_ref, g2_ref, n2_ref, wa_ref, wb_ref, wo_ref, wr_ref, wrl_ref,
                  wsg_ref, wsu_ref, wsd_ref, bias_ref, x1_ref, hp_ref, idx_ref, gate_ref, rank_ref,
                  cnt_ref, carry_ref, lg_ref, *scr):
    step = pl.program_id(0)

    @pl.when(step == 0)
    def _():
        carry_ref[...] = jnp.zeros_like(carry_ref)
        lg_ref[...] = jnp.zeros_like(lg_ref)

    _route(lg_ref[...], jnp.where(step > 0, 1.0, 0.0), bias_ref, idx_ref, gate_ref, rank_ref,
           cnt_ref, carry_ref)

    l0, l1, l2 = (_token_major(r, s) for r, s in zip((l0_ref, l1_ref, l2_ref), scr[:3]))
    o0, o1, o2 = (_token_major(r, s) for r, s in zip((o0_ref, o1_ref, o2_ref), scr[3:]))
    m = jnp.maximum(jnp.maximum(l0, l1), l2)
    e0, e1, e2 = jnp.exp(l0 - m), jnp.exp(l1 - m), jnp.exp(l2 - m)
    yb = (e0 * o0 + e1 * o1 + e2 * o2) / (e0 + e1 + e2)
    merged = (_sigmoid(ga_ref[...].astype(F32))
              * jnp.dot(ya_ref[...], wa_ref[...], preferred_element_type=F32)
              + _sigmoid(gb_ref[...].astype(F32))
              * jnp.dot(yb.astype(BF16), wb_ref[...], preferred_element_type=F32))
    x1 = x_ref[...] + g1_ref[0] * jnp.dot(merged.astype(BF16), wo_ref[...],
                                           preferred_element_type=F32)
    h2 = _rms(x1, n2_ref[...]) * (1.0 + sc2_ref[0]) + sh2_ref[0]
    hb = h2.astype(BF16)
    act = (_silu(jnp.dot(hb, wsg_ref[...], preferred_element_type=F32))
           * jnp.dot(hb, wsu_ref[...], preferred_element_type=F32))
    shared = jnp.dot(act.astype(BF16), wsd_ref[...], preferred_element_type=F32)
    x1_ref[...] = x1 + g2_ref[0] * shared
    hp_ref[...] = _pack_halves(h2)
    h_lo = (h2 - hb.astype(F32)).astype(BF16)
    nt = lambda a, b: lax.dot_general(a, b, (((1,), (1,)), ((), ())), preferred_element_type=F32)
    lg_ref[...] = nt(wr_ref[...], hb) + (nt(wr_ref[...], h_lo) + nt(wrl_ref[...], hb))


def _merge(ya, att, ga, gb, x2, gate1, scale2, shift2, gate2, norm2_g, wa, wb, wo, wr_t, wsg, wsu,
           wsd, router_bias, seq, tm):
    t, d = x2.shape
    n_e = wr_t.shape[0]
    wr_hi = wr_t.astype(BF16)
    wr_lo = (wr_t - wr_hi.astype(F32)).astype(BF16)
    n_per = seq // tm
    n_tiles = t // tm
    tile = lambda i: jnp.minimum(i, n_tiles - 1)
    per_b = lambda i: (tile(i) // n_per, 0, 0)
    rows = lambda wdt: pl.BlockSpec((tm, wdt), lambda i: (tile(i), 0))
    full = lambda a: pl.BlockSpec(a.shape, lambda i: (0,) * a.ndim)
    vec = pl.BlockSpec((1, 1, d), per_b)
    (o0, l0), (o1, l1), (o2, l2) = att
    gw = o0.shape[3]
    by_residue = lambda a: pl.BlockSpec((1, a.shape[1], tm // a.shape[1], gw),
                                        lambda i: (tile(i) // n_per, 0, tile(i) % n_per, 0))
    att_in = (o0, o1, o2, l0, l1, l2)
    bias_col = router_bias.reshape(n_e, 1)
    tok = pl.BlockSpec((TOP_K, tm), lambda i: (0, jnp.maximum(i - 1, 0)))
    return pl.pallas_call(
        _merge_kernel,
        out_shape=[jax.ShapeDtypeStruct((t, d), F32),
                   jax.ShapeDtypeStruct((t, d // 2), U32),
                   jax.ShapeDtypeStruct((TOP_K, t), I32), jax.ShapeDtypeStruct((TOP_K, t), F32),
                   jax.ShapeDtypeStruct((TOP_K, t), I32), jax.ShapeDtypeStruct((n_e, LANES), I32)],
        grid=(n_tiles + 1,),
        in_specs=[rows(ya.shape[1])] + [by_residue(a) for a in att_in] + [rows(d)] * 3
        + [vec, vec, vec, vec, pl.BlockSpec((1, d), lambda i: (0, 0))]
        + [full(a) for a in (wa, wb, wo, wr_hi, wr_lo, wsg, wsu, wsd, bias_col)],
        out_specs=[rows(d), rows(d // 2), tok, tok, tok,
                   pl.BlockSpec((n_e, LANES), lambda i: (0, 0))],
        scratch_shapes=[pltpu.VMEM((n_e, 1), F32), pltpu.VMEM((n_e, tm), F32)]
        + [pltpu.VMEM((gw // LANES, tm, LANES), F32)] * 6,
        compiler_params=_params(),
        name="merge_router",
    )(ya, *att_in, ga, gb, x2, gate1, scale2, shift2, gate2,
      norm2_g.reshape(1, d), wa, wb, wo, wr_hi, wr_lo, wsg, wsu, wsd, bias_col)


def _route(logits, live, bias_ref, idx_ref, gate_ref, rank_ref, cnt_ref, carry_ref):
    n_e, tt = logits.shape
    scores = _sigmoid(logits)
    sel = scores + bias_ref[...]
    eio = lax.broadcasted_iota(I32, (n_e, tt), 0)
    picked = jnp.zeros((n_e, tt), F32)
    idxs, vals = [], []
    for _ in range(TOP_K):
        m = jnp.max(sel, axis=0, keepdims=True)
        ik = jnp.min(jnp.where(sel == m, eio, n_e), axis=0, keepdims=True)
        hit = eio == ik
        vals.append(jnp.sum(jnp.where(hit, scores, 0.0), axis=0, keepdims=True))
        sel = jnp.where(hit, -jnp.inf, sel)
        picked = picked + jnp.where(hit, 1.0, 0.0)
        idxs.append(ik)
    denom = vals[0]
    for v in vals[1:]:
        denom = denom + v
    gate_ref[...] = jnp.concatenate([v / denom * ROUTE_SCALE for v in vals], axis=0)
    idx_ref[...] = jnp.concatenate(idxs, axis=0)

    upper = (lax.broadcasted_iota(I32, (tt, tt), 0) <= lax.broadcasted_iota(I32, (tt, tt), 1))
    incl = jnp.dot(picked.astype(BF16), jnp.where(upper, 1.0, 0.0).astype(BF16),
                   preferred_element_type=F32)
    before = incl - picked + carry_ref[...]
    rank_ref[...] = jnp.concatenate(
        [jnp.sum(jnp.where(eio == ik, before, 0.0), axis=0, keepdims=True) for ik in idxs],
        axis=0).astype(I32)
    carry_ref[...] = carry_ref[...] + jnp.sum(picked, axis=1, keepdims=True) * live
    cnt_ref[...] = jnp.broadcast_to(carry_ref[...], cnt_ref.shape).astype(I32)


def _dest_kernel(idx_ref, rank_ref, start_ref, o_ref):
    k, tt = idx_ref.shape
    n_e = start_ref.shape[0]
    eio = lax.broadcasted_iota(I32, (n_e, tt), 0)
    start = start_ref[...]
    rows = [jnp.sum(jnp.where(eio == idx_ref[r:r + 1, :], start, 0), axis=0, keepdims=True)
            for r in range(k)]
    o_ref[...] = jnp.concatenate(rows, axis=0) + rank_ref[...]


def _dest(idx, rank, seg_start, tt):
    k, t = idx.shape
    n_e = seg_start.shape[0]
    tok = pl.BlockSpec((k, tt), lambda i: (0, i))
    return pl.pallas_call(
        _dest_kernel,
        out_shape=jax.ShapeDtypeStruct((k, t), I32),
        grid=(t // tt,),
        in_specs=[tok, tok, pl.BlockSpec((n_e, 1), lambda i: (0, 0))],
        out_specs=tok,
        compiler_params=_params(),
        name="moe_dest",
    )(idx, rank, seg_start.reshape(n_e, 1))


def _sc_mesh():
    return plsc.VectorSubcoreMesh(core_axis_name="core", subcore_axis_name="subcore")


def _sc_scatter_rows(rows, dest, n_out):
    k, t = dest.shape
    w = rows.shape[1]
    mesh = _sc_mesh()
    n_workers = mesh.num_cores * mesh.num_subcores
    win_per_worker = t // (SC_WINDOW * n_workers)
    assert win_per_worker * SC_WINDOW * n_workers == t

    @functools.partial(
        pl.kernel, out_type=jax.ShapeDtypeStruct((n_out, w), rows.dtype), mesh=mesh,
        scratch_types=[pltpu.VMEM((SC_WINDOW, w), rows.dtype)]
        + [pltpu.VMEM((1, SC_WINDOW), I32)] * k + [pltpu.SemaphoreType.DMA],
        name="moe_dispatch_sc")
    def run(rows_hbm, idx_hbm, out_hbm, rows_v, *rest):
        idx_v, sem = rest[:k], rest[k]
        worker = lax.axis_index("subcore") * mesh.num_cores + lax.axis_index("core")

        @pl.loop(0, win_per_worker)
        def _(j):
            t0 = pl.multiple_of((worker * win_per_worker + j) * SC_WINDOW, SC_WINDOW)
            loads = [pltpu.async_copy(rows_hbm.at[pl.ds(t0, SC_WINDOW)], rows_v, sem)]
            loads += [pltpu.async_copy(idx_hbm.at[:, pl.ds(r * t + t0, SC_WINDOW)], idx_v[r], sem)
                      for r in range(k)]
            for c in loads:
                c.wait()
            copies = [pltpu.async_copy(rows_v, out_hbm.at[idx_v[r].at[0]], sem) for r in range(k)]
            for c in copies:
                c.wait()

    return run(rows, dest.reshape(1, k * t))


def _sc_gather_rows(table, dest):
    k, t = dest.shape
    w = table.shape[1]
    mesh = _sc_mesh()
    n_workers = mesh.num_cores * mesh.num_subcores
    win_per_worker = (k * t) // (SC_WINDOW * n_workers)
    assert win_per_worker * SC_WINDOW * n_workers == k * t and win_per_worker % 2 == 0

    @functools.partial(
        pl.kernel, out_type=jax.ShapeDtypeStruct((k * t, w), table.dtype), mesh=mesh,
        scratch_types=[pltpu.VMEM((SC_WINDOW, w), table.dtype), pltpu.VMEM((1, SC_WINDOW), I32),
                       pltpu.VMEM((1, SC_WINDOW), I32), pltpu.SemaphoreType.DMA],
        name="moe_gather_sc")
    def run(table_hbm, idx_hbm, out_hbm, rows_v, idx_a, idx_b, sem):
        worker = lax.axis_index("subcore") * mesh.num_cores + lax.axis_index("core")

        def window(j):
            return pl.ds(pl.multiple_of((worker * win_per_worker + j) * SC_WINDOW, SC_WINDOW),
                         SC_WINDOW)

        def idx_load(j, buf):
            return pltpu.make_async_copy(idx_hbm.at[:, window(j)], buf, sem)

        idx_load(0, idx_a).start()

        @pl.loop(0, win_per_worker, step=2)
        def _(j0):
            for b, (cur, nxt) in enumerate(((idx_a, idx_b), (idx_b, idx_a))):
                j = j0 + b
                idx_load(j, cur).wait()

                @pl.when(j + 1 < win_per_worker)
                def _():
                    idx_load(j + 1, nxt).start()

                pltpu.sync_copy(table_hbm.at[cur.at[0]], rows_v)
                pltpu.sync_copy(rows_v, out_hbm.at[window(j)])

    return run(table, dest.reshape(1, k * t))


def _expert_kernel(start_ref, nblk_ref, xs_ref, wg_ref, wu_ref, wd_ref, ys_ref,
                   xbuf, ybuf, wgb, wub, wdb, wbuf_g, wbuf_u, wbuf_d, sem_in, sem_out, sem_w):
    wbuf = (wbuf_g, wbuf_u, wbuf_d)
    e = pl.program_id(0)
    n_e = pl.num_programs(0)
    nb = nblk_ref[e]
    g0 = start_ref[e] // MOE_BLOCK
    n_used = start_ref[n_e - 1] // MOE_BLOCK + nblk_ref[n_e - 1]
    n_in, n_out = xbuf.shape[0], ybuf.shape[0]

    def rows(g):
        return pl.ds(pl.multiple_of(g * MOE_BLOCK, MOE_BLOCK), MOE_BLOCK)

    def in_copy(g):
        slot = lax.rem(g, n_in)
        return pltpu.make_async_copy(xs_ref.at[rows(g), :], xbuf.at[slot], sem_in.at[slot])

    def out_copy(g):
        slot = lax.rem(g, n_out)
        return pltpu.make_async_copy(ybuf.at[slot], ys_ref.at[rows(g), :], sem_out.at[slot])

    look = n_in - EXPERT_GROUP

    @pl.when(e == 0)
    def _():
        for g in range(look):
            @pl.when(g < n_used)
            def _():
                in_copy(g).start(priority=g % N_DMA_QUEUES)

    n_w = wbuf[0].shape[0]

    def weight_copies(ex):
        slot = lax.rem(ex, n_w)
        return [pltpu.make_async_copy(src.at[ex], buf.at[slot], sem_w.at[slot])
                for src, buf in zip((wg_ref, wu_ref, wd_ref), wbuf)]

    @pl.when(e == 0)
    def _():
        for ex in range(min(n_w, wg_ref.shape[0])):
            for c in weight_copies(ex):
                c.start()

    for c in weight_copies(e):
        c.wait()
    w_slot = lax.rem(e, n_w)

    @pl.when(nb > 0)
    def _():
        wgb[...] = wbuf[0][w_slot].astype(BF16)
        wub[...] = wbuf[1][w_slot].astype(BF16)
        wdb[...] = wbuf[2][w_slot].astype(BF16)

    @pl.when(e + n_w < n_e)
    def _():
        for c in weight_copies(e + n_w):
            c.start()

    @pl.when(nb > 0)
    def _():
        def swiglu(word):
            lo, hi = _unpack_halves(word)
            x = jnp.concatenate([lo.astype(BF16), hi.astype(BF16)], axis=1)
            gate = jnp.dot(x, wgb[...], preferred_element_type=F32)
            up = jnp.dot(x, wub[...], preferred_element_type=F32)
            act = (_silu(gate) * up).astype(BF16)
            return jnp.dot(act, wdb[...], preferred_element_type=F32)

        def process(g, m):
            for i in range(m):
                in_copy(g + i).wait()
            for i in range(m):
                @pl.when(g + i >= n_out)
                def _():
                    out_copy(g + i - n_out).wait()
            for i in range(m):
                @pl.when(g + look + i < n_used)
                def _():
                    in_copy(g + look + i).start(priority=i % N_DMA_QUEUES)
            y_all = swiglu(jnp.concatenate([xbuf[lax.rem(g + i, n_in)] for i in range(m)], axis=0))
            for i in range(m):
                ybuf[lax.rem(g + i, n_out)] = _pack_halves(y_all[i * MOE_BLOCK:(i + 1) * MOE_BLOCK])
            for i in range(m):
                out_copy(g + i).start(priority=(i + 1) % N_DMA_QUEUES)

        def group_body(p, carry):
            process(g0 + p * EXPERT_GROUP, EXPERT_GROUP)
            return carry

        lax.fori_loop(0, nb // EXPERT_GROUP, group_body, 0)
        for m in range(1, EXPERT_GROUP):
            @pl.when(lax.rem(nb, EXPERT_GROUP) == m)
            def _():
                process(g0 + nb - m, m)

    @pl.when(e == n_e - 1)
    def _():
        for i in range(n_out):
            @pl.when(n_used - 1 - i >= 0)
            def _():
                out_copy(n_used - 1 - i).wait()


def _experts(seg_start, seg_blocks, xs, wg, wu, wd):
    n_slots, half = xs.shape
    n_e, d, de = wg.shape
    n_w = EXPERT_WEIGHT_BUFFERS
    return pl.pallas_call(
        _expert_kernel,
        out_shape=jax.ShapeDtypeStruct((n_slots, half), U32),
        grid_spec=pltpu.PrefetchScalarGridSpec(
            num_scalar_prefetch=2,
            grid=(n_e,),
            in_specs=[pl.BlockSpec(memory_space=pl.ANY)] * 4,
            out_specs=pl.BlockSpec(memory_space=pl.ANY),
            scratch_shapes=[pltpu.VMEM((EXPERT_IN_RING, MOE_BLOCK, half), U32),
                            pltpu.VMEM((EXPERT_OUT_RING, MOE_BLOCK, half), U32),
                            pltpu.VMEM((d, de), BF16), pltpu.VMEM((d, de), BF16),
                            pltpu.VMEM((de, d), BF16),
                            pltpu.VMEM((n_w, d, de), F32), pltpu.VMEM((n_w, d, de), F32),
                            pltpu.VMEM((n_w, de, d), F32),
                            pltpu.SemaphoreType.DMA((EXPERT_IN_RING,)),
                            pltpu.SemaphoreType.DMA((EXPERT_OUT_RING,)),
                            pltpu.SemaphoreType.DMA((n_w,))]),
        compiler_params=_params(),
        name="moe_experts",
    )(seg_start, seg_blocks, xs, wg, wu, wd)


def _combine_kernel(yg_ref, gt_ref, x_ref, g2_ref, fg_ref, o_ref):
    k, tc = gt_ref.shape
    eye = jnp.where(lax.broadcasted_iota(I32, (tc, tc), 0) == lax.broadcasted_iota(I32, (tc, tc), 1),
                    1.0, 0.0).astype(BF16)
    rest, gt = gt_ref[...], None
    for _ in range(3):
        term = rest.astype(BF16)
        part = lax.dot_general(eye, term, (((1,), (1,)), ((), ())), preferred_element_type=F32)
        gt = part if gt is None else gt + part
        rest = rest - term.astype(F32)
    lo, hi = _unpack_halves(yg_ref[0])
    y_lo, y_hi = lo * gt[:, 0:1], hi * gt[:, 0:1]
    for r in range(1, k):
        lo, hi = _unpack_halves(yg_ref[r])
        y_lo, y_hi = y_lo + lo * gt[:, r:r + 1], y_hi + hi * gt[:, r:r + 1]
    y = jnp.concatenate([y_lo, y_hi], axis=1)
    o_ref[...] = _rms(x_ref[...] + g2_ref[0] * y, fg_ref[...])


def _combine_into_kernel(yg_ref, gt_ref, x_ref, g2_ref, fg_ref, prev_ref, o_ref):
    del prev_ref
    _combine_kernel(yg_ref, gt_ref, x_ref, g2_ref, fg_ref, o_ref)


def _combine(yg, tok0, gates, x1s, gate2, final_g, seq, tc, out_so_far=None):
    k, n, half = yg.shape
    t, d = x1s.shape
    assert n % tc == 0 and tok0 % tc == 0
    b0 = tok0 // tc
    args = [yg, gates, x1s, gate2, final_g.reshape(1, d)]
    in_specs = [pl.BlockSpec((k, tc, half), lambda i: (0, i, 0)),
                pl.BlockSpec((k, tc), lambda i: (0, i + b0)),
                pl.BlockSpec((tc, d), lambda i: (i + b0, 0)),
                pl.BlockSpec((1, 1, d), lambda i: (((i + b0) * tc) // seq, 0, 0)),
                pl.BlockSpec((1, d), lambda i: (0, 0))]
    aliases = {}
    kernel = _combine_kernel
    if out_so_far is not None:
        args.append(out_so_far)
        in_specs.append(pl.BlockSpec(memory_space=pl.ANY))
        aliases = {len(args) - 1: 0}
        kernel = _combine_into_kernel
    return pl.pallas_call(
        kernel,
        out_shape=jax.ShapeDtypeStruct((t, d), F32),
        grid=(n // tc,),
        in_specs=in_specs,
        out_specs=pl.BlockSpec((tc, d), lambda i: (i + b0, 0)),
        input_output_aliases=aliases,
        compiler_params=_params(),
        name="moe_combine",
    )(*args)


def _layer(x2, c, bsz, seq, lb_row, ada_w, ada_b, norm1_g, w_in, hg_norm_g, w_branch_a, w_branch_b,
           w_out, norm2_g, w_router, router_bias, w_exp_gate, w_exp_up, w_exp_down, w_sh_gate,
           w_sh_up, w_sh_down, final_g):
    t, d = x2.shape
    n_e = w_router.shape[1]
    mod = _ada(c, ada_w, ada_b).reshape(bsz, 6, 1, d)
    shift1, scale1, gate1, shift2, scale2, gate2 = (mod[:, j] for j in range(6))

    hw = hg_norm_g.shape[0]
    aw = len(ATT_GROUPS) * ATT_HEADS_PER_GROUP * ATT_HEAD_DIM
    flat_segs = [(0, hw, BF16), (hw, hw, F32), (2 * hw, hw, BF16), (3 * hw, hw, BF16),
                 (4 * hw + 3 * aw, d, BF16), (4 * hw + 3 * aw + d, d, BF16)]
    (hq, hf, hi, hg, ga, gb), qkv = _inproj(
        x2, norm1_g, scale1, shift1, w_in.astype(BF16), bsz, seq, flat_segs, 4 * hw,
        tm=IN_PROJ_TILE)

    ya = _hgrn(hq, hf, hi, hg, lb_row, hg_norm_g, bsz, seq, ts=HGRN_TILE)
    att = [_attn_group(*qkv[3 * g:3 * g + 3], g, blocks_per_step=ATT_BLOCKS_PER_STEP)
           for g in range(len(ATT_GROUPS))]

    x1s, hp, idx, gates, rank, cnt = _merge(
        ya, att, ga, gb, x2, gate1, scale2, shift2, gate2, norm2_g, w_branch_a.astype(BF16),
        w_branch_b.astype(BF16), w_out.astype(BF16), w_router.T, w_sh_gate.astype(BF16),
        w_sh_up.astype(BF16), w_sh_down.astype(BF16), router_bias, seq, tm=MERGE_TILE)
    counts = cnt[:, 0]
    padded = (counts + MOE_BLOCK - 1) // MOE_BLOCK * MOE_BLOCK
    seg_start = (jnp.cumsum(padded) - padded).astype(I32)
    n_blocks = -(-(t * TOP_K) // MOE_BLOCK) + n_e
    dest = _dest(idx, rank, seg_start, tt=DEST_TILE)

    xs = _sc_scatter_rows(hp, dest, n_blocks * MOE_BLOCK)
    ys = _experts(seg_start, (padded // MOE_BLOCK).astype(I32), xs, w_exp_gate, w_exp_up,
                  w_exp_down)
    out, n = None, t // COMBINE_PARTS
    for part in range(COMBINE_PARTS):
        yg = _sc_gather_rows(ys, dest[:, part * n:(part + 1) * n]).reshape(TOP_K, n, d // 2)
        out = _combine(yg, part * n, gates, x1s, gate2, final_g, seq, tc=COMBINE_TILE,
                       out_so_far=out)
    return out


def kernel(x, c, ada_w, ada_b, norm1_g, w_in, lb_logits, hg_norm_g, w_branch_a, w_branch_b, w_out,
           norm2_g, w_router, router_bias, w_exp_gate, w_exp_up, w_exp_down, w_sh_gate, w_sh_up,
           w_sh_down, final_g):
    bsz, seq, d = x.shape
    depth = ada_w.shape[0]
    assert depth == 1, "the last layer's kernels also apply the final norm"
    lb_table = jnp.cumsum(jax.nn.softmax(lb_logits.astype(F32), axis=0), axis=0)
    out = _layer(x.reshape(bsz * seq, d), c, bsz, seq, lb_table[0], ada_w[0], ada_b[0], norm1_g[0],
                 w_in[0], hg_norm_g[0], w_branch_a[0], w_branch_b[0], w_out[0], norm2_g[0],
                 w_router[0], router_bias[0], w_exp_gate[0], w_exp_up[0], w_exp_down[0],
                 w_sh_gate[0], w_sh_up[0], w_sh_down[0], final_g)
    return out.reshape(bsz, seq, d)
```

```python
import functools

import jax
import jax.numpy as jnp
from jax import lax
from jax.experimental import pallas as pl
from jax.experimental.pallas import tpu as pltpu
from jax.experimental.pallas import tpu_sc as plsc

F32 = jnp.float32
BF16 = jnp.bfloat16
I32 = jnp.int32
U32 = jnp.uint32
HIGHEST = lax.Precision.HIGHEST

HG_HEADS = 4
HG_BLOCK = 16
HG_CHUNK = 32
HG_MILD_DECAY = -80.0
ATT_GROUPS = ((128, 1), (512, 4), (2048, 16))
ATT_HEADS_PER_GROUP = 4
ATT_HEAD_DIM = 64
TOP_K = 8
ROUTE_SCALE = 2.5
MOE_BLOCK = 128
RMS_EPS = 1e-6
N_DMA_QUEUES = 2
SC_WINDOW = 128
COMBINE_PARTS = 8
EXPERT_WEIGHT_BUFFERS = 3
EXPERT_GROUP = 8
EXPERT_IN_RING = 16
EXPERT_OUT_RING = 12

LANES = 128
VMEM_LIMIT_BYTES = 56 * 1024 * 1024

IN_PROJ_TILE = 1024
HGRN_TILE = 512
ATT_BLOCKS_PER_STEP = 16
MERGE_TILE = 512
DEST_TILE = 2048
COMBINE_TILE = 512


def _sigmoid(x):
    return 1.0 / (1.0 + jnp.exp(-x))


def _silu(x):
    return x * _sigmoid(x)


def _rms(x, g):
    return x * lax.rsqrt(jnp.mean(x * x, axis=-1, keepdims=True) + RMS_EPS) * g


def _pack_halves(x):
    n = x.shape[1] // 2
    bits = lax.bitcast_convert_type(x.astype(BF16).astype(F32), U32)
    return (bits[:, :n] >> 16) | (bits[:, n:] & jnp.uint32(0xFFFF0000))


def _unpack_halves(word):
    lo = lax.bitcast_convert_type(word << 16, F32)
    hi = lax.bitcast_convert_type(word & jnp.uint32(0xFFFF0000), F32)
    return lo, hi


def _params(n_axes=1):
    return pltpu.CompilerParams(
        dimension_semantics=("arbitrary",) * n_axes, vmem_limit_bytes=VMEM_LIMIT_BYTES)


def _ada_kernel(c_ref, w_ref, b_ref, o_ref):
    sc = _silu(c_ref[...])
    o_ref[...] = jnp.dot(sc, w_ref[...], preferred_element_type=F32, precision=HIGHEST) + b_ref[...]


def _ada(c, w, b):
    bsz, d = c.shape
    n = w.shape[1]
    return pl.pallas_call(
        _ada_kernel,
        out_shape=jax.ShapeDtypeStruct((bsz, n), F32),
        grid=(n // d,),
        in_specs=[pl.BlockSpec((bsz, d), lambda j: (0, 0)),
                  pl.BlockSpec((d, d), lambda j: (0, j)),
                  pl.BlockSpec((1, d), lambda j: (0, j))],
        out_specs=pl.BlockSpec((bsz, d), lambda j: (0, j)),
        compiler_params=_params(),
        name="ada_mod",
    )(c, w, b.reshape(1, n))


def _inproj_kernel(n_flat, flat_ranges, att_c0, x_ref, g_ref, sc_ref, sh_ref, w_ref, *refs):
    flat_refs, att_refs, scr = refs[:n_flat], refs[n_flat:-1], refs[-1]
    tm = x_ref.shape[0]
    h = _rms(x_ref[...], g_ref[...]) * (1.0 + sc_ref[0]) + sh_ref[0]
    hb = h.astype(BF16)
    for (c0, c1), o_ref in zip(flat_ranges, flat_refs):
        o_ref[...] = jnp.dot(hb, w_ref[:, c0:c1], preferred_element_type=F32).astype(o_ref.dtype)
    gw = ATT_HEADS_PER_GROUP * ATT_HEAD_DIM
    n_groups = len(ATT_GROUPS)
    for part in range(3):
        c0 = att_c0 + part * n_groups * gw
        res = jnp.dot(hb, w_ref[:, c0:c0 + n_groups * gw], preferred_element_type=F32)
        if part == 0:
            res = res * (ATT_HEAD_DIM ** -0.5)
        for g, (_, dil) in enumerate(ATT_GROUPS):
            o_ref = att_refs[g * 3 + part]
            sub = res[:, g * gw:(g + 1) * gw]
            if dil == 1:
                o_ref[0, 0] = sub.astype(BF16)
            else:
                for c in range(gw // LANES):
                    scr[c] = sub[:, c * LANES:(c + 1) * LANES]
                for r in range(dil):
                    o_ref[0, r] = jnp.concatenate(
                        [scr[c, pl.ds(r, tm // dil, stride=dil), :] for c in range(gw // LANES)],
                        axis=1).astype(BF16)


def _inproj(x2, g, scale, shift, w_bf16, bsz, seq, flat_segs, att_c0, tm):
    t, d = x2.shape
    gw = ATT_HEADS_PER_GROUP * ATT_HEAD_DIM
    n_per = seq // tm
    per_b = lambda i: (i // n_per, 0, 0)
    att_shapes, att_specs = [], []
    for _, dil in ATT_GROUPS:
        for _ in range(3):
            att_shapes.append(jax.ShapeDtypeStruct((bsz, dil, seq // dil, gw), BF16))
            att_specs.append(pl.BlockSpec((1, dil, tm // dil, gw),
                                          lambda i: (i // n_per, 0, i % n_per, 0)))
    outs = pl.pallas_call(
        functools.partial(_inproj_kernel, len(flat_segs),
                          tuple((c0, c0 + wdt) for c0, wdt, _ in flat_segs), att_c0),
        out_shape=[jax.ShapeDtypeStruct((t, wdt), dt) for _, wdt, dt in flat_segs] + att_shapes,
        grid=(t // tm,),
        in_specs=[pl.BlockSpec((tm, d), lambda i: (i, 0)),
                  pl.BlockSpec((1, d), lambda i: (0, 0)),
                  pl.BlockSpec((1, 1, d), per_b),
                  pl.BlockSpec((1, 1, d), per_b),
                  pl.BlockSpec(w_bf16.shape, lambda i: (0, 0), pipeline_mode=pl.Buffered(1))],
        out_specs=[pl.BlockSpec((tm, wdt), lambda i: (i, 0)) for _, wdt, _ in flat_segs]
        + att_specs,
        scratch_shapes=[pltpu.VMEM((gw // LANES, tm, LANES), F32)],
        compiler_params=_params(),
        name="in_proj",
    )(x2, g.reshape(1, d), scale, shift, w_bf16)
    return outs[:len(flat_segs)], outs[len(flat_segs):]


def _hgrn_kernel(ts, q_ref, f_ref, v_ref, gt_ref, lb_ref, ng_ref, o_ref, st_ref, b_ref):
    dk = q_ref.shape[1] // HG_HEADS
    n_chunks = ts // HG_CHUNK
    n_blk = HG_CHUNK // HG_BLOCK

    @pl.when(pl.program_id(1) == 0)
    def _():
        st_ref[...] = jnp.zeros_like(st_ref)

    row = lax.broadcasted_iota(I32, (LANES, LANES), 0)
    col = lax.broadcasted_iota(I32, (LANES, LANES), 1)
    same_chunk = (row // HG_CHUNK) == (col // HG_CHUNK)
    cum_mat = jnp.where(same_chunk & (col <= row), 1.0, 0.0).astype(BF16)

    def chunk_cumsum(x):
        out = []
        for r0 in range(0, ts, LANES):
            rest = x[r0:r0 + LANES]
            acc = None
            for _ in range(3):
                term = rest.astype(BF16)
                part = jnp.dot(cum_mat, term, preferred_element_type=F32)
                acc = part if acc is None else acc + part
                rest = rest - term.astype(F32)
            out.append(acc)
        return jnp.concatenate(out, axis=0)

    def forget(cs):
        lb = lb_ref[:, cs]
        return lb + (1.0 - lb) * _sigmoid(f_ref[:, cs])

    b_min = None
    for h in range(HG_HEADS):
        cs = slice(h * dk, (h + 1) * dk)
        b = chunk_cumsum(jnp.log(forget(cs)))
        b_ref[:, cs] = b
        m = jnp.min(b)
        b_min = m if b_min is None else jnp.minimum(b_min, m)
    mild = b_min >= HG_MILD_DECAY

    def finish(h, o, st):
        cs = slice(h * dk, (h + 1) * dk)
        st_ref[h] = st
        y = _rms(o, ng_ref[:, cs]) * _silu(gt_ref[:, cs].astype(F32))
        o_ref[:, cs] = y.astype(o_ref.dtype)

    @pl.when(mild)
    def _():
        span = 2 * HG_CHUNK
        causal = (lax.broadcasted_iota(I32, (span, span), 0)
                  >= lax.broadcasted_iota(I32, (span, span), 1))
        nt = lambda x, y: lax.dot_general(x, y, (((1,), (1,)), ((), ())),
                                          preferred_element_type=F32)
        for h in range(HG_HEADS):
            cs = slice(h * dk, (h + 1) * dk)
            v = v_ref[:, cs]
            b = b_ref[:, cs]
            q = q_ref[:, cs].astype(F32)
            k = 1.0 - forget(cs)
            st = st_ref[h]
            o_rows = []
            for r0 in range(0, ts, span):
                sl = slice(r0, r0 + span)
                b_first, b_second = b[r0:r0 + HG_CHUNK], b[r0 + HG_CHUNK:r0 + span]
                end_first = b_first[HG_CHUNK - 1:HG_CHUNK]
                end_second = b_second[HG_CHUNK - 1:HG_CHUNK]
                e = jnp.exp(jnp.concatenate([b_first - end_first, b_second], axis=0))
                qe = (q[sl] * e).astype(BF16)
                ke = k[sl] / e
                a = jnp.where(causal, nt(qe, ke.astype(BF16)), 0.0).astype(BF16)
                st_in = (st * jnp.exp(end_first)).astype(BF16)
                o_rows.append(jnp.dot(a, v[sl], preferred_element_type=F32) + nt(qe, st_in))
                kend = (ke * jnp.exp(end_second)).astype(BF16)
                vt = v[sl].astype(F32).T.astype(BF16)
                st = (st * jnp.exp(end_first + end_second)
                      + jnp.dot(vt, kend, preferred_element_type=F32))
            finish(h, jnp.concatenate(o_rows, axis=0), st)

    @pl.when(jnp.logical_not(mild))
    def _():
        _hgrn_steep(ts, dk, n_chunks, n_blk, q_ref, v_ref, b_ref, st_ref, forget, finish)


def _hgrn_steep(ts, dk, n_chunks, n_blk, q_ref, v_ref, b_ref, st_ref, forget, finish):
    t_in_blk = lax.broadcasted_iota(I32, (ts, dk), 0) % HG_BLOCK

    for h in range(HG_HEADS):
        cs = slice(h * dk, (h + 1) * dk)
        q = q_ref[:, cs].astype(F32)
        v = v_ref[:, cs].astype(F32)
        k = 1.0 - forget(cs)
        b = b_ref[:, cs]

        o = jnp.sum(q * k, axis=-1, keepdims=True) * v
        for d in range(1, HG_BLOCK):
            k_d = pltpu.roll(k, d, axis=0)
            b_d = pltpu.roll(b, d, axis=0)
            v_d = pltpu.roll(v, d, axis=0)
            w = jnp.sum(q * k_d * jnp.exp(jnp.minimum(b - b_d, 0.0)), axis=-1, keepdims=True)
            o = o + jnp.where(t_in_blk >= d, w * v_d, 0.0)

        st = st_ref[h]
        o_rows = []
        for c in range(n_chunks):
            r0 = c * HG_CHUNK
            bc = b[r0:r0 + HG_CHUNK]
            qc = q[r0:r0 + HG_CHUNK]
            kc = k[r0:r0 + HG_CHUNK]
            vc = v[r0:r0 + HG_CHUNK].astype(BF16)
            st_b = st.astype(BF16)
            for i in range(n_blk):
                i0 = i * HG_BLOCK
                if i == 0:
                    qt = qc[:HG_BLOCK] * jnp.exp(bc[:HG_BLOCK])
                    qs = qt
                else:
                    ref_row = bc[i0 - 1:i0]
                    qt = qc[i0:i0 + HG_BLOCK] * jnp.exp(bc[i0:i0 + HG_BLOCK] - ref_row)
                    qs = qt * jnp.exp(ref_row)
                oi = lax.dot_general(qs.astype(BF16), st_b, (((1,), (1,)), ((), ())),
                                     preferred_element_type=F32)
                if i > 0:
                    kh = kc[:i0] * jnp.exp(ref_row - bc[:i0])
                    a = lax.dot_general(qt.astype(BF16), kh.astype(BF16), (((1,), (1,)), ((), ())),
                                        preferred_element_type=F32)
                    oi = oi + jnp.dot(a.astype(BF16), vc[:i0], preferred_element_type=F32)
                o_rows.append(oi)
            b_end = bc[HG_CHUNK - 1:HG_CHUNK]
            kend = kc * jnp.exp(b_end - bc)
            vt = v[r0:r0 + HG_CHUNK].T.astype(BF16)
            st = st * jnp.exp(b_end) + jnp.dot(vt, kend.astype(BF16), preferred_element_type=F32)
        finish(h, o + jnp.concatenate(o_rows, axis=0), st)


def _hgrn(hq, hf, hi, hg, lb, ng, bsz, seq, ts):
    t, w = hq.shape
    dk = w // HG_HEADS
    n_s = seq // ts
    tile = lambda b, s: (b * n_s + s, 0)
    return pl.pallas_call(
        functools.partial(_hgrn_kernel, ts),
        out_shape=jax.ShapeDtypeStruct((t, w), BF16),
        grid=(bsz, n_s),
        in_specs=[pl.BlockSpec((ts, w), tile)] * 4
        + [pl.BlockSpec((1, w), lambda b, s: (0, 0))] * 2,
        out_specs=pl.BlockSpec((ts, w), tile),
        scratch_shapes=[pltpu.VMEM((HG_HEADS, dk, dk), F32), pltpu.VMEM((ts, w), F32)],
        compiler_params=_params(2),
        name="hgrn2",
    )(hq, hf, hi, hg, lb.reshape(1, w), ng.reshape(1, w))


def _attn_kernel(nk, nq, nr, q_ref, kp_ref, kc_ref, vp_ref, vc_ref, o_ref, lse_ref):
    n = pl.program_id(2)
    e = ATT_HEAD_DIM
    i = lax.broadcasted_iota(I32, (nk, 2 * nk), 0)
    j = lax.broadcasted_iota(I32, (nk, 2 * nk), 1)
    band = (j >= i) & (j <= i + nk)
    first_head = lax.broadcasted_iota(I32, (nk, LANES), 1) < e
    zero = jnp.zeros((), q_ref.dtype)
    for r in range(nr):
        kk = jnp.concatenate([kp_ref[0, r], kc_ref[0, r]], axis=0)
        vv = jnp.concatenate([vp_ref[0, r], vc_ref[0, r]], axis=0)
        for b in range(nq):
            valid = band & ((j >= nk) | (n * nq + b > 0))
            rows = slice(b * nk, (b + 1) * nk)
            for c in range(0, ATT_HEADS_PER_GROUP * e, LANES):
                q = q_ref[0, r, rows, c:c + LANES]
                kb = kk[b * nk:(b + 2) * nk, c:c + LANES]
                vb = vv[b * nk:(b + 2) * nk, c:c + LANES]
                outs, lses = [], []
                for keep in (first_head, jnp.logical_not(first_head)):
                    s = lax.dot_general(jnp.where(keep, q, zero), kb, (((1,), (1,)), ((), ())),
                                        preferred_element_type=F32)
                    s = jnp.where(valid, s, -jnp.inf)
                    m = jnp.max(s, axis=-1, keepdims=True)
                    p = jnp.exp(s - m)
                    l = jnp.sum(p, axis=-1, keepdims=True)
                    outs.append(jnp.dot(p.astype(BF16), vb, preferred_element_type=F32) / l)
                    lses.append(m + jnp.log(l))
                o_ref[0, r, rows, c:c + LANES] = jnp.where(first_head, outs[0], outs[1])
                lse_ref[0, r, rows, c:c + LANES] = jnp.where(first_head, lses[0], lses[1])


def _attn_group(q, k, v, g, blocks_per_step):
    window, dil = ATT_GROUPS[g]
    nk = window // dil
    bsz, _, ln, gw = q.shape
    nq = min(blocks_per_step, ln // nk)
    nr = min(blocks_per_step // nq, dil)
    assert ln % (nk * nq) == 0 and dil % nr == 0 and 2 * ATT_HEAD_DIM == LANES
    cur = pl.BlockSpec((1, nr, nq * nk, gw), lambda b, r, n: (b, r, n, 0))
    prev = pl.BlockSpec((1, nr, nk, gw), lambda b, r, n: (b, r, jnp.maximum(n * nq - 1, 0), 0))
    return pl.pallas_call(
        functools.partial(_attn_kernel, nk, nq, nr),
        out_shape=[jax.ShapeDtypeStruct(q.shape, F32)] * 2,
        grid=(bsz, dil // nr, ln // (nk * nq)),
        in_specs=[cur, prev, cur, prev, cur],
        out_specs=[cur, cur],
        compiler_params=_params(3),
        name=f"dilated_attn_g{g}",
    )(q, k, k, v, v)


def _token_major(ref, scr):
    dil, rows = ref.shape[1], ref.shape[2]
    if dil == 1:
        return ref[0, 0]
    n_col = scr.shape[0]
    for r in range(dil):
        for c in range(n_col):
            scr[c, pl.ds(r, rows, stride=dil), :] = ref[0, r, :, c * LANES:(c + 1) * LANES]
    return jnp.concatenate([scr[c] for c in range(n_col)], axis=1)


def _merge_kernel(ya_ref, o0_ref, o1_ref, o2_ref, l0_ref, l1_ref, l2_ref, ga_ref, gb_ref, x_ref,
                  g1_ref, sc2_ref, sh2_ref, g2_ref, n2_ref, wa_ref, wb_ref, wo_ref, wr_ref, wrl_ref,
                  wsg_ref, wsu_ref, wsd_ref, bias_ref, x1_ref, hp_ref, idx_ref, gate_ref, rank_ref,
                  cnt_ref, carry_ref, lg_ref, *scr):
    step = pl.program_id(0)

    @pl.when(step == 0)
    def _():
        carry_ref[...] = jnp.zeros_like(carry_ref)
        lg_ref[...] = jnp.zeros_like(lg_ref)

    _route(lg_ref[...], jnp.where(step > 0, 1.0, 0.0), bias_ref, idx_ref, gate_ref, rank_ref,
           cnt_ref, carry_ref)

    l0, l1, l2 = (_token_major(r, s) for r, s in zip((l0_ref, l1_ref, l2_ref), scr[:3]))
    o0, o1, o2 = (_token_major(r, s) for r, s in zip((o0_ref, o1_ref, o2_ref), scr[3:]))
    m = jnp.maximum(jnp.maximum(l0, l1), l2)
    e0, e1, e2 = jnp.exp(l0 - m), jnp.exp(l1 - m), jnp.exp(l2 - m)
    yb = (e0 * o0 + e1 * o1 + e2 * o2) / (e0 + e1 + e2)
    merged = (_sigmoid(ga_ref[...].astype(F32))
              * jnp.dot(ya_ref[...], wa_ref[...], preferred_element_type=F32)
              + _sigmoid(gb_ref[...].astype(F32))
              * jnp.dot(yb.astype(BF16), wb_ref[...], preferred_element_type=F32))
    x1 = x_ref[...] + g1_ref[0] * jnp.dot(merged.astype(BF16), wo_ref[...],
                                           preferred_element_type=F32)
    h2 = _rms(x1, n2_ref[...]) * (1.0 + sc2_ref[0]) + sh2_ref[0]
    hb = h2.astype(BF16)
    act = (_silu(jnp.dot(hb, wsg_ref[...], preferred_element_type=F32))
           * jnp.dot(hb, wsu_ref[...], preferred_element_type=F32))
    shared = jnp.dot(act.astype(BF16), wsd_ref[...], preferred_element_type=F32)
    x1_ref[...] = x1 + g2_ref[0] * shared
    hp_ref[...] = _pack_halves(h2)
    h_lo = (h2 - hb.astype(F32)).astype(BF16)
    nt = lambda a, b: lax.dot_general(a, b, (((1,), (1,)), ((), ())), preferred_element_type=F32)
    lg_ref[...] = nt(wr_ref[...], hb) + (nt(wr_ref[...], h_lo) + nt(wrl_ref[...], hb))


def _merge(ya, att, ga, gb, x2, gate1, scale2, shift2, gate2, norm2_g, wa, wb, wo, wr_t, wsg, wsu,
           wsd, router_bias, seq, tm):
    t, d = x2.shape
    n_e = wr_t.shape[0]
    wr_hi = wr_t.astype(BF16)
    wr_lo = (wr_t - wr_hi.astype(F32)).astype(BF16)
    n_per = seq // tm
    n_tiles = t // tm
    tile = lambda i: jnp.minimum(i, n_tiles - 1)
    per_b = lambda i: (tile(i) // n_per, 0, 0)
    rows = lambda wdt: pl.BlockSpec((tm, wdt), lambda i: (tile(i), 0))
    full = lambda a: pl.BlockSpec(a.shape, lambda i: (0,) * a.ndim)
    vec = pl.BlockSpec((1, 1, d), per_b)
    (o0, l0), (o1, l1), (o2, l2) = att
    gw = o0.shape[3]
    by_residue = lambda a: pl.BlockSpec((1, a.shape[1], tm // a.shape[1], gw),
                                        lambda i: (tile(i) // n_per, 0, tile(i) % n_per, 0))
    att_in = (o0, o1, o2, l0, l1, l2)
    bias_col = router_bias.reshape(n_e, 1)
    tok = pl.BlockSpec((TOP_K, tm), lambda i: (0, jnp.maximum(i - 1, 0)))
    return pl.pallas_call(
        _merge_kernel,
        out_shape=[jax.ShapeDtypeStruct((t, d), F32),
                   jax.ShapeDtypeStruct((t, d // 2), U32),
                   jax.ShapeDtypeStruct((TOP_K, t), I32), jax.ShapeDtypeStruct((TOP_K, t), F32),
                   jax.ShapeDtypeStruct((TOP_K, t), I32), jax.ShapeDtypeStruct((n_e, LANES), I32)],
        grid=(n_tiles + 1,),
        in_specs=[rows(ya.shape[1])] + [by_residue(a) for a in att_in] + [rows(d)] * 3
        + [vec, vec, vec, vec, pl.BlockSpec((1, d), lambda i: (0, 0))]
        + [full(a) for a in (wa, wb, wo, wr_hi, wr_lo, wsg, wsu, wsd, bias_col)],
        out_specs=[rows(d), rows(d // 2), tok, tok, tok,
                   pl.BlockSpec((n_e, LANES), lambda i: (0, 0))],
        scratch_shapes=[pltpu.VMEM((n_e, 1), F32), pltpu.VMEM((n_e, tm), F32)]
        + [pltpu.VMEM((gw // LANES, tm, LANES), F32)] * 6,
        compiler_params=_params(),
        name="merge_router",
    )(ya, *att_in, ga, gb, x2, gate1, scale2, shift2, gate2,
      norm2_g.reshape(1, d), wa, wb, wo, wr_hi, wr_lo, wsg, wsu, wsd, bias_col)


def _route(logits, live, bias_ref, idx_ref, gate_ref, rank_ref, cnt_ref, carry_ref):
    n_e, tt = logits.shape
    scores = _sigmoid(logits)
    sel = scores + bias_ref[...]
    eio = lax.broadcasted_iota(I32, (n_e, tt), 0)
    picked = jnp.zeros((n_e, tt), F32)
    idxs, vals = [], []
    for _ in range(TOP_K):
        m = jnp.max(sel, axis=0, keepdims=True)
        ik = jnp.min(jnp.where(sel == m, eio, n_e), axis=0, keepdims=True)
        hit = eio == ik
        vals.append(jnp.sum(jnp.where(hit, scores, 0.0), axis=0, keepdims=True))
        sel = jnp.where(hit, -jnp.inf, sel)
        picked = picked + jnp.where(hit, 1.0, 0.0)
        idxs.append(ik)
    denom = vals[0]
    for v in vals[1:]:
        denom = denom + v
    gate_ref[...] = jnp.concatenate([v / denom * ROUTE_SCALE for v in vals], axis=0)
    idx_ref[...] = jnp.concatenate(idxs, axis=0)

    upper = (lax.broadcasted_iota(I32, (tt, tt), 0) <= lax.broadcasted_iota(I32, (tt, tt), 1))
    incl = jnp.dot(picked.astype(BF16), jnp.where(upper, 1.0, 0.0).astype(BF16),
                   preferred_element_type=F32)
    before = incl - picked + carry_ref[...]
    rank_ref[...] = jnp.concatenate(
        [jnp.sum(jnp.where(eio == ik, before, 0.0), axis=0, keepdims=True) for ik in idxs],
        axis=0).astype(I32)
    carry_ref[...] = carry_ref[...] + jnp.sum(picked, axis=1, keepdims=True) * live
    cnt_ref[...] = jnp.broadcast_to(carry_ref[...], cnt_ref.shape).astype(I32)


def _dest_kernel(idx_ref, rank_ref, start_ref, o_ref):
    k, tt = idx_ref.shape
    n_e = start_ref.shape[0]
    eio = lax.broadcasted_iota(I32, (n_e, tt), 0)
    start = start_ref[...]
    rows = [jnp.sum(jnp.where(eio == idx_ref[r:r + 1, :], start, 0), axis=0, keepdims=True)
            for r in range(k)]
    o_ref[...] = jnp.concatenate(rows, axis=0) + rank_ref[...]


def _dest(idx, rank, seg_start, tt):
    k, t = idx.shape
    n_e = seg_start.shape[0]
    tok = pl.BlockSpec((k, tt), lambda i: (0, i))
    return pl.pallas_call(
        _dest_kernel,
        out_shape=jax.ShapeDtypeStruct((k, t), I32),
        grid=(t // tt,),
        in_specs=[tok, tok, pl.BlockSpec((n_e, 1), lambda i: (0, 0))],
        out_specs=tok,
        compiler_params=_params(),
        name="moe_dest",
    )(idx, rank, seg_start.reshape(n_e, 1))


def _sc_mesh():
    return plsc.VectorSubcoreMesh(core_axis_name="core", subcore_axis_name="subcore")


def _sc_scatter_rows(rows, dest, n_out):
    k, t = dest.shape
    w = rows.shape[1]
    mesh = _sc_mesh()
    n_workers = mesh.num_cores * mesh.num_subcores
    win_per_worker = t // (SC_WINDOW * n_workers)
    assert win_per_worker * SC_WINDOW * n_workers == t

    @functools.partial(
        pl.kernel, out_type=jax.ShapeDtypeStruct((n_out, w), rows.dtype), mesh=mesh,
        scratch_types=[pltpu.VMEM((SC_WINDOW, w), rows.dtype)]
        + [pltpu.VMEM((1, SC_WINDOW), I32)] * k + [pltpu.SemaphoreType.DMA],
        name="moe_dispatch_sc")
    def run(rows_hbm, idx_hbm, out_hbm, rows_v, *rest):
        idx_v, sem = rest[:k], rest[k]
        worker = lax.axis_index("subcore") * mesh.num_cores + lax.axis_index("core")

        @pl.loop(0, win_per_worker)
        def _(j):
            t0 = pl.multiple_of((worker * win_per_worker + j) * SC_WINDOW, SC_WINDOW)
            loads = [pltpu.async_copy(rows_hbm.at[pl.ds(t0, SC_WINDOW)], rows_v, sem)]
            loads += [pltpu.async_copy(idx_hbm.at[:, pl.ds(r * t + t0, SC_WINDOW)], idx_v[r], sem)
                      for r in range(k)]
            for c in loads:
                c.wait()
            copies = [pltpu.async_copy(rows_v, out_hbm.at[idx_v[r].at[0]], sem) for r in range(k)]
            for c in copies:
                c.wait()

    return run(rows, dest.reshape(1, k * t))


def _sc_gather_rows(table, dest):
    k, t = dest.shape
    w = table.shape[1]
    mesh = _sc_mesh()
    n_workers = mesh.num_cores * mesh.num_subcores
    win_per_worker = (k * t) // (SC_WINDOW * n_workers)
    assert win_per_worker * SC_WINDOW * n_workers == k * t and win_per_worker % 2 == 0

    @functools.partial(
        pl.kernel, out_type=jax.ShapeDtypeStruct((k * t, w), table.dtype), mesh=mesh,
        scratch_types=[pltpu.VMEM((SC_WINDOW, w), table.dtype), pltpu.VMEM((1, SC_WINDOW), I32),
                       pltpu.VMEM((1, SC_WINDOW), I32), pltpu.SemaphoreType.DMA],
        name="moe_gather_sc")
    def run(table_hbm, idx_hbm, out_hbm, rows_v, idx_a, idx_b, sem):
        worker = lax.axis_index("subcore") * mesh.num_cores + lax.axis_index("core")

        def window(j):
            return pl.ds(pl.multiple_of((worker * win_per_worker + j) * SC_WINDOW, SC_WINDOW),
                         SC_WINDOW)

        def idx_load(j, buf):
            return pltpu.make_async_copy(idx_hbm.at[:, window(j)], buf, sem)

        idx_load(0, idx_a).start()

        @pl.loop(0, win_per_worker, step=2)
        def _(j0):
            for b, (cur, nxt) in enumerate(((idx_a, idx_b), (idx_b, idx_a))):
                j = j0 + b
                idx_load(j, cur).wait()

                @pl.when(j + 1 < win_per_worker)
                def _():
                    idx_load(j + 1, nxt).start()

                pltpu.sync_copy(table_hbm.at[cur.at[0]], rows_v)
                pltpu.sync_copy(rows_v, out_hbm.at[window(j)])

    return run(table, dest.reshape(1, k * t))


def _expert_kernel(start_ref, nblk_ref, xs_ref, wg_ref, wu_ref, wd_ref, ys_ref,
                   xbuf, ybuf, wgb, wub, wdb, wbuf_g, wbuf_u, wbuf_d, sem_in, sem_out, sem_w):
    wbuf = (wbuf_g, wbuf_u, wbuf_d)
    e = pl.program_id(0)
    n_e = pl.num_programs(0)
    nb = nblk_ref[e]
    g0 = start_ref[e] // MOE_BLOCK
    n_used = start_ref[n_e - 1] // MOE_BLOCK + nblk_ref[n_e - 1]
    n_in, n_out = xbuf.shape[0], ybuf.shape[0]

    def rows(g):
        return pl.ds(pl.multiple_of(g * MOE_BLOCK, MOE_BLOCK), MOE_BLOCK)

    def in_copy(g):
        slot = lax.rem(g, n_in)
        return pltpu.make_async_copy(xs_ref.at[rows(g), :], xbuf.at[slot], sem_in.at[slot])

    def out_copy(g):
        slot = lax.rem(g, n_out)
        return pltpu.make_async_copy(ybuf.at[slot], ys_ref.at[rows(g), :], sem_out.at[slot])

    look = n_in - EXPERT_GROUP

    @pl.when(e == 0)
    def _():
        for g in range(look):
            @pl.when(g < n_used)
            def _():
                in_copy(g).start(priority=g % N_DMA_QUEUES)

    n_w = wbuf[0].shape[0]

    def weight_copies(ex):
        slot = lax.rem(ex, n_w)
        return [pltpu.make_async_copy(src.at[ex], buf.at[slot], sem_w.at[slot])
                for src, buf in zip((wg_ref, wu_ref, wd_ref), wbuf)]

    @pl.when(e == 0)
    def _():
        for ex in range(min(n_w, wg_ref.shape[0])):
            for c in weight_copies(ex):
                c.start()

    for c in weight_copies(e):
        c.wait()
    w_slot = lax.rem(e, n_w)

    @pl.when(nb > 0)
    def _():
        wgb[...] = wbuf[0][w_slot].astype(BF16)
        wub[...] = wbuf[1][w_slot].astype(BF16)
        wdb[...] = wbuf[2][w_slot].astype(BF16)

    @pl.when(e + n_w < n_e)
    def _():
        for c in weight_copies(e + n_w):
            c.start()

    @pl.when(nb > 0)
    def _():
        def swiglu(word):
            lo, hi = _unpack_halves(word)
            x = jnp.concatenate([lo.astype(BF16), hi.astype(BF16)], axis=1)
            gate = jnp.dot(x, wgb[...], preferred_element_type=F32)
            up = jnp.dot(x, wub[...], preferred_element_type=F32)
            act = (_silu(gate) * up).astype(BF16)
            return jnp.dot(act, wdb[...], preferred_element_type=F32)

        def process(g, m):
            for i in range(m):
                in_copy(g + i).wait()
            for i in range(m):
                @pl.when(g + i >= n_out)
                def _():
                    out_copy(g + i - n_out).wait()
            for i in range(m):
                @pl.when(g + look + i < n_used)
                def _():
                    in_copy(g + look + i).start(priority=i % N_DMA_QUEUES)
            y_all = swiglu(jnp.concatenate([xbuf[lax.rem(g + i, n_in)] for i in range(m)], axis=0))
            for i in range(m):
                ybuf[lax.rem(g + i, n_out)] = _pack_halves(y_all[i * MOE_BLOCK:(i + 1) * MOE_BLOCK])
            for i in range(m):
                out_copy(g + i).start(priority=(i + 1) % N_DMA_QUEUES)

        def group_body(p, carry):
            process(g0 + p * EXPERT_GROUP, EXPERT_GROUP)
            return carry

        lax.fori_loop(0, nb // EXPERT_GROUP, group_body, 0)
        for m in range(1, EXPERT_GROUP):
            @pl.when(lax.rem(nb, EXPERT_GROUP) == m)
            def _():
                process(g0 + nb - m, m)

    @pl.when(e == n_e - 1)
    def _():
        for i in range(n_out):
            @pl.when(n_used - 1 - i >= 0)
            def _():
                out_copy(n_used - 1 - i).wait()


def _experts(seg_start, seg_blocks, xs, wg, wu, wd):
    n_slots, half = xs.shape
    n_e, d, de = wg.shape
    n_w = EXPERT_WEIGHT_BUFFERS
    return pl.pallas_call(
        _expert_kernel,
        out_shape=jax.ShapeDtypeStruct((n_slots, half), U32),
        grid_spec=pltpu.PrefetchScalarGridSpec(
            num_scalar_prefetch=2,
            grid=(n_e,),
            in_specs=[pl.BlockSpec(memory_space=pl.ANY)] * 4,
            out_specs=pl.BlockSpec(memory_space=pl.ANY),
            scratch_shapes=[pltpu.VMEM((EXPERT_IN_RING, MOE_BLOCK, half), U32),
                            pltpu.VMEM((EXPERT_OUT_RING, MOE_BLOCK, half), U32),
                            pltpu.VMEM((d, de), BF16), pltpu.VMEM((d, de), BF16),
                            pltpu.VMEM((de, d), BF16),
                            pltpu.VMEM((n_w, d, de), F32), pltpu.VMEM((n_w, d, de), F32),
                            pltpu.VMEM((n_w, de, d), F32),
                            pltpu.SemaphoreType.DMA((EXPERT_IN_RING,)),
                            pltpu.SemaphoreType.DMA((EXPERT_OUT_RING,)),
                            pltpu.SemaphoreType.DMA((n_w,))]),
        compiler_params=_params(),
        name="moe_experts",
    )(seg_start, seg_blocks, xs, wg, wu, wd)


def _combine_kernel(yg_ref, gt_ref, x_ref, g2_ref, fg_ref, o_ref):
    k, tc = gt_ref.shape
    eye = jnp.where(lax.broadcasted_iota(I32, (tc, tc), 0) == lax.broadcasted_iota(I32, (tc, tc), 1),
                    1.0, 0.0).astype(BF16)
    rest, gt = gt_ref[...], None
    for _ in range(3):
        term = rest.astype(BF16)
        part = lax.dot_general(eye, term, (((1,), (1,)), ((), ())), preferred_element_type=F32)
        gt = part if gt is None else gt + part
        rest = rest - term.astype(F32)
    lo, hi = _unpack_halves(yg_ref[0])
    y_lo, y_hi = lo * gt[:, 0:1], hi * gt[:, 0:1]
    for r in range(1, k):
        lo, hi = _unpack_halves(yg_ref[r])
        y_lo, y_hi = y_lo + lo * gt[:, r:r + 1], y_hi + hi * gt[:, r:r + 1]
    y = jnp.concatenate([y_lo, y_hi], axis=1)
    o_ref[...] = _rms(x_ref[...] + g2_ref[0] * y, fg_ref[...])


def _combine_into_kernel(yg_ref, gt_ref, x_ref, g2_ref, fg_ref, prev_ref, o_ref):
    del prev_ref
    _combine_kernel(yg_ref, gt_ref, x_ref, g2_ref, fg_ref, o_ref)


def _combine(yg, tok0, gates, x1s, gate2, final_g, seq, tc, out_so_far=None):
    k, n, half = yg.shape
    t, d = x1s.shape
    assert n % tc == 0 and tok0 % tc == 0
    b0 = tok0 // tc
    args = [yg, gates, x1s, gate2, final_g.reshape(1, d)]
    in_specs = [pl.BlockSpec((k, tc, half), lambda i: (0, i, 0)),
                pl.BlockSpec((k, tc), lambda i: (0, i + b0)),
                pl.BlockSpec((tc, d), lambda i: (i + b0, 0)),
                pl.BlockSpec((1, 1, d), lambda i: (((i + b0) * tc) // seq, 0, 0)),
                pl.BlockSpec((1, d), lambda i: (0, 0))]
    aliases = {}
    kernel = _combine_kernel
    if out_so_far is not None:
        args.append(out_so_far)
        in_specs.append(pl.BlockSpec(memory_space=pl.ANY))
        aliases = {len(args) - 1: 0}
        kernel = _combine_into_kernel
    return pl.pallas_call(
        kernel,
        out_shape=jax.ShapeDtypeStruct((t, d), F32),
        grid=(n // tc,),
        in_specs=in_specs,
        out_specs=pl.BlockSpec((tc, d), lambda i: (i + b0, 0)),
        input_output_aliases=aliases,
        compiler_params=_params(),
        name="moe_combine",
    )(*args)


def _layer(x2, c, bsz, seq, lb_row, ada_w, ada_b, norm1_g, w_in, hg_norm_g, w_branch_a, w_branch_b,
           w_out, norm2_g, w_router, router_bias, w_exp_gate, w_exp_up, w_exp_down, w_sh_gate,
           w_sh_up, w_sh_down, final_g):
    t, d = x2.shape
    n_e = w_router.shape[1]
    mod = _ada(c, ada_w, ada_b).reshape(bsz, 6, 1, d)
    shift1, scale1, gate1, shift2, scale2, gate2 = (mod[:, j] for j in range(6))

    hw = hg_norm_g.shape[0]
    aw = len(ATT_GROUPS) * ATT_HEADS_PER_GROUP * ATT_HEAD_DIM
    flat_segs = [(0, hw, BF16), (hw, hw, F32), (2 * hw, hw, BF16), (3 * hw, hw, BF16),
                 (4 * hw + 3 * aw, d, BF16), (4 * hw + 3 * aw + d, d, BF16)]
    (hq, hf, hi, hg, ga, gb), qkv = _inproj(
        x2, norm1_g, scale1, shift1, w_in.astype(BF16), bsz, seq, flat_segs, 4 * hw,
        tm=IN_PROJ_TILE)

    ya = _hgrn(hq, hf, hi, hg, lb_row, hg_norm_g, bsz, seq, ts=HGRN_TILE)
    att = [_attn_group(*qkv[3 * g:3 * g + 3], g, blocks_per_step=ATT_BLOCKS_PER_STEP)
           for g in range(len(ATT_GROUPS))]

    x1s, hp, idx, gates, rank, cnt = _merge(
        ya, att, ga, gb, x2, gate1, scale2, shift2, gate2, norm2_g, w_branch_a.astype(BF16),
        w_branch_b.astype(BF16), w_out.astype(BF16), w_router.T, w_sh_gate.astype(BF16),
        w_sh_up.astype(BF16), w_sh_down.astype(BF16), router_bias, seq, tm=MERGE_TILE)
    counts = cnt[:, 0]
    padded = (counts + MOE_BLOCK - 1) // MOE_BLOCK * MOE_BLOCK
    seg_start = (jnp.cumsum(padded) - padded).astype(I32)
    n_blocks = -(-(t * TOP_K) // MOE_BLOCK) + n_e
    dest = _dest(idx, rank, seg_start, tt=DEST_TILE)

    xs = _sc_scatter_rows(hp, dest, n_blocks * MOE_BLOCK)
    ys = _experts(seg_start, (padded // MOE_BLOCK).astype(I32), xs, w_exp_gate, w_exp_up,
                  w_exp_down)
    out, n = None, t // COMBINE_PARTS
    for part in range(COMBINE_PARTS):
        yg = _sc_gather_rows(ys, dest[:, part * n:(part + 1) * n]).reshape(TOP_K, n, d // 2)
        out = _combine(yg, part * n, gates, x1s, gate2, final_g, seq, tc=COMBINE_TILE,
                       out_so_far=out)
    return out


def kernel(x, c, ada_w, ada_b, norm1_g, w_in, lb_logits, hg_norm_g, w_branch_a, w_branch_b, w_out,
           norm2_g, w_router, router_bias, w_exp_gate, w_exp_up, w_exp_down, w_sh_gate, w_sh_up,
           w_sh_down, final_g):
    bsz, seq, d = x.shape
    depth = ada_w.shape[0]
    assert depth == 1, "the last layer's kernels also apply the final norm"
    lb_table = jnp.cumsum(jax.nn.softmax(lb_logits.astype(F32), axis=0), axis=0)
    out = _layer(x.reshape(bsz * seq, d), c, bsz, seq, lb_table[0], ada_w[0], ada_b[0], norm1_g[0],
                 w_in[0], hg_norm_g[0], w_branch_a[0], w_branch_b[0], w_out[0], norm2_g[0],
                 w_router[0], router_bias[0], w_exp_gate[0], w_exp_up[0], w_exp_down[0],
                 w_sh_gate[0], w_sh_up[0], w_sh_down[0], final_g)
    return out.reshape(bsz, seq, d)
```

```python
import functools

import jax
import jax.numpy as jnp
from jax import lax
from jax.experimental import pallas as pl
from jax.experimental.pallas import tpu as pltpu
from jax.experimental.pallas import tpu_sc as plsc

F32 = jnp.float32
BF16 = jnp.bfloat16
I32 = jnp.int32
U32 = jnp.uint32
HIGHEST = lax.Precision.HIGHEST

HG_HEADS = 4
HG_BLOCK = 16
HG_CHUNK = 32
HG_MILD_DECAY = -80.0
ATT_GROUPS = ((128, 1), (512, 4), (2048, 16))
ATT_HEADS_PER_GROUP = 4
ATT_HEAD_DIM = 64
TOP_K = 8
ROUTE_SCALE = 2.5
MOE_BLOCK = 128
RMS_EPS = 1e-6
N_DMA_QUEUES = 2
SC_WINDOW = 128
COMBINE_PARTS = 8
EXPERT_WEIGHT_BUFFERS = 3
EXPERT_GROUP = 8
EXPERT_IN_RING = 16
EXPERT_OUT_RING = 12

LANES = 128
VMEM_LIMIT_BYTES = 56 * 1024 * 1024

IN_PROJ_TILE = 1024
HGRN_TILE = 512
ATT_BLOCKS_PER_STEP = 16
MERGE_TILE = 512
DEST_TILE = 2048
COMBINE_TILE = 512


def _sigmoid(x):
    return 1.0 / (1.0 + jnp.exp(-x))


def _silu(x):
    return x * _sigmoid(x)


def _rms(x, g):
    return x * lax.rsqrt(jnp.mean(x * x, axis=-1, keepdims=True) + RMS_EPS) * g


def _pack_halves(x):
    n = x.shape[1] // 2
    bits = lax.bitcast_convert_type(x.astype(BF16).astype(F32), U32)
    return (bits[:, :n] >> 16) | (bits[:, n:] & jnp.uint32(0xFFFF0000))


def _unpack_halves(word):
    lo = lax.bitcast_convert_type(word << 16, F32)
    hi = lax.bitcast_convert_type(word & jnp.uint32(0xFFFF0000), F32)
    return lo, hi


def _params(n_axes=1):
    return pltpu.CompilerParams(
        dimension_semantics=("arbitrary",) * n_axes, vmem_limit_bytes=VMEM_LIMIT_BYTES)


def _ada_kernel(c_ref, w_ref, b_ref, o_ref):
    sc = _silu(c_ref[...])
    o_ref[...] = jnp.dot(sc, w_ref[...], preferred_element_type=F32, precision=HIGHEST) + b_ref[...]


def _ada(c, w, b):
    bsz, d = c.shape
    n = w.shape[1]
    return pl.pallas_call(
        _ada_kernel,
        out_shape=jax.ShapeDtypeStruct((bsz, n), F32),
        grid=(n // d,),
        in_specs=[pl.BlockSpec((bsz, d), lambda j: (0, 0)),
                  pl.BlockSpec((d, d), lambda j: (0, j)),
                  pl.BlockSpec((1, d), lambda j: (0, j))],
        out_specs=pl.BlockSpec((bsz, d), lambda j: (0, j)),
        compiler_params=_params(),
        name="ada_mod",
    )(c, w, b.reshape(1, n))


def _inproj_kernel(n_flat, flat_ranges, att_c0, x_ref, g_ref, sc_ref, sh_ref, w_ref, *refs):
    flat_refs, att_refs, scr = refs[:n_flat], refs[n_flat:-1], refs[-1]
    tm = x_ref.shape[0]
    h = _rms(x_ref[...], g_ref[...]) * (1.0 + sc_ref[0]) + sh_ref[0]
    hb = h.astype(BF16)
    for (c0, c1), o_ref in zip(flat_ranges, flat_refs):
        o_ref[...] = jnp.dot(hb, w_ref[:, c0:c1], preferred_element_type=F32).astype(o_ref.dtype)
    gw = ATT_HEADS_PER_GROUP * ATT_HEAD_DIM
    n_groups = len(ATT_GROUPS)
    for part in range(3):
        c0 = att_c0 + part * n_groups * gw
        res = jnp.dot(hb, w_ref[:, c0:c0 + n_groups * gw], preferred_element_type=F32)
        if part == 0:
            res = res * (ATT_HEAD_DIM ** -0.5)
        for g, (_, dil) in enumerate(ATT_GROUPS):
            o_ref = att_refs[g * 3 + part]
            sub = res[:, g * gw:(g + 1) * gw]
            if dil == 1:
                o_ref[0, 0] = sub.astype(BF16)
            else:
                for c in range(gw // LANES):
                    scr[c] = sub[:, c * LANES:(c + 1) * LANES]
                for r in range(dil):
                    o_ref[0, r] = jnp.concatenate(
                        [scr[c, pl.ds(r, tm // dil, stride=dil), :] for c in range(gw // LANES)],
                        axis=1).astype(BF16)


def _inproj(x2, g, scale, shift, w_bf16, bsz, seq, flat_segs, att_c0, tm):
    t, d = x2.shape
    gw = ATT_HEADS_PER_GROUP * ATT_HEAD_DIM
    n_per = seq // tm
    per_b = lambda i: (i // n_per, 0, 0)
    att_shapes, att_specs = [], []
    for _, dil in ATT_GROUPS:
        for _ in range(3):
            att_shapes.append(jax.ShapeDtypeStruct((bsz, dil, seq // dil, gw), BF16))
            att_specs.append(pl.BlockSpec((1, dil, tm // dil, gw),
                                          lambda i: (i // n_per, 0, i % n_per, 0)))
    outs = pl.pallas_call(
        functools.partial(_inproj_kernel, len(flat_segs),
                          tuple((c0, c0 + wdt) for c0, wdt, _ in flat_segs), att_c0),
        out_shape=[jax.ShapeDtypeStruct((t, wdt), dt) for _, wdt, dt in flat_segs] + att_shapes,
        grid=(t // tm,),
        in_specs=[pl.BlockSpec((tm, d), lambda i: (i, 0)),
                  pl.BlockSpec((1, d), lambda i: (0, 0)),
                  pl.BlockSpec((1, 1, d), per_b),
                  pl.BlockSpec((1, 1, d), per_b),
                  pl.BlockSpec(w_bf16.shape, lambda i: (0, 0), pipeline_mode=pl.Buffered(1))],
        out_specs=[pl.BlockSpec((tm, wdt), lambda i: (i, 0)) for _, wdt, _ in flat_segs]
        + att_specs,
        scratch_shapes=[pltpu.VMEM((gw // LANES, tm, LANES), F32)],
        compiler_params=_params(),
        name="in_proj",
    )(x2, g.reshape(1, d), scale, shift, w_bf16)
    return outs[:len(flat_segs)], outs[len(flat_segs):]


def _hgrn_kernel(ts, q_ref, f_ref, v_ref, gt_ref, lb_ref, ng_ref, o_ref, st_ref, b_ref):
    dk = q_ref.shape[1] // HG_HEADS
    n_chunks = ts // HG_CHUNK
    n_blk = HG_CHUNK // HG_BLOCK

    @pl.when(pl.program_id(1) == 0)
    def _():
        st_ref[...] = jnp.zeros_like(st_ref)

    row = lax.broadcasted_iota(I32, (LANES, LANES), 0)
    col = lax.broadcasted_iota(I32, (LANES, LANES), 1)
    same_chunk = (row // HG_CHUNK) == (col // HG_CHUNK)
    cum_mat = jnp.where(same_chunk & (col <= row), 1.0, 0.0).astype(BF16)

    def chunk_cumsum(x):
        out = []
        for r0 in range(0, ts, LANES):
            rest = x[r0:r0 + LANES]
            acc = None
            for _ in range(3):
                term = rest.astype(BF16)
                part = jnp.dot(cum_mat, term, preferred_element_type=F32)
                acc = part if acc is None else acc + part
                rest = rest - term.astype(F32)
            out.append(acc)
        return jnp.concatenate(out, axis=0)

    def forget(cs):
        lb = lb_ref[:, cs]
        return lb + (1.0 - lb) * _sigmoid(f_ref[:, cs])

    b_min = None
    for h in range(HG_HEADS):
        cs = slice(h * dk, (h + 1) * dk)
        b = chunk_cumsum(jnp.log(forget(cs)))
        b_ref[:, cs] = b
        m = jnp.min(b)
        b_min = m if b_min is None else jnp.minimum(b_min, m)
    mild = b_min >= HG_MILD_DECAY

    def finish(h, o, st):
        cs = slice(h * dk, (h + 1) * dk)
        st_ref[h] = st
        y = _rms(o, ng_ref[:, cs]) * _silu(gt_ref[:, cs].astype(F32))
        o_ref[:, cs] = y.astype(o_ref.dtype)

    @pl.when(mild)
    def _():
        span = 2 * HG_CHUNK
        causal = (lax.broadcasted_iota(I32, (span, span), 0)
                  >= lax.broadcasted_iota(I32, (span, span), 1))
        nt = lambda x, y: lax.dot_general(x, y, (((1,), (1,)), ((), ())),
                                          preferred_element_type=F32)
        for h in range(HG_HEADS):
            cs = slice(h * dk, (h + 1) * dk)
            v = v_ref[:, cs]
            b = b_ref[:, cs]
            q = q_ref[:, cs].astype(F32)
            k = 1.0 - forget(cs)
            st = st_ref[h]
            o_rows = []
            for r0 in range(0, ts, span):
                sl = slice(r0, r0 + span)
                b_first, b_second = b[r0:r0 + HG_CHUNK], b[r0 + HG_CHUNK:r0 + span]
                end_first = b_first[HG_CHUNK - 1:HG_CHUNK]
                end_second = b_second[HG_CHUNK - 1:HG_CHUNK]
                e = jnp.exp(jnp.concatenate([b_first - end_first, b_second], axis=0))
                qe = (q[sl] * e).astype(BF16)
                ke = k[sl] / e
                a = jnp.where(causal, nt(qe, ke.astype(BF16)), 0.0).astype(BF16)
                st_in = (st * jnp.exp(end_first)).astype(BF16)
                o_rows.append(jnp.dot(a, v[sl], preferred_element_type=F32) + nt(qe, st_in))
                kend = (ke * jnp.exp(end_second)).astype(BF16)
                vt = v[sl].astype(F32).T.astype(BF16)
                st = (st * jnp.exp(end_first + end_second)
                      + jnp.dot(vt, kend, preferred_element_type=F32))
            finish(h, jnp.concatenate(o_rows, axis=0), st)

    @pl.when(jnp.logical_not(mild))
    def _():
        _hgrn_steep(ts, dk, n_chunks, n_blk, q_ref, v_ref, b_ref, st_ref, forget, finish)


def _hgrn_steep(ts, dk, n_chunks, n_blk, q_ref, v_ref, b_ref, st_ref, forget, finish):
    t_in_blk = lax.broadcasted_iota(I32, (ts, dk), 0) % HG_BLOCK

    for h in range(HG_HEADS):
        cs = slice(h * dk, (h + 1) * dk)
        q = q_ref[:, cs].astype(F32)
        v = v_ref[:, cs].astype(F32)
        k = 1.0 - forget(cs)
        b = b_ref[:, cs]

        o = jnp.sum(q * k, axis=-1, keepdims=True) * v
        for d in range(1, HG_BLOCK):
            k_d = pltpu.roll(k, d, axis=0)
            b_d = pltpu.roll(b, d, axis=0)
            v_d = pltpu.roll(v, d, axis=0)
            w = jnp.sum(q * k_d * jnp.exp(jnp.minimum(b - b_d, 0.0)), axis=-1, keepdims=True)
            o = o + jnp.where(t_in_blk >= d, w * v_d, 0.0)

        st = st_ref[h]
        o_rows = []
        for c in range(n_chunks):
            r0 = c * HG_CHUNK
            bc = b[r0:r0 + HG_CHUNK]
            qc = q[r0:r0 + HG_CHUNK]
            kc = k[r0:r0 + HG_CHUNK]
            vc = v[r0:r0 + HG_CHUNK].astype(BF16)
            st_b = st.astype(BF16)
            for i in range(n_blk):
                i0 = i * HG_BLOCK
                if i == 0:
                    qt = qc[:HG_BLOCK] * jnp.exp(bc[:HG_BLOCK])
                    qs = qt
                else:
                    ref_row = bc[i0 - 1:i0]
                    qt = qc[i0:i0 + HG_BLOCK] * jnp.exp(bc[i0:i0 + HG_BLOCK] - ref_row)
                    qs = qt * jnp.exp(ref_row)
                oi = lax.dot_general(qs.astype(BF16), st_b, (((1,), (1,)), ((), ())),
                                     preferred_element_type=F32)
                if i > 0:
                    kh = kc[:i0] * jnp.exp(ref_row - bc[:i0])
                    a = lax.dot_general(qt.astype(BF16), kh.astype(BF16), (((1,), (1,)), ((), ())),
                                        preferred_element_type=F32)
                    oi = oi + jnp.dot(a.astype(BF16), vc[:i0], preferred_element_type=F32)
                o_rows.append(oi)
            b_end = bc[HG_CHUNK - 1:HG_CHUNK]
            kend = kc * jnp.exp(b_end - bc)
            vt = v[r0:r0 + HG_CHUNK].T.astype(BF16)
            st = st * jnp.exp(b_end) + jnp.dot(vt, kend.astype(BF16), preferred_element_type=F32)
        finish(h, o + jnp.concatenate(o_rows, axis=0), st)


def _hgrn(hq, hf, hi, hg, lb, ng, bsz, seq, ts):
    t, w = hq.shape
    dk = w // HG_HEADS
    n_s = seq // ts
    tile = lambda b, s: (b * n_s + s, 0)
    return pl.pallas_call(
        functools.partial(_hgrn_kernel, ts),
        out_shape=jax.ShapeDtypeStruct((t, w), BF16),
        grid=(bsz, n_s),
        in_specs=[pl.BlockSpec((ts, w), tile)] * 4
        + [pl.BlockSpec((1, w), lambda b, s: (0, 0))] * 2,
        out_specs=pl.BlockSpec((ts, w), tile),
        scratch_shapes=[pltpu.VMEM((HG_HEADS, dk, dk), F32), pltpu.VMEM((ts, w), F32)],
        compiler_params=_params(2),
        name="hgrn2",
    )(hq, hf, hi, hg, lb.reshape(1, w), ng.reshape(1, w))


def _attn_kernel(nk, nq, nr, q_ref, kp_ref, kc_ref, vp_ref, vc_ref, o_ref, lse_ref):
    n = pl.program_id(2)
    e = ATT_HEAD_DIM
    i = lax.broadcasted_iota(I32, (nk, 2 * nk), 0)
    j = lax.broadcasted_iota(I32, (nk, 2 * nk), 1)
    band = (j >= i) & (j <= i + nk)
    first_head = lax.broadcasted_iota(I32, (nk, LANES), 1) < e
    zero = jnp.zeros((), q_ref.dtype)
    for r in range(nr):
        kk = jnp.concatenate([kp_ref[0, r], kc_ref[0, r]], axis=0)
        vv = jnp.concatenate([vp_ref[0, r], vc_ref[0, r]], axis=0)
        for b in range(nq):
            valid = band & ((j >= nk) | (n * nq + b > 0))
            rows = slice(b * nk, (b + 1) * nk)
            for c in range(0, ATT_HEADS_PER_GROUP * e, LANES):
                q = q_ref[0, r, rows, c:c + LANES]
                kb = kk[b * nk:(b + 2) * nk, c:c + LANES]
                vb = vv[b * nk:(b + 2) * nk, c:c + LANES]
                outs, lses = [], []
                for keep in (first_head, jnp.logical_not(first_head)):
                    s = lax.dot_general(jnp.where(keep, q, zero), kb, (((1,), (1,)), ((), ())),
                                        preferred_element_type=F32)
                    s = jnp.where(valid, s, -jnp.inf)
                    m = jnp.max(s, axis=-1, keepdims=True)
                    p = jnp.exp(s - m)
                    l = jnp.sum(p, axis=-1, keepdims=True)
                    outs.append(jnp.dot(p.astype(BF16), vb, preferred_element_type=F32) / l)
                    lses.append(m + jnp.log(l))
                o_ref[0, r, rows, c:c + LANES] = jnp.where(first_head, outs[0], outs[1])
                lse_ref[0, r, rows, c:c + LANES] = jnp.where(first_head, lses[0], lses[1])


def _attn_group(q, k, v, g, blocks_per_step):
    window, dil = ATT_GROUPS[g]
    nk = window // dil
    bsz, _, ln, gw = q.shape
    nq = min(blocks_per_step, ln // nk)
    nr = min(blocks_per_step // nq, dil)
    assert ln % (nk * nq) == 0 and dil % nr == 0 and 2 * ATT_HEAD_DIM == LANES
    cur = pl.BlockSpec((1, nr, nq * nk, gw), lambda b, r, n: (b, r, n, 0))
    prev = pl.BlockSpec((1, nr, nk, gw), lambda b, r, n: (b, r, jnp.maximum(n * nq - 1, 0), 0))
    return pl.pallas_call(
        functools.partial(_attn_kernel, nk, nq, nr),
        out_shape=[jax.ShapeDtypeStruct(q.shape, F32)] * 2,
        grid=(bsz, dil // nr, ln // (nk * nq)),
        in_specs=[cur, prev, cur, prev, cur],
        out_specs=[cur, cur],
        compiler_params=_params(3),
        name=f"dilated_attn_g{g}",
    )(q, k, k, v, v)


def _token_major(ref, scr):
    dil, rows = ref.shape[1], ref.shape[2]
    if dil == 1:
        return ref[0, 0]
    n_col = scr.shape[0]
    for r in range(dil):
        for c in range(n_col):
            scr[c, pl.ds(r, rows, stride=dil), :] = ref[0, r, :, c * LANES:(c + 1) * LANES]
    return jnp.concatenate([scr[c] for c in range(n_col)], axis=1)


def _merge_kernel(ya_ref, o0_ref, o1_ref, o2_ref, l0_ref, l1_ref, l2_ref, ga_ref, gb_ref, x_ref,
                  g1_ref, sc2_ref, sh2_ref, g2_ref, n2_ref, wa_ref, wb_ref, wo_ref, wr_ref, wrl_ref,
                  wsg_ref, wsu_ref, wsd_ref, bias_ref, x1_ref, hp_ref, idx_ref, gate_ref, rank_ref,
                  cnt_ref, carry_ref, lg_ref, *scr):
    step = pl.program_id(0)

    @pl.when(step == 0)
    def _():
        carry_ref[...] = jnp.zeros_like(carry_ref)
        lg_ref[...] = jnp.zeros_like(lg_ref)

    _route(lg_ref[...], jnp.where(step > 0, 1.0, 0.0), bias_ref, idx_ref, gate_ref, rank_ref,
           cnt_ref, carry_ref)

    l0, l1, l2 = (_token_major(r, s) for r, s in zip((l0_ref, l1_ref, l2_ref), scr[:3]))
    o0, o1, o2 = (_token_major(r, s) for r, s in zip((o0_ref, o1_ref, o2_ref), scr[3:]))
    m = jnp.maximum(jnp.maximum(l0, l1), l2)
    e0, e1, e2 = jnp.exp(l0 - m), jnp.exp(l1 - m), jnp.exp(l2 - m)
    yb = (e0 * o0 + e1 * o1 + e2 * o2) / (e0 + e1 + e2)
    merged = (_sigmoid(ga_ref[...].astype(F32))
              * jnp.dot(ya_ref[...], wa_ref[...], preferred_element_type=F32)
              + _sigmoid(gb_ref[...].astype(F32))
              * jnp.dot(yb.astype(BF16), wb_ref[...], preferred_element_type=F32))
    x1 = x_ref[...] + g1_ref[0] * jnp.dot(merged.astype(BF16), wo_ref[...],
                                           preferred_element_type=F32)
    h2 = _rms(x1, n2_ref[...]) * (1.0 + sc2_ref[0]) + sh2_ref[0]
    hb = h2.astype(BF16)
    act = (_silu(jnp.dot(hb, wsg_ref[...], preferred_element_type=F32))
           * jnp.dot(hb, wsu_ref[...], preferred_element_type=F32))
    shared = jnp.dot(act.astype(BF16), wsd_ref[...], preferred_element_type=F32)
    x1_ref[...] = x1 + g2_ref[0] * shared
    hp_ref[...] = _pack_halves(h2)
    h_lo = (h2 - hb.astype(F32)).astype(BF16)
    nt = lambda a, b: lax.dot_general(a, b, (((1,), (1,)), ((), ())), preferred_element_type=F32)
    lg_ref[...] = nt(wr_ref[...], hb) + (nt(wr_ref[...], h_lo) + nt(wrl_ref[...], hb))


def _merge(ya, att, ga, gb, x2, gate1, scale2, shift2, gate2, norm2_g, wa, wb, wo, wr_t, wsg, wsu,
           wsd, router_bias, seq, tm):
    t, d = x2.shape
    n_e = wr_t.shape[0]
    wr_hi = wr_t.astype(BF16)
    wr_lo = (wr_t - wr_hi.astype(F32)).astype(BF16)
    n_per = seq // tm
    n_tiles = t // tm
    tile = lambda i: jnp.minimum(i, n_tiles - 1)
    per_b = lambda i: (tile(i) // n_per, 0, 0)
    rows = lambda wdt: pl.BlockSpec((tm, wdt), lambda i: (tile(i), 0))
    full = lambda a: pl.BlockSpec(a.shape, lambda i: (0,) * a.ndim)
    vec = pl.BlockSpec((1, 1, d), per_b)
    (o0, l0), (o1, l1), (o2, l2) = att
    gw = o0.shape[3]
    by_residue = lambda a: pl.BlockSpec((1, a.shape[1], tm // a.shape[1], gw),
                                        lambda i: (tile(i) // n_per, 0, tile(i) % n_per, 0))
    att_in = (o0, o1, o2, l0, l1, l2)
    bias_col = router_bias.reshape(n_e, 1)
    tok = pl.BlockSpec((TOP_K, tm), lambda i: (0, jnp.maximum(i - 1, 0)))
    return pl.pallas_call(
        _merge_kernel,
        out_shape=[jax.ShapeDtypeStruct((t, d), F32),
                   jax.ShapeDtypeStruct((t, d // 2), U32),
                   jax.ShapeDtypeStruct((TOP_K, t), I32), jax.ShapeDtypeStruct((TOP_K, t), F32),
                   jax.ShapeDtypeStruct((TOP_K, t), I32), jax.ShapeDtypeStruct((n_e, LANES), I32)],
        grid=(n_tiles + 1,),
        in_specs=[rows(ya.shape[1])] + [by_residue(a) for a in att_in] + [rows(d)] * 3
        + [vec, vec, vec, vec, pl.BlockSpec((1, d), lambda i: (0, 0))]
        + [full(a) for a in (wa, wb, wo, wr_hi, wr_lo, wsg, wsu, wsd, bias_col)],
        out_specs=[rows(d), rows(d // 2), tok, tok, tok,
                   pl.BlockSpec((n_e, LANES), lambda i: (0, 0))],
        scratch_shapes=[pltpu.VMEM((n_e, 1), F32), pltpu.VMEM((n_e, tm), F32)]
        + [pltpu.VMEM((gw // LANES, tm, LANES), F32)] * 6,
        compiler_params=_params(),
        name="merge_router",
    )(ya, *att_in, ga, gb, x2, gate1, scale2, shift2, gate2,
      norm2_g.reshape(1, d), wa, wb, wo, wr_hi, wr_lo, wsg, wsu, wsd, bias_col)


def _route(logits, live, bias_ref, idx_ref, gate_ref, rank_ref, cnt_ref, carry_ref):
    n_e, tt = logits.shape
    scores = _sigmoid(logits)
    sel = scores + bias_ref[...]
    eio = lax.broadcasted_iota(I32, (n_e, tt), 0)
    picked = jnp.zeros((n_e, tt), F32)
    idxs, vals = [], []
    for _ in range(TOP_K):
        m = jnp.max(sel, axis=0, keepdims=True)
        ik = jnp.min(jnp.where(sel == m, eio, n_e), axis=0, keepdims=True)
        hit = eio == ik
        vals.append(jnp.sum(jnp.where(hit, scores, 0.0), axis=0, keepdims=True))
        sel = jnp.where(hit, -jnp.inf, sel)
        picked = picked + jnp.where(hit, 1.0, 0.0)
        idxs.append(ik)
    denom = vals[0]
    for v in vals[1:]:
        denom = denom + v
    gate_ref[...] = jnp.concatenate([v / denom * ROUTE_SCALE for v in vals], axis=0)
    idx_ref[...] = jnp.concatenate(idxs, axis=0)

    upper = (lax.broadcasted_iota(I32, (tt, tt), 0) <= lax.broadcasted_iota(I32, (tt, tt), 1))
    incl = jnp.dot(picked.astype(BF16), jnp.where(upper, 1.0, 0.0).astype(BF16),
                   preferred_element_type=F32)
    before = incl - picked + carry_ref[...]
    rank_ref[...] = jnp.concatenate(
        [jnp.sum(jnp.where(eio == ik, before, 0.0), axis=0, keepdims=True) for ik in idxs],
        axis=0).astype(I32)
    carry_ref[...] = carry_ref[...] + jnp.sum(picked, axis=1, keepdims=True) * live
    cnt_ref[...] = jnp.broadcast_to(carry_ref[...], cnt_ref.shape).astype(I32)


def _dest_kernel(idx_ref, rank_ref, start_ref, o_ref):
    k, tt = idx_ref.shape
    n_e = start_ref.shape[0]
    eio = lax.broadcasted_iota(I32, (n_e, tt), 0)
    start = start_ref[...]
    rows = [jnp.sum(jnp.where(eio == idx_ref[r:r + 1, :], start, 0), axis=0, keepdims=True)
            for r in range(k)]
    o_ref[...] = jnp.concatenate(rows, axis=0) + rank_ref[...]


def _dest(idx, rank, seg_start, tt):
    k, t = idx.shape
    n_e = seg_start.shape[0]
    tok = pl.BlockSpec((k, tt), lambda i: (0, i))
    return pl.pallas_call(
        _dest_kernel,
        out_shape=jax.ShapeDtypeStruct((k, t), I32),
        grid=(t // tt,),
        in_specs=[tok, tok, pl.BlockSpec((n_e, 1), lambda i: (0, 0))],
        out_specs=tok,
        compiler_params=_params(),
        name="moe_dest",
    )(idx, rank, seg_start.reshape(n_e, 1))


def _sc_mesh():
    return plsc.VectorSubcoreMesh(core_axis_name="core", subcore_axis_name="subcore")


def _sc_scatter_rows(rows, dest, n_out):
    k, t = dest.shape
    w = rows.shape[1]
    mesh = _sc_mesh()
    n_workers = mesh.num_cores * mesh.num_subcores
    win_per_worker = t // (SC_WINDOW * n_workers)
    assert win_per_worker * SC_WINDOW * n_workers == t and win_per_worker % 2 == 0

    @functools.partial(
        pl.kernel, out_type=jax.ShapeDtypeStruct((n_out, w), rows.dtype), mesh=mesh,
        scratch_types=[pltpu.VMEM((SC_WINDOW, w), rows.dtype)]
        + [pltpu.VMEM((1, SC_WINDOW), I32)] * (2 * k)
        + [pltpu.SemaphoreType.DMA, pltpu.SemaphoreType.DMA],
        name="moe_dispatch_sc")
    def run(rows_hbm, idx_hbm, out_hbm, rows_v, *rest):
        idx_a, idx_b, sem, sem_idx = rest[:k], rest[k:2 * k], rest[2 * k], rest[2 * k + 1]
        worker = lax.axis_index("subcore") * mesh.num_cores + lax.axis_index("core")

        def first_token(j):
            return pl.multiple_of((worker * win_per_worker + j) * SC_WINDOW, SC_WINDOW)

        def idx_loads(j, bufs):
            return [pltpu.make_async_copy(idx_hbm.at[:, pl.ds(r * t + first_token(j), SC_WINDOW)],
                                          bufs[r], sem_idx) for r in range(k)]

        for c in idx_loads(0, idx_a):
            c.start()

        @pl.loop(0, win_per_worker, step=2)
        def _(j0):
            for b, (cur, nxt) in enumerate(((idx_a, idx_b), (idx_b, idx_a))):
                j = j0 + b
                rows_load = pltpu.async_copy(rows_hbm.at[pl.ds(first_token(j), SC_WINDOW)],
                                             rows_v, sem)
                for c in idx_loads(j, cur):
                    c.wait()

                @pl.when(j + 1 < win_per_worker)
                def _():
                    for c in idx_loads(j + 1, nxt):
                        c.start()

                rows_load.wait()
                copies = [pltpu.async_copy(rows_v, out_hbm.at[cur[r].at[0]], sem)
                          for r in range(k)]
                for c in copies:
                    c.wait()

    return run(rows, dest.reshape(1, k * t))


def _sc_gather_rows(table, dest):
    k, t = dest.shape
    w = table.shape[1]
    mesh = _sc_mesh()
    n_workers = mesh.num_cores * mesh.num_subcores
    win_per_worker = (k * t) // (SC_WINDOW * n_workers)
    assert win_per_worker * SC_WINDOW * n_workers == k * t and win_per_worker % 2 == 0

    @functools.partial(
        pl.kernel, out_type=jax.ShapeDtypeStruct((k * t, w), table.dtype), mesh=mesh,
        scratch_types=[pltpu.VMEM((SC_WINDOW, w), table.dtype), pltpu.VMEM((1, SC_WINDOW), I32),
                       pltpu.VMEM((1, SC_WINDOW), I32), pltpu.SemaphoreType.DMA],
        name="moe_gather_sc")
    def run(table_hbm, idx_hbm, out_hbm, rows_v, idx_a, idx_b, sem):
        worker = lax.axis_index("subcore") * mesh.num_cores + lax.axis_index("core")

        def window(j):
            return pl.ds(pl.multiple_of((worker * win_per_worker + j) * SC_WINDOW, SC_WINDOW),
                         SC_WINDOW)

        def idx_load(j, buf):
            return pltpu.make_async_copy(idx_hbm.at[:, window(j)], buf, sem)

        idx_load(0, idx_a).start()

        @pl.loop(0, win_per_worker, step=2)
        def _(j0):
            for b, (cur, nxt) in enumerate(((idx_a, idx_b), (idx_b, idx_a))):
                j = j0 + b
                idx_load(j, cur).wait()

                @pl.when(j + 1 < win_per_worker)
                def _():
                    idx_load(j + 1, nxt).start()

                pltpu.sync_copy(table_hbm.at[cur.at[0]], rows_v)
                pltpu.sync_copy(rows_v, out_hbm.at[window(j)])

    return run(table, dest.reshape(1, k * t))


def _expert_kernel(start_ref, nblk_ref, xs_ref, wg_ref, wu_ref, wd_ref, ys_ref,
                   xbuf, ybuf, wgb, wub, wdb, wbuf_g, wbuf_u, wbuf_d, sem_in, sem_out, sem_w):
    wbuf = (wbuf_g, wbuf_u, wbuf_d)
    e = pl.program_id(0)
    n_e = pl.num_programs(0)
    nb = nblk_ref[e]
    g0 = start_ref[e] // MOE_BLOCK
    n_used = start_ref[n_e - 1] // MOE_BLOCK + nblk_ref[n_e - 1]
    n_in, n_out = xbuf.shape[0], ybuf.shape[0]

    def rows(g):
        return pl.ds(pl.multiple_of(g * MOE_BLOCK, MOE_BLOCK), MOE_BLOCK)

    def in_copy(g):
        slot = lax.rem(g, n_in)
        return pltpu.make_async_copy(xs_ref.at[rows(g), :], xbuf.at[slot], sem_in.at[slot])

    def out_copy(g):
        slot = lax.rem(g, n_out)
        return pltpu.make_async_copy(ybuf.at[slot], ys_ref.at[rows(g), :], sem_out.at[slot])

    look = n_in - EXPERT_GROUP

    @pl.when(e == 0)
    def _():
        for g in range(look):
            @pl.when(g < n_used)
            def _():
                in_copy(g).start(priority=g % N_DMA_QUEUES)

    n_w = wbuf[0].shape[0]

    def weight_copies(ex):
        slot = lax.rem(ex, n_w)
        return [pltpu.make_async_copy(src.at[ex], buf.at[slot], sem_w.at[slot])
                for src, buf in zip((wg_ref, wu_ref, wd_ref), wbuf)]

    @pl.when(e == 0)
    def _():
        for ex in range(min(n_w, wg_ref.shape[0])):
            for c in weight_copies(ex):
                c.start()

    for c in weight_copies(e):
        c.wait()
    w_slot = lax.rem(e, n_w)

    @pl.when(nb > 0)
    def _():
        wgb[...] = wbuf[0][w_slot].astype(BF16)
        wub[...] = wbuf[1][w_slot].astype(BF16)
        wdb[...] = wbuf[2][w_slot].astype(BF16)

    @pl.when(e + n_w < n_e)
    def _():
        for c in weight_copies(e + n_w):
            c.start()

    @pl.when(nb > 0)
    def _():
        def swiglu(word):
            lo, hi = _unpack_halves(word)
            x = jnp.concatenate([lo.astype(BF16), hi.astype(BF16)], axis=1)
            gate = jnp.dot(x, wgb[...], preferred_element_type=F32)
            up = jnp.dot(x, wub[...], preferred_element_type=F32)
            act = (_silu(gate) * up).astype(BF16)
            return jnp.dot(act, wdb[...], preferred_element_type=F32)

        def process(g, m):
            for i in range(m):
                in_copy(g + i).wait()
            for i in range(m):
                @pl.when(g + i >= n_out)
                def _():
                    out_copy(g + i - n_out).wait()
            for i in range(m):
                @pl.when(g + look + i < n_used)
                def _():
                    in_copy(g + look + i).start(priority=i % N_DMA_QUEUES)
            y_all = swiglu(jnp.concatenate([xbuf[lax.rem(g + i, n_in)] for i in range(m)], axis=0))
            for i in range(m):
                ybuf[lax.rem(g + i, n_out)] = _pack_halves(y_all[i * MOE_BLOCK:(i + 1) * MOE_BLOCK])
            for i in range(m):
                out_copy(g + i).start(priority=(i + 1) % N_DMA_QUEUES)

        def group_body(p, carry):
            process(g0 + p * EXPERT_GROUP, EXPERT_GROUP)
            return carry

        lax.fori_loop(0, nb // EXPERT_GROUP, group_body, 0)
        for m in range(1, EXPERT_GROUP):
            @pl.when(lax.rem(nb, EXPERT_GROUP) == m)
            def _():
                process(g0 + nb - m, m)

    @pl.when(e == n_e - 1)
    def _():
        for i in range(n_out):
            @pl.when(n_used - 1 - i >= 0)
            def _():
                out_copy(n_used - 1 - i).wait()


def _experts(seg_start, seg_blocks, xs, wg, wu, wd):
    n_slots, half = xs.shape
    n_e, d, de = wg.shape
    n_w = EXPERT_WEIGHT_BUFFERS
    return pl.pallas_call(
        _expert_kernel,
        out_shape=jax.ShapeDtypeStruct((n_slots, half), U32),
        grid_spec=pltpu.PrefetchScalarGridSpec(
            num_scalar_prefetch=2,
            grid=(n_e,),
            in_specs=[pl.BlockSpec(memory_space=pl.ANY)] * 4,
            out_specs=pl.BlockSpec(memory_space=pl.ANY),
            scratch_shapes=[pltpu.VMEM((EXPERT_IN_RING, MOE_BLOCK, half), U32),
                            pltpu.VMEM((EXPERT_OUT_RING, MOE_BLOCK, half), U32),
                            pltpu.VMEM((d, de), BF16), pltpu.VMEM((d, de), BF16),
                            pltpu.VMEM((de, d), BF16),
                            pltpu.VMEM((n_w, d, de), F32), pltpu.VMEM((n_w, d, de), F32),
                            pltpu.VMEM((n_w, de, d), F32),
                            pltpu.SemaphoreType.DMA((EXPERT_IN_RING,)),
                            pltpu.SemaphoreType.DMA((EXPERT_OUT_RING,)),
                            pltpu.SemaphoreType.DMA((n_w,))]),
        compiler_params=_params(),
        name="moe_experts",
    )(seg_start, seg_blocks, xs, wg, wu, wd)


def _combine_kernel(yg_ref, gt_ref, x_ref, g2_ref, fg_ref, o_ref):
    k, tc = gt_ref.shape
    eye = jnp.where(lax.broadcasted_iota(I32, (tc, tc), 0) == lax.broadcasted_iota(I32, (tc, tc), 1),
                    1.0, 0.0).astype(BF16)
    rest, gt = gt_ref[...], None
    for _ in range(3):
        term = rest.astype(BF16)
        part = lax.dot_general(eye, term, (((1,), (1,)), ((), ())), preferred_element_type=F32)
        gt = part if gt is None else gt + part
        rest = rest - term.astype(F32)
    lo, hi = _unpack_halves(yg_ref[0])
    y_lo, y_hi = lo * gt[:, 0:1], hi * gt[:, 0:1]
    for r in range(1, k):
        lo, hi = _unpack_halves(yg_ref[r])
        y_lo, y_hi = y_lo + lo * gt[:, r:r + 1], y_hi + hi * gt[:, r:r + 1]
    y = jnp.concatenate([y_lo, y_hi], axis=1)
    o_ref[...] = _rms(x_ref[...] + g2_ref[0] * y, fg_ref[...])


def _combine_into_kernel(yg_ref, gt_ref, x_ref, g2_ref, fg_ref, prev_ref, o_ref):
    del prev_ref
    _combine_kernel(yg_ref, gt_ref, x_ref, g2_ref, fg_ref, o_ref)


def _combine(yg, tok0, gates, x1s, gate2, final_g, seq, tc, out_so_far=None):
    k, n, half = yg.shape
    t, d = x1s.shape
    assert n % tc == 0 and tok0 % tc == 0
    b0 = tok0 // tc
    args = [yg, gates, x1s, gate2, final_g.reshape(1, d)]
    in_specs = [pl.BlockSpec((k, tc, half), lambda i: (0, i, 0)),
                pl.BlockSpec((k, tc), lambda i: (0, i + b0)),
                pl.BlockSpec((tc, d), lambda i: (i + b0, 0)),
                pl.BlockSpec((1, 1, d), lambda i: (((i + b0) * tc) // seq, 0, 0)),
                pl.BlockSpec((1, d), lambda i: (0, 0))]
    aliases = {}
    kernel = _combine_kernel
    if out_so_far is not None:
        args.append(out_so_far)
        in_specs.append(pl.BlockSpec(memory_space=pl.ANY))
        aliases = {len(args) - 1: 0}
        kernel = _combine_into_kernel
    return pl.pallas_call(
        kernel,
        out_shape=jax.ShapeDtypeStruct((t, d), F32),
        grid=(n // tc,),
        in_specs=in_specs,
        out_specs=pl.BlockSpec((tc, d), lambda i: (i + b0, 0)),
        input_output_aliases=aliases,
        compiler_params=_params(),
        name="moe_combine",
    )(*args)


def _layer(x2, c, bsz, seq, lb_row, ada_w, ada_b, norm1_g, w_in, hg_norm_g, w_branch_a, w_branch_b,
           w_out, norm2_g, w_router, router_bias, w_exp_gate, w_exp_up, w_exp_down, w_sh_gate,
           w_sh_up, w_sh_down, final_g):
    t, d = x2.shape
    n_e = w_router.shape[1]
    mod = _ada(c, ada_w, ada_b).reshape(bsz, 6, 1, d)
    shift1, scale1, gate1, shift2, scale2, gate2 = (mod[:, j] for j in range(6))

    hw = hg_norm_g.shape[0]
    aw = len(ATT_GROUPS) * ATT_HEADS_PER_GROUP * ATT_HEAD_DIM
    flat_segs = [(0, hw, BF16), (hw, hw, F32), (2 * hw, hw, BF16), (3 * hw, hw, BF16),
                 (4 * hw + 3 * aw, d, BF16), (4 * hw + 3 * aw + d, d, BF16)]
    (hq, hf, hi, hg, ga, gb), qkv = _inproj(
        x2, norm1_g, scale1, shift1, w_in.astype(BF16), bsz, seq, flat_segs, 4 * hw,
        tm=IN_PROJ_TILE)

    ya = _hgrn(hq, hf, hi, hg, lb_row, hg_norm_g, bsz, seq, ts=HGRN_TILE)
    att = [_attn_group(*qkv[3 * g:3 * g + 3], g, blocks_per_step=ATT_BLOCKS_PER_STEP)
           for g in range(len(ATT_GROUPS))]

    x1s, hp, idx, gates, rank, cnt = _merge(
        ya, att, ga, gb, x2, gate1, scale2, shift2, gate2, norm2_g, w_branch_a.astype(BF16),
        w_branch_b.astype(BF16), w_out.astype(BF16), w_router.T, w_sh_gate.astype(BF16),
        w_sh_up.astype(BF16), w_sh_down.astype(BF16), router_bias, seq, tm=MERGE_TILE)
    counts = cnt[:, 0]
    padded = (counts + MOE_BLOCK - 1) // MOE_BLOCK * MOE_BLOCK
    seg_start = (jnp.cumsum(padded) - padded).astype(I32)
    n_blocks = -(-(t * TOP_K) // MOE_BLOCK) + n_e
    dest = _dest(idx, rank, seg_start, tt=DEST_TILE)

    xs = _sc_scatter_rows(hp, dest, n_blocks * MOE_BLOCK)
    ys = _experts(seg_start, (padded // MOE_BLOCK).astype(I32), xs, w_exp_gate, w_exp_up,
                  w_exp_down)
    out, n = None, t // COMBINE_PARTS
    for part in range(COMBINE_PARTS):
        yg = _sc_gather_rows(ys, dest[:, part * n:(part + 1) * n]).reshape(TOP_K, n, d // 2)
        out = _combine(yg, part * n, gates, x1s, gate2, final_g, seq, tc=COMBINE_TILE,
                       out_so_far=out)
    return out


def kernel(x, c, ada_w, ada_b, norm1_g, w_in, lb_logits, hg_norm_g, w_branch_a, w_branch_b, w_out,
           norm2_g, w_router, router_bias, w_exp_gate, w_exp_up, w_exp_down, w_sh_gate, w_sh_up,
           w_sh_down, final_g):
    bsz, seq, d = x.shape
    depth = ada_w.shape[0]
    assert depth == 1, "the last layer's kernels also apply the final norm"
    lb_table = jnp.cumsum(jax.nn.softmax(lb_logits.astype(F32), axis=0), axis=0)
    out = _layer(x.reshape(bsz * seq, d), c, bsz, seq, lb_table[0], ada_w[0], ada_b[0], norm1_g[0],
                 w_in[0], hg_norm_g[0], w_branch_a[0], w_branch_b[0], w_out[0], norm2_g[0],
                 w_router[0], router_bias[0], w_exp_gate[0], w_exp_up[0], w_exp_down[0],
                 w_sh_gate[0], w_sh_up[0], w_sh_down[0], final_g)
    return out.reshape(bsz, seq, d)
```
